```python
import math
import jax, jax.numpy as jnp
from jax import lax
import numpy as np

D_MODEL = 1024
BATCH = 16
SEQ = 256
DEPTH = 2
DEC_BATCH = 2
DEC_SEQ = 2048
PAST_LEN = 256

GRID_W = 64
EPS = 1e-6
NEG_BIG = -1e30
ATTN_QBLK = 128
A_HEADS = 4
A_DIM = 32
ROPE_BASE = 10000.0
B_HEADS = 4
B_DK = 64
B_DV = 64
B_CHUNK = 32
C_HEADS = 4
C_DIM = 64
NA_WIN_H = 8
NA_WIN_W = 16
NA_QW = 16
NA_KW = NA_QW + NA_WIN_W
D_GROUPS = 4
D_GROUP_DIM = 64
A_W = A_HEADS * 2 * A_DIM
B_KW = B_HEADS * B_DK
B_VW = B_HEADS * B_DV
C_W = C_HEADS * C_DIM
D_W = D_GROUPS * D_GROUP_DIM
MIX_W = A_W + B_VW + C_W + D_W
PROJ_SPLITS = (A_W, A_W, A_W, B_KW, B_KW, B_KW, B_VW, B_VW, C_W, C_W, C_W, D_W)
PROJ_W = 3 * A_W + 3 * B_KW + 2 * B_VW + 3 * C_W + D_W
N_EXPERTS = 16
N_GROUPS = 4
TOP_K = 2
D_EXPERT = 512

kernel_name = "hybrid_dit_prefix_step"


def rmsnorm(x, g):
    xf = x.astype(jnp.float32)
    y = xf * lax.rsqrt(jnp.mean(xf * xf, axis=-1, keepdims=True) + EPS)
    return (y * g.astype(jnp.float32)).astype(x.dtype)


def split_heads(t, n, d):
    b, l, _ = t.shape
    return t.reshape(b, l, n, d).transpose(0, 2, 1, 3)


def merge_heads(t):
    b, h, l, d = t.shape
    return t.transpose(0, 2, 1, 3).reshape(b, l, h * d)


def axial_rope(t):
    l = t.shape[2]
    nf = A_DIM // 4
    freqs = ROPE_BASE ** (-jnp.arange(nf, dtype=jnp.float32) / nf)
    pos = jnp.arange(l)
    row = (pos // GRID_W).astype(jnp.float32)
    col = (pos % GRID_W).astype(jnp.float32)
    ang = jnp.concatenate([row[:, None] * freqs, col[:, None] * freqs], axis=-1)
    cos = jnp.cos(ang).astype(t.dtype)
    sin = jnp.sin(ang).astype(t.dtype)
    tp = t.reshape(t.shape[:-1] + (A_DIM // 2, 2))
    t0, t1 = tp[..., 0], tp[..., 1]
    out = jnp.stack([t0 * cos - t1 * sin, t0 * sin + t1 * cos], axis=-1)
    return out.reshape(t.shape)


def to_query_blocks(q):
    b, h, l, d = q.shape
    qb = min(ATTN_QBLK, l)
    return q.reshape(b, h, l // qb, qb, d).transpose(2, 0, 1, 3, 4)


def from_query_blocks(o):
    nb, b, h, qb, d = o.shape
    return o.transpose(1, 2, 0, 3, 4).reshape(b, h, nb * qb, d)


def diff_attention(q1, q2, k1, k2, v, lam):
    scale = A_DIM ** -0.5

    def one(qs):
        qa, qb = qs
        p1 = jax.nn.softmax(jnp.einsum("bhqd,bhkd->bhqk", qa, k1).astype(jnp.float32) * scale, axis=-1)
        p2 = jax.nn.softmax(jnp.einsum("bhqd,bhkd->bhqk", qb, k2).astype(jnp.float32) * scale, axis=-1)
        w = (p1 - lam * p2).astype(v.dtype)
        return jnp.einsum("bhqk,bhkv->bhqv", w, v)

    return from_query_blocks(lax.map(one, (to_query_blocks(q1), to_query_blocks(q2))))


def softmax_attention(q, k, v):
    scale = q.shape[-1] ** -0.5

    def one(qb):
        p = jax.nn.softmax(jnp.einsum("bhqd,bhkd->bhqk", qb, k).astype(jnp.float32) * scale, axis=-1)
        return jnp.einsum("bhqk,bhkv->bhqv", p.astype(v.dtype), v)

    return from_query_blocks(lax.map(one, to_query_blocks(q)))


def hgrn2_chunked(q, k, v, logf, s0):
    bsz, h, l, dk = q.shape
    dv = v.shape[-1]
    n = l // B_CHUNK
    r = lambda t: t.astype(jnp.float32).reshape(bsz, h, n, B_CHUNK, t.shape[-1])
    q, k, v, lf = r(q), r(k), r(v), r(logf)
    b = jnp.cumsum(lf, axis=3)
    b_last = b[:, :, :, -1:, :]
    causal = jnp.tril(jnp.ones((B_CHUNK, B_CHUNK), dtype=bool))[:, :, None]
    diff = b[:, :, :, :, None, :] - b[:, :, :, None, :, :]
    decay = jnp.where(causal, jnp.exp(jnp.where(causal, diff, 0.0)), 0.0)
    attn = jnp.einsum("bhntk,bhntsk,bhnsk->bhnts", q, decay, k)
    o_intra = jnp.einsum("bhnts,bhnsv->bhntv", attn, v)
    q_dec = q * jnp.exp(b)
    k_dec = k * jnp.exp(b_last - b)
    kv = jnp.einsum("bhnsk,bhnsv->nbhkv", k_dec, v)
    g = jnp.exp(b_last[:, :, :, 0, :]).transpose(2, 0, 1, 3)

    def step(s, xs):
        kv_n, g_n = xs
        return g_n[..., None] * s + kv_n, s

    s_final, s_prev = lax.scan(step, s0.astype(jnp.float32), (kv, g))
    o_inter = jnp.einsum("bhntk,nbhkv->bhntv", q_dec, s_prev)
    return (o_intra + o_inter).reshape(bsz, h, l, dv), s_final


def hgrn2_mixer(bq, bff, bfb, bv, bg, lb_f, lb_b, norm_g, s0_f, s0_b):
    bsz, l, _ = bq.shape
    q = split_heads(jax.nn.silu(bq.astype(jnp.float32)), B_HEADS, B_DK)
    v = split_heads(bv.astype(jnp.float32), B_HEADS, B_DV)

    def forget(z, lb):
        z = split_heads(z.astype(jnp.float32), B_HEADS, B_DK)
        lb = lb.astype(jnp.float32).reshape(B_HEADS, 1, B_DK)
        f = lb + (1.0 - lb) * jax.nn.sigmoid(z)
        i = (1.0 - lb) * jax.nn.sigmoid(-z)
        return jnp.log(f), i

    logf_f, k_f = forget(bff, lb_f)
    logf_b, k_b = forget(bfb, lb_b)
    o_f, s_f = hgrn2_chunked(q, k_f, v, logf_f, s0_f)
    flip = lambda t: jnp.flip(t, axis=2)
    o_b, s_b = hgrn2_chunked(flip(q), flip(k_b), flip(v), flip(logf_b), s0_b)
    o = rmsnorm(o_f + flip(o_b), norm_g)
    o = merge_heads(o) * jax.nn.silu(bg.astype(jnp.float32))
    return o.astype(bq.dtype), s_f, s_b


def na_latent(q, k, v, rpb, ctx_k, ctx_v):
    bsz, h, s, dh = q.shape
    rows = s // GRID_W
    wr = min(NA_WIN_H, rows)
    ncb = GRID_W // NA_QW
    r = np.arange(rows)
    row_idx = np.clip(r - wr // 2, 0, rows - wr)[:, None] + np.arange(wr)[None, :]
    nb = np.arange(ncb)
    col_idx = np.clip(nb * NA_QW - NA_WIN_W // 2, 0, GRID_W - NA_KW)[:, None] + np.arange(NA_KW)[None, :]
    qcol = nb[:, None] * NA_QW + np.arange(NA_QW)[None, :]
    cs = np.clip(qcol - NA_WIN_W // 2, 0, GRID_W - NA_WIN_W)
    valid = (col_idx[:, None, :] >= cs[:, :, None]) & (col_idx[:, None, :] < cs[:, :, None] + NA_WIN_W)
    dc_idx = np.clip(col_idx[:, None, :] - qcol[:, :, None] + NA_WIN_W - 1, 0, 2 * NA_WIN_W - 2)
    dr_idx = row_idx - r[:, None] + NA_WIN_H - 1
    bias = rpb.astype(jnp.float32)[:, dr_idx[:, None, None, :, None], dc_idx[None, :, :, None, :]]
    bias = jnp.where(valid[None, None, :, :, None, :], bias, NEG_BIG).reshape(h, rows, ncb, NA_QW, wr * NA_KW)
    kg = k.reshape(bsz, h, rows, GRID_W, dh)
    vg = v.reshape(bsz, h, rows, GRID_W, dh)
    ri = row_idx[:, None, :, None]
    ci = col_idx[None, :, None, :]
    kb = kg[:, :, ri, ci].reshape(bsz, h, rows, ncb, wr * NA_KW, dh)
    vb = vg[:, :, ri, ci].reshape(bsz, h, rows, ncb, wr * NA_KW, dh)
    qb = q.reshape(bsz, h, rows, ncb, NA_QW, dh)
    scale = dh ** -0.5
    s_loc = jnp.einsum("bhrnqd,bhrnkd->bhrnqk", qb, kb).astype(jnp.float32) * scale + bias
    s_ctx = jnp.einsum("bhrnqd,bhcd->bhrnqc", qb, ctx_k).astype(jnp.float32) * scale
    p = jax.nn.softmax(jnp.concatenate([s_loc, s_ctx], axis=-1), axis=-1).astype(v.dtype)
    nloc = wr * NA_KW
    o = jnp.einsum("bhrnqk,bhrnkd->bhrnqd", p[..., :nloc], vb) + jnp.einsum("bhrnqc,bhcd->bhrnqd", p[..., nloc:], ctx_v)
    return o.reshape(bsz, h, s, dh)


def fourier_mix(u):
    b, l, _ = u.shape
    uf = u.astype(jnp.float32).reshape(b, l, D_GROUPS, D_GROUP_DIM)
    y = jnp.fft.fftn(uf, axes=(1, 3), norm="ortho").real
    return y.reshape(b, l, D_W).astype(u.dtype)


def moe(h, router_w, router_b, w_gate, w_up, w_down):
    t = h.shape[0]
    per = N_EXPERTS // N_GROUPS
    scores = jax.nn.sigmoid((h @ router_w).astype(jnp.float32))
    sel = scores + router_b.astype(jnp.float32)
    group_score = jnp.sum(lax.top_k(sel.reshape(t, N_GROUPS, per), 2)[0], axis=-1)
    best = jnp.argmax(group_score, axis=-1)
    in_group = (jnp.arange(N_EXPERTS) // per)[None, :] == best[:, None]
    _, idx = lax.top_k(jnp.where(in_group, sel, NEG_BIG), TOP_K)
    w = jnp.take_along_axis(scores, idx, axis=-1)
    w = w / jnp.sum(w, axis=-1, keepdims=True)
    gates = jnp.sum(jax.nn.one_hot(idx, N_EXPERTS, dtype=jnp.float32) * w[..., None], axis=1)
    a = jnp.einsum("td,edf->tef", h, w_gate)
    u = jnp.einsum("td,edf->tef", h, w_up)
    z = jax.nn.silu(a) * u * gates[:, :, None].astype(h.dtype)
    return jnp.einsum("tef,efd->td", z, w_down)


def trunk_layer(x, mod, layer_idx, norm1_g, norm2_g, w_in, w_out, lam_qk, subln_g, lb_f, lb_b, hgrn_norm_g, rpb,
                router_w, router_b, w_gate, w_up, w_down, ctx):
    bsz, l, _ = x.shape
    latent = ctx is not None
    shift1, scale1, gate1, shift2, scale2, gate2 = jnp.split(mod, 6, axis=-1)
    h = rmsnorm(x, norm1_g) * (1 + scale1) + shift1
    offsets = [int(o) for o in np.cumsum(PROJ_SPLITS)[:-1]]
    aq, ak, av, bq, bff, bfb, bv, bg, cq, ck, cv, du = jnp.split(h @ w_in, offsets, axis=-1)

    aq = aq.reshape(bsz, l, A_HEADS, 2, A_DIM).transpose(0, 2, 3, 1, 4)
    ak = ak.reshape(bsz, l, A_HEADS, 2, A_DIM).transpose(0, 2, 3, 1, 4)
    q1, q2, k1, k2 = aq[:, :, 0], aq[:, :, 1], ak[:, :, 0], ak[:, :, 1]
    va = split_heads(av, A_HEADS, 2 * A_DIM)
    if latent:
        q1, q2, k1, k2 = axial_rope(q1), axial_rope(q2), axial_rope(k1), axial_rope(k2)
    ka = jnp.concatenate([k1, k2], axis=-1)
    if latent:
        ka_all = jnp.concatenate([ctx[0], ka], axis=2)
        va_all = jnp.concatenate([ctx[1], va], axis=2)
    else:
        ka_all, va_all = ka, va
    lam_init = 0.8 - 0.6 * math.exp(-0.3 * layer_idx)
    lq = lam_qk.astype(jnp.float32)
    lam = jnp.exp(jnp.sum(lq[0] * lq[1])) - jnp.exp(jnp.sum(lq[2] * lq[3])) + lam_init
    o_a = diff_attention(q1, q2, ka_all[..., :A_DIM], ka_all[..., A_DIM:], va_all, lam)
    o_a = merge_heads(rmsnorm(o_a, subln_g) * (1.0 - lam_init))

    if latent:
        s0_f, s0_b = ctx[4][:, 0], ctx[4][:, 1]
    else:
        s0_f = jnp.zeros((bsz, B_HEADS, B_DK, B_DV), jnp.float32)
        s0_b = s0_f
    o_b, s_f, s_b = hgrn2_mixer(bq, bff, bfb, bv, bg, lb_f, lb_b, hgrn_norm_g, s0_f, s0_b)

    qc, kc, vc = split_heads(cq, C_HEADS, C_DIM), split_heads(ck, C_HEADS, C_DIM), split_heads(cv, C_HEADS, C_DIM)
    if latent:
        o_c = na_latent(qc, kc, vc, rpb, ctx[2], ctx[3])
    else:
        o_c = softmax_attention(qc, kc, vc)

    o_d = fourier_mix(du)

    mixed = jnp.concatenate([o_a, o_b, merge_heads(o_c), o_d], axis=-1) @ w_out
    x = x + gate1 * mixed
    h2 = rmsnorm(x, norm2_g) * (1 + scale2) + shift2
    x = x + gate2 * moe(h2.reshape(bsz * l, -1), router_w, router_b, w_gate, w_up, w_down).reshape(bsz, l, -1)
    if latent:
        return x
    return x, (ka, va, kc, vc, jnp.stack([s_f, s_b], axis=1).astype(x.dtype))


def setup_inputs(seed: int = 0) -> dict:
    key = jax.random.key(seed)
    ks = jax.random.split(key, 26)
    f32 = jnp.float32
    nrm = lambda k, shape, s: jax.random.normal(k, shape, f32) * s
    return {
        "x_prompt": nrm(ks[0], (BATCH, SEQ, D_MODEL), 1.0),
        "x_sample": nrm(ks[1], (DEC_BATCH, DEC_SEQ, D_MODEL), 1.0),
        "cache_diff_k": nrm(ks[2], (DEC_BATCH, DEPTH, A_HEADS, PAST_LEN, 2 * A_DIM), 1.0),
        "cache_diff_v": nrm(ks[3], (DEC_BATCH, DEPTH, A_HEADS, PAST_LEN, 2 * A_DIM), 1.0),
        "cache_na_k": nrm(ks[4], (DEC_BATCH, DEPTH, C_HEADS, PAST_LEN, C_DIM), 1.0),
        "cache_na_v": nrm(ks[5], (DEC_BATCH, DEPTH, C_HEADS, PAST_LEN, C_DIM), 1.0),
        "state_hgrn": nrm(ks[6], (DEC_BATCH, DEPTH, 2, B_HEADS, B_DK, B_DV), 0.5),
        "c": nrm(ks[7], (DEC_BATCH, D_MODEL), 1.0),
        "c_ctx": nrm(ks[8], (D_MODEL,), 1.0),
        "norm1_g": 1.0 + nrm(ks[9], (DEPTH, D_MODEL), 0.05),
        "norm2_g": 1.0 + nrm(ks[10], (DEPTH, D_MODEL), 0.05),
        "ada_w": nrm(ks[11], (DEPTH, D_MODEL, 6 * D_MODEL), 0.5 * D_MODEL ** -0.5),
        "ada_b": nrm(ks[12], (DEPTH, 6 * D_MODEL), 0.02),
        "w_in": nrm(ks[13], (DEPTH, D_MODEL, PROJ_W), D_MODEL ** -0.5),
        "w_out": nrm(ks[14], (DEPTH, MIX_W, D_MODEL), MIX_W ** -0.5),
        "diff_lambda": nrm(ks[15], (DEPTH, 4, A_DIM), 0.1),
        "diff_subln_g": 1.0 + nrm(ks[16], (DEPTH, 2 * A_DIM), 0.05),
        "hgrn_lb_logits": nrm(ks[17], (DEPTH, 2, B_KW), 0.5),
        "hgrn_norm_g": 1.0 + nrm(ks[18], (DEPTH, B_DV), 0.05),
        "na_rpb": nrm(ks[19], (DEPTH, C_HEADS, 2 * NA_WIN_H - 1, 2 * NA_WIN_W - 1), 0.5),
        "router_w": nrm(ks[20], (D_MODEL, N_EXPERTS), D_MODEL ** -0.5),
        "router_b": nrm(ks[21], (N_EXPERTS,), 0.01),
        "moe_w_gate": nrm(ks[22], (DEPTH, N_EXPERTS, D_MODEL, D_EXPERT), D_MODEL ** -0.5),
        "moe_w_up": nrm(ks[23], (DEPTH, N_EXPERTS, D_MODEL, D_EXPERT), D_MODEL ** -0.5),
        "moe_w_down": nrm(ks[24], (DEPTH, N_EXPERTS, D_EXPERT, D_MODEL), D_EXPERT ** -0.5),
        "final_norm_g": 1.0 + nrm(ks[25], (D_MODEL,), 0.05),
    }


def reference(x_prompt, x_sample, cache_diff_k, cache_diff_v, cache_na_k, cache_na_v, state_hgrn, c, c_ctx,
              norm1_g, norm2_g, ada_w, ada_b, w_in, w_out, diff_lambda, diff_subln_g, hgrn_lb_logits, hgrn_norm_g,
              na_rpb, router_w, router_b, moe_w_gate, moe_w_up, moe_w_down, final_norm_g):
    lb_sm = jax.nn.softmax(hgrn_lb_logits.astype(jnp.float32), axis=0)
    lb_all = jnp.cumsum(lb_sm, axis=0) - lb_sm[0:1]
    xp, xs = x_prompt, x_sample
    dk_l, dv_l, nk_l, nv_l, hs_l = [], [], [], [], []
    for l in range(DEPTH):
        mod_ctx = (jax.nn.silu(c_ctx) @ ada_w[l] + ada_b[l])[None, None, :]
        mod_lat = (jax.nn.silu(c) @ ada_w[l] + ada_b[l])[:, None, :]
        shared = (norm1_g[l], norm2_g[l], w_in[l], w_out[l], diff_lambda[l], diff_subln_g[l], lb_all[l, 0],
                  lb_all[l, 1], hgrn_norm_g[l], na_rpb[l], router_w, router_b, moe_w_gate[l], moe_w_up[l],
                  moe_w_down[l])
        xp, (dk, dv, nk, nv, hs) = trunk_layer(xp, mod_ctx, l, *shared, None)
        dk_l.append(dk)
        dv_l.append(dv)
        nk_l.append(nk)
        nv_l.append(nv)
        hs_l.append(hs)
        xs = trunk_layer(xs, mod_lat, l, *shared,
                         (cache_diff_k[:, l], cache_diff_v[:, l], cache_na_k[:, l], cache_na_v[:, l], state_hgrn[:, l]))
    y_prompt = rmsnorm(xp, final_norm_g)
    y_sample = rmsnorm(xs, final_norm_g)
    new_diff_k = jnp.stack(dk_l, axis=1)
    new_diff_v = jnp.stack(dv_l, axis=1)
    new_na_k = jnp.stack(nk_l, axis=1)
    new_na_v = jnp.stack(nv_l, axis=1)
    new_state_hgrn = jnp.stack(hs_l, axis=1)
    return (y_prompt, y_sample, new_diff_k, new_diff_v, new_na_k, new_na_v, new_state_hgrn)
```

```python
import functools
import math

import numpy as np
import jax
import jax.numpy as jnp
from jax import lax
from jax.experimental import pallas as pl
from jax.experimental.pallas import tpu as pltpu

F32 = jnp.float32
BF16 = jnp.bfloat16
HIGHEST = lax.Precision.HIGHEST

D_MODEL = 1024
BATCH = 16
SEQ = 256
DEPTH = 2
DEC_BATCH = 2
DEC_SEQ = 2048
PAST_LEN = 256
GRID_W = 64
GRID_H = DEC_SEQ // GRID_W
EPS = 1e-6
NEG_BIG = -1e30
HEADS = 4
HEAD_DIM = 64
MIX_BLK = HEADS * HEAD_DIM
A_DIM = 32
ROPE_BASE = 10000.0
B_CHUNK = 32
NA_WIN_H = 8
NA_WIN_W = 16
N_EXPERTS = 16
N_GROUPS = 4
D_EXPERT = 512
PROJ_W = 12 * MIX_BLK
TP = BATCH * SEQ
TL = DEC_BATCH * DEC_SEQ
T = TP + TL
TM = 256
N_TILES = T // TM
CTX_TILES = TP // TM
LAT_TILES_PER_SEQ = DEC_SEQ // TM
(C_AQ, C_AK, C_AV, C_BQ, C_BFF, C_BFB, C_BV, C_BG, C_CQ, C_CK, C_CV, C_DU) = range(12)
NA_SLAB_ROWS = 12
NA_SLAB = NA_SLAB_ROWS * GRID_W
VMEM_LIMIT = 56 * 1024 * 1024


def _cparams(sem):
    return pltpu.CompilerParams(dimension_semantics=sem, vmem_limit_bytes=VMEM_LIMIT)


def _nt_dot(a, b, precision=None):
    return lax.dot_general(a, b, (((1,), (1,)), ((), ())), precision=precision,
                           preferred_element_type=F32)


def _head_lanes(width=MIX_BLK):
    return lax.broadcasted_iota(jnp.int32, (1, width), 1)


def _lane_range(lane, lo, n):
    return (lane >= lo) & (lane < lo + n)


def _same_head_matrix():
    r = lax.broadcasted_iota(jnp.int32, (MIX_BLK, MIX_BLK), 0) // HEAD_DIM
    c = lax.broadcasted_iota(jnp.int32, (MIX_BLK, MIX_BLK), 1) // HEAD_DIM
    return r == c


def _head_mean_square(o):
    ones = jnp.where(_same_head_matrix(), 1.0, 0.0).astype(F32)
    return jnp.dot(o * o, ones, precision=HIGHEST, preferred_element_type=F32) * (1.0 / HEAD_DIM)


def _mod_row(i):
    return jnp.where(i < CTX_TILES, 0, 1 + (i - CTX_TILES) // LAT_TILES_PER_SEQ)


def _mod_kernel(c_ref, w_ref, b_ref, o_ref):
    w = w_ref[0]
    for r in range(3):
        c = c_ref[r]
        s = c * jax.nn.sigmoid(c)
        o_ref[0, r:r + 1, :] = jnp.sum(s * w, axis=0, keepdims=True) + b_ref[0]


def modulation(c_rows, ada_w, ada_b):
    nt = 768
    n_out = 6 * D_MODEL
    return pl.pallas_call(
        _mod_kernel,
        grid=(DEPTH, n_out // nt),
        in_specs=[
            pl.BlockSpec((3, D_MODEL, 1), lambda l, j: (0, 0, 0)),
            pl.BlockSpec((1, D_MODEL, nt), lambda l, j: (l, 0, j)),
            pl.BlockSpec((1, 1, nt), lambda l, j: (l, 0, j)),
        ],
        out_specs=pl.BlockSpec((1, 3, nt), lambda l, j: (l, 0, j)),
        out_shape=jax.ShapeDtypeStruct((DEPTH, 3, n_out), F32),
        compiler_params=_cparams(("arbitrary", "arbitrary")),
        name="modulation",
    )(c_rows[:, :, None], ada_w, ada_b[:, None, :])


def _proj_kernel(x_ref, mod_ref, g_ref, w_ref, cos_ref, sa_ref, sb_ref, o_ref):
    x = x_ref[...]
    ms = jnp.mean(x * x, axis=-1, keepdims=True)
    mod = mod_ref[0]
    h = x * lax.rsqrt(ms + EPS) * g_ref[...] * (1.0 + mod[1:2]) + mod[0:1]
    p = jnp.dot(h.astype(BF16), w_ref[...], preferred_element_type=F32)
    t = p[:, :2 * MIX_BLK]
    o_ref[:, :2 * MIX_BLK] = (t * cos_ref[...] + pltpu.roll(t, 1, 1) * sa_ref[...]
                              + pltpu.roll(t, 2 * MIX_BLK - 1, 1) * sb_ref[...])
    o_ref[:, 2 * MIX_BLK:] = p[:, 2 * MIX_BLK:]


def _rope_tables():
    nf = A_DIM // 4
    freqs = ROPE_BASE ** (-jnp.arange(nf, dtype=F32) / nf)
    pos = jnp.arange(DEC_SEQ)
    row = (pos // GRID_W).astype(F32)
    col = (pos % GRID_W).astype(F32)
    ang = jnp.concatenate([row[:, None] * freqs, col[:, None] * freqs], axis=-1)
    cos = jnp.repeat(jnp.cos(ang), 2, axis=-1)
    sin = jnp.repeat(jnp.sin(ang), 2, axis=-1)
    odd = (jnp.arange(A_DIM) % 2 == 1)[None, :]
    sin_from_left = jnp.where(odd, sin, 0.0)
    sin_from_right = jnp.where(odd, 0.0, -sin)
    reps = 2 * MIX_BLK // A_DIM
    ident = (jnp.ones((TM, 2 * MIX_BLK), F32), jnp.zeros((TM, 2 * MIX_BLK), F32))
    return (jnp.concatenate([jnp.tile(cos, (1, reps)), ident[0]], axis=0),
            jnp.concatenate([jnp.tile(sin_from_left, (1, reps)), ident[1]], axis=0),
            jnp.concatenate([jnp.tile(sin_from_right, (1, reps)), ident[1]], axis=0))


def projection(x, mod, norm_g, w_in_bf16, rope):
    def rope_idx(i):
        return (jnp.where(i < CTX_TILES, LAT_TILES_PER_SEQ, (i - CTX_TILES) % LAT_TILES_PER_SEQ), 0)

    rope_spec = pl.BlockSpec((TM, 2 * MIX_BLK), rope_idx)
    return pl.pallas_call(
        _proj_kernel,
        grid=(N_TILES,),
        in_specs=[
            pl.BlockSpec((TM, D_MODEL), lambda i: (i, 0)),
            pl.BlockSpec((1, 6, D_MODEL), lambda i: (_mod_row(i), 0, 0)),
            pl.BlockSpec((1, D_MODEL), lambda i: (0, 0)),
            pl.BlockSpec((D_MODEL, PROJ_W), lambda i: (0, 0)),
            rope_spec, rope_spec, rope_spec,
        ],
        out_specs=pl.BlockSpec((TM, PROJ_W), lambda i: (i, 0)),
        out_shape=jax.ShapeDtypeStruct((T, PROJ_W), F32),
        compiler_params=_cparams(("arbitrary",)),
        name="projection",
    )(x, mod, norm_g[None, :], w_in_bf16, *rope)


def _softmax_rows(s):
    m = jnp.max(s, axis=-1, keepdims=True)
    e = jnp.exp(s - m)
    return e / jnp.sum(e, axis=-1, keepdims=True)


def _attn_kernel(lam_ref, q_ref, k_ref, v_ref, g_ref, o_ref, *, n_maps, post_scale):
    q = q_ref[...]
    kb = k_ref[...].astype(BF16)
    vb = v_ref[...].astype(BF16)
    lane = _head_lanes()
    map_dim = HEAD_DIM // n_maps
    scale = map_dim ** -0.5
    o = jnp.zeros(q.shape, F32)
    for h in range(HEADS):
        probs = []
        for j in range(n_maps):
            qm = jnp.where(_lane_range(lane, h * HEAD_DIM + j * map_dim, map_dim), q, 0.0)
            probs.append(_softmax_rows(_nt_dot(qm.astype(BF16), kb) * scale))
        w = probs[0] - lam_ref[0] * probs[1] if n_maps == 2 else probs[0]
        oh = jnp.dot(w.astype(BF16), vb, preferred_element_type=F32)
        o = jnp.where(_lane_range(lane, h * HEAD_DIM, HEAD_DIM), oh, o)
    if n_maps == 2:
        o = o * lax.rsqrt(_head_mean_square(o) + EPS) * g_ref[...] * post_scale
    o_ref[...] = o


def attention(q_arr, q_idx, k_arr, k_idx, v_arr, v_idx, n_seq, q_len, kv_len, lam, norm_g, *,
              n_maps, post_scale):
    kern = functools.partial(_attn_kernel, n_maps=n_maps, post_scale=post_scale)
    return pl.pallas_call(
        kern,
        grid=(n_seq, q_len // TM),
        in_specs=[
            pl.BlockSpec(memory_space=pltpu.SMEM),
            pl.BlockSpec((TM, MIX_BLK), q_idx),
            pl.BlockSpec((kv_len, MIX_BLK), k_idx),
            pl.BlockSpec((kv_len, MIX_BLK), v_idx),
            pl.BlockSpec((1, MIX_BLK), lambda b, i: (0, 0)),
        ],
        out_specs=pl.BlockSpec((TM, MIX_BLK), lambda b, i: (b * (q_len // TM) + i, 0)),
        out_shape=jax.ShapeDtypeStruct((n_seq * q_len, MIX_BLK), F32),
        compiler_params=_cparams(("arbitrary", "arbitrary")),
        name="attention_%dmap_%d" % (n_maps, kv_len),
    )(lam, q_arr, k_arr, v_arr, norm_g)


def _na_slab_start(i):
    return jnp.clip(i - 1, 0, GRID_H // 4 - NA_SLAB_ROWS // 4)


def _na_kernel(q_ref, k_ref, v_ref, kc_ref, vc_ref, bias_ref, o_ref):
    i = pl.program_id(1)
    start = pl.multiple_of(_na_slab_start(i) * TM, TM)
    ks = k_ref[pl.ds(start, NA_SLAB), :].astype(BF16)
    vs = v_ref[pl.ds(start, NA_SLAB), :].astype(BF16)
    kc = kc_ref[...].astype(BF16)
    vc = vc_ref[...].astype(BF16)
    q = q_ref[...]
    lane = _head_lanes()
    scale = HEAD_DIM ** -0.5
    o = jnp.zeros(q.shape, F32)
    for h in range(HEADS):
        in_head = _lane_range(lane, h * HEAD_DIM, HEAD_DIM)
        qm = jnp.where(in_head, q, 0.0).astype(BF16)
        s_loc = _nt_dot(qm, ks) * scale + bias_ref[0, h]
        s_ctx = _nt_dot(qm, kc) * scale
        m = jnp.maximum(jnp.max(s_loc, axis=-1, keepdims=True), jnp.max(s_ctx, axis=-1, keepdims=True))
        e_loc = jnp.exp(s_loc - m)
        e_ctx = jnp.exp(s_ctx - m)
        den = jnp.sum(e_loc, axis=-1, keepdims=True) + jnp.sum(e_ctx, axis=-1, keepdims=True)
        oh = (jnp.dot(e_loc.astype(BF16), vs, preferred_element_type=F32)
              + jnp.dot(e_ctx.astype(BF16), vc, preferred_element_type=F32)) / den
        o = jnp.where(in_head, oh, o)
    o_ref[...] = o


def _na_bias_tables(rpb):
    tables = []
    for tile in (0, 1, GRID_H // 4 - 1):
        slab0 = int(np.clip(tile - 1, 0, GRID_H // 4 - NA_SLAB_ROWS // 4)) * 4
        tq = np.arange(TM)
        ts = np.arange(NA_SLAB)
        rq = (tile * 4 + tq // GRID_W)[:, None]
        cq = (tq % GRID_W)[:, None]
        rk = (slab0 + ts // GRID_W)[None, :]
        ck = (ts % GRID_W)[None, :]
        wr0 = np.clip(rq - NA_WIN_H // 2, 0, GRID_H - NA_WIN_H)
        wc0 = np.clip(cq - NA_WIN_W // 2, 0, GRID_W - NA_WIN_W)
        valid = (rk >= wr0) & (rk < wr0 + NA_WIN_H) & (ck >= wc0) & (ck < wc0 + NA_WIN_W)
        dr = np.clip(rk - rq + NA_WIN_H - 1, 0, 2 * NA_WIN_H - 2) + 0 * ck
        dc = np.clip(ck - cq + NA_WIN_W - 1, 0, 2 * NA_WIN_W - 2) + 0 * rk
        tables.append(jnp.where(valid[None], rpb.astype(F32)[:, dr, dc], NEG_BIG))
    return jnp.stack(tables, axis=0)


def na_latent(p, kc, vc, bias):
    n_t = LAT_TILES_PER_SEQ
    seq_blk0 = TP // DEC_SEQ

    def bias_idx(b, i):
        return (jnp.minimum(i, 1) + i // (n_t - 1), 0, 0, 0)

    return pl.pallas_call(
        _na_kernel,
        grid=(DEC_BATCH, n_t),
        in_specs=[
            pl.BlockSpec((TM, MIX_BLK), lambda b, i: (CTX_TILES + b * n_t + i, C_CQ)),
            pl.BlockSpec((DEC_SEQ, MIX_BLK), lambda b, i: (seq_blk0 + b, C_CK)),
            pl.BlockSpec((DEC_SEQ, MIX_BLK), lambda b, i: (seq_blk0 + b, C_CV)),
            pl.BlockSpec((PAST_LEN, MIX_BLK), lambda b, i: (b, 0)),
            pl.BlockSpec((PAST_LEN, MIX_BLK), lambda b, i: (b, 0)),
            pl.BlockSpec((1, HEADS, TM, NA_SLAB), bias_idx),
        ],
        out_specs=pl.BlockSpec((TM, MIX_BLK), lambda b, i: (b * n_t + i, 0)),
        out_shape=jax.ShapeDtypeStruct((TL, MIX_BLK), F32),
        compiler_params=_cparams(("arbitrary", "arbitrary")),
        name="na_latent",
    )(p, p, p, kc, vc, bias)


def _chunk_matrices():
    r = lax.broadcasted_iota(jnp.int32, (TM, TM), 0)
    c = lax.broadcasted_iota(jnp.int32, (TM, TM), 1)
    same = (r // B_CHUNK) == (c // B_CHUNK)
    return same & (c <= r), same & (c >= r)


def _hgrn_direction(q_ref, f_ref, v_ref, lb, st_ref, o_ref, reverse):
    n_ch = TM // B_CHUNK
    lower, upper = _chunk_matrices()
    tri = upper if reverse else lower
    zq = q_ref[...]
    q = zq * jax.nn.sigmoid(zq)
    z = f_ref[...]
    logf = jnp.log(lb + (1.0 - lb) * jax.nn.sigmoid(z))
    kk = (1.0 - lb) * jax.nn.sigmoid(-z)
    v = v_ref[...]
    b = jnp.dot(jnp.where(tri, 1.0, 0.0).astype(F32), logf, precision=HIGHEST,
                preferred_element_type=F32)
    b3 = b.reshape(n_ch, B_CHUNK, MIX_BLK)
    mid = B_CHUNK // 2 if reverse else B_CHUNK // 2 - 1
    end = 0 if reverse else B_CHUNK - 1
    b_mid = b3[:, mid:mid + 1, :]
    b_end = b3[:, end:end + 1, :]
    q3 = q.reshape(n_ch, B_CHUNK, MIX_BLK)
    k3 = kk.reshape(n_ch, B_CHUNK, MIX_BLK)
    q_in = (q3 * jnp.exp(b3 - b_mid)).reshape(TM, MIX_BLK)
    k_in = (k3 * jnp.exp(b_mid - b3)).reshape(TM, MIX_BLK).astype(BF16)
    q_dec = (q3 * jnp.exp(b3)).reshape(TM, MIX_BLK).astype(BF16)
    k_dec = (k3 * jnp.exp(b_end - b3)).reshape(TM, MIX_BLK)
    g = jnp.exp(b_end)
    vb = v.astype(BF16)
    vt = v.T.astype(BF16)
    lane = _head_lanes()
    o = jnp.zeros((TM, MIX_BLK), F32)
    for h in range(HEADS):
        in_head = _lane_range(lane, h * HEAD_DIM, HEAD_DIM)
        a = _nt_dot(jnp.where(in_head, q_in, 0.0).astype(BF16), k_in)
        a = jnp.where(tri, a, 0.0).astype(BF16)
        o = jnp.where(in_head, jnp.dot(a, vb, preferred_element_type=F32), o)
    same_head = _same_head_matrix()
    row_chunk = lax.broadcasted_iota(jnp.int32, (TM, 1), 0) // B_CHUNK
    st = st_ref[...]
    order = range(n_ch - 1, -1, -1) if reverse else range(n_ch)
    for c in order:
        rows = slice(c * B_CHUNK, (c + 1) * B_CHUNK)
        o_ref[rows, :] = o[rows, :] + _nt_dot(q_dec[rows, :], st.astype(BF16))
        kv = jnp.dot(vt, jnp.where(row_chunk == c, k_dec, 0.0).astype(BF16),
                     preferred_element_type=F32)
        st = st * g[c] + jnp.where(same_head, kv, 0.0)
    st_ref[...] = st


def _hgrn_kernel(qf_ref, ff_ref, vf_ref, qb_ref, fb_ref, vb_ref, lb_ref, s0_ref,
                 of_ref, ob_ref, s_ref, stf_ref, stb_ref, *, has_s0):
    j = pl.program_id(1)

    @pl.when(j == 0)
    def _():
        if has_s0:
            stf_ref[...] = s0_ref[0, 0]
            stb_ref[...] = s0_ref[0, 1]
        else:
            stf_ref[...] = jnp.zeros((MIX_BLK, MIX_BLK), F32)
            stb_ref[...] = jnp.zeros((MIX_BLK, MIX_BLK), F32)

    lb = lb_ref[...]
    _hgrn_direction(qf_ref, ff_ref, vf_ref, lb[0:1], stf_ref, of_ref, False)
    _hgrn_direction(qb_ref, fb_ref, vb_ref, lb[1:2], stb_ref, ob_ref, True)

    @pl.when(j == pl.num_programs(1) - 1)
    def _():
        s_ref[0, 0] = stf_ref[...]
        s_ref[0, 1] = stb_ref[...]


def hgrn(p, row_tile0, n_seq, seq_len, lb, s0):
    nb = seq_len // TM
    has_s0 = s0 is not None
    if s0 is None:
        s0 = jnp.zeros((1, 2, MIX_BLK, MIX_BLK), F32)

    def fwd(col):
        return pl.BlockSpec((TM, MIX_BLK), lambda s, j: (row_tile0 + s * nb + j, col))

    def bwd(col):
        return pl.BlockSpec((TM, MIX_BLK), lambda s, j: (row_tile0 + s * nb + nb - 1 - j, col))

    state_spec = pl.BlockSpec((1, 2, MIX_BLK, MIX_BLK), lambda s, j: (s if has_s0 else 0, 0, 0, 0))
    out_rows = n_seq * seq_len
    return pl.pallas_call(
        functools.partial(_hgrn_kernel, has_s0=has_s0),
        grid=(n_seq, nb),
        in_specs=[fwd(C_BQ), fwd(C_BFF), fwd(C_BV), bwd(C_BQ), bwd(C_BFB), bwd(C_BV),
                  pl.BlockSpec((2, MIX_BLK), lambda s, j: (0, 0)), state_spec],
        out_specs=[
            pl.BlockSpec((TM, MIX_BLK), lambda s, j: (s * nb + j, 0)),
            pl.BlockSpec((TM, MIX_BLK), lambda s, j: (s * nb + nb - 1 - j, 0)),
            pl.BlockSpec((1, 2, MIX_BLK, MIX_BLK), lambda s, j: (s, 0, 0, 0)),
        ],
        out_shape=[jax.ShapeDtypeStruct((out_rows, MIX_BLK), F32),
                   jax.ShapeDtypeStruct((out_rows, MIX_BLK), F32),
                   jax.ShapeDtypeStruct((n_seq, 2, MIX_BLK, MIX_BLK), F32)],
        scratch_shapes=[pltpu.VMEM((MIX_BLK, MIX_BLK), F32), pltpu.VMEM((MIX_BLK, MIX_BLK), F32)],
        compiler_params=_cparams(("arbitrary", "arbitrary")),
        name="hgrn_%d" % seq_len,
    )(p, p, p, p, p, p, lb, s0)


def _state_to_blockdiag(s):
    st = jnp.swapaxes(s.astype(F32), -1, -2)
    eye = jnp.eye(HEADS, dtype=F32)
    full = st[:, :, :, :, None, :] * eye[None, None, :, None, :, None]
    return full.reshape(s.shape[0], 2, MIX_BLK, MIX_BLK)


def _blockdiag_to_state(st):
    n = st.shape[0]
    r = st.reshape(n, 2, HEADS, HEAD_DIM, HEADS, HEAD_DIM)
    diag = jnp.stack([r[:, :, h, :, h, :] for h in range(HEADS)], axis=2)
    return jnp.swapaxes(diag, -1, -2)


def _fft_kernel(u_ref, c64_ref, s64_ref, cl_ref, sl_ref, o_ref, a_ref, b_ref, *, norm):
    @pl.when(pl.program_id(1) == 0)
    def _():
        u = u_ref[...].astype(BF16)
        a_ref[...] = jnp.dot(u, c64_ref[...], preferred_element_type=F32).astype(BF16)
        b_ref[...] = jnp.dot(u, s64_ref[...], preferred_element_type=F32).astype(BF16)

    o_ref[...] = (jnp.dot(cl_ref[...], a_ref[...], preferred_element_type=F32)
                  - jnp.dot(sl_ref[...], b_ref[...], preferred_element_type=F32)) * norm


def _dft_tables(n):
    k = np.arange(n)
    ang = 2.0 * np.pi * ((k[:, None] * k[None, :]) % n) / n
    return np.cos(ang), np.sin(ang)


def _dft_constants(seq_len):
    c64, s64 = _dft_tables(HEAD_DIM)
    eye = np.eye(HEADS)
    cl, sl = _dft_tables(seq_len)
    as_bf16 = lambda a: jnp.asarray(a, F32).astype(BF16)
    return as_bf16(np.kron(eye, c64)), as_bf16(np.kron(eye, s64)), as_bf16(cl), as_bf16(sl)


def fourier_mix(p, row_blk0, n_seq, seq_len, consts):
    c64, s64, cl, sl = consts
    nb = seq_len // TM
    norm = 1.0 / math.sqrt(seq_len * HEAD_DIM)
    return pl.pallas_call(
        functools.partial(_fft_kernel, norm=norm),
        grid=(n_seq, nb),
        in_specs=[
            pl.BlockSpec((seq_len, MIX_BLK), lambda s, i: (row_blk0 + s, C_DU)),
            pl.BlockSpec((MIX_BLK, MIX_BLK), lambda s, i: (0, 0)),
            pl.BlockSpec((MIX_BLK, MIX_BLK), lambda s, i: (0, 0)),
            pl.BlockSpec((TM, seq_len), lambda s, i: (i, 0)),
            pl.BlockSpec((TM, seq_len), lambda s, i: (i, 0)),
        ],
        out_specs=pl.BlockSpec((TM, MIX_BLK), lambda s, i: (s * nb + i, 0)),
        out_shape=jax.ShapeDtypeStruct((n_seq * seq_len, MIX_BLK), F32),
        scratch_shapes=[pltpu.VMEM((seq_len, MIX_BLK), BF16), pltpu.VMEM((seq_len, MIX_BLK), BF16)],
        compiler_params=_cparams(("arbitrary", "arbitrary")),
        name="fourier_%d" % seq_len,
    )(p, c64, s64, cl, sl)


def _route(logits_t, rb):
    per = N_EXPERTS // N_GROUPS
    score = [jax.nn.sigmoid(logits_t[e:e + 1, :]) for e in range(N_EXPERTS)]
    sel = [score[e] + rb[e:e + 1, :] for e in range(N_EXPERTS)]
    gscore = []
    for g in range(N_GROUPS):
        vals = sel[g * per:(g + 1) * per]
        best = None
        for a in range(per):
            for b in range(a + 1, per):
                pair = vals[a] + vals[b]
                best = pair if best is None else jnp.maximum(best, pair)
        gscore.append(best)
    chosen = []
    for g in range(N_GROUPS):
        ok = None
        for j in range(N_GROUPS):
            if j == g:
                continue
            cond = gscore[g] > gscore[j] if j < g else gscore[g] >= gscore[j]
            ok = cond if ok is None else ok & cond
        chosen.append(ok)
    picked = []
    for e in range(N_EXPERTS):
        g = e // per
        rank = jnp.zeros_like(sel[e])
        for j in range(g * per, (g + 1) * per):
            if j == e:
                continue
            ahead = sel[j] >= sel[e] if j < e else sel[j] > sel[e]
            rank = rank + jnp.where(ahead, 1.0, 0.0)
        picked.append(chosen[g] & (rank < 2.0))
    wsum = jnp.zeros_like(score[0])
    for e in range(N_EXPERTS):
        wsum = wsum + jnp.where(picked[e], score[e], 0.0)
    return jnp.concatenate([jnp.where(picked[e], score[e] / wsum, 0.0) for e in range(N_EXPERTS)], axis=0)


def _out_kernel(x_ref, oa_ref, of_ref, ob_ref, oc_ref, od_ref, bg_ref, mod_ref, hg_ref, w_ref,
                g2_ref, rw_ref, rb_ref, x1_ref, h2_ref, gates_ref):
    mod = mod_ref[0]
    hb = of_ref[...] + ob_ref[...]
    zg = bg_ref[...]
    hb = hb * lax.rsqrt(_head_mean_square(hb) + EPS) * hg_ref[...] * (zg * jax.nn.sigmoid(zg))
    parts = (oa_ref[...], hb, oc_ref[...], od_ref[...])
    mixed = jnp.zeros((TM, D_MODEL), F32)
    for n, part in enumerate(parts):
        mixed = mixed + jnp.dot(part.astype(BF16), w_ref[n * MIX_BLK:(n + 1) * MIX_BLK, :],
                                preferred_element_type=F32)
    x1 = x_ref[...] + mod[2:3] * mixed
    x1_ref[...] = x1
    ms = jnp.mean(x1 * x1, axis=-1, keepdims=True)
    h2 = x1 * lax.rsqrt(ms + EPS) * g2_ref[...] * (1.0 + mod[4:5]) + mod[3:4]
    h2_ref[...] = h2.astype(BF16)
    gates_ref[...] = _route(_nt_dot(rw_ref[...], h2, precision=HIGHEST), rb_ref[...])


def out_and_route(x, o_a, o_f, o_b, o_c, o_d, p, mod, hgrn_g, w_out_bf16, norm2_g, router_w, router_b):
    tile = lambda w: pl.BlockSpec((TM, w), lambda i: (i, 0))
    full = lambda r, c: pl.BlockSpec((r, c), lambda i: (0, 0))
    return pl.pallas_call(
        _out_kernel,
        grid=(N_TILES,),
        in_specs=[
            tile(D_MODEL), tile(MIX_BLK), tile(MIX_BLK), tile(MIX_BLK), tile(MIX_BLK), tile(MIX_BLK),
            pl.BlockSpec((TM, MIX_BLK), lambda i: (i, C_BG)),
            pl.BlockSpec((1, 6, D_MODEL), lambda i: (_mod_row(i), 0, 0)),
            full(1, MIX_BLK), full(D_MODEL, D_MODEL), full(1, D_MODEL),
            full(N_EXPERTS, D_MODEL), full(N_EXPERTS, 1),
        ],
        out_specs=[tile(D_MODEL), tile(D_MODEL), pl.BlockSpec((N_EXPERTS, TM), lambda i: (0, i))],
        out_shape=[jax.ShapeDtypeStruct((T, D_MODEL), F32),
                   jax.ShapeDtypeStruct((T, D_MODEL), BF16),
                   jax.ShapeDtypeStruct((N_EXPERTS, T), F32)],
        compiler_params=_cparams(("arbitrary",)),
        name="out_and_route",
    )(x, o_a, o_f, o_b, o_c, o_d, p, mod, jnp.tile(hgrn_g, HEADS)[None, :], w_out_bf16,
      norm2_g[None, :], router_w.T, router_b[:, None])


def _moe_kernel(h_ref, gates_ref, wg_ref, wu_ref, wd_ref, x1_ref, mod_ref, fg_ref, o_ref, acc_ref,
                *, final):
    e = pl.program_id(1)

    @pl.when(e == 0)
    def _():
        acc_ref[...] = jnp.zeros(acc_ref.shape, F32)

    h = h_ref[...]
    a = jnp.dot(h, wg_ref[0], preferred_element_type=F32)
    u = jnp.dot(h, wu_ref[0], preferred_element_type=F32)
    lane = lax.broadcasted_iota(jnp.int32, (1, N_EXPERTS), 1)
    gate = jnp.sum(jnp.where(lane == e, gates_ref[...], 0.0), axis=-1, keepdims=True)
    zz = a * jax.nn.sigmoid(a) * u * gate
    acc_ref[...] += jnp.dot(zz.astype(BF16), wd_ref[0], preferred_element_type=F32)

    @pl.when(e == N_EXPERTS - 1)
    def _():
        x2 = x1_ref[...] + mod_ref[0][5:6] * acc_ref[...]
        if final:
            ms = jnp.mean(x2 * x2, axis=-1, keepdims=True)
            x2 = x2 * lax.rsqrt(ms + EPS) * fg_ref[...]
        o_ref[...] = x2


def moe(h2, gates, wg, wu, wd, x1, mod, final_g, final):
    return pl.pallas_call(
        functools.partial(_moe_kernel, final=final),
        grid=(N_TILES, N_EXPERTS),
        in_specs=[
            pl.BlockSpec((TM, D_MODEL), lambda i, e: (i, 0)),
            pl.BlockSpec((TM, N_EXPERTS), lambda i, e: (i, 0)),
            pl.BlockSpec((1, D_MODEL, D_EXPERT), lambda i, e: (e, 0, 0)),
            pl.BlockSpec((1, D_MODEL, D_EXPERT), lambda i, e: (e, 0, 0)),
            pl.BlockSpec((1, D_EXPERT, D_MODEL), lambda i, e: (e, 0, 0)),
            pl.BlockSpec((TM, D_MODEL), lambda i, e: (i, 0)),
            pl.BlockSpec((1, 6, D_MODEL), lambda i, e: (_mod_row(i), 0, 0)),
            pl.BlockSpec((1, D_MODEL), lambda i, e: (0, 0)),
        ],
        out_specs=pl.BlockSpec((TM, D_MODEL), lambda i, e: (i, 0)),
        out_shape=jax.ShapeDtypeStruct((T, D_MODEL), F32),
        scratch_shapes=[pltpu.VMEM((TM, D_MODEL), F32)],
        compiler_params=_cparams(("arbitrary", "arbitrary")),
        name="moe_final" if final else "moe",
    )(h2, gates, wg, wu, wd, x1, mod, final_g[None, :])


def _heads_to_lanes(t):
    b, h, l, d = t.shape
    return t.transpose(0, 2, 1, 3).reshape(b * l, h * d)


def _lanes_to_heads(t, b):
    return t.reshape(b, -1, HEADS, HEAD_DIM).transpose(0, 2, 1, 3)


def _col(p, c, rows=slice(None)):
    return p[rows, c * MIX_BLK:(c + 1) * MIX_BLK]


def kernel(x_prompt, x_sample, cache_diff_k, cache_diff_v, cache_na_k, cache_na_v, state_hgrn, c, c_ctx,
           norm1_g, norm2_g, ada_w, ada_b, w_in, w_out, diff_lambda, diff_subln_g, hgrn_lb_logits,
           hgrn_norm_g, na_rpb, router_w, router_b, moe_w_gate, moe_w_up, moe_w_down, final_norm_g):
    x = jnp.concatenate([x_prompt.reshape(TP, D_MODEL), x_sample.reshape(TL, D_MODEL)], axis=0)
    mods = modulation(jnp.concatenate([c_ctx[None, :], c], axis=0), ada_w, ada_b)
    mods = mods.reshape(DEPTH, 3, 6, D_MODEL)
    lb_sm = jax.nn.softmax(hgrn_lb_logits.astype(F32), axis=0)
    lb_all = jnp.cumsum(lb_sm, axis=0) - lb_sm[0:1]
    rope = _rope_tables()
    dft_ctx = _dft_constants(SEQ)
    dft_lat = _dft_constants(DEC_SEQ)
    lat_tile0 = CTX_TILES
    n_lt = LAT_TILES_PER_SEQ
    lat = slice(TP, T)
    ctx = slice(0, TP)
    caches = [[], [], [], [], []]
    for l in range(DEPTH):
        p = projection(x, mods[l], norm1_g[l], w_in[l].astype(BF16), rope)

        lq = diff_lambda[l].astype(F32)
        lam_init = 0.8 - 0.6 * math.exp(-0.3 * l)
        lam = (jnp.exp(jnp.sum(lq[0] * lq[1])) - jnp.exp(jnp.sum(lq[2] * lq[3])) + lam_init).reshape(1)
        subln = jnp.tile(diff_subln_g[l], HEADS)[None, :]
        ka_all = jnp.concatenate([_heads_to_lanes(cache_diff_k[:, l]).reshape(DEC_BATCH, PAST_LEN, MIX_BLK),
                                  _col(p, C_AK, lat).reshape(DEC_BATCH, DEC_SEQ, MIX_BLK)], axis=1)
        va_all = jnp.concatenate([_heads_to_lanes(cache_diff_v[:, l]).reshape(DEC_BATCH, PAST_LEN, MIX_BLK),
                                  _col(p, C_AV, lat).reshape(DEC_BATCH, DEC_SEQ, MIX_BLK)], axis=1)
        kv_len = PAST_LEN + DEC_SEQ
        diff = functools.partial(attention, lam=lam, norm_g=subln, n_maps=2, post_scale=1.0 - lam_init)
        oa_ctx = diff(p, lambda b, i: (b, C_AQ), p, lambda b, i: (b, C_AK), p, lambda b, i: (b, C_AV),
                      BATCH, SEQ, SEQ)
        oa_lat = diff(p, lambda b, i: (lat_tile0 + b * n_lt + i, C_AQ),
                      ka_all.reshape(DEC_BATCH * kv_len, MIX_BLK), lambda b, i: (b, 0),
                      va_all.reshape(DEC_BATCH * kv_len, MIX_BLK), lambda b, i: (b, 0),
                      DEC_BATCH, DEC_SEQ, kv_len)

        of_ctx, ob_ctx, st_ctx = hgrn(p, 0, BATCH, SEQ, lb_all[l], None)
        of_lat, ob_lat, _ = hgrn(p, lat_tile0, DEC_BATCH, DEC_SEQ, lb_all[l],
                                 _state_to_blockdiag(state_hgrn[:, l]))

        oc_ctx = attention(p, lambda b, i: (b, C_CQ), p, lambda b, i: (b, C_CK), p, lambda b, i: (b, C_CV),
                           BATCH, SEQ, SEQ, lam, subln, n_maps=1, post_scale=1.0)
        oc_lat = na_latent(p, _heads_to_lanes(cache_na_k[:, l]), _heads_to_lanes(cache_na_v[:, l]),
                           _na_bias_tables(na_rpb[l]))

        od_ctx = fourier_mix(p, 0, BATCH, SEQ, dft_ctx)
        od_lat = fourier_mix(p, TP // DEC_SEQ, DEC_BATCH, DEC_SEQ, dft_lat)

        cat = lambda a, b: jnp.concatenate([a, b], axis=0)
        x1, h2, gates_t = out_and_route(x, cat(oa_ctx, oa_lat), cat(of_ctx, of_lat), cat(ob_ctx, ob_lat),
                                        cat(oc_ctx, oc_lat), cat(od_ctx, od_lat), p, mods[l], hgrn_norm_g[l],
                                        w_out[l].astype(BF16), norm2_g[l], router_w, router_b)
        x = moe(h2, gates_t.T, moe_w_gate[l].astype(BF16), moe_w_up[l].astype(BF16),
                moe_w_down[l].astype(BF16), x1, mods[l], final_norm_g, final=(l == DEPTH - 1))

        caches[0].append(_lanes_to_heads(_col(p, C_AK, ctx), BATCH))
        caches[1].append(_lanes_to_heads(_col(p, C_AV, ctx), BATCH))
        caches[2].append(_lanes_to_heads(_col(p, C_CK, ctx), BATCH))
        caches[3].append(_lanes_to_heads(_col(p, C_CV, ctx), BATCH))
        caches[4].append(_blockdiag_to_state(st_ctx))
    y_prompt = x[:TP].reshape(BATCH, SEQ, D_MODEL)
    y_sample = x[TP:].reshape(DEC_BATCH, DEC_SEQ, D_MODEL)
    return (y_prompt, y_sample) + tuple(jnp.stack(cc, axis=1) for cc in caches)
```

```python
import functools
import math

import numpy as np
import jax
import jax.numpy as jnp
from jax import lax
from jax.experimental import pallas as pl
from jax.experimental.pallas import tpu as pltpu

F32 = jnp.float32
BF16 = jnp.bfloat16
HIGHEST = lax.Precision.HIGHEST

D_MODEL = 1024
BATCH = 16
SEQ = 256
DEPTH = 2
DEC_BATCH = 2
DEC_SEQ = 2048
PAST_LEN = 256
GRID_W = 64
GRID_H = DEC_SEQ // GRID_W
EPS = 1e-6
NEG_BIG = -1e30
HEADS = 4
HEAD_DIM = 64
MIX_BLK = HEADS * HEAD_DIM
A_DIM = 32
ROPE_BASE = 10000.0
B_CHUNK = 32
NA_WIN_H = 8
NA_WIN_W = 16
N_EXPERTS = 16
N_GROUPS = 4
D_EXPERT = 512
PROJ_W = 12 * MIX_BLK
TP = BATCH * SEQ
TL = DEC_BATCH * DEC_SEQ
T = TP + TL
TM = 256
N_TILES = T // TM
CTX_TILES = TP // TM
LAT_TILES_PER_SEQ = DEC_SEQ // TM
(C_AQ, C_AK, C_AV, C_BQ, C_BFF, C_BFB, C_BV, C_BG, C_CQ, C_CK, C_CV, C_DU) = range(12)
NA_SLAB_ROWS = 12
NA_SLAB = NA_SLAB_ROWS * GRID_W
VMEM_LIMIT = 56 * 1024 * 1024


def _cparams(sem):
    return pltpu.CompilerParams(dimension_semantics=sem, vmem_limit_bytes=VMEM_LIMIT)


def _nt_dot(a, b, precision=None):
    return lax.dot_general(a, b, (((1,), (1,)), ((), ())), precision=precision,
                           preferred_element_type=F32)


def _head_lanes(width=MIX_BLK):
    return lax.broadcasted_iota(jnp.int32, (1, width), 1)


def _lane_range(lane, lo, n):
    return (lane >= lo) & (lane < lo + n)


def _same_head_matrix():
    r = lax.broadcasted_iota(jnp.int32, (MIX_BLK, MIX_BLK), 0) // HEAD_DIM
    c = lax.broadcasted_iota(jnp.int32, (MIX_BLK, MIX_BLK), 1) // HEAD_DIM
    return r == c


def _head_mean_square(o):
    ones = jnp.where(_same_head_matrix(), 1.0, 0.0).astype(F32)
    return jnp.dot(o * o, ones, precision=HIGHEST, preferred_element_type=F32) * (1.0 / HEAD_DIM)


def _mod_row(i):
    return jnp.where(i < CTX_TILES, 0, 1 + (i - CTX_TILES) // LAT_TILES_PER_SEQ)


def _mod_kernel(c_ref, w_ref, b_ref, o_ref):
    w = w_ref[0]
    for r in range(3):
        c = c_ref[r]
        s = c * jax.nn.sigmoid(c)
        o_ref[0, r:r + 1, :] = jnp.sum(s * w, axis=0, keepdims=True) + b_ref[0]


def modulation(c_rows, ada_w, ada_b):
    nt = 768
    n_out = 6 * D_MODEL
    return pl.pallas_call(
        _mod_kernel,
        grid=(DEPTH, n_out // nt),
        in_specs=[
            pl.BlockSpec((3, D_MODEL, 1), lambda l, j: (0, 0, 0)),
            pl.BlockSpec((1, D_MODEL, nt), lambda l, j: (l, 0, j)),
            pl.BlockSpec((1, 1, nt), lambda l, j: (l, 0, j)),
        ],
        out_specs=pl.BlockSpec((1, 3, nt), lambda l, j: (l, 0, j)),
        out_shape=jax.ShapeDtypeStruct((DEPTH, 3, n_out), F32),
        compiler_params=_cparams(("arbitrary", "arbitrary")),
        name="modulation",
    )(c_rows[:, :, None], ada_w, ada_b[:, None, :])


def _proj_kernel(x_ref, mod_ref, g_ref, w_ref, cos_ref, sa_ref, sb_ref, o_ref):
    x = x_ref[...]
    ms = jnp.mean(x * x, axis=-1, keepdims=True)
    mod = mod_ref[0]
    h = x * lax.rsqrt(ms + EPS) * g_ref[...] * (1.0 + mod[1:2]) + mod[0:1]
    p = jnp.dot(h.astype(BF16), w_ref[...], preferred_element_type=F32)
    t = p[:, :2 * MIX_BLK]
    o_ref[:, :2 * MIX_BLK] = (t * cos_ref[...] + pltpu.roll(t, 1, 1) * sa_ref[...]
                              + pltpu.roll(t, 2 * MIX_BLK - 1, 1) * sb_ref[...])
    o_ref[:, 2 * MIX_BLK:] = p[:, 2 * MIX_BLK:]


def _rope_tables():
    nf = A_DIM // 4
    freqs = ROPE_BASE ** (-jnp.arange(nf, dtype=F32) / nf)
    pos = jnp.arange(DEC_SEQ)
    row = (pos // GRID_W).astype(F32)
    col = (pos % GRID_W).astype(F32)
    ang = jnp.concatenate([row[:, None] * freqs, col[:, None] * freqs], axis=-1)
    cos = jnp.repeat(jnp.cos(ang), 2, axis=-1)
    sin = jnp.repeat(jnp.sin(ang), 2, axis=-1)
    odd = (jnp.arange(A_DIM) % 2 == 1)[None, :]
    sin_from_left = jnp.where(odd, sin, 0.0)
    sin_from_right = jnp.where(odd, 0.0, -sin)
    reps = 2 * MIX_BLK // A_DIM
    ident = (jnp.ones((TM, 2 * MIX_BLK), F32), jnp.zeros((TM, 2 * MIX_BLK), F32))
    return (jnp.concatenate([jnp.tile(cos, (1, reps)), ident[0]], axis=0),
            jnp.concatenate([jnp.tile(sin_from_left, (1, reps)), ident[1]], axis=0),
            jnp.concatenate([jnp.tile(sin_from_right, (1, reps)), ident[1]], axis=0))


def projection(x, mod, norm_g, w_in_bf16, rope):
    def rope_idx(i):
        return (jnp.where(i < CTX_TILES, LAT_TILES_PER_SEQ, (i - CTX_TILES) % LAT_TILES_PER_SEQ), 0)

    rope_spec = pl.BlockSpec((TM, 2 * MIX_BLK), rope_idx)
    return pl.pallas_call(
        _proj_kernel,
        grid=(N_TILES,),
        in_specs=[
            pl.BlockSpec((TM, D_MODEL), lambda i: (i, 0)),
            pl.BlockSpec((1, 6, D_MODEL), lambda i: (_mod_row(i), 0, 0)),
            pl.BlockSpec((1, D_MODEL), lambda i: (0, 0)),
            pl.BlockSpec((D_MODEL, PROJ_W), lambda i: (0, 0)),
            rope_spec, rope_spec, rope_spec,
        ],
        out_specs=pl.BlockSpec((TM, PROJ_W), lambda i: (i, 0)),
        out_shape=jax.ShapeDtypeStruct((T, PROJ_W), F32),
        compiler_params=_cparams(("arbitrary",)),
        name="projection",
    )(x, mod, norm_g[None, :], w_in_bf16, *rope)


def _softmax_rows(s):
    m = jnp.max(s, axis=-1, keepdims=True)
    e = jnp.exp(s - m)
    return e / jnp.sum(e, axis=-1, keepdims=True)


def _attn_kernel(lam_ref, q_ref, k_ref, v_ref, g_ref, o_ref, *, n_maps, post_scale):
    q = q_ref[...]
    kb = k_ref[...].astype(BF16)
    vb = v_ref[...].astype(BF16)
    lane = _head_lanes()
    map_dim = HEAD_DIM // n_maps
    scale = map_dim ** -0.5
    o = jnp.zeros(q.shape, F32)
    for h in range(HEADS):
        probs = []
        for j in range(n_maps):
            qm = jnp.where(_lane_range(lane, h * HEAD_DIM + j * map_dim, map_dim), q, 0.0)
            probs.append(_softmax_rows(_nt_dot(qm.astype(BF16), kb) * scale))
        w = probs[0] - lam_ref[0] * probs[1] if n_maps == 2 else probs[0]
        oh = jnp.dot(w.astype(BF16), vb, preferred_element_type=F32)
        o = jnp.where(_lane_range(lane, h * HEAD_DIM, HEAD_DIM), oh, o)
    if n_maps == 2:
        o = o * lax.rsqrt(_head_mean_square(o) + EPS) * g_ref[...] * post_scale
    o_ref[...] = o


def attention(q_arr, q_idx, k_arr, k_idx, v_arr, v_idx, n_seq, q_len, kv_len, lam, norm_g, *,
              n_maps, post_scale):
    kern = functools.partial(_attn_kernel, n_maps=n_maps, post_scale=post_scale)
    return pl.pallas_call(
        kern,
        grid=(n_seq, q_len // TM),
        in_specs=[
            pl.BlockSpec(memory_space=pltpu.SMEM),
            pl.BlockSpec((TM, MIX_BLK), q_idx),
            pl.BlockSpec((kv_len, MIX_BLK), k_idx),
            pl.BlockSpec((kv_len, MIX_BLK), v_idx),
            pl.BlockSpec((1, MIX_BLK), lambda b, i: (0, 0)),
        ],
        out_specs=pl.BlockSpec((TM, MIX_BLK), lambda b, i: (b * (q_len // TM) + i, 0)),
        out_shape=jax.ShapeDtypeStruct((n_seq * q_len, MIX_BLK), F32),
        compiler_params=_cparams(("arbitrary", "arbitrary")),
        name="attention_%dmap_%d" % (n_maps, kv_len),
    )(lam, q_arr, k_arr, v_arr, norm_g)


def _na_slab_start(i):
    return jnp.clip(i - 1, 0, GRID_H // 4 - NA_SLAB_ROWS // 4)


def _na_kernel(q_ref, k_ref, v_ref, kc_ref, vc_ref, bias_ref, o_ref):
    i = pl.program_id(1)
    start = pl.multiple_of(_na_slab_start(i) * TM, TM)
    ks = k_ref[pl.ds(start, NA_SLAB), :].astype(BF16)
    vs = v_ref[pl.ds(start, NA_SLAB), :].astype(BF16)
    kc = kc_ref[...].astype(BF16)
    vc = vc_ref[...].astype(BF16)
    q = q_ref[...]
    lane = _head_lanes()
    scale = HEAD_DIM ** -0.5
    o = jnp.zeros(q.shape, F32)
    for h in range(HEADS):
        in_head = _lane_range(lane, h * HEAD_DIM, HEAD_DIM)
        qm = jnp.where(in_head, q, 0.0).astype(BF16)
        s_loc = _nt_dot(qm, ks) * scale + bias_ref[0, h]
        s_ctx = _nt_dot(qm, kc) * scale
        m = jnp.maximum(jnp.max(s_loc, axis=-1, keepdims=True), jnp.max(s_ctx, axis=-1, keepdims=True))
        e_loc = jnp.exp(s_loc - m)
        e_ctx = jnp.exp(s_ctx - m)
        den = jnp.sum(e_loc, axis=-1, keepdims=True) + jnp.sum(e_ctx, axis=-1, keepdims=True)
        oh = (jnp.dot(e_loc.astype(BF16), vs, preferred_element_type=F32)
              + jnp.dot(e_ctx.astype(BF16), vc, preferred_element_type=F32)) / den
        o = jnp.where(in_head, oh, o)
    o_ref[...] = o


def _na_bias_tables(rpb):
    n_dr, n_dc = 2 * NA_WIN_H - 1, 2 * NA_WIN_W - 1
    cq = np.arange(GRID_W)[:, None]
    ck = np.arange(GRID_W)[None, :]
    wc0 = np.clip(cq - NA_WIN_W // 2, 0, GRID_W - NA_WIN_W)
    col_ok = (ck >= wc0) & (ck < wc0 + NA_WIN_W)
    col_pick = np.clip(ck - cq + NA_WIN_W - 1, 0, n_dc - 1)[..., None] == np.arange(n_dc)
    row_pick, valid = [], []
    for tile in (0, 1, GRID_H // 4 - 1):
        slab0 = int(np.clip(tile - 1, 0, GRID_H // 4 - NA_SLAB_ROWS // 4)) * 4
        rq = (tile * 4 + np.arange(4))[:, None]
        rk = (slab0 + np.arange(NA_SLAB_ROWS))[None, :]
        wr0 = np.clip(rq - NA_WIN_H // 2, 0, GRID_H - NA_WIN_H)
        row_ok = (rk >= wr0) & (rk < wr0 + NA_WIN_H)
        row_pick.append(np.clip(rk - rq + NA_WIN_H - 1, 0, n_dr - 1)[..., None] == np.arange(n_dr))
        valid.append(row_ok[:, None, :, None] & col_ok[None, :, None, :])
    by_col = jnp.einsum("hab,qcb->haqc", rpb.astype(F32), jnp.asarray(col_pick, F32), precision=HIGHEST)
    table = jnp.einsum("prka,haqc->phrqkc", jnp.asarray(np.stack(row_pick), F32), by_col, precision=HIGHEST)
    table = jnp.where(jnp.asarray(np.stack(valid))[:, None], table, NEG_BIG)
    return table.reshape(3, HEADS, TM, NA_SLAB)


def na_latent(p, kc, vc, bias):
    n_t = LAT_TILES_PER_SEQ
    seq_blk0 = TP // DEC_SEQ

    def bias_idx(b, i):
        return (jnp.minimum(i, 1) + i // (n_t - 1), 0, 0, 0)

    return pl.pallas_call(
        _na_kernel,
        grid=(DEC_BATCH, n_t),
        in_specs=[
            pl.BlockSpec((TM, MIX_BLK), lambda b, i: (CTX_TILES + b * n_t + i, C_CQ)),
            pl.BlockSpec((DEC_SEQ, MIX_BLK), lambda b, i: (seq_blk0 + b, C_CK)),
            pl.BlockSpec((DEC_SEQ, MIX_BLK), lambda b, i: (seq_blk0 + b, C_CV)),
            pl.BlockSpec((PAST_LEN, MIX_BLK), lambda b, i: (b, 0)),
            pl.BlockSpec((PAST_LEN, MIX_BLK), lambda b, i: (b, 0)),
            pl.BlockSpec((1, HEADS, TM, NA_SLAB), bias_idx),
        ],
        out_specs=pl.BlockSpec((TM, MIX_BLK), lambda b, i: (b * n_t + i, 0)),
        out_shape=jax.ShapeDtypeStruct((TL, MIX_BLK), F32),
        compiler_params=_cparams(("arbitrary", "arbitrary")),
        name="na_latent",
    )(p, p, p, kc, vc, bias)


def _chunk_matrices():
    r = lax.broadcasted_iota(jnp.int32, (TM, TM), 0)
    c = lax.broadcasted_iota(jnp.int32, (TM, TM), 1)
    same = (r // B_CHUNK) == (c // B_CHUNK)
    return same & (c <= r), same & (c >= r)


def _hgrn_direction(q_ref, f_ref, v_ref, lb, st_ref, o_ref, reverse):
    n_ch = TM // B_CHUNK
    lower, upper = _chunk_matrices()
    tri = upper if reverse else lower
    zq = q_ref[...]
    q = zq * jax.nn.sigmoid(zq)
    z = f_ref[...]
    logf = jnp.log(lb + (1.0 - lb) * jax.nn.sigmoid(z))
    kk = (1.0 - lb) * jax.nn.sigmoid(-z)
    v = v_ref[...]
    b = jnp.dot(jnp.where(tri, 1.0, 0.0).astype(F32), logf, precision=HIGHEST,
                preferred_element_type=F32)
    b3 = b.reshape(n_ch, B_CHUNK, MIX_BLK)
    mid = B_CHUNK // 2 if reverse else B_CHUNK // 2 - 1
    end = 0 if reverse else B_CHUNK - 1
    b_mid = b3[:, mid:mid + 1, :]
    b_end = b3[:, end:end + 1, :]
    q3 = q.reshape(n_ch, B_CHUNK, MIX_BLK)
    k3 = kk.reshape(n_ch, B_CHUNK, MIX_BLK)
    q_in = (q3 * jnp.exp(b3 - b_mid)).reshape(TM, MIX_BLK)
    k_in = (k3 * jnp.exp(b_mid - b3)).reshape(TM, MIX_BLK).astype(BF16)
    q_dec = (q3 * jnp.exp(b3)).reshape(TM, MIX_BLK).astype(BF16)
    k_dec = (k3 * jnp.exp(b_end - b3)).reshape(TM, MIX_BLK)
    g = jnp.exp(b_end)
    vb = v.astype(BF16)
    vt = v.T.astype(BF16)
    lane = _head_lanes()
    o = jnp.zeros((TM, MIX_BLK), F32)
    for h in range(HEADS):
        in_head = _lane_range(lane, h * HEAD_DIM, HEAD_DIM)
        a = _nt_dot(jnp.where(in_head, q_in, 0.0).astype(BF16), k_in)
        a = jnp.where(tri, a, 0.0).astype(BF16)
        o = jnp.where(in_head, jnp.dot(a, vb, preferred_element_type=F32), o)
    same_head = _same_head_matrix()
    row_chunk = lax.broadcasted_iota(jnp.int32, (TM, 1), 0) // B_CHUNK
    st = st_ref[...]
    order = range(n_ch - 1, -1, -1) if reverse else range(n_ch)
    for c in order:
        rows = slice(c * B_CHUNK, (c + 1) * B_CHUNK)
        o_ref[rows, :] = o[rows, :] + _nt_dot(q_dec[rows, :], st.astype(BF16))
        kv = jnp.dot(vt, jnp.where(row_chunk == c, k_dec, 0.0).astype(BF16),
                     preferred_element_type=F32)
        st = st * g[c] + jnp.where(same_head, kv, 0.0)
    st_ref[...] = st


def _hgrn_kernel(qf_ref, ff_ref, vf_ref, qb_ref, fb_ref, vb_ref, lb_ref, s0_ref,
                 of_ref, ob_ref, s_ref, stf_ref, stb_ref, *, has_s0):
    j = pl.program_id(1)

    @pl.when(j == 0)
    def _():
        if has_s0:
            stf_ref[...] = s0_ref[0, 0]
            stb_ref[...] = s0_ref[0, 1]
        else:
            stf_ref[...] = jnp.zeros((MIX_BLK, MIX_BLK), F32)
            stb_ref[...] = jnp.zeros((MIX_BLK, MIX_BLK), F32)

    lb = lb_ref[...]
    _hgrn_direction(qf_ref, ff_ref, vf_ref, lb[0:1], stf_ref, of_ref, False)
    _hgrn_direction(qb_ref, fb_ref, vb_ref, lb[1:2], stb_ref, ob_ref, True)

    @pl.when(j == pl.num_programs(1) - 1)
    def _():
        s_ref[0, 0] = stf_ref[...]
        s_ref[0, 1] = stb_ref[...]


def hgrn(p, row_tile0, n_seq, seq_len, lb, s0):
    nb = seq_len // TM
    has_s0 = s0 is not None
    if s0 is None:
        s0 = jnp.zeros((1, 2, MIX_BLK, MIX_BLK), F32)

    def fwd(col):
        return pl.BlockSpec((TM, MIX_BLK), lambda s, j: (row_tile0 + s * nb + j, col))

    def bwd(col):
        return pl.BlockSpec((TM, MIX_BLK), lambda s, j: (row_tile0 + s * nb + nb - 1 - j, col))

    state_spec = pl.BlockSpec((1, 2, MIX_BLK, MIX_BLK), lambda s, j: (s if has_s0 else 0, 0, 0, 0))
    out_rows = n_seq * seq_len
    return pl.pallas_call(
        functools.partial(_hgrn_kernel, has_s0=has_s0),
        grid=(n_seq, nb),
        in_specs=[fwd(C_BQ), fwd(C_BFF), fwd(C_BV), bwd(C_BQ), bwd(C_BFB), bwd(C_BV),
                  pl.BlockSpec((2, MIX_BLK), lambda s, j: (0, 0)), state_spec],
        out_specs=[
            pl.BlockSpec((TM, MIX_BLK), lambda s, j: (s * nb + j, 0)),
            pl.BlockSpec((TM, MIX_BLK), lambda s, j: (s * nb + nb - 1 - j, 0)),
            pl.BlockSpec((1, 2, MIX_BLK, MIX_BLK), lambda s, j: (s, 0, 0, 0)),
        ],
        out_shape=[jax.ShapeDtypeStruct((out_rows, MIX_BLK), F32),
                   jax.ShapeDtypeStruct((out_rows, MIX_BLK), F32),
                   jax.ShapeDtypeStruct((n_seq, 2, MIX_BLK, MIX_BLK), F32)],
        scratch_shapes=[pltpu.VMEM((MIX_BLK, MIX_BLK), F32), pltpu.VMEM((MIX_BLK, MIX_BLK), F32)],
        compiler_params=_cparams(("arbitrary", "arbitrary")),
        name="hgrn_%d" % seq_len,
    )(p, p, p, p, p, p, lb, s0)


def _state_to_blockdiag(s):
    st = jnp.swapaxes(s.astype(F32), -1, -2)
    eye = jnp.eye(HEADS, dtype=F32)
    full = st[:, :, :, :, None, :] * eye[None, None, :, None, :, None]
    return full.reshape(s.shape[0], 2, MIX_BLK, MIX_BLK)


def _blockdiag_to_state(st):
    n = st.shape[0]
    r = st.reshape(n, 2, HEADS, HEAD_DIM, HEADS, HEAD_DIM)
    diag = jnp.stack([r[:, :, h, :, h, :] for h in range(HEADS)], axis=2)
    return jnp.swapaxes(diag, -1, -2)


def _fft_kernel(u_ref, c64_ref, s64_ref, cl_ref, sl_ref, o_ref, a_ref, b_ref, *, norm):
    @pl.when(pl.program_id(1) == 0)
    def _():
        u = u_ref[...].astype(BF16)
        a_ref[...] = jnp.dot(u, c64_ref[...], preferred_element_type=F32).astype(BF16)
        b_ref[...] = jnp.dot(u, s64_ref[...], preferred_element_type=F32).astype(BF16)

    o_ref[...] = (jnp.dot(cl_ref[...], a_ref[...], preferred_element_type=F32)
                  - jnp.dot(sl_ref[...], b_ref[...], preferred_element_type=F32)) * norm


def _dft_tables(n):
    k = np.arange(n)
    ang = 2.0 * np.pi * ((k[:, None] * k[None, :]) % n) / n
    return np.cos(ang), np.sin(ang)


def _dft_constants(seq_len):
    c64, s64 = _dft_tables(HEAD_DIM)
    eye = np.eye(HEADS)
    cl, sl = _dft_tables(seq_len)
    as_bf16 = lambda a: jnp.asarray(a, F32).astype(BF16)
    return as_bf16(np.kron(eye, c64)), as_bf16(np.kron(eye, s64)), as_bf16(cl), as_bf16(sl)


def fourier_mix(p, row_blk0, n_seq, seq_len, consts):
    c64, s64, cl, sl = consts
    nb = seq_len // TM
    norm = 1.0 / math.sqrt(seq_len * HEAD_DIM)
    return pl.pallas_call(
        functools.partial(_fft_kernel, norm=norm),
        grid=(n_seq, nb),
        in_specs=[
            pl.BlockSpec((seq_len, MIX_BLK), lambda s, i: (row_blk0 + s, C_DU)),
            pl.BlockSpec((MIX_BLK, MIX_BLK), lambda s, i: (0, 0)),
            pl.BlockSpec((MIX_BLK, MIX_BLK), lambda s, i: (0, 0)),
            pl.BlockSpec((TM, seq_len), lambda s, i: (i, 0)),
            pl.BlockSpec((TM, seq_len), lambda s, i: (i, 0)),
        ],
        out_specs=pl.BlockSpec((TM, MIX_BLK), lambda s, i: (s * nb + i, 0)),
        out_shape=jax.ShapeDtypeStruct((n_seq * seq_len, MIX_BLK), F32),
        scratch_shapes=[pltpu.VMEM((seq_len, MIX_BLK), BF16), pltpu.VMEM((seq_len, MIX_BLK), BF16)],
        compiler_params=_cparams(("arbitrary", "arbitrary")),
        name="fourier_%d" % seq_len,
    )(p, c64, s64, cl, sl)


def _route(logits_t, rb):
    per = N_EXPERTS // N_GROUPS
    score = [jax.nn.sigmoid(logits_t[e:e + 1, :]) for e in range(N_EXPERTS)]
    sel = [score[e] + rb[e:e + 1, :] for e in range(N_EXPERTS)]
    gscore = []
    for g in range(N_GROUPS):
        vals = sel[g * per:(g + 1) * per]
        best = None
        for a in range(per):
            for b in range(a + 1, per):
                pair = vals[a] + vals[b]
                best = pair if best is None else jnp.maximum(best, pair)
        gscore.append(best)
    chosen = []
    for g in range(N_GROUPS):
        ok = None
        for j in range(N_GROUPS):
            if j == g:
                continue
            cond = gscore[g] > gscore[j] if j < g else gscore[g] >= gscore[j]
            ok = cond if ok is None else ok & cond
        chosen.append(ok)
    picked = []
    for e in range(N_EXPERTS):
        g = e // per
        rank = jnp.zeros_like(sel[e])
        for j in range(g * per, (g + 1) * per):
            if j == e:
                continue
            ahead = sel[j] >= sel[e] if j < e else sel[j] > sel[e]
            rank = rank + jnp.where(ahead, 1.0, 0.0)
        picked.append(chosen[g] & (rank < 2.0))
    wsum = jnp.zeros_like(score[0])
    for e in range(N_EXPERTS):
        wsum = wsum + jnp.where(picked[e], score[e], 0.0)
    return jnp.concatenate([jnp.where(picked[e], score[e] / wsum, 0.0) for e in range(N_EXPERTS)], axis=0)


def _out_kernel(x_ref, oa_ref, of_ref, ob_ref, oc_ref, od_ref, bg_ref, mod_ref, hg_ref, w_ref,
                g2_ref, rw_ref, rb_ref, x1_ref, h2_ref, gates_ref):
    mod = mod_ref[0]
    hb = of_ref[...] + ob_ref[...]
    zg = bg_ref[...]
    hb = hb * lax.rsqrt(_head_mean_square(hb) + EPS) * hg_ref[...] * (zg * jax.nn.sigmoid(zg))
    parts = (oa_ref[...], hb, oc_ref[...], od_ref[...])
    mixed = jnp.zeros((TM, D_MODEL), F32)
    for n, part in enumerate(parts):
        mixed = mixed + jnp.dot(part.astype(BF16), w_ref[n * MIX_BLK:(n + 1) * MIX_BLK, :],
                                preferred_element_type=F32)
    x1 = x_ref[...] + mod[2:3] * mixed
    x1_ref[...] = x1
    ms = jnp.mean(x1 * x1, axis=-1, keepdims=True)
    h2 = x1 * lax.rsqrt(ms + EPS) * g2_ref[...] * (1.0 + mod[4:5]) + mod[3:4]
    h2_ref[...] = h2.astype(BF16)
    gates_ref[...] = _route(_nt_dot(rw_ref[...], h2, precision=HIGHEST), rb_ref[...])


def out_and_route(x, o_a, o_f, o_b, o_c, o_d, p, mod, hgrn_g, w_out_bf16, norm2_g, router_w, router_b):
    tile = lambda w: pl.BlockSpec((TM, w), lambda i: (i, 0))
    full = lambda r, c: pl.BlockSpec((r, c), lambda i: (0, 0))
    return pl.pallas_call(
        _out_kernel,
        grid=(N_TILES,),
        in_specs=[
            tile(D_MODEL), tile(MIX_BLK), tile(MIX_BLK), tile(MIX_BLK), tile(MIX_BLK), tile(MIX_BLK),
            pl.BlockSpec((TM, MIX_BLK), lambda i: (i, C_BG)),
            pl.BlockSpec((1, 6, D_MODEL), lambda i: (_mod_row(i), 0, 0)),
            full(1, MIX_BLK), full(D_MODEL, D_MODEL), full(1, D_MODEL),
            full(N_EXPERTS, D_MODEL), full(N_EXPERTS, 1),
        ],
        out_specs=[tile(D_MODEL), tile(D_MODEL), pl.BlockSpec((N_EXPERTS, TM), lambda i: (0, i))],
        out_shape=[jax.ShapeDtypeStruct((T, D_MODEL), F32),
                   jax.ShapeDtypeStruct((T, D_MODEL), BF16),
                   jax.ShapeDtypeStruct((N_EXPERTS, T), F32)],
        compiler_params=_cparams(("arbitrary",)),
        name="out_and_route",
    )(x, o_a, o_f, o_b, o_c, o_d, p, mod, jnp.tile(hgrn_g, HEADS)[None, :], w_out_bf16,
      norm2_g[None, :], router_w.T, router_b[:, None])


def _moe_kernel(h_ref, gates_ref, wg_ref, wu_ref, wd_ref, x1_ref, mod_ref, fg_ref, o_ref, acc_ref,
                *, final):
    e = pl.program_id(1)

    @pl.when(e == 0)
    def _():
        acc_ref[...] = jnp.zeros(acc_ref.shape, F32)

    h = h_ref[...]
    a = jnp.dot(h, wg_ref[0], preferred_element_type=F32)
    u = jnp.dot(h, wu_ref[0], preferred_element_type=F32)
    lane = lax.broadcasted_iota(jnp.int32, (1, N_EXPERTS), 1)
    gate = jnp.sum(jnp.where(lane == e, gates_ref[...], 0.0), axis=-1, keepdims=True)
    zz = a * jax.nn.sigmoid(a) * u * gate
    acc_ref[...] += jnp.dot(zz.astype(BF16), wd_ref[0], preferred_element_type=F32)

    @pl.when(e == N_EXPERTS - 1)
    def _():
        x2 = x1_ref[...] + mod_ref[0][5:6] * acc_ref[...]
        if final:
            ms = jnp.mean(x2 * x2, axis=-1, keepdims=True)
            x2 = x2 * lax.rsqrt(ms + EPS) * fg_ref[...]
        o_ref[...] = x2


def moe(h2, gates, wg, wu, wd, x1, mod, final_g, final):
    return pl.pallas_call(
        functools.partial(_moe_kernel, final=final),
        grid=(N_TILES, N_EXPERTS),
        in_specs=[
            pl.BlockSpec((TM, D_MODEL), lambda i, e: (i, 0)),
            pl.BlockSpec((TM, N_EXPERTS), lambda i, e: (i, 0)),
            pl.BlockSpec((1, D_MODEL, D_EXPERT), lambda i, e: (e, 0, 0)),
            pl.BlockSpec((1, D_MODEL, D_EXPERT), lambda i, e: (e, 0, 0)),
            pl.BlockSpec((1, D_EXPERT, D_MODEL), lambda i, e: (e, 0, 0)),
            pl.BlockSpec((TM, D_MODEL), lambda i, e: (i, 0)),
            pl.BlockSpec((1, 6, D_MODEL), lambda i, e: (_mod_row(i), 0, 0)),
            pl.BlockSpec((1, D_MODEL), lambda i, e: (0, 0)),
        ],
        out_specs=pl.BlockSpec((TM, D_MODEL), lambda i, e: (i, 0)),
        out_shape=jax.ShapeDtypeStruct((T, D_MODEL), F32),
        scratch_shapes=[pltpu.VMEM((TM, D_MODEL), F32)],
        compiler_params=_cparams(("arbitrary", "arbitrary")),
        name="moe_final" if final else "moe",
    )(h2, gates, wg, wu, wd, x1, mod, final_g[None, :])


def _heads_to_lanes(t):
    b, h, l, d = t.shape
    return t.transpose(0, 2, 1, 3).reshape(b * l, h * d)


def _lanes_to_heads(t, b):
    return t.reshape(b, -1, HEADS, HEAD_DIM).transpose(0, 2, 1, 3)


def _col(p, c, rows=slice(None)):
    return p[rows, c * MIX_BLK:(c + 1) * MIX_BLK]


def kernel(x_prompt, x_sample, cache_diff_k, cache_diff_v, cache_na_k, cache_na_v, state_hgrn, c, c_ctx,
           norm1_g, norm2_g, ada_w, ada_b, w_in, w_out, diff_lambda, diff_subln_g, hgrn_lb_logits,
           hgrn_norm_g, na_rpb, router_w, router_b, moe_w_gate, moe_w_up, moe_w_down, final_norm_g):
    x = jnp.concatenate([x_prompt.reshape(TP, D_MODEL), x_sample.reshape(TL, D_MODEL)], axis=0)
    mods = modulation(jnp.concatenate([c_ctx[None, :], c], axis=0), ada_w, ada_b)
    mods = mods.reshape(DEPTH, 3, 6, D_MODEL)
    lb_sm = jax.nn.softmax(hgrn_lb_logits.astype(F32), axis=0)
    lb_all = jnp.cumsum(lb_sm, axis=0) - lb_sm[0:1]
    rope = _rope_tables()
    dft_ctx = _dft_constants(SEQ)
    dft_lat = _dft_constants(DEC_SEQ)
    lat_tile0 = CTX_TILES
    n_lt = LAT_TILES_PER_SEQ
    lat = slice(TP, T)
    ctx = slice(0, TP)
    caches = [[], [], [], [], []]
    for l in range(DEPTH):
        p = projection(x, mods[l], norm1_g[l], w_in[l].astype(BF16), rope)

        lq = diff_lambda[l].astype(F32)
        lam_init = 0.8 - 0.6 * math.exp(-0.3 * l)
        lam = (jnp.exp(jnp.sum(lq[0] * lq[1])) - jnp.exp(jnp.sum(lq[2] * lq[3])) + lam_init).reshape(1)
        subln = jnp.tile(diff_subln_g[l], HEADS)[None, :]
        ka_all = jnp.concatenate([_heads_to_lanes(cache_diff_k[:, l]).reshape(DEC_BATCH, PAST_LEN, MIX_BLK),
                                  _col(p, C_AK, lat).reshape(DEC_BATCH, DEC_SEQ, MIX_BLK)], axis=1)
        va_all = jnp.concatenate([_heads_to_lanes(cache_diff_v[:, l]).reshape(DEC_BATCH, PAST_LEN, MIX_BLK),
                                  _col(p, C_AV, lat).reshape(DEC_BATCH, DEC_SEQ, MIX_BLK)], axis=1)
        kv_len = PAST_LEN + DEC_SEQ
        diff = functools.partial(attention, lam=lam, norm_g=subln, n_maps=2, post_scale=1.0 - lam_init)
        oa_ctx = diff(p, lambda b, i: (b, C_AQ), p, lambda b, i: (b, C_AK), p, lambda b, i: (b, C_AV),
                      BATCH, SEQ, SEQ)
        oa_lat = diff(p, lambda b, i: (lat_tile0 + b * n_lt + i, C_AQ),
                      ka_all.reshape(DEC_BATCH * kv_len, MIX_BLK), lambda b, i: (b, 0),
                      va_all.reshape(DEC_BATCH * kv_len, MIX_BLK), lambda b, i: (b, 0),
                      DEC_BATCH, DEC_SEQ, kv_len)

        of_ctx, ob_ctx, st_ctx = hgrn(p, 0, BATCH, SEQ, lb_all[l], None)
        of_lat, ob_lat, _ = hgrn(p, lat_tile0, DEC_BATCH, DEC_SEQ, lb_all[l],
                                 _state_to_blockdiag(state_hgrn[:, l]))

        oc_ctx = attention(p, lambda b, i: (b, C_CQ), p, lambda b, i: (b, C_CK), p, lambda b, i: (b, C_CV),
                           BATCH, SEQ, SEQ, lam, subln, n_maps=1, post_scale=1.0)
        oc_lat = na_latent(p, _heads_to_lanes(cache_na_k[:, l]), _heads_to_lanes(cache_na_v[:, l]),
                           _na_bias_tables(na_rpb[l]))

        od_ctx = fourier_mix(p, 0, BATCH, SEQ, dft_ctx)
        od_lat = fourier_mix(p, TP // DEC_SEQ, DEC_BATCH, DEC_SEQ, dft_lat)

        cat = lambda a, b: jnp.concatenate([a, b], axis=0)
        x1, h2, gates_t = out_and_route(x, cat(oa_ctx, oa_lat), cat(of_ctx, of_lat), cat(ob_ctx, ob_lat),
                                        cat(oc_ctx, oc_lat), cat(od_ctx, od_lat), p, mods[l], hgrn_norm_g[l],
                                        w_out[l].astype(BF16), norm2_g[l], router_w, router_b)
        x = moe(h2, gates_t.T, moe_w_gate[l].astype(BF16), moe_w_up[l].astype(BF16),
                moe_w_down[l].astype(BF16), x1, mods[l], final_norm_g, final=(l == DEPTH - 1))

        caches[0].append(_lanes_to_heads(_col(p, C_AK, ctx), BATCH))
        caches[1].append(_lanes_to_heads(_col(p, C_AV, ctx), BATCH))
        caches[2].append(_lanes_to_heads(_col(p, C_CK, ctx), BATCH))
        caches[3].append(_lanes_to_heads(_col(p, C_CV, ctx), BATCH))
        caches[4].append(_blockdiag_to_state(st_ctx))
    y_prompt = x[:TP].reshape(BATCH, SEQ, D_MODEL)
    y_sample = x[TP:].reshape(DEC_BATCH, DEC_SEQ, D_MODEL)
    return (y_prompt, y_sample) + tuple(jnp.stack(cc, axis=1) for cc in caches)
```

```python
import functools
import math

import numpy as np
import jax
import jax.numpy as jnp
from jax import lax
from jax.experimental import pallas as pl
from jax.experimental.pallas import tpu as pltpu

F32 = jnp.float32
BF16 = jnp.bfloat16
HIGHEST = lax.Precision.HIGHEST

D_MODEL = 1024
BATCH = 16
SEQ = 256
DEPTH = 2
DEC_BATCH = 2
DEC_SEQ = 2048
PAST_LEN = 256
GRID_W = 64
GRID_H = DEC_SEQ // GRID_W
EPS = 1e-6
NEG_BIG = -1e30
HEADS = 4
HEAD_DIM = 64
MIX_BLK = HEADS * HEAD_DIM
A_DIM = 32
ROPE_BASE = 10000.0
B_CHUNK = 32
NA_WIN_H = 8
NA_WIN_W = 16
N_EXPERTS = 16
N_GROUPS = 4
D_EXPERT = 512
PROJ_W = 12 * MIX_BLK
TP = BATCH * SEQ
TL = DEC_BATCH * DEC_SEQ
T = TP + TL
TM = 256
N_TILES = T // TM
CTX_TILES = TP // TM
LAT_TILES_PER_SEQ = DEC_SEQ // TM
(C_AQ, C_AK, C_AV, C_BQ, C_BFF, C_BFB, C_BV, C_BG, C_CQ, C_CK, C_CV, C_DU) = range(12)
NA_SLAB_ROWS = 12
NA_SLAB = NA_SLAB_ROWS * GRID_W
LANES = 128
ROW_W = D_MODEL + LANES
EXPERT_PAIRS = ((0, 1), (0, 2), (0, 3), (1, 3), (2, 3), (2, 1))
N_BUCKETS = N_GROUPS * len(EXPERT_PAIRS)
BUCKET_ROWS = 32
MAX_TILES = T // TM + N_BUCKETS
N_SLOTS = MAX_TILES * TM
VMEM_LIMIT = 56 * 1024 * 1024


def _cparams(sem):
    return pltpu.CompilerParams(dimension_semantics=sem, vmem_limit_bytes=VMEM_LIMIT)


def _nt_dot(a, b, precision=None):
    return lax.dot_general(a, b, (((1,), (1,)), ((), ())), precision=precision,
                           preferred_element_type=F32)


def _head_lanes(width=MIX_BLK):
    return lax.broadcasted_iota(jnp.int32, (1, width), 1)


def _lane_range(lane, lo, n):
    return (lane >= lo) & (lane < lo + n)


def _same_head_matrix():
    r = lax.broadcasted_iota(jnp.int32, (MIX_BLK, MIX_BLK), 0) // HEAD_DIM
    c = lax.broadcasted_iota(jnp.int32, (MIX_BLK, MIX_BLK), 1) // HEAD_DIM
    return r == c


def _head_mean_square(o):
    ones = jnp.where(_same_head_matrix(), 1.0, 0.0).astype(F32)
    return jnp.dot(o * o, ones, precision=HIGHEST, preferred_element_type=F32) * (1.0 / HEAD_DIM)


def _mod_row(i):
    return jnp.where(i < CTX_TILES, 0, 1 + (i - CTX_TILES) // LAT_TILES_PER_SEQ)


def _mod_kernel(c_ref, w_ref, b_ref, o_ref):
    w = w_ref[0]
    for r in range(3):
        c = c_ref[r]
        s = c * jax.nn.sigmoid(c)
        o_ref[0, r:r + 1, :] = jnp.sum(s * w, axis=0, keepdims=True) + b_ref[0]


def modulation(c_rows, ada_w, ada_b):
    nt = 768
    n_out = 6 * D_MODEL
    return pl.pallas_call(
        _mod_kernel,
        grid=(DEPTH, n_out // nt),
        in_specs=[
            pl.BlockSpec((3, D_MODEL, 1), lambda l, j: (0, 0, 0)),
            pl.BlockSpec((1, D_MODEL, nt), lambda l, j: (l, 0, j)),
            pl.BlockSpec((1, 1, nt), lambda l, j: (l, 0, j)),
        ],
        out_specs=pl.BlockSpec((1, 3, nt), lambda l, j: (l, 0, j)),
        out_shape=jax.ShapeDtypeStruct((DEPTH, 3, n_out), F32),
        compiler_params=_cparams(("arbitrary", "arbitrary")),
        name="modulation",
    )(c_rows[:, :, None], ada_w, ada_b[:, None, :])


def _proj_kernel(x_ref, mod_ref, g_ref, w_ref, cos_ref, sa_ref, sb_ref, o_ref):
    _proj_body(x_ref[...], mod_ref, g_ref, w_ref, cos_ref, sa_ref, sb_ref, o_ref)


def _proj_after_moe_kernel(dest_ref, y_ref, x1_ref, modp_ref, mod_ref, g_ref, w_ref, cos_ref, sa_ref, sb_ref,
                           x2_ref, o_ref, buf_ref, sem):
    x2 = _moe_residual(dest_ref, y_ref, buf_ref, sem, x1_ref, modp_ref)
    x2_ref[...] = x2
    _proj_body(x2, mod_ref, g_ref, w_ref, cos_ref, sa_ref, sb_ref, o_ref)


def _proj_body(x, mod_ref, g_ref, w_ref, cos_ref, sa_ref, sb_ref, o_ref):
    ms = jnp.mean(x * x, axis=-1, keepdims=True)
    mod = mod_ref[0]
    h = x * lax.rsqrt(ms + EPS) * g_ref[...] * (1.0 + mod[1:2]) + mod[0:1]
    p = jnp.dot(h.astype(BF16), w_ref[...], preferred_element_type=F32)
    t = p[:, :2 * MIX_BLK]
    o_ref[:, :2 * MIX_BLK] = (t * cos_ref[...] + pltpu.roll(t, 1, 1) * sa_ref[...]
                              + pltpu.roll(t, 2 * MIX_BLK - 1, 1) * sb_ref[...])
    o_ref[:, 2 * MIX_BLK:] = p[:, 2 * MIX_BLK:]


def _rope_tables():
    nf = A_DIM // 4
    freqs = ROPE_BASE ** (-jnp.arange(nf, dtype=F32) / nf)
    pos = jnp.arange(DEC_SEQ)
    row = (pos // GRID_W).astype(F32)
    col = (pos % GRID_W).astype(F32)
    ang = jnp.concatenate([row[:, None] * freqs, col[:, None] * freqs], axis=-1)
    cos = jnp.repeat(jnp.cos(ang), 2, axis=-1)
    sin = jnp.repeat(jnp.sin(ang), 2, axis=-1)
    odd = (jnp.arange(A_DIM) % 2 == 1)[None, :]
    sin_from_left = jnp.where(odd, sin, 0.0)
    sin_from_right = jnp.where(odd, 0.0, -sin)
    reps = 2 * MIX_BLK // A_DIM
    ident = (jnp.ones((TM, 2 * MIX_BLK), F32), jnp.zeros((TM, 2 * MIX_BLK), F32))
    return (jnp.concatenate([jnp.tile(cos, (1, reps)), ident[0]], axis=0),
            jnp.concatenate([jnp.tile(sin_from_left, (1, reps)), ident[1]], axis=0),
            jnp.concatenate([jnp.tile(sin_from_right, (1, reps)), ident[1]], axis=0))


def _rope_block(i):
    return (jnp.where(i < CTX_TILES, LAT_TILES_PER_SEQ, (i - CTX_TILES) % LAT_TILES_PER_SEQ), 0)


def projection_after_moe(dest, y_slots, x1, mod_prev, mod, norm_g, w_in_bf16, rope):
    rope_spec = pl.BlockSpec((TM, 2 * MIX_BLK), lambda i, d: _rope_block(i))
    mod_spec = pl.BlockSpec((1, 6, D_MODEL), lambda i, d: (_mod_row(i), 0, 0))
    return pl.pallas_call(
        _proj_after_moe_kernel,
        grid_spec=pltpu.PrefetchScalarGridSpec(
            num_scalar_prefetch=1,
            grid=(N_TILES,),
            in_specs=[pl.BlockSpec(memory_space=pl.ANY),
                      pl.BlockSpec((TM, D_MODEL), lambda i, d: (i, 0)),
                      mod_spec, mod_spec,
                      pl.BlockSpec((1, D_MODEL), lambda i, d: (0, 0)),
                      pl.BlockSpec((D_MODEL, PROJ_W), lambda i, d: (0, 0)),
                      rope_spec, rope_spec, rope_spec],
            out_specs=[pl.BlockSpec((TM, D_MODEL), lambda i, d: (i, 0)),
                       pl.BlockSpec((TM, PROJ_W), lambda i, d: (i, 0))],
            scratch_shapes=_GATHER_SCRATCH,
        ),
        out_shape=[jax.ShapeDtypeStruct((T, D_MODEL), F32), jax.ShapeDtypeStruct((T, PROJ_W), F32)],
        compiler_params=_cparams(("arbitrary",)),
        name="projection_after_moe",
    )(dest, y_slots, x1, mod_prev, mod, norm_g[None, :], w_in_bf16, *rope)


def projection(x, mod, norm_g, w_in_bf16, rope):
    rope_spec = pl.BlockSpec((TM, 2 * MIX_BLK), _rope_block)
    return pl.pallas_call(
        _proj_kernel,
        grid=(N_TILES,),
        in_specs=[
            pl.BlockSpec((TM, D_MODEL), lambda i: (i, 0)),
            pl.BlockSpec((1, 6, D_MODEL), lambda i: (_mod_row(i), 0, 0)),
            pl.BlockSpec((1, D_MODEL), lambda i: (0, 0)),
            pl.BlockSpec((D_MODEL, PROJ_W), lambda i: (0, 0)),
            rope_spec, rope_spec, rope_spec,
        ],
        out_specs=pl.BlockSpec((TM, PROJ_W), lambda i: (i, 0)),
        out_shape=jax.ShapeDtypeStruct((T, PROJ_W), F32),
        compiler_params=_cparams(("arbitrary",)),
        name="projection",
    )(x, mod, norm_g[None, :], w_in_bf16, *rope)


def _softmax_rows(s):
    m = jnp.max(s, axis=-1, keepdims=True)
    e = jnp.exp(s - m)
    return e / jnp.sum(e, axis=-1, keepdims=True)


def _attn_kernel(lam_ref, q_ref, k_ref, v_ref, g_ref, o_ref, *, n_maps, post_scale):
    q = q_ref[...]
    kb = k_ref[...].astype(BF16)
    vb = v_ref[...].astype(BF16)
    lane = _head_lanes()
    map_dim = HEAD_DIM // n_maps
    scale = map_dim ** -0.5
    o = jnp.zeros(q.shape, F32)
    for h in range(HEADS):
        probs = []
        for j in range(n_maps):
            qm = jnp.where(_lane_range(lane, h * HEAD_DIM + j * map_dim, map_dim), q, 0.0)
            probs.append(_softmax_rows(_nt_dot(qm.astype(BF16), kb) * scale))
        w = probs[0] - lam_ref[0] * probs[1] if n_maps == 2 else probs[0]
        oh = jnp.dot(w.astype(BF16), vb, preferred_element_type=F32)
        o = jnp.where(_lane_range(lane, h * HEAD_DIM, HEAD_DIM), oh, o)
    if n_maps == 2:
        o = o * lax.rsqrt(_head_mean_square(o) + EPS) * g_ref[...] * post_scale
    o_ref[...] = o


def attention(q_arr, q_idx, k_arr, k_idx, v_arr, v_idx, n_seq, q_len, kv_len, lam, norm_g, *,
              n_maps, post_scale):
    kern = functools.partial(_attn_kernel, n_maps=n_maps, post_scale=post_scale)
    return pl.pallas_call(
        kern,
        grid=(n_seq, q_len // TM),
        in_specs=[
            pl.BlockSpec(memory_space=pltpu.SMEM),
            pl.BlockSpec((TM, MIX_BLK), q_idx),
            pl.BlockSpec((kv_len, MIX_BLK), k_idx),
            pl.BlockSpec((kv_len, MIX_BLK), v_idx),
            pl.BlockSpec((1, MIX_BLK), lambda b, i: (0, 0)),
        ],
        out_specs=pl.BlockSpec((TM, MIX_BLK), lambda b, i: (b * (q_len // TM) + i, 0)),
        out_shape=jax.ShapeDtypeStruct((n_seq * q_len, MIX_BLK), F32),
        compiler_params=_cparams(("arbitrary", "arbitrary")),
        name="attention_%dmap_%d" % (n_maps, kv_len),
    )(lam, q_arr, k_arr, v_arr, norm_g)


def _na_slab_start(i):
    return jnp.clip(i - 1, 0, GRID_H // 4 - NA_SLAB_ROWS // 4)


def _na_kernel(q_ref, k_ref, v_ref, kc_ref, vc_ref, bias_ref, o_ref):
    i = pl.program_id(1)
    start = pl.multiple_of(_na_slab_start(i) * TM, TM)
    ks = k_ref[pl.ds(start, NA_SLAB), :].astype(BF16)
    vs = v_ref[pl.ds(start, NA_SLAB), :].astype(BF16)
    kc = kc_ref[...].astype(BF16)
    vc = vc_ref[...].astype(BF16)
    q = q_ref[...]
    lane = _head_lanes()
    scale = HEAD_DIM ** -0.5
    o = jnp.zeros(q.shape, F32)
    for h in range(HEADS):
        in_head = _lane_range(lane, h * HEAD_DIM, HEAD_DIM)
        qm = jnp.where(in_head, q, 0.0).astype(BF16)
        s_loc = _nt_dot(qm, ks) * scale + bias_ref[0, h]
        s_ctx = _nt_dot(qm, kc) * scale
        m = jnp.maximum(jnp.max(s_loc, axis=-1, keepdims=True), jnp.max(s_ctx, axis=-1, keepdims=True))
        e_loc = jnp.exp(s_loc - m)
        e_ctx = jnp.exp(s_ctx - m)
        den = jnp.sum(e_loc, axis=-1, keepdims=True) + jnp.sum(e_ctx, axis=-1, keepdims=True)
        oh = (jnp.dot(e_loc.astype(BF16), vs, preferred_element_type=F32)
              + jnp.dot(e_ctx.astype(BF16), vc, preferred_element_type=F32)) / den
        o = jnp.where(in_head, oh, o)
    o_ref[...] = o


def _na_bias_tables(rpb):
    n_dr, n_dc = 2 * NA_WIN_H - 1, 2 * NA_WIN_W - 1
    cq = np.arange(GRID_W)[:, None]
    ck = np.arange(GRID_W)[None, :]
    wc0 = np.clip(cq - NA_WIN_W // 2, 0, GRID_W - NA_WIN_W)
    col_ok = (ck >= wc0) & (ck < wc0 + NA_WIN_W)
    col_pick = np.clip(ck - cq + NA_WIN_W - 1, 0, n_dc - 1)[..., None] == np.arange(n_dc)
    row_pick, valid = [], []
    for tile in (0, 1, GRID_H // 4 - 1):
        slab0 = int(np.clip(tile - 1, 0, GRID_H // 4 - NA_SLAB_ROWS // 4)) * 4
        rq = (tile * 4 + np.arange(4))[:, None]
        rk = (slab0 + np.arange(NA_SLAB_ROWS))[None, :]
        wr0 = np.clip(rq - NA_WIN_H // 2, 0, GRID_H - NA_WIN_H)
        row_ok = (rk >= wr0) & (rk < wr0 + NA_WIN_H)
        row_pick.append(np.clip(rk - rq + NA_WIN_H - 1, 0, n_dr - 1)[..., None] == np.arange(n_dr))
        valid.append(row_ok[:, None, :, None] & col_ok[None, :, None, :])
    by_col = jnp.einsum("hab,qcb->haqc", rpb.astype(F32), jnp.asarray(col_pick, F32), precision=HIGHEST)
    table = jnp.einsum("prka,haqc->phrqkc", jnp.asarray(np.stack(row_pick), F32), by_col, precision=HIGHEST)
    table = jnp.where(jnp.asarray(np.stack(valid))[:, None], table, NEG_BIG)
    return table.reshape(3, HEADS, TM, NA_SLAB)


def na_latent(p, kc, vc, bias):
    n_t = LAT_TILES_PER_SEQ
    seq_blk0 = TP // DEC_SEQ

    def bias_idx(b, i):
        return (jnp.minimum(i, 1) + i // (n_t - 1), 0, 0, 0)

    return pl.pallas_call(
        _na_kernel,
        grid=(DEC_BATCH, n_t),
        in_specs=[
            pl.BlockSpec((TM, MIX_BLK), lambda b, i: (CTX_TILES + b * n_t + i, C_CQ)),
            pl.BlockSpec((DEC_SEQ, MIX_BLK), lambda b, i: (seq_blk0 + b, C_CK)),
            pl.BlockSpec((DEC_SEQ, MIX_BLK), lambda b, i: (seq_blk0 + b, C_CV)),
            pl.BlockSpec((PAST_LEN, MIX_BLK), lambda b, i: (b, 0)),
            pl.BlockSpec((PAST_LEN, MIX_BLK), lambda b, i: (b, 0)),
            pl.BlockSpec((1, HEADS, TM, NA_SLAB), bias_idx),
        ],
        out_specs=pl.BlockSpec((TM, MIX_BLK), lambda b, i: (b * n_t + i, 0)),
        out_shape=jax.ShapeDtypeStruct((TL, MIX_BLK), F32),
        compiler_params=_cparams(("arbitrary", "arbitrary")),
        name="na_latent",
    )(p, p, p, kc, vc, bias)


def _chunk_matrices():
    r = lax.broadcasted_iota(jnp.int32, (TM, TM), 0)
    c = lax.broadcasted_iota(jnp.int32, (TM, TM), 1)
    same = (r // B_CHUNK) == (c // B_CHUNK)
    return same & (c <= r), same & (c >= r)


def _hgrn_direction(q_ref, f_ref, v_ref, lb, st_ref, o_ref, reverse):
    n_ch = TM // B_CHUNK
    lower, upper = _chunk_matrices()
    tri = upper if reverse else lower
    zq = q_ref[...]
    q = zq * jax.nn.sigmoid(zq)
    z = f_ref[...]
    logf = jnp.log(lb + (1.0 - lb) * jax.nn.sigmoid(z))
    kk = (1.0 - lb) * jax.nn.sigmoid(-z)
    v = v_ref[...]
    b = jnp.dot(jnp.where(tri, 1.0, 0.0).astype(F32), logf, precision=HIGHEST,
                preferred_element_type=F32)
    b3 = b.reshape(n_ch, B_CHUNK, MIX_BLK)
    mid = B_CHUNK // 2 if reverse else B_CHUNK // 2 - 1
    end = 0 if reverse else B_CHUNK - 1
    b_mid = b3[:, mid:mid + 1, :]
    b_end = b3[:, end:end + 1, :]
    q3 = q.reshape(n_ch, B_CHUNK, MIX_BLK)
    k3 = kk.reshape(n_ch, B_CHUNK, MIX_BLK)
    q_in = (q3 * jnp.exp(b3 - b_mid)).reshape(TM, MIX_BLK)
    k_in = (k3 * jnp.exp(b_mid - b3)).reshape(TM, MIX_BLK).astype(BF16)
    q_dec = (q3 * jnp.exp(b3)).reshape(TM, MIX_BLK).astype(BF16)
    k_dec = (k3 * jnp.exp(b_end - b3)).reshape(TM, MIX_BLK)
    g = jnp.exp(b_end)
    vb = v.astype(BF16)
    vt = v.T.astype(BF16)
    lane = _head_lanes()
    o = jnp.zeros((TM, MIX_BLK), F32)
    for h in range(HEADS):
        in_head = _lane_range(lane, h * HEAD_DIM, HEAD_DIM)
        a = _nt_dot(jnp.where(in_head, q_in, 0.0).astype(BF16), k_in)
        a = jnp.where(tri, a, 0.0).astype(BF16)
        o = jnp.where(in_head, jnp.dot(a, vb, preferred_element_type=F32), o)
    same_head = _same_head_matrix()
    row_chunk = lax.broadcasted_iota(jnp.int32, (TM, 1), 0) // B_CHUNK
    st = st_ref[...]
    order = range(n_ch - 1, -1, -1) if reverse else range(n_ch)
    for c in order:
        rows = slice(c * B_CHUNK, (c + 1) * B_CHUNK)
        o_ref[rows, :] = o[rows, :] + _nt_dot(q_dec[rows, :], st.astype(BF16))
        kv = jnp.dot(vt, jnp.where(row_chunk == c, k_dec, 0.0).astype(BF16),
                     preferred_element_type=F32)
        st = st * g[c] + jnp.where(same_head, kv, 0.0)
    st_ref[...] = st


def _hgrn_kernel(qf_ref, ff_ref, vf_ref, qb_ref, fb_ref, vb_ref, lb_ref, s0_ref,
                 of_ref, ob_ref, s_ref, stf_ref, stb_ref, *, has_s0):
    j = pl.program_id(1)

    @pl.when(j == 0)
    def _():
        if has_s0:
            stf_ref[...] = s0_ref[0, 0]
            stb_ref[...] = s0_ref[0, 1]
        else:
            stf_ref[...] = jnp.zeros((MIX_BLK, MIX_BLK), F32)
            stb_ref[...] = jnp.zeros((MIX_BLK, MIX_BLK), F32)

    lb = lb_ref[...]
    _hgrn_direction(qf_ref, ff_ref, vf_ref, lb[0:1], stf_ref, of_ref, False)
    _hgrn_direction(qb_ref, fb_ref, vb_ref, lb[1:2], stb_ref, ob_ref, True)

    @pl.when(j == pl.num_programs(1) - 1)
    def _():
        s_ref[0, 0] = stf_ref[...]
        s_ref[0, 1] = stb_ref[...]


def hgrn(p, row_tile0, n_seq, seq_len, lb, s0):
    nb = seq_len // TM
    has_s0 = s0 is not None
    if s0 is None:
        s0 = jnp.zeros((1, 2, MIX_BLK, MIX_BLK), F32)

    def fwd(col):
        return pl.BlockSpec((TM, MIX_BLK), lambda s, j: (row_tile0 + s * nb + j, col))

    def bwd(col):
        return pl.BlockSpec((TM, MIX_BLK), lambda s, j: (row_tile0 + s * nb + nb - 1 - j, col))

    state_spec = pl.BlockSpec((1, 2, MIX_BLK, MIX_BLK), lambda s, j: (s if has_s0 else 0, 0, 0, 0))
    out_rows = n_seq * seq_len
    return pl.pallas_call(
        functools.partial(_hgrn_kernel, has_s0=has_s0),
        grid=(n_seq, nb),
        in_specs=[fwd(C_BQ), fwd(C_BFF), fwd(C_BV), bwd(C_BQ), bwd(C_BFB), bwd(C_BV),
                  pl.BlockSpec((2, MIX_BLK), lambda s, j: (0, 0)), state_spec],
        out_specs=[
            pl.BlockSpec((TM, MIX_BLK), lambda s, j: (s * nb + j, 0)),
            pl.BlockSpec((TM, MIX_BLK), lambda s, j: (s * nb + nb - 1 - j, 0)),
            pl.BlockSpec((1, 2, MIX_BLK, MIX_BLK), lambda s, j: (s, 0, 0, 0)),
        ],
        out_shape=[jax.ShapeDtypeStruct((out_rows, MIX_BLK), F32),
                   jax.ShapeDtypeStruct((out_rows, MIX_BLK), F32),
                   jax.ShapeDtypeStruct((n_seq, 2, MIX_BLK, MIX_BLK), F32)],
        scratch_shapes=[pltpu.VMEM((MIX_BLK, MIX_BLK), F32), pltpu.VMEM((MIX_BLK, MIX_BLK), F32)],
        compiler_params=_cparams(("arbitrary", "arbitrary")),
        name="hgrn_%d" % seq_len,
    )(p, p, p, p, p, p, lb, s0)


def _state_to_blockdiag(s):
    st = jnp.swapaxes(s.astype(F32), -1, -2)
    eye = jnp.eye(HEADS, dtype=F32)
    full = st[:, :, :, :, None, :] * eye[None, None, :, None, :, None]
    return full.reshape(s.shape[0], 2, MIX_BLK, MIX_BLK)


def _blockdiag_to_state(st):
    n = st.shape[0]
    r = st.reshape(n, 2, HEADS, HEAD_DIM, HEADS, HEAD_DIM)
    diag = jnp.stack([r[:, :, h, :, h, :] for h in range(HEADS)], axis=2)
    return jnp.swapaxes(diag, -1, -2)


def _fft_kernel(u_ref, c64_ref, s64_ref, cl_ref, sl_ref, o_ref, a_ref, b_ref, *, norm):
    @pl.when(pl.program_id(1) == 0)
    def _():
        u = u_ref[...].astype(BF16)
        a_ref[...] = jnp.dot(u, c64_ref[...], preferred_element_type=F32).astype(BF16)
        b_ref[...] = jnp.dot(u, s64_ref[...], preferred_element_type=F32).astype(BF16)

    o_ref[...] = (jnp.dot(cl_ref[...], a_ref[...], preferred_element_type=F32)
                  - jnp.dot(sl_ref[...], b_ref[...], preferred_element_type=F32)) * norm


def _dft_tables(n):
    k = np.arange(n)
    ang = 2.0 * np.pi * ((k[:, None] * k[None, :]) % n) / n
    return np.cos(ang), np.sin(ang)


def _dft_constants(seq_len):
    c64, s64 = _dft_tables(HEAD_DIM)
    eye = np.eye(HEADS)
    cl, sl = _dft_tables(seq_len)
    as_bf16 = lambda a: jnp.asarray(a, F32).astype(BF16)
    return as_bf16(np.kron(eye, c64)), as_bf16(np.kron(eye, s64)), as_bf16(cl), as_bf16(sl)


def fourier_mix(p, row_blk0, n_seq, seq_len, consts):
    c64, s64, cl, sl = consts
    nb = seq_len // TM
    norm = 1.0 / math.sqrt(seq_len * HEAD_DIM)
    return pl.pallas_call(
        functools.partial(_fft_kernel, norm=norm),
        grid=(n_seq, nb),
        in_specs=[
            pl.BlockSpec((seq_len, MIX_BLK), lambda s, i: (row_blk0 + s, C_DU)),
            pl.BlockSpec((MIX_BLK, MIX_BLK), lambda s, i: (0, 0)),
            pl.BlockSpec((MIX_BLK, MIX_BLK), lambda s, i: (0, 0)),
            pl.BlockSpec((TM, seq_len), lambda s, i: (i, 0)),
            pl.BlockSpec((TM, seq_len), lambda s, i: (i, 0)),
        ],
        out_specs=pl.BlockSpec((TM, MIX_BLK), lambda s, i: (s * nb + i, 0)),
        out_shape=jax.ShapeDtypeStruct((n_seq * seq_len, MIX_BLK), F32),
        scratch_shapes=[pltpu.VMEM((seq_len, MIX_BLK), BF16), pltpu.VMEM((seq_len, MIX_BLK), BF16)],
        compiler_params=_cparams(("arbitrary", "arbitrary")),
        name="fourier_%d" % seq_len,
    )(p, c64, s64, cl, sl)


def _route(logits_t, rb):
    per = N_EXPERTS // N_GROUPS
    score = [jax.nn.sigmoid(logits_t[e:e + 1, :]) for e in range(N_EXPERTS)]
    sel = [score[e] + rb[e:e + 1, :] for e in range(N_EXPERTS)]
    gscore = []
    for g in range(N_GROUPS):
        vals = sel[g * per:(g + 1) * per]
        best = None
        for a in range(per):
            for b in range(a + 1, per):
                pair = vals[a] + vals[b]
                best = pair if best is None else jnp.maximum(best, pair)
        gscore.append(best)
    chosen = []
    for g in range(N_GROUPS):
        ok = None
        for j in range(N_GROUPS):
            if j == g:
                continue
            cond = gscore[g] > gscore[j] if j < g else gscore[g] >= gscore[j]
            ok = cond if ok is None else ok & cond
        chosen.append(ok)
    picked = []
    for e in range(N_EXPERTS):
        g = e // per
        rank = jnp.zeros_like(sel[e])
        for j in range(g * per, (g + 1) * per):
            if j == e:
                continue
            ahead = sel[j] >= sel[e] if j < e else sel[j] > sel[e]
            rank = rank + jnp.where(ahead, 1.0, 0.0)
        picked.append(chosen[g] & (rank < 2.0))
    wsum = jnp.zeros_like(score[0])
    for e in range(N_EXPERTS):
        wsum = wsum + jnp.where(picked[e], score[e], 0.0)
    bucket = jnp.zeros_like(wsum)
    w_a = jnp.zeros_like(wsum)
    w_b = jnp.zeros_like(wsum)
    for g in range(N_GROUPS):
        for n, (a, b) in enumerate(EXPERT_PAIRS):
            hit = picked[g * per + a] & picked[g * per + b]
            bucket = jnp.where(hit, float(g * len(EXPERT_PAIRS) + n), bucket)
            w_a = jnp.where(hit, score[g * per + a] / wsum, w_a)
            w_b = jnp.where(hit, score[g * per + b] / wsum, w_b)
    return bucket, w_a, w_b


def _out_kernel(x_ref, oa_ref, of_ref, ob_ref, oc_ref, od_ref, bg_ref, mod_ref, hg_ref, w_ref,
                g2_ref, rw_ref, rb_ref, x1_ref, h2_ref, bucket_ref, rank_ref, counts_ref, run_ref):
    @pl.when(pl.program_id(0) == 0)
    def _():
        run_ref[...] = jnp.zeros(run_ref.shape, F32)

    mod = mod_ref[0]
    hb = of_ref[...] + ob_ref[...]
    zg = bg_ref[...]
    hb = hb * lax.rsqrt(_head_mean_square(hb) + EPS) * hg_ref[...] * (zg * jax.nn.sigmoid(zg))
    parts = (oa_ref[...], hb, oc_ref[...], od_ref[...])
    mixed = jnp.zeros((TM, D_MODEL), F32)
    for n, part in enumerate(parts):
        mixed = mixed + jnp.dot(part.astype(BF16), w_ref[n * MIX_BLK:(n + 1) * MIX_BLK, :],
                                preferred_element_type=F32)
    x1 = x_ref[...] + mod[2:3] * mixed
    x1_ref[...] = x1
    ms = jnp.mean(x1 * x1, axis=-1, keepdims=True)
    h2 = x1 * lax.rsqrt(ms + EPS) * g2_ref[...] * (1.0 + mod[4:5]) + mod[3:4]
    bucket, w_a, w_b = _route(_nt_dot(rw_ref[...], h2, precision=HIGHEST), rb_ref[...])
    h2_ref[:, :D_MODEL] = h2
    h2_ref[:, D_MODEL:] = jnp.concatenate([w_a, w_b, jnp.zeros((LANES - 2, TM), F32)], axis=0).T
    onehot = jnp.where(lax.broadcasted_iota(jnp.int32, (BUCKET_ROWS, 1), 0).astype(F32) == bucket, 1.0, 0.0)
    s_idx = lax.broadcasted_iota(jnp.int32, (TM, TM), 0)
    t_idx = lax.broadcasted_iota(jnp.int32, (TM, TM), 1)
    prefix = jnp.dot(onehot.astype(BF16), jnp.where(s_idx <= t_idx, 1.0, 0.0).astype(BF16),
                     preferred_element_type=F32)
    run = run_ref[...]
    rank = jnp.sum(onehot * (prefix - 1.0 + run[:, 0:1]), axis=0, keepdims=True)
    run = run + jnp.sum(onehot, axis=1, keepdims=True)
    run_ref[...] = run
    bucket_ref[...] = bucket.astype(jnp.int32)
    rank_ref[...] = rank.astype(jnp.int32)
    counts_ref[...] = run


def out_and_route(x, o_a, o_f, o_b, o_c, o_d, p, mod, hgrn_g, w_out_bf16, norm2_g, router_w, router_b):
    tile = lambda w: pl.BlockSpec((TM, w), lambda i: (i, 0))
    full = lambda r, c: pl.BlockSpec((r, c), lambda i: (0, 0))
    return pl.pallas_call(
        _out_kernel,
        grid=(N_TILES,),
        in_specs=[
            tile(D_MODEL), tile(MIX_BLK), tile(MIX_BLK), tile(MIX_BLK), tile(MIX_BLK), tile(MIX_BLK),
            pl.BlockSpec((TM, MIX_BLK), lambda i: (i, C_BG)),
            pl.BlockSpec((1, 6, D_MODEL), lambda i: (_mod_row(i), 0, 0)),
            full(1, MIX_BLK), full(D_MODEL, D_MODEL), full(1, D_MODEL),
            full(N_EXPERTS, D_MODEL), full(N_EXPERTS, 1),
        ],
        out_specs=[tile(D_MODEL), tile(ROW_W), pl.BlockSpec((1, TM), lambda i: (0, i)),
                   pl.BlockSpec((1, TM), lambda i: (0, i)), full(BUCKET_ROWS, LANES)],
        out_shape=[jax.ShapeDtypeStruct((T, D_MODEL), F32),
                   jax.ShapeDtypeStruct((T, ROW_W), F32),
                   jax.ShapeDtypeStruct((1, T), jnp.int32),
                   jax.ShapeDtypeStruct((1, T), jnp.int32),
                   jax.ShapeDtypeStruct((BUCKET_ROWS, LANES), F32)],
        scratch_shapes=[pltpu.VMEM((BUCKET_ROWS, LANES), F32)],
        compiler_params=_cparams(("arbitrary",)),
        name="out_and_route",
    )(x, o_a, o_f, o_b, o_c, o_d, p, mod, jnp.tile(hgrn_g, HEADS)[None, :], w_out_bf16,
      norm2_g[None, :], router_w.T, router_b[:, None])


def routing_plan(bucket, rank, counts):
    counts = counts[:N_BUCKETS, 0].astype(jnp.int32)
    n_tiles = (counts + TM - 1) // TM
    tile_end = jnp.cumsum(n_tiles)
    tile_start = tile_end - n_tiles
    buckets = jnp.arange(N_BUCKETS, dtype=jnp.int32)
    start_of_token = jnp.sum(jnp.where(bucket[0][:, None] == buckets[None, :], tile_start[None, :], 0), axis=1)
    dest = start_of_token * TM + rank[0]
    tiles = jnp.arange(MAX_TILES, dtype=jnp.int32)
    valid = tiles < tile_end[-1]
    tile_bucket = jnp.sum((jnp.minimum(tiles, tile_end[-1] - 1)[:, None] >= tile_end[None, :]).astype(jnp.int32), axis=1)
    pair_a = np.array([a for a, _ in EXPERT_PAIRS], np.int32)
    pair_b = np.array([b for _, b in EXPERT_PAIRS], np.int32)
    per = N_EXPERTS // N_GROUPS
    exp_a = jnp.asarray((np.arange(N_BUCKETS) // len(EXPERT_PAIRS)) * per + np.tile(pair_a, N_GROUPS), jnp.int32)
    exp_b = jnp.asarray((np.arange(N_BUCKETS) // len(EXPERT_PAIRS)) * per + np.tile(pair_b, N_GROUPS), jnp.int32)
    pick = tile_bucket[:, None] == buckets[None, :]
    tile_a = jnp.sum(jnp.where(pick, exp_a[None, :], 0), axis=1)
    tile_b = jnp.sum(jnp.where(pick, exp_b[None, :], 0), axis=1)
    return dest.astype(jnp.int32), tile_a, tile_b, valid.astype(jnp.int32)


def _row_copy(src, src_row, dst, dst_row, sem):
    return pltpu.make_async_copy(src.at[pl.ds(src_row, 1), :], dst.at[pl.ds(dst_row, 1), :], sem)


def _scatter_kernel(dest_ref, h_ref, init_ref, o_ref, sem):
    del init_ref
    base = pl.program_id(0) * TM

    def send(r, carry):
        _row_copy(h_ref, r, o_ref, dest_ref[base + r], sem).start()
        return carry

    lax.fori_loop(0, TM, send, 0)
    pltpu.make_async_copy(h_ref, o_ref.at[pl.ds(0, TM), :], sem).wait()


def scatter_to_slots(h2, dest):
    return pl.pallas_call(
        _scatter_kernel,
        grid_spec=pltpu.PrefetchScalarGridSpec(
            num_scalar_prefetch=1,
            grid=(N_TILES,),
            in_specs=[pl.BlockSpec((TM, ROW_W), lambda i, d: (i, 0)),
                      pl.BlockSpec(memory_space=pl.ANY)],
            out_specs=pl.BlockSpec(memory_space=pl.ANY),
            scratch_shapes=[pltpu.SemaphoreType.DMA(())],
        ),
        out_shape=jax.ShapeDtypeStruct((N_SLOTS, ROW_W), F32),
        input_output_aliases={2: 0},
        compiler_params=_cparams(("arbitrary",)),
        name="scatter_to_slots",
    )(dest, h2, jnp.zeros((N_SLOTS, ROW_W), F32))


def _moe_kernel(ta_ref, tb_ref, valid_ref, h_ref, wga_ref, wua_ref, wda_ref, wgb_ref, wub_ref, wdb_ref,
                o_ref):
    del ta_ref, tb_ref
    i = pl.program_id(0)

    @pl.when(valid_ref[i] == 1)
    def _():
        x = h_ref[:, :D_MODEL].astype(BF16)
        gates = h_ref[:, D_MODEL:]
        y = jnp.zeros((TM, D_MODEL), F32)
        for n, (wg, wu, wd) in enumerate(((wga_ref, wua_ref, wda_ref), (wgb_ref, wub_ref, wdb_ref))):
            a = jnp.dot(x, wg[0].astype(BF16), preferred_element_type=F32)
            u = jnp.dot(x, wu[0].astype(BF16), preferred_element_type=F32)
            z = a * jax.nn.sigmoid(a) * u * gates[:, n:n + 1]
            y = y + jnp.dot(z.astype(BF16), wd[0].astype(BF16), preferred_element_type=F32)
        o_ref[...] = y

    @pl.when(valid_ref[i] == 0)
    def _():
        o_ref[...] = jnp.zeros((TM, D_MODEL), F32)


def moe(h_slots, tile_a, tile_b, valid, wg, wu, wd):
    up_a = pl.BlockSpec((1, D_MODEL, D_EXPERT), lambda i, ta, tb, v: (ta[i], 0, 0))
    up_b = pl.BlockSpec((1, D_MODEL, D_EXPERT), lambda i, ta, tb, v: (tb[i], 0, 0))
    down_a = pl.BlockSpec((1, D_EXPERT, D_MODEL), lambda i, ta, tb, v: (ta[i], 0, 0))
    down_b = pl.BlockSpec((1, D_EXPERT, D_MODEL), lambda i, ta, tb, v: (tb[i], 0, 0))
    return pl.pallas_call(
        _moe_kernel,
        grid_spec=pltpu.PrefetchScalarGridSpec(
            num_scalar_prefetch=3,
            grid=(MAX_TILES,),
            in_specs=[pl.BlockSpec((TM, ROW_W), lambda i, ta, tb, v: (i, 0)),
                      up_a, up_a, down_a, up_b, up_b, down_b],
            out_specs=pl.BlockSpec((TM, D_MODEL), lambda i, ta, tb, v: (i, 0)),
        ),
        out_shape=jax.ShapeDtypeStruct((N_SLOTS, D_MODEL), F32),
        compiler_params=_cparams(("arbitrary",)),
        name="moe",
    )(tile_a, tile_b, valid, h_slots, wg, wu, wd, wg, wu, wd)


def _gather_tile(dest_ref, y_ref, buf_ref, sem, tile, slot):
    def fetch(r, carry):
        pltpu.make_async_copy(y_ref.at[pl.ds(dest_ref[tile * TM + r], 1), :],
                              buf_ref.at[slot, pl.ds(r, 1), :], sem.at[slot]).start()
        return carry

    lax.fori_loop(0, TM, fetch, 0)


def _moe_residual(dest_ref, y_ref, buf_ref, sem, x1_ref, mod_ref):
    i = pl.program_id(0)
    slot = i % 2

    @pl.when(i == 0)
    def _():
        _gather_tile(dest_ref, y_ref, buf_ref, sem, 0, 0)

    @pl.when(i + 1 < pl.num_programs(0))
    def _():
        _gather_tile(dest_ref, y_ref, buf_ref, sem, i + 1, 1 - slot)

    pltpu.make_async_copy(y_ref.at[pl.ds(0, TM), :], buf_ref.at[slot], sem.at[slot]).wait()
    return x1_ref[...] + mod_ref[0][5:6] * buf_ref[slot]


def _final_kernel(dest_ref, y_ref, x1_ref, mod_ref, g_ref, o_ref, buf_ref, sem):
    x2 = _moe_residual(dest_ref, y_ref, buf_ref, sem, x1_ref, mod_ref)
    ms = jnp.mean(x2 * x2, axis=-1, keepdims=True)
    o_ref[...] = x2 * lax.rsqrt(ms + EPS) * g_ref[...]


_GATHER_SCRATCH = [pltpu.VMEM((2, TM, D_MODEL), F32), pltpu.SemaphoreType.DMA((2,))]


def final_norm(dest, y_slots, x1, mod, final_g):
    return pl.pallas_call(
        _final_kernel,
        grid_spec=pltpu.PrefetchScalarGridSpec(
            num_scalar_prefetch=1,
            grid=(N_TILES,),
            in_specs=[pl.BlockSpec(memory_space=pl.ANY),
                      pl.BlockSpec((TM, D_MODEL), lambda i, d: (i, 0)),
                      pl.BlockSpec((1, 6, D_MODEL), lambda i, d: (_mod_row(i), 0, 0)),
                      pl.BlockSpec((1, D_MODEL), lambda i, d: (0, 0))],
            out_specs=pl.BlockSpec((TM, D_MODEL), lambda i, d: (i, 0)),
            scratch_shapes=_GATHER_SCRATCH,
        ),
        out_shape=jax.ShapeDtypeStruct((T, D_MODEL), F32),
        compiler_params=_cparams(("arbitrary",)),
        name="final_norm",
    )(dest, y_slots, x1, mod, final_g[None, :])


def _heads_to_lanes(t):
    b, h, l, d = t.shape
    return t.transpose(0, 2, 1, 3).reshape(b * l, h * d)


def _lanes_to_heads(t, b):
    return t.reshape(b, -1, HEADS, HEAD_DIM).transpose(0, 2, 1, 3)


def _col(p, c, rows=slice(None)):
    return p[rows, c * MIX_BLK:(c + 1) * MIX_BLK]


def kernel(x_prompt, x_sample, cache_diff_k, cache_diff_v, cache_na_k, cache_na_v, state_hgrn, c, c_ctx,
           norm1_g, norm2_g, ada_w, ada_b, w_in, w_out, diff_lambda, diff_subln_g, hgrn_lb_logits,
           hgrn_norm_g, na_rpb, router_w, router_b, moe_w_gate, moe_w_up, moe_w_down, final_norm_g):
    x = jnp.concatenate([x_prompt.reshape(TP, D_MODEL), x_sample.reshape(TL, D_MODEL)], axis=0)
    mods = modulation(jnp.concatenate([c_ctx[None, :], c], axis=0), ada_w, ada_b)
    mods = mods.reshape(DEPTH, 3, 6, D_MODEL)
    lb_sm = jax.nn.softmax(hgrn_lb_logits.astype(F32), axis=0)
    lb_all = jnp.cumsum(lb_sm, axis=0) - lb_sm[0:1]
    rope = _rope_tables()
    dft_ctx = _dft_constants(SEQ)
    dft_lat = _dft_constants(DEC_SEQ)
    lat_tile0 = CTX_TILES
    n_lt = LAT_TILES_PER_SEQ
    lat = slice(TP, T)
    ctx = slice(0, TP)
    caches = [[], [], [], [], []]
    moe_state = None
    for l in range(DEPTH):
        if moe_state is None:
            p = projection(x, mods[l], norm1_g[l], w_in[l].astype(BF16), rope)
        else:
            x, p = projection_after_moe(*moe_state, mods[l - 1], mods[l], norm1_g[l], w_in[l].astype(BF16), rope)

        lq = diff_lambda[l].astype(F32)
        lam_init = 0.8 - 0.6 * math.exp(-0.3 * l)
        lam = (jnp.exp(jnp.sum(lq[0] * lq[1])) - jnp.exp(jnp.sum(lq[2] * lq[3])) + lam_init).reshape(1)
        subln = jnp.tile(diff_subln_g[l], HEADS)[None, :]
        ka_all = jnp.concatenate([_heads_to_lanes(cache_diff_k[:, l]).reshape(DEC_BATCH, PAST_LEN, MIX_BLK),
                                  _col(p, C_AK, lat).reshape(DEC_BATCH, DEC_SEQ, MIX_BLK)], axis=1)
        va_all = jnp.concatenate([_heads_to_lanes(cache_diff_v[:, l]).reshape(DEC_BATCH, PAST_LEN, MIX_BLK),
                                  _col(p, C_AV, lat).reshape(DEC_BATCH, DEC_SEQ, MIX_BLK)], axis=1)
        kv_len = PAST_LEN + DEC_SEQ
        diff = functools.partial(attention, lam=lam, norm_g=subln, n_maps=2, post_scale=1.0 - lam_init)
        oa_ctx = diff(p, lambda b, i: (b, C_AQ), p, lambda b, i: (b, C_AK), p, lambda b, i: (b, C_AV),
                      BATCH, SEQ, SEQ)
        oa_lat = diff(p, lambda b, i: (lat_tile0 + b * n_lt + i, C_AQ),
                      ka_all.reshape(DEC_BATCH * kv_len, MIX_BLK), lambda b, i: (b, 0),
                      va_all.reshape(DEC_BATCH * kv_len, MIX_BLK), lambda b, i: (b, 0),
                      DEC_BATCH, DEC_SEQ, kv_len)

        of_ctx, ob_ctx, st_ctx = hgrn(p, 0, BATCH, SEQ, lb_all[l], None)
        of_lat, ob_lat, _ = hgrn(p, lat_tile0, DEC_BATCH, DEC_SEQ, lb_all[l],
                                 _state_to_blockdiag(state_hgrn[:, l]))

        oc_ctx = attention(p, lambda b, i: (b, C_CQ), p, lambda b, i: (b, C_CK), p, lambda b, i: (b, C_CV),
                           BATCH, SEQ, SEQ, lam, subln, n_maps=1, post_scale=1.0)
        oc_lat = na_latent(p, _heads_to_lanes(cache_na_k[:, l]), _heads_to_lanes(cache_na_v[:, l]),
                           _na_bias_tables(na_rpb[l]))

        od_ctx = fourier_mix(p, 0, BATCH, SEQ, dft_ctx)
        od_lat = fourier_mix(p, TP // DEC_SEQ, DEC_BATCH, DEC_SEQ, dft_lat)

        cat = lambda a, b: jnp.concatenate([a, b], axis=0)
        x1, h2, bucket, rank, counts = out_and_route(
            x, cat(oa_ctx, oa_lat), cat(of_ctx, of_lat), cat(ob_ctx, ob_lat), cat(oc_ctx, oc_lat),
            cat(od_ctx, od_lat), p, mods[l], hgrn_norm_g[l], w_out[l].astype(BF16), norm2_g[l],
            router_w, router_b)
        dest, tile_a, tile_b, valid = routing_plan(bucket, rank, counts)
        y_slots = moe(scatter_to_slots(h2, dest), tile_a, tile_b, valid,
                      moe_w_gate[l], moe_w_up[l], moe_w_down[l])
        moe_state = (dest, y_slots, x1)

        caches[0].append(_lanes_to_heads(_col(p, C_AK, ctx), BATCH))
        caches[1].append(_lanes_to_heads(_col(p, C_AV, ctx), BATCH))
        caches[2].append(_lanes_to_heads(_col(p, C_CK, ctx), BATCH))
        caches[3].append(_lanes_to_heads(_col(p, C_CV, ctx), BATCH))
        caches[4].append(_blockdiag_to_state(st_ctx))
    x = final_norm(*moe_state, mods[DEPTH - 1], final_norm_g)
    y_prompt = x[:TP].reshape(BATCH, SEQ, D_MODEL)
    y_sample = x[TP:].reshape(DEC_BATCH, DEC_SEQ, D_MODEL)
    return (y_prompt, y_sample) + tuple(jnp.stack(cc, axis=1) for cc in caches)
```

```python
import functools
import math

import numpy as np
import jax
import jax.numpy as jnp
from jax import lax
from jax.experimental import pallas as pl
from jax.experimental.pallas import tpu as pltpu

F32 = jnp.float32
BF16 = jnp.bfloat16
HIGHEST = lax.Precision.HIGHEST

D_MODEL = 1024
BATCH = 16
SEQ = 256
DEPTH = 2
DEC_BATCH = 2
DEC_SEQ = 2048
PAST_LEN = 256
GRID_W = 64
GRID_H = DEC_SEQ // GRID_W
EPS = 1e-6
NEG_BIG = -1e30
HEADS = 4
HEAD_DIM = 64
MIX_BLK = HEADS * HEAD_DIM
A_DIM = 32
ROPE_BASE = 10000.0
B_CHUNK = 32
NA_WIN_H = 8
NA_WIN_W = 16
N_EXPERTS = 16
N_GROUPS = 4
D_EXPERT = 512
PROJ_W = 12 * MIX_BLK
TP = BATCH * SEQ
TL = DEC_BATCH * DEC_SEQ
T = TP + TL
TM = 256
N_TILES = T // TM
CTX_TILES = TP // TM
LAT_TILES_PER_SEQ = DEC_SEQ // TM
(C_AQ, C_AK, C_AV, C_BQ, C_BFF, C_BFB, C_BV, C_BG, C_CQ, C_CK, C_CV, C_DU) = range(12)
NA_SLAB_ROWS = 12
NA_SLAB = NA_SLAB_ROWS * GRID_W
LANES = 128
ROW_W = D_MODEL + LANES
EXPERT_PAIRS = ((0, 1), (0, 2), (0, 3), (1, 3), (2, 3), (2, 1))
N_BUCKETS = N_GROUPS * len(EXPERT_PAIRS)
BUCKET_ROWS = 32
MAX_TILES = T // TM + N_BUCKETS
N_SLOTS = MAX_TILES * TM
ROW_DMA_UNROLL = 8
VMEM_LIMIT = 56 * 1024 * 1024


def _cparams(sem):
    return pltpu.CompilerParams(dimension_semantics=sem, vmem_limit_bytes=VMEM_LIMIT)


def _nt_dot(a, b, precision=None):
    return lax.dot_general(a, b, (((1,), (1,)), ((), ())), precision=precision,
                           preferred_element_type=F32)


def _head_lanes(width=MIX_BLK):
    return lax.broadcasted_iota(jnp.int32, (1, width), 1)


def _lane_range(lane, lo, n):
    return (lane >= lo) & (lane < lo + n)


def _same_head_matrix():
    r = lax.broadcasted_iota(jnp.int32, (MIX_BLK, MIX_BLK), 0) // HEAD_DIM
    c = lax.broadcasted_iota(jnp.int32, (MIX_BLK, MIX_BLK), 1) // HEAD_DIM
    return r == c


def _head_mean_square(o):
    ones = jnp.where(_same_head_matrix(), 1.0, 0.0).astype(F32)
    return jnp.dot(o * o, ones, precision=HIGHEST, preferred_element_type=F32) * (1.0 / HEAD_DIM)


def _mod_row(i):
    return jnp.where(i < CTX_TILES, 0, 1 + (i - CTX_TILES) // LAT_TILES_PER_SEQ)


def _mod_kernel(c_ref, w_ref, b_ref, o_ref):
    w = w_ref[0]
    for r in range(3):
        c = c_ref[r]
        s = c * jax.nn.sigmoid(c)
        o_ref[0, r:r + 1, :] = jnp.sum(s * w, axis=0, keepdims=True) + b_ref[0]


def modulation(c_rows, ada_w, ada_b):
    nt = 768
    n_out = 6 * D_MODEL
    return pl.pallas_call(
        _mod_kernel,
        grid=(DEPTH, n_out // nt),
        in_specs=[
            pl.BlockSpec((3, D_MODEL, 1), lambda l, j: (0, 0, 0)),
            pl.BlockSpec((1, D_MODEL, nt), lambda l, j: (l, 0, j)),
            pl.BlockSpec((1, 1, nt), lambda l, j: (l, 0, j)),
        ],
        out_specs=pl.BlockSpec((1, 3, nt), lambda l, j: (l, 0, j)),
        out_shape=jax.ShapeDtypeStruct((DEPTH, 3, n_out), F32),
        compiler_params=_cparams(("arbitrary", "arbitrary")),
        name="modulation",
    )(c_rows[:, :, None], ada_w, ada_b[:, None, :])


def _is_ctx_tile():
    return pl.program_id(0) < CTX_TILES


def _ctx_tile(i, *_):
    return (jnp.minimum(i, CTX_TILES - 1), 0)


def _lat_tile(i, *_):
    return (jnp.maximum(i - CTX_TILES, 0), 0)


def _proj_kernel(xc_ref, xl_ref, mod_ref, g_ref, w_ref, cos_ref, sa_ref, sb_ref, o_ref, *cache_refs):
    x = jnp.where(_is_ctx_tile(), xc_ref[...], xl_ref[...])
    _proj_body(x, mod_ref, g_ref, w_ref, cos_ref, sa_ref, sb_ref, o_ref, cache_refs)


def _proj_after_moe_kernel(dest_ref, y_ref, x1_ref, modp_ref, mod_ref, g_ref, w_ref, cos_ref, sa_ref, sb_ref,
                           x2_ref, o_ref, *rest):
    cache_refs, (buf_ref, sem) = rest[:4], rest[4:]
    x2 = _moe_residual(dest_ref, y_ref, buf_ref, sem, x1_ref, modp_ref)
    x2_ref[...] = x2
    _proj_body(x2, mod_ref, g_ref, w_ref, cos_ref, sa_ref, sb_ref, o_ref, cache_refs)


def _proj_body(x, mod_ref, g_ref, w_ref, cos_ref, sa_ref, sb_ref, o_ref, cache_refs):
    ms = jnp.mean(x * x, axis=-1, keepdims=True)
    mod = mod_ref[0]
    h = x * lax.rsqrt(ms + EPS) * g_ref[...] * (1.0 + mod[1:2]) + mod[0:1]
    p = jnp.dot(h.astype(BF16), w_ref[0], preferred_element_type=F32)
    t = p[:, :2 * MIX_BLK]
    o_ref[:, :2 * MIX_BLK] = (t * cos_ref[...] + pltpu.roll(t, 1, 1) * sa_ref[...]
                              + pltpu.roll(t, 2 * MIX_BLK - 1, 1) * sb_ref[...])
    o_ref[:, 2 * MIX_BLK:] = p[:, 2 * MIX_BLK:]

    @pl.when(_is_ctx_tile())
    def _():
        for ref, col in zip(cache_refs, (C_AK, C_AV, C_CK, C_CV)):
            for hd in range(HEADS):
                lo = col * MIX_BLK + hd * HEAD_DIM
                ref[0, hd] = p[:, lo:lo + HEAD_DIM]


_CACHE_SHAPE = jax.ShapeDtypeStruct((BATCH, HEADS, SEQ, HEAD_DIM), F32)


def _cache_spec():
    return pl.BlockSpec((1, HEADS, SEQ, HEAD_DIM), lambda i, *_: (jnp.minimum(i, CTX_TILES - 1), 0, 0, 0))


def _rope_tables():
    nf = A_DIM // 4
    freqs = ROPE_BASE ** (-jnp.arange(nf, dtype=F32) / nf)
    pos = jnp.arange(DEC_SEQ)
    row = (pos // GRID_W).astype(F32)
    col = (pos % GRID_W).astype(F32)
    ang = jnp.concatenate([row[:, None] * freqs, col[:, None] * freqs], axis=-1)
    cos = jnp.repeat(jnp.cos(ang), 2, axis=-1)
    sin = jnp.repeat(jnp.sin(ang), 2, axis=-1)
    odd = (jnp.arange(A_DIM) % 2 == 1)[None, :]
    sin_from_left = jnp.where(odd, sin, 0.0)
    sin_from_right = jnp.where(odd, 0.0, -sin)
    reps = 2 * MIX_BLK // A_DIM
    ident = (jnp.ones((TM, 2 * MIX_BLK), F32), jnp.zeros((TM, 2 * MIX_BLK), F32))
    return (jnp.concatenate([jnp.tile(cos, (1, reps)), ident[0]], axis=0),
            jnp.concatenate([jnp.tile(sin_from_left, (1, reps)), ident[1]], axis=0),
            jnp.concatenate([jnp.tile(sin_from_right, (1, reps)), ident[1]], axis=0))


def _rope_block(i):
    return (jnp.where(i < CTX_TILES, LAT_TILES_PER_SEQ, (i - CTX_TILES) % LAT_TILES_PER_SEQ), 0)


def projection_after_moe(dest, y_slots, x1, mod_prev, mod, layer, norm_g, w_in_bf16, rope):
    rope_spec = pl.BlockSpec((TM, 2 * MIX_BLK), lambda i, d: _rope_block(i))
    mod_spec = pl.BlockSpec((1, 6, D_MODEL), lambda i, d: (_mod_row(i), 0, 0))
    return pl.pallas_call(
        _proj_after_moe_kernel,
        grid_spec=pltpu.PrefetchScalarGridSpec(
            num_scalar_prefetch=1,
            grid=(N_TILES,),
            in_specs=[pl.BlockSpec(memory_space=pl.ANY),
                      pl.BlockSpec((TM, D_MODEL), lambda i, d: (i, 0)),
                      mod_spec, mod_spec,
                      pl.BlockSpec((1, D_MODEL), lambda i, d: (0, 0)),
                      pl.BlockSpec((1, D_MODEL, PROJ_W), lambda i, d: (layer, 0, 0)),
                      rope_spec, rope_spec, rope_spec],
            out_specs=[pl.BlockSpec((TM, D_MODEL), lambda i, d: (i, 0)),
                       pl.BlockSpec((TM, PROJ_W), lambda i, d: (i, 0))] + [_cache_spec()] * 4,
            scratch_shapes=_GATHER_SCRATCH,
        ),
        out_shape=[jax.ShapeDtypeStruct((T, D_MODEL), F32), jax.ShapeDtypeStruct((T, PROJ_W), F32)]
        + [_CACHE_SHAPE] * 4,
        compiler_params=_cparams(("arbitrary",)),
        name="projection_after_moe",
    )(dest, y_slots, x1, mod_prev, mod, norm_g[None, :], w_in_bf16, *rope)


def projection(x_ctx, x_lat, mod, layer, norm_g, w_in_bf16, rope):
    rope_spec = pl.BlockSpec((TM, 2 * MIX_BLK), _rope_block)
    return pl.pallas_call(
        _proj_kernel,
        grid=(N_TILES,),
        in_specs=[
            pl.BlockSpec((TM, D_MODEL), _ctx_tile),
            pl.BlockSpec((TM, D_MODEL), _lat_tile),
            pl.BlockSpec((1, 6, D_MODEL), lambda i: (_mod_row(i), 0, 0)),
            pl.BlockSpec((1, D_MODEL), lambda i: (0, 0)),
            pl.BlockSpec((1, D_MODEL, PROJ_W), lambda i: (layer, 0, 0)),
            rope_spec, rope_spec, rope_spec,
        ],
        out_specs=[pl.BlockSpec((TM, PROJ_W), lambda i: (i, 0))] + [_cache_spec()] * 4,
        out_shape=[jax.ShapeDtypeStruct((T, PROJ_W), F32)] + [_CACHE_SHAPE] * 4,
        compiler_params=_cparams(("arbitrary",)),
        name="projection",
    )(x_ctx, x_lat, mod, norm_g[None, :], w_in_bf16, *rope)


def _softmax_rows(s):
    m = jnp.max(s, axis=-1, keepdims=True)
    e = jnp.exp(s - m)
    return e / jnp.sum(e, axis=-1, keepdims=True)


def _attn_kernel(lam_ref, q_ref, k_ref, v_ref, *rest, n_maps, post_scale, with_cache):
    q = q_ref[...]
    kb = k_ref[...].astype(BF16)
    vb = v_ref[...].astype(BF16)
    if with_cache:
        kc_ref, vc_ref, g_ref, o_ref = rest
        kb = jnp.concatenate([kc_ref[...].astype(BF16), kb], axis=0)
        vb = jnp.concatenate([vc_ref[...].astype(BF16), vb], axis=0)
    else:
        g_ref, o_ref = rest
    lane = _head_lanes()
    map_dim = HEAD_DIM // n_maps
    scale = map_dim ** -0.5
    o = jnp.zeros(q.shape, F32)
    for h in range(HEADS):
        probs = []
        for j in range(n_maps):
            qm = jnp.where(_lane_range(lane, h * HEAD_DIM + j * map_dim, map_dim), q, 0.0)
            probs.append(_softmax_rows(_nt_dot(qm.astype(BF16), kb) * scale))
        w = probs[0] - lam_ref[0] * probs[1] if n_maps == 2 else probs[0]
        oh = jnp.dot(w.astype(BF16), vb, preferred_element_type=F32)
        o = jnp.where(_lane_range(lane, h * HEAD_DIM, HEAD_DIM), oh, o)
    if n_maps == 2:
        o = o * lax.rsqrt(_head_mean_square(o) + EPS) * g_ref[...] * post_scale
    o_ref[...] = o


def attention(p, row_blk0, cols, n_seq, seq_len, lam, norm_g, *, n_maps, post_scale, cache=None):
    nb = seq_len // TM
    kern = functools.partial(_attn_kernel, n_maps=n_maps, post_scale=post_scale, with_cache=cache is not None)
    kv_spec = lambda col: pl.BlockSpec((seq_len, MIX_BLK), lambda b, i: (row_blk0 + b, col))
    cache_specs = [pl.BlockSpec((PAST_LEN, MIX_BLK), lambda b, i: (b, 0))] * 2 if cache is not None else []
    return pl.pallas_call(
        kern,
        grid=(n_seq, nb),
        in_specs=[
            pl.BlockSpec(memory_space=pltpu.SMEM),
            pl.BlockSpec((TM, MIX_BLK), lambda b, i: ((row_blk0 + b) * nb + i, cols[0])),
            kv_spec(cols[1]), kv_spec(cols[2]), *cache_specs,
            pl.BlockSpec((1, MIX_BLK), lambda b, i: (0, 0)),
        ],
        out_specs=pl.BlockSpec((TM, MIX_BLK), lambda b, i: (b * nb + i, 0)),
        out_shape=jax.ShapeDtypeStruct((n_seq * seq_len, MIX_BLK), F32),
        compiler_params=_cparams(("arbitrary", "arbitrary")),
        name="attention_%dmap_%d" % (n_maps, seq_len),
    )(lam, p, p, p, *(cache or ()), norm_g)


def _na_slab_start(i):
    return jnp.clip(i - 1, 0, GRID_H // 4 - NA_SLAB_ROWS // 4)


def _na_kernel(q_ref, k_ref, v_ref, kc_ref, vc_ref, bias_ref, o_ref):
    i = pl.program_id(1)
    start = pl.multiple_of(_na_slab_start(i) * TM, TM)
    ks = k_ref[pl.ds(start, NA_SLAB), :].astype(BF16)
    vs = v_ref[pl.ds(start, NA_SLAB), :].astype(BF16)
    kc = kc_ref[...].astype(BF16)
    vc = vc_ref[...].astype(BF16)
    q = q_ref[...]
    lane = _head_lanes()
    scale = HEAD_DIM ** -0.5
    o = jnp.zeros(q.shape, F32)
    for h in range(HEADS):
        in_head = _lane_range(lane, h * HEAD_DIM, HEAD_DIM)
        qm = jnp.where(in_head, q, 0.0).astype(BF16)
        s_loc = _nt_dot(qm, ks) * scale + bias_ref[0, h]
        s_ctx = _nt_dot(qm, kc) * scale
        m = jnp.maximum(jnp.max(s_loc, axis=-1, keepdims=True), jnp.max(s_ctx, axis=-1, keepdims=True))
        e_loc = jnp.exp(s_loc - m)
        e_ctx = jnp.exp(s_ctx - m)
        den = jnp.sum(e_loc, axis=-1, keepdims=True) + jnp.sum(e_ctx, axis=-1, keepdims=True)
        oh = (jnp.dot(e_loc.astype(BF16), vs, preferred_element_type=F32)
              + jnp.dot(e_ctx.astype(BF16), vc, preferred_element_type=F32)) / den
        o = jnp.where(in_head, oh, o)
    o_ref[...] = o


def _na_bias_tables(rpb):
    n_dr, n_dc = 2 * NA_WIN_H - 1, 2 * NA_WIN_W - 1
    cq = np.arange(GRID_W)[:, None]
    ck = np.arange(GRID_W)[None, :]
    wc0 = np.clip(cq - NA_WIN_W // 2, 0, GRID_W - NA_WIN_W)
    col_ok = (ck >= wc0) & (ck < wc0 + NA_WIN_W)
    col_pick = np.clip(ck - cq + NA_WIN_W - 1, 0, n_dc - 1)[..., None] == np.arange(n_dc)
    row_pick, row_ok = [], []
    for tile in (0, 1, GRID_H // 4 - 1):
        slab0 = int(np.clip(tile - 1, 0, GRID_H // 4 - NA_SLAB_ROWS // 4)) * 4
        rq = (tile * 4 + np.arange(4))[:, None]
        rk = (slab0 + np.arange(NA_SLAB) // GRID_W)[None, :]
        wr0 = np.clip(rq - NA_WIN_H // 2, 0, GRID_H - NA_WIN_H)
        row_ok.append((rk >= wr0) & (rk < wr0 + NA_WIN_H))
        row_pick.append(np.clip(rk - rq + NA_WIN_H - 1, 0, n_dr - 1)[:, None, :] == np.arange(n_dr)[:, None])
    by_col = jnp.einsum("hab,qcb->haqc", rpb.astype(F32), jnp.asarray(col_pick, F32), precision=HIGHEST)
    by_col = jnp.tile(by_col, (1, 1, 1, NA_SLAB_ROWS))
    pick = jnp.asarray(np.stack(row_pick), F32)
    table = jnp.sum(pick[:, None, :, :, None, :] * by_col[None, :, None, :, :, :], axis=3)
    valid = np.stack(row_ok)[:, None, :, None, :] & np.tile(col_ok, (1, NA_SLAB_ROWS))[None, None, None]
    table = jnp.where(jnp.asarray(valid), table, NEG_BIG)
    return table.reshape(3, HEADS, TM, NA_SLAB)


def na_latent(p, kc, vc, bias):
    n_t = LAT_TILES_PER_SEQ
    seq_blk0 = TP // DEC_SEQ

    def bias_idx(b, i):
        return (jnp.minimum(i, 1) + i // (n_t - 1), 0, 0, 0)

    return pl.pallas_call(
        _na_kernel,
        grid=(DEC_BATCH, n_t),
        in_specs=[
            pl.BlockSpec((TM, MIX_BLK), lambda b, i: (CTX_TILES + b * n_t + i, C_CQ)),
            pl.BlockSpec((DEC_SEQ, MIX_BLK), lambda b, i: (seq_blk0 + b, C_CK)),
            pl.BlockSpec((DEC_SEQ, MIX_BLK), lambda b, i: (seq_blk0 + b, C_CV)),
            pl.BlockSpec((PAST_LEN, MIX_BLK), lambda b, i: (b, 0)),
            pl.BlockSpec((PAST_LEN, MIX_BLK), lambda b, i: (b, 0)),
            pl.BlockSpec((1, HEADS, TM, NA_SLAB), bias_idx),
        ],
        out_specs=pl.BlockSpec((TM, MIX_BLK), lambda b, i: (b * n_t + i, 0)),
        out_shape=jax.ShapeDtypeStruct((TL, MIX_BLK), F32),
        compiler_params=_cparams(("arbitrary", "arbitrary")),
        name="na_latent",
    )(p, p, p, kc, vc, bias)


def _chunk_matrices():
    r = lax.broadcasted_iota(jnp.int32, (TM, TM), 0)
    c = lax.broadcasted_iota(jnp.int32, (TM, TM), 1)
    same = (r // B_CHUNK) == (c // B_CHUNK)
    return same & (c <= r), same & (c >= r)


def _hgrn_direction(q_ref, f_ref, v_ref, lb, st_ref, o_ref, reverse):
    n_ch = TM // B_CHUNK
    lower, upper = _chunk_matrices()
    tri = upper if reverse else lower
    zq = q_ref[...]
    q = zq * jax.nn.sigmoid(zq)
    z = f_ref[...]
    logf = jnp.log(lb + (1.0 - lb) * jax.nn.sigmoid(z))
    kk = (1.0 - lb) * jax.nn.sigmoid(-z)
    v = v_ref[...]
    b = jnp.dot(jnp.where(tri, 1.0, 0.0).astype(F32), logf, precision=HIGHEST,
                preferred_element_type=F32)
    b3 = b.reshape(n_ch, B_CHUNK, MIX_BLK)
    mid = B_CHUNK // 2 if reverse else B_CHUNK // 2 - 1
    end = 0 if reverse else B_CHUNK - 1
    b_mid = b3[:, mid:mid + 1, :]
    b_end = b3[:, end:end + 1, :]
    q3 = q.reshape(n_ch, B_CHUNK, MIX_BLK)
    k3 = kk.reshape(n_ch, B_CHUNK, MIX_BLK)
    q_in = (q3 * jnp.exp(b3 - b_mid)).reshape(TM, MIX_BLK)
    k_in = (k3 * jnp.exp(b_mid - b3)).reshape(TM, MIX_BLK).astype(BF16)
    q_dec = (q3 * jnp.exp(b3)).reshape(TM, MIX_BLK).astype(BF16)
    k_dec = (k3 * jnp.exp(b_end - b3)).reshape(TM, MIX_BLK)
    g = jnp.exp(b_end)
    vb = v.astype(BF16)
    vt = v.T.astype(BF16)
    lane = _head_lanes()
    o = jnp.zeros((TM, MIX_BLK), F32)
    for h in range(HEADS):
        in_head = _lane_range(lane, h * HEAD_DIM, HEAD_DIM)
        a = _nt_dot(jnp.where(in_head, q_in, 0.0).astype(BF16), k_in)
        a = jnp.where(tri, a, 0.0).astype(BF16)
        o = jnp.where(in_head, jnp.dot(a, vb, preferred_element_type=F32), o)
    same_head = _same_head_matrix()
    row_chunk = lax.broadcasted_iota(jnp.int32, (TM, 1), 0) // B_CHUNK
    st = st_ref[...]
    order = range(n_ch - 1, -1, -1) if reverse else range(n_ch)
    for c in order:
        rows = slice(c * B_CHUNK, (c + 1) * B_CHUNK)
        o_ref[rows, :] = o[rows, :] + _nt_dot(q_dec[rows, :], st.astype(BF16))
        kv = jnp.dot(vt, jnp.where(row_chunk == c, k_dec, 0.0).astype(BF16),
                     preferred_element_type=F32)
        st = st * g[c] + jnp.where(same_head, kv, 0.0)
    st_ref[...] = st


def _hgrn_kernel(qf_ref, ff_ref, vf_ref, qb_ref, fb_ref, vb_ref, lb_ref, s0_ref,
                 of_ref, ob_ref, s_ref, stf_ref, stb_ref, *, has_s0):
    j = pl.program_id(1)

    @pl.when(j == 0)
    def _():
        if has_s0:
            stf_ref[...] = s0_ref[0, 0]
            stb_ref[...] = s0_ref[0, 1]
        else:
            stf_ref[...] = jnp.zeros((MIX_BLK, MIX_BLK), F32)
            stb_ref[...] = jnp.zeros((MIX_BLK, MIX_BLK), F32)

    lb = lb_ref[...]
    _hgrn_direction(qf_ref, ff_ref, vf_ref, lb[0:1], stf_ref, of_ref, False)
    _hgrn_direction(qb_ref, fb_ref, vb_ref, lb[1:2], stb_ref, ob_ref, True)

    @pl.when(j == pl.num_programs(1) - 1)
    def _():
        for d, st_ref in enumerate((stf_ref, stb_ref)):
            s = st_ref[...].T
            for hd in range(HEADS):
                lo = hd * HEAD_DIM
                s_ref[0, d, hd] = s[lo:lo + HEAD_DIM, lo:lo + HEAD_DIM]


def hgrn(p, row_tile0, n_seq, seq_len, lb, s0):
    nb = seq_len // TM
    has_s0 = s0 is not None
    if s0 is None:
        s0 = jnp.zeros((1, 2, MIX_BLK, MIX_BLK), F32)

    def fwd(col):
        return pl.BlockSpec((TM, MIX_BLK), lambda s, j: (row_tile0 + s * nb + j, col))

    def bwd(col):
        return pl.BlockSpec((TM, MIX_BLK), lambda s, j: (row_tile0 + s * nb + nb - 1 - j, col))

    state_spec = pl.BlockSpec((1, 2, MIX_BLK, MIX_BLK), lambda s, j: (s if has_s0 else 0, 0, 0, 0))
    out_rows = n_seq * seq_len
    return pl.pallas_call(
        functools.partial(_hgrn_kernel, has_s0=has_s0),
        grid=(n_seq, nb),
        in_specs=[fwd(C_BQ), fwd(C_BFF), fwd(C_BV), bwd(C_BQ), bwd(C_BFB), bwd(C_BV),
                  pl.BlockSpec((2, MIX_BLK), lambda s, j: (0, 0)), state_spec],
        out_specs=[
            pl.BlockSpec((TM, MIX_BLK), lambda s, j: (s * nb + j, 0)),
            pl.BlockSpec((TM, MIX_BLK), lambda s, j: (s * nb + nb - 1 - j, 0)),
            pl.BlockSpec((1, 2, HEADS, HEAD_DIM, HEAD_DIM), lambda s, j: (s, 0, 0, 0, 0)),
        ],
        out_shape=[jax.ShapeDtypeStruct((out_rows, MIX_BLK), F32),
                   jax.ShapeDtypeStruct((out_rows, MIX_BLK), F32),
                   jax.ShapeDtypeStruct((n_seq, 2, HEADS, HEAD_DIM, HEAD_DIM), F32)],
        scratch_shapes=[pltpu.VMEM((MIX_BLK, MIX_BLK), F32), pltpu.VMEM((MIX_BLK, MIX_BLK), F32)],
        compiler_params=_cparams(("arbitrary", "arbitrary")),
        name="hgrn_%d" % seq_len,
    )(p, p, p, p, p, p, lb, s0)


def _state_to_blockdiag(s):
    st = jnp.swapaxes(s.astype(F32), -1, -2)
    eye = jnp.eye(HEADS, dtype=F32)
    full = st[:, :, :, :, None, :] * eye[None, None, :, None, :, None]
    return full.reshape(s.shape[0], 2, MIX_BLK, MIX_BLK)


def _fft_kernel(u_ref, c64_ref, s64_ref, cl_ref, sl_ref, o_ref, a_ref, b_ref, *, norm):
    @pl.when(pl.program_id(1) == 0)
    def _():
        u = u_ref[...].astype(BF16)
        a_ref[...] = jnp.dot(u, c64_ref[...], preferred_element_type=F32).astype(BF16)
        b_ref[...] = jnp.dot(u, s64_ref[...], preferred_element_type=F32).astype(BF16)

    o_ref[...] = (jnp.dot(cl_ref[...], a_ref[...], preferred_element_type=F32)
                  - jnp.dot(sl_ref[...], b_ref[...], preferred_element_type=F32)) * norm


def _dft_tables(n):
    k = np.arange(n)
    ang = 2.0 * np.pi * ((k[:, None] * k[None, :]) % n) / n
    return np.cos(ang), np.sin(ang)


def _dft_constants(seq_len):
    c64, s64 = _dft_tables(HEAD_DIM)
    eye = np.eye(HEADS)
    cl, sl = _dft_tables(seq_len)
    as_bf16 = lambda a: jnp.asarray(a, F32).astype(BF16)
    return as_bf16(np.kron(eye, c64)), as_bf16(np.kron(eye, s64)), as_bf16(cl), as_bf16(sl)


def fourier_mix(p, row_blk0, n_seq, seq_len, consts):
    c64, s64, cl, sl = consts
    nb = seq_len // TM
    norm = 1.0 / math.sqrt(seq_len * HEAD_DIM)
    return pl.pallas_call(
        functools.partial(_fft_kernel, norm=norm),
        grid=(n_seq, nb),
        in_specs=[
            pl.BlockSpec((seq_len, MIX_BLK), lambda s, i: (row_blk0 + s, C_DU)),
            pl.BlockSpec((MIX_BLK, MIX_BLK), lambda s, i: (0, 0)),
            pl.BlockSpec((MIX_BLK, MIX_BLK), lambda s, i: (0, 0)),
            pl.BlockSpec((TM, seq_len), lambda s, i: (i, 0)),
            pl.BlockSpec((TM, seq_len), lambda s, i: (i, 0)),
        ],
        out_specs=pl.BlockSpec((TM, MIX_BLK), lambda s, i: (s * nb + i, 0)),
        out_shape=jax.ShapeDtypeStruct((n_seq * seq_len, MIX_BLK), F32),
        scratch_shapes=[pltpu.VMEM((seq_len, MIX_BLK), BF16), pltpu.VMEM((seq_len, MIX_BLK), BF16)],
        compiler_params=_cparams(("arbitrary", "arbitrary")),
        name="fourier_%d" % seq_len,
    )(p, c64, s64, cl, sl)


def _route(logits_t, rb):
    per = N_EXPERTS // N_GROUPS
    score = [jax.nn.sigmoid(logits_t[e:e + 1, :]) for e in range(N_EXPERTS)]
    sel = [score[e] + rb[e:e + 1, :] for e in range(N_EXPERTS)]
    gscore = []
    for g in range(N_GROUPS):
        vals = sel[g * per:(g + 1) * per]
        best = None
        for a in range(per):
            for b in range(a + 1, per):
                pair = vals[a] + vals[b]
                best = pair if best is None else jnp.maximum(best, pair)
        gscore.append(best)
    chosen = []
    for g in range(N_GROUPS):
        ok = None
        for j in range(N_GROUPS):
            if j == g:
                continue
            cond = gscore[g] > gscore[j] if j < g else gscore[g] >= gscore[j]
            ok = cond if ok is None else ok & cond
        chosen.append(ok)
    picked = []
    for e in range(N_EXPERTS):
        g = e // per
        rank = jnp.zeros_like(sel[e])
        for j in range(g * per, (g + 1) * per):
            if j == e:
                continue
            ahead = sel[j] >= sel[e] if j < e else sel[j] > sel[e]
            rank = rank + jnp.where(ahead, 1.0, 0.0)
        picked.append(chosen[g] & (rank < 2.0))
    wsum = jnp.zeros_like(score[0])
    for e in range(N_EXPERTS):
        wsum = wsum + jnp.where(picked[e], score[e], 0.0)
    bucket = jnp.zeros_like(wsum)
    w_a = jnp.zeros_like(wsum)
    w_b = jnp.zeros_like(wsum)
    for g in range(N_GROUPS):
        for n, (a, b) in enumerate(EXPERT_PAIRS):
            hit = picked[g * per + a] & picked[g * per + b]
            bucket = jnp.where(hit, float(g * len(EXPERT_PAIRS) + n), bucket)
            w_a = jnp.where(hit, score[g * per + a] / wsum, w_a)
            w_b = jnp.where(hit, score[g * per + b] / wsum, w_b)
    return bucket, w_a, w_b


def _out_kernel(*refs):
    streams, rest = refs[:12], refs[12:]
    (bg_ref, mod_ref, hg_ref, w_ref, g2_ref, rw_ref, rb_ref,
     x1_ref, h2_ref, bucket_ref, rank_ref, counts_ref, run_ref) = rest
    is_ctx = _is_ctx_tile()
    x, o_a, o_f, o_b, o_c, o_d = (jnp.where(is_ctx, streams[2 * n][...], streams[2 * n + 1][...])
                                  for n in range(6))

    @pl.when(pl.program_id(0) == 0)
    def _():
        run_ref[...] = jnp.zeros(run_ref.shape, F32)

    mod = mod_ref[0]
    hb = o_f + o_b
    zg = bg_ref[...]
    hb = hb * lax.rsqrt(_head_mean_square(hb) + EPS) * hg_ref[...] * (zg * jax.nn.sigmoid(zg))
    parts = (o_a, hb, o_c, o_d)
    mixed = jnp.zeros((TM, D_MODEL), F32)
    for n, part in enumerate(parts):
        mixed = mixed + jnp.dot(part.astype(BF16), w_ref[0, n * MIX_BLK:(n + 1) * MIX_BLK, :],
                                preferred_element_type=F32)
    x1 = x + mod[2:3] * mixed
    x1_ref[...] = x1
    ms = jnp.mean(x1 * x1, axis=-1, keepdims=True)
    h2 = x1 * lax.rsqrt(ms + EPS) * g2_ref[...] * (1.0 + mod[4:5]) + mod[3:4]
    bucket, w_a, w_b = _route(_nt_dot(rw_ref[...], h2, precision=HIGHEST), rb_ref[...])
    h2_ref[:, :D_MODEL] = h2
    h2_ref[:, D_MODEL:] = jnp.concatenate([w_a, w_b, jnp.zeros((LANES - 2, TM), F32)], axis=0).T
    onehot = jnp.where(lax.broadcasted_iota(jnp.int32, (BUCKET_ROWS, 1), 0).astype(F32) == bucket, 1.0, 0.0)
    s_idx = lax.broadcasted_iota(jnp.int32, (TM, TM), 0)
    t_idx = lax.broadcasted_iota(jnp.int32, (TM, TM), 1)
    prefix = jnp.dot(onehot.astype(BF16), jnp.where(s_idx <= t_idx, 1.0, 0.0).astype(BF16),
                     preferred_element_type=F32)
    run = run_ref[...]
    rank = jnp.sum(onehot * (prefix - 1.0 + run[:, 0:1]), axis=0, keepdims=True)
    run = run + jnp.sum(onehot, axis=1, keepdims=True)
    run_ref[...] = run
    bucket_ref[...] = bucket.astype(jnp.int32)
    rank_ref[...] = rank.astype(jnp.int32)
    counts_ref[...] = run


def out_and_route(x_pair, x_specs, mixer_pairs, p, mod, layer, hgrn_g, w_out_bf16, norm2_g, router_w, router_b):
    tile = lambda w: pl.BlockSpec((TM, w), lambda i: (i, 0))
    full = lambda r, c: pl.BlockSpec((r, c), lambda i: (0, 0))
    stream_specs = list(x_specs)
    stream_args = list(x_pair)
    for o_ctx, o_lat in mixer_pairs:
        stream_specs += [pl.BlockSpec((TM, MIX_BLK), _ctx_tile), pl.BlockSpec((TM, MIX_BLK), _lat_tile)]
        stream_args += [o_ctx, o_lat]
    return pl.pallas_call(
        _out_kernel,
        grid=(N_TILES,),
        in_specs=stream_specs + [
            pl.BlockSpec((TM, MIX_BLK), lambda i: (i, C_BG)),
            pl.BlockSpec((1, 6, D_MODEL), lambda i: (_mod_row(i), 0, 0)),
            full(1, MIX_BLK), pl.BlockSpec((1, D_MODEL, D_MODEL), lambda i: (layer, 0, 0)), full(1, D_MODEL),
            full(N_EXPERTS, D_MODEL), full(N_EXPERTS, 1),
        ],
        out_specs=[tile(D_MODEL), tile(ROW_W), pl.BlockSpec((1, TM), lambda i: (0, i)),
                   pl.BlockSpec((1, TM), lambda i: (0, i)), full(BUCKET_ROWS, LANES)],
        out_shape=[jax.ShapeDtypeStruct((T, D_MODEL), F32),
                   jax.ShapeDtypeStruct((T, ROW_W), F32),
                   jax.ShapeDtypeStruct((1, T), jnp.int32),
                   jax.ShapeDtypeStruct((1, T), jnp.int32),
                   jax.ShapeDtypeStruct((BUCKET_ROWS, LANES), F32)],
        scratch_shapes=[pltpu.VMEM((BUCKET_ROWS, LANES), F32)],
        compiler_params=_cparams(("arbitrary",)),
        name="out_and_route",
    )(*stream_args, p, mod, jnp.tile(hgrn_g, HEADS)[None, :], w_out_bf16,
      norm2_g[None, :], router_w.T, router_b[:, None])


def routing_plan(bucket, rank, counts):
    counts = counts[:N_BUCKETS, 0].astype(jnp.int32)
    n_tiles = (counts + TM - 1) // TM
    tile_end = jnp.cumsum(n_tiles)
    tile_start = tile_end - n_tiles
    buckets = jnp.arange(N_BUCKETS, dtype=jnp.int32)
    start_of_token = jnp.sum(jnp.where(bucket[0][:, None] == buckets[None, :], tile_start[None, :], 0), axis=1)
    dest = start_of_token * TM + rank[0]
    tiles = jnp.arange(MAX_TILES, dtype=jnp.int32)
    valid = tiles < tile_end[-1]
    tile_bucket = jnp.sum((jnp.minimum(tiles, tile_end[-1] - 1)[:, None] >= tile_end[None, :]).astype(jnp.int32), axis=1)
    pair_a = np.array([a for a, _ in EXPERT_PAIRS], np.int32)
    pair_b = np.array([b for _, b in EXPERT_PAIRS], np.int32)
    per = N_EXPERTS // N_GROUPS
    exp_a = jnp.asarray((np.arange(N_BUCKETS) // len(EXPERT_PAIRS)) * per + np.tile(pair_a, N_GROUPS), jnp.int32)
    exp_b = jnp.asarray((np.arange(N_BUCKETS) // len(EXPERT_PAIRS)) * per + np.tile(pair_b, N_GROUPS), jnp.int32)
    pick = tile_bucket[:, None] == buckets[None, :]
    tile_a = jnp.sum(jnp.where(pick, exp_a[None, :], 0), axis=1)
    tile_b = jnp.sum(jnp.where(pick, exp_b[None, :], 0), axis=1)
    return dest.astype(jnp.int32), tile_a, tile_b, valid.astype(jnp.int32)


def _row_copy(src, src_row, dst, dst_row, sem):
    return pltpu.make_async_copy(src.at[pl.ds(src_row, 1), :], dst.at[pl.ds(dst_row, 1), :], sem)


def _scatter_kernel(dest_ref, h_ref, init_ref, o_ref, sem):
    del init_ref
    base = pl.program_id(0) * TM

    def send(r, carry):
        _row_copy(h_ref, r, o_ref, dest_ref[base + r], sem).start()
        return carry

    lax.fori_loop(0, TM, send, 0, unroll=ROW_DMA_UNROLL)
    pltpu.make_async_copy(h_ref, o_ref.at[pl.ds(0, TM), :], sem).wait()


def scatter_to_slots(h2, dest):
    return pl.pallas_call(
        _scatter_kernel,
        grid_spec=pltpu.PrefetchScalarGridSpec(
            num_scalar_prefetch=1,
            grid=(N_TILES,),
            in_specs=[pl.BlockSpec((TM, ROW_W), lambda i, d: (i, 0)),
                      pl.BlockSpec(memory_space=pl.ANY)],
            out_specs=pl.BlockSpec(memory_space=pl.ANY),
            scratch_shapes=[pltpu.SemaphoreType.DMA(())],
        ),
        out_shape=jax.ShapeDtypeStruct((N_SLOTS, ROW_W), F32),
        input_output_aliases={2: 0},
        compiler_params=_cparams(("arbitrary",)),
        name="scatter_to_slots",
    )(dest, h2, jnp.zeros((N_SLOTS, ROW_W), F32))


def _moe_kernel(ta_ref, tb_ref, valid_ref, h_ref, wga_ref, wua_ref, wda_ref, wgb_ref, wub_ref, wdb_ref,
                o_ref):
    del ta_ref, tb_ref
    i = pl.program_id(0)

    @pl.when(valid_ref[i] == 1)
    def _():
        x = h_ref[:, :D_MODEL].astype(BF16)
        gates = h_ref[:, D_MODEL:]
        y = jnp.zeros((TM, D_MODEL), F32)
        for n, (wg, wu, wd) in enumerate(((wga_ref, wua_ref, wda_ref), (wgb_ref, wub_ref, wdb_ref))):
            a = jnp.dot(x, wg[0, 0].astype(BF16), preferred_element_type=F32)
            u = jnp.dot(x, wu[0, 0].astype(BF16), preferred_element_type=F32)
            z = a * jax.nn.sigmoid(a) * u * gates[:, n:n + 1]
            y = y + jnp.dot(z.astype(BF16), wd[0, 0].astype(BF16), preferred_element_type=F32)
        o_ref[...] = y

    @pl.when(valid_ref[i] == 0)
    def _():
        o_ref[...] = jnp.zeros((TM, D_MODEL), F32)


def moe(h_slots, tile_a, tile_b, valid, layer, wg, wu, wd):
    up_a = pl.BlockSpec((1, 1, D_MODEL, D_EXPERT), lambda i, ta, tb, v: (layer, ta[i], 0, 0))
    up_b = pl.BlockSpec((1, 1, D_MODEL, D_EXPERT), lambda i, ta, tb, v: (layer, tb[i], 0, 0))
    down_a = pl.BlockSpec((1, 1, D_EXPERT, D_MODEL), lambda i, ta, tb, v: (layer, ta[i], 0, 0))
    down_b = pl.BlockSpec((1, 1, D_EXPERT, D_MODEL), lambda i, ta, tb, v: (layer, tb[i], 0, 0))
    return pl.pallas_call(
        _moe_kernel,
        grid_spec=pltpu.PrefetchScalarGridSpec(
            num_scalar_prefetch=3,
            grid=(MAX_TILES,),
            in_specs=[pl.BlockSpec((TM, ROW_W), lambda i, ta, tb, v: (i, 0)),
                      up_a, up_a, down_a, up_b, up_b, down_b],
            out_specs=pl.BlockSpec((TM, D_MODEL), lambda i, ta, tb, v: (i, 0)),
        ),
        out_shape=jax.ShapeDtypeStruct((N_SLOTS, D_MODEL), F32),
        compiler_params=_cparams(("arbitrary",)),
        name="moe",
    )(tile_a, tile_b, valid, h_slots, wg, wu, wd, wg, wu, wd)


def _gather_tile(dest_ref, y_ref, buf_ref, sem, tile, slot):
    def fetch(r, carry):
        pltpu.make_async_copy(y_ref.at[pl.ds(dest_ref[tile * TM + r], 1), :],
                              buf_ref.at[slot, pl.ds(r, 1), :], sem.at[slot]).start()
        return carry

    lax.fori_loop(0, TM, fetch, 0, unroll=ROW_DMA_UNROLL)


def _moe_residual(dest_ref, y_ref, buf_ref, sem, x1_ref, mod_ref):
    i = pl.program_id(0)
    slot = i % 2

    @pl.when(i == 0)
    def _():
        _gather_tile(dest_ref, y_ref, buf_ref, sem, 0, 0)

    @pl.when(i + 1 < pl.num_programs(0))
    def _():
        _gather_tile(dest_ref, y_ref, buf_ref, sem, i + 1, 1 - slot)

    pltpu.make_async_copy(y_ref.at[pl.ds(0, TM), :], buf_ref.at[slot], sem.at[slot]).wait()
    return x1_ref[...] + mod_ref[0][5:6] * buf_ref[slot]


def _final_kernel(dest_ref, y_ref, x1_ref, mod_ref, g_ref, oc_ref, ol_ref, buf_ref, sem):
    x2 = _moe_residual(dest_ref, y_ref, buf_ref, sem, x1_ref, mod_ref)
    ms = jnp.mean(x2 * x2, axis=-1, keepdims=True)
    y = x2 * lax.rsqrt(ms + EPS) * g_ref[...]

    @pl.when(_is_ctx_tile())
    def _():
        oc_ref[...] = y

    @pl.when(jnp.logical_not(_is_ctx_tile()))
    def _():
        ol_ref[...] = y


_GATHER_SCRATCH = [pltpu.VMEM((2, TM, D_MODEL), F32), pltpu.SemaphoreType.DMA((2,))]


def final_norm(dest, y_slots, x1, mod, final_g):
    return pl.pallas_call(
        _final_kernel,
        grid_spec=pltpu.PrefetchScalarGridSpec(
            num_scalar_prefetch=1,
            grid=(N_TILES,),
            in_specs=[pl.BlockSpec(memory_space=pl.ANY),
                      pl.BlockSpec((TM, D_MODEL), lambda i, d: (i, 0)),
                      pl.BlockSpec((1, 6, D_MODEL), lambda i, d: (_mod_row(i), 0, 0)),
                      pl.BlockSpec((1, D_MODEL), lambda i, d: (0, 0))],
            out_specs=[pl.BlockSpec((TM, D_MODEL), _ctx_tile), pl.BlockSpec((TM, D_MODEL), _lat_tile)],
            scratch_shapes=_GATHER_SCRATCH,
        ),
        out_shape=[jax.ShapeDtypeStruct((TP, D_MODEL), F32), jax.ShapeDtypeStruct((TL, D_MODEL), F32)],
        compiler_params=_cparams(("arbitrary",)),
        name="final_norm",
    )(dest, y_slots, x1, mod, final_g[None, :])


def _heads_to_lanes(t):
    b, h, l, d = t.shape
    return t.transpose(0, 2, 1, 3).reshape(b * l, h * d)


def kernel(x_prompt, x_sample, cache_diff_k, cache_diff_v, cache_na_k, cache_na_v, state_hgrn, c, c_ctx,
           norm1_g, norm2_g, ada_w, ada_b, w_in, w_out, diff_lambda, diff_subln_g, hgrn_lb_logits,
           hgrn_norm_g, na_rpb, router_w, router_b, moe_w_gate, moe_w_up, moe_w_down, final_norm_g):
    assert SEQ == TM and PAST_LEN == TM and DEC_SEQ % TM == 0 and TP % DEC_SEQ == 0
    x_pair = (x_prompt.reshape(TP, D_MODEL), x_sample.reshape(TL, D_MODEL))
    x_specs = (pl.BlockSpec((TM, D_MODEL), _ctx_tile), pl.BlockSpec((TM, D_MODEL), _lat_tile))
    w_in_bf16 = w_in.astype(BF16)
    w_out_bf16 = w_out.astype(BF16)
    mods = modulation(jnp.concatenate([c_ctx[None, :], c], axis=0), ada_w, ada_b)
    mods = mods.reshape(DEPTH, 3, 6, D_MODEL)
    lb_sm = jax.nn.softmax(hgrn_lb_logits.astype(F32), axis=0)
    lb_all = jnp.cumsum(lb_sm, axis=0) - lb_sm[0:1]
    rope = _rope_tables()
    dft_ctx = _dft_constants(SEQ)
    dft_lat = _dft_constants(DEC_SEQ)
    lat_blk0 = TP // DEC_SEQ
    caches = [[], [], [], [], []]
    moe_state = None
    for l in range(DEPTH):
        if moe_state is None:
            p, *new_kv = projection(*x_pair, mods[l], l, norm1_g[l], w_in_bf16, rope)
        else:
            x, p, *new_kv = projection_after_moe(*moe_state, mods[l - 1], mods[l], l, norm1_g[l], w_in_bf16, rope)
            x_pair = (x, x)
            x_specs = (pl.BlockSpec((TM, D_MODEL), _ctx_tile),
                       pl.BlockSpec((TM, D_MODEL), lambda i: (jnp.maximum(i, CTX_TILES), 0)))

        lq = diff_lambda[l].astype(F32)
        lam_init = 0.8 - 0.6 * math.exp(-0.3 * l)
        lam = (jnp.exp(jnp.sum(lq[0] * lq[1])) - jnp.exp(jnp.sum(lq[2] * lq[3])) + lam_init).reshape(1)
        subln = jnp.tile(diff_subln_g[l], HEADS)[None, :]
        diff = functools.partial(attention, p, cols=(C_AQ, C_AK, C_AV), lam=lam, norm_g=subln, n_maps=2,
                                 post_scale=1.0 - lam_init)
        oa_ctx = diff(row_blk0=0, n_seq=BATCH, seq_len=SEQ)
        oa_lat = diff(row_blk0=lat_blk0, n_seq=DEC_BATCH, seq_len=DEC_SEQ,
                      cache=(_heads_to_lanes(cache_diff_k[:, l]), _heads_to_lanes(cache_diff_v[:, l])))

        of_ctx, ob_ctx, st_ctx = hgrn(p, 0, BATCH, SEQ, lb_all[l], None)
        of_lat, ob_lat, _ = hgrn(p, CTX_TILES, DEC_BATCH, DEC_SEQ, lb_all[l],
                                 _state_to_blockdiag(state_hgrn[:, l]))

        oc_ctx = attention(p, 0, (C_CQ, C_CK, C_CV), BATCH, SEQ, lam, subln, n_maps=1, post_scale=1.0)
        oc_lat = na_latent(p, _heads_to_lanes(cache_na_k[:, l]), _heads_to_lanes(cache_na_v[:, l]),
                           _na_bias_tables(na_rpb[l]))

        od_ctx = fourier_mix(p, 0, BATCH, SEQ, dft_ctx)
        od_lat = fourier_mix(p, lat_blk0, DEC_BATCH, DEC_SEQ, dft_lat)

        x1, h2, bucket, rank, counts = out_and_route(
            x_pair, x_specs, ((oa_ctx, oa_lat), (of_ctx, of_lat), (ob_ctx, ob_lat), (oc_ctx, oc_lat),
                              (od_ctx, od_lat)),
            p, mods[l], l, hgrn_norm_g[l], w_out_bf16, norm2_g[l], router_w, router_b)
        dest, tile_a, tile_b, valid = routing_plan(bucket, rank, counts)
        y_slots = moe(scatter_to_slots(h2, dest), tile_a, tile_b, valid, l, moe_w_gate, moe_w_up, moe_w_down)
        moe_state = (dest, y_slots, x1)

        for cache, new in zip(caches, new_kv + [st_ctx]):
            cache.append(new)
    y_prompt, y_sample = final_norm(*moe_state, mods[DEPTH - 1], final_norm_g)
    return (y_prompt.reshape(BATCH, SEQ, D_MODEL), y_sample.reshape(DEC_BATCH, DEC_SEQ, D_MODEL)) + tuple(
        jnp.stack(cc, axis=1) for cc in caches)
```

```python
import functools
import math

import numpy as np
import jax
import jax.numpy as jnp
from jax import lax
from jax.experimental import pallas as pl
from jax.experimental.pallas import tpu as pltpu

F32 = jnp.float32
BF16 = jnp.bfloat16
HIGHEST = lax.Precision.HIGHEST

D_MODEL = 1024
BATCH = 16
SEQ = 256
DEPTH = 2
DEC_BATCH = 2
DEC_SEQ = 2048
PAST_LEN = 256
GRID_W = 64
GRID_H = DEC_SEQ // GRID_W
EPS = 1e-6
NEG_BIG = -1e30
HEADS = 4
HEAD_DIM = 64
MIX_BLK = HEADS * HEAD_DIM
A_DIM = 32
ROPE_BASE = 10000.0
B_CHUNK = 32
NA_WIN_H = 8
NA_WIN_W = 16
N_EXPERTS = 16
N_GROUPS = 4
D_EXPERT = 512
PROJ_W = 12 * MIX_BLK
TP = BATCH * SEQ
TL = DEC_BATCH * DEC_SEQ
T = TP + TL
TM = 256
N_TILES = T // TM
CTX_TILES = TP // TM
LAT_TILES_PER_SEQ = DEC_SEQ // TM
(C_AQ, C_AK, C_AV, C_BQ, C_BFF, C_BFB, C_BV, C_BG, C_CQ, C_CK, C_CV, C_DU) = range(12)
NA_SLAB_ROWS = 12
NA_SLAB = NA_SLAB_ROWS * GRID_W
LANES = 128
ROW_W = D_MODEL + LANES
EXPERT_PAIRS = ((0, 1), (0, 2), (0, 3), (1, 3), (2, 3), (2, 1))
N_BUCKETS = N_GROUPS * len(EXPERT_PAIRS)
BUCKET_ROWS = 32
MAX_TILES = T // TM + N_BUCKETS
N_SLOTS = MAX_TILES * TM
ROW_DMA_UNROLL = 8
VMEM_LIMIT = 56 * 1024 * 1024


def _cparams(sem):
    return pltpu.CompilerParams(dimension_semantics=sem, vmem_limit_bytes=VMEM_LIMIT)


def _head_lanes(width=MIX_BLK):
    return lax.broadcasted_iota(jnp.int32, (1, width), 1)


def _lane_range(lane, lo, n):
    return (lane >= lo) & (lane < lo + n)


def _same_head_matrix():
    r = lax.broadcasted_iota(jnp.int32, (MIX_BLK, MIX_BLK), 0) // HEAD_DIM
    c = lax.broadcasted_iota(jnp.int32, (MIX_BLK, MIX_BLK), 1) // HEAD_DIM
    return r == c


def _bf16_pieces(x, n):
    pieces = []
    for _ in range(n):
        piece = x.astype(BF16)
        pieces.append(piece)
        x = x - piece.astype(F32)
    return pieces


def _select_sum_left(onehot_bf16, x):
    return sum(jnp.dot(onehot_bf16, piece, preferred_element_type=F32) for piece in _bf16_pieces(x, 3))


def _select_sum_right(x, onehot_bf16):
    return sum(jnp.dot(piece, onehot_bf16, preferred_element_type=F32) for piece in _bf16_pieces(x, 3))


def _head_mean_square(o):
    ones = jnp.where(_same_head_matrix(), 1.0, 0.0).astype(BF16)
    return _select_sum_right(o * o, ones) * (1.0 / HEAD_DIM)


def _mod_row(i):
    return jnp.where(i < CTX_TILES, 0, 1 + (i - CTX_TILES) // LAT_TILES_PER_SEQ)


def _mod_kernel(c_ref, w_ref, b_ref, o_ref):
    w = w_ref[0]
    for r in range(3):
        c = c_ref[r]
        s = c * jax.nn.sigmoid(c)
        o_ref[0, r:r + 1, :] = jnp.sum(s * w, axis=0, keepdims=True) + b_ref[0]


def modulation(c_rows, ada_w, ada_b):
    nt = 768
    n_out = 6 * D_MODEL
    return pl.pallas_call(
        _mod_kernel,
        grid=(DEPTH, n_out // nt),
        in_specs=[
            pl.BlockSpec((3, D_MODEL, 1), lambda l, j: (0, 0, 0)),
            pl.BlockSpec((1, D_MODEL, nt), lambda l, j: (l, 0, j)),
            pl.BlockSpec((1, 1, nt), lambda l, j: (l, 0, j)),
        ],
        out_specs=pl.BlockSpec((1, 3, nt), lambda l, j: (l, 0, j)),
        out_shape=jax.ShapeDtypeStruct((DEPTH, 3, n_out), F32),
        compiler_params=_cparams(("arbitrary", "arbitrary")),
        name="modulation",
    )(c_rows[:, :, None], ada_w, ada_b[:, None, :])


def _is_ctx_tile():
    return pl.program_id(0) < CTX_TILES


def _ctx_tile(i, *_):
    return (jnp.minimum(i, CTX_TILES - 1), 0)


def _lat_tile(i, *_):
    return (jnp.maximum(i - CTX_TILES, 0), 0)


def _proj_kernel(xc_ref, xl_ref, mod_ref, g_ref, w_ref, cos_ref, sa_ref, sb_ref, o_ref, *cache_refs):
    x = jnp.where(_is_ctx_tile(), xc_ref[...], xl_ref[...])
    _proj_body(x, mod_ref, g_ref, w_ref, cos_ref, sa_ref, sb_ref, o_ref, cache_refs)


def _proj_after_moe_kernel(dest_ref, y_ref, x1_ref, modp_ref, mod_ref, g_ref, w_ref, cos_ref, sa_ref, sb_ref,
                           x2_ref, o_ref, *rest):
    cache_refs, (buf_ref, sem) = rest[:4], rest[4:]
    x2 = _moe_residual(dest_ref, y_ref, buf_ref, sem, x1_ref, modp_ref)
    x2_ref[...] = x2
    _proj_body(x2, mod_ref, g_ref, w_ref, cos_ref, sa_ref, sb_ref, o_ref, cache_refs)


def _proj_body(x, mod_ref, g_ref, w_ref, cos_ref, sa_ref, sb_ref, o_ref, cache_refs):
    ms = jnp.mean(x * x, axis=-1, keepdims=True)
    mod = mod_ref[0]
    h = x * lax.rsqrt(ms + EPS) * g_ref[...] * (1.0 + mod[1:2]) + mod[0:1]
    p = jnp.dot(h.astype(BF16), w_ref[0], preferred_element_type=F32)
    t = p[:, :2 * MIX_BLK]
    o_ref[:, :2 * MIX_BLK] = (t * cos_ref[...] + pltpu.roll(t, 1, 1) * sa_ref[...]
                              + pltpu.roll(t, 2 * MIX_BLK - 1, 1) * sb_ref[...])
    o_ref[:, 2 * MIX_BLK:] = p[:, 2 * MIX_BLK:]

    @pl.when(_is_ctx_tile())
    def _():
        for ref, col in zip(cache_refs, (C_AK, C_AV, C_CK, C_CV)):
            for hd in range(HEADS):
                lo = col * MIX_BLK + hd * HEAD_DIM
                ref[0, hd] = p[:, lo:lo + HEAD_DIM]


_CACHE_SHAPE = jax.ShapeDtypeStruct((BATCH, HEADS, SEQ, HEAD_DIM), F32)


def _cache_spec():
    return pl.BlockSpec((1, HEADS, SEQ, HEAD_DIM), lambda i, *_: (jnp.minimum(i, CTX_TILES - 1), 0, 0, 0))


def _rope_tables():
    nf = A_DIM // 4
    freqs = ROPE_BASE ** (-jnp.arange(nf, dtype=F32) / nf)
    pos = jnp.arange(DEC_SEQ)
    row = (pos // GRID_W).astype(F32)
    col = (pos % GRID_W).astype(F32)
    ang = jnp.concatenate([row[:, None] * freqs, col[:, None] * freqs], axis=-1)
    cos = jnp.repeat(jnp.cos(ang), 2, axis=-1)
    sin = jnp.repeat(jnp.sin(ang), 2, axis=-1)
    odd = (jnp.arange(A_DIM) % 2 == 1)[None, :]
    sin_from_left = jnp.where(odd, sin, 0.0)
    sin_from_right = jnp.where(odd, 0.0, -sin)
    reps = 2 * MIX_BLK // A_DIM
    ident = (jnp.ones((TM, 2 * MIX_BLK), F32), jnp.zeros((TM, 2 * MIX_BLK), F32))
    return (jnp.concatenate([jnp.tile(cos, (1, reps)), ident[0]], axis=0),
            jnp.concatenate([jnp.tile(sin_from_left, (1, reps)), ident[1]], axis=0),
            jnp.concatenate([jnp.tile(sin_from_right, (1, reps)), ident[1]], axis=0))


def _rope_block(i):
    return (jnp.where(i < CTX_TILES, LAT_TILES_PER_SEQ, (i - CTX_TILES) % LAT_TILES_PER_SEQ), 0)


def projection_after_moe(dest, y_slots, x1, mod_prev, mod, layer, norm_g, w_in_bf16, rope):
    rope_spec = pl.BlockSpec((TM, 2 * MIX_BLK), lambda i, d: _rope_block(i))
    mod_spec = pl.BlockSpec((1, 6, D_MODEL), lambda i, d: (_mod_row(i), 0, 0))
    return pl.pallas_call(
        _proj_after_moe_kernel,
        grid_spec=pltpu.PrefetchScalarGridSpec(
            num_scalar_prefetch=1,
            grid=(N_TILES,),
            in_specs=[pl.BlockSpec(memory_space=pl.ANY),
                      pl.BlockSpec((TM, D_MODEL), lambda i, d: (i, 0)),
                      mod_spec, mod_spec,
                      pl.BlockSpec((1, D_MODEL), lambda i, d: (0, 0)),
                      pl.BlockSpec((1, D_MODEL, PROJ_W), lambda i, d: (layer, 0, 0)),
                      rope_spec, rope_spec, rope_spec],
            out_specs=[pl.BlockSpec((TM, D_MODEL), lambda i, d: (i, 0)),
                       pl.BlockSpec((TM, PROJ_W), lambda i, d: (i, 0))] + [_cache_spec()] * 4,
            scratch_shapes=_GATHER_SCRATCH,
        ),
        out_shape=[jax.ShapeDtypeStruct((T, D_MODEL), F32), jax.ShapeDtypeStruct((T, PROJ_W), F32)]
        + [_CACHE_SHAPE] * 4,
        compiler_params=_cparams(("arbitrary",)),
        name="projection_after_moe",
    )(dest, y_slots, x1, mod_prev, mod, norm_g[None, :], w_in_bf16, *rope)


def projection(x_ctx, x_lat, mod, layer, norm_g, w_in_bf16, rope):
    rope_spec = pl.BlockSpec((TM, 2 * MIX_BLK), _rope_block)
    return pl.pallas_call(
        _proj_kernel,
        grid=(N_TILES,),
        in_specs=[
            pl.BlockSpec((TM, D_MODEL), _ctx_tile),
            pl.BlockSpec((TM, D_MODEL), _lat_tile),
            pl.BlockSpec((1, 6, D_MODEL), lambda i: (_mod_row(i), 0, 0)),
            pl.BlockSpec((1, D_MODEL), lambda i: (0, 0)),
            pl.BlockSpec((1, D_MODEL, PROJ_W), lambda i: (layer, 0, 0)),
            rope_spec, rope_spec, rope_spec,
        ],
        out_specs=[pl.BlockSpec((TM, PROJ_W), lambda i: (i, 0))] + [_cache_spec()] * 4,
        out_shape=[jax.ShapeDtypeStruct((T, PROJ_W), F32)] + [_CACHE_SHAPE] * 4,
        compiler_params=_cparams(("arbitrary",)),
        name="projection",
    )(x_ctx, x_lat, mod, norm_g[None, :], w_in_bf16, *rope)


LOG2_E = 1.4426950408889634


def _exp2_rows(s):
    e = jnp.exp2(s - jnp.max(s, axis=-1, keepdims=True))
    return e, 1.0 / jnp.sum(e, axis=-1, keepdims=True)


def _attn_kernel(lam_ref, q_ref, k_ref, v_ref, *rest, n_maps, post_scale, with_cache):
    if with_cache:
        kc_ref, vc_ref, g_ref, o_ref, kt_ref, vb_ref = rest
    else:
        g_ref, o_ref, kt_ref, vb_ref = rest

    @pl.when(pl.program_id(1) == 0)
    def _():
        k = k_ref[...]
        v = v_ref[...]
        if with_cache:
            k = jnp.concatenate([kc_ref[...], k], axis=0)
            v = jnp.concatenate([vc_ref[...], v], axis=0)
        kt_ref[...] = k.T.astype(BF16)
        vb_ref[...] = v.astype(BF16)

    lane = _head_lanes()
    map_dim = HEAD_DIM // n_maps
    q = q_ref[...] * (map_dim ** -0.5 * LOG2_E)
    kt = kt_ref[...]
    vb = vb_ref[...]
    o = jnp.zeros(q.shape, F32)
    for h in range(HEADS):
        parts = []
        for j in range(n_maps):
            qm = jnp.where(_lane_range(lane, h * HEAD_DIM + j * map_dim, map_dim), q, 0.0)
            parts.append(_exp2_rows(jnp.dot(qm.astype(BF16), kt, preferred_element_type=F32)))
        w = parts[0][0] * parts[0][1]
        if n_maps == 2:
            w = w - parts[1][0] * (lam_ref[0] * parts[1][1])
        oh = jnp.dot(w.astype(BF16), vb, preferred_element_type=F32)
        o = jnp.where(_lane_range(lane, h * HEAD_DIM, HEAD_DIM), oh, o)
    if n_maps == 2:
        o = o * lax.rsqrt(_head_mean_square(o) + EPS) * g_ref[...] * post_scale
    o_ref[...] = o


def attention(p, row_blk0, cols, n_seq, seq_len, lam, norm_g, *, n_maps, post_scale, cache=None):
    nb = seq_len // TM
    kv_len = seq_len + (PAST_LEN if cache is not None else 0)
    kern = functools.partial(_attn_kernel, n_maps=n_maps, post_scale=post_scale, with_cache=cache is not None)
    kv_spec = lambda col: pl.BlockSpec((seq_len, MIX_BLK), lambda b, i: (row_blk0 + b, col))
    cache_specs = [pl.BlockSpec((PAST_LEN, MIX_BLK), lambda b, i: (b, 0))] * 2 if cache is not None else []
    return pl.pallas_call(
        kern,
        grid=(n_seq, nb),
        in_specs=[
            pl.BlockSpec(memory_space=pltpu.SMEM),
            pl.BlockSpec((TM, MIX_BLK), lambda b, i: ((row_blk0 + b) * nb + i, cols[0])),
            kv_spec(cols[1]), kv_spec(cols[2]), *cache_specs,
            pl.BlockSpec((1, MIX_BLK), lambda b, i: (0, 0)),
        ],
        out_specs=pl.BlockSpec((TM, MIX_BLK), lambda b, i: (b * nb + i, 0)),
        out_shape=jax.ShapeDtypeStruct((n_seq * seq_len, MIX_BLK), F32),
        scratch_shapes=[pltpu.VMEM((MIX_BLK, kv_len), BF16), pltpu.VMEM((kv_len, MIX_BLK), BF16)],
        compiler_params=_cparams(("arbitrary", "arbitrary")),
        name="attention_%dmap_%d" % (n_maps, seq_len),
    )(lam, p, p, p, *(cache or ()), norm_g)


def _na_slab_start(i):
    return jnp.clip(i - 1, 0, GRID_H // 4 - NA_SLAB_ROWS // 4)


def _na_kernel(q_ref, k_ref, v_ref, kc_ref, vc_ref, bias_ref, o_ref):
    i = pl.program_id(1)
    start = pl.multiple_of(_na_slab_start(i) * TM, TM)
    ks_t = k_ref[pl.ds(start, NA_SLAB), :].T.astype(BF16)
    vs = v_ref[pl.ds(start, NA_SLAB), :].astype(BF16)
    kc_t = kc_ref[...].T.astype(BF16)
    vc = vc_ref[...].astype(BF16)
    q = q_ref[...] * (HEAD_DIM ** -0.5)
    lane = _head_lanes()
    o = jnp.zeros(q.shape, F32)
    for h in range(HEADS):
        in_head = _lane_range(lane, h * HEAD_DIM, HEAD_DIM)
        qm = jnp.where(in_head, q, 0.0).astype(BF16)
        s_loc = jnp.dot(qm, ks_t, preferred_element_type=F32) + bias_ref[0, h]
        s_ctx = jnp.dot(qm, kc_t, preferred_element_type=F32)
        m = jnp.maximum(jnp.max(s_loc, axis=-1, keepdims=True), jnp.max(s_ctx, axis=-1, keepdims=True))
        e_loc = jnp.exp(s_loc - m)
        e_ctx = jnp.exp(s_ctx - m)
        den = jnp.sum(e_loc, axis=-1, keepdims=True) + jnp.sum(e_ctx, axis=-1, keepdims=True)
        oh = (jnp.dot(e_loc.astype(BF16), vs, preferred_element_type=F32)
              + jnp.dot(e_ctx.astype(BF16), vc, preferred_element_type=F32)) / den
        o = jnp.where(in_head, oh, o)
    o_ref[...] = o


def _na_bias_tables(rpb):
    n_dr, n_dc = 2 * NA_WIN_H - 1, 2 * NA_WIN_W - 1
    cq = np.arange(GRID_W)[:, None]
    ck = np.arange(GRID_W)[None, :]
    wc0 = np.clip(cq - NA_WIN_W // 2, 0, GRID_W - NA_WIN_W)
    col_ok = (ck >= wc0) & (ck < wc0 + NA_WIN_W)
    col_pick = np.clip(ck - cq + NA_WIN_W - 1, 0, n_dc - 1)[..., None] == np.arange(n_dc)
    row_pick, row_ok = [], []
    for tile in (0, 1, GRID_H // 4 - 1):
        slab0 = int(np.clip(tile - 1, 0, GRID_H // 4 - NA_SLAB_ROWS // 4)) * 4
        rq = (tile * 4 + np.arange(4))[:, None]
        rk = (slab0 + np.arange(NA_SLAB) // GRID_W)[None, :]
        wr0 = np.clip(rq - NA_WIN_H // 2, 0, GRID_H - NA_WIN_H)
        row_ok.append((rk >= wr0) & (rk < wr0 + NA_WIN_H))
        row_pick.append(np.clip(rk - rq + NA_WIN_H - 1, 0, n_dr - 1)[:, None, :] == np.arange(n_dr)[:, None])
    by_col = jnp.einsum("hab,qcb->haqc", rpb.astype(F32), jnp.asarray(col_pick, F32), precision=HIGHEST)
    by_col = jnp.tile(by_col, (1, 1, 1, NA_SLAB_ROWS))
    pick = jnp.asarray(np.stack(row_pick), F32)
    table = jnp.sum(pick[:, None, :, :, None, :] * by_col[None, :, None, :, :, :], axis=3)
    valid = np.stack(row_ok)[:, None, :, None, :] & np.tile(col_ok, (1, NA_SLAB_ROWS))[None, None, None]
    table = jnp.where(jnp.asarray(valid), table, NEG_BIG)
    return table.reshape(3, HEADS, TM, NA_SLAB)


def na_latent(p, kc, vc, bias):
    n_t = LAT_TILES_PER_SEQ
    seq_blk0 = TP // DEC_SEQ

    def bias_idx(b, i):
        return (jnp.minimum(i, 1) + i // (n_t - 1), 0, 0, 0)

    return pl.pallas_call(
        _na_kernel,
        grid=(DEC_BATCH, n_t),
        in_specs=[
            pl.BlockSpec((TM, MIX_BLK), lambda b, i: (CTX_TILES + b * n_t + i, C_CQ)),
            pl.BlockSpec((DEC_SEQ, MIX_BLK), lambda b, i: (seq_blk0 + b, C_CK)),
            pl.BlockSpec((DEC_SEQ, MIX_BLK), lambda b, i: (seq_blk0 + b, C_CV)),
            pl.BlockSpec((PAST_LEN, MIX_BLK), lambda b, i: (b, 0)),
            pl.BlockSpec((PAST_LEN, MIX_BLK), lambda b, i: (b, 0)),
            pl.BlockSpec((1, HEADS, TM, NA_SLAB), bias_idx),
        ],
        out_specs=pl.BlockSpec((TM, MIX_BLK), lambda b, i: (b * n_t + i, 0)),
        out_shape=jax.ShapeDtypeStruct((TL, MIX_BLK), F32),
        compiler_params=_cparams(("arbitrary", "arbitrary")),
        name="na_latent",
    )(p, p, p, kc, vc, bias)


MAX_EXPONENT = 80.0


def _hgrn_direction(q_ref, f_ref, v_ref, lb, st_ref, o_ref, reverse):
    n_ch = TM // B_CHUNK
    r_idx = lax.broadcasted_iota(jnp.int32, (TM, TM), 0)
    c_idx = lax.broadcasted_iota(jnp.int32, (TM, TM), 1)
    tri = (c_idx >= r_idx) if reverse else (c_idx <= r_idx)
    zq = q_ref[...]
    q = zq * jax.nn.sigmoid(zq)
    z = f_ref[...]
    logf = jnp.log(lb + (1.0 - lb) * jax.nn.sigmoid(z))
    kk = (1.0 - lb) * jax.nn.sigmoid(-z)
    b = _select_sum_left(jnp.where(tri, 1.0, 0.0).astype(BF16), logf)
    b3 = b.reshape(n_ch, B_CHUNK, MIX_BLK)
    mid = B_CHUNK // 2 if reverse else B_CHUNK // 2 - 1
    q_in = (q.reshape(b3.shape) * jnp.exp(b3 - b3[:, mid:mid + 1, :])).reshape(TM, MIX_BLK)
    q_dec = (q * jnp.exp(b)).astype(BF16)
    b_t = b.T
    kk_t = kk.T
    far = 0 if reverse else TM - 1
    b_far = b_t[:, far:far + 1]
    k_dec_t = (kk_t * jnp.exp(b_far - b_t)).astype(BF16)
    vb = v_ref[...].astype(BF16)
    st = st_ref[...]
    o_state = jnp.dot(q_dec, st.astype(BF16), preferred_element_type=F32)
    kv = jnp.dot(k_dec_t, vb, preferred_element_type=F32)
    st_ref[...] = st * jnp.exp(b_far) + jnp.where(_same_head_matrix(), kv, 0.0)
    lane = _head_lanes()
    token = lax.broadcasted_iota(jnp.int32, (1, TM), 1)
    local = lax.broadcasted_iota(jnp.int32, (HEADS * B_CHUNK, 1), 0) % B_CHUNK
    heads = [_lane_range(lane, h * HEAD_DIM, HEAD_DIM) for h in range(HEADS)]
    for c in range(n_ch):
        rows = slice(c * B_CHUNK, (c + 1) * B_CHUNK)
        ref = b_t[:, c * B_CHUNK + mid:c * B_CHUNK + mid + 1]
        k_c_t = (kk_t * jnp.exp(jnp.minimum(ref - b_t, MAX_EXPONENT))).astype(BF16)
        q_c = q_in[rows, :]
        lhs = jnp.concatenate([jnp.where(in_head, q_c, 0.0) for in_head in heads], axis=0)
        a = jnp.dot(lhs.astype(BF16), k_c_t, preferred_element_type=F32)
        t_abs = c * B_CHUNK + local
        a = jnp.where((token >= t_abs) if reverse else (token <= t_abs), a, 0.0)
        res = jnp.dot(a.astype(BF16), vb, preferred_element_type=F32)
        o_c = o_state[rows, :]
        for h, in_head in enumerate(heads):
            o_c = o_c + jnp.where(in_head, res[h * B_CHUNK:(h + 1) * B_CHUNK, :], 0.0)
        o_ref[rows, :] = o_c


def _hgrn_kernel(qf_ref, ff_ref, vf_ref, qb_ref, fb_ref, vb_ref, lb_ref, s0_ref,
                 of_ref, ob_ref, s_ref, stf_ref, stb_ref, *, has_s0):
    j = pl.program_id(1)

    @pl.when(j == 0)
    def _():
        if has_s0:
            stf_ref[...] = s0_ref[0, 0]
            stb_ref[...] = s0_ref[0, 1]
        else:
            stf_ref[...] = jnp.zeros((MIX_BLK, MIX_BLK), F32)
            stb_ref[...] = jnp.zeros((MIX_BLK, MIX_BLK), F32)

    lb = lb_ref[...]
    _hgrn_direction(qf_ref, ff_ref, vf_ref, lb[0:1], stf_ref, of_ref, False)
    _hgrn_direction(qb_ref, fb_ref, vb_ref, lb[1:2], stb_ref, ob_ref, True)

    @pl.when(j == pl.num_programs(1) - 1)
    def _():
        for d, st_ref in enumerate((stf_ref, stb_ref)):
            s = st_ref[...]
            for hd in range(HEADS):
                lo = hd * HEAD_DIM
                s_ref[0, d, hd] = s[lo:lo + HEAD_DIM, lo:lo + HEAD_DIM]


def hgrn(p, row_tile0, n_seq, seq_len, lb, s0):
    nb = seq_len // TM
    has_s0 = s0 is not None
    if s0 is None:
        s0 = jnp.zeros((1, 2, MIX_BLK, MIX_BLK), F32)

    def fwd(col):
        return pl.BlockSpec((TM, MIX_BLK), lambda s, j: (row_tile0 + s * nb + j, col))

    def bwd(col):
        return pl.BlockSpec((TM, MIX_BLK), lambda s, j: (row_tile0 + s * nb + nb - 1 - j, col))

    state_spec = pl.BlockSpec((1, 2, MIX_BLK, MIX_BLK), lambda s, j: (s if has_s0 else 0, 0, 0, 0))
    out_rows = n_seq * seq_len
    return pl.pallas_call(
        functools.partial(_hgrn_kernel, has_s0=has_s0),
        grid=(n_seq, nb),
        in_specs=[fwd(C_BQ), fwd(C_BFF), fwd(C_BV), bwd(C_BQ), bwd(C_BFB), bwd(C_BV),
                  pl.BlockSpec((2, MIX_BLK), lambda s, j: (0, 0)), state_spec],
        out_specs=[
            pl.BlockSpec((TM, MIX_BLK), lambda s, j: (s * nb + j, 0)),
            pl.BlockSpec((TM, MIX_BLK), lambda s, j: (s * nb + nb - 1 - j, 0)),
            pl.BlockSpec((1, 2, HEADS, HEAD_DIM, HEAD_DIM), lambda s, j: (s, 0, 0, 0, 0)),
        ],
        out_shape=[jax.ShapeDtypeStruct((out_rows, MIX_BLK), F32),
                   jax.ShapeDtypeStruct((out_rows, MIX_BLK), F32),
                   jax.ShapeDtypeStruct((n_seq, 2, HEADS, HEAD_DIM, HEAD_DIM), F32)],
        scratch_shapes=[pltpu.VMEM((MIX_BLK, MIX_BLK), F32), pltpu.VMEM((MIX_BLK, MIX_BLK), F32)],
        compiler_params=_cparams(("arbitrary", "arbitrary")),
        name="hgrn_%d" % seq_len,
    )(p, p, p, p, p, p, lb, s0)


def _state_to_blockdiag(s):
    eye = jnp.eye(HEADS, dtype=F32)
    full = s.astype(F32)[:, :, :, :, None, :] * eye[None, None, :, None, :, None]
    return full.reshape(s.shape[0], 2, MIX_BLK, MIX_BLK)


def _fft_kernel(u_ref, c64_ref, s64_ref, cl_ref, sl_ref, o_ref, a_ref, b_ref, *, norm):
    @pl.when(pl.program_id(1) == 0)
    def _():
        u = u_ref[...].astype(BF16)
        a_ref[...] = jnp.dot(u, c64_ref[...], preferred_element_type=F32).astype(BF16)
        b_ref[...] = jnp.dot(u, s64_ref[...], preferred_element_type=F32).astype(BF16)

    o_ref[...] = (jnp.dot(cl_ref[...], a_ref[...], preferred_element_type=F32)
                  - jnp.dot(sl_ref[...], b_ref[...], preferred_element_type=F32)) * norm


def _dft_tables(n):
    k = np.arange(n)
    ang = 2.0 * np.pi * ((k[:, None] * k[None, :]) % n) / n
    return np.cos(ang), np.sin(ang)


def _dft_constants(seq_len):
    c64, s64 = _dft_tables(HEAD_DIM)
    eye = np.eye(HEADS)
    cl, sl = _dft_tables(seq_len)
    as_bf16 = lambda a: jnp.asarray(a, F32).astype(BF16)
    return as_bf16(np.kron(eye, c64)), as_bf16(np.kron(eye, s64)), as_bf16(cl), as_bf16(sl)


def fourier_mix(p, row_blk0, n_seq, seq_len, consts):
    c64, s64, cl, sl = consts
    nb = seq_len // TM
    norm = 1.0 / math.sqrt(seq_len * HEAD_DIM)
    return pl.pallas_call(
        functools.partial(_fft_kernel, norm=norm),
        grid=(n_seq, nb),
        in_specs=[
            pl.BlockSpec((seq_len, MIX_BLK), lambda s, i: (row_blk0 + s, C_DU)),
            pl.BlockSpec((MIX_BLK, MIX_BLK), lambda s, i: (0, 0)),
            pl.BlockSpec((MIX_BLK, MIX_BLK), lambda s, i: (0, 0)),
            pl.BlockSpec((TM, seq_len), lambda s, i: (i, 0)),
            pl.BlockSpec((TM, seq_len), lambda s, i: (i, 0)),
        ],
        out_specs=pl.BlockSpec((TM, MIX_BLK), lambda s, i: (s * nb + i, 0)),
        out_shape=jax.ShapeDtypeStruct((n_seq * seq_len, MIX_BLK), F32),
        scratch_shapes=[pltpu.VMEM((seq_len, MIX_BLK), BF16), pltpu.VMEM((seq_len, MIX_BLK), BF16)],
        compiler_params=_cparams(("arbitrary", "arbitrary")),
        name="fourier_%d" % seq_len,
    )(p, c64, s64, cl, sl)


def _route(logits_t, rb):
    per = N_EXPERTS // N_GROUPS
    score = [jax.nn.sigmoid(logits_t[e:e + 1, :]) for e in range(N_EXPERTS)]
    sel = [score[e] + rb[e:e + 1, :] for e in range(N_EXPERTS)]
    gscore = []
    for g in range(N_GROUPS):
        vals = sel[g * per:(g + 1) * per]
        best = None
        for a in range(per):
            for b in range(a + 1, per):
                pair = vals[a] + vals[b]
                best = pair if best is None else jnp.maximum(best, pair)
        gscore.append(best)
    chosen = []
    for g in range(N_GROUPS):
        ok = None
        for j in range(N_GROUPS):
            if j == g:
                continue
            cond = gscore[g] > gscore[j] if j < g else gscore[g] >= gscore[j]
            ok = cond if ok is None else ok & cond
        chosen.append(ok)
    picked = []
    for e in range(N_EXPERTS):
        g = e // per
        rank = jnp.zeros_like(sel[e])
        for j in range(g * per, (g + 1) * per):
            if j == e:
                continue
            ahead = sel[j] >= sel[e] if j < e else sel[j] > sel[e]
            rank = rank + jnp.where(ahead, 1.0, 0.0)
        picked.append(chosen[g] & (rank < 2.0))
    wsum = jnp.zeros_like(score[0])
    for e in range(N_EXPERTS):
        wsum = wsum + jnp.where(picked[e], score[e], 0.0)
    bucket = jnp.zeros_like(wsum)
    w_a = jnp.zeros_like(wsum)
    w_b = jnp.zeros_like(wsum)
    for g in range(N_GROUPS):
        for n, (a, b) in enumerate(EXPERT_PAIRS):
            hit = picked[g * per + a] & picked[g * per + b]
            bucket = jnp.where(hit, float(g * len(EXPERT_PAIRS) + n), bucket)
            w_a = jnp.where(hit, score[g * per + a] / wsum, w_a)
            w_b = jnp.where(hit, score[g * per + b] / wsum, w_b)
    return bucket, w_a, w_b


def _out_kernel(*refs):
    streams, rest = refs[:12], refs[12:]
    (bg_ref, mod_ref, hg_ref, w_ref, g2_ref, rw_ref, rb_ref,
     x1_ref, h2_ref, bucket_ref, rank_ref, counts_ref, run_ref) = rest
    is_ctx = _is_ctx_tile()
    x, o_a, o_f, o_b, o_c, o_d = (jnp.where(is_ctx, streams[2 * n][...], streams[2 * n + 1][...])
                                  for n in range(6))

    @pl.when(pl.program_id(0) == 0)
    def _():
        run_ref[...] = jnp.zeros(run_ref.shape, F32)

    mod = mod_ref[0]
    hb = o_f + o_b
    zg = bg_ref[...]
    hb = hb * lax.rsqrt(_head_mean_square(hb) + EPS) * hg_ref[...] * (zg * jax.nn.sigmoid(zg))
    parts = (o_a, hb, o_c, o_d)
    mixed = jnp.zeros((TM, D_MODEL), F32)
    for n, part in enumerate(parts):
        mixed = mixed + jnp.dot(part.astype(BF16), w_ref[0, n * MIX_BLK:(n + 1) * MIX_BLK, :],
                                preferred_element_type=F32)
    x1 = x + mod[2:3] * mixed
    x1_ref[...] = x1
    ms = jnp.mean(x1 * x1, axis=-1, keepdims=True)
    h2 = x1 * lax.rsqrt(ms + EPS) * g2_ref[...] * (1.0 + mod[4:5]) + mod[3:4]
    rw = rw_ref[...]
    r = sum(jnp.dot(piece, rw, preferred_element_type=F32) for piece in _bf16_pieces(h2, 2))
    bucket, w_a, w_b = _route((r[:, :LANES] + r[:, LANES:]).T, rb_ref[...])
    h2_ref[:, :D_MODEL] = h2
    h2_ref[:, D_MODEL:] = jnp.concatenate([w_a, w_b, jnp.zeros((LANES - 2, TM), F32)], axis=0).T
    onehot = jnp.where(lax.broadcasted_iota(jnp.int32, (BUCKET_ROWS, 1), 0).astype(F32) == bucket, 1.0, 0.0)
    s_idx = lax.broadcasted_iota(jnp.int32, (TM, TM), 0)
    t_idx = lax.broadcasted_iota(jnp.int32, (TM, TM), 1)
    prefix = jnp.dot(onehot.astype(BF16), jnp.where(s_idx <= t_idx, 1.0, 0.0).astype(BF16),
                     preferred_element_type=F32)
    run = run_ref[...]
    rank = jnp.sum(onehot * (prefix - 1.0 + run[:, 0:1]), axis=0, keepdims=True)
    run = run + jnp.sum(onehot, axis=1, keepdims=True)
    run_ref[...] = run
    bucket_ref[...] = bucket.astype(jnp.int32)
    rank_ref[...] = rank.astype(jnp.int32)
    counts_ref[...] = run


def out_and_route(x_pair, x_specs, mixer_pairs, p, mod, layer, hgrn_g, w_out_bf16, norm2_g, router_pieces,
                  router_b):
    tile = lambda w: pl.BlockSpec((TM, w), lambda i: (i, 0))
    full = lambda r, c: pl.BlockSpec((r, c), lambda i: (0, 0))
    stream_specs = list(x_specs)
    stream_args = list(x_pair)
    for o_ctx, o_lat in mixer_pairs:
        stream_specs += [pl.BlockSpec((TM, MIX_BLK), _ctx_tile), pl.BlockSpec((TM, MIX_BLK), _lat_tile)]
        stream_args += [o_ctx, o_lat]
    return pl.pallas_call(
        _out_kernel,
        grid=(N_TILES,),
        in_specs=stream_specs + [
            pl.BlockSpec((TM, MIX_BLK), lambda i: (i, C_BG)),
            pl.BlockSpec((1, 6, D_MODEL), lambda i: (_mod_row(i), 0, 0)),
            full(1, MIX_BLK), pl.BlockSpec((1, D_MODEL, D_MODEL), lambda i: (layer, 0, 0)), full(1, D_MODEL),
            full(D_MODEL, 2 * LANES), full(N_EXPERTS, 1),
        ],
        out_specs=[tile(D_MODEL), tile(ROW_W), pl.BlockSpec((1, TM), lambda i: (0, i)),
                   pl.BlockSpec((1, TM), lambda i: (0, i)), full(BUCKET_ROWS, LANES)],
        out_shape=[jax.ShapeDtypeStruct((T, D_MODEL), F32),
                   jax.ShapeDtypeStruct((T, ROW_W), F32),
                   jax.ShapeDtypeStruct((1, T), jnp.int32),
                   jax.ShapeDtypeStruct((1, T), jnp.int32),
                   jax.ShapeDtypeStruct((BUCKET_ROWS, LANES), F32)],
        scratch_shapes=[pltpu.VMEM((BUCKET_ROWS, LANES), F32)],
        compiler_params=_cparams(("arbitrary",)),
        name="out_and_route",
    )(*stream_args, p, mod, jnp.tile(hgrn_g, HEADS)[None, :], w_out_bf16,
      norm2_g[None, :], router_pieces, router_b[:, None])


def _router_pieces(router_w):
    hi, lo = _bf16_pieces(router_w.astype(F32), 2)
    pad = lambda a: jnp.pad(a, ((0, 0), (0, LANES - N_EXPERTS)))
    return jnp.concatenate([pad(hi), pad(lo)], axis=1)


def routing_plan(bucket, rank, counts):
    counts = counts[:N_BUCKETS, 0].astype(jnp.int32)
    n_tiles = (counts + TM - 1) // TM
    tile_end = jnp.cumsum(n_tiles)
    tile_start = tile_end - n_tiles
    buckets = jnp.arange(N_BUCKETS, dtype=jnp.int32)
    start_of_token = jnp.sum(jnp.where(bucket[0][:, None] == buckets[None, :], tile_start[None, :], 0), axis=1)
    dest = start_of_token * TM + rank[0]
    tiles = jnp.arange(MAX_TILES, dtype=jnp.int32)
    valid = tiles < tile_end[-1]
    tile_bucket = jnp.sum((jnp.minimum(tiles, tile_end[-1] - 1)[:, None] >= tile_end[None, :]).astype(jnp.int32), axis=1)
    pair_a = np.array([a for a, _ in EXPERT_PAIRS], np.int32)
    pair_b = np.array([b for _, b in EXPERT_PAIRS], np.int32)
    per = N_EXPERTS // N_GROUPS
    exp_a = jnp.asarray((np.arange(N_BUCKETS) // len(EXPERT_PAIRS)) * per + np.tile(pair_a, N_GROUPS), jnp.int32)
    exp_b = jnp.asarray((np.arange(N_BUCKETS) // len(EXPERT_PAIRS)) * per + np.tile(pair_b, N_GROUPS), jnp.int32)
    pick = tile_bucket[:, None] == buckets[None, :]
    tile_a = jnp.sum(jnp.where(pick, exp_a[None, :], 0), axis=1)
    tile_b = jnp.sum(jnp.where(pick, exp_b[None, :], 0), axis=1)
    return dest.astype(jnp.int32), tile_a, tile_b, valid.astype(jnp.int32)


def _row_copy(src, src_row, dst, dst_row, sem):
    return pltpu.make_async_copy(src.at[pl.ds(src_row, 1), :], dst.at[pl.ds(dst_row, 1), :], sem)


def _scatter_kernel(dest_ref, h_ref, init_ref, o_ref, sem):
    del init_ref
    base = pl.program_id(0) * TM

    def send(r, carry):
        _row_copy(h_ref, r, o_ref, dest_ref[base + r], sem).start()
        return carry

    lax.fori_loop(0, TM, send, 0, unroll=ROW_DMA_UNROLL)
    pltpu.make_async_copy(h_ref, o_ref.at[pl.ds(0, TM), :], sem).wait()


def scatter_to_slots(h2, dest):
    return pl.pallas_call(
        _scatter_kernel,
        grid_spec=pltpu.PrefetchScalarGridSpec(
            num_scalar_prefetch=1,
            grid=(N_TILES,),
            in_specs=[pl.BlockSpec((TM, ROW_W), lambda i, d: (i, 0)),
                      pl.BlockSpec(memory_space=pl.ANY)],
            out_specs=pl.BlockSpec(memory_space=pl.ANY),
            scratch_shapes=[pltpu.SemaphoreType.DMA(())],
        ),
        out_shape=jax.ShapeDtypeStruct((N_SLOTS, ROW_W), F32),
        input_output_aliases={2: 0},
        compiler_params=_cparams(("arbitrary",)),
        name="scatter_to_slots",
    )(dest, h2, jnp.zeros((N_SLOTS, ROW_W), F32))


def _moe_kernel(ta_ref, tb_ref, valid_ref, h_ref, wga_ref, wua_ref, wda_ref, wgb_ref, wub_ref, wdb_ref,
                o_ref):
    del ta_ref, tb_ref
    i = pl.program_id(0)

    @pl.when(valid_ref[i] == 1)
    def _():
        x = h_ref[:, :D_MODEL].astype(BF16)
        gates = h_ref[:, D_MODEL:]
        y = jnp.zeros((TM, D_MODEL), F32)
        for n, (wg, wu, wd) in enumerate(((wga_ref, wua_ref, wda_ref), (wgb_ref, wub_ref, wdb_ref))):
            a = jnp.dot(x, wg[0, 0].astype(BF16), preferred_element_type=F32)
            u = jnp.dot(x, wu[0, 0].astype(BF16), preferred_element_type=F32)
            z = a * jax.nn.sigmoid(a) * u * gates[:, n:n + 1]
            y = y + jnp.dot(z.astype(BF16), wd[0, 0].astype(BF16), preferred_element_type=F32)
        o_ref[...] = y

    @pl.when(valid_ref[i] == 0)
    def _():
        o_ref[...] = jnp.zeros((TM, D_MODEL), F32)


def moe(h_slots, tile_a, tile_b, valid, layer, wg, wu, wd):
    up_a = pl.BlockSpec((1, 1, D_MODEL, D_EXPERT), lambda i, ta, tb, v: (layer, ta[i], 0, 0))
    up_b = pl.BlockSpec((1, 1, D_MODEL, D_EXPERT), lambda i, ta, tb, v: (layer, tb[i], 0, 0))
    down_a = pl.BlockSpec((1, 1, D_EXPERT, D_MODEL), lambda i, ta, tb, v: (layer, ta[i], 0, 0))
    down_b = pl.BlockSpec((1, 1, D_EXPERT, D_MODEL), lambda i, ta, tb, v: (layer, tb[i], 0, 0))
    return pl.pallas_call(
        _moe_kernel,
        grid_spec=pltpu.PrefetchScalarGridSpec(
            num_scalar_prefetch=3,
            grid=(MAX_TILES,),
            in_specs=[pl.BlockSpec((TM, ROW_W), lambda i, ta, tb, v: (i, 0)),
                      up_a, up_a, down_a, up_b, up_b, down_b],
            out_specs=pl.BlockSpec((TM, D_MODEL), lambda i, ta, tb, v: (i, 0)),
        ),
        out_shape=jax.ShapeDtypeStruct((N_SLOTS, D_MODEL), F32),
        compiler_params=_cparams(("arbitrary",)),
        name="moe",
    )(tile_a, tile_b, valid, h_slots, wg, wu, wd, wg, wu, wd)


def _gather_tile(dest_ref, y_ref, buf_ref, sem, tile, slot):
    def fetch(r, carry):
        pltpu.make_async_copy(y_ref.at[pl.ds(dest_ref[tile * TM + r], 1), :],
                              buf_ref.at[slot, pl.ds(r, 1), :], sem.at[slot]).start()
        return carry

    lax.fori_loop(0, TM, fetch, 0, unroll=ROW_DMA_UNROLL)


def _moe_residual(dest_ref, y_ref, buf_ref, sem, x1_ref, mod_ref):
    i = pl.program_id(0)
    slot = i % 2

    @pl.when(i == 0)
    def _():
        _gather_tile(dest_ref, y_ref, buf_ref, sem, 0, 0)

    @pl.when(i + 1 < pl.num_programs(0))
    def _():
        _gather_tile(dest_ref, y_ref, buf_ref, sem, i + 1, 1 - slot)

    pltpu.make_async_copy(y_ref.at[pl.ds(0, TM), :], buf_ref.at[slot], sem.at[slot]).wait()
    return x1_ref[...] + mod_ref[0][5:6] * buf_ref[slot]


def _final_kernel(dest_ref, y_ref, x1_ref, mod_ref, g_ref, oc_ref, ol_ref, buf_ref, sem):
    x2 = _moe_residual(dest_ref, y_ref, buf_ref, sem, x1_ref, mod_ref)
    ms = jnp.mean(x2 * x2, axis=-1, keepdims=True)
    y = x2 * lax.rsqrt(ms + EPS) * g_ref[...]

    @pl.when(_is_ctx_tile())
    def _():
        oc_ref[...] = y

    @pl.when(jnp.logical_not(_is_ctx_tile()))
    def _():
        ol_ref[...] = y


_GATHER_SCRATCH = [pltpu.VMEM((2, TM, D_MODEL), F32), pltpu.SemaphoreType.DMA((2,))]


def final_norm(dest, y_slots, x1, mod, final_g):
    return pl.pallas_call(
        _final_kernel,
        grid_spec=pltpu.PrefetchScalarGridSpec(
            num_scalar_prefetch=1,
            grid=(N_TILES,),
            in_specs=[pl.BlockSpec(memory_space=pl.ANY),
                      pl.BlockSpec((TM, D_MODEL), lambda i, d: (i, 0)),
                      pl.BlockSpec((1, 6, D_MODEL), lambda i, d: (_mod_row(i), 0, 0)),
                      pl.BlockSpec((1, D_MODEL), lambda i, d: (0, 0))],
            out_specs=[pl.BlockSpec((TM, D_MODEL), _ctx_tile), pl.BlockSpec((TM, D_MODEL), _lat_tile)],
            scratch_shapes=_GATHER_SCRATCH,
        ),
        out_shape=[jax.ShapeDtypeStruct((TP, D_MODEL), F32), jax.ShapeDtypeStruct((TL, D_MODEL), F32)],
        compiler_params=_cparams(("arbitrary",)),
        name="final_norm",
    )(dest, y_slots, x1, mod, final_g[None, :])


def _heads_to_lanes(t):
    b, h, l, d = t.shape
    return t.transpose(0, 2, 1, 3).reshape(b * l, h * d)


def kernel(x_prompt, x_sample, cache_diff_k, cache_diff_v, cache_na_k, cache_na_v, state_hgrn, c, c_ctx,
           norm1_g, norm2_g, ada_w, ada_b, w_in, w_out, diff_lambda, diff_subln_g, hgrn_lb_logits,
           hgrn_norm_g, na_rpb, router_w, router_b, moe_w_gate, moe_w_up, moe_w_down, final_norm_g):
    assert SEQ == TM and PAST_LEN == TM and DEC_SEQ % TM == 0 and TP % DEC_SEQ == 0
    x_pair = (x_prompt.reshape(TP, D_MODEL), x_sample.reshape(TL, D_MODEL))
    x_specs = (pl.BlockSpec((TM, D_MODEL), _ctx_tile), pl.BlockSpec((TM, D_MODEL), _lat_tile))
    w_in_bf16 = w_in.astype(BF16)
    w_out_bf16 = w_out.astype(BF16)
    router_pieces = _router_pieces(router_w)
    mods = modulation(jnp.concatenate([c_ctx[None, :], c], axis=0), ada_w, ada_b)
    mods = mods.reshape(DEPTH, 3, 6, D_MODEL)
    lb_sm = jax.nn.softmax(hgrn_lb_logits.astype(F32), axis=0)
    lb_all = jnp.cumsum(lb_sm, axis=0) - lb_sm[0:1]
    rope = _rope_tables()
    dft_ctx = _dft_constants(SEQ)
    dft_lat = _dft_constants(DEC_SEQ)
    lat_blk0 = TP // DEC_SEQ
    caches = [[], [], [], [], []]
    moe_state = None
    for l in range(DEPTH):
        if moe_state is None:
            p, *new_kv = projection(*x_pair, mods[l], l, norm1_g[l], w_in_bf16, rope)
        else:
            x, p, *new_kv = projection_after_moe(*moe_state, mods[l - 1], mods[l], l, norm1_g[l], w_in_bf16, rope)
            x_pair = (x, x)
            x_specs = (pl.BlockSpec((TM, D_MODEL), _ctx_tile),
                       pl.BlockSpec((TM, D_MODEL), lambda i: (jnp.maximum(i, CTX_TILES), 0)))

        lq = diff_lambda[l].astype(F32)
        lam_init = 0.8 - 0.6 * math.exp(-0.3 * l)
        lam = (jnp.exp(jnp.sum(lq[0] * lq[1])) - jnp.exp(jnp.sum(lq[2] * lq[3])) + lam_init).reshape(1)
        subln = jnp.tile(diff_subln_g[l], HEADS)[None, :]
        diff = functools.partial(attention, p, cols=(C_AQ, C_AK, C_AV), lam=lam, norm_g=subln, n_maps=2,
                                 post_scale=1.0 - lam_init)
        oa_ctx = diff(row_blk0=0, n_seq=BATCH, seq_len=SEQ)
        oa_lat = diff(row_blk0=lat_blk0, n_seq=DEC_BATCH, seq_len=DEC_SEQ,
                      cache=(_heads_to_lanes(cache_diff_k[:, l]), _heads_to_lanes(cache_diff_v[:, l])))

        of_ctx, ob_ctx, st_ctx = hgrn(p, 0, BATCH, SEQ, lb_all[l], None)
        of_lat, ob_lat, _ = hgrn(p, CTX_TILES, DEC_BATCH, DEC_SEQ, lb_all[l],
                                 _state_to_blockdiag(state_hgrn[:, l]))

        oc_ctx = attention(p, 0, (C_CQ, C_CK, C_CV), BATCH, SEQ, lam, subln, n_maps=1, post_scale=1.0)
        oc_lat = na_latent(p, _heads_to_lanes(cache_na_k[:, l]), _heads_to_lanes(cache_na_v[:, l]),
                           _na_bias_tables(na_rpb[l]))

        od_ctx = fourier_mix(p, 0, BATCH, SEQ, dft_ctx)
        od_lat = fourier_mix(p, lat_blk0, DEC_BATCH, DEC_SEQ, dft_lat)

        x1, h2, bucket, rank, counts = out_and_route(
            x_pair, x_specs, ((oa_ctx, oa_lat), (of_ctx, of_lat), (ob_ctx, ob_lat), (oc_ctx, oc_lat),
                              (od_ctx, od_lat)),
            p, mods[l], l, hgrn_norm_g[l], w_out_bf16, norm2_g[l], router_pieces, router_b)
        dest, tile_a, tile_b, valid = routing_plan(bucket, rank, counts)
        y_slots = moe(scatter_to_slots(h2, dest), tile_a, tile_b, valid, l, moe_w_gate, moe_w_up, moe_w_down)
        moe_state = (dest, y_slots, x1)

        for cache, new in zip(caches, new_kv + [st_ctx]):
            cache.append(new)
    y_prompt, y_sample = final_norm(*moe_state, mods[DEPTH - 1], final_norm_g)
    return (y_prompt.reshape(BATCH, SEQ, D_MODEL), y_sample.reshape(DEC_BATCH, DEC_SEQ, D_MODEL)) + tuple(
        jnp.stack(cc, axis=1) for cc in caches)
```

```python
import functools
import math

import numpy as np
import jax
import jax.numpy as jnp
from jax import lax
from jax.experimental import pallas as pl
from jax.experimental.pallas import tpu as pltpu

F32 = jnp.float32
BF16 = jnp.bfloat16
HIGHEST = lax.Precision.HIGHEST

D_MODEL = 1024
BATCH = 16
SEQ = 256
DEPTH = 2
DEC_BATCH = 2
DEC_SEQ = 2048
PAST_LEN = 256
GRID_W = 64
GRID_H = DEC_SEQ // GRID_W
EPS = 1e-6
NEG_BIG = -1e30
HEADS = 4
HEAD_DIM = 64
MIX_BLK = HEADS * HEAD_DIM
A_DIM = 32
ROPE_BASE = 10000.0
B_CHUNK = 32
NA_WIN_H = 8
NA_WIN_W = 16
N_EXPERTS = 16
N_GROUPS = 4
D_EXPERT = 512
PROJ_W = 12 * MIX_BLK
TP = BATCH * SEQ
TL = DEC_BATCH * DEC_SEQ
T = TP + TL
TM = 256
N_TILES = T // TM
CTX_TILES = TP // TM
LAT_TILES_PER_SEQ = DEC_SEQ // TM
(C_AQ, C_AK, C_AV, C_BQ, C_BFF, C_BFB, C_BV, C_BG, C_CQ, C_CK, C_CV, C_DU) = range(12)
NA_SLAB_ROWS = 12
NA_SLAB = NA_SLAB_ROWS * GRID_W
LANES = 128
ROW_W = D_MODEL + LANES
EXPERT_PAIRS = ((0, 1), (0, 2), (0, 3), (1, 3), (2, 3), (2, 1))
N_BUCKETS = N_GROUPS * len(EXPERT_PAIRS)
BUCKET_ROWS = 32
TM_MOE = 384
MAX_TILES = -(-T // TM_MOE) + N_BUCKETS
N_SLOTS = MAX_TILES * TM_MOE
VMEM_LIMIT = 56 * 1024 * 1024


def _cparams(sem):
    return pltpu.CompilerParams(dimension_semantics=sem, vmem_limit_bytes=VMEM_LIMIT)


def _head_lanes(width=MIX_BLK):
    return lax.broadcasted_iota(jnp.int32, (1, width), 1)


def _lane_range(lane, lo, n):
    return (lane >= lo) & (lane < lo + n)


def _same_head_matrix():
    r = lax.broadcasted_iota(jnp.int32, (MIX_BLK, MIX_BLK), 0) // HEAD_DIM
    c = lax.broadcasted_iota(jnp.int32, (MIX_BLK, MIX_BLK), 1) // HEAD_DIM
    return r == c


def _bf16_pieces(x, n):
    pieces = []
    for _ in range(n):
        piece = x.astype(BF16)
        pieces.append(piece)
        x = x - piece.astype(F32)
    return pieces


def _select_sum_left(onehot_bf16, x):
    return sum(jnp.dot(onehot_bf16, piece, preferred_element_type=F32) for piece in _bf16_pieces(x, 3))


def _select_sum_right(x, onehot_bf16):
    return sum(jnp.dot(piece, onehot_bf16, preferred_element_type=F32) for piece in _bf16_pieces(x, 3))


def _head_mean_square(o):
    ones = jnp.where(_same_head_matrix(), 1.0, 0.0).astype(BF16)
    return _select_sum_right(o * o, ones) * (1.0 / HEAD_DIM)


def _mod_row(i):
    return jnp.where(i < CTX_TILES, 0, 1 + (i - CTX_TILES) // LAT_TILES_PER_SEQ)


def _mod_kernel(c_ref, w_ref, b_ref, o_ref):
    w = w_ref[0]
    for r in range(3):
        c = c_ref[r]
        s = c * jax.nn.sigmoid(c)
        o_ref[0, r:r + 1, :] = jnp.sum(s * w, axis=0, keepdims=True) + b_ref[0]


def modulation(c_rows, ada_w, ada_b):
    nt = 768
    n_out = 6 * D_MODEL
    return pl.pallas_call(
        _mod_kernel,
        grid=(DEPTH, n_out // nt),
        in_specs=[
            pl.BlockSpec((3, D_MODEL, 1), lambda l, j: (0, 0, 0)),
            pl.BlockSpec((1, D_MODEL, nt), lambda l, j: (l, 0, j)),
            pl.BlockSpec((1, 1, nt), lambda l, j: (l, 0, j)),
        ],
        out_specs=pl.BlockSpec((1, 3, nt), lambda l, j: (l, 0, j)),
        out_shape=jax.ShapeDtypeStruct((DEPTH, 3, n_out), F32),
        compiler_params=_cparams(("arbitrary", "arbitrary")),
        name="modulation",
    )(c_rows[:, :, None], ada_w, ada_b[:, None, :])


def _is_ctx_tile():
    return pl.program_id(0) < CTX_TILES


def _ctx_tile(i, *_):
    return (jnp.minimum(i, CTX_TILES - 1), 0)


def _lat_tile(i, *_):
    return (jnp.maximum(i - CTX_TILES, 0), 0)


def _proj_kernel(xc_ref, xl_ref, mod_ref, g_ref, w_ref, cos_ref, sa_ref, sb_ref, o_ref, *cache_refs):
    x = jnp.where(_is_ctx_tile(), xc_ref[...], xl_ref[...])
    _proj_body(x, mod_ref, g_ref, w_ref, cos_ref, sa_ref, sb_ref, o_ref, cache_refs)


def _proj_after_moe_kernel(dest_ref, y_ref, x1_ref, modp_ref, mod_ref, g_ref, w_ref, cos_ref, sa_ref, sb_ref,
                           *rest):
    x2_ref, o_ref = rest[4:6]
    cache_refs, (buf_ref, sem) = rest[6:10], rest[10:]
    x2 = _moe_residual(dest_ref, y_ref, buf_ref, sem, x1_ref, modp_ref)
    x2_ref[...] = x2
    _proj_body(x2, mod_ref, g_ref, w_ref, cos_ref, sa_ref, sb_ref, o_ref, cache_refs)


def _proj_body(x, mod_ref, g_ref, w_ref, cos_ref, sa_ref, sb_ref, o_ref, cache_refs):
    ms = jnp.mean(x * x, axis=-1, keepdims=True)
    mod = mod_ref[0]
    h = x * lax.rsqrt(ms + EPS) * g_ref[...] * (1.0 + mod[1:2]) + mod[0:1]
    p = jnp.dot(h.astype(BF16), w_ref[0], preferred_element_type=F32)
    t = p[:, :2 * MIX_BLK]
    o_ref[:, :2 * MIX_BLK] = (t * cos_ref[...] + pltpu.roll(t, 1, 1) * sa_ref[...]
                              + pltpu.roll(t, 2 * MIX_BLK - 1, 1) * sb_ref[...])
    o_ref[:, 2 * MIX_BLK:] = p[:, 2 * MIX_BLK:]

    @pl.when(_is_ctx_tile())
    def _():
        for ref, col in zip(cache_refs, (C_AK, C_AV, C_CK, C_CV)):
            for hd in range(HEADS):
                lo = col * MIX_BLK + hd * HEAD_DIM
                ref[0, 0, hd] = p[:, lo:lo + HEAD_DIM]
            if ref.shape[1] > 1:
                ref[0, 1:] = jnp.zeros((ref.shape[1] - 1,) + tuple(ref.shape[2:]), F32)


_CACHE_SHAPE = jax.ShapeDtypeStruct((BATCH, DEPTH, HEADS, SEQ, HEAD_DIM), F32)


def _cache_spec(layer):
    n_layers = DEPTH if layer == 0 else 1
    return pl.BlockSpec((1, n_layers, HEADS, SEQ, HEAD_DIM),
                        lambda i, *_: (jnp.minimum(i, CTX_TILES - 1), layer, 0, 0, 0))


def _rope_tables():
    nf = A_DIM // 4
    freqs = ROPE_BASE ** (-jnp.arange(nf, dtype=F32) / nf)
    pos = jnp.arange(DEC_SEQ)
    row = (pos // GRID_W).astype(F32)
    col = (pos % GRID_W).astype(F32)
    ang = jnp.concatenate([row[:, None] * freqs, col[:, None] * freqs], axis=-1)
    cos = jnp.repeat(jnp.cos(ang), 2, axis=-1)
    sin = jnp.repeat(jnp.sin(ang), 2, axis=-1)
    odd = (jnp.arange(A_DIM) % 2 == 1)[None, :]
    sin_from_left = jnp.where(odd, sin, 0.0)
    sin_from_right = jnp.where(odd, 0.0, -sin)
    reps = 2 * MIX_BLK // A_DIM
    ident = (jnp.ones((TM, 2 * MIX_BLK), F32), jnp.zeros((TM, 2 * MIX_BLK), F32))
    return (jnp.concatenate([jnp.tile(cos, (1, reps)), ident[0]], axis=0),
            jnp.concatenate([jnp.tile(sin_from_left, (1, reps)), ident[1]], axis=0),
            jnp.concatenate([jnp.tile(sin_from_right, (1, reps)), ident[1]], axis=0))


def _rope_block(i):
    return (jnp.where(i < CTX_TILES, LAT_TILES_PER_SEQ, (i - CTX_TILES) % LAT_TILES_PER_SEQ), 0)


def projection_after_moe(dest, y_slots, x1, mod_prev, mod, layer, norm_g, w_in_bf16, rope, caches):
    rope_spec = pl.BlockSpec((TM, 2 * MIX_BLK), lambda i, d: _rope_block(i))
    mod_spec = pl.BlockSpec((1, 6, D_MODEL), lambda i, d: (_mod_row(i), 0, 0))
    n_in = 10
    return pl.pallas_call(
        _proj_after_moe_kernel,
        grid_spec=pltpu.PrefetchScalarGridSpec(
            num_scalar_prefetch=1,
            grid=(N_TILES,),
            in_specs=[pl.BlockSpec(memory_space=pl.ANY),
                      pl.BlockSpec((TM, D_MODEL), lambda i, d: (i, 0)),
                      mod_spec, mod_spec,
                      pl.BlockSpec((1, D_MODEL), lambda i, d: (0, 0)),
                      pl.BlockSpec((1, D_MODEL, PROJ_W), lambda i, d: (layer, 0, 0)),
                      rope_spec, rope_spec, rope_spec] + [pl.BlockSpec(memory_space=pl.ANY)] * 4,
            out_specs=[pl.BlockSpec((TM, D_MODEL), lambda i, d: (i, 0)),
                       pl.BlockSpec((TM, PROJ_W), lambda i, d: (i, 0))] + [_cache_spec(layer)] * 4,
            scratch_shapes=_GATHER_SCRATCH,
        ),
        out_shape=[jax.ShapeDtypeStruct((T, D_MODEL), F32), jax.ShapeDtypeStruct((T, PROJ_W), F32)]
        + [_CACHE_SHAPE] * 4,
        input_output_aliases={n_in + n: 2 + n for n in range(4)},
        compiler_params=_cparams(("arbitrary",)),
        name="projection_after_moe",
    )(dest, y_slots, x1, mod_prev, mod, norm_g[None, :], w_in_bf16, *rope, *caches)


def projection(x_ctx, x_lat, mod, norm_g, w_in_bf16, rope):
    layer = 0
    rope_spec = pl.BlockSpec((TM, 2 * MIX_BLK), _rope_block)
    return pl.pallas_call(
        _proj_kernel,
        grid=(N_TILES,),
        in_specs=[
            pl.BlockSpec((TM, D_MODEL), _ctx_tile),
            pl.BlockSpec((TM, D_MODEL), _lat_tile),
            pl.BlockSpec((1, 6, D_MODEL), lambda i: (_mod_row(i), 0, 0)),
            pl.BlockSpec((1, D_MODEL), lambda i: (0, 0)),
            pl.BlockSpec((1, D_MODEL, PROJ_W), lambda i: (layer, 0, 0)),
            rope_spec, rope_spec, rope_spec,
        ],
        out_specs=[pl.BlockSpec((TM, PROJ_W), lambda i: (i, 0))] + [_cache_spec(layer)] * 4,
        out_shape=[jax.ShapeDtypeStruct((T, PROJ_W), F32)] + [_CACHE_SHAPE] * 4,
        compiler_params=_cparams(("arbitrary",)),
        name="projection",
    )(x_ctx, x_lat, mod, norm_g[None, :], w_in_bf16, *rope)


LOG2_E = 1.4426950408889634


def _exp2_rows(s):
    e = jnp.exp2(s - jnp.max(s, axis=-1, keepdims=True))
    return e, 1.0 / jnp.sum(e, axis=-1, keepdims=True)


def _attn_kernel(lam_ref, q_ref, k_ref, v_ref, *rest, n_maps, post_scale, with_cache):
    if with_cache:
        kc_ref, vc_ref, g_ref, o_ref, kt_ref, vb_ref = rest
    else:
        g_ref, o_ref, kt_ref, vb_ref = rest

    @pl.when(pl.program_id(1) == 0)
    def _():
        k = k_ref[...]
        v = v_ref[...]
        if with_cache:
            k = jnp.concatenate([kc_ref[...], k], axis=0)
            v = jnp.concatenate([vc_ref[...], v], axis=0)
        kt_ref[...] = k.T.astype(BF16)
        vb_ref[...] = v.astype(BF16)

    lane = _head_lanes()
    map_dim = HEAD_DIM // n_maps
    q = q_ref[...] * (map_dim ** -0.5 * LOG2_E)
    kt = kt_ref[...]
    vb = vb_ref[...]
    o = jnp.zeros(q.shape, F32)
    for h in range(HEADS):
        parts = []
        for j in range(n_maps):
            qm = jnp.where(_lane_range(lane, h * HEAD_DIM + j * map_dim, map_dim), q, 0.0)
            parts.append(_exp2_rows(jnp.dot(qm.astype(BF16), kt, preferred_element_type=F32)))
        w = parts[0][0] * parts[0][1]
        if n_maps == 2:
            w = w - parts[1][0] * (lam_ref[0] * parts[1][1])
        oh = jnp.dot(w.astype(BF16), vb, preferred_element_type=F32)
        o = jnp.where(_lane_range(lane, h * HEAD_DIM, HEAD_DIM), oh, o)
    if n_maps == 2:
        o = o * lax.rsqrt(_head_mean_square(o) + EPS) * g_ref[...] * post_scale
    o_ref[...] = o


def attention(p, row_blk0, cols, n_seq, seq_len, lam, norm_g, *, n_maps, post_scale, cache=None):
    nb = seq_len // TM
    kv_len = seq_len + (PAST_LEN if cache is not None else 0)
    kern = functools.partial(_attn_kernel, n_maps=n_maps, post_scale=post_scale, with_cache=cache is not None)
    kv_spec = lambda col: pl.BlockSpec((seq_len, MIX_BLK), lambda b, i: (row_blk0 + b, col))
    cache_specs = [pl.BlockSpec((PAST_LEN, MIX_BLK), lambda b, i: (b, 0))] * 2 if cache is not None else []
    return pl.pallas_call(
        kern,
        grid=(n_seq, nb),
        in_specs=[
            pl.BlockSpec(memory_space=pltpu.SMEM),
            pl.BlockSpec((TM, MIX_BLK), lambda b, i: ((row_blk0 + b) * nb + i, cols[0])),
            kv_spec(cols[1]), kv_spec(cols[2]), *cache_specs,
            pl.BlockSpec((1, MIX_BLK), lambda b, i: (0, 0)),
        ],
        out_specs=pl.BlockSpec((TM, MIX_BLK), lambda b, i: (b * nb + i, 0)),
        out_shape=jax.ShapeDtypeStruct((n_seq * seq_len, MIX_BLK), F32),
        scratch_shapes=[pltpu.VMEM((MIX_BLK, kv_len), BF16), pltpu.VMEM((kv_len, MIX_BLK), BF16)],
        compiler_params=_cparams(("arbitrary", "arbitrary")),
        name="attention_%dmap_%d" % (n_maps, seq_len),
    )(lam, p, p, p, *(cache or ()), norm_g)


def _na_slab_start(i):
    return jnp.clip(i - 1, 0, GRID_H // 4 - NA_SLAB_ROWS // 4)


def _na_kernel(q_ref, k_ref, v_ref, kc_ref, vc_ref, bias_ref, o_ref):
    i = pl.program_id(1)
    start = pl.multiple_of(_na_slab_start(i) * TM, TM)
    ks_t = k_ref[pl.ds(start, NA_SLAB), :].T.astype(BF16)
    vs = v_ref[pl.ds(start, NA_SLAB), :].astype(BF16)
    kc_t = kc_ref[...].T.astype(BF16)
    vc = vc_ref[...].astype(BF16)
    q = q_ref[...] * (HEAD_DIM ** -0.5)
    lane = _head_lanes()
    o = jnp.zeros(q.shape, F32)
    for h in range(HEADS):
        in_head = _lane_range(lane, h * HEAD_DIM, HEAD_DIM)
        qm = jnp.where(in_head, q, 0.0).astype(BF16)
        s_loc = jnp.dot(qm, ks_t, preferred_element_type=F32) + bias_ref[0, h]
        s_ctx = jnp.dot(qm, kc_t, preferred_element_type=F32)
        m = jnp.maximum(jnp.max(s_loc, axis=-1, keepdims=True), jnp.max(s_ctx, axis=-1, keepdims=True))
        e_loc = jnp.exp(s_loc - m)
        e_ctx = jnp.exp(s_ctx - m)
        den = jnp.sum(e_loc, axis=-1, keepdims=True) + jnp.sum(e_ctx, axis=-1, keepdims=True)
        oh = (jnp.dot(e_loc.astype(BF16), vs, preferred_element_type=F32)
              + jnp.dot(e_ctx.astype(BF16), vc, preferred_element_type=F32)) / den
        o = jnp.where(in_head, oh, o)
    o_ref[...] = o


def _na_bias_tables(rpb):
    n_dr, n_dc = 2 * NA_WIN_H - 1, 2 * NA_WIN_W - 1
    cq = np.arange(GRID_W)[:, None]
    ck = np.arange(GRID_W)[None, :]
    wc0 = np.clip(cq - NA_WIN_W // 2, 0, GRID_W - NA_WIN_W)
    col_ok = (ck >= wc0) & (ck < wc0 + NA_WIN_W)
    col_pick = np.clip(ck - cq + NA_WIN_W - 1, 0, n_dc - 1)[..., None] == np.arange(n_dc)
    by_col = jnp.einsum("hab,qcb->haqc", rpb.astype(F32), jnp.asarray(col_pick, F32), precision=HIGHEST)
    margin = 4
    by_col = jnp.pad(by_col.transpose(0, 2, 1, 3), ((0, 0), (0, 0), (margin, margin), (0, 0)))
    by_col = by_col.reshape(HEADS, GRID_W, (n_dr + 2 * margin) * GRID_W)
    pieces, row_ok = [], []
    for tile in (0, 1, GRID_H // 4 - 1):
        slab0 = int(np.clip(tile - 1, 0, GRID_H // 4 - NA_SLAB_ROWS // 4)) * 4
        rq = tile * 4 + np.arange(4)
        rk = (slab0 + np.arange(NA_SLAB) // GRID_W)[None, :]
        wr0 = np.clip(rq - NA_WIN_H // 2, 0, GRID_H - NA_WIN_H)[:, None]
        row_ok.append((rk >= wr0) & (rk < wr0 + NA_WIN_H))
        for r in rq:
            first = slab0 - int(r) + NA_WIN_H - 1 + margin
            assert 0 <= first and first + NA_SLAB_ROWS <= n_dr + 2 * margin
            pieces.append(by_col[:, :, first * GRID_W:first * GRID_W + NA_SLAB])
    table = jnp.stack(pieces).reshape(3, 4, HEADS, GRID_W, NA_SLAB).transpose(0, 2, 1, 3, 4)
    valid = np.stack(row_ok)[:, None, :, None, :] & np.tile(col_ok, (1, NA_SLAB_ROWS))[None, None, None]
    table = jnp.where(jnp.asarray(valid), table, NEG_BIG)
    return table.reshape(3, HEADS, TM, NA_SLAB)


def na_latent(p, kc, vc, bias):
    n_t = LAT_TILES_PER_SEQ
    seq_blk0 = TP // DEC_SEQ

    def bias_idx(b, i):
        return (jnp.minimum(i, 1) + i // (n_t - 1), 0, 0, 0)

    return pl.pallas_call(
        _na_kernel,
        grid=(DEC_BATCH, n_t),
        in_specs=[
            pl.BlockSpec((TM, MIX_BLK), lambda b, i: (CTX_TILES + b * n_t + i, C_CQ)),
            pl.BlockSpec((DEC_SEQ, MIX_BLK), lambda b, i: (seq_blk0 + b, C_CK)),
            pl.BlockSpec((DEC_SEQ, MIX_BLK), lambda b, i: (seq_blk0 + b, C_CV)),
            pl.BlockSpec((PAST_LEN, MIX_BLK), lambda b, i: (b, 0)),
            pl.BlockSpec((PAST_LEN, MIX_BLK), lambda b, i: (b, 0)),
            pl.BlockSpec((1, HEADS, TM, NA_SLAB), bias_idx),
        ],
        out_specs=pl.BlockSpec((TM, MIX_BLK), lambda b, i: (b * n_t + i, 0)),
        out_shape=jax.ShapeDtypeStruct((TL, MIX_BLK), F32),
        compiler_params=_cparams(("arbitrary", "arbitrary")),
        name="na_latent",
    )(p, p, p, kc, vc, bias)


MAX_EXPONENT = 80.0


def _hgrn_direction(q_ref, f_ref, v_ref, lb, st_ref, o_ref, reverse):
    n_ch = TM // B_CHUNK
    r_idx = lax.broadcasted_iota(jnp.int32, (TM, TM), 0)
    c_idx = lax.broadcasted_iota(jnp.int32, (TM, TM), 1)
    tri = (c_idx >= r_idx) if reverse else (c_idx <= r_idx)
    zq = q_ref[...]
    q = zq * jax.nn.sigmoid(zq)
    z = f_ref[...]
    logf = jnp.log(lb + (1.0 - lb) * jax.nn.sigmoid(z))
    kk = (1.0 - lb) * jax.nn.sigmoid(-z)
    b = _select_sum_left(jnp.where(tri, 1.0, 0.0).astype(BF16), logf)
    b3 = b.reshape(n_ch, B_CHUNK, MIX_BLK)
    mid = B_CHUNK // 2 if reverse else B_CHUNK // 2 - 1
    q_in = (q.reshape(b3.shape) * jnp.exp(b3 - b3[:, mid:mid + 1, :])).reshape(TM, MIX_BLK)
    q_dec = (q * jnp.exp(b)).astype(BF16)
    b_t = b.T
    kk_t = kk.T
    far = 0 if reverse else TM - 1
    b_far = b_t[:, far:far + 1]
    k_dec_t = (kk_t * jnp.exp(b_far - b_t)).astype(BF16)
    vb = v_ref[...].astype(BF16)
    st = st_ref[...]
    o_state = jnp.dot(q_dec, st.astype(BF16), preferred_element_type=F32)
    kv = jnp.dot(k_dec_t, vb, preferred_element_type=F32)
    st_ref[...] = st * jnp.exp(b_far) + jnp.where(_same_head_matrix(), kv, 0.0)
    lane = _head_lanes()
    token = lax.broadcasted_iota(jnp.int32, (1, TM), 1)
    local = lax.broadcasted_iota(jnp.int32, (HEADS * B_CHUNK, 1), 0) % B_CHUNK
    heads = [_lane_range(lane, h * HEAD_DIM, HEAD_DIM) for h in range(HEADS)]
    for c in range(n_ch):
        rows = slice(c * B_CHUNK, (c + 1) * B_CHUNK)
        ref = b_t[:, c * B_CHUNK + mid:c * B_CHUNK + mid + 1]
        k_c_t = (kk_t * jnp.exp(jnp.minimum(ref - b_t, MAX_EXPONENT))).astype(BF16)
        q_c = q_in[rows, :]
        lhs = jnp.concatenate([jnp.where(in_head, q_c, 0.0) for in_head in heads], axis=0)
        a = jnp.dot(lhs.astype(BF16), k_c_t, preferred_element_type=F32)
        t_abs = c * B_CHUNK + local
        a = jnp.where((token >= t_abs) if reverse else (token <= t_abs), a, 0.0)
        res = jnp.dot(a.astype(BF16), vb, preferred_element_type=F32)
        o_c = o_state[rows, :]
        for h, in_head in enumerate(heads):
            o_c = o_c + jnp.where(in_head, res[h * B_CHUNK:(h + 1) * B_CHUNK, :], 0.0)
        o_ref[rows, :] = o_c


def _hgrn_kernel(qf_ref, ff_ref, vf_ref, qb_ref, fb_ref, vb_ref, lb_ref, s0_ref,
                 of_ref, ob_ref, s_ref, stf_ref, stb_ref, *, has_s0):
    j = pl.program_id(1)

    @pl.when(j == 0)
    def _():
        if has_s0:
            stf_ref[...] = s0_ref[0, 0]
            stb_ref[...] = s0_ref[0, 1]
        else:
            stf_ref[...] = jnp.zeros((MIX_BLK, MIX_BLK), F32)
            stb_ref[...] = jnp.zeros((MIX_BLK, MIX_BLK), F32)

    lb = lb_ref[...]
    _hgrn_direction(qf_ref, ff_ref, vf_ref, lb[0:1], stf_ref, of_ref, False)
    _hgrn_direction(qb_ref, fb_ref, vb_ref, lb[1:2], stb_ref, ob_ref, True)

    @pl.when(j == pl.num_programs(1) - 1)
    def _():
        for d, st_ref in enumerate((stf_ref, stb_ref)):
            s = st_ref[...]
            for hd in range(HEADS):
                lo = hd * HEAD_DIM
                s_ref[0, d, hd] = s[lo:lo + HEAD_DIM, lo:lo + HEAD_DIM]


def hgrn(p, row_tile0, n_seq, seq_len, lb, s0):
    nb = seq_len // TM
    has_s0 = s0 is not None
    if s0 is None:
        s0 = jnp.zeros((1, 2, MIX_BLK, MIX_BLK), F32)

    def fwd(col):
        return pl.BlockSpec((TM, MIX_BLK), lambda s, j: (row_tile0 + s * nb + j, col))

    def bwd(col):
        return pl.BlockSpec((TM, MIX_BLK), lambda s, j: (row_tile0 + s * nb + nb - 1 - j, col))

    state_spec = pl.BlockSpec((1, 2, MIX_BLK, MIX_BLK), lambda s, j: (s if has_s0 else 0, 0, 0, 0))
    out_rows = n_seq * seq_len
    return pl.pallas_call(
        functools.partial(_hgrn_kernel, has_s0=has_s0),
        grid=(n_seq, nb),
        in_specs=[fwd(C_BQ), fwd(C_BFF), fwd(C_BV), bwd(C_BQ), bwd(C_BFB), bwd(C_BV),
                  pl.BlockSpec((2, MIX_BLK), lambda s, j: (0, 0)), state_spec],
        out_specs=[
            pl.BlockSpec((TM, MIX_BLK), lambda s, j: (s * nb + j, 0)),
            pl.BlockSpec((TM, MIX_BLK), lambda s, j: (s * nb + nb - 1 - j, 0)),
            pl.BlockSpec((1, 2, HEADS, HEAD_DIM, HEAD_DIM), lambda s, j: (s, 0, 0, 0, 0)),
        ],
        out_shape=[jax.ShapeDtypeStruct((out_rows, MIX_BLK), F32),
                   jax.ShapeDtypeStruct((out_rows, MIX_BLK), F32),
                   jax.ShapeDtypeStruct((n_seq, 2, HEADS, HEAD_DIM, HEAD_DIM), F32)],
        scratch_shapes=[pltpu.VMEM((MIX_BLK, MIX_BLK), F32), pltpu.VMEM((MIX_BLK, MIX_BLK), F32)],
        compiler_params=_cparams(("arbitrary", "arbitrary")),
        name="hgrn_%d" % seq_len,
    )(p, p, p, p, p, p, lb, s0)


def _state_to_blockdiag(s):
    eye = jnp.eye(HEADS, dtype=F32)
    full = s.astype(F32)[:, :, :, :, None, :] * eye[None, None, :, None, :, None]
    return full.reshape(s.shape[0], 2, MIX_BLK, MIX_BLK)


def _fft_kernel(u_ref, c64_ref, s64_ref, cl_ref, sl_ref, o_ref, a_ref, b_ref, *, norm):
    @pl.when(pl.program_id(1) == 0)
    def _():
        u = u_ref[...].astype(BF16)
        a_ref[...] = jnp.dot(u, c64_ref[...], preferred_element_type=F32).astype(BF16)
        b_ref[...] = jnp.dot(u, s64_ref[...], preferred_element_type=F32).astype(BF16)

    o_ref[...] = (jnp.dot(cl_ref[...], a_ref[...], preferred_element_type=F32)
                  - jnp.dot(sl_ref[...], b_ref[...], preferred_element_type=F32)) * norm


def _dft_tables(n):
    k = np.arange(n)
    ang = 2.0 * np.pi * ((k[:, None] * k[None, :]) % n) / n
    return np.cos(ang), np.sin(ang)


def _dft_constants(seq_len):
    c64, s64 = _dft_tables(HEAD_DIM)
    eye = np.eye(HEADS)
    cl, sl = _dft_tables(seq_len)
    as_bf16 = lambda a: jnp.asarray(a, F32).astype(BF16)
    return as_bf16(np.kron(eye, c64)), as_bf16(np.kron(eye, s64)), as_bf16(cl), as_bf16(sl)


def fourier_mix(p, row_blk0, n_seq, seq_len, consts):
    c64, s64, cl, sl = consts
    nb = seq_len // TM
    norm = 1.0 / math.sqrt(seq_len * HEAD_DIM)
    return pl.pallas_call(
        functools.partial(_fft_kernel, norm=norm),
        grid=(n_seq, nb),
        in_specs=[
            pl.BlockSpec((seq_len, MIX_BLK), lambda s, i: (row_blk0 + s, C_DU)),
            pl.BlockSpec((MIX_BLK, MIX_BLK), lambda s, i: (0, 0)),
            pl.BlockSpec((MIX_BLK, MIX_BLK), lambda s, i: (0, 0)),
            pl.BlockSpec((TM, seq_len), lambda s, i: (i, 0)),
            pl.BlockSpec((TM, seq_len), lambda s, i: (i, 0)),
        ],
        out_specs=pl.BlockSpec((TM, MIX_BLK), lambda s, i: (s * nb + i, 0)),
        out_shape=jax.ShapeDtypeStruct((n_seq * seq_len, MIX_BLK), F32),
        scratch_shapes=[pltpu.VMEM((seq_len, MIX_BLK), BF16), pltpu.VMEM((seq_len, MIX_BLK), BF16)],
        compiler_params=_cparams(("arbitrary", "arbitrary")),
        name="fourier_%d" % seq_len,
    )(p, c64, s64, cl, sl)


def _route(logits_t, rb):
    per = N_EXPERTS // N_GROUPS
    score = [jax.nn.sigmoid(logits_t[e:e + 1, :]) for e in range(N_EXPERTS)]
    sel = [score[e] + rb[e:e + 1, :] for e in range(N_EXPERTS)]
    gscore = []
    for g in range(N_GROUPS):
        vals = sel[g * per:(g + 1) * per]
        best = None
        for a in range(per):
            for b in range(a + 1, per):
                pair = vals[a] + vals[b]
                best = pair if best is None else jnp.maximum(best, pair)
        gscore.append(best)
    chosen = []
    for g in range(N_GROUPS):
        ok = None
        for j in range(N_GROUPS):
            if j == g:
                continue
            cond = gscore[g] > gscore[j] if j < g else gscore[g] >= gscore[j]
            ok = cond if ok is None else ok & cond
        chosen.append(ok)
    picked = []
    for e in range(N_EXPERTS):
        g = e // per
        rank = jnp.zeros_like(sel[e])
        for j in range(g * per, (g + 1) * per):
            if j == e:
                continue
            ahead = sel[j] >= sel[e] if j < e else sel[j] > sel[e]
            rank = rank + jnp.where(ahead, 1.0, 0.0)
        picked.append(chosen[g] & (rank < 2.0))
    wsum = jnp.zeros_like(score[0])
    for e in range(N_EXPERTS):
        wsum = wsum + jnp.where(picked[e], score[e], 0.0)
    bucket = jnp.zeros_like(wsum)
    w_a = jnp.zeros_like(wsum)
    w_b = jnp.zeros_like(wsum)
    for g in range(N_GROUPS):
        for n, (a, b) in enumerate(EXPERT_PAIRS):
            hit = picked[g * per + a] & picked[g * per + b]
            bucket = jnp.where(hit, float(g * len(EXPERT_PAIRS) + n), bucket)
            w_a = jnp.where(hit, score[g * per + a] / wsum, w_a)
            w_b = jnp.where(hit, score[g * per + b] / wsum, w_b)
    return bucket, w_a, w_b


def _out_kernel(*refs):
    streams, rest = refs[:12], refs[12:]
    (bg_ref, mod_ref, hg_ref, w_ref, g2_ref, rw_ref, rb_ref,
     x1_ref, h2_ref, bucket_ref, rank_ref, counts_ref, run_ref) = rest
    is_ctx = _is_ctx_tile()
    x, o_a, o_f, o_b, o_c, o_d = (jnp.where(is_ctx, streams[2 * n][...], streams[2 * n + 1][...])
                                  for n in range(6))

    @pl.when(pl.program_id(0) == 0)
    def _():
        run_ref[...] = jnp.zeros(run_ref.shape, F32)

    mod = mod_ref[0]
    hb = o_f + o_b
    zg = bg_ref[...]
    hb = hb * lax.rsqrt(_head_mean_square(hb) + EPS) * hg_ref[...] * (zg * jax.nn.sigmoid(zg))
    parts = (o_a, hb, o_c, o_d)
    mixed = jnp.zeros((TM, D_MODEL), F32)
    for n, part in enumerate(parts):
        mixed = mixed + jnp.dot(part.astype(BF16), w_ref[0, n * MIX_BLK:(n + 1) * MIX_BLK, :],
                                preferred_element_type=F32)
    x1 = x + mod[2:3] * mixed
    x1_ref[...] = x1
    ms = jnp.mean(x1 * x1, axis=-1, keepdims=True)
    h2 = x1 * lax.rsqrt(ms + EPS) * g2_ref[...] * (1.0 + mod[4:5]) + mod[3:4]
    rw = rw_ref[...]
    r = sum(jnp.dot(piece, rw, preferred_element_type=F32) for piece in _bf16_pieces(h2, 2))
    bucket, w_a, w_b = _route((r[:, :LANES] + r[:, LANES:]).T, rb_ref[...])
    h2_ref[:, :D_MODEL] = h2
    h2_ref[:, D_MODEL:] = jnp.concatenate([w_a, w_b, jnp.zeros((LANES - 2, TM), F32)], axis=0).T
    onehot = jnp.where(lax.broadcasted_iota(jnp.int32, (BUCKET_ROWS, 1), 0).astype(F32) == bucket, 1.0, 0.0)
    s_idx = lax.broadcasted_iota(jnp.int32, (TM, TM), 0)
    t_idx = lax.broadcasted_iota(jnp.int32, (TM, TM), 1)
    prefix = jnp.dot(onehot.astype(BF16), jnp.where(s_idx <= t_idx, 1.0, 0.0).astype(BF16),
                     preferred_element_type=F32)
    run = run_ref[...]
    rank = jnp.sum(onehot * (prefix - 1.0 + run[:, 0:1]), axis=0, keepdims=True)
    run = run + jnp.sum(onehot, axis=1, keepdims=True)
    run_ref[...] = run
    bucket_ref[...] = bucket.astype(jnp.int32)
    rank_ref[...] = rank.astype(jnp.int32)
    counts_ref[...] = run


def out_and_route(x_pair, x_specs, mixer_pairs, p, mod, layer, hgrn_g, w_out_bf16, norm2_g, router_pieces,
                  router_b):
    tile = lambda w: pl.BlockSpec((TM, w), lambda i: (i, 0))
    full = lambda r, c: pl.BlockSpec((r, c), lambda i: (0, 0))
    stream_specs = list(x_specs)
    stream_args = list(x_pair)
    for o_ctx, o_lat in mixer_pairs:
        stream_specs += [pl.BlockSpec((TM, MIX_BLK), _ctx_tile), pl.BlockSpec((TM, MIX_BLK), _lat_tile)]
        stream_args += [o_ctx, o_lat]
    return pl.pallas_call(
        _out_kernel,
        grid=(N_TILES,),
        in_specs=stream_specs + [
            pl.BlockSpec((TM, MIX_BLK), lambda i: (i, C_BG)),
            pl.BlockSpec((1, 6, D_MODEL), lambda i: (_mod_row(i), 0, 0)),
            full(1, MIX_BLK), pl.BlockSpec((1, D_MODEL, D_MODEL), lambda i: (layer, 0, 0)), full(1, D_MODEL),
            full(D_MODEL, 2 * LANES), full(N_EXPERTS, 1),
        ],
        out_specs=[tile(D_MODEL), tile(ROW_W), pl.BlockSpec((1, TM), lambda i: (0, i)),
                   pl.BlockSpec((1, TM), lambda i: (0, i)), full(BUCKET_ROWS, LANES)],
        out_shape=[jax.ShapeDtypeStruct((T, D_MODEL), F32),
                   jax.ShapeDtypeStruct((T, ROW_W), F32),
                   jax.ShapeDtypeStruct((1, T), jnp.int32),
                   jax.ShapeDtypeStruct((1, T), jnp.int32),
                   jax.ShapeDtypeStruct((BUCKET_ROWS, LANES), F32)],
        scratch_shapes=[pltpu.VMEM((BUCKET_ROWS, LANES), F32)],
        compiler_params=_cparams(("arbitrary",)),
        name="out_and_route",
    )(*stream_args, p, mod, jnp.tile(hgrn_g, HEADS)[None, :], w_out_bf16,
      norm2_g[None, :], router_pieces, router_b[:, None])


def _router_pieces(router_w):
    hi, lo = _bf16_pieces(router_w.astype(F32), 2)
    pad = lambda a: jnp.pad(a, ((0, 0), (0, LANES - N_EXPERTS)))
    return jnp.concatenate([pad(hi), pad(lo)], axis=1)


def routing_plan(bucket, rank, counts):
    counts = counts[:N_BUCKETS, 0].astype(jnp.int32)
    n_tiles = (counts + TM_MOE - 1) // TM_MOE
    tile_end = jnp.cumsum(n_tiles)
    tile_start = tile_end - n_tiles
    buckets = jnp.arange(N_BUCKETS, dtype=jnp.int32)
    start_of_token = jnp.sum(jnp.where(bucket[0][:, None] == buckets[None, :], tile_start[None, :], 0), axis=1)
    dest = start_of_token * TM_MOE + rank[0]
    tiles = jnp.arange(MAX_TILES, dtype=jnp.int32)
    valid = tiles < tile_end[-1]
    tile_bucket = jnp.sum((jnp.minimum(tiles, tile_end[-1] - 1)[:, None] >= tile_end[None, :]).astype(jnp.int32), axis=1)
    pair_a = np.array([a for a, _ in EXPERT_PAIRS], np.int32)
    pair_b = np.array([b for _, b in EXPERT_PAIRS], np.int32)
    per = N_EXPERTS // N_GROUPS
    exp_a = jnp.asarray((np.arange(N_BUCKETS) // len(EXPERT_PAIRS)) * per + np.tile(pair_a, N_GROUPS), jnp.int32)
    exp_b = jnp.asarray((np.arange(N_BUCKETS) // len(EXPERT_PAIRS)) * per + np.tile(pair_b, N_GROUPS), jnp.int32)
    pick = tile_bucket[:, None] == buckets[None, :]
    tile_a = jnp.sum(jnp.where(pick, exp_a[None, :], 0), axis=1)
    tile_b = jnp.sum(jnp.where(pick, exp_b[None, :], 0), axis=1)
    return dest.astype(jnp.int32), tile_a, tile_b, valid.astype(jnp.int32)


def _row_copy(src, src_row, dst, dst_row, sem):
    return pltpu.make_async_copy(src.at[pl.ds(src_row, 1), :], dst.at[pl.ds(dst_row, 1), :], sem)


def _scatter_kernel(dest_ref, h_ref, init_ref, o_ref, sem):
    del init_ref
    base = pl.program_id(0) * TM

    for r in range(TM):
        _row_copy(h_ref, r, o_ref, dest_ref[base + r], sem).start()
    pltpu.make_async_copy(h_ref, o_ref.at[pl.ds(0, TM), :], sem).wait()


def scatter_to_slots(h2, dest):
    return pl.pallas_call(
        _scatter_kernel,
        grid_spec=pltpu.PrefetchScalarGridSpec(
            num_scalar_prefetch=1,
            grid=(N_TILES,),
            in_specs=[pl.BlockSpec((TM, ROW_W), lambda i, d: (i, 0)),
                      pl.BlockSpec(memory_space=pl.ANY)],
            out_specs=pl.BlockSpec(memory_space=pl.ANY),
            scratch_shapes=[pltpu.SemaphoreType.DMA(())],
        ),
        out_shape=jax.ShapeDtypeStruct((N_SLOTS, ROW_W), F32),
        input_output_aliases={2: 0},
        compiler_params=_cparams(("arbitrary",)),
        name="scatter_to_slots",
    )(dest, h2, jnp.zeros((N_SLOTS, ROW_W), F32))


def _moe_kernel(ta_ref, tb_ref, valid_ref, h_ref, wga_ref, wua_ref, wda_ref, wgb_ref, wub_ref, wdb_ref,
                o_ref):
    del ta_ref, tb_ref
    i = pl.program_id(0)

    @pl.when(valid_ref[i] == 1)
    def _():
        x = h_ref[:, :D_MODEL].astype(BF16)
        gates = h_ref[:, D_MODEL:]
        y = jnp.zeros((TM_MOE, D_MODEL), F32)
        for n, (wg, wu, wd) in enumerate(((wga_ref, wua_ref, wda_ref), (wgb_ref, wub_ref, wdb_ref))):
            a = jnp.dot(x, wg[0, 0].astype(BF16), preferred_element_type=F32)
            u = jnp.dot(x, wu[0, 0].astype(BF16), preferred_element_type=F32)
            z = a * jax.nn.sigmoid(a) * u * gates[:, n:n + 1]
            y = y + jnp.dot(z.astype(BF16), wd[0, 0].astype(BF16), preferred_element_type=F32)
        o_ref[...] = y

    @pl.when(valid_ref[i] == 0)
    def _():
        o_ref[...] = jnp.zeros((TM_MOE, D_MODEL), F32)


def moe(h_slots, tile_a, tile_b, valid, layer, wg, wu, wd):
    up_a = pl.BlockSpec((1, 1, D_MODEL, D_EXPERT), lambda i, ta, tb, v: (layer, ta[i], 0, 0))
    up_b = pl.BlockSpec((1, 1, D_MODEL, D_EXPERT), lambda i, ta, tb, v: (layer, tb[i], 0, 0))
    down_a = pl.BlockSpec((1, 1, D_EXPERT, D_MODEL), lambda i, ta, tb, v: (layer, ta[i], 0, 0))
    down_b = pl.BlockSpec((1, 1, D_EXPERT, D_MODEL), lambda i, ta, tb, v: (layer, tb[i], 0, 0))
    return pl.pallas_call(
        _moe_kernel,
        grid_spec=pltpu.PrefetchScalarGridSpec(
            num_scalar_prefetch=3,
            grid=(MAX_TILES,),
            in_specs=[pl.BlockSpec((TM_MOE, ROW_W), lambda i, ta, tb, v: (i, 0)),
                      up_a, up_a, down_a, up_b, up_b, down_b],
            out_specs=pl.BlockSpec((TM_MOE, D_MODEL), lambda i, ta, tb, v: (i, 0)),
        ),
        out_shape=jax.ShapeDtypeStruct((N_SLOTS, D_MODEL), F32),
        compiler_params=_cparams(("arbitrary",)),
        name="moe",
    )(tile_a, tile_b, valid, h_slots, wg, wu, wd, wg, wu, wd)


def _gather_tile(dest_ref, y_ref, buf_ref, sem, tile, slot):
    for r in range(TM):
        pltpu.make_async_copy(y_ref.at[pl.ds(dest_ref[tile * TM + r], 1), :],
                              buf_ref.at[slot, pl.ds(r, 1), :], sem.at[slot]).start()


def _moe_residual(dest_ref, y_ref, buf_ref, sem, x1_ref, mod_ref):
    i = pl.program_id(0)
    slot = i % 2

    @pl.when(i == 0)
    def _():
        _gather_tile(dest_ref, y_ref, buf_ref, sem, 0, 0)

    @pl.when(i + 1 < pl.num_programs(0))
    def _():
        _gather_tile(dest_ref, y_ref, buf_ref, sem, i + 1, 1 - slot)

    pltpu.make_async_copy(y_ref.at[pl.ds(0, TM), :], buf_ref.at[slot], sem.at[slot]).wait()
    return x1_ref[...] + mod_ref[0][5:6] * buf_ref[slot]


def _final_kernel(dest_ref, y_ref, x1_ref, mod_ref, g_ref, oc_ref, ol_ref, buf_ref, sem):
    x2 = _moe_residual(dest_ref, y_ref, buf_ref, sem, x1_ref, mod_ref)
    ms = jnp.mean(x2 * x2, axis=-1, keepdims=True)
    y = x2 * lax.rsqrt(ms + EPS) * g_ref[...]

    @pl.when(_is_ctx_tile())
    def _():
        oc_ref[...] = y

    @pl.when(jnp.logical_not(_is_ctx_tile()))
    def _():
        ol_ref[...] = y


_GATHER_SCRATCH = [pltpu.VMEM((2, TM, D_MODEL), F32), pltpu.SemaphoreType.DMA((2,))]


def final_norm(dest, y_slots, x1, mod, final_g):
    return pl.pallas_call(
        _final_kernel,
        grid_spec=pltpu.PrefetchScalarGridSpec(
            num_scalar_prefetch=1,
            grid=(N_TILES,),
            in_specs=[pl.BlockSpec(memory_space=pl.ANY),
                      pl.BlockSpec((TM, D_MODEL), lambda i, d: (i, 0)),
                      pl.BlockSpec((1, 6, D_MODEL), lambda i, d: (_mod_row(i), 0, 0)),
                      pl.BlockSpec((1, D_MODEL), lambda i, d: (0, 0))],
            out_specs=[pl.BlockSpec((TM, D_MODEL), _ctx_tile), pl.BlockSpec((TM, D_MODEL), _lat_tile)],
            scratch_shapes=_GATHER_SCRATCH,
        ),
        out_shape=[jax.ShapeDtypeStruct((TP, D_MODEL), F32), jax.ShapeDtypeStruct((TL, D_MODEL), F32)],
        compiler_params=_cparams(("arbitrary",)),
        name="final_norm",
    )(dest, y_slots, x1, mod, final_g[None, :])


def _heads_to_lanes(t):
    b, h, l, d = t.shape
    return t.transpose(0, 2, 1, 3).reshape(b * l, h * d)


def kernel(x_prompt, x_sample, cache_diff_k, cache_diff_v, cache_na_k, cache_na_v, state_hgrn, c, c_ctx,
           norm1_g, norm2_g, ada_w, ada_b, w_in, w_out, diff_lambda, diff_subln_g, hgrn_lb_logits,
           hgrn_norm_g, na_rpb, router_w, router_b, moe_w_gate, moe_w_up, moe_w_down, final_norm_g):
    assert SEQ == TM and PAST_LEN == TM and DEC_SEQ % TM == 0 and TP % DEC_SEQ == 0
    x_pair = (x_prompt.reshape(TP, D_MODEL), x_sample.reshape(TL, D_MODEL))
    x_specs = (pl.BlockSpec((TM, D_MODEL), _ctx_tile), pl.BlockSpec((TM, D_MODEL), _lat_tile))
    w_in_bf16 = w_in.astype(BF16)
    w_out_bf16 = w_out.astype(BF16)
    router_pieces = _router_pieces(router_w)
    mods = modulation(jnp.concatenate([c_ctx[None, :], c], axis=0), ada_w, ada_b)
    mods = mods.reshape(DEPTH, 3, 6, D_MODEL)
    lb_sm = jax.nn.softmax(hgrn_lb_logits.astype(F32), axis=0)
    lb_all = jnp.cumsum(lb_sm, axis=0) - lb_sm[0:1]
    rope = _rope_tables()
    dft_ctx = _dft_constants(SEQ)
    dft_lat = _dft_constants(DEC_SEQ)
    lat_blk0 = TP // DEC_SEQ
    states = []
    moe_state = None
    for l in range(DEPTH):
        if moe_state is None:
            p, *new_kv = projection(*x_pair, mods[l], norm1_g[l], w_in_bf16, rope)
        else:
            x, p, *new_kv = projection_after_moe(*moe_state, mods[l - 1], mods[l], l, norm1_g[l], w_in_bf16, rope,
                                                 new_kv)
            x_pair = (x, x)
            x_specs = (pl.BlockSpec((TM, D_MODEL), _ctx_tile),
                       pl.BlockSpec((TM, D_MODEL), lambda i: (jnp.maximum(i, CTX_TILES), 0)))

        lq = diff_lambda[l].astype(F32)
        lam_init = 0.8 - 0.6 * math.exp(-0.3 * l)
        lam = (jnp.exp(jnp.sum(lq[0] * lq[1])) - jnp.exp(jnp.sum(lq[2] * lq[3])) + lam_init).reshape(1)
        subln = jnp.tile(diff_subln_g[l], HEADS)[None, :]
        diff = functools.partial(attention, p, cols=(C_AQ, C_AK, C_AV), lam=lam, norm_g=subln, n_maps=2,
                                 post_scale=1.0 - lam_init)
        oa_ctx = diff(row_blk0=0, n_seq=BATCH, seq_len=SEQ)
        oa_lat = diff(row_blk0=lat_blk0, n_seq=DEC_BATCH, seq_len=DEC_SEQ,
                      cache=(_heads_to_lanes(cache_diff_k[:, l]), _heads_to_lanes(cache_diff_v[:, l])))

        of_ctx, ob_ctx, st_ctx = hgrn(p, 0, BATCH, SEQ, lb_all[l], None)
        of_lat, ob_lat, _ = hgrn(p, CTX_TILES, DEC_BATCH, DEC_SEQ, lb_all[l],
                                 _state_to_blockdiag(state_hgrn[:, l]))

        oc_ctx = attention(p, 0, (C_CQ, C_CK, C_CV), BATCH, SEQ, lam, subln, n_maps=1, post_scale=1.0)
        oc_lat = na_latent(p, _heads_to_lanes(cache_na_k[:, l]), _heads_to_lanes(cache_na_v[:, l]),
                           _na_bias_tables(na_rpb[l]))

        od_ctx = fourier_mix(p, 0, BATCH, SEQ, dft_ctx)
        od_lat = fourier_mix(p, lat_blk0, DEC_BATCH, DEC_SEQ, dft_lat)

        x1, h2, bucket, rank, counts = out_and_route(
            x_pair, x_specs, ((oa_ctx, oa_lat), (of_ctx, of_lat), (ob_ctx, ob_lat), (oc_ctx, oc_lat),
                              (od_ctx, od_lat)),
            p, mods[l], l, hgrn_norm_g[l], w_out_bf16, norm2_g[l], router_pieces, router_b)
        dest, tile_a, tile_b, valid = routing_plan(bucket, rank, counts)
        y_slots = moe(scatter_to_slots(h2, dest), tile_a, tile_b, valid, l, moe_w_gate, moe_w_up, moe_w_down)
        moe_state = (dest, y_slots, x1)

        states.append(st_ctx)
    y_prompt, y_sample = final_norm(*moe_state, mods[DEPTH - 1], final_norm_g)
    return (y_prompt.reshape(BATCH, SEQ, D_MODEL), y_sample.reshape(DEC_BATCH, DEC_SEQ, D_MODEL),
            *new_kv, jnp.stack(states, axis=1))
```

```python
import functools
import math

import numpy as np
import jax
import jax.numpy as jnp
from jax import lax
from jax.experimental import pallas as pl
from jax.experimental.pallas import tpu as pltpu

F32 = jnp.float32
BF16 = jnp.bfloat16
HIGHEST = lax.Precision.HIGHEST

D_MODEL = 1024
BATCH = 16
SEQ = 256
DEPTH = 2
DEC_BATCH = 2
DEC_SEQ = 2048
PAST_LEN = 256
GRID_W = 64
GRID_H = DEC_SEQ // GRID_W
EPS = 1e-6
NEG_BIG = -1e30
HEADS = 4
HEAD_DIM = 64
MIX_BLK = HEADS * HEAD_DIM
A_DIM = 32
ROPE_BASE = 10000.0
B_CHUNK = 32
NA_WIN_H = 8
NA_WIN_W = 16
N_EXPERTS = 16
N_GROUPS = 4
D_EXPERT = 512
PROJ_W = 12 * MIX_BLK
TP = BATCH * SEQ
TL = DEC_BATCH * DEC_SEQ
T = TP + TL
TM = 256
N_TILES = T // TM
CTX_TILES = TP // TM
LAT_TILES_PER_SEQ = DEC_SEQ // TM
(C_AQ, C_AK, C_AV, C_BQ, C_BFF, C_BFB, C_BV, C_BG, C_CQ, C_CK, C_CV, C_DU) = range(12)
NA_SLAB_ROWS = 12
NA_SLAB = NA_SLAB_ROWS * GRID_W
LANES = 128
ROW_W = D_MODEL + LANES
EXPERT_PAIRS = ((0, 1), (0, 2), (0, 3), (1, 3), (2, 3), (2, 1))
N_BUCKETS = N_GROUPS * len(EXPERT_PAIRS)
BUCKET_ROWS = 32
TM_MOE = 384
MAX_TILES = -(-T // TM_MOE) + N_BUCKETS
N_SLOTS = MAX_TILES * TM_MOE
VMEM_LIMIT = 56 * 1024 * 1024


def _cparams(sem):
    return pltpu.CompilerParams(dimension_semantics=sem, vmem_limit_bytes=VMEM_LIMIT)


def _head_lanes(width=MIX_BLK):
    return lax.broadcasted_iota(jnp.int32, (1, width), 1)


def _lane_range(lane, lo, n):
    return (lane >= lo) & (lane < lo + n)


def _same_head_matrix():
    r = lax.broadcasted_iota(jnp.int32, (MIX_BLK, MIX_BLK), 0) // HEAD_DIM
    c = lax.broadcasted_iota(jnp.int32, (MIX_BLK, MIX_BLK), 1) // HEAD_DIM
    return r == c


def _bf16_pieces(x, n):
    pieces = []
    for _ in range(n):
        piece = x.astype(BF16)
        pieces.append(piece)
        x = x - piece.astype(F32)
    return pieces


def _select_sum_left(onehot_bf16, x):
    return sum(jnp.dot(onehot_bf16, piece, preferred_element_type=F32) for piece in _bf16_pieces(x, 3))


def _select_sum_right(x, onehot_bf16):
    return sum(jnp.dot(piece, onehot_bf16, preferred_element_type=F32) for piece in _bf16_pieces(x, 3))


def _head_mean_square(o):
    ones = jnp.where(_same_head_matrix(), 1.0, 0.0).astype(BF16)
    return _select_sum_right(o * o, ones) * (1.0 / HEAD_DIM)


def _mod_row(i):
    return jnp.where(i < CTX_TILES, 0, 1 + (i - CTX_TILES) // LAT_TILES_PER_SEQ)


def _mod_kernel(c_ref, w_ref, b_ref, o_ref):
    w = w_ref[0]
    for r in range(3):
        c = c_ref[r]
        s = c * jax.nn.sigmoid(c)
        o_ref[0, r:r + 1, :] = jnp.sum(s * w, axis=0, keepdims=True) + b_ref[0]


def modulation(c_rows, ada_w, ada_b):
    nt = 1536
    n_out = 6 * D_MODEL
    return pl.pallas_call(
        _mod_kernel,
        grid=(DEPTH, n_out // nt),
        in_specs=[
            pl.BlockSpec((3, D_MODEL, 1), lambda l, j: (0, 0, 0)),
            pl.BlockSpec((1, D_MODEL, nt), lambda l, j: (l, 0, j)),
            pl.BlockSpec((1, 1, nt), lambda l, j: (l, 0, j)),
        ],
        out_specs=pl.BlockSpec((1, 3, nt), lambda l, j: (l, 0, j)),
        out_shape=jax.ShapeDtypeStruct((DEPTH, 3, n_out), F32),
        compiler_params=_cparams(("arbitrary", "arbitrary")),
        name="modulation",
    )(c_rows[:, :, None], ada_w, ada_b[:, None, :])


def _is_ctx_tile():
    return pl.program_id(0) < CTX_TILES


def _ctx_tile(i, *_):
    return (jnp.minimum(i, CTX_TILES - 1), 0)


def _lat_tile(i, *_):
    return (jnp.maximum(i - CTX_TILES, 0), 0)


def _proj_kernel(xc_ref, xl_ref, mod_ref, g_ref, w_ref, cos_ref, sa_ref, sb_ref, o_ref, *cache_refs):
    x = jnp.where(_is_ctx_tile(), xc_ref[...], xl_ref[...])
    _proj_body(x, mod_ref, g_ref, w_ref, cos_ref, sa_ref, sb_ref, o_ref, cache_refs)


def _proj_after_moe_kernel(dest_ref, y_ref, x1_ref, modp_ref, mod_ref, g_ref, w_ref, cos_ref, sa_ref, sb_ref,
                           *rest):
    x2_ref, o_ref = rest[4:6]
    cache_refs, (buf_ref, sem) = rest[6:10], rest[10:]
    x2 = _moe_residual(dest_ref, y_ref, buf_ref, sem, x1_ref, modp_ref)
    x2_ref[...] = x2
    _proj_body(x2, mod_ref, g_ref, w_ref, cos_ref, sa_ref, sb_ref, o_ref, cache_refs)


def _proj_body(x, mod_ref, g_ref, w_ref, cos_ref, sa_ref, sb_ref, o_ref, cache_refs):
    ms = jnp.mean(x * x, axis=-1, keepdims=True)
    mod = mod_ref[0]
    h = x * lax.rsqrt(ms + EPS) * g_ref[...] * (1.0 + mod[1:2]) + mod[0:1]
    p = jnp.dot(h.astype(BF16), w_ref[0], preferred_element_type=F32)
    t = p[:, :2 * MIX_BLK]
    o_ref[:, :2 * MIX_BLK] = (t * cos_ref[...] + pltpu.roll(t, 1, 1) * sa_ref[...]
                              + pltpu.roll(t, 2 * MIX_BLK - 1, 1) * sb_ref[...])
    o_ref[:, 2 * MIX_BLK:] = p[:, 2 * MIX_BLK:]

    @pl.when(_is_ctx_tile())
    def _():
        for ref, col in zip(cache_refs, (C_AK, C_AV, C_CK, C_CV)):
            for hd in range(HEADS):
                lo = col * MIX_BLK + hd * HEAD_DIM
                ref[0, 0, hd] = p[:, lo:lo + HEAD_DIM]
            if ref.shape[1] > 1:
                ref[0, 1:] = jnp.zeros((ref.shape[1] - 1,) + tuple(ref.shape[2:]), F32)


_CACHE_SHAPE = jax.ShapeDtypeStruct((BATCH, DEPTH, HEADS, SEQ, HEAD_DIM), F32)


def _cache_spec(layer):
    n_layers = DEPTH if layer == 0 else 1
    return pl.BlockSpec((1, n_layers, HEADS, SEQ, HEAD_DIM),
                        lambda i, *_: (jnp.minimum(i, CTX_TILES - 1), layer, 0, 0, 0))


def _rope_tables():
    nf = A_DIM // 4
    freqs = ROPE_BASE ** (-np.arange(nf, dtype=np.float64) / nf)
    pos = np.arange(DEC_SEQ)
    row = (pos // GRID_W).astype(np.float64)
    col = (pos % GRID_W).astype(np.float64)
    ang = np.concatenate([row[:, None] * freqs, col[:, None] * freqs], axis=-1)
    cos = np.repeat(np.cos(ang), 2, axis=-1)
    sin = np.repeat(np.sin(ang), 2, axis=-1)
    odd = (np.arange(A_DIM) % 2 == 1)[None, :]
    sin_from_left = np.where(odd, sin, 0.0)
    sin_from_right = np.where(odd, 0.0, -sin)
    reps = 2 * MIX_BLK // A_DIM
    ident = (np.ones((TM, 2 * MIX_BLK)), np.zeros((TM, 2 * MIX_BLK)))
    return tuple(jnp.asarray(np.concatenate([np.tile(t, (1, reps)), tail], axis=0), F32)
                 for t, tail in ((cos, ident[0]), (sin_from_left, ident[1]), (sin_from_right, ident[1])))


def _rope_block(i):
    return (jnp.where(i < CTX_TILES, LAT_TILES_PER_SEQ, (i - CTX_TILES) % LAT_TILES_PER_SEQ), 0)


def projection_after_moe(dest, y_slots, x1, mod_prev, mod, layer, norm_g, w_in_bf16, rope, caches):
    rope_spec = pl.BlockSpec((TM, 2 * MIX_BLK), lambda i, d: _rope_block(i))
    mod_spec = pl.BlockSpec((1, 6, D_MODEL), lambda i, d: (_mod_row(i), 0, 0))
    n_in = 10
    return pl.pallas_call(
        _proj_after_moe_kernel,
        grid_spec=pltpu.PrefetchScalarGridSpec(
            num_scalar_prefetch=1,
            grid=(N_TILES,),
            in_specs=[pl.BlockSpec(memory_space=pl.ANY),
                      pl.BlockSpec((TM, D_MODEL), lambda i, d: (i, 0)),
                      mod_spec, mod_spec,
                      pl.BlockSpec((1, D_MODEL), lambda i, d: (0, 0)),
                      pl.BlockSpec((1, D_MODEL, PROJ_W), lambda i, d: (layer, 0, 0)),
                      rope_spec, rope_spec, rope_spec] + [pl.BlockSpec(memory_space=pl.ANY)] * 4,
            out_specs=[pl.BlockSpec((TM, D_MODEL), lambda i, d: (i, 0)),
                       pl.BlockSpec((TM, PROJ_W), lambda i, d: (i, 0))] + [_cache_spec(layer)] * 4,
            scratch_shapes=_GATHER_SCRATCH,
        ),
        out_shape=[jax.ShapeDtypeStruct((T, D_MODEL), F32), jax.ShapeDtypeStruct((T, PROJ_W), F32)]
        + [_CACHE_SHAPE] * 4,
        input_output_aliases={n_in + n: 2 + n for n in range(4)},
        compiler_params=_cparams(("arbitrary",)),
        name="projection_after_moe",
    )(dest, y_slots, x1, mod_prev, mod, norm_g[None, :], w_in_bf16, *rope, *caches)


def projection(x_ctx, x_lat, mod, norm_g, w_in_bf16, rope):
    layer = 0
    rope_spec = pl.BlockSpec((TM, 2 * MIX_BLK), _rope_block)
    return pl.pallas_call(
        _proj_kernel,
        grid=(N_TILES,),
        in_specs=[
            pl.BlockSpec((TM, D_MODEL), _ctx_tile),
            pl.BlockSpec((TM, D_MODEL), _lat_tile),
            pl.BlockSpec((1, 6, D_MODEL), lambda i: (_mod_row(i), 0, 0)),
            pl.BlockSpec((1, D_MODEL), lambda i: (0, 0)),
            pl.BlockSpec((1, D_MODEL, PROJ_W), lambda i: (layer, 0, 0)),
            rope_spec, rope_spec, rope_spec,
        ],
        out_specs=[pl.BlockSpec((TM, PROJ_W), lambda i: (i, 0))] + [_cache_spec(layer)] * 4,
        out_shape=[jax.ShapeDtypeStruct((T, PROJ_W), F32)] + [_CACHE_SHAPE] * 4,
        compiler_params=_cparams(("arbitrary",)),
        name="projection",
    )(x_ctx, x_lat, mod, norm_g[None, :], w_in_bf16, *rope)


LOG2_E = 1.4426950408889634


def _exp2_rows(s):
    e = jnp.exp2(s - jnp.max(s, axis=-1, keepdims=True))
    return e, 1.0 / jnp.sum(e, axis=-1, keepdims=True)


def _attn_kernel(lam_ref, q_ref, k_ref, v_ref, *rest, n_maps, post_scale, with_cache):
    if with_cache:
        kc_ref, vc_ref, g_ref, o_ref, kt_ref, vb_ref = rest
    else:
        g_ref, o_ref, kt_ref, vb_ref = rest

    @pl.when(pl.program_id(1) == 0)
    def _():
        k = k_ref[...]
        v = v_ref[...]
        if with_cache:
            k = jnp.concatenate([_cache_heads_on_lanes(kc_ref), k], axis=0)
            v = jnp.concatenate([_cache_heads_on_lanes(vc_ref), v], axis=0)
        kt_ref[...] = k.T.astype(BF16)
        vb_ref[...] = v.astype(BF16)

    lane = _head_lanes()
    map_dim = HEAD_DIM // n_maps
    q = q_ref[...] * (map_dim ** -0.5 * LOG2_E)
    kt = kt_ref[...]
    vb = vb_ref[...]
    o = jnp.zeros(q.shape, F32)
    for h in range(HEADS):
        parts = []
        for j in range(n_maps):
            qm = jnp.where(_lane_range(lane, h * HEAD_DIM + j * map_dim, map_dim), q, 0.0)
            parts.append(_exp2_rows(jnp.dot(qm.astype(BF16), kt, preferred_element_type=F32)))
        w = parts[0][0] * parts[0][1]
        if n_maps == 2:
            w = w - parts[1][0] * (lam_ref[0] * parts[1][1])
        oh = jnp.dot(w.astype(BF16), vb, preferred_element_type=F32)
        o = jnp.where(_lane_range(lane, h * HEAD_DIM, HEAD_DIM), oh, o)
    if n_maps == 2:
        o = o * lax.rsqrt(_head_mean_square(o) + EPS) * g_ref[...] * post_scale
    o_ref[...] = o


def _cache_block_spec(layer):
    return pl.BlockSpec((1, 1, HEADS, PAST_LEN, HEAD_DIM), lambda b, i: (b, layer, 0, 0, 0))


def _cache_heads_on_lanes(ref):
    return jnp.concatenate([ref[0, 0, h] for h in range(HEADS)], axis=1)


def attention(p, row_blk0, cols, n_seq, seq_len, lam, norm_g, *, n_maps, post_scale, cache=None, layer=0):
    nb = seq_len // TM
    kv_len = seq_len + (PAST_LEN if cache is not None else 0)
    kern = functools.partial(_attn_kernel, n_maps=n_maps, post_scale=post_scale, with_cache=cache is not None)
    kv_spec = lambda col: pl.BlockSpec((seq_len, MIX_BLK), lambda b, i: (row_blk0 + b, col))
    cache_specs = [_cache_block_spec(layer)] * 2 if cache is not None else []
    return pl.pallas_call(
        kern,
        grid=(n_seq, nb),
        in_specs=[
            pl.BlockSpec(memory_space=pltpu.SMEM),
            pl.BlockSpec((TM, MIX_BLK), lambda b, i: ((row_blk0 + b) * nb + i, cols[0])),
            kv_spec(cols[1]), kv_spec(cols[2]), *cache_specs,
            pl.BlockSpec((1, MIX_BLK), lambda b, i: (0, 0)),
        ],
        out_specs=pl.BlockSpec((TM, MIX_BLK), lambda b, i: (b * nb + i, 0)),
        out_shape=jax.ShapeDtypeStruct((n_seq * seq_len, MIX_BLK), F32),
        scratch_shapes=[pltpu.VMEM((MIX_BLK, kv_len), BF16), pltpu.VMEM((kv_len, MIX_BLK), BF16)],
        compiler_params=_cparams(("arbitrary", "arbitrary")),
        name="attention_%dmap_%d" % (n_maps, seq_len),
    )(lam, p, p, p, *(cache or ()), norm_g)


def _na_slab_start(i):
    return jnp.clip(i - 1, 0, GRID_H // 4 - NA_SLAB_ROWS // 4)


def _na_kernel(q_ref, k_ref, v_ref, kc_ref, vc_ref, bias_ref, o_ref):
    i = pl.program_id(1)
    start = pl.multiple_of(_na_slab_start(i) * TM, TM)
    ks_t = k_ref[pl.ds(start, NA_SLAB), :].T.astype(BF16)
    vs = v_ref[pl.ds(start, NA_SLAB), :].astype(BF16)
    kc_t = _cache_heads_on_lanes(kc_ref).T.astype(BF16)
    vc = _cache_heads_on_lanes(vc_ref).astype(BF16)
    q = q_ref[...] * (HEAD_DIM ** -0.5)
    lane = _head_lanes()
    o = jnp.zeros(q.shape, F32)
    for h in range(HEADS):
        in_head = _lane_range(lane, h * HEAD_DIM, HEAD_DIM)
        qm = jnp.where(in_head, q, 0.0).astype(BF16)
        s_loc = jnp.dot(qm, ks_t, preferred_element_type=F32) + bias_ref[0, h]
        s_ctx = jnp.dot(qm, kc_t, preferred_element_type=F32)
        m = jnp.maximum(jnp.max(s_loc, axis=-1, keepdims=True), jnp.max(s_ctx, axis=-1, keepdims=True))
        e_loc = jnp.exp(s_loc - m)
        e_ctx = jnp.exp(s_ctx - m)
        den = jnp.sum(e_loc, axis=-1, keepdims=True) + jnp.sum(e_ctx, axis=-1, keepdims=True)
        oh = (jnp.dot(e_loc.astype(BF16), vs, preferred_element_type=F32)
              + jnp.dot(e_ctx.astype(BF16), vc, preferred_element_type=F32)) / den
        o = jnp.where(in_head, oh, o)
    o_ref[...] = o


def _na_bias_tables(rpb):
    n_dr, n_dc = 2 * NA_WIN_H - 1, 2 * NA_WIN_W - 1
    cq = np.arange(GRID_W)[:, None]
    ck = np.arange(GRID_W)[None, :]
    wc0 = np.clip(cq - NA_WIN_W // 2, 0, GRID_W - NA_WIN_W)
    col_ok = (ck >= wc0) & (ck < wc0 + NA_WIN_W)
    col_pick = np.clip(ck - cq + NA_WIN_W - 1, 0, n_dc - 1)[..., None] == np.arange(n_dc)
    by_col = jnp.einsum("hab,qcb->haqc", rpb.astype(F32), jnp.asarray(col_pick, F32), precision=HIGHEST)
    margin = 4
    by_col = jnp.pad(by_col.transpose(0, 2, 1, 3), ((0, 0), (0, 0), (margin, margin), (0, 0)))
    by_col = by_col.reshape(HEADS, GRID_W, (n_dr + 2 * margin) * GRID_W)
    pieces, row_ok = [], []
    for tile in (0, 1, GRID_H // 4 - 1):
        slab0 = int(np.clip(tile - 1, 0, GRID_H // 4 - NA_SLAB_ROWS // 4)) * 4
        rq = tile * 4 + np.arange(4)
        rk = (slab0 + np.arange(NA_SLAB) // GRID_W)[None, :]
        wr0 = np.clip(rq - NA_WIN_H // 2, 0, GRID_H - NA_WIN_H)[:, None]
        row_ok.append((rk >= wr0) & (rk < wr0 + NA_WIN_H))
        for r in rq:
            first = slab0 - int(r) + NA_WIN_H - 1 + margin
            assert 0 <= first and first + NA_SLAB_ROWS <= n_dr + 2 * margin
            pieces.append(by_col[:, :, first * GRID_W:first * GRID_W + NA_SLAB])
    table = jnp.stack(pieces).reshape(3, 4, HEADS, GRID_W, NA_SLAB).transpose(0, 2, 1, 3, 4)
    valid = np.stack(row_ok)[:, None, :, None, :] & np.tile(col_ok, (1, NA_SLAB_ROWS))[None, None, None]
    table = jnp.where(jnp.asarray(valid), table, NEG_BIG)
    return table.reshape(3, HEADS, TM, NA_SLAB)


def na_latent(p, kc, vc, layer, bias):
    n_t = LAT_TILES_PER_SEQ
    seq_blk0 = TP // DEC_SEQ

    def bias_idx(b, i):
        return (jnp.minimum(i, 1) + i // (n_t - 1), 0, 0, 0)

    return pl.pallas_call(
        _na_kernel,
        grid=(DEC_BATCH, n_t),
        in_specs=[
            pl.BlockSpec((TM, MIX_BLK), lambda b, i: (CTX_TILES + b * n_t + i, C_CQ)),
            pl.BlockSpec((DEC_SEQ, MIX_BLK), lambda b, i: (seq_blk0 + b, C_CK)),
            pl.BlockSpec((DEC_SEQ, MIX_BLK), lambda b, i: (seq_blk0 + b, C_CV)),
            _cache_block_spec(layer), _cache_block_spec(layer),
            pl.BlockSpec((1, HEADS, TM, NA_SLAB), bias_idx),
        ],
        out_specs=pl.BlockSpec((TM, MIX_BLK), lambda b, i: (b * n_t + i, 0)),
        out_shape=jax.ShapeDtypeStruct((TL, MIX_BLK), F32),
        compiler_params=_cparams(("arbitrary", "arbitrary")),
        name="na_latent",
    )(p, p, p, kc, vc, bias)


MAX_EXPONENT = 80.0


def _hgrn_direction(q_ref, f_ref, v_ref, lb, st_ref, o_ref, reverse):
    n_ch = TM // B_CHUNK
    r_idx = lax.broadcasted_iota(jnp.int32, (TM, TM), 0)
    c_idx = lax.broadcasted_iota(jnp.int32, (TM, TM), 1)
    tri = (c_idx >= r_idx) if reverse else (c_idx <= r_idx)
    zq = q_ref[...]
    q = zq * jax.nn.sigmoid(zq)
    z = f_ref[...]
    logf = jnp.log(lb + (1.0 - lb) * jax.nn.sigmoid(z))
    kk = (1.0 - lb) * jax.nn.sigmoid(-z)
    b = _select_sum_left(jnp.where(tri, 1.0, 0.0).astype(BF16), logf)
    b3 = b.reshape(n_ch, B_CHUNK, MIX_BLK)
    mid = B_CHUNK // 2 if reverse else B_CHUNK // 2 - 1
    q_in = (q.reshape(b3.shape) * jnp.exp(b3 - b3[:, mid:mid + 1, :])).reshape(TM, MIX_BLK)
    q_dec = (q * jnp.exp(b)).astype(BF16)
    b_t = b.T
    kk_t = kk.T
    far = 0 if reverse else TM - 1
    b_far = b_t[:, far:far + 1]
    k_dec_t = (kk_t * jnp.exp(b_far - b_t)).astype(BF16)
    vb = v_ref[...].astype(BF16)
    st = st_ref[...]
    o_state = jnp.dot(q_dec, st.astype(BF16), preferred_element_type=F32)
    kv = jnp.dot(k_dec_t, vb, preferred_element_type=F32)
    st_ref[...] = st * jnp.exp(b_far) + jnp.where(_same_head_matrix(), kv, 0.0)
    lane = _head_lanes()
    token = lax.broadcasted_iota(jnp.int32, (1, TM), 1)
    local = lax.broadcasted_iota(jnp.int32, (HEADS * B_CHUNK, 1), 0) % B_CHUNK
    heads = [_lane_range(lane, h * HEAD_DIM, HEAD_DIM) for h in range(HEADS)]
    for c in range(n_ch):
        rows = slice(c * B_CHUNK, (c + 1) * B_CHUNK)
        ref = b_t[:, c * B_CHUNK + mid:c * B_CHUNK + mid + 1]
        k_c_t = (kk_t * jnp.exp(jnp.minimum(ref - b_t, MAX_EXPONENT))).astype(BF16)
        q_c = q_in[rows, :]
        lhs = jnp.concatenate([jnp.where(in_head, q_c, 0.0) for in_head in heads], axis=0)
        a = jnp.dot(lhs.astype(BF16), k_c_t, preferred_element_type=F32)
        t_abs = c * B_CHUNK + local
        a = jnp.where((token >= t_abs) if reverse else (token <= t_abs), a, 0.0)
        res = jnp.dot(a.astype(BF16), vb, preferred_element_type=F32)
        o_c = o_state[rows, :]
        for h, in_head in enumerate(heads):
            o_c = o_c + jnp.where(in_head, res[h * B_CHUNK:(h + 1) * B_CHUNK, :], 0.0)
        o_ref[rows, :] = o_c


def _hgrn_kernel(qf_ref, ff_ref, vf_ref, qb_ref, fb_ref, vb_ref, lb_ref, s0_ref,
                 of_ref, ob_ref, s_ref, stf_ref, stb_ref, *, has_s0):
    j = pl.program_id(1)

    @pl.when(j == 0)
    def _():
        if has_s0:
            stf_ref[...] = s0_ref[0, 0]
            stb_ref[...] = s0_ref[0, 1]
        else:
            stf_ref[...] = jnp.zeros((MIX_BLK, MIX_BLK), F32)
            stb_ref[...] = jnp.zeros((MIX_BLK, MIX_BLK), F32)

    lb = lb_ref[...]
    _hgrn_direction(qf_ref, ff_ref, vf_ref, lb[0:1], stf_ref, of_ref, False)
    _hgrn_direction(qb_ref, fb_ref, vb_ref, lb[1:2], stb_ref, ob_ref, True)

    @pl.when(j == pl.num_programs(1) - 1)
    def _():
        for d, st_ref in enumerate((stf_ref, stb_ref)):
            s = st_ref[...]
            for hd in range(HEADS):
                lo = hd * HEAD_DIM
                s_ref[0, d, hd] = s[lo:lo + HEAD_DIM, lo:lo + HEAD_DIM]


def hgrn(p, row_tile0, n_seq, seq_len, lb, s0):
    nb = seq_len // TM
    has_s0 = s0 is not None
    if s0 is None:
        s0 = jnp.zeros((1, 2, MIX_BLK, MIX_BLK), F32)

    def fwd(col):
        return pl.BlockSpec((TM, MIX_BLK), lambda s, j: (row_tile0 + s * nb + j, col))

    def bwd(col):
        return pl.BlockSpec((TM, MIX_BLK), lambda s, j: (row_tile0 + s * nb + nb - 1 - j, col))

    state_spec = pl.BlockSpec((1, 2, MIX_BLK, MIX_BLK), lambda s, j: (s if has_s0 else 0, 0, 0, 0))
    out_rows = n_seq * seq_len
    return pl.pallas_call(
        functools.partial(_hgrn_kernel, has_s0=has_s0),
        grid=(n_seq, nb),
        in_specs=[fwd(C_BQ), fwd(C_BFF), fwd(C_BV), bwd(C_BQ), bwd(C_BFB), bwd(C_BV),
                  pl.BlockSpec((2, MIX_BLK), lambda s, j: (0, 0)), state_spec],
        out_specs=[
            pl.BlockSpec((TM, MIX_BLK), lambda s, j: (s * nb + j, 0)),
            pl.BlockSpec((TM, MIX_BLK), lambda s, j: (s * nb + nb - 1 - j, 0)),
            pl.BlockSpec((1, 2, HEADS, HEAD_DIM, HEAD_DIM), lambda s, j: (s, 0, 0, 0, 0)),
        ],
        out_shape=[jax.ShapeDtypeStruct((out_rows, MIX_BLK), F32),
                   jax.ShapeDtypeStruct((out_rows, MIX_BLK), F32),
                   jax.ShapeDtypeStruct((n_seq, 2, HEADS, HEAD_DIM, HEAD_DIM), F32)],
        scratch_shapes=[pltpu.VMEM((MIX_BLK, MIX_BLK), F32), pltpu.VMEM((MIX_BLK, MIX_BLK), F32)],
        compiler_params=_cparams(("arbitrary", "arbitrary")),
        name="hgrn_%d" % seq_len,
    )(p, p, p, p, p, p, lb, s0)


def _state_to_blockdiag(s):
    eye = jnp.eye(HEADS, dtype=F32)
    full = s.astype(F32)[:, :, :, :, None, :] * eye[None, None, :, None, :, None]
    return full.reshape(s.shape[0], 2, MIX_BLK, MIX_BLK)


def _fft_kernel(u_ref, c64_ref, s64_ref, cl_ref, sl_ref, o_ref, a_ref, b_ref, *, norm):
    @pl.when(pl.program_id(1) == 0)
    def _():
        u = u_ref[...].astype(BF16)
        a_ref[...] = jnp.dot(u, c64_ref[...], preferred_element_type=F32).astype(BF16)
        b_ref[...] = jnp.dot(u, s64_ref[...], preferred_element_type=F32).astype(BF16)

    o_ref[...] = (jnp.dot(cl_ref[...], a_ref[...], preferred_element_type=F32)
                  - jnp.dot(sl_ref[...], b_ref[...], preferred_element_type=F32)) * norm


def _dft_tables(n):
    k = np.arange(n)
    ang = 2.0 * np.pi * ((k[:, None] * k[None, :]) % n) / n
    return np.cos(ang), np.sin(ang)


def _dft_constants(seq_len):
    c64, s64 = _dft_tables(HEAD_DIM)
    eye = np.eye(HEADS)
    cl, sl = _dft_tables(seq_len)
    as_bf16 = lambda a: jnp.asarray(a, F32).astype(BF16)
    return as_bf16(np.kron(eye, c64)), as_bf16(np.kron(eye, s64)), as_bf16(cl), as_bf16(sl)


def fourier_mix(p, row_blk0, n_seq, seq_len, consts):
    c64, s64, cl, sl = consts
    nb = seq_len // TM
    norm = 1.0 / math.sqrt(seq_len * HEAD_DIM)
    return pl.pallas_call(
        functools.partial(_fft_kernel, norm=norm),
        grid=(n_seq, nb),
        in_specs=[
            pl.BlockSpec((seq_len, MIX_BLK), lambda s, i: (row_blk0 + s, C_DU)),
            pl.BlockSpec((MIX_BLK, MIX_BLK), lambda s, i: (0, 0)),
            pl.BlockSpec((MIX_BLK, MIX_BLK), lambda s, i: (0, 0)),
            pl.BlockSpec((TM, seq_len), lambda s, i: (i, 0)),
            pl.BlockSpec((TM, seq_len), lambda s, i: (i, 0)),
        ],
        out_specs=pl.BlockSpec((TM, MIX_BLK), lambda s, i: (s * nb + i, 0)),
        out_shape=jax.ShapeDtypeStruct((n_seq * seq_len, MIX_BLK), F32),
        scratch_shapes=[pltpu.VMEM((seq_len, MIX_BLK), BF16), pltpu.VMEM((seq_len, MIX_BLK), BF16)],
        compiler_params=_cparams(("arbitrary", "arbitrary")),
        name="fourier_%d" % seq_len,
    )(p, c64, s64, cl, sl)


def _route(logits_t, rb):
    per = N_EXPERTS // N_GROUPS
    score = [jax.nn.sigmoid(logits_t[e:e + 1, :]) for e in range(N_EXPERTS)]
    sel = [score[e] + rb[e:e + 1, :] for e in range(N_EXPERTS)]
    gscore = []
    for g in range(N_GROUPS):
        vals = sel[g * per:(g + 1) * per]
        best = None
        for a in range(per):
            for b in range(a + 1, per):
                pair = vals[a] + vals[b]
                best = pair if best is None else jnp.maximum(best, pair)
        gscore.append(best)
    chosen = []
    for g in range(N_GROUPS):
        ok = None
        for j in range(N_GROUPS):
            if j == g:
                continue
            cond = gscore[g] > gscore[j] if j < g else gscore[g] >= gscore[j]
            ok = cond if ok is None else ok & cond
        chosen.append(ok)
    picked = []
    for e in range(N_EXPERTS):
        g = e // per
        rank = jnp.zeros_like(sel[e])
        for j in range(g * per, (g + 1) * per):
            if j == e:
                continue
            ahead = sel[j] >= sel[e] if j < e else sel[j] > sel[e]
            rank = rank + jnp.where(ahead, 1.0, 0.0)
        picked.append(chosen[g] & (rank < 2.0))
    wsum = jnp.zeros_like(score[0])
    for e in range(N_EXPERTS):
        wsum = wsum + jnp.where(picked[e], score[e], 0.0)
    bucket = jnp.zeros_like(wsum)
    w_a = jnp.zeros_like(wsum)
    w_b = jnp.zeros_like(wsum)
    for g in range(N_GROUPS):
        for n, (a, b) in enumerate(EXPERT_PAIRS):
            hit = picked[g * per + a] & picked[g * per + b]
            bucket = jnp.where(hit, float(g * len(EXPERT_PAIRS) + n), bucket)
            w_a = jnp.where(hit, score[g * per + a] / wsum, w_a)
            w_b = jnp.where(hit, score[g * per + b] / wsum, w_b)
    return bucket, w_a, w_b


def _out_kernel(*refs):
    streams, rest = refs[:12], refs[12:]
    (bg_ref, mod_ref, hg_ref, w_ref, g2_ref, rw_ref, rb_ref,
     x1_ref, h2_ref, bucket_ref, rank_ref, counts_ref, run_ref) = rest
    is_ctx = _is_ctx_tile()
    x, o_a, o_f, o_b, o_c, o_d = (jnp.where(is_ctx, streams[2 * n][...], streams[2 * n + 1][...])
                                  for n in range(6))

    @pl.when(pl.program_id(0) == 0)
    def _():
        run_ref[...] = jnp.zeros(run_ref.shape, F32)

    mod = mod_ref[0]
    hb = o_f + o_b
    zg = bg_ref[...]
    hb = hb * lax.rsqrt(_head_mean_square(hb) + EPS) * hg_ref[...] * (zg * jax.nn.sigmoid(zg))
    parts = (o_a, hb, o_c, o_d)
    mixed = jnp.zeros((TM, D_MODEL), F32)
    for n, part in enumerate(parts):
        mixed = mixed + jnp.dot(part.astype(BF16), w_ref[0, n * MIX_BLK:(n + 1) * MIX_BLK, :],
                                preferred_element_type=F32)
    x1 = x + mod[2:3] * mixed
    x1_ref[...] = x1
    ms = jnp.mean(x1 * x1, axis=-1, keepdims=True)
    h2 = x1 * lax.rsqrt(ms + EPS) * g2_ref[...] * (1.0 + mod[4:5]) + mod[3:4]
    rw = rw_ref[...]
    r = sum(jnp.dot(piece, rw, preferred_element_type=F32) for piece in _bf16_pieces(h2, 2))
    bucket, w_a, w_b = _route((r[:, :LANES] + r[:, LANES:]).T, rb_ref[...])
    h2_ref[:, :D_MODEL] = h2
    h2_ref[:, D_MODEL:] = jnp.concatenate([w_a, w_b, jnp.zeros((LANES - 2, TM), F32)], axis=0).T
    onehot = jnp.where(lax.broadcasted_iota(jnp.int32, (BUCKET_ROWS, 1), 0).astype(F32) == bucket, 1.0, 0.0)
    s_idx = lax.broadcasted_iota(jnp.int32, (TM, TM), 0)
    t_idx = lax.broadcasted_iota(jnp.int32, (TM, TM), 1)
    prefix = jnp.dot(onehot.astype(BF16), jnp.where(s_idx <= t_idx, 1.0, 0.0).astype(BF16),
                     preferred_element_type=F32)
    run = run_ref[...]
    rank = jnp.sum(onehot * (prefix - 1.0 + run[:, 0:1]), axis=0, keepdims=True)
    run = run + jnp.sum(onehot, axis=1, keepdims=True)
    run_ref[...] = run
    bucket_ref[...] = bucket.astype(jnp.int32)
    rank_ref[...] = rank.astype(jnp.int32)
    counts_ref[...] = run


def out_and_route(x_pair, x_specs, mixer_pairs, p, mod, layer, hgrn_g, w_out_bf16, norm2_g, router_pieces,
                  router_b):
    tile = lambda w: pl.BlockSpec((TM, w), lambda i: (i, 0))
    full = lambda r, c: pl.BlockSpec((r, c), lambda i: (0, 0))
    stream_specs = list(x_specs)
    stream_args = list(x_pair)
    for o_ctx, o_lat in mixer_pairs:
        stream_specs += [pl.BlockSpec((TM, MIX_BLK), _ctx_tile), pl.BlockSpec((TM, MIX_BLK), _lat_tile)]
        stream_args += [o_ctx, o_lat]
    return pl.pallas_call(
        _out_kernel,
        grid=(N_TILES,),
        in_specs=stream_specs + [
            pl.BlockSpec((TM, MIX_BLK), lambda i: (i, C_BG)),
            pl.BlockSpec((1, 6, D_MODEL), lambda i: (_mod_row(i), 0, 0)),
            full(1, MIX_BLK), pl.BlockSpec((1, D_MODEL, D_MODEL), lambda i: (layer, 0, 0)), full(1, D_MODEL),
            full(D_MODEL, 2 * LANES), full(N_EXPERTS, 1),
        ],
        out_specs=[tile(D_MODEL), tile(ROW_W), pl.BlockSpec((1, TM), lambda i: (0, i)),
                   pl.BlockSpec((1, TM), lambda i: (0, i)), full(BUCKET_ROWS, LANES)],
        out_shape=[jax.ShapeDtypeStruct((T, D_MODEL), F32),
                   jax.ShapeDtypeStruct((T, ROW_W), F32),
                   jax.ShapeDtypeStruct((1, T), jnp.int32),
                   jax.ShapeDtypeStruct((1, T), jnp.int32),
                   jax.ShapeDtypeStruct((BUCKET_ROWS, LANES), F32)],
        scratch_shapes=[pltpu.VMEM((BUCKET_ROWS, LANES), F32)],
        compiler_params=_cparams(("arbitrary",)),
        name="out_and_route",
    )(*stream_args, p, mod, jnp.tile(hgrn_g, HEADS)[None, :], w_out_bf16,
      norm2_g[None, :], router_pieces, router_b[:, None])


def _router_pieces(router_w):
    hi, lo = _bf16_pieces(router_w.astype(F32), 2)
    pad = lambda a: jnp.pad(a, ((0, 0), (0, LANES - N_EXPERTS)))
    return jnp.concatenate([pad(hi), pad(lo)], axis=1)


def routing_plan(bucket, rank, counts):
    counts = counts[:N_BUCKETS, 0].astype(jnp.int32)
    n_tiles = (counts + TM_MOE - 1) // TM_MOE
    tile_end = jnp.cumsum(n_tiles)
    tile_start = tile_end - n_tiles
    buckets = jnp.arange(N_BUCKETS, dtype=jnp.int32)
    start_of_token = jnp.sum(jnp.where(bucket[0][:, None] == buckets[None, :], tile_start[None, :], 0), axis=1)
    dest = start_of_token * TM_MOE + rank[0]
    tiles = jnp.arange(MAX_TILES, dtype=jnp.int32)
    valid = tiles < tile_end[-1]
    tile_bucket = jnp.sum((jnp.minimum(tiles, tile_end[-1] - 1)[:, None] >= tile_end[None, :]).astype(jnp.int32), axis=1)
    pair_a = np.array([a for a, _ in EXPERT_PAIRS], np.int32)
    pair_b = np.array([b for _, b in EXPERT_PAIRS], np.int32)
    per = N_EXPERTS // N_GROUPS
    exp_a = jnp.asarray((np.arange(N_BUCKETS) // len(EXPERT_PAIRS)) * per + np.tile(pair_a, N_GROUPS), jnp.int32)
    exp_b = jnp.asarray((np.arange(N_BUCKETS) // len(EXPERT_PAIRS)) * per + np.tile(pair_b, N_GROUPS), jnp.int32)
    pick = tile_bucket[:, None] == buckets[None, :]
    tile_a = jnp.sum(jnp.where(pick, exp_a[None, :], 0), axis=1)
    tile_b = jnp.sum(jnp.where(pick, exp_b[None, :], 0), axis=1)
    return dest.astype(jnp.int32), tile_a, tile_b, valid.astype(jnp.int32), (tile_end[-1:] - 1).astype(jnp.int32)


def _row_copy(src, src_row, dst, dst_row, sem):
    return pltpu.make_async_copy(src.at[pl.ds(src_row, 1), :], dst.at[pl.ds(dst_row, 1), :], sem)


def _scatter_kernel(dest_ref, h_ref, init_ref, o_ref, sem):
    del init_ref
    base = pl.program_id(0) * TM

    for r in range(TM):
        _row_copy(h_ref, r, o_ref, dest_ref[base + r], sem).start()
    pltpu.make_async_copy(h_ref, o_ref.at[pl.ds(0, TM), :], sem).wait()


def scatter_to_slots(h2, dest, slots):
    return pl.pallas_call(
        _scatter_kernel,
        grid_spec=pltpu.PrefetchScalarGridSpec(
            num_scalar_prefetch=1,
            grid=(N_TILES,),
            in_specs=[pl.BlockSpec((TM, ROW_W), lambda i, d: (i, 0)),
                      pl.BlockSpec(memory_space=pl.ANY)],
            out_specs=pl.BlockSpec(memory_space=pl.ANY),
            scratch_shapes=[pltpu.SemaphoreType.DMA(())],
        ),
        out_shape=jax.ShapeDtypeStruct((N_SLOTS, ROW_W), F32),
        input_output_aliases={2: 0},
        compiler_params=_cparams(("arbitrary",)),
        name="scatter_to_slots",
    )(dest, h2, slots)


def _moe_kernel(ta_ref, tb_ref, valid_ref, last_ref, h_ref, wga_ref, wua_ref, wda_ref, wgb_ref, wub_ref, wdb_ref,
                o_ref):
    del ta_ref, tb_ref, last_ref
    i = pl.program_id(0)

    @pl.when(valid_ref[i] == 1)
    def _():
        x = h_ref[:, :D_MODEL].astype(BF16)
        gates = h_ref[:, D_MODEL:]
        y = jnp.zeros((TM_MOE, D_MODEL), F32)
        for n, (wg, wu, wd) in enumerate(((wga_ref, wua_ref, wda_ref), (wgb_ref, wub_ref, wdb_ref))):
            a = jnp.dot(x, wg[0, 0].astype(BF16), preferred_element_type=F32)
            u = jnp.dot(x, wu[0, 0].astype(BF16), preferred_element_type=F32)
            z = a * jax.nn.sigmoid(a) * u * gates[:, n:n + 1]
            y = y + jnp.dot(z.astype(BF16), wd[0, 0].astype(BF16), preferred_element_type=F32)
        o_ref[...] = y

    @pl.when(valid_ref[i] == 0)
    def _():
        o_ref[...] = jnp.zeros((TM_MOE, D_MODEL), F32)


def moe(h_slots, tile_a, tile_b, valid, last, layer, wg, wu, wd):
    up_a = pl.BlockSpec((1, 1, D_MODEL, D_EXPERT), lambda i, ta, tb, v, last: (layer, ta[i], 0, 0))
    up_b = pl.BlockSpec((1, 1, D_MODEL, D_EXPERT), lambda i, ta, tb, v, last: (layer, tb[i], 0, 0))
    down_a = pl.BlockSpec((1, 1, D_EXPERT, D_MODEL), lambda i, ta, tb, v, last: (layer, ta[i], 0, 0))
    down_b = pl.BlockSpec((1, 1, D_EXPERT, D_MODEL), lambda i, ta, tb, v, last: (layer, tb[i], 0, 0))
    return pl.pallas_call(
        _moe_kernel,
        grid_spec=pltpu.PrefetchScalarGridSpec(
            num_scalar_prefetch=4,
            grid=(MAX_TILES,),
            in_specs=[pl.BlockSpec((TM_MOE, ROW_W), lambda i, ta, tb, v, last: (jnp.minimum(i, last[0]), 0)),
                      up_a, up_a, down_a, up_b, up_b, down_b],
            out_specs=pl.BlockSpec((TM_MOE, D_MODEL), lambda i, ta, tb, v, last: (i, 0)),
        ),
        out_shape=jax.ShapeDtypeStruct((N_SLOTS, D_MODEL), F32),
        compiler_params=_cparams(("arbitrary",)),
        name="moe",
    )(tile_a, tile_b, valid, last, h_slots, wg, wu, wd, wg, wu, wd)


def _gather_tile(dest_ref, y_ref, buf_ref, sem, tile, slot):
    for r in range(TM):
        pltpu.make_async_copy(y_ref.at[pl.ds(dest_ref[tile * TM + r], 1), :],
                              buf_ref.at[slot, pl.ds(r, 1), :], sem.at[slot]).start()


def _moe_residual(dest_ref, y_ref, buf_ref, sem, x1_ref, mod_ref):
    i = pl.program_id(0)
    slot = i % 2

    @pl.when(i == 0)
    def _():
        _gather_tile(dest_ref, y_ref, buf_ref, sem, 0, 0)

    @pl.when(i + 1 < pl.num_programs(0))
    def _():
        _gather_tile(dest_ref, y_ref, buf_ref, sem, i + 1, 1 - slot)

    pltpu.make_async_copy(y_ref.at[pl.ds(0, TM), :], buf_ref.at[slot], sem.at[slot]).wait()
    return x1_ref[...] + mod_ref[0][5:6] * buf_ref[slot]


def _final_kernel(dest_ref, y_ref, x1_ref, mod_ref, g_ref, oc_ref, ol_ref, buf_ref, sem):
    x2 = _moe_residual(dest_ref, y_ref, buf_ref, sem, x1_ref, mod_ref)
    ms = jnp.mean(x2 * x2, axis=-1, keepdims=True)
    y = x2 * lax.rsqrt(ms + EPS) * g_ref[...]

    @pl.when(_is_ctx_tile())
    def _():
        oc_ref[...] = y

    @pl.when(jnp.logical_not(_is_ctx_tile()))
    def _():
        ol_ref[...] = y


_GATHER_SCRATCH = [pltpu.VMEM((2, TM, D_MODEL), F32), pltpu.SemaphoreType.DMA((2,))]


def final_norm(dest, y_slots, x1, mod, final_g):
    return pl.pallas_call(
        _final_kernel,
        grid_spec=pltpu.PrefetchScalarGridSpec(
            num_scalar_prefetch=1,
            grid=(N_TILES,),
            in_specs=[pl.BlockSpec(memory_space=pl.ANY),
                      pl.BlockSpec((TM, D_MODEL), lambda i, d: (i, 0)),
                      pl.BlockSpec((1, 6, D_MODEL), lambda i, d: (_mod_row(i), 0, 0)),
                      pl.BlockSpec((1, D_MODEL), lambda i, d: (0, 0))],
            out_specs=[pl.BlockSpec((TM, D_MODEL), _ctx_tile), pl.BlockSpec((TM, D_MODEL), _lat_tile)],
            scratch_shapes=_GATHER_SCRATCH,
        ),
        out_shape=[jax.ShapeDtypeStruct((TP, D_MODEL), F32), jax.ShapeDtypeStruct((TL, D_MODEL), F32)],
        compiler_params=_cparams(("arbitrary",)),
        name="final_norm",
    )(dest, y_slots, x1, mod, final_g[None, :])


def kernel(x_prompt, x_sample, cache_diff_k, cache_diff_v, cache_na_k, cache_na_v, state_hgrn, c, c_ctx,
           norm1_g, norm2_g, ada_w, ada_b, w_in, w_out, diff_lambda, diff_subln_g, hgrn_lb_logits,
           hgrn_norm_g, na_rpb, router_w, router_b, moe_w_gate, moe_w_up, moe_w_down, final_norm_g):
    assert SEQ == TM and PAST_LEN == TM and DEC_SEQ % TM == 0 and TP % DEC_SEQ == 0
    x_pair = (x_prompt.reshape(TP, D_MODEL), x_sample.reshape(TL, D_MODEL))
    x_specs = (pl.BlockSpec((TM, D_MODEL), _ctx_tile), pl.BlockSpec((TM, D_MODEL), _lat_tile))
    w_in_bf16 = w_in.astype(BF16)
    w_out_bf16 = w_out.astype(BF16)
    router_pieces = _router_pieces(router_w)
    mods = modulation(jnp.concatenate([c_ctx[None, :], c], axis=0), ada_w, ada_b)
    mods = mods.reshape(DEPTH, 3, 6, D_MODEL)
    lb_sm = jax.nn.softmax(hgrn_lb_logits.astype(F32), axis=0)
    lb_all = jnp.cumsum(lb_sm, axis=0) - lb_sm[0:1]
    rope = _rope_tables()
    dft_ctx = _dft_constants(SEQ)
    dft_lat = _dft_constants(DEC_SEQ)
    lat_blk0 = TP // DEC_SEQ
    states = []
    moe_state = None
    for l in range(DEPTH):
        if moe_state is None:
            p, *new_kv = projection(*x_pair, mods[l], norm1_g[l], w_in_bf16, rope)
        else:
            x, p, *new_kv = projection_after_moe(*moe_state, mods[l - 1], mods[l], l, norm1_g[l], w_in_bf16, rope,
                                                 new_kv)
            x_pair = (x, x)
            x_specs = (pl.BlockSpec((TM, D_MODEL), _ctx_tile),
                       pl.BlockSpec((TM, D_MODEL), lambda i: (jnp.maximum(i, CTX_TILES), 0)))

        lq = diff_lambda[l].astype(F32)
        lam_init = 0.8 - 0.6 * math.exp(-0.3 * l)
        lam = (jnp.exp(jnp.sum(lq[0] * lq[1])) - jnp.exp(jnp.sum(lq[2] * lq[3])) + lam_init).reshape(1)
        subln = jnp.tile(diff_subln_g[l], HEADS)[None, :]
        diff = functools.partial(attention, p, cols=(C_AQ, C_AK, C_AV), lam=lam, norm_g=subln, n_maps=2,
                                 post_scale=1.0 - lam_init)
        oa_ctx = diff(row_blk0=0, n_seq=BATCH, seq_len=SEQ)
        oa_lat = diff(row_blk0=lat_blk0, n_seq=DEC_BATCH, seq_len=DEC_SEQ,
                      cache=(cache_diff_k, cache_diff_v), layer=l)

        of_ctx, ob_ctx, st_ctx = hgrn(p, 0, BATCH, SEQ, lb_all[l], None)
        of_lat, ob_lat, _ = hgrn(p, CTX_TILES, DEC_BATCH, DEC_SEQ, lb_all[l],
                                 _state_to_blockdiag(state_hgrn[:, l]))

        oc_ctx = attention(p, 0, (C_CQ, C_CK, C_CV), BATCH, SEQ, lam, subln, n_maps=1, post_scale=1.0)
        oc_lat = na_latent(p, cache_na_k, cache_na_v, l,
                           _na_bias_tables(na_rpb[l]))

        od_ctx = fourier_mix(p, 0, BATCH, SEQ, dft_ctx)
        od_lat = fourier_mix(p, lat_blk0, DEC_BATCH, DEC_SEQ, dft_lat)

        x1, h2, bucket, rank, counts = out_and_route(
            x_pair, x_specs, ((oa_ctx, oa_lat), (of_ctx, of_lat), (ob_ctx, ob_lat), (oc_ctx, oc_lat),
                              (od_ctx, od_lat)),
            p, mods[l], l, hgrn_norm_g[l], w_out_bf16, norm2_g[l], router_pieces, router_b)
        dest, *tile_plan = routing_plan(bucket, rank, counts)
        h_slots = scatter_to_slots(h2, dest, jnp.zeros((N_SLOTS, ROW_W), F32) if l == 0 else h_slots)
        y_slots = moe(h_slots, *tile_plan, l, moe_w_gate, moe_w_up, moe_w_down)
        moe_state = (dest, y_slots, x1)

        states.append(st_ctx)
    y_prompt, y_sample = final_norm(*moe_state, mods[DEPTH - 1], final_norm_g)
    return (y_prompt.reshape(BATCH, SEQ, D_MODEL), y_sample.reshape(DEC_BATCH, DEC_SEQ, D_MODEL),
            *new_kv, jnp.stack(states, axis=1))
```

```python
import functools
import math

import numpy as np
import jax
import jax.numpy as jnp
from jax import lax
from jax.experimental import pallas as pl
from jax.experimental.pallas import tpu as pltpu

F32 = jnp.float32
BF16 = jnp.bfloat16
HIGHEST = lax.Precision.HIGHEST

D_MODEL = 1024
BATCH = 16
SEQ = 256
DEPTH = 2
DEC_BATCH = 2
DEC_SEQ = 2048
PAST_LEN = 256
GRID_W = 64
GRID_H = DEC_SEQ // GRID_W
EPS = 1e-6
NEG_BIG = -1e30
HEADS = 4
HEAD_DIM = 64
MIX_BLK = HEADS * HEAD_DIM
A_DIM = 32
ROPE_BASE = 10000.0
B_CHUNK = 32
NA_WIN_H = 8
NA_WIN_W = 16
N_EXPERTS = 16
N_GROUPS = 4
D_EXPERT = 512
PROJ_W = 12 * MIX_BLK
TP = BATCH * SEQ
TL = DEC_BATCH * DEC_SEQ
T = TP + TL
TM = 256
N_TILES = T // TM
CTX_TILES = TP // TM
LAT_TILES_PER_SEQ = DEC_SEQ // TM
(C_AQ, C_AK, C_AV, C_BQ, C_BFF, C_BFB, C_BV, C_BG, C_CQ, C_CK, C_CV, C_DU) = range(12)
PM_BLOCKS = (C_AQ, C_AK, C_AV, C_CQ, C_CK, C_CV, C_DU)
PG_BLOCKS = (C_BQ, C_BFF, C_BFB, C_BV, C_BG)
(M_AQ, M_AK, M_AV, M_CQ, M_CK, M_CV, M_DU) = range(len(PM_BLOCKS))
(G_BQ, G_BFF, G_BFB, G_BV, G_BG) = range(len(PG_BLOCKS))
PM_W = len(PM_BLOCKS) * MIX_BLK
PG_W = len(PG_BLOCKS) * MIX_BLK
NA_SLAB_ROWS = 12
NA_SLAB = NA_SLAB_ROWS * GRID_W
LANES = 128
ROW_W = D_MODEL + LANES
EXPERT_PAIRS = ((0, 1), (0, 2), (0, 3), (1, 3), (2, 3), (2, 1))
N_BUCKETS = N_GROUPS * len(EXPERT_PAIRS)
BUCKET_ROWS = 32
TM_OUT = 512
TM_MOE = 384
MAX_TILES = -(-T // TM_MOE) + N_BUCKETS
N_SLOTS = MAX_TILES * TM_MOE
VMEM_LIMIT = 56 * 1024 * 1024


def _cparams(sem):
    return pltpu.CompilerParams(dimension_semantics=sem, vmem_limit_bytes=VMEM_LIMIT)


def _head_lanes(width=MIX_BLK):
    return lax.broadcasted_iota(jnp.int32, (1, width), 1)


def _lane_range(lane, lo, n):
    return (lane >= lo) & (lane < lo + n)


def _same_head_matrix():
    r = lax.broadcasted_iota(jnp.int32, (MIX_BLK, MIX_BLK), 0) // HEAD_DIM
    c = lax.broadcasted_iota(jnp.int32, (MIX_BLK, MIX_BLK), 1) // HEAD_DIM
    return r == c


def _bf16_pieces(x, n):
    pieces = []
    for _ in range(n):
        piece = x.astype(BF16)
        pieces.append(piece)
        x = x - piece.astype(F32)
    return pieces


def _select_sum_left(onehot_bf16, x):
    return sum(jnp.dot(onehot_bf16, piece, preferred_element_type=F32) for piece in _bf16_pieces(x, 3))


def _select_sum_right(x, onehot_bf16):
    return sum(jnp.dot(piece, onehot_bf16, preferred_element_type=F32) for piece in _bf16_pieces(x, 3))


def _head_mean_square(o):
    ones = jnp.where(_same_head_matrix(), 1.0, 0.0).astype(BF16)
    return _select_sum_right(o * o, ones) * (1.0 / HEAD_DIM)


def _mod_row(i):
    return jnp.where(i < CTX_TILES, 0, 1 + (i - CTX_TILES) // LAT_TILES_PER_SEQ)


def _mod_kernel(c_ref, w_ref, b_ref, o_ref):
    w = w_ref[0]
    for r in range(3):
        c = c_ref[r]
        s = c * jax.nn.sigmoid(c)
        o_ref[0, r:r + 1, :] = jnp.sum(s * w, axis=0, keepdims=True) + b_ref[0]


def modulation(c_rows, ada_w, ada_b):
    nt = 768
    n_out = 6 * D_MODEL
    return pl.pallas_call(
        _mod_kernel,
        grid=(DEPTH, n_out // nt),
        in_specs=[
            pl.BlockSpec((3, D_MODEL, 1), lambda l, j: (0, 0, 0)),
            pl.BlockSpec((1, D_MODEL, nt), lambda l, j: (l, 0, j)),
            pl.BlockSpec((1, 1, nt), lambda l, j: (l, 0, j)),
        ],
        out_specs=pl.BlockSpec((1, 3, nt), lambda l, j: (l, 0, j)),
        out_shape=jax.ShapeDtypeStruct((DEPTH, 3, n_out), F32),
        compiler_params=_cparams(("arbitrary", "arbitrary")),
        name="modulation",
    )(c_rows[:, :, None], ada_w, ada_b[:, None, :])


def _is_ctx_tile():
    return pl.program_id(0) < CTX_TILES


def _ctx_tile(i, *_):
    return (jnp.minimum(i, CTX_TILES - 1), 0)


def _lat_tile(i, *_):
    return (jnp.maximum(i - CTX_TILES, 0), 0)


def _proj_kernel(xc_ref, xl_ref, mod_ref, g_ref, w_ref, cos_ref, sa_ref, sb_ref, pm_ref, pg_ref, *cache_refs):
    x = jnp.where(_is_ctx_tile(), xc_ref[...], xl_ref[...])
    _proj_body(x, mod_ref, g_ref, w_ref, cos_ref, sa_ref, sb_ref, pm_ref, pg_ref, cache_refs)


def _proj_after_moe_kernel(dest_ref, y_ref, x1_ref, modp_ref, mod_ref, g_ref, w_ref, cos_ref, sa_ref, sb_ref,
                           *rest):
    x2_ref, pm_ref, pg_ref = rest[4:7]
    cache_refs, (buf_ref, sem) = rest[7:11], rest[11:]
    x2 = _moe_residual(dest_ref, y_ref, buf_ref, sem, x1_ref, modp_ref)
    x2_ref[...] = x2
    _proj_body(x2, mod_ref, g_ref, w_ref, cos_ref, sa_ref, sb_ref, pm_ref, pg_ref, cache_refs)


def _proj_body(x, mod_ref, g_ref, w_ref, cos_ref, sa_ref, sb_ref, pm_ref, pg_ref, cache_refs):
    ms = jnp.mean(x * x, axis=-1, keepdims=True)
    mod = mod_ref[0]
    h = x * lax.rsqrt(ms + EPS) * g_ref[...] * (1.0 + mod[1:2]) + mod[0:1]
    p = jnp.dot(h.astype(BF16), w_ref[0], preferred_element_type=F32)
    t = p[:, :2 * MIX_BLK]
    pm_ref[:, :2 * MIX_BLK] = (t * cos_ref[...] + pltpu.roll(t, 1, 1) * sa_ref[...]
                               + pltpu.roll(t, 2 * MIX_BLK - 1, 1) * sb_ref[...]).astype(BF16)
    pm_ref[:, 2 * MIX_BLK:] = p[:, 2 * MIX_BLK:PM_W].astype(BF16)
    pg_ref[...] = p[:, PM_W:]

    @pl.when(_is_ctx_tile())
    def _():
        for ref, col in zip(cache_refs, (M_AK, M_AV, M_CK, M_CV)):
            for hd in range(HEADS):
                lo = col * MIX_BLK + hd * HEAD_DIM
                ref[0, 0, hd] = p[:, lo:lo + HEAD_DIM]
            if ref.shape[1] > 1:
                ref[0, 1:] = jnp.zeros((ref.shape[1] - 1,) + tuple(ref.shape[2:]), F32)


_CACHE_SHAPE = jax.ShapeDtypeStruct((BATCH, DEPTH, HEADS, SEQ, HEAD_DIM), F32)


def _cache_spec(layer):
    n_layers = DEPTH if layer == 0 else 1
    return pl.BlockSpec((1, n_layers, HEADS, SEQ, HEAD_DIM),
                        lambda i, *_: (jnp.minimum(i, CTX_TILES - 1), layer, 0, 0, 0))


def _rope_tables():
    nf = A_DIM // 4
    freqs = ROPE_BASE ** (-np.arange(nf, dtype=np.float64) / nf)
    pos = np.arange(DEC_SEQ)
    row = (pos // GRID_W).astype(np.float64)
    col = (pos % GRID_W).astype(np.float64)
    ang = np.concatenate([row[:, None] * freqs, col[:, None] * freqs], axis=-1)
    cos = np.repeat(np.cos(ang), 2, axis=-1)
    sin = np.repeat(np.sin(ang), 2, axis=-1)
    odd = (np.arange(A_DIM) % 2 == 1)[None, :]
    sin_from_left = np.where(odd, sin, 0.0)
    sin_from_right = np.where(odd, 0.0, -sin)
    reps = 2 * MIX_BLK // A_DIM
    ident = (np.ones((TM, 2 * MIX_BLK)), np.zeros((TM, 2 * MIX_BLK)))
    return tuple(jnp.asarray(np.concatenate([np.tile(t, (1, reps)), tail], axis=0), F32)
                 for t, tail in ((cos, ident[0]), (sin_from_left, ident[1]), (sin_from_right, ident[1])))


def _rope_block(i):
    return (jnp.where(i < CTX_TILES, LAT_TILES_PER_SEQ, (i - CTX_TILES) % LAT_TILES_PER_SEQ), 0)


def projection_after_moe(dest, y_slots, x1, mod_prev, mod, layer, norm_g, w_in_bf16, rope, caches):
    rope_spec = pl.BlockSpec((TM, 2 * MIX_BLK), lambda i, d: _rope_block(i))
    mod_spec = pl.BlockSpec((1, 6, D_MODEL), lambda i, d: (_mod_row(i), 0, 0))
    n_in = 10
    return pl.pallas_call(
        _proj_after_moe_kernel,
        grid_spec=pltpu.PrefetchScalarGridSpec(
            num_scalar_prefetch=1,
            grid=(N_TILES,),
            in_specs=[pl.BlockSpec(memory_space=pl.ANY),
                      pl.BlockSpec((TM, D_MODEL), lambda i, d: (i, 0)),
                      mod_spec, mod_spec,
                      pl.BlockSpec((1, D_MODEL), lambda i, d: (0, 0)),
                      pl.BlockSpec((1, D_MODEL, PROJ_W), lambda i, d: (layer, 0, 0)),
                      rope_spec, rope_spec, rope_spec] + [pl.BlockSpec(memory_space=pl.ANY)] * 4,
            out_specs=[pl.BlockSpec((TM, D_MODEL), lambda i, d: (i, 0)),
                       pl.BlockSpec((TM, PM_W), lambda i, d: (i, 0)),
                       pl.BlockSpec((TM, PG_W), lambda i, d: (i, 0))] + [_cache_spec(layer)] * 4,
            scratch_shapes=_GATHER_SCRATCH,
        ),
        out_shape=[jax.ShapeDtypeStruct((T, D_MODEL), F32), jax.ShapeDtypeStruct((T, PM_W), BF16),
                   jax.ShapeDtypeStruct((T, PG_W), F32)] + [_CACHE_SHAPE] * 4,
        input_output_aliases={n_in + n: 3 + n for n in range(4)},
        compiler_params=_cparams(("arbitrary",)),
        name="projection_after_moe",
    )(dest, y_slots, x1, mod_prev, mod, norm_g[None, :], w_in_bf16, *rope, *caches)


def projection(x_ctx, x_lat, mod, norm_g, w_in_bf16, rope):
    layer = 0
    rope_spec = pl.BlockSpec((TM, 2 * MIX_BLK), _rope_block)
    return pl.pallas_call(
        _proj_kernel,
        grid=(N_TILES,),
        in_specs=[
            pl.BlockSpec((TM, D_MODEL), _ctx_tile),
            pl.BlockSpec((TM, D_MODEL), _lat_tile),
            pl.BlockSpec((1, 6, D_MODEL), lambda i: (_mod_row(i), 0, 0)),
            pl.BlockSpec((1, D_MODEL), lambda i: (0, 0)),
            pl.BlockSpec((1, D_MODEL, PROJ_W), lambda i: (layer, 0, 0)),
            rope_spec, rope_spec, rope_spec,
        ],
        out_specs=[pl.BlockSpec((TM, PM_W), lambda i: (i, 0)), pl.BlockSpec((TM, PG_W), lambda i: (i, 0))]
        + [_cache_spec(layer)] * 4,
        out_shape=[jax.ShapeDtypeStruct((T, PM_W), BF16), jax.ShapeDtypeStruct((T, PG_W), F32)]
        + [_CACHE_SHAPE] * 4,
        compiler_params=_cparams(("arbitrary",)),
        name="projection",
    )(x_ctx, x_lat, mod, norm_g[None, :], w_in_bf16, *rope)


LOG2_E = 1.4426950408889634


def _exp2_rows(s):
    e = jnp.exp2(s - jnp.max(s, axis=-1, keepdims=True))
    return e, 1.0 / jnp.sum(e, axis=-1, keepdims=True)


def _attn_kernel(lam_ref, q_ref, k_ref, v_ref, *rest, n_maps, post_scale, with_cache):
    if with_cache:
        kc_ref, vc_ref, g_ref, o_ref, kt_ref, vb_ref = rest
    else:
        g_ref, o_ref, kt_ref, vb_ref = rest

    @pl.when(pl.program_id(1) == 0)
    def _():
        k = k_ref[...].astype(F32)
        v = v_ref[...]
        if with_cache:
            k = jnp.concatenate([_cache_heads_on_lanes(kc_ref), k], axis=0)
            v = jnp.concatenate([_cache_heads_on_lanes(vc_ref).astype(BF16), v], axis=0)
        kt_ref[...] = k.T.astype(BF16)
        vb_ref[...] = v

    lane = _head_lanes()
    map_dim = HEAD_DIM // n_maps
    q = q_ref[...].astype(F32) * (map_dim ** -0.5 * LOG2_E)
    kt = kt_ref[...]
    vb = vb_ref[...]
    o = jnp.zeros(q.shape, F32)
    for h in range(HEADS):
        parts = []
        for j in range(n_maps):
            qm = jnp.where(_lane_range(lane, h * HEAD_DIM + j * map_dim, map_dim), q, 0.0)
            parts.append(_exp2_rows(jnp.dot(qm.astype(BF16), kt, preferred_element_type=F32)))
        w = parts[0][0] * parts[0][1]
        if n_maps == 2:
            w = w - parts[1][0] * (lam_ref[0] * parts[1][1])
        oh = jnp.dot(w.astype(BF16), vb, preferred_element_type=F32)
        o = jnp.where(_lane_range(lane, h * HEAD_DIM, HEAD_DIM), oh, o)
    if n_maps == 2:
        o = o * lax.rsqrt(_head_mean_square(o) + EPS) * g_ref[...] * post_scale
    o_ref[...] = o


def _cache_block_spec(layer):
    return pl.BlockSpec((1, 1, HEADS, PAST_LEN, HEAD_DIM), lambda b, i: (b, layer, 0, 0, 0))


def _cache_heads_on_lanes(ref):
    return jnp.concatenate([ref[0, 0, h] for h in range(HEADS)], axis=1)


def attention(p, row_blk0, cols, n_seq, seq_len, lam, norm_g, *, n_maps, post_scale, cache=None, layer=0):
    nb = seq_len // TM
    kv_len = seq_len + (PAST_LEN if cache is not None else 0)
    kern = functools.partial(_attn_kernel, n_maps=n_maps, post_scale=post_scale, with_cache=cache is not None)
    kv_spec = lambda col: pl.BlockSpec((seq_len, MIX_BLK), lambda b, i: (row_blk0 + b, col))
    cache_specs = [_cache_block_spec(layer)] * 2 if cache is not None else []
    return pl.pallas_call(
        kern,
        grid=(n_seq, nb),
        in_specs=[
            pl.BlockSpec(memory_space=pltpu.SMEM),
            pl.BlockSpec((TM, MIX_BLK), lambda b, i: ((row_blk0 + b) * nb + i, cols[0])),
            kv_spec(cols[1]), kv_spec(cols[2]), *cache_specs,
            pl.BlockSpec((1, MIX_BLK), lambda b, i: (0, 0)),
        ],
        out_specs=pl.BlockSpec((TM, MIX_BLK), lambda b, i: (b * nb + i, 0)),
        out_shape=jax.ShapeDtypeStruct((n_seq * seq_len, MIX_BLK), F32),
        scratch_shapes=[pltpu.VMEM((MIX_BLK, kv_len), BF16), pltpu.VMEM((kv_len, MIX_BLK), BF16)],
        compiler_params=_cparams(("arbitrary", "arbitrary")),
        name="attention_%dmap_%d" % (n_maps, seq_len),
    )(lam, p, p, p, *(cache or ()), norm_g)


def _na_slab_start(i):
    return jnp.clip(i - 1, 0, GRID_H // 4 - NA_SLAB_ROWS // 4)


def _na_kernel(q_ref, k_ref, v_ref, kc_ref, vc_ref, bias_ref, o_ref):
    i = pl.program_id(1)
    start = pl.multiple_of(_na_slab_start(i) * TM, TM)
    ks_t = k_ref[pl.ds(start, NA_SLAB), :].astype(F32).T.astype(BF16)
    vs = v_ref[pl.ds(start, NA_SLAB), :]
    kc_t = _cache_heads_on_lanes(kc_ref).T.astype(BF16)
    vc = _cache_heads_on_lanes(vc_ref).astype(BF16)
    q = q_ref[...].astype(F32) * (HEAD_DIM ** -0.5)
    lane = _head_lanes()
    o = jnp.zeros(q.shape, F32)
    for h in range(HEADS):
        in_head = _lane_range(lane, h * HEAD_DIM, HEAD_DIM)
        qm = jnp.where(in_head, q, 0.0).astype(BF16)
        s_loc = jnp.dot(qm, ks_t, preferred_element_type=F32) + bias_ref[0, h]
        s_ctx = jnp.dot(qm, kc_t, preferred_element_type=F32)
        m = jnp.maximum(jnp.max(s_loc, axis=-1, keepdims=True), jnp.max(s_ctx, axis=-1, keepdims=True))
        e_loc = jnp.exp(s_loc - m)
        e_ctx = jnp.exp(s_ctx - m)
        den = jnp.sum(e_loc, axis=-1, keepdims=True) + jnp.sum(e_ctx, axis=-1, keepdims=True)
        oh = (jnp.dot(e_loc.astype(BF16), vs, preferred_element_type=F32)
              + jnp.dot(e_ctx.astype(BF16), vc, preferred_element_type=F32)) / den
        o = jnp.where(in_head, oh, o)
    o_ref[...] = o


def _na_bias_tables(rpb):
    n_dr, n_dc = 2 * NA_WIN_H - 1, 2 * NA_WIN_W - 1
    cq = np.arange(GRID_W)[:, None]
    ck = np.arange(GRID_W)[None, :]
    wc0 = np.clip(cq - NA_WIN_W // 2, 0, GRID_W - NA_WIN_W)
    col_ok = (ck >= wc0) & (ck < wc0 + NA_WIN_W)
    col_pick = np.clip(ck - cq + NA_WIN_W - 1, 0, n_dc - 1)[..., None] == np.arange(n_dc)
    by_col = jnp.einsum("hab,qcb->haqc", rpb.astype(F32), jnp.asarray(col_pick, F32), precision=HIGHEST)
    margin = 4
    by_col = jnp.pad(by_col.transpose(0, 2, 1, 3), ((0, 0), (0, 0), (margin, margin), (0, 0)))
    by_col = by_col.reshape(HEADS, GRID_W, (n_dr + 2 * margin) * GRID_W)
    pieces, row_ok = [], []
    for tile in (0, 1, GRID_H // 4 - 1):
        slab0 = int(np.clip(tile - 1, 0, GRID_H // 4 - NA_SLAB_ROWS // 4)) * 4
        rq = tile * 4 + np.arange(4)
        rk = (slab0 + np.arange(NA_SLAB) // GRID_W)[None, :]
        wr0 = np.clip(rq - NA_WIN_H // 2, 0, GRID_H - NA_WIN_H)[:, None]
        row_ok.append((rk >= wr0) & (rk < wr0 + NA_WIN_H))
        for r in rq:
            first = slab0 - int(r) + NA_WIN_H - 1 + margin
            assert 0 <= first and first + NA_SLAB_ROWS <= n_dr + 2 * margin
            pieces.append(by_col[:, :, first * GRID_W:first * GRID_W + NA_SLAB])
    table = jnp.stack(pieces).reshape(3, 4, HEADS, GRID_W, NA_SLAB).transpose(0, 2, 1, 3, 4)
    valid = np.stack(row_ok)[:, None, :, None, :] & np.tile(col_ok, (1, NA_SLAB_ROWS))[None, None, None]
    table = jnp.where(jnp.asarray(valid), table, NEG_BIG)
    return table.reshape(3, HEADS, TM, NA_SLAB)


def na_latent(p, kc, vc, layer, bias):
    n_t = LAT_TILES_PER_SEQ
    seq_blk0 = TP // DEC_SEQ

    def bias_idx(b, i):
        return (jnp.minimum(i, 1) + i // (n_t - 1), 0, 0, 0)

    return pl.pallas_call(
        _na_kernel,
        grid=(DEC_BATCH, n_t),
        in_specs=[
            pl.BlockSpec((TM, MIX_BLK), lambda b, i: (CTX_TILES + b * n_t + i, M_CQ)),
            pl.BlockSpec((DEC_SEQ, MIX_BLK), lambda b, i: (seq_blk0 + b, M_CK)),
            pl.BlockSpec((DEC_SEQ, MIX_BLK), lambda b, i: (seq_blk0 + b, M_CV)),
            _cache_block_spec(layer), _cache_block_spec(layer),
            pl.BlockSpec((1, HEADS, TM, NA_SLAB), bias_idx),
        ],
        out_specs=pl.BlockSpec((TM, MIX_BLK), lambda b, i: (b * n_t + i, 0)),
        out_shape=jax.ShapeDtypeStruct((TL, MIX_BLK), F32),
        compiler_params=_cparams(("arbitrary", "arbitrary")),
        name="na_latent",
    )(p, p, p, kc, vc, bias)


MAX_EXPONENT = 80.0


def _hgrn_direction(q_ref, f_ref, v_ref, lb, st_ref, o_ref, reverse):
    n_ch = TM // B_CHUNK
    r_idx = lax.broadcasted_iota(jnp.int32, (TM, TM), 0)
    c_idx = lax.broadcasted_iota(jnp.int32, (TM, TM), 1)
    tri = (c_idx >= r_idx) if reverse else (c_idx <= r_idx)
    zq = q_ref[...]
    q = zq * jax.nn.sigmoid(zq)
    z = f_ref[...]
    logf = jnp.log(lb + (1.0 - lb) * jax.nn.sigmoid(z))
    kk = (1.0 - lb) * jax.nn.sigmoid(-z)
    b = _select_sum_left(jnp.where(tri, 1.0, 0.0).astype(BF16), logf)
    b3 = b.reshape(n_ch, B_CHUNK, MIX_BLK)
    mid = B_CHUNK // 2 if reverse else B_CHUNK // 2 - 1
    q_in = (q.reshape(b3.shape) * jnp.exp(b3 - b3[:, mid:mid + 1, :])).reshape(TM, MIX_BLK)
    q_dec = (q * jnp.exp(b)).astype(BF16)
    b_t = b.T
    kk_t = kk.T
    far = 0 if reverse else TM - 1
    b_far = b_t[:, far:far + 1]
    k_dec_t = (kk_t * jnp.exp(b_far - b_t)).astype(BF16)
    vb = v_ref[...].astype(BF16)
    st = st_ref[...]
    o_state = jnp.dot(q_dec, st.astype(BF16), preferred_element_type=F32)
    kv = jnp.dot(k_dec_t, vb, preferred_element_type=F32)
    st_ref[...] = st * jnp.exp(b_far) + jnp.where(_same_head_matrix(), kv, 0.0)
    lane = _head_lanes()
    token = lax.broadcasted_iota(jnp.int32, (1, TM), 1)
    local = lax.broadcasted_iota(jnp.int32, (HEADS * B_CHUNK, 1), 0) % B_CHUNK
    heads = [_lane_range(lane, h * HEAD_DIM, HEAD_DIM) for h in range(HEADS)]
    for c in range(n_ch):
        rows = slice(c * B_CHUNK, (c + 1) * B_CHUNK)
        ref = b_t[:, c * B_CHUNK + mid:c * B_CHUNK + mid + 1]
        k_c_t = (kk_t * jnp.exp(jnp.minimum(ref - b_t, MAX_EXPONENT))).astype(BF16)
        q_c = q_in[rows, :]
        lhs = jnp.concatenate([jnp.where(in_head, q_c, 0.0) for in_head in heads], axis=0)
        a = jnp.dot(lhs.astype(BF16), k_c_t, preferred_element_type=F32)
        t_abs = c * B_CHUNK + local
        a = jnp.where((token >= t_abs) if reverse else (token <= t_abs), a, 0.0)
        res = jnp.dot(a.astype(BF16), vb, preferred_element_type=F32)
        o_c = o_state[rows, :]
        for h, in_head in enumerate(heads):
            o_c = o_c + jnp.where(in_head, res[h * B_CHUNK:(h + 1) * B_CHUNK, :], 0.0)
        o_ref[rows, :] = o_c


def _hgrn_kernel(qf_ref, ff_ref, vf_ref, qb_ref, fb_ref, vb_ref, lb_ref, s0_ref,
                 of_ref, ob_ref, s_ref, stf_ref, stb_ref, *, has_s0):
    j = pl.program_id(1)

    @pl.when(j == 0)
    def _():
        if has_s0:
            stf_ref[...] = s0_ref[0, 0]
            stb_ref[...] = s0_ref[0, 1]
        else:
            stf_ref[...] = jnp.zeros((MIX_BLK, MIX_BLK), F32)
            stb_ref[...] = jnp.zeros((MIX_BLK, MIX_BLK), F32)

    lb = lb_ref[...]
    _hgrn_direction(qf_ref, ff_ref, vf_ref, lb[0:1], stf_ref, of_ref, False)
    _hgrn_direction(qb_ref, fb_ref, vb_ref, lb[1:2], stb_ref, ob_ref, True)

    @pl.when(j == pl.num_programs(1) - 1)
    def _():
        for d, st_ref in enumerate((stf_ref, stb_ref)):
            s = st_ref[...]
            for hd in range(HEADS):
                lo = hd * HEAD_DIM
                s_ref[0, d, hd] = s[lo:lo + HEAD_DIM, lo:lo + HEAD_DIM]


def hgrn(p, row_tile0, n_seq, seq_len, lb, s0):
    nb = seq_len // TM
    has_s0 = s0 is not None
    if s0 is None:
        s0 = jnp.zeros((1, 2, MIX_BLK, MIX_BLK), F32)

    def fwd(col):
        return pl.BlockSpec((TM, MIX_BLK), lambda s, j: (row_tile0 + s * nb + j, col))

    def bwd(col):
        return pl.BlockSpec((TM, MIX_BLK), lambda s, j: (row_tile0 + s * nb + nb - 1 - j, col))

    state_spec = pl.BlockSpec((1, 2, MIX_BLK, MIX_BLK), lambda s, j: (s if has_s0 else 0, 0, 0, 0))
    out_rows = n_seq * seq_len
    return pl.pallas_call(
        functools.partial(_hgrn_kernel, has_s0=has_s0),
        grid=(n_seq, nb),
        in_specs=[fwd(G_BQ), fwd(G_BFF), fwd(G_BV), bwd(G_BQ), bwd(G_BFB), bwd(G_BV),
                  pl.BlockSpec((2, MIX_BLK), lambda s, j: (0, 0)), state_spec],
        out_specs=[
            pl.BlockSpec((TM, MIX_BLK), lambda s, j: (s * nb + j, 0)),
            pl.BlockSpec((TM, MIX_BLK), lambda s, j: (s * nb + nb - 1 - j, 0)),
            pl.BlockSpec((1, 2, HEADS, HEAD_DIM, HEAD_DIM), lambda s, j: (s, 0, 0, 0, 0)),
        ],
        out_shape=[jax.ShapeDtypeStruct((out_rows, MIX_BLK), F32),
                   jax.ShapeDtypeStruct((out_rows, MIX_BLK), F32),
                   jax.ShapeDtypeStruct((n_seq, 2, HEADS, HEAD_DIM, HEAD_DIM), F32)],
        scratch_shapes=[pltpu.VMEM((MIX_BLK, MIX_BLK), F32), pltpu.VMEM((MIX_BLK, MIX_BLK), F32)],
        compiler_params=_cparams(("arbitrary", "arbitrary")),
        name="hgrn_%d" % seq_len,
    )(p, p, p, p, p, p, lb, s0)


def _state_to_blockdiag(s):
    eye = jnp.eye(HEADS, dtype=F32)
    full = s.astype(F32)[:, :, :, :, None, :] * eye[None, None, :, None, :, None]
    return full.reshape(s.shape[0], 2, MIX_BLK, MIX_BLK)


def _fft_kernel(u_ref, c64_ref, s64_ref, cl_ref, sl_ref, o_ref, a_ref, b_ref, *, norm):
    @pl.when(pl.program_id(1) == 0)
    def _():
        u = u_ref[...]
        a_ref[...] = jnp.dot(u, c64_ref[...], preferred_element_type=F32).astype(BF16)
        b_ref[...] = jnp.dot(u, s64_ref[...], preferred_element_type=F32).astype(BF16)

    o_ref[...] = (jnp.dot(cl_ref[...], a_ref[...], preferred_element_type=F32)
                  - jnp.dot(sl_ref[...], b_ref[...], preferred_element_type=F32)) * norm


def _dft_tables(n):
    k = np.arange(n)
    ang = 2.0 * np.pi * ((k[:, None] * k[None, :]) % n) / n
    return np.cos(ang), np.sin(ang)


def _dft_constants(seq_len):
    c64, s64 = _dft_tables(HEAD_DIM)
    eye = np.eye(HEADS)
    cl, sl = _dft_tables(seq_len)
    as_bf16 = lambda a: jnp.asarray(a, F32).astype(BF16)
    return as_bf16(np.kron(eye, c64)), as_bf16(np.kron(eye, s64)), as_bf16(cl), as_bf16(sl)


def fourier_mix(p, row_blk0, n_seq, seq_len, consts):
    c64, s64, cl, sl = consts
    nb = seq_len // TM
    norm = 1.0 / math.sqrt(seq_len * HEAD_DIM)
    return pl.pallas_call(
        functools.partial(_fft_kernel, norm=norm),
        grid=(n_seq, nb),
        in_specs=[
            pl.BlockSpec((seq_len, MIX_BLK), lambda s, i: (row_blk0 + s, M_DU)),
            pl.BlockSpec((MIX_BLK, MIX_BLK), lambda s, i: (0, 0)),
            pl.BlockSpec((MIX_BLK, MIX_BLK), lambda s, i: (0, 0)),
            pl.BlockSpec((TM, seq_len), lambda s, i: (i, 0)),
            pl.BlockSpec((TM, seq_len), lambda s, i: (i, 0)),
        ],
        out_specs=pl.BlockSpec((TM, MIX_BLK), lambda s, i: (s * nb + i, 0)),
        out_shape=jax.ShapeDtypeStruct((n_seq * seq_len, MIX_BLK), F32),
        scratch_shapes=[pltpu.VMEM((seq_len, MIX_BLK), BF16), pltpu.VMEM((seq_len, MIX_BLK), BF16)],
        compiler_params=_cparams(("arbitrary", "arbitrary")),
        name="fourier_%d" % seq_len,
    )(p, c64, s64, cl, sl)


def _route(logits_t, rb):
    per = N_EXPERTS // N_GROUPS
    score = [jax.nn.sigmoid(logits_t[e:e + 1, :]) for e in range(N_EXPERTS)]
    sel = [score[e] + rb[e:e + 1, :] for e in range(N_EXPERTS)]
    gscore = []
    for g in range(N_GROUPS):
        vals = sel[g * per:(g + 1) * per]
        best = None
        for a in range(per):
            for b in range(a + 1, per):
                pair = vals[a] + vals[b]
                best = pair if best is None else jnp.maximum(best, pair)
        gscore.append(best)
    chosen = []
    for g in range(N_GROUPS):
        ok = None
        for j in range(N_GROUPS):
            if j == g:
                continue
            cond = gscore[g] > gscore[j] if j < g else gscore[g] >= gscore[j]
            ok = cond if ok is None else ok & cond
        chosen.append(ok)
    picked = []
    for e in range(N_EXPERTS):
        g = e // per
        rank = jnp.zeros_like(sel[e])
        for j in range(g * per, (g + 1) * per):
            if j == e:
                continue
            ahead = sel[j] >= sel[e] if j < e else sel[j] > sel[e]
            rank = rank + jnp.where(ahead, 1.0, 0.0)
        picked.append(chosen[g] & (rank < 2.0))
    wsum = jnp.zeros_like(score[0])
    for e in range(N_EXPERTS):
        wsum = wsum + jnp.where(picked[e], score[e], 0.0)
    bucket = jnp.zeros_like(wsum)
    w_a = jnp.zeros_like(wsum)
    w_b = jnp.zeros_like(wsum)
    for g in range(N_GROUPS):
        for n, (a, b) in enumerate(EXPERT_PAIRS):
            hit = picked[g * per + a] & picked[g * per + b]
            bucket = jnp.where(hit, float(g * len(EXPERT_PAIRS) + n), bucket)
            w_a = jnp.where(hit, score[g * per + a] / wsum, w_a)
            w_b = jnp.where(hit, score[g * per + b] / wsum, w_b)
    return bucket, w_a, w_b


def _out_kernel(*refs):
    streams, rest = refs[:12], refs[12:]
    (bg_ref, mod_ref, hg_ref, w_ref, g2_ref, rw_ref, rb_ref,
     x1_ref, h2_ref, bucket_ref, rank_ref, counts_ref, run_ref) = rest
    is_ctx = pl.program_id(0) < TP // TM_OUT
    x, o_a, o_f, o_b, o_c, o_d = (jnp.where(is_ctx, streams[2 * n][...], streams[2 * n + 1][...])
                                  for n in range(6))

    @pl.when(pl.program_id(0) == 0)
    def _():
        run_ref[...] = jnp.zeros(run_ref.shape, F32)

    mod = mod_ref[0]
    hb = o_f + o_b
    zg = bg_ref[...]
    hb = hb * lax.rsqrt(_head_mean_square(hb) + EPS) * hg_ref[...] * (zg * jax.nn.sigmoid(zg))
    parts = (o_a, hb, o_c, o_d)
    mixed = jnp.zeros((TM_OUT, D_MODEL), F32)
    for n, part in enumerate(parts):
        mixed = mixed + jnp.dot(part.astype(BF16), w_ref[0, n * MIX_BLK:(n + 1) * MIX_BLK, :],
                                preferred_element_type=F32)
    x1 = x + mod[2:3] * mixed
    x1_ref[...] = x1
    ms = jnp.mean(x1 * x1, axis=-1, keepdims=True)
    h2 = x1 * lax.rsqrt(ms + EPS) * g2_ref[...] * (1.0 + mod[4:5]) + mod[3:4]
    rw = rw_ref[...]
    r = sum(jnp.dot(piece, rw, preferred_element_type=F32) for piece in _bf16_pieces(h2, 2))
    bucket, w_a, w_b = _route((r[:, :LANES] + r[:, LANES:]).T, rb_ref[...])
    h2_ref[:, :D_MODEL] = h2
    h2_ref[:, D_MODEL:] = jnp.concatenate([w_a, w_b, jnp.zeros((LANES - 2, TM_OUT), F32)], axis=0).T
    onehot = jnp.where(lax.broadcasted_iota(jnp.int32, (BUCKET_ROWS, 1), 0).astype(F32) == bucket, 1.0, 0.0)
    s_idx = lax.broadcasted_iota(jnp.int32, (TM_OUT, TM_OUT), 0)
    t_idx = lax.broadcasted_iota(jnp.int32, (TM_OUT, TM_OUT), 1)
    prefix = jnp.dot(onehot.astype(BF16), jnp.where(s_idx <= t_idx, 1.0, 0.0).astype(BF16),
                     preferred_element_type=F32)
    run = run_ref[...]
    rank = jnp.sum(onehot * (prefix - 1.0 + run[:, 0:1]), axis=0, keepdims=True)
    run = run + jnp.sum(onehot, axis=1, keepdims=True)
    run_ref[...] = run
    bucket_ref[...] = bucket.astype(jnp.int32)
    rank_ref[...] = rank.astype(jnp.int32)
    counts_ref[...] = run


def out_and_route(x_pair, x_is_combined, mixer_pairs, p, mod, layer, hgrn_g, w_out_bf16, norm2_g, router_pieces,
                  router_b):
    n_ctx = TP // TM_OUT
    ctx_tile = lambda i: (jnp.minimum(i, n_ctx - 1), 0)
    lat_tile = lambda i: (jnp.maximum(i - n_ctx, 0), 0)
    mod_row = lambda i: jnp.where(i < n_ctx, 0, 1 + (i - n_ctx) // (DEC_SEQ // TM_OUT))
    tile = lambda w: pl.BlockSpec((TM_OUT, w), lambda i: (i, 0))
    full = lambda r, c: pl.BlockSpec((r, c), lambda i: (0, 0))
    stream_specs = [pl.BlockSpec((TM_OUT, D_MODEL), ctx_tile),
                    pl.BlockSpec((TM_OUT, D_MODEL),
                                 (lambda i: (jnp.maximum(i, n_ctx), 0)) if x_is_combined else lat_tile)]
    stream_args = list(x_pair)
    for o_ctx, o_lat in mixer_pairs:
        stream_specs += [pl.BlockSpec((TM_OUT, MIX_BLK), ctx_tile), pl.BlockSpec((TM_OUT, MIX_BLK), lat_tile)]
        stream_args += [o_ctx, o_lat]
    return pl.pallas_call(
        _out_kernel,
        grid=(T // TM_OUT,),
        in_specs=stream_specs + [
            pl.BlockSpec((TM_OUT, MIX_BLK), lambda i: (i, G_BG)),
            pl.BlockSpec((1, 6, D_MODEL), lambda i: (mod_row(i), 0, 0)),
            full(1, MIX_BLK), pl.BlockSpec((1, D_MODEL, D_MODEL), lambda i: (layer, 0, 0)), full(1, D_MODEL),
            full(D_MODEL, 2 * LANES), full(N_EXPERTS, 1),
        ],
        out_specs=[tile(D_MODEL), tile(ROW_W), pl.BlockSpec((1, TM_OUT), lambda i: (0, i)),
                   pl.BlockSpec((1, TM_OUT), lambda i: (0, i)), full(BUCKET_ROWS, LANES)],
        out_shape=[jax.ShapeDtypeStruct((T, D_MODEL), F32),
                   jax.ShapeDtypeStruct((T, ROW_W), F32),
                   jax.ShapeDtypeStruct((1, T), jnp.int32),
                   jax.ShapeDtypeStruct((1, T), jnp.int32),
                   jax.ShapeDtypeStruct((BUCKET_ROWS, LANES), F32)],
        scratch_shapes=[pltpu.VMEM((BUCKET_ROWS, LANES), F32)],
        compiler_params=_cparams(("arbitrary",)),
        name="out_and_route",
    )(*stream_args, p, mod, jnp.tile(hgrn_g, HEADS)[None, :], w_out_bf16,
      norm2_g[None, :], router_pieces, router_b[:, None])


def _router_pieces(router_w):
    hi, lo = _bf16_pieces(router_w.astype(F32), 2)
    pad = lambda a: jnp.pad(a, ((0, 0), (0, LANES - N_EXPERTS)))
    return jnp.concatenate([pad(hi), pad(lo)], axis=1)


def routing_plan(bucket, rank, counts):
    counts = counts[:N_BUCKETS, 0].astype(jnp.int32)
    n_tiles = (counts + TM_MOE - 1) // TM_MOE
    tile_end = jnp.cumsum(n_tiles)
    tile_start = tile_end - n_tiles
    buckets = jnp.arange(N_BUCKETS, dtype=jnp.int32)
    start_of_token = jnp.sum(jnp.where(bucket[0][:, None] == buckets[None, :], tile_start[None, :], 0), axis=1)
    dest = start_of_token * TM_MOE + rank[0]
    tiles = jnp.arange(MAX_TILES, dtype=jnp.int32)
    valid = tiles < tile_end[-1]
    tile_bucket = jnp.sum((jnp.minimum(tiles, tile_end[-1] - 1)[:, None] >= tile_end[None, :]).astype(jnp.int32), axis=1)
    pair_a = np.array([a for a, _ in EXPERT_PAIRS], np.int32)
    pair_b = np.array([b for _, b in EXPERT_PAIRS], np.int32)
    per = N_EXPERTS // N_GROUPS
    exp_a = jnp.asarray((np.arange(N_BUCKETS) // len(EXPERT_PAIRS)) * per + np.tile(pair_a, N_GROUPS), jnp.int32)
    exp_b = jnp.asarray((np.arange(N_BUCKETS) // len(EXPERT_PAIRS)) * per + np.tile(pair_b, N_GROUPS), jnp.int32)
    pick = tile_bucket[:, None] == buckets[None, :]
    tile_a = jnp.sum(jnp.where(pick, exp_a[None, :], 0), axis=1)
    tile_b = jnp.sum(jnp.where(pick, exp_b[None, :], 0), axis=1)
    return dest.astype(jnp.int32), tile_a, tile_b, valid.astype(jnp.int32), (tile_end[-1:] - 1).astype(jnp.int32)


def _row_copy(src, src_row, dst, dst_row, sem):
    return pltpu.make_async_copy(src.at[pl.ds(src_row, 1), :], dst.at[pl.ds(dst_row, 1), :], sem)


def _scatter_kernel(dest_ref, h_ref, init_ref, o_ref, sem):
    del init_ref
    base = pl.program_id(0) * TM

    for r in range(TM):
        _row_copy(h_ref, r, o_ref, dest_ref[base + r], sem).start()
    pltpu.make_async_copy(h_ref, o_ref.at[pl.ds(0, TM), :], sem).wait()


def scatter_to_slots(h2, dest, slots):
    return pl.pallas_call(
        _scatter_kernel,
        grid_spec=pltpu.PrefetchScalarGridSpec(
            num_scalar_prefetch=1,
            grid=(N_TILES,),
            in_specs=[pl.BlockSpec((TM, ROW_W), lambda i, d: (i, 0)),
                      pl.BlockSpec(memory_space=pl.ANY)],
            out_specs=pl.BlockSpec(memory_space=pl.ANY),
            scratch_shapes=[pltpu.SemaphoreType.DMA(())],
        ),
        out_shape=jax.ShapeDtypeStruct((N_SLOTS, ROW_W), F32),
        input_output_aliases={2: 0},
        compiler_params=_cparams(("arbitrary",)),
        name="scatter_to_slots",
    )(dest, h2, slots)


def _moe_kernel(ta_ref, tb_ref, valid_ref, last_ref, h_ref, wga_ref, wua_ref, wda_ref, wgb_ref, wub_ref, wdb_ref,
                o_ref):
    del ta_ref, tb_ref, last_ref
    i = pl.program_id(0)

    @pl.when(valid_ref[i] == 1)
    def _():
        x = h_ref[:, :D_MODEL].astype(BF16)
        gates = h_ref[:, D_MODEL:]
        y = jnp.zeros((TM_MOE, D_MODEL), F32)
        for n, (wg, wu, wd) in enumerate(((wga_ref, wua_ref, wda_ref), (wgb_ref, wub_ref, wdb_ref))):
            a = jnp.dot(x, wg[0, 0].astype(BF16), preferred_element_type=F32)
            u = jnp.dot(x, wu[0, 0].astype(BF16), preferred_element_type=F32)
            z = a * jax.nn.sigmoid(a) * u * gates[:, n:n + 1]
            y = y + jnp.dot(z.astype(BF16), wd[0, 0].astype(BF16), preferred_element_type=F32)
        o_ref[...] = y

    @pl.when(valid_ref[i] == 0)
    def _():
        o_ref[...] = jnp.zeros((TM_MOE, D_MODEL), F32)


def moe(h_slots, tile_a, tile_b, valid, last, layer, wg, wu, wd):
    up_a = pl.BlockSpec((1, 1, D_MODEL, D_EXPERT), lambda i, ta, tb, v, last: (layer, ta[i], 0, 0))
    up_b = pl.BlockSpec((1, 1, D_MODEL, D_EXPERT), lambda i, ta, tb, v, last: (layer, tb[i], 0, 0))
    down_a = pl.BlockSpec((1, 1, D_EXPERT, D_MODEL), lambda i, ta, tb, v, last: (layer, ta[i], 0, 0))
    down_b = pl.BlockSpec((1, 1, D_EXPERT, D_MODEL), lambda i, ta, tb, v, last: (layer, tb[i], 0, 0))
    return pl.pallas_call(
        _moe_kernel,
        grid_spec=pltpu.PrefetchScalarGridSpec(
            num_scalar_prefetch=4,
            grid=(MAX_TILES,),
            in_specs=[pl.BlockSpec((TM_MOE, ROW_W), lambda i, ta, tb, v, last: (jnp.minimum(i, last[0]), 0)),
                      up_a, up_a, down_a, up_b, up_b, down_b],
            out_specs=pl.BlockSpec((TM_MOE, D_MODEL), lambda i, ta, tb, v, last: (i, 0)),
        ),
        out_shape=jax.ShapeDtypeStruct((N_SLOTS, D_MODEL), F32),
        compiler_params=_cparams(("arbitrary",)),
        name="moe",
    )(tile_a, tile_b, valid, last, h_slots, wg, wu, wd, wg, wu, wd)


def _gather_tile(dest_ref, y_ref, buf_ref, sem, tile, slot):
    for r in range(TM):
        pltpu.make_async_copy(y_ref.at[pl.ds(dest_ref[tile * TM + r], 1), :],
                              buf_ref.at[slot, pl.ds(r, 1), :], sem.at[slot]).start()


def _moe_residual(dest_ref, y_ref, buf_ref, sem, x1_ref, mod_ref):
    i = pl.program_id(0)
    slot = i % 2

    @pl.when(i == 0)
    def _():
        _gather_tile(dest_ref, y_ref, buf_ref, sem, 0, 0)

    @pl.when(i + 1 < pl.num_programs(0))
    def _():
        _gather_tile(dest_ref, y_ref, buf_ref, sem, i + 1, 1 - slot)

    pltpu.make_async_copy(y_ref.at[pl.ds(0, TM), :], buf_ref.at[slot], sem.at[slot]).wait()
    return x1_ref[...] + mod_ref[0][5:6] * buf_ref[slot]


def _final_kernel(dest_ref, y_ref, x1_ref, mod_ref, g_ref, oc_ref, ol_ref, buf_ref, sem):
    x2 = _moe_residual(dest_ref, y_ref, buf_ref, sem, x1_ref, mod_ref)
    ms = jnp.mean(x2 * x2, axis=-1, keepdims=True)
    y = x2 * lax.rsqrt(ms + EPS) * g_ref[...]

    @pl.when(_is_ctx_tile())
    def _():
        oc_ref[...] = y

    @pl.when(jnp.logical_not(_is_ctx_tile()))
    def _():
        ol_ref[...] = y


_GATHER_SCRATCH = [pltpu.VMEM((2, TM, D_MODEL), F32), pltpu.SemaphoreType.DMA((2,))]


def final_norm(dest, y_slots, x1, mod, final_g):
    return pl.pallas_call(
        _final_kernel,
        grid_spec=pltpu.PrefetchScalarGridSpec(
            num_scalar_prefetch=1,
            grid=(N_TILES,),
            in_specs=[pl.BlockSpec(memory_space=pl.ANY),
                      pl.BlockSpec((TM, D_MODEL), lambda i, d: (i, 0)),
                      pl.BlockSpec((1, 6, D_MODEL), lambda i, d: (_mod_row(i), 0, 0)),
                      pl.BlockSpec((1, D_MODEL), lambda i, d: (0, 0))],
            out_specs=[pl.BlockSpec((TM, D_MODEL), _ctx_tile), pl.BlockSpec((TM, D_MODEL), _lat_tile)],
            scratch_shapes=_GATHER_SCRATCH,
        ),
        out_shape=[jax.ShapeDtypeStruct((TP, D_MODEL), F32), jax.ShapeDtypeStruct((TL, D_MODEL), F32)],
        compiler_params=_cparams(("arbitrary",)),
        name="final_norm",
    )(dest, y_slots, x1, mod, final_g[None, :])


def kernel(x_prompt, x_sample, cache_diff_k, cache_diff_v, cache_na_k, cache_na_v, state_hgrn, c, c_ctx,
           norm1_g, norm2_g, ada_w, ada_b, w_in, w_out, diff_lambda, diff_subln_g, hgrn_lb_logits,
           hgrn_norm_g, na_rpb, router_w, router_b, moe_w_gate, moe_w_up, moe_w_down, final_norm_g):
    assert SEQ == TM and PAST_LEN == TM and DEC_SEQ % TM_OUT == 0 and TP % DEC_SEQ == 0
    x_pair = (x_prompt.reshape(TP, D_MODEL), x_sample.reshape(TL, D_MODEL))
    w_in_bf16 = jnp.concatenate([w_in[:, :, c * MIX_BLK:(c + 1) * MIX_BLK] for c in PM_BLOCKS + PG_BLOCKS],
                                axis=-1).astype(BF16)
    w_out_bf16 = w_out.astype(BF16)
    router_pieces = _router_pieces(router_w)
    mods = modulation(jnp.concatenate([c_ctx[None, :], c], axis=0), ada_w, ada_b)
    mods = mods.reshape(DEPTH, 3, 6, D_MODEL)
    lb_sm = jax.nn.softmax(hgrn_lb_logits.astype(F32), axis=0)
    lb_all = jnp.cumsum(lb_sm, axis=0) - lb_sm[0:1]
    rope = _rope_tables()
    dft_ctx = _dft_constants(SEQ)
    dft_lat = _dft_constants(DEC_SEQ)
    lat_blk0 = TP // DEC_SEQ
    states = []
    moe_state = None
    for l in range(DEPTH):
        if moe_state is None:
            pm, pg, *new_kv = projection(*x_pair, mods[l], norm1_g[l], w_in_bf16, rope)
        else:
            x, pm, pg, *new_kv = projection_after_moe(*moe_state, mods[l - 1], mods[l], l, norm1_g[l], w_in_bf16, rope,
                                                 new_kv)
            x_pair = (x, x)

        lq = diff_lambda[l].astype(F32)
        lam_init = 0.8 - 0.6 * math.exp(-0.3 * l)
        lam = (jnp.exp(jnp.sum(lq[0] * lq[1])) - jnp.exp(jnp.sum(lq[2] * lq[3])) + lam_init).reshape(1)
        subln = jnp.tile(diff_subln_g[l], HEADS)[None, :]
        diff = functools.partial(attention, pm, cols=(M_AQ, M_AK, M_AV), lam=lam, norm_g=subln, n_maps=2,
                                 post_scale=1.0 - lam_init)
        oa_ctx = diff(row_blk0=0, n_seq=BATCH, seq_len=SEQ)
        oa_lat = diff(row_blk0=lat_blk0, n_seq=DEC_BATCH, seq_len=DEC_SEQ,
                      cache=(cache_diff_k, cache_diff_v), layer=l)

        of_ctx, ob_ctx, st_ctx = hgrn(pg, 0, BATCH, SEQ, lb_all[l], None)
        of_lat, ob_lat, _ = hgrn(pg, CTX_TILES, DEC_BATCH, DEC_SEQ, lb_all[l],
                                 _state_to_blockdiag(state_hgrn[:, l]))

        oc_ctx = attention(pm, 0, (M_CQ, M_CK, M_CV), BATCH, SEQ, lam, subln, n_maps=1, post_scale=1.0)
        oc_lat = na_latent(pm, cache_na_k, cache_na_v, l,
                           _na_bias_tables(na_rpb[l]))

        od_ctx = fourier_mix(pm, 0, BATCH, SEQ, dft_ctx)
        od_lat = fourier_mix(pm, lat_blk0, DEC_BATCH, DEC_SEQ, dft_lat)

        x1, h2, bucket, rank, counts = out_and_route(
            x_pair, l > 0, ((oa_ctx, oa_lat), (of_ctx, of_lat), (ob_ctx, ob_lat), (oc_ctx, oc_lat),
                            (od_ctx, od_lat)),
            pg, mods[l], l, hgrn_norm_g[l], w_out_bf16, norm2_g[l], router_pieces, router_b)
        dest, *tile_plan = routing_plan(bucket, rank, counts)
        h_slots = scatter_to_slots(h2, dest, jnp.zeros((N_SLOTS, ROW_W), F32) if l == 0 else h_slots)
        y_slots = moe(h_slots, *tile_plan, l, moe_w_gate, moe_w_up, moe_w_down)
        moe_state = (dest, y_slots, x1)

        states.append(st_ctx)
    y_prompt, y_sample = final_norm(*moe_state, mods[DEPTH - 1], final_norm_g)
    return (y_prompt.reshape(BATCH, SEQ, D_MODEL), y_sample.reshape(DEC_BATCH, DEC_SEQ, D_MODEL),
            *new_kv, jnp.stack(states, axis=1))
```

```python
import functools
import math
from typing import Any, NamedTuple

import numpy as np
import jax
import jax.numpy as jnp
from jax import lax
from jax.experimental import pallas as pl
from jax.experimental.pallas import tpu as pltpu

F32 = jnp.float32
BF16 = jnp.bfloat16
HIGHEST = lax.Precision.HIGHEST

D_MODEL = 1024
BATCH = 16
SEQ = 256
DEPTH = 2
DEC_BATCH = 2
DEC_SEQ = 2048
PAST_LEN = 256
GRID_W = 64
GRID_H = DEC_SEQ // GRID_W
EPS = 1e-6
NEG_BIG = -1e30
HEADS = 4
HEAD_DIM = 64
MIX_BLK = HEADS * HEAD_DIM
A_DIM = 32
ROPE_BASE = 10000.0
B_CHUNK = 32
NA_WIN_H = 8
NA_WIN_W = 16
N_EXPERTS = 16
N_GROUPS = 4
D_EXPERT = 512
PROJ_W = 12 * MIX_BLK
TP = BATCH * SEQ
TL = DEC_BATCH * DEC_SEQ
T = TP + TL
TM = 256
N_TILES = T // TM
CTX_TILES = TP // TM
LAT_TILES_PER_SEQ = DEC_SEQ // TM
(C_AQ, C_AK, C_AV, C_BQ, C_BFF, C_BFB, C_BV, C_BG, C_CQ, C_CK, C_CV, C_DU) = range(12)
PM_BLOCKS = (C_AQ, C_AK, C_AV, C_CQ, C_CK, C_CV, C_DU)
PG_BLOCKS = (C_BQ, C_BFF, C_BFB, C_BV, C_BG)
(M_AQ, M_AK, M_AV, M_CQ, M_CK, M_CV, M_DU) = range(len(PM_BLOCKS))
(G_BQ, G_BFF, G_BFB, G_BV, G_BG) = range(len(PG_BLOCKS))
PM_W = len(PM_BLOCKS) * MIX_BLK
PG_W = len(PG_BLOCKS) * MIX_BLK
NA_SLAB_ROWS = 12
NA_SLAB = NA_SLAB_ROWS * GRID_W
LANES = 128
ROW_W = D_MODEL + LANES
EXPERT_PAIRS = ((0, 1), (0, 2), (0, 3), (1, 3), (2, 3), (2, 1))
N_BUCKETS = N_GROUPS * len(EXPERT_PAIRS)
BUCKET_ROWS = 32
TM_OUT = 512
TM_MOE = 384
MAX_TILES = -(-T // TM_MOE) + N_BUCKETS
N_SLOTS = MAX_TILES * TM_MOE
VMEM_LIMIT = 56 * 1024 * 1024


def _cparams(sem):
    return pltpu.CompilerParams(dimension_semantics=sem, vmem_limit_bytes=VMEM_LIMIT)


class Part(NamedTuple):
    kernel: Any
    in_specs: list
    args: list
    out_specs: list
    out_shapes: list
    scratch: list


def _run_parts_kernel(*refs, layout):
    n_in = sum(n for _, n, _, _ in layout)
    n_out = sum(n for _, _, n, _ in layout)
    ins, outs, scratch = refs[:n_in], refs[n_in:n_in + n_out], refs[n_in + n_out:]
    i = o = s = 0
    for kernel, k_in, k_out, k_scratch in layout:
        kernel(*ins[i:i + k_in], *outs[o:o + k_out], *scratch[s:s + k_scratch])
        i, o, s = i + k_in, o + k_out, s + k_scratch


def run_parts(parts, grid, name):
    layout = tuple((p.kernel, len(p.in_specs), len(p.out_specs), len(p.scratch)) for p in parts)
    outs = pl.pallas_call(
        functools.partial(_run_parts_kernel, layout=layout),
        grid=grid,
        in_specs=[s for p in parts for s in p.in_specs],
        out_specs=[s for p in parts for s in p.out_specs],
        out_shape=[s for p in parts for s in p.out_shapes],
        scratch_shapes=[s for p in parts for s in p.scratch],
        compiler_params=_cparams(("arbitrary", "arbitrary")),
        name=name,
    )(*[a for p in parts for a in p.args])
    result, o = [], 0
    for p in parts:
        result.append(outs[o:o + len(p.out_specs)])
        o += len(p.out_specs)
    return result


def _head_lanes(width=MIX_BLK):
    return lax.broadcasted_iota(jnp.int32, (1, width), 1)


def _lane_range(lane, lo, n):
    return (lane >= lo) & (lane < lo + n)


def _same_head_matrix():
    r = lax.broadcasted_iota(jnp.int32, (MIX_BLK, MIX_BLK), 0) // HEAD_DIM
    c = lax.broadcasted_iota(jnp.int32, (MIX_BLK, MIX_BLK), 1) // HEAD_DIM
    return r == c


def _bf16_pieces(x, n):
    pieces = []
    for _ in range(n):
        piece = x.astype(BF16)
        pieces.append(piece)
        x = x - piece.astype(F32)
    return pieces


def _select_sum_left(onehot_bf16, x):
    return sum(jnp.dot(onehot_bf16, piece, preferred_element_type=F32) for piece in _bf16_pieces(x, 3))


def _select_sum_right(x, onehot_bf16):
    return sum(jnp.dot(piece, onehot_bf16, preferred_element_type=F32) for piece in _bf16_pieces(x, 3))


def _head_mean_square(o):
    ones = jnp.where(_same_head_matrix(), 1.0, 0.0).astype(BF16)
    return _select_sum_right(o * o, ones) * (1.0 / HEAD_DIM)


def _mod_row(i):
    return jnp.where(i < CTX_TILES, 0, 1 + (i - CTX_TILES) // LAT_TILES_PER_SEQ)


def _mod_kernel(c_ref, w_ref, b_ref, o_ref):
    w = w_ref[0]
    for r in range(3):
        c = c_ref[r]
        s = c * jax.nn.sigmoid(c)
        o_ref[0, r:r + 1, :] = jnp.sum(s * w, axis=0, keepdims=True) + b_ref[0]


def modulation(c_rows, ada_w, ada_b):
    nt = 768
    n_out = 6 * D_MODEL
    return pl.pallas_call(
        _mod_kernel,
        grid=(DEPTH, n_out // nt),
        in_specs=[
            pl.BlockSpec((3, D_MODEL, 1), lambda l, j: (0, 0, 0)),
            pl.BlockSpec((1, D_MODEL, nt), lambda l, j: (l, 0, j)),
            pl.BlockSpec((1, 1, nt), lambda l, j: (l, 0, j)),
        ],
        out_specs=pl.BlockSpec((1, 3, nt), lambda l, j: (l, 0, j)),
        out_shape=jax.ShapeDtypeStruct((DEPTH, 3, n_out), F32),
        compiler_params=_cparams(("arbitrary", "arbitrary")),
        name="modulation",
    )(c_rows[:, :, None], ada_w, ada_b[:, None, :])


def _is_ctx_tile():
    return pl.program_id(0) < CTX_TILES


def _ctx_tile(i, *_):
    return (jnp.minimum(i, CTX_TILES - 1), 0)


def _lat_tile(i, *_):
    return (jnp.maximum(i - CTX_TILES, 0), 0)


def _proj_kernel(xc_ref, xl_ref, mod_ref, g_ref, w_ref, cos_ref, sa_ref, sb_ref, pm_ref, pg_ref, *cache_refs):
    x = jnp.where(_is_ctx_tile(), xc_ref[...], xl_ref[...])
    _proj_body(x, mod_ref, g_ref, w_ref, cos_ref, sa_ref, sb_ref, pm_ref, pg_ref, cache_refs)


def _proj_after_moe_kernel(dest_ref, y_ref, x1_ref, modp_ref, mod_ref, g_ref, w_ref, cos_ref, sa_ref, sb_ref,
                           *rest):
    x2_ref, pm_ref, pg_ref = rest[4:7]
    cache_refs, (buf_ref, sem) = rest[7:11], rest[11:]
    x2 = _moe_residual(dest_ref, y_ref, buf_ref, sem, x1_ref, modp_ref)
    x2_ref[...] = x2
    _proj_body(x2, mod_ref, g_ref, w_ref, cos_ref, sa_ref, sb_ref, pm_ref, pg_ref, cache_refs)


def _proj_body(x, mod_ref, g_ref, w_ref, cos_ref, sa_ref, sb_ref, pm_ref, pg_ref, cache_refs):
    ms = jnp.mean(x * x, axis=-1, keepdims=True)
    mod = mod_ref[0]
    h = x * lax.rsqrt(ms + EPS) * g_ref[...] * (1.0 + mod[1:2]) + mod[0:1]
    p = jnp.dot(h.astype(BF16), w_ref[0], preferred_element_type=F32)
    t = p[:, :2 * MIX_BLK]
    pm_ref[:, :2 * MIX_BLK] = (t * cos_ref[...] + pltpu.roll(t, 1, 1) * sa_ref[...]
                               + pltpu.roll(t, 2 * MIX_BLK - 1, 1) * sb_ref[...]).astype(BF16)
    pm_ref[:, 2 * MIX_BLK:] = p[:, 2 * MIX_BLK:PM_W].astype(BF16)
    pg_ref[...] = p[:, PM_W:]

    @pl.when(_is_ctx_tile())
    def _():
        for ref, col in zip(cache_refs, (M_AK, M_AV, M_CK, M_CV)):
            for hd in range(HEADS):
                lo = col * MIX_BLK + hd * HEAD_DIM
                ref[0, 0, hd] = p[:, lo:lo + HEAD_DIM]
            if ref.shape[1] > 1:
                ref[0, 1:] = jnp.zeros((ref.shape[1] - 1,) + tuple(ref.shape[2:]), F32)


_CACHE_SHAPE = jax.ShapeDtypeStruct((BATCH, DEPTH, HEADS, SEQ, HEAD_DIM), F32)


def _cache_spec(layer):
    n_layers = DEPTH if layer == 0 else 1
    return pl.BlockSpec((1, n_layers, HEADS, SEQ, HEAD_DIM),
                        lambda i, *_: (jnp.minimum(i, CTX_TILES - 1), layer, 0, 0, 0))


def _rope_tables():
    nf = A_DIM // 4
    freqs = ROPE_BASE ** (-np.arange(nf, dtype=np.float64) / nf)
    pos = np.arange(DEC_SEQ)
    row = (pos // GRID_W).astype(np.float64)
    col = (pos % GRID_W).astype(np.float64)
    ang = np.concatenate([row[:, None] * freqs, col[:, None] * freqs], axis=-1)
    cos = np.repeat(np.cos(ang), 2, axis=-1)
    sin = np.repeat(np.sin(ang), 2, axis=-1)
    odd = (np.arange(A_DIM) % 2 == 1)[None, :]
    sin_from_left = np.where(odd, sin, 0.0)
    sin_from_right = np.where(odd, 0.0, -sin)
    reps = 2 * MIX_BLK // A_DIM
    ident = (np.ones((TM, 2 * MIX_BLK)), np.zeros((TM, 2 * MIX_BLK)))
    return tuple(jnp.asarray(np.concatenate([np.tile(t, (1, reps)), tail], axis=0), F32)
                 for t, tail in ((cos, ident[0]), (sin_from_left, ident[1]), (sin_from_right, ident[1])))


def _rope_block(i):
    return (jnp.where(i < CTX_TILES, LAT_TILES_PER_SEQ, (i - CTX_TILES) % LAT_TILES_PER_SEQ), 0)


def projection_after_moe(dest, y_slots, x1, mod_prev, mod, layer, norm_g, w_in_bf16, rope, caches):
    rope_spec = pl.BlockSpec((TM, 2 * MIX_BLK), lambda i, d: _rope_block(i))
    mod_spec = pl.BlockSpec((1, 6, D_MODEL), lambda i, d: (_mod_row(i), 0, 0))
    n_in = 10
    return pl.pallas_call(
        _proj_after_moe_kernel,
        grid_spec=pltpu.PrefetchScalarGridSpec(
            num_scalar_prefetch=1,
            grid=(N_TILES,),
            in_specs=[pl.BlockSpec(memory_space=pl.ANY),
                      pl.BlockSpec((TM, D_MODEL), lambda i, d: (i, 0)),
                      mod_spec, mod_spec,
                      pl.BlockSpec((1, D_MODEL), lambda i, d: (0, 0)),
                      pl.BlockSpec((1, D_MODEL, PROJ_W), lambda i, d: (layer, 0, 0)),
                      rope_spec, rope_spec, rope_spec] + [pl.BlockSpec(memory_space=pl.ANY)] * 4,
            out_specs=[pl.BlockSpec((TM, D_MODEL), lambda i, d: (i, 0)),
                       pl.BlockSpec((TM, PM_W), lambda i, d: (i, 0)),
                       pl.BlockSpec((TM, PG_W), lambda i, d: (i, 0))] + [_cache_spec(layer)] * 4,
            scratch_shapes=_GATHER_SCRATCH,
        ),
        out_shape=[jax.ShapeDtypeStruct((T, D_MODEL), F32), jax.ShapeDtypeStruct((T, PM_W), BF16),
                   jax.ShapeDtypeStruct((T, PG_W), F32)] + [_CACHE_SHAPE] * 4,
        input_output_aliases={n_in + n: 3 + n for n in range(4)},
        compiler_params=_cparams(("arbitrary",)),
        name="projection_after_moe",
    )(dest, y_slots, x1, mod_prev, mod, norm_g[None, :], w_in_bf16, *rope, *caches)


def projection(x_ctx, x_lat, mod, norm_g, w_in_bf16, rope):
    layer = 0
    rope_spec = pl.BlockSpec((TM, 2 * MIX_BLK), _rope_block)
    return pl.pallas_call(
        _proj_kernel,
        grid=(N_TILES,),
        in_specs=[
            pl.BlockSpec((TM, D_MODEL), _ctx_tile),
            pl.BlockSpec((TM, D_MODEL), _lat_tile),
            pl.BlockSpec((1, 6, D_MODEL), lambda i: (_mod_row(i), 0, 0)),
            pl.BlockSpec((1, D_MODEL), lambda i: (0, 0)),
            pl.BlockSpec((1, D_MODEL, PROJ_W), lambda i: (layer, 0, 0)),
            rope_spec, rope_spec, rope_spec,
        ],
        out_specs=[pl.BlockSpec((TM, PM_W), lambda i: (i, 0)), pl.BlockSpec((TM, PG_W), lambda i: (i, 0))]
        + [_cache_spec(layer)] * 4,
        out_shape=[jax.ShapeDtypeStruct((T, PM_W), BF16), jax.ShapeDtypeStruct((T, PG_W), F32)]
        + [_CACHE_SHAPE] * 4,
        compiler_params=_cparams(("arbitrary",)),
        name="projection",
    )(x_ctx, x_lat, mod, norm_g[None, :], w_in_bf16, *rope)


LOG2_E = 1.4426950408889634


def _exp2_rows(s):
    e = jnp.exp2(s - jnp.max(s, axis=-1, keepdims=True))
    return e, 1.0 / jnp.sum(e, axis=-1, keepdims=True)


def _attn_kernel(lam_ref, q_ref, k_ref, v_ref, *rest, n_maps, post_scale, with_cache):
    if with_cache:
        kc_ref, vc_ref, g_ref, o_ref, kt_ref, vb_ref = rest
    else:
        g_ref, o_ref, kt_ref, vb_ref = rest

    @pl.when(pl.program_id(1) == 0)
    def _():
        k = k_ref[...].astype(F32)
        v = v_ref[...]
        if with_cache:
            k = jnp.concatenate([_cache_heads_on_lanes(kc_ref), k], axis=0)
            v = jnp.concatenate([_cache_heads_on_lanes(vc_ref).astype(BF16), v], axis=0)
        kt_ref[...] = k.T.astype(BF16)
        vb_ref[...] = v

    lane = _head_lanes()
    map_dim = HEAD_DIM // n_maps
    q = q_ref[...].astype(F32) * (map_dim ** -0.5 * LOG2_E)
    kt = kt_ref[...]
    vb = vb_ref[...]
    o = jnp.zeros(q.shape, F32)
    for h in range(HEADS):
        parts = []
        for j in range(n_maps):
            qm = jnp.where(_lane_range(lane, h * HEAD_DIM + j * map_dim, map_dim), q, 0.0)
            parts.append(_exp2_rows(jnp.dot(qm.astype(BF16), kt, preferred_element_type=F32)))
        w = parts[0][0] * parts[0][1]
        if n_maps == 2:
            w = w - parts[1][0] * (lam_ref[0] * parts[1][1])
        oh = jnp.dot(w.astype(BF16), vb, preferred_element_type=F32)
        o = jnp.where(_lane_range(lane, h * HEAD_DIM, HEAD_DIM), oh, o)
    if n_maps == 2:
        o = o * lax.rsqrt(_head_mean_square(o) + EPS) * g_ref[...] * post_scale
    o_ref[...] = o


def _cache_block_spec(layer):
    return pl.BlockSpec((1, 1, HEADS, PAST_LEN, HEAD_DIM), lambda b, i: (b, layer, 0, 0, 0))


def _cache_heads_on_lanes(ref):
    return jnp.concatenate([ref[0, 0, h] for h in range(HEADS)], axis=1)


def attention(p, row_blk0, cols, n_seq, seq_len, lam, norm_g, *, n_maps, post_scale, cache=None, layer=0):
    nb = seq_len // TM
    kv_len = seq_len + (PAST_LEN if cache is not None else 0)
    kern = functools.partial(_attn_kernel, n_maps=n_maps, post_scale=post_scale, with_cache=cache is not None)
    kv_spec = lambda col: pl.BlockSpec((seq_len, MIX_BLK), lambda b, i: (row_blk0 + b, col))
    cache_specs = [_cache_block_spec(layer)] * 2 if cache is not None else []
    return Part(
        kernel=kern,
        in_specs=[
            pl.BlockSpec(memory_space=pltpu.SMEM),
            pl.BlockSpec((TM, MIX_BLK), lambda b, i: ((row_blk0 + b) * nb + i, cols[0])),
            kv_spec(cols[1]), kv_spec(cols[2]), *cache_specs,
            pl.BlockSpec((1, MIX_BLK), lambda b, i: (0, 0)),
        ],
        out_specs=[pl.BlockSpec((TM, MIX_BLK), lambda b, i: (b * nb + i, 0))],
        out_shapes=[jax.ShapeDtypeStruct((n_seq * seq_len, MIX_BLK), F32)],
        scratch=[pltpu.VMEM((MIX_BLK, kv_len), BF16), pltpu.VMEM((kv_len, MIX_BLK), BF16)],
        args=[lam, p, p, p, *(cache or ()), norm_g])


def _na_slab_start(i):
    return jnp.clip(i - 1, 0, GRID_H // 4 - NA_SLAB_ROWS // 4)


def _na_kernel(q_ref, k_ref, v_ref, kc_ref, vc_ref, bias_ref, o_ref):
    i = pl.program_id(1)
    start = pl.multiple_of(_na_slab_start(i) * TM, TM)
    ks_t = k_ref[pl.ds(start, NA_SLAB), :].astype(F32).T.astype(BF16)
    vs = v_ref[pl.ds(start, NA_SLAB), :]
    kc_t = _cache_heads_on_lanes(kc_ref).T.astype(BF16)
    vc = _cache_heads_on_lanes(vc_ref).astype(BF16)
    q = q_ref[...].astype(F32) * (HEAD_DIM ** -0.5)
    lane = _head_lanes()
    o = jnp.zeros(q.shape, F32)
    for h in range(HEADS):
        in_head = _lane_range(lane, h * HEAD_DIM, HEAD_DIM)
        qm = jnp.where(in_head, q, 0.0).astype(BF16)
        s_loc = jnp.dot(qm, ks_t, preferred_element_type=F32) + bias_ref[0, h]
        s_ctx = jnp.dot(qm, kc_t, preferred_element_type=F32)
        m = jnp.maximum(jnp.max(s_loc, axis=-1, keepdims=True), jnp.max(s_ctx, axis=-1, keepdims=True))
        e_loc = jnp.exp(s_loc - m)
        e_ctx = jnp.exp(s_ctx - m)
        den = jnp.sum(e_loc, axis=-1, keepdims=True) + jnp.sum(e_ctx, axis=-1, keepdims=True)
        oh = (jnp.dot(e_loc.astype(BF16), vs, preferred_element_type=F32)
              + jnp.dot(e_ctx.astype(BF16), vc, preferred_element_type=F32)) / den
        o = jnp.where(in_head, oh, o)
    o_ref[...] = o


def _na_bias_tables(rpb):
    n_dr, n_dc = 2 * NA_WIN_H - 1, 2 * NA_WIN_W - 1
    cq = np.arange(GRID_W)[:, None]
    ck = np.arange(GRID_W)[None, :]
    wc0 = np.clip(cq - NA_WIN_W // 2, 0, GRID_W - NA_WIN_W)
    col_ok = (ck >= wc0) & (ck < wc0 + NA_WIN_W)
    col_pick = np.clip(ck - cq + NA_WIN_W - 1, 0, n_dc - 1)[..., None] == np.arange(n_dc)
    by_col = jnp.einsum("hab,qcb->haqc", rpb.astype(F32), jnp.asarray(col_pick, F32), precision=HIGHEST)
    margin = 4
    by_col = jnp.pad(by_col.transpose(0, 2, 1, 3), ((0, 0), (0, 0), (margin, margin), (0, 0)))
    by_col = by_col.reshape(HEADS, GRID_W, (n_dr + 2 * margin) * GRID_W)
    pieces, row_ok = [], []
    for tile in (0, 1, GRID_H // 4 - 1):
        slab0 = int(np.clip(tile - 1, 0, GRID_H // 4 - NA_SLAB_ROWS // 4)) * 4
        rq = tile * 4 + np.arange(4)
        rk = (slab0 + np.arange(NA_SLAB) // GRID_W)[None, :]
        wr0 = np.clip(rq - NA_WIN_H // 2, 0, GRID_H - NA_WIN_H)[:, None]
        row_ok.append((rk >= wr0) & (rk < wr0 + NA_WIN_H))
        for r in rq:
            first = slab0 - int(r) + NA_WIN_H - 1 + margin
            assert 0 <= first and first + NA_SLAB_ROWS <= n_dr + 2 * margin
            pieces.append(by_col[:, :, first * GRID_W:first * GRID_W + NA_SLAB])
    table = jnp.stack(pieces).reshape(3, 4, HEADS, GRID_W, NA_SLAB).transpose(0, 2, 1, 3, 4)
    valid = np.stack(row_ok)[:, None, :, None, :] & np.tile(col_ok, (1, NA_SLAB_ROWS))[None, None, None]
    table = jnp.where(jnp.asarray(valid), table, NEG_BIG)
    return table.reshape(3, HEADS, TM, NA_SLAB)


def na_latent(p, kc, vc, layer, bias):
    n_t = LAT_TILES_PER_SEQ
    seq_blk0 = TP // DEC_SEQ

    def bias_idx(b, i):
        return (jnp.minimum(i, 1) + i // (n_t - 1), 0, 0, 0)

    return Part(
        kernel=_na_kernel,
        in_specs=[
            pl.BlockSpec((TM, MIX_BLK), lambda b, i: (CTX_TILES + b * n_t + i, M_CQ)),
            pl.BlockSpec((DEC_SEQ, MIX_BLK), lambda b, i: (seq_blk0 + b, M_CK)),
            pl.BlockSpec((DEC_SEQ, MIX_BLK), lambda b, i: (seq_blk0 + b, M_CV)),
            _cache_block_spec(layer), _cache_block_spec(layer),
            pl.BlockSpec((1, HEADS, TM, NA_SLAB), bias_idx),
        ],
        out_specs=[pl.BlockSpec((TM, MIX_BLK), lambda b, i: (b * n_t + i, 0))],
        out_shapes=[jax.ShapeDtypeStruct((TL, MIX_BLK), F32)],
        scratch=[],
        args=[p, p, p, kc, vc, bias])


MAX_EXPONENT = 80.0


def _hgrn_direction(q_ref, f_ref, v_ref, lb, st_ref, o_ref, reverse):
    n_ch = TM // B_CHUNK
    r_idx = lax.broadcasted_iota(jnp.int32, (TM, TM), 0)
    c_idx = lax.broadcasted_iota(jnp.int32, (TM, TM), 1)
    tri = (c_idx >= r_idx) if reverse else (c_idx <= r_idx)
    zq = q_ref[...]
    q = zq * jax.nn.sigmoid(zq)
    z = f_ref[...]
    logf = jnp.log(lb + (1.0 - lb) * jax.nn.sigmoid(z))
    kk = (1.0 - lb) * jax.nn.sigmoid(-z)
    b = _select_sum_left(jnp.where(tri, 1.0, 0.0).astype(BF16), logf)
    b3 = b.reshape(n_ch, B_CHUNK, MIX_BLK)
    mid = B_CHUNK // 2 if reverse else B_CHUNK // 2 - 1
    q_in = (q.reshape(b3.shape) * jnp.exp(b3 - b3[:, mid:mid + 1, :])).reshape(TM, MIX_BLK)
    q_dec = (q * jnp.exp(b)).astype(BF16)
    b_t = b.T
    kk_t = kk.T
    far = 0 if reverse else TM - 1
    b_far = b_t[:, far:far + 1]
    k_dec_t = (kk_t * jnp.exp(b_far - b_t)).astype(BF16)
    vb = v_ref[...].astype(BF16)
    st = st_ref[...]
    o_state = jnp.dot(q_dec, st.astype(BF16), preferred_element_type=F32)
    kv = jnp.dot(k_dec_t, vb, preferred_element_type=F32)
    st_ref[...] = st * jnp.exp(b_far) + jnp.where(_same_head_matrix(), kv, 0.0)
    lane = _head_lanes()
    token = lax.broadcasted_iota(jnp.int32, (1, TM), 1)
    local = lax.broadcasted_iota(jnp.int32, (HEADS * B_CHUNK, 1), 0) % B_CHUNK
    heads = [_lane_range(lane, h * HEAD_DIM, HEAD_DIM) for h in range(HEADS)]
    for c in range(n_ch):
        rows = slice(c * B_CHUNK, (c + 1) * B_CHUNK)
        ref = b_t[:, c * B_CHUNK + mid:c * B_CHUNK + mid + 1]
        k_c_t = (kk_t * jnp.exp(jnp.minimum(ref - b_t, MAX_EXPONENT))).astype(BF16)
        q_c = q_in[rows, :]
        lhs = jnp.concatenate([jnp.where(in_head, q_c, 0.0) for in_head in heads], axis=0)
        a = jnp.dot(lhs.astype(BF16), k_c_t, preferred_element_type=F32)
        t_abs = c * B_CHUNK + local
        a = jnp.where((token >= t_abs) if reverse else (token <= t_abs), a, 0.0)
        res = jnp.dot(a.astype(BF16), vb, preferred_element_type=F32)
        o_c = o_state[rows, :]
        for h, in_head in enumerate(heads):
            o_c = o_c + jnp.where(in_head, res[h * B_CHUNK:(h + 1) * B_CHUNK, :], 0.0)
        o_ref[rows, :] = o_c


def _hgrn_kernel(qf_ref, ff_ref, vf_ref, qb_ref, fb_ref, vb_ref, lb_ref, s0_ref,
                 of_ref, ob_ref, s_ref, stf_ref, stb_ref, *, has_s0):
    j = pl.program_id(1)

    @pl.when(j == 0)
    def _():
        if has_s0:
            stf_ref[...] = s0_ref[0, 0]
            stb_ref[...] = s0_ref[0, 1]
        else:
            stf_ref[...] = jnp.zeros((MIX_BLK, MIX_BLK), F32)
            stb_ref[...] = jnp.zeros((MIX_BLK, MIX_BLK), F32)

    lb = lb_ref[...]
    _hgrn_direction(qf_ref, ff_ref, vf_ref, lb[0:1], stf_ref, of_ref, False)
    _hgrn_direction(qb_ref, fb_ref, vb_ref, lb[1:2], stb_ref, ob_ref, True)

    @pl.when(j == pl.num_programs(1) - 1)
    def _():
        for d, st_ref in enumerate((stf_ref, stb_ref)):
            s = st_ref[...]
            for hd in range(HEADS):
                lo = hd * HEAD_DIM
                s_ref[0, d, hd] = s[lo:lo + HEAD_DIM, lo:lo + HEAD_DIM]


def hgrn(p, row_tile0, n_seq, seq_len, lb, s0):
    nb = seq_len // TM
    has_s0 = s0 is not None
    if s0 is None:
        s0 = jnp.zeros((1, 2, MIX_BLK, MIX_BLK), F32)

    def fwd(col):
        return pl.BlockSpec((TM, MIX_BLK), lambda s, j: (row_tile0 + s * nb + j, col))

    def bwd(col):
        return pl.BlockSpec((TM, MIX_BLK), lambda s, j: (row_tile0 + s * nb + nb - 1 - j, col))

    state_spec = pl.BlockSpec((1, 2, MIX_BLK, MIX_BLK), lambda s, j: (s if has_s0 else 0, 0, 0, 0))
    out_rows = n_seq * seq_len
    return Part(
        kernel=functools.partial(_hgrn_kernel, has_s0=has_s0),
        in_specs=[fwd(G_BQ), fwd(G_BFF), fwd(G_BV), bwd(G_BQ), bwd(G_BFB), bwd(G_BV),
                  pl.BlockSpec((2, MIX_BLK), lambda s, j: (0, 0)), state_spec],
        out_specs=[
            pl.BlockSpec((TM, MIX_BLK), lambda s, j: (s * nb + j, 0)),
            pl.BlockSpec((TM, MIX_BLK), lambda s, j: (s * nb + nb - 1 - j, 0)),
            pl.BlockSpec((1, 2, HEADS, HEAD_DIM, HEAD_DIM), lambda s, j: (s, 0, 0, 0, 0)),
        ],
        out_shapes=[jax.ShapeDtypeStruct((out_rows, MIX_BLK), F32),
                    jax.ShapeDtypeStruct((out_rows, MIX_BLK), F32),
                    jax.ShapeDtypeStruct((n_seq, 2, HEADS, HEAD_DIM, HEAD_DIM), F32)],
        scratch=[pltpu.VMEM((MIX_BLK, MIX_BLK), F32), pltpu.VMEM((MIX_BLK, MIX_BLK), F32)],
        args=[p, p, p, p, p, p, lb, s0])


def _state_to_blockdiag(s):
    eye = jnp.eye(HEADS, dtype=F32)
    full = s.astype(F32)[:, :, :, :, None, :] * eye[None, None, :, None, :, None]
    return full.reshape(s.shape[0], 2, MIX_BLK, MIX_BLK)


def _fft_kernel(u_ref, c64_ref, s64_ref, cl_ref, sl_ref, o_ref, a_ref, b_ref, *, norm):
    @pl.when(pl.program_id(1) == 0)
    def _():
        u = u_ref[...]
        a_ref[...] = jnp.dot(u, c64_ref[...], preferred_element_type=F32).astype(BF16)
        b_ref[...] = jnp.dot(u, s64_ref[...], preferred_element_type=F32).astype(BF16)

    o_ref[...] = (jnp.dot(cl_ref[...], a_ref[...], preferred_element_type=F32)
                  - jnp.dot(sl_ref[...], b_ref[...], preferred_element_type=F32)) * norm


def _dft_tables(n):
    k = np.arange(n)
    ang = 2.0 * np.pi * ((k[:, None] * k[None, :]) % n) / n
    return np.cos(ang), np.sin(ang)


def _dft_constants(seq_len):
    c64, s64 = _dft_tables(HEAD_DIM)
    eye = np.eye(HEADS)
    cl, sl = _dft_tables(seq_len)
    as_bf16 = lambda a: jnp.asarray(a, F32).astype(BF16)
    return as_bf16(np.kron(eye, c64)), as_bf16(np.kron(eye, s64)), as_bf16(cl), as_bf16(sl)


def fourier_mix(p, row_blk0, n_seq, seq_len, consts):
    c64, s64, cl, sl = consts
    nb = seq_len // TM
    norm = 1.0 / math.sqrt(seq_len * HEAD_DIM)
    return Part(
        kernel=functools.partial(_fft_kernel, norm=norm),
        in_specs=[
            pl.BlockSpec((seq_len, MIX_BLK), lambda s, i: (row_blk0 + s, M_DU)),
            pl.BlockSpec((MIX_BLK, MIX_BLK), lambda s, i: (0, 0)),
            pl.BlockSpec((MIX_BLK, MIX_BLK), lambda s, i: (0, 0)),
            pl.BlockSpec((TM, seq_len), lambda s, i: (i, 0)),
            pl.BlockSpec((TM, seq_len), lambda s, i: (i, 0)),
        ],
        out_specs=[pl.BlockSpec((TM, MIX_BLK), lambda s, i: (s * nb + i, 0))],
        out_shapes=[jax.ShapeDtypeStruct((n_seq * seq_len, MIX_BLK), F32)],
        scratch=[pltpu.VMEM((seq_len, MIX_BLK), BF16), pltpu.VMEM((seq_len, MIX_BLK), BF16)],
        args=[p, c64, s64, cl, sl])


def _route(logits_t, rb):
    per = N_EXPERTS // N_GROUPS
    score = [jax.nn.sigmoid(logits_t[e:e + 1, :]) for e in range(N_EXPERTS)]
    sel = [score[e] + rb[e:e + 1, :] for e in range(N_EXPERTS)]
    gscore = []
    for g in range(N_GROUPS):
        vals = sel[g * per:(g + 1) * per]
        best = None
        for a in range(per):
            for b in range(a + 1, per):
                pair = vals[a] + vals[b]
                best = pair if best is None else jnp.maximum(best, pair)
        gscore.append(best)
    chosen = []
    for g in range(N_GROUPS):
        ok = None
        for j in range(N_GROUPS):
            if j == g:
                continue
            cond = gscore[g] > gscore[j] if j < g else gscore[g] >= gscore[j]
            ok = cond if ok is None else ok & cond
        chosen.append(ok)
    picked = []
    for e in range(N_EXPERTS):
        g = e // per
        rank = jnp.zeros_like(sel[e])
        for j in range(g * per, (g + 1) * per):
            if j == e:
                continue
            ahead = sel[j] >= sel[e] if j < e else sel[j] > sel[e]
            rank = rank + jnp.where(ahead, 1.0, 0.0)
        picked.append(chosen[g] & (rank < 2.0))
    wsum = jnp.zeros_like(score[0])
    for e in range(N_EXPERTS):
        wsum = wsum + jnp.where(picked[e], score[e], 0.0)
    bucket = jnp.zeros_like(wsum)
    w_a = jnp.zeros_like(wsum)
    w_b = jnp.zeros_like(wsum)
    for g in range(N_GROUPS):
        for n, (a, b) in enumerate(EXPERT_PAIRS):
            hit = picked[g * per + a] & picked[g * per + b]
            bucket = jnp.where(hit, float(g * len(EXPERT_PAIRS) + n), bucket)
            w_a = jnp.where(hit, score[g * per + a] / wsum, w_a)
            w_b = jnp.where(hit, score[g * per + b] / wsum, w_b)
    return bucket, w_a, w_b


def _out_kernel(*refs):
    streams, rest = refs[:12], refs[12:]
    (bg_ref, mod_ref, hg_ref, w_ref, g2_ref, rw_ref, rb_ref,
     x1_ref, h2_ref, bucket_ref, rank_ref, counts_ref, run_ref) = rest
    is_ctx = pl.program_id(0) < TP // TM_OUT
    x, o_a, o_f, o_b, o_c, o_d = (jnp.where(is_ctx, streams[2 * n][...], streams[2 * n + 1][...])
                                  for n in range(6))

    @pl.when(pl.program_id(0) == 0)
    def _():
        run_ref[...] = jnp.zeros(run_ref.shape, F32)

    mod = mod_ref[0]
    hb = o_f + o_b
    zg = bg_ref[...]
    hb = hb * lax.rsqrt(_head_mean_square(hb) + EPS) * hg_ref[...] * (zg * jax.nn.sigmoid(zg))
    parts = (o_a, hb, o_c, o_d)
    mixed = jnp.zeros((TM_OUT, D_MODEL), F32)
    for n, part in enumerate(parts):
        mixed = mixed + jnp.dot(part.astype(BF16), w_ref[0, n * MIX_BLK:(n + 1) * MIX_BLK, :],
                                preferred_element_type=F32)
    x1 = x + mod[2:3] * mixed
    x1_ref[...] = x1
    ms = jnp.mean(x1 * x1, axis=-1, keepdims=True)
    h2 = x1 * lax.rsqrt(ms + EPS) * g2_ref[...] * (1.0 + mod[4:5]) + mod[3:4]
    rw = rw_ref[...]
    r = sum(jnp.dot(piece, rw, preferred_element_type=F32) for piece in _bf16_pieces(h2, 2))
    bucket, w_a, w_b = _route((r[:, :LANES] + r[:, LANES:]).T, rb_ref[...])
    h2_ref[:, :D_MODEL] = h2
    h2_ref[:, D_MODEL:] = jnp.concatenate([w_a, w_b, jnp.zeros((LANES - 2, TM_OUT), F32)], axis=0).T
    onehot = jnp.where(lax.broadcasted_iota(jnp.int32, (BUCKET_ROWS, 1), 0).astype(F32) == bucket, 1.0, 0.0)
    s_idx = lax.broadcasted_iota(jnp.int32, (TM_OUT, TM_OUT), 0)
    t_idx = lax.broadcasted_iota(jnp.int32, (TM_OUT, TM_OUT), 1)
    prefix = jnp.dot(onehot.astype(BF16), jnp.where(s_idx <= t_idx, 1.0, 0.0).astype(BF16),
                     preferred_element_type=F32)
    run = run_ref[...]
    rank = jnp.sum(onehot * (prefix - 1.0 + run[:, 0:1]), axis=0, keepdims=True)
    run = run + jnp.sum(onehot, axis=1, keepdims=True)
    run_ref[...] = run
    bucket_ref[...] = bucket.astype(jnp.int32)
    rank_ref[...] = rank.astype(jnp.int32)
    counts_ref[...] = run


def out_and_route(x_pair, x_is_combined, mixer_pairs, p, mod, layer, hgrn_g, w_out_bf16, norm2_g, router_pieces,
                  router_b):
    n_ctx = TP // TM_OUT
    ctx_tile = lambda i: (jnp.minimum(i, n_ctx - 1), 0)
    lat_tile = lambda i: (jnp.maximum(i - n_ctx, 0), 0)
    mod_row = lambda i: jnp.where(i < n_ctx, 0, 1 + (i - n_ctx) // (DEC_SEQ // TM_OUT))
    tile = lambda w: pl.BlockSpec((TM_OUT, w), lambda i: (i, 0))
    full = lambda r, c: pl.BlockSpec((r, c), lambda i: (0, 0))
    stream_specs = [pl.BlockSpec((TM_OUT, D_MODEL), ctx_tile),
                    pl.BlockSpec((TM_OUT, D_MODEL),
                                 (lambda i: (jnp.maximum(i, n_ctx), 0)) if x_is_combined else lat_tile)]
    stream_args = list(x_pair)
    for o_ctx, o_lat in mixer_pairs:
        stream_specs += [pl.BlockSpec((TM_OUT, MIX_BLK), ctx_tile), pl.BlockSpec((TM_OUT, MIX_BLK), lat_tile)]
        stream_args += [o_ctx, o_lat]
    return pl.pallas_call(
        _out_kernel,
        grid=(T // TM_OUT,),
        in_specs=stream_specs + [
            pl.BlockSpec((TM_OUT, MIX_BLK), lambda i: (i, G_BG)),
            pl.BlockSpec((1, 6, D_MODEL), lambda i: (mod_row(i), 0, 0)),
            full(1, MIX_BLK), pl.BlockSpec((1, D_MODEL, D_MODEL), lambda i: (layer, 0, 0)), full(1, D_MODEL),
            full(D_MODEL, 2 * LANES), full(N_EXPERTS, 1),
        ],
        out_specs=[tile(D_MODEL), tile(ROW_W), pl.BlockSpec((1, TM_OUT), lambda i: (0, i)),
                   pl.BlockSpec((1, TM_OUT), lambda i: (0, i)), full(BUCKET_ROWS, LANES)],
        out_shape=[jax.ShapeDtypeStruct((T, D_MODEL), F32),
                   jax.ShapeDtypeStruct((T, ROW_W), F32),
                   jax.ShapeDtypeStruct((1, T), jnp.int32),
                   jax.ShapeDtypeStruct((1, T), jnp.int32),
                   jax.ShapeDtypeStruct((BUCKET_ROWS, LANES), F32)],
        scratch_shapes=[pltpu.VMEM((BUCKET_ROWS, LANES), F32)],
        compiler_params=_cparams(("arbitrary",)),
        name="out_and_route",
    )(*stream_args, p, mod, jnp.tile(hgrn_g, HEADS)[None, :], w_out_bf16,
      norm2_g[None, :], router_pieces, router_b[:, None])


def _router_pieces(router_w):
    hi, lo = _bf16_pieces(router_w.astype(F32), 2)
    pad = lambda a: jnp.pad(a, ((0, 0), (0, LANES - N_EXPERTS)))
    return jnp.concatenate([pad(hi), pad(lo)], axis=1)


def routing_plan(bucket, rank, counts):
    counts = counts[:N_BUCKETS, 0].astype(jnp.int32)
    n_tiles = (counts + TM_MOE - 1) // TM_MOE
    tile_end = jnp.cumsum(n_tiles)
    tile_start = tile_end - n_tiles
    buckets = jnp.arange(N_BUCKETS, dtype=jnp.int32)
    start_of_token = jnp.sum(jnp.where(bucket[0][:, None] == buckets[None, :], tile_start[None, :], 0), axis=1)
    dest = start_of_token * TM_MOE + rank[0]
    tiles = jnp.arange(MAX_TILES, dtype=jnp.int32)
    valid = tiles < tile_end[-1]
    tile_bucket = jnp.sum((jnp.minimum(tiles, tile_end[-1] - 1)[:, None] >= tile_end[None, :]).astype(jnp.int32), axis=1)
    pair_a = np.array([a for a, _ in EXPERT_PAIRS], np.int32)
    pair_b = np.array([b for _, b in EXPERT_PAIRS], np.int32)
    per = N_EXPERTS // N_GROUPS
    exp_a = jnp.asarray((np.arange(N_BUCKETS) // len(EXPERT_PAIRS)) * per + np.tile(pair_a, N_GROUPS), jnp.int32)
    exp_b = jnp.asarray((np.arange(N_BUCKETS) // len(EXPERT_PAIRS)) * per + np.tile(pair_b, N_GROUPS), jnp.int32)
    pick = tile_bucket[:, None] == buckets[None, :]
    tile_a = jnp.sum(jnp.where(pick, exp_a[None, :], 0), axis=1)
    tile_b = jnp.sum(jnp.where(pick, exp_b[None, :], 0), axis=1)
    return dest.astype(jnp.int32), tile_a, tile_b, valid.astype(jnp.int32), (tile_end[-1:] - 1).astype(jnp.int32)


def _row_copy(src, src_row, dst, dst_row, sem):
    return pltpu.make_async_copy(src.at[pl.ds(src_row, 1), :], dst.at[pl.ds(dst_row, 1), :], sem)


def _scatter_kernel(dest_ref, h_ref, init_ref, o_ref, sem):
    del init_ref
    base = pl.program_id(0) * TM

    for r in range(TM):
        _row_copy(h_ref, r, o_ref, dest_ref[base + r], sem).start()
    pltpu.make_async_copy(h_ref, o_ref.at[pl.ds(0, TM), :], sem).wait()


def scatter_to_slots(h2, dest, slots):
    return pl.pallas_call(
        _scatter_kernel,
        grid_spec=pltpu.PrefetchScalarGridSpec(
            num_scalar_prefetch=1,
            grid=(N_TILES,),
            in_specs=[pl.BlockSpec((TM, ROW_W), lambda i, d: (i, 0)),
                      pl.BlockSpec(memory_space=pl.ANY)],
            out_specs=pl.BlockSpec(memory_space=pl.ANY),
            scratch_shapes=[pltpu.SemaphoreType.DMA(())],
        ),
        out_shape=jax.ShapeDtypeStruct((N_SLOTS, ROW_W), F32),
        input_output_aliases={2: 0},
        compiler_params=_cparams(("arbitrary",)),
        name="scatter_to_slots",
    )(dest, h2, slots)


def _moe_kernel(ta_ref, tb_ref, valid_ref, last_ref, h_ref, wga_ref, wua_ref, wda_ref, wgb_ref, wub_ref, wdb_ref,
                o_ref):
    del ta_ref, tb_ref, last_ref
    i = pl.program_id(0)

    @pl.when(valid_ref[i] == 1)
    def _():
        x = h_ref[:, :D_MODEL].astype(BF16)
        gates = h_ref[:, D_MODEL:]
        y = jnp.zeros((TM_MOE, D_MODEL), F32)
        for n, (wg, wu, wd) in enumerate(((wga_ref, wua_ref, wda_ref), (wgb_ref, wub_ref, wdb_ref))):
            a = jnp.dot(x, wg[0, 0].astype(BF16), preferred_element_type=F32)
            u = jnp.dot(x, wu[0, 0].astype(BF16), preferred_element_type=F32)
            z = a * jax.nn.sigmoid(a) * u * gates[:, n:n + 1]
            y = y + jnp.dot(z.astype(BF16), wd[0, 0].astype(BF16), preferred_element_type=F32)
        o_ref[...] = y

    @pl.when(valid_ref[i] == 0)
    def _():
        o_ref[...] = jnp.zeros((TM_MOE, D_MODEL), F32)


def moe(h_slots, tile_a, tile_b, valid, last, layer, wg, wu, wd):
    up_a = pl.BlockSpec((1, 1, D_MODEL, D_EXPERT), lambda i, ta, tb, v, last: (layer, ta[i], 0, 0))
    up_b = pl.BlockSpec((1, 1, D_MODEL, D_EXPERT), lambda i, ta, tb, v, last: (layer, tb[i], 0, 0))
    down_a = pl.BlockSpec((1, 1, D_EXPERT, D_MODEL), lambda i, ta, tb, v, last: (layer, ta[i], 0, 0))
    down_b = pl.BlockSpec((1, 1, D_EXPERT, D_MODEL), lambda i, ta, tb, v, last: (layer, tb[i], 0, 0))
    return pl.pallas_call(
        _moe_kernel,
        grid_spec=pltpu.PrefetchScalarGridSpec(
            num_scalar_prefetch=4,
            grid=(MAX_TILES,),
            in_specs=[pl.BlockSpec((TM_MOE, ROW_W), lambda i, ta, tb, v, last: (jnp.minimum(i, last[0]), 0)),
                      up_a, up_a, down_a, up_b, up_b, down_b],
            out_specs=pl.BlockSpec((TM_MOE, D_MODEL), lambda i, ta, tb, v, last: (i, 0)),
        ),
        out_shape=jax.ShapeDtypeStruct((N_SLOTS, D_MODEL), F32),
        compiler_params=_cparams(("arbitrary",)),
        name="moe",
    )(tile_a, tile_b, valid, last, h_slots, wg, wu, wd, wg, wu, wd)


def _gather_tile(dest_ref, y_ref, buf_ref, sem, tile, slot):
    for r in range(TM):
        pltpu.make_async_copy(y_ref.at[pl.ds(dest_ref[tile * TM + r], 1), :],
                              buf_ref.at[slot, pl.ds(r, 1), :], sem.at[slot]).start()


def _moe_residual(dest_ref, y_ref, buf_ref, sem, x1_ref, mod_ref):
    i = pl.program_id(0)
    slot = i % 2

    @pl.when(i == 0)
    def _():
        _gather_tile(dest_ref, y_ref, buf_ref, sem, 0, 0)

    @pl.when(i + 1 < pl.num_programs(0))
    def _():
        _gather_tile(dest_ref, y_ref, buf_ref, sem, i + 1, 1 - slot)

    pltpu.make_async_copy(y_ref.at[pl.ds(0, TM), :], buf_ref.at[slot], sem.at[slot]).wait()
    return x1_ref[...] + mod_ref[0][5:6] * buf_ref[slot]


def _final_kernel(dest_ref, y_ref, x1_ref, mod_ref, g_ref, oc_ref, ol_ref, buf_ref, sem):
    x2 = _moe_residual(dest_ref, y_ref, buf_ref, sem, x1_ref, mod_ref)
    ms = jnp.mean(x2 * x2, axis=-1, keepdims=True)
    y = x2 * lax.rsqrt(ms + EPS) * g_ref[...]

    @pl.when(_is_ctx_tile())
    def _():
        oc_ref[...] = y

    @pl.when(jnp.logical_not(_is_ctx_tile()))
    def _():
        ol_ref[...] = y


_GATHER_SCRATCH = [pltpu.VMEM((2, TM, D_MODEL), F32), pltpu.SemaphoreType.DMA((2,))]


def final_norm(dest, y_slots, x1, mod, final_g):
    return pl.pallas_call(
        _final_kernel,
        grid_spec=pltpu.PrefetchScalarGridSpec(
            num_scalar_prefetch=1,
            grid=(N_TILES,),
            in_specs=[pl.BlockSpec(memory_space=pl.ANY),
                      pl.BlockSpec((TM, D_MODEL), lambda i, d: (i, 0)),
                      pl.BlockSpec((1, 6, D_MODEL), lambda i, d: (_mod_row(i), 0, 0)),
                      pl.BlockSpec((1, D_MODEL), lambda i, d: (0, 0))],
            out_specs=[pl.BlockSpec((TM, D_MODEL), _ctx_tile), pl.BlockSpec((TM, D_MODEL), _lat_tile)],
            scratch_shapes=_GATHER_SCRATCH,
        ),
        out_shape=[jax.ShapeDtypeStruct((TP, D_MODEL), F32), jax.ShapeDtypeStruct((TL, D_MODEL), F32)],
        compiler_params=_cparams(("arbitrary",)),
        name="final_norm",
    )(dest, y_slots, x1, mod, final_g[None, :])


def kernel(x_prompt, x_sample, cache_diff_k, cache_diff_v, cache_na_k, cache_na_v, state_hgrn, c, c_ctx,
           norm1_g, norm2_g, ada_w, ada_b, w_in, w_out, diff_lambda, diff_subln_g, hgrn_lb_logits,
           hgrn_norm_g, na_rpb, router_w, router_b, moe_w_gate, moe_w_up, moe_w_down, final_norm_g):
    assert SEQ == TM and PAST_LEN == TM and DEC_SEQ % TM_OUT == 0 and TP % DEC_SEQ == 0
    x_pair = (x_prompt.reshape(TP, D_MODEL), x_sample.reshape(TL, D_MODEL))
    w_in_bf16 = jnp.concatenate([w_in[:, :, c * MIX_BLK:(c + 1) * MIX_BLK] for c in PM_BLOCKS + PG_BLOCKS],
                                axis=-1).astype(BF16)
    w_out_bf16 = w_out.astype(BF16)
    router_pieces = _router_pieces(router_w)
    mods = modulation(jnp.concatenate([c_ctx[None, :], c], axis=0), ada_w, ada_b)
    mods = mods.reshape(DEPTH, 3, 6, D_MODEL)
    lb_sm = jax.nn.softmax(hgrn_lb_logits.astype(F32), axis=0)
    lb_all = jnp.cumsum(lb_sm, axis=0) - lb_sm[0:1]
    rope = _rope_tables()
    dft_ctx = _dft_constants(SEQ)
    dft_lat = _dft_constants(DEC_SEQ)
    lat_blk0 = TP // DEC_SEQ
    states = []
    moe_state = None
    for l in range(DEPTH):
        if moe_state is None:
            pm, pg, *new_kv = projection(*x_pair, mods[l], norm1_g[l], w_in_bf16, rope)
        else:
            x, pm, pg, *new_kv = projection_after_moe(*moe_state, mods[l - 1], mods[l], l, norm1_g[l], w_in_bf16, rope,
                                                 new_kv)
            x_pair = (x, x)

        lq = diff_lambda[l].astype(F32)
        lam_init = 0.8 - 0.6 * math.exp(-0.3 * l)
        lam = (jnp.exp(jnp.sum(lq[0] * lq[1])) - jnp.exp(jnp.sum(lq[2] * lq[3])) + lam_init).reshape(1)
        subln = jnp.tile(diff_subln_g[l], HEADS)[None, :]
        diff = functools.partial(attention, pm, cols=(M_AQ, M_AK, M_AV), lam=lam, norm_g=subln, n_maps=2,
                                 post_scale=1.0 - lam_init)
        (oa_ctx,), (of_ctx, ob_ctx, st_ctx), (oc_ctx,), (od_ctx,) = run_parts(
            [diff(row_blk0=0, n_seq=BATCH, seq_len=SEQ),
             hgrn(pg, 0, BATCH, SEQ, lb_all[l], None),
             attention(pm, 0, (M_CQ, M_CK, M_CV), BATCH, SEQ, lam, subln, n_maps=1, post_scale=1.0),
             fourier_mix(pm, 0, BATCH, SEQ, dft_ctx)],
            (BATCH, SEQ // TM), "context_mixers")
        (oa_lat,), (of_lat, ob_lat, _), (oc_lat,), (od_lat,) = run_parts(
            [diff(row_blk0=lat_blk0, n_seq=DEC_BATCH, seq_len=DEC_SEQ, cache=(cache_diff_k, cache_diff_v), layer=l),
             hgrn(pg, CTX_TILES, DEC_BATCH, DEC_SEQ, lb_all[l], _state_to_blockdiag(state_hgrn[:, l])),
             na_latent(pm, cache_na_k, cache_na_v, l, _na_bias_tables(na_rpb[l])),
             fourier_mix(pm, lat_blk0, DEC_BATCH, DEC_SEQ, dft_lat)],
            (DEC_BATCH, DEC_SEQ // TM), "latent_mixers")

        x1, h2, bucket, rank, counts = out_and_route(
            x_pair, l > 0, ((oa_ctx, oa_lat), (of_ctx, of_lat), (ob_ctx, ob_lat), (oc_ctx, oc_lat),
                            (od_ctx, od_lat)),
            pg, mods[l], l, hgrn_norm_g[l], w_out_bf16, norm2_g[l], router_pieces, router_b)
        dest, *tile_plan = routing_plan(bucket, rank, counts)
        h_slots = scatter_to_slots(h2, dest, jnp.zeros((N_SLOTS, ROW_W), F32) if l == 0 else h_slots)
        y_slots = moe(h_slots, *tile_plan, l, moe_w_gate, moe_w_up, moe_w_down)
        moe_state = (dest, y_slots, x1)

        states.append(st_ctx)
    y_prompt, y_sample = final_norm(*moe_state, mods[DEPTH - 1], final_norm_g)
    return (y_prompt.reshape(BATCH, SEQ, D_MODEL), y_sample.reshape(DEC_BATCH, DEC_SEQ, D_MODEL),
            *new_kv, jnp.stack(states, axis=1))
```

```python
import functools
import math
from typing import Any, NamedTuple

import numpy as np
import jax
import jax.numpy as jnp
from jax import lax
from jax.experimental import pallas as pl
from jax.experimental.pallas import tpu as pltpu

F32 = jnp.float32
BF16 = jnp.bfloat16
HIGHEST = lax.Precision.HIGHEST

D_MODEL = 1024
BATCH = 16
SEQ = 256
DEPTH = 2
DEC_BATCH = 2
DEC_SEQ = 2048
PAST_LEN = 256
GRID_W = 64
GRID_H = DEC_SEQ // GRID_W
EPS = 1e-6
NEG_BIG = -1e30
HEADS = 4
HEAD_DIM = 64
MIX_BLK = HEADS * HEAD_DIM
A_DIM = 32
ROPE_BASE = 10000.0
B_CHUNK = 32
NA_WIN_H = 8
NA_WIN_W = 16
N_EXPERTS = 16
N_GROUPS = 4
D_EXPERT = 512
PROJ_W = 12 * MIX_BLK
TP = BATCH * SEQ
TL = DEC_BATCH * DEC_SEQ
T = TP + TL
TM = 256
N_TILES = T // TM
CTX_TILES = TP // TM
LAT_TILES_PER_SEQ = DEC_SEQ // TM
(C_AQ, C_AK, C_AV, C_BQ, C_BFF, C_BFB, C_BV, C_BG, C_CQ, C_CK, C_CV, C_DU) = range(12)
PM_BLOCKS = (C_AQ, C_AK, C_AV, C_CQ, C_CK, C_CV, C_DU)
PG_BLOCKS = (C_BQ, C_BFF, C_BFB, C_BV, C_BG)
(M_AQ, M_AK, M_AV, M_CQ, M_CK, M_CV, M_DU) = range(len(PM_BLOCKS))
(G_BQ, G_BFF, G_BFB, G_BV, G_BG) = range(len(PG_BLOCKS))
PM_W = len(PM_BLOCKS) * MIX_BLK
PG_W = len(PG_BLOCKS) * MIX_BLK
NA_SLAB_ROWS = 12
NA_SLAB = NA_SLAB_ROWS * GRID_W
LANES = 128
ROW_W = D_MODEL + LANES
EXPERT_PAIRS = ((0, 1), (0, 2), (0, 3), (1, 3), (2, 3), (2, 1))
N_BUCKETS = N_GROUPS * len(EXPERT_PAIRS)
BUCKET_ROWS = 32
N_DMA_PRIORITIES = 2
TM_OUT = 512
TM_MOE = 384
MAX_TILES = -(-T // TM_MOE) + N_BUCKETS
N_SLOTS = MAX_TILES * TM_MOE
VMEM_LIMIT = 56 * 1024 * 1024


def _cparams(sem):
    return pltpu.CompilerParams(dimension_semantics=sem, vmem_limit_bytes=VMEM_LIMIT)


class Part(NamedTuple):
    kernel: Any
    in_specs: list
    args: list
    out_specs: list
    out_shapes: list
    scratch: list


def _run_parts_kernel(*refs, layout):
    n_in = sum(n for _, n, _, _ in layout)
    n_out = sum(n for _, _, n, _ in layout)
    ins, outs, scratch = refs[:n_in], refs[n_in:n_in + n_out], refs[n_in + n_out:]
    i = o = s = 0
    for kernel, k_in, k_out, k_scratch in layout:
        kernel(*ins[i:i + k_in], *outs[o:o + k_out], *scratch[s:s + k_scratch])
        i, o, s = i + k_in, o + k_out, s + k_scratch


def run_parts(parts, grid, name):
    layout = tuple((p.kernel, len(p.in_specs), len(p.out_specs), len(p.scratch)) for p in parts)
    outs = pl.pallas_call(
        functools.partial(_run_parts_kernel, layout=layout),
        grid=grid,
        in_specs=[s for p in parts for s in p.in_specs],
        out_specs=[s for p in parts for s in p.out_specs],
        out_shape=[s for p in parts for s in p.out_shapes],
        scratch_shapes=[s for p in parts for s in p.scratch],
        compiler_params=_cparams(("arbitrary", "arbitrary")),
        name=name,
    )(*[a for p in parts for a in p.args])
    result, o = [], 0
    for p in parts:
        result.append(outs[o:o + len(p.out_specs)])
        o += len(p.out_specs)
    return result


def _head_lanes(width=MIX_BLK):
    return lax.broadcasted_iota(jnp.int32, (1, width), 1)


def _lane_range(lane, lo, n):
    return (lane >= lo) & (lane < lo + n)


def _same_head_matrix():
    r = lax.broadcasted_iota(jnp.int32, (MIX_BLK, MIX_BLK), 0) // HEAD_DIM
    c = lax.broadcasted_iota(jnp.int32, (MIX_BLK, MIX_BLK), 1) // HEAD_DIM
    return r == c


def _bf16_pieces(x, n):
    pieces = []
    for _ in range(n):
        piece = x.astype(BF16)
        pieces.append(piece)
        x = x - piece.astype(F32)
    return pieces


def _select_sum_left(onehot_bf16, x):
    return sum(jnp.dot(onehot_bf16, piece, preferred_element_type=F32) for piece in _bf16_pieces(x, 3))


def _select_sum_right(x, onehot_bf16):
    return sum(jnp.dot(piece, onehot_bf16, preferred_element_type=F32) for piece in _bf16_pieces(x, 3))


def _head_mean_square(o):
    ones = jnp.where(_same_head_matrix(), 1.0, 0.0).astype(BF16)
    return _select_sum_right(o * o, ones) * (1.0 / HEAD_DIM)


def _mod_row(i):
    return jnp.where(i < CTX_TILES, 0, 1 + (i - CTX_TILES) // LAT_TILES_PER_SEQ)


def _mod_kernel(c_ref, w_ref, b_ref, o_ref):
    w = w_ref[0]
    for r in range(3):
        c = c_ref[r]
        s = c * jax.nn.sigmoid(c)
        o_ref[0, r:r + 1, :] = jnp.sum(s * w, axis=0, keepdims=True) + b_ref[0]


def modulation(c_rows, ada_w, ada_b):
    nt = 768
    n_out = 6 * D_MODEL
    return pl.pallas_call(
        _mod_kernel,
        grid=(DEPTH, n_out // nt),
        in_specs=[
            pl.BlockSpec((3, D_MODEL, 1), lambda l, j: (0, 0, 0)),
            pl.BlockSpec((1, D_MODEL, nt), lambda l, j: (l, 0, j)),
            pl.BlockSpec((1, 1, nt), lambda l, j: (l, 0, j)),
        ],
        out_specs=pl.BlockSpec((1, 3, nt), lambda l, j: (l, 0, j)),
        out_shape=jax.ShapeDtypeStruct((DEPTH, 3, n_out), F32),
        compiler_params=_cparams(("arbitrary", "arbitrary")),
        name="modulation",
    )(c_rows[:, :, None], ada_w, ada_b[:, None, :])


def _is_ctx_tile():
    return pl.program_id(0) < CTX_TILES


def _ctx_tile(i, *_):
    return (jnp.minimum(i, CTX_TILES - 1), 0)


def _lat_tile(i, *_):
    return (jnp.maximum(i - CTX_TILES, 0), 0)


def _proj_kernel(xc_ref, xl_ref, mod_ref, g_ref, w_ref, cos_ref, sa_ref, sb_ref, pm_ref, pg_ref, *cache_refs):
    x = jnp.where(_is_ctx_tile(), xc_ref[...], xl_ref[...])
    _proj_body(x, mod_ref, g_ref, w_ref, cos_ref, sa_ref, sb_ref, pm_ref, pg_ref, cache_refs)


def _proj_after_moe_kernel(dest_ref, y_ref, x1_ref, modp_ref, mod_ref, g_ref, w_ref, cos_ref, sa_ref, sb_ref,
                           *rest):
    x2_ref, pm_ref, pg_ref = rest[4:7]
    cache_refs, (buf_ref, sem) = rest[7:11], rest[11:]
    x2 = _moe_residual(dest_ref, y_ref, buf_ref, sem, x1_ref, modp_ref)
    x2_ref[...] = x2
    _proj_body(x2, mod_ref, g_ref, w_ref, cos_ref, sa_ref, sb_ref, pm_ref, pg_ref, cache_refs)


def _proj_body(x, mod_ref, g_ref, w_ref, cos_ref, sa_ref, sb_ref, pm_ref, pg_ref, cache_refs):
    ms = jnp.mean(x * x, axis=-1, keepdims=True)
    mod = mod_ref[0]
    h = x * lax.rsqrt(ms + EPS) * g_ref[...] * (1.0 + mod[1:2]) + mod[0:1]
    p = jnp.dot(h.astype(BF16), w_ref[0], preferred_element_type=F32)
    t = p[:, :2 * MIX_BLK]
    pm_ref[:, :2 * MIX_BLK] = (t * cos_ref[...] + pltpu.roll(t, 1, 1) * sa_ref[...]
                               + pltpu.roll(t, 2 * MIX_BLK - 1, 1) * sb_ref[...]).astype(BF16)
    pm_ref[:, 2 * MIX_BLK:] = p[:, 2 * MIX_BLK:PM_W].astype(BF16)
    pg_ref[...] = p[:, PM_W:]

    @pl.when(_is_ctx_tile())
    def _():
        for ref, col in zip(cache_refs, (M_AK, M_AV, M_CK, M_CV)):
            for hd in range(HEADS):
                lo = col * MIX_BLK + hd * HEAD_DIM
                ref[0, 0, hd] = p[:, lo:lo + HEAD_DIM]
            if ref.shape[1] > 1:
                ref[0, 1:] = jnp.zeros((ref.shape[1] - 1,) + tuple(ref.shape[2:]), F32)


_CACHE_SHAPE = jax.ShapeDtypeStruct((BATCH, DEPTH, HEADS, SEQ, HEAD_DIM), F32)


def _cache_spec(layer):
    n_layers = DEPTH if layer == 0 else 1
    return pl.BlockSpec((1, n_layers, HEADS, SEQ, HEAD_DIM),
                        lambda i, *_: (jnp.minimum(i, CTX_TILES - 1), layer, 0, 0, 0))


def _rope_tables():
    nf = A_DIM // 4
    freqs = ROPE_BASE ** (-np.arange(nf, dtype=np.float64) / nf)
    pos = np.arange(DEC_SEQ)
    row = (pos // GRID_W).astype(np.float64)
    col = (pos % GRID_W).astype(np.float64)
    ang = np.concatenate([row[:, None] * freqs, col[:, None] * freqs], axis=-1)
    cos = np.repeat(np.cos(ang), 2, axis=-1)
    sin = np.repeat(np.sin(ang), 2, axis=-1)
    odd = (np.arange(A_DIM) % 2 == 1)[None, :]
    sin_from_left = np.where(odd, sin, 0.0)
    sin_from_right = np.where(odd, 0.0, -sin)
    reps = 2 * MIX_BLK // A_DIM
    ident = (np.ones((TM, 2 * MIX_BLK)), np.zeros((TM, 2 * MIX_BLK)))
    return tuple(jnp.asarray(np.concatenate([np.tile(t, (1, reps)), tail], axis=0), F32)
                 for t, tail in ((cos, ident[0]), (sin_from_left, ident[1]), (sin_from_right, ident[1])))


def _rope_block(i):
    return (jnp.where(i < CTX_TILES, LAT_TILES_PER_SEQ, (i - CTX_TILES) % LAT_TILES_PER_SEQ), 0)


def projection_after_moe(dest, y_slots, x1, mod_prev, mod, layer, norm_g, w_in_bf16, rope, caches):
    rope_spec = pl.BlockSpec((TM, 2 * MIX_BLK), lambda i, d: _rope_block(i))
    mod_spec = pl.BlockSpec((1, 6, D_MODEL), lambda i, d: (_mod_row(i), 0, 0))
    n_in = 10
    return pl.pallas_call(
        _proj_after_moe_kernel,
        grid_spec=pltpu.PrefetchScalarGridSpec(
            num_scalar_prefetch=1,
            grid=(N_TILES,),
            in_specs=[pl.BlockSpec(memory_space=pl.ANY),
                      pl.BlockSpec((TM, D_MODEL), lambda i, d: (i, 0)),
                      mod_spec, mod_spec,
                      pl.BlockSpec((1, D_MODEL), lambda i, d: (0, 0)),
                      pl.BlockSpec((1, D_MODEL, PROJ_W), lambda i, d: (layer, 0, 0)),
                      rope_spec, rope_spec, rope_spec] + [pl.BlockSpec(memory_space=pl.ANY)] * 4,
            out_specs=[pl.BlockSpec((TM, D_MODEL), lambda i, d: (i, 0)),
                       pl.BlockSpec((TM, PM_W), lambda i, d: (i, 0)),
                       pl.BlockSpec((TM, PG_W), lambda i, d: (i, 0))] + [_cache_spec(layer)] * 4,
            scratch_shapes=_GATHER_SCRATCH,
        ),
        out_shape=[jax.ShapeDtypeStruct((T, D_MODEL), F32), jax.ShapeDtypeStruct((T, PM_W), BF16),
                   jax.ShapeDtypeStruct((T, PG_W), F32)] + [_CACHE_SHAPE] * 4,
        input_output_aliases={n_in + n: 3 + n for n in range(4)},
        compiler_params=_cparams(("arbitrary",)),
        name="projection_after_moe",
    )(dest, y_slots, x1, mod_prev, mod, norm_g[None, :], w_in_bf16, *rope, *caches)


def projection(x_ctx, x_lat, mod, norm_g, w_in_bf16, rope):
    layer = 0
    rope_spec = pl.BlockSpec((TM, 2 * MIX_BLK), _rope_block)
    return pl.pallas_call(
        _proj_kernel,
        grid=(N_TILES,),
        in_specs=[
            pl.BlockSpec((TM, D_MODEL), _ctx_tile),
            pl.BlockSpec((TM, D_MODEL), _lat_tile),
            pl.BlockSpec((1, 6, D_MODEL), lambda i: (_mod_row(i), 0, 0)),
            pl.BlockSpec((1, D_MODEL), lambda i: (0, 0)),
            pl.BlockSpec((1, D_MODEL, PROJ_W), lambda i: (layer, 0, 0)),
            rope_spec, rope_spec, rope_spec,
        ],
        out_specs=[pl.BlockSpec((TM, PM_W), lambda i: (i, 0)), pl.BlockSpec((TM, PG_W), lambda i: (i, 0))]
        + [_cache_spec(layer)] * 4,
        out_shape=[jax.ShapeDtypeStruct((T, PM_W), BF16), jax.ShapeDtypeStruct((T, PG_W), F32)]
        + [_CACHE_SHAPE] * 4,
        compiler_params=_cparams(("arbitrary",)),
        name="projection",
    )(x_ctx, x_lat, mod, norm_g[None, :], w_in_bf16, *rope)


LOG2_E = 1.4426950408889634


def _exp2_rows(s):
    e = jnp.exp2(s - jnp.max(s, axis=-1, keepdims=True))
    return e, 1.0 / jnp.sum(e, axis=-1, keepdims=True)


def _attn_kernel(lam_ref, q_ref, k_ref, v_ref, *rest, n_maps, post_scale, with_cache):
    if with_cache:
        kc_ref, vc_ref, g_ref, o_ref, kt_ref, vb_ref = rest
    else:
        g_ref, o_ref, kt_ref, vb_ref = rest

    @pl.when(pl.program_id(1) == 0)
    def _():
        k = k_ref[...].astype(F32)
        v = v_ref[...]
        if with_cache:
            k = jnp.concatenate([_cache_heads_on_lanes(kc_ref), k], axis=0)
            v = jnp.concatenate([_cache_heads_on_lanes(vc_ref).astype(BF16), v], axis=0)
        kt_ref[...] = k.T.astype(BF16)
        vb_ref[...] = v

    lane = _head_lanes()
    map_dim = HEAD_DIM // n_maps
    q = q_ref[...].astype(F32) * (map_dim ** -0.5 * LOG2_E)
    kt = kt_ref[...]
    vb = vb_ref[...]
    o = jnp.zeros(q.shape, F32)
    for h in range(HEADS):
        parts = []
        for j in range(n_maps):
            qm = jnp.where(_lane_range(lane, h * HEAD_DIM + j * map_dim, map_dim), q, 0.0)
            parts.append(_exp2_rows(jnp.dot(qm.astype(BF16), kt, preferred_element_type=F32)))
        w = parts[0][0] * parts[0][1]
        if n_maps == 2:
            w = w - parts[1][0] * (lam_ref[0] * parts[1][1])
        oh = jnp.dot(w.astype(BF16), vb, preferred_element_type=F32)
        o = jnp.where(_lane_range(lane, h * HEAD_DIM, HEAD_DIM), oh, o)
    if n_maps == 2:
        o = o * lax.rsqrt(_head_mean_square(o) + EPS) * g_ref[...] * post_scale
    o_ref[...] = o


def _cache_block_spec(layer):
    return pl.BlockSpec((1, 1, HEADS, PAST_LEN, HEAD_DIM), lambda b, i: (b, layer, 0, 0, 0))


def _cache_heads_on_lanes(ref):
    return jnp.concatenate([ref[0, 0, h] for h in range(HEADS)], axis=1)


def attention(p, row_blk0, cols, n_seq, seq_len, lam, norm_g, *, n_maps, post_scale, cache=None, layer=0):
    nb = seq_len // TM
    kv_len = seq_len + (PAST_LEN if cache is not None else 0)
    kern = functools.partial(_attn_kernel, n_maps=n_maps, post_scale=post_scale, with_cache=cache is not None)
    kv_spec = lambda col: pl.BlockSpec((seq_len, MIX_BLK), lambda b, i: (row_blk0 + b, col))
    cache_specs = [_cache_block_spec(layer)] * 2 if cache is not None else []
    return Part(
        kernel=kern,
        in_specs=[
            pl.BlockSpec(memory_space=pltpu.SMEM),
            pl.BlockSpec((TM, MIX_BLK), lambda b, i: ((row_blk0 + b) * nb + i, cols[0])),
            kv_spec(cols[1]), kv_spec(cols[2]), *cache_specs,
            pl.BlockSpec((1, MIX_BLK), lambda b, i: (0, 0)),
        ],
        out_specs=[pl.BlockSpec((TM, MIX_BLK), lambda b, i: (b * nb + i, 0))],
        out_shapes=[jax.ShapeDtypeStruct((n_seq * seq_len, MIX_BLK), F32)],
        scratch=[pltpu.VMEM((MIX_BLK, kv_len), BF16), pltpu.VMEM((kv_len, MIX_BLK), BF16)],
        args=[lam, p, p, p, *(cache or ()), norm_g])


def _na_slab_start(i):
    return jnp.clip(i - 1, 0, GRID_H // 4 - NA_SLAB_ROWS // 4)


def _na_kernel(q_ref, k_ref, v_ref, kc_ref, vc_ref, bias_ref, o_ref):
    i = pl.program_id(1)
    start = pl.multiple_of(_na_slab_start(i) * TM, TM)
    ks_t = k_ref[pl.ds(start, NA_SLAB), :].astype(F32).T.astype(BF16)
    vs = v_ref[pl.ds(start, NA_SLAB), :]
    kc_t = _cache_heads_on_lanes(kc_ref).T.astype(BF16)
    vc = _cache_heads_on_lanes(vc_ref).astype(BF16)
    q = q_ref[...].astype(F32) * (HEAD_DIM ** -0.5)
    lane = _head_lanes()
    o = jnp.zeros(q.shape, F32)
    for h in range(HEADS):
        in_head = _lane_range(lane, h * HEAD_DIM, HEAD_DIM)
        qm = jnp.where(in_head, q, 0.0).astype(BF16)
        s_loc = jnp.dot(qm, ks_t, preferred_element_type=F32) + bias_ref[0, h]
        s_ctx = jnp.dot(qm, kc_t, preferred_element_type=F32)
        m = jnp.maximum(jnp.max(s_loc, axis=-1, keepdims=True), jnp.max(s_ctx, axis=-1, keepdims=True))
        e_loc = jnp.exp(s_loc - m)
        e_ctx = jnp.exp(s_ctx - m)
        den = jnp.sum(e_loc, axis=-1, keepdims=True) + jnp.sum(e_ctx, axis=-1, keepdims=True)
        oh = (jnp.dot(e_loc.astype(BF16), vs, preferred_element_type=F32)
              + jnp.dot(e_ctx.astype(BF16), vc, preferred_element_type=F32)) / den
        o = jnp.where(in_head, oh, o)
    o_ref[...] = o


def _na_bias_tables(rpb):
    n_dr, n_dc = 2 * NA_WIN_H - 1, 2 * NA_WIN_W - 1
    cq = np.arange(GRID_W)[:, None]
    ck = np.arange(GRID_W)[None, :]
    wc0 = np.clip(cq - NA_WIN_W // 2, 0, GRID_W - NA_WIN_W)
    col_ok = (ck >= wc0) & (ck < wc0 + NA_WIN_W)
    col_pick = np.clip(ck - cq + NA_WIN_W - 1, 0, n_dc - 1)[..., None] == np.arange(n_dc)
    by_col = jnp.einsum("hab,qcb->haqc", rpb.astype(F32), jnp.asarray(col_pick, F32), precision=HIGHEST)
    margin = 4
    by_col = jnp.pad(by_col.transpose(0, 2, 1, 3), ((0, 0), (0, 0), (margin, margin), (0, 0)))
    by_col = by_col.reshape(HEADS, GRID_W, (n_dr + 2 * margin) * GRID_W)
    pieces, row_ok = [], []
    for tile in (0, 1, GRID_H // 4 - 1):
        slab0 = int(np.clip(tile - 1, 0, GRID_H // 4 - NA_SLAB_ROWS // 4)) * 4
        rq = tile * 4 + np.arange(4)
        rk = (slab0 + np.arange(NA_SLAB) // GRID_W)[None, :]
        wr0 = np.clip(rq - NA_WIN_H // 2, 0, GRID_H - NA_WIN_H)[:, None]
        row_ok.append((rk >= wr0) & (rk < wr0 + NA_WIN_H))
        for r in rq:
            first = slab0 - int(r) + NA_WIN_H - 1 + margin
            assert 0 <= first and first + NA_SLAB_ROWS <= n_dr + 2 * margin
            pieces.append(by_col[:, :, first * GRID_W:first * GRID_W + NA_SLAB])
    table = jnp.stack(pieces).reshape(3, 4, HEADS, GRID_W, NA_SLAB).transpose(0, 2, 1, 3, 4)
    valid = np.stack(row_ok)[:, None, :, None, :] & np.tile(col_ok, (1, NA_SLAB_ROWS))[None, None, None]
    table = jnp.where(jnp.asarray(valid), table, NEG_BIG)
    return table.reshape(3, HEADS, TM, NA_SLAB)


def na_latent(p, kc, vc, layer, bias):
    n_t = LAT_TILES_PER_SEQ
    seq_blk0 = TP // DEC_SEQ

    def bias_idx(b, i):
        return (jnp.minimum(i, 1) + i // (n_t - 1), 0, 0, 0)

    return Part(
        kernel=_na_kernel,
        in_specs=[
            pl.BlockSpec((TM, MIX_BLK), lambda b, i: (CTX_TILES + b * n_t + i, M_CQ)),
            pl.BlockSpec((DEC_SEQ, MIX_BLK), lambda b, i: (seq_blk0 + b, M_CK)),
            pl.BlockSpec((DEC_SEQ, MIX_BLK), lambda b, i: (seq_blk0 + b, M_CV)),
            _cache_block_spec(layer), _cache_block_spec(layer),
            pl.BlockSpec((1, HEADS, TM, NA_SLAB), bias_idx),
        ],
        out_specs=[pl.BlockSpec((TM, MIX_BLK), lambda b, i: (b * n_t + i, 0))],
        out_shapes=[jax.ShapeDtypeStruct((TL, MIX_BLK), F32)],
        scratch=[],
        args=[p, p, p, kc, vc, bias])


MAX_EXPONENT = 80.0


def _hgrn_direction(q_ref, f_ref, v_ref, lb, st_ref, o_ref, reverse):
    n_ch = TM // B_CHUNK
    r_idx = lax.broadcasted_iota(jnp.int32, (TM, TM), 0)
    c_idx = lax.broadcasted_iota(jnp.int32, (TM, TM), 1)
    tri = (c_idx >= r_idx) if reverse else (c_idx <= r_idx)
    zq = q_ref[...]
    q = zq * jax.nn.sigmoid(zq)
    z = f_ref[...]
    gate = (1.0 - lb) * jax.nn.sigmoid(z)
    logf = jnp.log(lb + gate)
    kk = (1.0 - lb) - gate
    b = _select_sum_left(jnp.where(tri, 1.0, 0.0).astype(BF16), logf)
    b3 = b.reshape(n_ch, B_CHUNK, MIX_BLK)
    mid = B_CHUNK // 2 if reverse else B_CHUNK // 2 - 1
    q_in = (q.reshape(b3.shape) * jnp.exp(b3 - b3[:, mid:mid + 1, :])).reshape(TM, MIX_BLK)
    q_dec = (q * jnp.exp(b)).astype(BF16)
    b_t = b.T
    kk_t = kk.T
    far = 0 if reverse else TM - 1
    b_far = b_t[:, far:far + 1]
    k_dec_t = (kk_t * jnp.exp(b_far - b_t)).astype(BF16)
    vb = v_ref[...].astype(BF16)
    st = st_ref[...]
    o_state = jnp.dot(q_dec, st.astype(BF16), preferred_element_type=F32)
    kv = jnp.dot(k_dec_t, vb, preferred_element_type=F32)
    st_ref[...] = st * jnp.exp(b_far) + jnp.where(_same_head_matrix(), kv, 0.0)
    lane = _head_lanes()
    token = lax.broadcasted_iota(jnp.int32, (1, TM), 1)
    local = lax.broadcasted_iota(jnp.int32, (HEADS * B_CHUNK, 1), 0) % B_CHUNK
    heads = [_lane_range(lane, h * HEAD_DIM, HEAD_DIM) for h in range(HEADS)]
    for c in range(n_ch):
        rows = slice(c * B_CHUNK, (c + 1) * B_CHUNK)
        ref = b_t[:, c * B_CHUNK + mid:c * B_CHUNK + mid + 1]
        k_c_t = (kk_t * jnp.exp(jnp.minimum(ref - b_t, MAX_EXPONENT))).astype(BF16)
        q_c = q_in[rows, :]
        lhs = jnp.concatenate([jnp.where(in_head, q_c, 0.0) for in_head in heads], axis=0)
        a = jnp.dot(lhs.astype(BF16), k_c_t, preferred_element_type=F32)
        t_abs = c * B_CHUNK + local
        a = jnp.where((token >= t_abs) if reverse else (token <= t_abs), a, 0.0)
        res = jnp.dot(a.astype(BF16), vb, preferred_element_type=F32)
        o_c = o_state[rows, :]
        for h, in_head in enumerate(heads):
            o_c = o_c + jnp.where(in_head, res[h * B_CHUNK:(h + 1) * B_CHUNK, :], 0.0)
        o_ref[rows, :] = o_c


def _hgrn_kernel(qf_ref, ff_ref, vf_ref, qb_ref, fb_ref, vb_ref, lb_ref, s0_ref,
                 of_ref, ob_ref, s_ref, stf_ref, stb_ref, *, has_s0):
    j = pl.program_id(1)

    @pl.when(j == 0)
    def _():
        if has_s0:
            stf_ref[...] = s0_ref[0, 0]
            stb_ref[...] = s0_ref[0, 1]
        else:
            stf_ref[...] = jnp.zeros((MIX_BLK, MIX_BLK), F32)
            stb_ref[...] = jnp.zeros((MIX_BLK, MIX_BLK), F32)

    lb = lb_ref[...]
    _hgrn_direction(qf_ref, ff_ref, vf_ref, lb[0:1], stf_ref, of_ref, False)
    _hgrn_direction(qb_ref, fb_ref, vb_ref, lb[1:2], stb_ref, ob_ref, True)

    @pl.when(j == pl.num_programs(1) - 1)
    def _():
        for d, st_ref in enumerate((stf_ref, stb_ref)):
            s = st_ref[...]
            for hd in range(HEADS):
                lo = hd * HEAD_DIM
                s_ref[0, d, hd] = s[lo:lo + HEAD_DIM, lo:lo + HEAD_DIM]


def hgrn(p, row_tile0, n_seq, seq_len, lb, s0):
    nb = seq_len // TM
    has_s0 = s0 is not None
    if s0 is None:
        s0 = jnp.zeros((1, 2, MIX_BLK, MIX_BLK), F32)

    def fwd(col):
        return pl.BlockSpec((TM, MIX_BLK), lambda s, j: (row_tile0 + s * nb + j, col))

    def bwd(col):
        return pl.BlockSpec((TM, MIX_BLK), lambda s, j: (row_tile0 + s * nb + nb - 1 - j, col))

    state_spec = pl.BlockSpec((1, 2, MIX_BLK, MIX_BLK), lambda s, j: (s if has_s0 else 0, 0, 0, 0))
    out_rows = n_seq * seq_len
    return Part(
        kernel=functools.partial(_hgrn_kernel, has_s0=has_s0),
        in_specs=[fwd(G_BQ), fwd(G_BFF), fwd(G_BV), bwd(G_BQ), bwd(G_BFB), bwd(G_BV),
                  pl.BlockSpec((2, MIX_BLK), lambda s, j: (0, 0)), state_spec],
        out_specs=[
            pl.BlockSpec((TM, MIX_BLK), lambda s, j: (s * nb + j, 0)),
            pl.BlockSpec((TM, MIX_BLK), lambda s, j: (s * nb + nb - 1 - j, 0)),
            pl.BlockSpec((1, 2, HEADS, HEAD_DIM, HEAD_DIM), lambda s, j: (s, 0, 0, 0, 0)),
        ],
        out_shapes=[jax.ShapeDtypeStruct((out_rows, MIX_BLK), F32),
                    jax.ShapeDtypeStruct((out_rows, MIX_BLK), F32),
                    jax.ShapeDtypeStruct((n_seq, 2, HEADS, HEAD_DIM, HEAD_DIM), F32)],
        scratch=[pltpu.VMEM((MIX_BLK, MIX_BLK), F32), pltpu.VMEM((MIX_BLK, MIX_BLK), F32)],
        args=[p, p, p, p, p, p, lb, s0])


def _state_to_blockdiag(s):
    eye = jnp.eye(HEADS, dtype=F32)
    full = s.astype(F32)[:, :, :, :, None, :] * eye[None, None, :, None, :, None]
    return full.reshape(s.shape[0], 2, MIX_BLK, MIX_BLK)


def _fft_kernel(u_ref, c64_ref, s64_ref, cl_ref, sl_ref, o_ref, a_ref, b_ref, *, norm):
    @pl.when(pl.program_id(1) == 0)
    def _():
        u = u_ref[...]
        a_ref[...] = jnp.dot(u, c64_ref[...], preferred_element_type=F32).astype(BF16)
        b_ref[...] = jnp.dot(u, s64_ref[...], preferred_element_type=F32).astype(BF16)

    o_ref[...] = (jnp.dot(cl_ref[...], a_ref[...], preferred_element_type=F32)
                  - jnp.dot(sl_ref[...], b_ref[...], preferred_element_type=F32)) * norm


def _dft_tables(n):
    k = np.arange(n)
    ang = 2.0 * np.pi * ((k[:, None] * k[None, :]) % n) / n
    return np.cos(ang), np.sin(ang)


def _dft_constants(seq_len):
    c64, s64 = _dft_tables(HEAD_DIM)
    eye = np.eye(HEADS)
    cl, sl = _dft_tables(seq_len)
    as_bf16 = lambda a: jnp.asarray(a, F32).astype(BF16)
    return as_bf16(np.kron(eye, c64)), as_bf16(np.kron(eye, s64)), as_bf16(cl), as_bf16(sl)


def fourier_mix(p, row_blk0, n_seq, seq_len, consts):
    c64, s64, cl, sl = consts
    nb = seq_len // TM
    norm = 1.0 / math.sqrt(seq_len * HEAD_DIM)
    return Part(
        kernel=functools.partial(_fft_kernel, norm=norm),
        in_specs=[
            pl.BlockSpec((seq_len, MIX_BLK), lambda s, i: (row_blk0 + s, M_DU)),
            pl.BlockSpec((MIX_BLK, MIX_BLK), lambda s, i: (0, 0)),
            pl.BlockSpec((MIX_BLK, MIX_BLK), lambda s, i: (0, 0)),
            pl.BlockSpec((TM, seq_len), lambda s, i: (i, 0)),
            pl.BlockSpec((TM, seq_len), lambda s, i: (i, 0)),
        ],
        out_specs=[pl.BlockSpec((TM, MIX_BLK), lambda s, i: (s * nb + i, 0))],
        out_shapes=[jax.ShapeDtypeStruct((n_seq * seq_len, MIX_BLK), F32)],
        scratch=[pltpu.VMEM((seq_len, MIX_BLK), BF16), pltpu.VMEM((seq_len, MIX_BLK), BF16)],
        args=[p, c64, s64, cl, sl])


def _route(logits_t, rb):
    per = N_EXPERTS // N_GROUPS
    score = [jax.nn.sigmoid(logits_t[e:e + 1, :]) for e in range(N_EXPERTS)]
    sel = [score[e] + rb[e:e + 1, :] for e in range(N_EXPERTS)]
    gscore = []
    for g in range(N_GROUPS):
        vals = sel[g * per:(g + 1) * per]
        best = None
        for a in range(per):
            for b in range(a + 1, per):
                pair = vals[a] + vals[b]
                best = pair if best is None else jnp.maximum(best, pair)
        gscore.append(best)
    chosen = []
    for g in range(N_GROUPS):
        ok = None
        for j in range(N_GROUPS):
            if j == g:
                continue
            cond = gscore[g] > gscore[j] if j < g else gscore[g] >= gscore[j]
            ok = cond if ok is None else ok & cond
        chosen.append(ok)
    picked = []
    for e in range(N_EXPERTS):
        g = e // per
        rank = jnp.zeros_like(sel[e])
        for j in range(g * per, (g + 1) * per):
            if j == e:
                continue
            ahead = sel[j] >= sel[e] if j < e else sel[j] > sel[e]
            rank = rank + jnp.where(ahead, 1.0, 0.0)
        picked.append(chosen[g] & (rank < 2.0))
    wsum = jnp.zeros_like(score[0])
    for e in range(N_EXPERTS):
        wsum = wsum + jnp.where(picked[e], score[e], 0.0)
    bucket = jnp.zeros_like(wsum)
    w_a = jnp.zeros_like(wsum)
    w_b = jnp.zeros_like(wsum)
    for g in range(N_GROUPS):
        for n, (a, b) in enumerate(EXPERT_PAIRS):
            hit = picked[g * per + a] & picked[g * per + b]
            bucket = jnp.where(hit, float(g * len(EXPERT_PAIRS) + n), bucket)
            w_a = jnp.where(hit, score[g * per + a] / wsum, w_a)
            w_b = jnp.where(hit, score[g * per + b] / wsum, w_b)
    return bucket, w_a, w_b


def _out_kernel(*refs):
    streams, rest = refs[:12], refs[12:]
    (bg_ref, mod_ref, hg_ref, w_ref, g2_ref, rw_ref, rb_ref,
     x1_ref, h2_ref, bucket_ref, rank_ref, counts_ref, run_ref) = rest
    is_ctx = pl.program_id(0) < TP // TM_OUT
    x, o_a, o_f, o_b, o_c, o_d = (jnp.where(is_ctx, streams[2 * n][...], streams[2 * n + 1][...])
                                  for n in range(6))

    @pl.when(pl.program_id(0) == 0)
    def _():
        run_ref[...] = jnp.zeros(run_ref.shape, F32)

    mod = mod_ref[0]
    hb = o_f + o_b
    zg = bg_ref[...]
    hb = hb * lax.rsqrt(_head_mean_square(hb) + EPS) * hg_ref[...] * (zg * jax.nn.sigmoid(zg))
    parts = (o_a, hb, o_c, o_d)
    mixed = jnp.zeros((TM_OUT, D_MODEL), F32)
    for n, part in enumerate(parts):
        mixed = mixed + jnp.dot(part.astype(BF16), w_ref[0, n * MIX_BLK:(n + 1) * MIX_BLK, :],
                                preferred_element_type=F32)
    x1 = x + mod[2:3] * mixed
    x1_ref[...] = x1
    ms = jnp.mean(x1 * x1, axis=-1, keepdims=True)
    h2 = x1 * lax.rsqrt(ms + EPS) * g2_ref[...] * (1.0 + mod[4:5]) + mod[3:4]
    rw = rw_ref[...]
    r = sum(jnp.dot(piece, rw, preferred_element_type=F32) for piece in _bf16_pieces(h2, 2))
    bucket, w_a, w_b = _route((r[:, :LANES] + r[:, LANES:]).T, rb_ref[...])
    h2_ref[:, :D_MODEL] = h2
    h2_ref[:, D_MODEL:] = jnp.concatenate([w_a, w_b, jnp.zeros((LANES - 2, TM_OUT), F32)], axis=0).T
    onehot = jnp.where(lax.broadcasted_iota(jnp.int32, (BUCKET_ROWS, 1), 0).astype(F32) == bucket, 1.0, 0.0)
    s_idx = lax.broadcasted_iota(jnp.int32, (TM_OUT, TM_OUT), 0)
    t_idx = lax.broadcasted_iota(jnp.int32, (TM_OUT, TM_OUT), 1)
    prefix = jnp.dot(onehot.astype(BF16), jnp.where(s_idx <= t_idx, 1.0, 0.0).astype(BF16),
                     preferred_element_type=F32)
    run = run_ref[...]
    rank = jnp.sum(onehot * (prefix - 1.0 + run[:, 0:1]), axis=0, keepdims=True)
    run = run + jnp.sum(onehot, axis=1, keepdims=True)
    run_ref[...] = run
    bucket_ref[...] = bucket.astype(jnp.int32)
    rank_ref[...] = rank.astype(jnp.int32)
    counts_ref[...] = run


def out_and_route(x_pair, x_is_combined, mixer_pairs, p, mod, layer, hgrn_g, w_out_bf16, norm2_g, router_pieces,
                  router_b):
    n_ctx = TP // TM_OUT
    ctx_tile = lambda i: (jnp.minimum(i, n_ctx - 1), 0)
    lat_tile = lambda i: (jnp.maximum(i - n_ctx, 0), 0)
    mod_row = lambda i: jnp.where(i < n_ctx, 0, 1 + (i - n_ctx) // (DEC_SEQ // TM_OUT))
    tile = lambda w: pl.BlockSpec((TM_OUT, w), lambda i: (i, 0))
    full = lambda r, c: pl.BlockSpec((r, c), lambda i: (0, 0))
    stream_specs = [pl.BlockSpec((TM_OUT, D_MODEL), ctx_tile),
                    pl.BlockSpec((TM_OUT, D_MODEL),
                                 (lambda i: (jnp.maximum(i, n_ctx), 0)) if x_is_combined else lat_tile)]
    stream_args = list(x_pair)
    for o_ctx, o_lat in mixer_pairs:
        stream_specs += [pl.BlockSpec((TM_OUT, MIX_BLK), ctx_tile), pl.BlockSpec((TM_OUT, MIX_BLK), lat_tile)]
        stream_args += [o_ctx, o_lat]
    return pl.pallas_call(
        _out_kernel,
        grid=(T // TM_OUT,),
        in_specs=stream_specs + [
            pl.BlockSpec((TM_OUT, MIX_BLK), lambda i: (i, G_BG)),
            pl.BlockSpec((1, 6, D_MODEL), lambda i: (mod_row(i), 0, 0)),
            full(1, MIX_BLK), pl.BlockSpec((1, D_MODEL, D_MODEL), lambda i: (layer, 0, 0)), full(1, D_MODEL),
            full(D_MODEL, 2 * LANES), full(N_EXPERTS, 1),
        ],
        out_specs=[tile(D_MODEL), tile(ROW_W), pl.BlockSpec((1, TM_OUT), lambda i: (0, i)),
                   pl.BlockSpec((1, TM_OUT), lambda i: (0, i)), full(BUCKET_ROWS, LANES)],
        out_shape=[jax.ShapeDtypeStruct((T, D_MODEL), F32),
                   jax.ShapeDtypeStruct((T, ROW_W), F32),
                   jax.ShapeDtypeStruct((1, T), jnp.int32),
                   jax.ShapeDtypeStruct((1, T), jnp.int32),
                   jax.ShapeDtypeStruct((BUCKET_ROWS, LANES), F32)],
        scratch_shapes=[pltpu.VMEM((BUCKET_ROWS, LANES), F32)],
        compiler_params=_cparams(("arbitrary",)),
        name="out_and_route",
    )(*stream_args, p, mod, jnp.tile(hgrn_g, HEADS)[None, :], w_out_bf16,
      norm2_g[None, :], router_pieces, router_b[:, None])


def _router_pieces(router_w):
    hi, lo = _bf16_pieces(router_w.astype(F32), 2)
    pad = lambda a: jnp.pad(a, ((0, 0), (0, LANES - N_EXPERTS)))
    return jnp.concatenate([pad(hi), pad(lo)], axis=1)


def routing_plan(bucket, rank, counts):
    counts = counts[:N_BUCKETS, 0].astype(jnp.int32)
    n_tiles = (counts + TM_MOE - 1) // TM_MOE
    tile_end = jnp.cumsum(n_tiles)
    tile_start = tile_end - n_tiles
    buckets = jnp.arange(N_BUCKETS, dtype=jnp.int32)
    start_of_token = jnp.sum(jnp.where(bucket[0][:, None] == buckets[None, :], tile_start[None, :], 0), axis=1)
    dest = start_of_token * TM_MOE + rank[0]
    tiles = jnp.arange(MAX_TILES, dtype=jnp.int32)
    valid = tiles < tile_end[-1]
    tile_bucket = jnp.sum((jnp.minimum(tiles, tile_end[-1] - 1)[:, None] >= tile_end[None, :]).astype(jnp.int32), axis=1)
    pair_a = np.array([a for a, _ in EXPERT_PAIRS], np.int32)
    pair_b = np.array([b for _, b in EXPERT_PAIRS], np.int32)
    per = N_EXPERTS // N_GROUPS
    exp_a = jnp.asarray((np.arange(N_BUCKETS) // len(EXPERT_PAIRS)) * per + np.tile(pair_a, N_GROUPS), jnp.int32)
    exp_b = jnp.asarray((np.arange(N_BUCKETS) // len(EXPERT_PAIRS)) * per + np.tile(pair_b, N_GROUPS), jnp.int32)
    pick = tile_bucket[:, None] == buckets[None, :]
    tile_a = jnp.sum(jnp.where(pick, exp_a[None, :], 0), axis=1)
    tile_b = jnp.sum(jnp.where(pick, exp_b[None, :], 0), axis=1)
    return dest.astype(jnp.int32), tile_a, tile_b, valid.astype(jnp.int32), (tile_end[-1:] - 1).astype(jnp.int32)


def _row_copy(src, src_row, dst, dst_row, sem):
    return pltpu.make_async_copy(src.at[pl.ds(src_row, 1), :], dst.at[pl.ds(dst_row, 1), :], sem)


def _scatter_kernel(dest_ref, h_ref, init_ref, o_ref, sem):
    del init_ref
    base = pl.program_id(0) * TM

    for r in range(TM):
        _row_copy(h_ref, r, o_ref, dest_ref[base + r], sem).start(priority=r % N_DMA_PRIORITIES)
    pltpu.make_async_copy(h_ref, o_ref.at[pl.ds(0, TM), :], sem).wait()


def scatter_to_slots(h2, dest, slots):
    return pl.pallas_call(
        _scatter_kernel,
        grid_spec=pltpu.PrefetchScalarGridSpec(
            num_scalar_prefetch=1,
            grid=(N_TILES,),
            in_specs=[pl.BlockSpec((TM, ROW_W), lambda i, d: (i, 0)),
                      pl.BlockSpec(memory_space=pl.ANY)],
            out_specs=pl.BlockSpec(memory_space=pl.ANY),
            scratch_shapes=[pltpu.SemaphoreType.DMA(())],
        ),
        out_shape=jax.ShapeDtypeStruct((N_SLOTS, ROW_W), F32),
        input_output_aliases={2: 0},
        compiler_params=_cparams(("arbitrary",)),
        name="scatter_to_slots",
    )(dest, h2, slots)


def _moe_kernel(ta_ref, tb_ref, valid_ref, last_ref, h_ref, wga_ref, wua_ref, wda_ref, wgb_ref, wub_ref, wdb_ref,
                o_ref):
    del ta_ref, tb_ref, last_ref
    i = pl.program_id(0)

    @pl.when(valid_ref[i] == 1)
    def _():
        x = h_ref[:, :D_MODEL].astype(BF16)
        gates = h_ref[:, D_MODEL:]
        y = jnp.zeros((TM_MOE, D_MODEL), F32)
        for n, (wg, wu, wd) in enumerate(((wga_ref, wua_ref, wda_ref), (wgb_ref, wub_ref, wdb_ref))):
            a = jnp.dot(x, wg[0, 0].astype(BF16), preferred_element_type=F32)
            u = jnp.dot(x, wu[0, 0].astype(BF16), preferred_element_type=F32)
            z = a * jax.nn.sigmoid(a) * u * gates[:, n:n + 1]
            y = y + jnp.dot(z.astype(BF16), wd[0, 0].astype(BF16), preferred_element_type=F32)
        o_ref[...] = y

    @pl.when(valid_ref[i] == 0)
    def _():
        o_ref[...] = jnp.zeros((TM_MOE, D_MODEL), F32)


def moe(h_slots, tile_a, tile_b, valid, last, layer, wg, wu, wd):
    up_a = pl.BlockSpec((1, 1, D_MODEL, D_EXPERT), lambda i, ta, tb, v, last: (layer, ta[i], 0, 0))
    up_b = pl.BlockSpec((1, 1, D_MODEL, D_EXPERT), lambda i, ta, tb, v, last: (layer, tb[i], 0, 0))
    down_a = pl.BlockSpec((1, 1, D_EXPERT, D_MODEL), lambda i, ta, tb, v, last: (layer, ta[i], 0, 0))
    down_b = pl.BlockSpec((1, 1, D_EXPERT, D_MODEL), lambda i, ta, tb, v, last: (layer, tb[i], 0, 0))
    return pl.pallas_call(
        _moe_kernel,
        grid_spec=pltpu.PrefetchScalarGridSpec(
            num_scalar_prefetch=4,
            grid=(MAX_TILES,),
            in_specs=[pl.BlockSpec((TM_MOE, ROW_W), lambda i, ta, tb, v, last: (jnp.minimum(i, last[0]), 0)),
                      up_a, up_a, down_a, up_b, up_b, down_b],
            out_specs=pl.BlockSpec((TM_MOE, D_MODEL), lambda i, ta, tb, v, last: (i, 0)),
        ),
        out_shape=jax.ShapeDtypeStruct((N_SLOTS, D_MODEL), F32),
        compiler_params=_cparams(("arbitrary",)),
        name="moe",
    )(tile_a, tile_b, valid, last, h_slots, wg, wu, wd, wg, wu, wd)


def _gather_tile(dest_ref, y_ref, buf_ref, sem, tile, slot):
    for r in range(TM):
        pltpu.make_async_copy(y_ref.at[pl.ds(dest_ref[tile * TM + r], 1), :],
                              buf_ref.at[slot, pl.ds(r, 1), :], sem.at[slot]).start(priority=r % N_DMA_PRIORITIES)


def _moe_residual(dest_ref, y_ref, buf_ref, sem, x1_ref, mod_ref):
    i = pl.program_id(0)
    slot = i % 2

    @pl.when(i == 0)
    def _():
        _gather_tile(dest_ref, y_ref, buf_ref, sem, 0, 0)

    @pl.when(i + 1 < pl.num_programs(0))
    def _():
        _gather_tile(dest_ref, y_ref, buf_ref, sem, i + 1, 1 - slot)

    pltpu.make_async_copy(y_ref.at[pl.ds(0, TM), :], buf_ref.at[slot], sem.at[slot]).wait()
    return x1_ref[...] + mod_ref[0][5:6] * buf_ref[slot]


def _final_kernel(dest_ref, y_ref, x1_ref, mod_ref, g_ref, oc_ref, ol_ref, buf_ref, sem):
    x2 = _moe_residual(dest_ref, y_ref, buf_ref, sem, x1_ref, mod_ref)
    ms = jnp.mean(x2 * x2, axis=-1, keepdims=True)
    y = x2 * lax.rsqrt(ms + EPS) * g_ref[...]

    @pl.when(_is_ctx_tile())
    def _():
        oc_ref[...] = y

    @pl.when(jnp.logical_not(_is_ctx_tile()))
    def _():
        ol_ref[...] = y


_GATHER_SCRATCH = [pltpu.VMEM((2, TM, D_MODEL), F32), pltpu.SemaphoreType.DMA((2,))]


def final_norm(dest, y_slots, x1, mod, final_g):
    return pl.pallas_call(
        _final_kernel,
        grid_spec=pltpu.PrefetchScalarGridSpec(
            num_scalar_prefetch=1,
            grid=(N_TILES,),
            in_specs=[pl.BlockSpec(memory_space=pl.ANY),
                      pl.BlockSpec((TM, D_MODEL), lambda i, d: (i, 0)),
                      pl.BlockSpec((1, 6, D_MODEL), lambda i, d: (_mod_row(i), 0, 0)),
                      pl.BlockSpec((1, D_MODEL), lambda i, d: (0, 0))],
            out_specs=[pl.BlockSpec((TM, D_MODEL), _ctx_tile), pl.BlockSpec((TM, D_MODEL), _lat_tile)],
            scratch_shapes=_GATHER_SCRATCH,
        ),
        out_shape=[jax.ShapeDtypeStruct((TP, D_MODEL), F32), jax.ShapeDtypeStruct((TL, D_MODEL), F32)],
        compiler_params=_cparams(("arbitrary",)),
        name="final_norm",
    )(dest, y_slots, x1, mod, final_g[None, :])


def kernel(x_prompt, x_sample, cache_diff_k, cache_diff_v, cache_na_k, cache_na_v, state_hgrn, c, c_ctx,
           norm1_g, norm2_g, ada_w, ada_b, w_in, w_out, diff_lambda, diff_subln_g, hgrn_lb_logits,
           hgrn_norm_g, na_rpb, router_w, router_b, moe_w_gate, moe_w_up, moe_w_down, final_norm_g):
    assert SEQ == TM and PAST_LEN == TM and DEC_SEQ % TM_OUT == 0 and TP % DEC_SEQ == 0
    x_pair = (x_prompt.reshape(TP, D_MODEL), x_sample.reshape(TL, D_MODEL))
    w_in_bf16 = jnp.concatenate([w_in[:, :, c * MIX_BLK:(c + 1) * MIX_BLK] for c in PM_BLOCKS + PG_BLOCKS],
                                axis=-1).astype(BF16)
    w_out_bf16 = w_out.astype(BF16)
    router_pieces = _router_pieces(router_w)
    mods = modulation(jnp.concatenate([c_ctx[None, :], c], axis=0), ada_w, ada_b)
    mods = mods.reshape(DEPTH, 3, 6, D_MODEL)
    lb_sm = jax.nn.softmax(hgrn_lb_logits.astype(F32), axis=0)
    lb_all = jnp.cumsum(lb_sm, axis=0) - lb_sm[0:1]
    rope = _rope_tables()
    dft_ctx = _dft_constants(SEQ)
    dft_lat = _dft_constants(DEC_SEQ)
    lat_blk0 = TP // DEC_SEQ
    states = []
    moe_state = None
    for l in range(DEPTH):
        if moe_state is None:
            pm, pg, *new_kv = projection(*x_pair, mods[l], norm1_g[l], w_in_bf16, rope)
        else:
            x, pm, pg, *new_kv = projection_after_moe(*moe_state, mods[l - 1], mods[l], l, norm1_g[l], w_in_bf16, rope,
                                                 new_kv)
            x_pair = (x, x)

        lq = diff_lambda[l].astype(F32)
        lam_init = 0.8 - 0.6 * math.exp(-0.3 * l)
        lam = (jnp.exp(jnp.sum(lq[0] * lq[1])) - jnp.exp(jnp.sum(lq[2] * lq[3])) + lam_init).reshape(1)
        subln = jnp.tile(diff_subln_g[l], HEADS)[None, :]
        diff = functools.partial(attention, pm, cols=(M_AQ, M_AK, M_AV), lam=lam, norm_g=subln, n_maps=2,
                                 post_scale=1.0 - lam_init)
        (oa_ctx,), (of_ctx, ob_ctx, st_ctx), (oc_ctx,), (od_ctx,) = run_parts(
            [diff(row_blk0=0, n_seq=BATCH, seq_len=SEQ),
             hgrn(pg, 0, BATCH, SEQ, lb_all[l], None),
             attention(pm, 0, (M_CQ, M_CK, M_CV), BATCH, SEQ, lam, subln, n_maps=1, post_scale=1.0),
             fourier_mix(pm, 0, BATCH, SEQ, dft_ctx)],
            (BATCH, SEQ // TM), "context_mixers")
        (oa_lat,), (of_lat, ob_lat, _), (oc_lat,), (od_lat,) = run_parts(
            [diff(row_blk0=lat_blk0, n_seq=DEC_BATCH, seq_len=DEC_SEQ, cache=(cache_diff_k, cache_diff_v), layer=l),
             hgrn(pg, CTX_TILES, DEC_BATCH, DEC_SEQ, lb_all[l], _state_to_blockdiag(state_hgrn[:, l])),
             na_latent(pm, cache_na_k, cache_na_v, l, _na_bias_tables(na_rpb[l])),
             fourier_mix(pm, lat_blk0, DEC_BATCH, DEC_SEQ, dft_lat)],
            (DEC_BATCH, DEC_SEQ // TM), "latent_mixers")

        x1, h2, bucket, rank, counts = out_and_route(
            x_pair, l > 0, ((oa_ctx, oa_lat), (of_ctx, of_lat), (ob_ctx, ob_lat), (oc_ctx, oc_lat),
                            (od_ctx, od_lat)),
            pg, mods[l], l, hgrn_norm_g[l], w_out_bf16, norm2_g[l], router_pieces, router_b)
        dest, *tile_plan = routing_plan(bucket, rank, counts)
        h_slots = scatter_to_slots(h2, dest, jnp.zeros((N_SLOTS, ROW_W), F32) if l == 0 else h_slots)
        y_slots = moe(h_slots, *tile_plan, l, moe_w_gate, moe_w_up, moe_w_down)
        moe_state = (dest, y_slots, x1)

        states.append(st_ctx)
    y_prompt, y_sample = final_norm(*moe_state, mods[DEPTH - 1], final_norm_g)
    return (y_prompt.reshape(BATCH, SEQ, D_MODEL), y_sample.reshape(DEC_BATCH, DEC_SEQ, D_MODEL),
            *new_kv, jnp.stack(states, axis=1))
```

```python
import functools
import math
from typing import Any, NamedTuple

import numpy as np
import jax
import jax.numpy as jnp
from jax import lax
from jax.experimental import pallas as pl
from jax.experimental.pallas import tpu as pltpu

F32 = jnp.float32
BF16 = jnp.bfloat16
HIGHEST = lax.Precision.HIGHEST

D_MODEL = 1024
BATCH = 16
SEQ = 256
DEPTH = 2
DEC_BATCH = 2
DEC_SEQ = 2048
PAST_LEN = 256
GRID_W = 64
GRID_H = DEC_SEQ // GRID_W
EPS = 1e-6
NEG_BIG = -1e30
HEADS = 4
HEAD_DIM = 64
MIX_BLK = HEADS * HEAD_DIM
A_DIM = 32
ROPE_BASE = 10000.0
B_CHUNK = 32
NA_WIN_H = 8
NA_WIN_W = 16
N_EXPERTS = 16
N_GROUPS = 4
D_EXPERT = 512
PROJ_W = 12 * MIX_BLK
TP = BATCH * SEQ
TL = DEC_BATCH * DEC_SEQ
T = TP + TL
TM = 256
N_TILES = T // TM
CTX_TILES = TP // TM
LAT_TILES_PER_SEQ = DEC_SEQ // TM
(C_AQ, C_AK, C_AV, C_BQ, C_BFF, C_BFB, C_BV, C_BG, C_CQ, C_CK, C_CV, C_DU) = range(12)
PM_BLOCKS = (C_AQ, C_AK, C_AV, C_CQ, C_CK, C_CV, C_DU)
PG_BLOCKS = (C_BQ, C_BFF, C_BFB, C_BV, C_BG)
(M_AQ, M_AK, M_AV, M_CQ, M_CK, M_CV, M_DU) = range(len(PM_BLOCKS))
(G_BQ, G_BFF, G_BFB, G_BV, G_BG) = range(len(PG_BLOCKS))
PM_W = len(PM_BLOCKS) * MIX_BLK
PG_W = len(PG_BLOCKS) * MIX_BLK
NA_SLAB_ROWS = 12
NA_SLAB = NA_SLAB_ROWS * GRID_W
LANES = 128
ROW_W = D_MODEL + LANES
EXPERT_PAIRS = ((0, 1), (0, 2), (0, 3), (1, 3), (2, 3), (2, 1))
N_BUCKETS = N_GROUPS * len(EXPERT_PAIRS)
BUCKET_ROWS = 32
N_DMA_PRIORITIES = 2
TM_OUT = 512
TM_MOE = 384
MAX_TILES = -(-T // TM_MOE) + N_BUCKETS
N_SLOTS = MAX_TILES * TM_MOE
VMEM_LIMIT = 56 * 1024 * 1024


def _cparams(sem):
    return pltpu.CompilerParams(dimension_semantics=sem, vmem_limit_bytes=VMEM_LIMIT)


class Part(NamedTuple):
    kernel: Any
    in_specs: list
    args: list
    out_specs: list
    out_shapes: list
    scratch: list


def _run_parts_kernel(*refs, layout):
    n_in = sum(n for _, n, _, _ in layout)
    n_out = sum(n for _, _, n, _ in layout)
    ins, outs, scratch = refs[:n_in], refs[n_in:n_in + n_out], refs[n_in + n_out:]
    i = o = s = 0
    for kernel, k_in, k_out, k_scratch in layout:
        kernel(*ins[i:i + k_in], *outs[o:o + k_out], *scratch[s:s + k_scratch])
        i, o, s = i + k_in, o + k_out, s + k_scratch


def run_parts(parts, grid, name):
    layout = tuple((p.kernel, len(p.in_specs), len(p.out_specs), len(p.scratch)) for p in parts)
    outs = pl.pallas_call(
        functools.partial(_run_parts_kernel, layout=layout),
        grid=grid,
        in_specs=[s for p in parts for s in p.in_specs],
        out_specs=[s for p in parts for s in p.out_specs],
        out_shape=[s for p in parts for s in p.out_shapes],
        scratch_shapes=[s for p in parts for s in p.scratch],
        compiler_params=_cparams(("arbitrary", "arbitrary")),
        name=name,
    )(*[a for p in parts for a in p.args])
    result, o = [], 0
    for p in parts:
        result.append(outs[o:o + len(p.out_specs)])
        o += len(p.out_specs)
    return result


def _head_lanes(width=MIX_BLK):
    return lax.broadcasted_iota(jnp.int32, (1, width), 1)


def _lane_range(lane, lo, n):
    return (lane >= lo) & (lane < lo + n)


def _same_head_matrix():
    r = lax.broadcasted_iota(jnp.int32, (MIX_BLK, MIX_BLK), 0) // HEAD_DIM
    c = lax.broadcasted_iota(jnp.int32, (MIX_BLK, MIX_BLK), 1) // HEAD_DIM
    return r == c


def _bf16_pieces(x, n):
    pieces = []
    for _ in range(n):
        piece = x.astype(BF16)
        pieces.append(piece)
        x = x - piece.astype(F32)
    return pieces


def _select_sum_left(onehot_bf16, x):
    return sum(jnp.dot(onehot_bf16, piece, preferred_element_type=F32) for piece in _bf16_pieces(x, 3))


def _select_sum_right(x, onehot_bf16):
    return sum(jnp.dot(piece, onehot_bf16, preferred_element_type=F32) for piece in _bf16_pieces(x, 3))


def _head_mean_square(o):
    ones = jnp.where(_same_head_matrix(), 1.0, 0.0).astype(BF16)
    return _select_sum_right(o * o, ones) * (1.0 / HEAD_DIM)


def _mod_row(i):
    return jnp.where(i < CTX_TILES, 0, 1 + (i - CTX_TILES) // LAT_TILES_PER_SEQ)


def _mod_kernel(c_ref, w_ref, b_ref, o_ref):
    w = w_ref[0]
    for r in range(3):
        c = c_ref[r]
        s = c * jax.nn.sigmoid(c)
        o_ref[0, r:r + 1, :] = jnp.sum(s * w, axis=0, keepdims=True) + b_ref[0]


def modulation(c_rows, ada_w, ada_b):
    nt = 768
    n_out = 6 * D_MODEL
    return pl.pallas_call(
        _mod_kernel,
        grid=(DEPTH, n_out // nt),
        in_specs=[
            pl.BlockSpec((3, D_MODEL, 1), lambda l, j: (0, 0, 0)),
            pl.BlockSpec((1, D_MODEL, nt), lambda l, j: (l, 0, j)),
            pl.BlockSpec((1, 1, nt), lambda l, j: (l, 0, j)),
        ],
        out_specs=pl.BlockSpec((1, 3, nt), lambda l, j: (l, 0, j)),
        out_shape=jax.ShapeDtypeStruct((DEPTH, 3, n_out), F32),
        compiler_params=_cparams(("arbitrary", "arbitrary")),
        name="modulation",
    )(c_rows[:, :, None], ada_w, ada_b[:, None, :])


def _is_ctx_tile():
    return pl.program_id(0) < CTX_TILES


def _ctx_tile(i, *_):
    return (jnp.minimum(i, CTX_TILES - 1), 0)


def _lat_tile(i, *_):
    return (jnp.maximum(i - CTX_TILES, 0), 0)


def _proj_kernel(xc_ref, xl_ref, mod_ref, g_ref, w_ref, cos_ref, sa_ref, sb_ref, pm_ref, pg_ref, *cache_refs):
    x = jnp.where(_is_ctx_tile(), xc_ref[...], xl_ref[...])
    _proj_body(x, mod_ref, g_ref, w_ref, cos_ref, sa_ref, sb_ref, pm_ref, pg_ref, cache_refs)


def _proj_after_moe_kernel(dest_ref, y_ref, x1_ref, modp_ref, mod_ref, g_ref, w_ref, cos_ref, sa_ref, sb_ref,
                           *rest):
    x2_ref, pm_ref, pg_ref = rest[4:7]
    cache_refs, (buf_ref, sem) = rest[7:11], rest[11:]
    x2 = _moe_residual(dest_ref, y_ref, buf_ref, sem, x1_ref, modp_ref)
    x2_ref[...] = x2
    _proj_body(x2, mod_ref, g_ref, w_ref, cos_ref, sa_ref, sb_ref, pm_ref, pg_ref, cache_refs)


def _proj_body(x, mod_ref, g_ref, w_ref, cos_ref, sa_ref, sb_ref, pm_ref, pg_ref, cache_refs):
    ms = jnp.mean(x * x, axis=-1, keepdims=True)
    mod = mod_ref[0]
    h = x * lax.rsqrt(ms + EPS) * g_ref[...] * (1.0 + mod[1:2]) + mod[0:1]
    p = jnp.dot(h.astype(BF16), w_ref[0], preferred_element_type=F32)
    t = p[:, :2 * MIX_BLK]
    pm_ref[:, :2 * MIX_BLK] = (t * cos_ref[...] + pltpu.roll(t, 1, 1) * sa_ref[...]
                               + pltpu.roll(t, 2 * MIX_BLK - 1, 1) * sb_ref[...]).astype(BF16)
    pm_ref[:, 2 * MIX_BLK:] = p[:, 2 * MIX_BLK:PM_W].astype(BF16)
    pg_ref[...] = p[:, PM_W:]

    @pl.when(_is_ctx_tile())
    def _():
        for ref, col in zip(cache_refs, (M_AK, M_AV, M_CK, M_CV)):
            for hd in range(HEADS):
                lo = col * MIX_BLK + hd * HEAD_DIM
                ref[0, 0, hd] = p[:, lo:lo + HEAD_DIM]
            if ref.shape[1] > 1:
                ref[0, 1:] = jnp.zeros((ref.shape[1] - 1,) + tuple(ref.shape[2:]), F32)


_CACHE_SHAPE = jax.ShapeDtypeStruct((BATCH, DEPTH, HEADS, SEQ, HEAD_DIM), F32)


def _cache_spec(layer):
    n_layers = DEPTH if layer == 0 else 1
    return pl.BlockSpec((1, n_layers, HEADS, SEQ, HEAD_DIM),
                        lambda i, *_: (jnp.minimum(i, CTX_TILES - 1), layer, 0, 0, 0))


def _rope_tables():
    nf = A_DIM // 4
    freqs = ROPE_BASE ** (-np.arange(nf, dtype=np.float64) / nf)
    pos = np.arange(DEC_SEQ)
    row = (pos // GRID_W).astype(np.float64)
    col = (pos % GRID_W).astype(np.float64)
    ang = np.concatenate([row[:, None] * freqs, col[:, None] * freqs], axis=-1)
    cos = np.repeat(np.cos(ang), 2, axis=-1)
    sin = np.repeat(np.sin(ang), 2, axis=-1)
    odd = (np.arange(A_DIM) % 2 == 1)[None, :]
    sin_from_left = np.where(odd, sin, 0.0)
    sin_from_right = np.where(odd, 0.0, -sin)
    reps = 2 * MIX_BLK // A_DIM
    ident = (np.ones((TM, 2 * MIX_BLK)), np.zeros((TM, 2 * MIX_BLK)))
    return tuple(jnp.asarray(np.concatenate([np.tile(t, (1, reps)), tail], axis=0), F32)
                 for t, tail in ((cos, ident[0]), (sin_from_left, ident[1]), (sin_from_right, ident[1])))


def _rope_block(i):
    return (jnp.where(i < CTX_TILES, LAT_TILES_PER_SEQ, (i - CTX_TILES) % LAT_TILES_PER_SEQ), 0)


def projection_after_moe(dest, y_slots, x1, mod_prev, mod, layer, norm_g, w_in_bf16, rope, caches):
    rope_spec = pl.BlockSpec((TM, 2 * MIX_BLK), lambda i, d: _rope_block(i))
    mod_spec = pl.BlockSpec((1, 6, D_MODEL), lambda i, d: (_mod_row(i), 0, 0))
    n_in = 10
    return pl.pallas_call(
        _proj_after_moe_kernel,
        grid_spec=pltpu.PrefetchScalarGridSpec(
            num_scalar_prefetch=1,
            grid=(N_TILES,),
            in_specs=[pl.BlockSpec(memory_space=pl.ANY),
                      pl.BlockSpec((TM, D_MODEL), lambda i, d: (i, 0)),
                      mod_spec, mod_spec,
                      pl.BlockSpec((1, D_MODEL), lambda i, d: (0, 0)),
                      pl.BlockSpec((1, D_MODEL, PROJ_W), lambda i, d: (layer, 0, 0)),
                      rope_spec, rope_spec, rope_spec] + [pl.BlockSpec(memory_space=pl.ANY)] * 4,
            out_specs=[pl.BlockSpec((TM, D_MODEL), lambda i, d: (i, 0)),
                       pl.BlockSpec((TM, PM_W), lambda i, d: (i, 0)),
                       pl.BlockSpec((TM, PG_W), lambda i, d: (i, 0))] + [_cache_spec(layer)] * 4,
            scratch_shapes=_GATHER_SCRATCH,
        ),
        out_shape=[jax.ShapeDtypeStruct((T, D_MODEL), F32), jax.ShapeDtypeStruct((T, PM_W), BF16),
                   jax.ShapeDtypeStruct((T, PG_W), F32)] + [_CACHE_SHAPE] * 4,
        input_output_aliases={n_in + n: 3 + n for n in range(4)},
        compiler_params=_cparams(("arbitrary",)),
        name="projection_after_moe",
    )(dest, y_slots, x1, mod_prev, mod, norm_g[None, :], w_in_bf16, *rope, *caches)


def projection(x_ctx, x_lat, mod, norm_g, w_in_bf16, rope):
    layer = 0
    rope_spec = pl.BlockSpec((TM, 2 * MIX_BLK), _rope_block)
    return pl.pallas_call(
        _proj_kernel,
        grid=(N_TILES,),
        in_specs=[
            pl.BlockSpec((TM, D_MODEL), _ctx_tile),
            pl.BlockSpec((TM, D_MODEL), _lat_tile),
            pl.BlockSpec((1, 6, D_MODEL), lambda i: (_mod_row(i), 0, 0)),
            pl.BlockSpec((1, D_MODEL), lambda i: (0, 0)),
            pl.BlockSpec((1, D_MODEL, PROJ_W), lambda i: (layer, 0, 0)),
            rope_spec, rope_spec, rope_spec,
        ],
        out_specs=[pl.BlockSpec((TM, PM_W), lambda i: (i, 0)), pl.BlockSpec((TM, PG_W), lambda i: (i, 0))]
        + [_cache_spec(layer)] * 4,
        out_shape=[jax.ShapeDtypeStruct((T, PM_W), BF16), jax.ShapeDtypeStruct((T, PG_W), F32)]
        + [_CACHE_SHAPE] * 4,
        compiler_params=_cparams(("arbitrary",)),
        name="projection",
    )(x_ctx, x_lat, mod, norm_g[None, :], w_in_bf16, *rope)


LOG2_E = 1.4426950408889634


def _exp2_rows(s):
    e = jnp.exp2(s - jnp.max(s, axis=-1, keepdims=True))
    return e, 1.0 / jnp.sum(e, axis=-1, keepdims=True)


def _attn_kernel(lam_ref, q_ref, k_ref, v_ref, *rest, n_maps, post_scale, with_cache):
    if with_cache:
        kc_ref, vc_ref, g_ref, o_ref, kt_ref, vb_ref = rest
    else:
        g_ref, o_ref, kt_ref, vb_ref = rest

    @pl.when(pl.program_id(1) == 0)
    def _():
        k = k_ref[...].astype(F32)
        v = v_ref[...]
        if with_cache:
            k = jnp.concatenate([_cache_heads_on_lanes(kc_ref), k], axis=0)
            v = jnp.concatenate([_cache_heads_on_lanes(vc_ref).astype(BF16), v], axis=0)
        kt_ref[...] = k.T.astype(BF16)
        vb_ref[...] = v

    lane = _head_lanes()
    map_dim = HEAD_DIM // n_maps
    q = q_ref[...].astype(F32) * (map_dim ** -0.5 * LOG2_E)
    kt = kt_ref[...]
    vb = vb_ref[...]
    o = jnp.zeros(q.shape, F32)
    for h in range(HEADS):
        parts = []
        for j in range(n_maps):
            qm = jnp.where(_lane_range(lane, h * HEAD_DIM + j * map_dim, map_dim), q, 0.0)
            parts.append(_exp2_rows(jnp.dot(qm.astype(BF16), kt, preferred_element_type=F32)))
        w = parts[0][0] * parts[0][1]
        if n_maps == 2:
            w = w - parts[1][0] * (lam_ref[0] * parts[1][1])
        oh = jnp.dot(w.astype(BF16), vb, preferred_element_type=F32)
        o = jnp.where(_lane_range(lane, h * HEAD_DIM, HEAD_DIM), oh, o)
    if n_maps == 2:
        o = o * lax.rsqrt(_head_mean_square(o) + EPS) * g_ref[...] * post_scale
    o_ref[...] = o


def _cache_block_spec(layer):
    return pl.BlockSpec((1, 1, HEADS, PAST_LEN, HEAD_DIM), lambda b, i: (b, layer, 0, 0, 0))


def _cache_heads_on_lanes(ref):
    return jnp.concatenate([ref[0, 0, h] for h in range(HEADS)], axis=1)


def attention(p, row_blk0, cols, n_seq, seq_len, lam, norm_g, *, n_maps, post_scale, cache=None, layer=0):
    nb = seq_len // TM
    kv_len = seq_len + (PAST_LEN if cache is not None else 0)
    kern = functools.partial(_attn_kernel, n_maps=n_maps, post_scale=post_scale, with_cache=cache is not None)
    kv_spec = lambda col: pl.BlockSpec((seq_len, MIX_BLK), lambda b, i: (row_blk0 + b, col))
    cache_specs = [_cache_block_spec(layer)] * 2 if cache is not None else []
    return Part(
        kernel=kern,
        in_specs=[
            pl.BlockSpec(memory_space=pltpu.SMEM),
            pl.BlockSpec((TM, MIX_BLK), lambda b, i: ((row_blk0 + b) * nb + i, cols[0])),
            kv_spec(cols[1]), kv_spec(cols[2]), *cache_specs,
            pl.BlockSpec((1, MIX_BLK), lambda b, i: (0, 0)),
        ],
        out_specs=[pl.BlockSpec((TM, MIX_BLK), lambda b, i: (b * nb + i, 0))],
        out_shapes=[jax.ShapeDtypeStruct((n_seq * seq_len, MIX_BLK), F32)],
        scratch=[pltpu.VMEM((MIX_BLK, kv_len), BF16), pltpu.VMEM((kv_len, MIX_BLK), BF16)],
        args=[lam, p, p, p, *(cache or ()), norm_g])


def _na_slab_start(i):
    return jnp.clip(i - 1, 0, GRID_H // 4 - NA_SLAB_ROWS // 4)


def _na_kernel(q_ref, k_ref, v_ref, kc_ref, vc_ref, bias_ref, o_ref):
    i = pl.program_id(1)
    start = pl.multiple_of(_na_slab_start(i) * TM, TM)
    ks_t = k_ref[pl.ds(start, NA_SLAB), :].astype(F32).T.astype(BF16)
    vs = v_ref[pl.ds(start, NA_SLAB), :]
    kc_t = _cache_heads_on_lanes(kc_ref).T.astype(BF16)
    vc = _cache_heads_on_lanes(vc_ref).astype(BF16)
    q = q_ref[...].astype(F32) * (HEAD_DIM ** -0.5)
    lane = _head_lanes()
    o = jnp.zeros(q.shape, F32)
    for h in range(HEADS):
        in_head = _lane_range(lane, h * HEAD_DIM, HEAD_DIM)
        qm = jnp.where(in_head, q, 0.0).astype(BF16)
        s_loc = jnp.dot(qm, ks_t, preferred_element_type=F32) + bias_ref[0, h]
        s_ctx = jnp.dot(qm, kc_t, preferred_element_type=F32)
        m = jnp.maximum(jnp.max(s_loc, axis=-1, keepdims=True), jnp.max(s_ctx, axis=-1, keepdims=True))
        e_loc = jnp.exp(s_loc - m)
        e_ctx = jnp.exp(s_ctx - m)
        den = jnp.sum(e_loc, axis=-1, keepdims=True) + jnp.sum(e_ctx, axis=-1, keepdims=True)
        oh = (jnp.dot(e_loc.astype(BF16), vs, preferred_element_type=F32)
              + jnp.dot(e_ctx.astype(BF16), vc, preferred_element_type=F32)) / den
        o = jnp.where(in_head, oh, o)
    o_ref[...] = o


def _na_bias_tables(rpb):
    n_dr, n_dc = 2 * NA_WIN_H - 1, 2 * NA_WIN_W - 1
    cq = np.arange(GRID_W)[:, None]
    ck = np.arange(GRID_W)[None, :]
    wc0 = np.clip(cq - NA_WIN_W // 2, 0, GRID_W - NA_WIN_W)
    col_ok = (ck >= wc0) & (ck < wc0 + NA_WIN_W)
    col_pick = np.clip(ck - cq + NA_WIN_W - 1, 0, n_dc - 1)[..., None] == np.arange(n_dc)
    by_col = jnp.einsum("hab,qcb->haqc", rpb.astype(F32), jnp.asarray(col_pick, F32), precision=HIGHEST)
    margin = 4
    by_col = jnp.pad(by_col.transpose(0, 2, 1, 3), ((0, 0), (0, 0), (margin, margin), (0, 0)))
    by_col = by_col.reshape(HEADS, GRID_W, (n_dr + 2 * margin) * GRID_W)
    pieces, row_ok = [], []
    for tile in (0, 1, GRID_H // 4 - 1):
        slab0 = int(np.clip(tile - 1, 0, GRID_H // 4 - NA_SLAB_ROWS // 4)) * 4
        rq = tile * 4 + np.arange(4)
        rk = (slab0 + np.arange(NA_SLAB) // GRID_W)[None, :]
        wr0 = np.clip(rq - NA_WIN_H // 2, 0, GRID_H - NA_WIN_H)[:, None]
        row_ok.append((rk >= wr0) & (rk < wr0 + NA_WIN_H))
        for r in rq:
            first = slab0 - int(r) + NA_WIN_H - 1 + margin
            assert 0 <= first and first + NA_SLAB_ROWS <= n_dr + 2 * margin
            pieces.append(by_col[:, :, first * GRID_W:first * GRID_W + NA_SLAB])
    table = jnp.stack(pieces).reshape(3, 4, HEADS, GRID_W, NA_SLAB).transpose(0, 2, 1, 3, 4)
    valid = np.stack(row_ok)[:, None, :, None, :] & np.tile(col_ok, (1, NA_SLAB_ROWS))[None, None, None]
    table = jnp.where(jnp.asarray(valid), table, NEG_BIG)
    return table.reshape(3, HEADS, TM, NA_SLAB)


def na_latent(p, kc, vc, layer, bias):
    n_t = LAT_TILES_PER_SEQ
    seq_blk0 = TP // DEC_SEQ

    def bias_idx(b, i):
        return (jnp.minimum(i, 1) + i // (n_t - 1), 0, 0, 0)

    return Part(
        kernel=_na_kernel,
        in_specs=[
            pl.BlockSpec((TM, MIX_BLK), lambda b, i: (CTX_TILES + b * n_t + i, M_CQ)),
            pl.BlockSpec((DEC_SEQ, MIX_BLK), lambda b, i: (seq_blk0 + b, M_CK)),
            pl.BlockSpec((DEC_SEQ, MIX_BLK), lambda b, i: (seq_blk0 + b, M_CV)),
            _cache_block_spec(layer), _cache_block_spec(layer),
            pl.BlockSpec((1, HEADS, TM, NA_SLAB), bias_idx),
        ],
        out_specs=[pl.BlockSpec((TM, MIX_BLK), lambda b, i: (b * n_t + i, 0))],
        out_shapes=[jax.ShapeDtypeStruct((TL, MIX_BLK), F32)],
        scratch=[],
        args=[p, p, p, kc, vc, bias])


MAX_EXPONENT = 80.0


def _hgrn_direction(q_ref, f_ref, v_ref, lb, st_ref, o_ref, reverse):
    n_ch = TM // B_CHUNK
    r_idx = lax.broadcasted_iota(jnp.int32, (TM, TM), 0)
    c_idx = lax.broadcasted_iota(jnp.int32, (TM, TM), 1)
    tri = (c_idx >= r_idx) if reverse else (c_idx <= r_idx)
    zq = q_ref[...]
    q = zq * jax.nn.sigmoid(zq)
    z = f_ref[...]
    gate = (1.0 - lb) * jax.nn.sigmoid(z)
    logf = jnp.log(lb + gate)
    kk = (1.0 - lb) - gate
    b = _select_sum_left(jnp.where(tri, 1.0, 0.0).astype(BF16), logf)
    b3 = b.reshape(n_ch, B_CHUNK, MIX_BLK)
    mid = B_CHUNK // 2 if reverse else B_CHUNK // 2 - 1
    q_in = (q.reshape(b3.shape) * jnp.exp(b3 - b3[:, mid:mid + 1, :])).reshape(TM, MIX_BLK)
    q_dec = (q * jnp.exp(b)).astype(BF16)
    b_t = b.T
    kk_t = kk.T
    far = 0 if reverse else TM - 1
    b_far = b_t[:, far:far + 1]
    k_dec_t = (kk_t * jnp.exp(b_far - b_t)).astype(BF16)
    vb = v_ref[...].astype(BF16)
    st = st_ref[...]
    o_state = jnp.dot(q_dec, st.astype(BF16), preferred_element_type=F32)
    kv = jnp.dot(k_dec_t, vb, preferred_element_type=F32)
    st_ref[...] = st * jnp.exp(b_far) + jnp.where(_same_head_matrix(), kv, 0.0)
    lane = _head_lanes()
    token = lax.broadcasted_iota(jnp.int32, (1, TM), 1)
    local = lax.broadcasted_iota(jnp.int32, (HEADS * B_CHUNK, 1), 0) % B_CHUNK
    heads = [_lane_range(lane, h * HEAD_DIM, HEAD_DIM) for h in range(HEADS)]
    for c in range(n_ch):
        rows = slice(c * B_CHUNK, (c + 1) * B_CHUNK)
        ref = b_t[:, c * B_CHUNK + mid:c * B_CHUNK + mid + 1]
        k_c_t = (kk_t * jnp.exp(jnp.minimum(ref - b_t, MAX_EXPONENT))).astype(BF16)
        q_c = q_in[rows, :]
        lhs = jnp.concatenate([jnp.where(in_head, q_c, 0.0) for in_head in heads], axis=0)
        a = jnp.dot(lhs.astype(BF16), k_c_t, preferred_element_type=F32)
        t_abs = c * B_CHUNK + local
        a = jnp.where((token >= t_abs) if reverse else (token <= t_abs), a, 0.0)
        res = jnp.dot(a.astype(BF16), vb, preferred_element_type=F32)
        o_c = o_state[rows, :]
        for h, in_head in enumerate(heads):
            o_c = o_c + jnp.where(in_head, res[h * B_CHUNK:(h + 1) * B_CHUNK, :], 0.0)
        o_ref[rows, :] = o_c


def _hgrn_kernel(qf_ref, ff_ref, vf_ref, qb_ref, fb_ref, vb_ref, lb_ref, s0_ref,
                 of_ref, ob_ref, s_ref, stf_ref, stb_ref, *, has_s0):
    j = pl.program_id(1)

    @pl.when(j == 0)
    def _():
        if has_s0:
            stf_ref[...] = s0_ref[0, 0]
            stb_ref[...] = s0_ref[0, 1]
        else:
            stf_ref[...] = jnp.zeros((MIX_BLK, MIX_BLK), F32)
            stb_ref[...] = jnp.zeros((MIX_BLK, MIX_BLK), F32)

    lb = lb_ref[...]
    _hgrn_direction(qf_ref, ff_ref, vf_ref, lb[0:1], stf_ref, of_ref, False)
    _hgrn_direction(qb_ref, fb_ref, vb_ref, lb[1:2], stb_ref, ob_ref, True)

    @pl.when(j == pl.num_programs(1) - 1)
    def _():
        for d, st_ref in enumerate((stf_ref, stb_ref)):
            s = st_ref[...]
            for hd in range(HEADS):
                lo = hd * HEAD_DIM
                s_ref[0, d, hd] = s[lo:lo + HEAD_DIM, lo:lo + HEAD_DIM]


def hgrn(p, row_tile0, n_seq, seq_len, lb, s0):
    nb = seq_len // TM
    has_s0 = s0 is not None
    if s0 is None:
        s0 = jnp.zeros((1, 2, MIX_BLK, MIX_BLK), F32)

    def fwd(col):
        return pl.BlockSpec((TM, MIX_BLK), lambda s, j: (row_tile0 + s * nb + j, col))

    def bwd(col):
        return pl.BlockSpec((TM, MIX_BLK), lambda s, j: (row_tile0 + s * nb + nb - 1 - j, col))

    state_spec = pl.BlockSpec((1, 2, MIX_BLK, MIX_BLK), lambda s, j: (s if has_s0 else 0, 0, 0, 0))
    out_rows = n_seq * seq_len
    return Part(
        kernel=functools.partial(_hgrn_kernel, has_s0=has_s0),
        in_specs=[fwd(G_BQ), fwd(G_BFF), fwd(G_BV), bwd(G_BQ), bwd(G_BFB), bwd(G_BV),
                  pl.BlockSpec((2, MIX_BLK), lambda s, j: (0, 0)), state_spec],
        out_specs=[
            pl.BlockSpec((TM, MIX_BLK), lambda s, j: (s * nb + j, 0)),
            pl.BlockSpec((TM, MIX_BLK), lambda s, j: (s * nb + nb - 1 - j, 0)),
            pl.BlockSpec((1, 2, HEADS, HEAD_DIM, HEAD_DIM), lambda s, j: (s, 0, 0, 0, 0)),
        ],
        out_shapes=[jax.ShapeDtypeStruct((out_rows, MIX_BLK), F32),
                    jax.ShapeDtypeStruct((out_rows, MIX_BLK), F32),
                    jax.ShapeDtypeStruct((n_seq, 2, HEADS, HEAD_DIM, HEAD_DIM), F32)],
        scratch=[pltpu.VMEM((MIX_BLK, MIX_BLK), F32), pltpu.VMEM((MIX_BLK, MIX_BLK), F32)],
        args=[p, p, p, p, p, p, lb, s0])


def _state_to_blockdiag(s):
    eye = jnp.eye(HEADS, dtype=F32)
    full = s.astype(F32)[:, :, :, :, None, :] * eye[None, None, :, None, :, None]
    return full.reshape(s.shape[0], 2, MIX_BLK, MIX_BLK)


def _fft_kernel(u_ref, c64_ref, s64_ref, cl_ref, sl_ref, o_ref, a_ref, b_ref, *, norm):
    @pl.when(pl.program_id(1) == 0)
    def _():
        u = u_ref[...]
        a_ref[...] = jnp.dot(u, c64_ref[...], preferred_element_type=F32).astype(BF16)
        b_ref[...] = jnp.dot(u, s64_ref[...], preferred_element_type=F32).astype(BF16)

    o_ref[...] = (jnp.dot(cl_ref[...], a_ref[...], preferred_element_type=F32)
                  - jnp.dot(sl_ref[...], b_ref[...], preferred_element_type=F32)) * norm


def _dft_tables(n):
    k = np.arange(n)
    ang = 2.0 * np.pi * ((k[:, None] * k[None, :]) % n) / n
    return np.cos(ang), np.sin(ang)


def _dft_constants(seq_len):
    c64, s64 = _dft_tables(HEAD_DIM)
    eye = np.eye(HEADS)
    cl, sl = _dft_tables(seq_len)
    as_bf16 = lambda a: jnp.asarray(a, F32).astype(BF16)
    return as_bf16(np.kron(eye, c64)), as_bf16(np.kron(eye, s64)), as_bf16(cl), as_bf16(sl)


def fourier_mix(p, row_blk0, n_seq, seq_len, consts):
    c64, s64, cl, sl = consts
    nb = seq_len // TM
    norm = 1.0 / math.sqrt(seq_len * HEAD_DIM)
    return Part(
        kernel=functools.partial(_fft_kernel, norm=norm),
        in_specs=[
            pl.BlockSpec((seq_len, MIX_BLK), lambda s, i: (row_blk0 + s, M_DU)),
            pl.BlockSpec((MIX_BLK, MIX_BLK), lambda s, i: (0, 0)),
            pl.BlockSpec((MIX_BLK, MIX_BLK), lambda s, i: (0, 0)),
            pl.BlockSpec((TM, seq_len), lambda s, i: (i, 0)),
            pl.BlockSpec((TM, seq_len), lambda s, i: (i, 0)),
        ],
        out_specs=[pl.BlockSpec((TM, MIX_BLK), lambda s, i: (s * nb + i, 0))],
        out_shapes=[jax.ShapeDtypeStruct((n_seq * seq_len, MIX_BLK), F32)],
        scratch=[pltpu.VMEM((seq_len, MIX_BLK), BF16), pltpu.VMEM((seq_len, MIX_BLK), BF16)],
        args=[p, c64, s64, cl, sl])


def _route(logits_t, rb):
    per = N_EXPERTS // N_GROUPS
    score = [jax.nn.sigmoid(logits_t[e:e + 1, :]) for e in range(N_EXPERTS)]
    sel = [score[e] + rb[e:e + 1, :] for e in range(N_EXPERTS)]
    gscore = []
    for g in range(N_GROUPS):
        vals = sel[g * per:(g + 1) * per]
        best = None
        for a in range(per):
            for b in range(a + 1, per):
                pair = vals[a] + vals[b]
                best = pair if best is None else jnp.maximum(best, pair)
        gscore.append(best)
    chosen = []
    for g in range(N_GROUPS):
        ok = None
        for j in range(N_GROUPS):
            if j == g:
                continue
            cond = gscore[g] > gscore[j] if j < g else gscore[g] >= gscore[j]
            ok = cond if ok is None else ok & cond
        chosen.append(ok)
    picked = []
    for e in range(N_EXPERTS):
        g = e // per
        rank = jnp.zeros_like(sel[e])
        for j in range(g * per, (g + 1) * per):
            if j == e:
                continue
            ahead = sel[j] >= sel[e] if j < e else sel[j] > sel[e]
            rank = rank + jnp.where(ahead, 1.0, 0.0)
        picked.append(chosen[g] & (rank < 2.0))
    wsum = jnp.zeros_like(score[0])
    for e in range(N_EXPERTS):
        wsum = wsum + jnp.where(picked[e], score[e], 0.0)
    bucket = jnp.zeros_like(wsum)
    w_a = jnp.zeros_like(wsum)
    w_b = jnp.zeros_like(wsum)
    for g in range(N_GROUPS):
        for n, (a, b) in enumerate(EXPERT_PAIRS):
            hit = picked[g * per + a] & picked[g * per + b]
            bucket = jnp.where(hit, float(g * len(EXPERT_PAIRS) + n), bucket)
            w_a = jnp.where(hit, score[g * per + a] / wsum, w_a)
            w_b = jnp.where(hit, score[g * per + b] / wsum, w_b)
    return bucket, w_a, w_b


def _out_kernel(*refs):
    streams, rest = refs[:12], refs[12:]
    (bg_ref, mod_ref, hg_ref, w_ref, g2_ref, rw_ref, rb_ref,
     x1_ref, h2_ref, bucket_ref, rank_ref, counts_ref, run_ref) = rest
    is_ctx = pl.program_id(0) < TP // TM_OUT
    x, o_a, o_f, o_b, o_c, o_d = (jnp.where(is_ctx, streams[2 * n][...], streams[2 * n + 1][...])
                                  for n in range(6))

    @pl.when(pl.program_id(0) == 0)
    def _():
        run_ref[...] = jnp.zeros(run_ref.shape, F32)

    mod = mod_ref[0]
    hb = o_f + o_b
    zg = bg_ref[...]
    hb = hb * lax.rsqrt(_head_mean_square(hb) + EPS) * hg_ref[...] * (zg * jax.nn.sigmoid(zg))
    parts = (o_a, hb, o_c, o_d)
    mixed = jnp.zeros((TM_OUT, D_MODEL), F32)
    for n, part in enumerate(parts):
        mixed = mixed + jnp.dot(part.astype(BF16), w_ref[0, n * MIX_BLK:(n + 1) * MIX_BLK, :],
                                preferred_element_type=F32)
    x1 = x + mod[2:3] * mixed
    x1_ref[...] = x1
    ms = jnp.mean(x1 * x1, axis=-1, keepdims=True)
    h2 = x1 * lax.rsqrt(ms + EPS) * g2_ref[...] * (1.0 + mod[4:5]) + mod[3:4]
    rw = rw_ref[...]
    r = sum(jnp.dot(piece, rw, preferred_element_type=F32) for piece in _bf16_pieces(h2, 2))
    bucket, w_a, w_b = _route((r[:, :LANES] + r[:, LANES:]).T, rb_ref[...])
    h2_ref[:, :D_MODEL] = h2
    h2_ref[:, D_MODEL:] = jnp.concatenate([w_a, w_b, jnp.zeros((LANES - 2, TM_OUT), F32)], axis=0).T
    onehot = jnp.where(lax.broadcasted_iota(jnp.int32, (BUCKET_ROWS, 1), 0).astype(F32) == bucket, 1.0, 0.0)
    s_idx = lax.broadcasted_iota(jnp.int32, (TM_OUT, TM_OUT), 0)
    t_idx = lax.broadcasted_iota(jnp.int32, (TM_OUT, TM_OUT), 1)
    prefix = jnp.dot(onehot.astype(BF16), jnp.where(s_idx <= t_idx, 1.0, 0.0).astype(BF16),
                     preferred_element_type=F32)
    run = run_ref[...]
    rank = jnp.sum(onehot * (prefix - 1.0 + run[:, 0:1]), axis=0, keepdims=True)
    run = run + jnp.sum(onehot, axis=1, keepdims=True)
    run_ref[...] = run
    bucket_ref[...] = bucket.astype(jnp.int32)
    rank_ref[...] = rank.astype(jnp.int32)
    counts_ref[...] = run


def out_and_route(x_pair, x_is_combined, mixer_pairs, p, mod, layer, hgrn_g, w_out_bf16, norm2_g, router_pieces,
                  router_b):
    n_ctx = TP // TM_OUT
    ctx_tile = lambda i: (jnp.minimum(i, n_ctx - 1), 0)
    lat_tile = lambda i: (jnp.maximum(i - n_ctx, 0), 0)
    mod_row = lambda i: jnp.where(i < n_ctx, 0, 1 + (i - n_ctx) // (DEC_SEQ // TM_OUT))
    tile = lambda w: pl.BlockSpec((TM_OUT, w), lambda i: (i, 0))
    full = lambda r, c: pl.BlockSpec((r, c), lambda i: (0, 0))
    stream_specs = [pl.BlockSpec((TM_OUT, D_MODEL), ctx_tile),
                    pl.BlockSpec((TM_OUT, D_MODEL),
                                 (lambda i: (jnp.maximum(i, n_ctx), 0)) if x_is_combined else lat_tile)]
    stream_args = list(x_pair)
    for o_ctx, o_lat in mixer_pairs:
        stream_specs += [pl.BlockSpec((TM_OUT, MIX_BLK), ctx_tile), pl.BlockSpec((TM_OUT, MIX_BLK), lat_tile)]
        stream_args += [o_ctx, o_lat]
    return pl.pallas_call(
        _out_kernel,
        grid=(T // TM_OUT,),
        in_specs=stream_specs + [
            pl.BlockSpec((TM_OUT, MIX_BLK), lambda i: (i, G_BG)),
            pl.BlockSpec((1, 6, D_MODEL), lambda i: (mod_row(i), 0, 0)),
            full(1, MIX_BLK), pl.BlockSpec((1, D_MODEL, D_MODEL), lambda i: (layer, 0, 0)), full(1, D_MODEL),
            full(D_MODEL, 2 * LANES), full(N_EXPERTS, 1),
        ],
        out_specs=[tile(D_MODEL), tile(ROW_W), pl.BlockSpec((1, TM_OUT), lambda i: (0, i)),
                   pl.BlockSpec((1, TM_OUT), lambda i: (0, i)), full(BUCKET_ROWS, LANES)],
        out_shape=[jax.ShapeDtypeStruct((T, D_MODEL), F32),
                   jax.ShapeDtypeStruct((T, ROW_W), F32),
                   jax.ShapeDtypeStruct((1, T), jnp.int32),
                   jax.ShapeDtypeStruct((1, T), jnp.int32),
                   jax.ShapeDtypeStruct((BUCKET_ROWS, LANES), F32)],
        scratch_shapes=[pltpu.VMEM((BUCKET_ROWS, LANES), F32)],
        compiler_params=_cparams(("arbitrary",)),
        name="out_and_route",
    )(*stream_args, p, mod, jnp.tile(hgrn_g, HEADS)[None, :], w_out_bf16,
      norm2_g[None, :], router_pieces, router_b[:, None])


def _router_pieces(router_w):
    hi, lo = _bf16_pieces(router_w.astype(F32), 2)
    pad = lambda a: jnp.pad(a, ((0, 0), (0, LANES - N_EXPERTS)))
    return jnp.concatenate([pad(hi), pad(lo)], axis=1)


def routing_plan(bucket, rank, counts):
    counts = counts[:N_BUCKETS, 0].astype(jnp.int32)
    n_tiles = (counts + TM_MOE - 1) // TM_MOE
    tile_end = jnp.cumsum(n_tiles)
    tile_start = tile_end - n_tiles
    buckets = jnp.arange(N_BUCKETS, dtype=jnp.int32)
    start_of_token = jnp.sum(jnp.where(bucket[0][:, None] == buckets[None, :], tile_start[None, :], 0), axis=1)
    dest = start_of_token * TM_MOE + rank[0]
    tiles = jnp.arange(MAX_TILES, dtype=jnp.int32)
    valid = tiles < tile_end[-1]
    tile_bucket = jnp.sum((jnp.minimum(tiles, tile_end[-1] - 1)[:, None] >= tile_end[None, :]).astype(jnp.int32), axis=1)
    pair_a = np.array([a for a, _ in EXPERT_PAIRS], np.int32)
    pair_b = np.array([b for _, b in EXPERT_PAIRS], np.int32)
    per = N_EXPERTS // N_GROUPS
    exp_a = jnp.asarray((np.arange(N_BUCKETS) // len(EXPERT_PAIRS)) * per + np.tile(pair_a, N_GROUPS), jnp.int32)
    exp_b = jnp.asarray((np.arange(N_BUCKETS) // len(EXPERT_PAIRS)) * per + np.tile(pair_b, N_GROUPS), jnp.int32)
    pick = tile_bucket[:, None] == buckets[None, :]
    tile_a = jnp.sum(jnp.where(pick, exp_a[None, :], 0), axis=1)
    tile_b = jnp.sum(jnp.where(pick, exp_b[None, :], 0), axis=1)
    return dest.astype(jnp.int32), tile_a, tile_b, valid.astype(jnp.int32), (tile_end[-1:] - 1).astype(jnp.int32)


def _row_copy(src, src_row, dst, dst_row, sem):
    return pltpu.make_async_copy(src.at[pl.ds(src_row, 1), :], dst.at[pl.ds(dst_row, 1), :], sem)


def _scatter_kernel(dest_ref, h_ref, init_ref, o_ref, sem):
    del init_ref
    base = pl.program_id(0) * TM

    for r in range(TM):
        _row_copy(h_ref, r, o_ref, dest_ref[base + r], sem).start(priority=r % N_DMA_PRIORITIES)
    pltpu.make_async_copy(h_ref, o_ref.at[pl.ds(0, TM), :], sem).wait()


def scatter_to_slots(h2, dest, slots):
    return pl.pallas_call(
        _scatter_kernel,
        grid_spec=pltpu.PrefetchScalarGridSpec(
            num_scalar_prefetch=1,
            grid=(N_TILES,),
            in_specs=[pl.BlockSpec((TM, ROW_W), lambda i, d: (i, 0)),
                      pl.BlockSpec(memory_space=pl.ANY)],
            out_specs=pl.BlockSpec(memory_space=pl.ANY),
            scratch_shapes=[pltpu.SemaphoreType.DMA(())],
        ),
        out_shape=jax.ShapeDtypeStruct((N_SLOTS, ROW_W), F32),
        input_output_aliases={2: 0},
        compiler_params=_cparams(("arbitrary",)),
        name="scatter_to_slots",
    )(dest, h2, slots)


def _moe_kernel(ta_ref, tb_ref, valid_ref, last_ref, h_ref, wga_ref, wua_ref, wda_ref, wgb_ref, wub_ref, wdb_ref,
                o_ref):
    del ta_ref, tb_ref, last_ref
    i = pl.program_id(0)

    @pl.when(valid_ref[i] == 1)
    def _():
        x = h_ref[:, :D_MODEL].astype(BF16)
        gates = h_ref[:, D_MODEL:]
        y = jnp.zeros((TM_MOE, D_MODEL), F32)
        for n, (wg, wu, wd) in enumerate(((wga_ref, wua_ref, wda_ref), (wgb_ref, wub_ref, wdb_ref))):
            a = jnp.dot(x, wg[0, 0].astype(BF16), preferred_element_type=F32)
            u = jnp.dot(x, wu[0, 0].astype(BF16), preferred_element_type=F32)
            z = a * jax.nn.sigmoid(a) * u * gates[:, n:n + 1]
            y = y + jnp.dot(z.astype(BF16), wd[0, 0].astype(BF16), preferred_element_type=F32)
        o_ref[:, 0, :] = y

    @pl.when(valid_ref[i] == 0)
    def _():
        o_ref[...] = jnp.zeros((TM_MOE, 1, D_MODEL), F32)


def moe(h_slots, tile_a, tile_b, valid, last, layer, wg, wu, wd):
    up_a = pl.BlockSpec((1, 1, D_MODEL, D_EXPERT), lambda i, ta, tb, v, last: (layer, ta[i], 0, 0))
    up_b = pl.BlockSpec((1, 1, D_MODEL, D_EXPERT), lambda i, ta, tb, v, last: (layer, tb[i], 0, 0))
    down_a = pl.BlockSpec((1, 1, D_EXPERT, D_MODEL), lambda i, ta, tb, v, last: (layer, ta[i], 0, 0))
    down_b = pl.BlockSpec((1, 1, D_EXPERT, D_MODEL), lambda i, ta, tb, v, last: (layer, tb[i], 0, 0))
    return pl.pallas_call(
        _moe_kernel,
        grid_spec=pltpu.PrefetchScalarGridSpec(
            num_scalar_prefetch=4,
            grid=(MAX_TILES,),
            in_specs=[pl.BlockSpec((TM_MOE, ROW_W), lambda i, ta, tb, v, last: (jnp.minimum(i, last[0]), 0)),
                      up_a, up_a, down_a, up_b, up_b, down_b],
            out_specs=pl.BlockSpec((TM_MOE, 1, D_MODEL), lambda i, ta, tb, v, last: (i, 0, 0)),
        ),
        out_shape=jax.ShapeDtypeStruct((N_SLOTS, 1, D_MODEL), F32),
        compiler_params=_cparams(("arbitrary",)),
        name="moe",
    )(tile_a, tile_b, valid, last, h_slots, wg, wu, wd, wg, wu, wd)


def _gather_tile(dest_ref, y_ref, buf_ref, sem, tile, slot):
    for r in range(TM):
        pltpu.make_async_copy(y_ref.at[pl.ds(dest_ref[tile * TM + r], 1)],
                              buf_ref.at[slot, pl.ds(r, 1)], sem.at[slot]).start(priority=r % N_DMA_PRIORITIES)


def _moe_residual(dest_ref, y_ref, buf_ref, sem, x1_ref, mod_ref):
    i = pl.program_id(0)
    slot = i % 2

    @pl.when(i == 0)
    def _():
        _gather_tile(dest_ref, y_ref, buf_ref, sem, 0, 0)

    @pl.when(i + 1 < pl.num_programs(0))
    def _():
        _gather_tile(dest_ref, y_ref, buf_ref, sem, i + 1, 1 - slot)

    pltpu.make_async_copy(y_ref.at[pl.ds(0, TM)], buf_ref.at[slot], sem.at[slot]).wait()
    return x1_ref[...] + mod_ref[0][5:6] * buf_ref[slot, :, 0, :]


def _final_kernel(dest_ref, y_ref, x1_ref, mod_ref, g_ref, oc_ref, ol_ref, buf_ref, sem):
    x2 = _moe_residual(dest_ref, y_ref, buf_ref, sem, x1_ref, mod_ref)
    ms = jnp.mean(x2 * x2, axis=-1, keepdims=True)
    y = x2 * lax.rsqrt(ms + EPS) * g_ref[...]

    @pl.when(_is_ctx_tile())
    def _():
        oc_ref[...] = y

    @pl.when(jnp.logical_not(_is_ctx_tile()))
    def _():
        ol_ref[...] = y


_GATHER_SCRATCH = [pltpu.VMEM((2, TM, 1, D_MODEL), F32), pltpu.SemaphoreType.DMA((2,))]


def final_norm(dest, y_slots, x1, mod, final_g):
    return pl.pallas_call(
        _final_kernel,
        grid_spec=pltpu.PrefetchScalarGridSpec(
            num_scalar_prefetch=1,
            grid=(N_TILES,),
            in_specs=[pl.BlockSpec(memory_space=pl.ANY),
                      pl.BlockSpec((TM, D_MODEL), lambda i, d: (i, 0)),
                      pl.BlockSpec((1, 6, D_MODEL), lambda i, d: (_mod_row(i), 0, 0)),
                      pl.BlockSpec((1, D_MODEL), lambda i, d: (0, 0))],
            out_specs=[pl.BlockSpec((TM, D_MODEL), _ctx_tile), pl.BlockSpec((TM, D_MODEL), _lat_tile)],
            scratch_shapes=_GATHER_SCRATCH,
        ),
        out_shape=[jax.ShapeDtypeStruct((TP, D_MODEL), F32), jax.ShapeDtypeStruct((TL, D_MODEL), F32)],
        compiler_params=_cparams(("arbitrary",)),
        name="final_norm",
    )(dest, y_slots, x1, mod, final_g[None, :])


def kernel(x_prompt, x_sample, cache_diff_k, cache_diff_v, cache_na_k, cache_na_v, state_hgrn, c, c_ctx,
           norm1_g, norm2_g, ada_w, ada_b, w_in, w_out, diff_lambda, diff_subln_g, hgrn_lb_logits,
           hgrn_norm_g, na_rpb, router_w, router_b, moe_w_gate, moe_w_up, moe_w_down, final_norm_g):
    assert SEQ == TM and PAST_LEN == TM and DEC_SEQ % TM_OUT == 0 and TP % DEC_SEQ == 0
    x_pair = (x_prompt.reshape(TP, D_MODEL), x_sample.reshape(TL, D_MODEL))
    w_in_bf16 = jnp.concatenate([w_in[:, :, c * MIX_BLK:(c + 1) * MIX_BLK] for c in PM_BLOCKS + PG_BLOCKS],
                                axis=-1).astype(BF16)
    w_out_bf16 = w_out.astype(BF16)
    router_pieces = _router_pieces(router_w)
    mods = modulation(jnp.concatenate([c_ctx[None, :], c], axis=0), ada_w, ada_b)
    mods = mods.reshape(DEPTH, 3, 6, D_MODEL)
    lb_sm = jax.nn.softmax(hgrn_lb_logits.astype(F32), axis=0)
    lb_all = jnp.cumsum(lb_sm, axis=0) - lb_sm[0:1]
    rope = _rope_tables()
    dft_ctx = _dft_constants(SEQ)
    dft_lat = _dft_constants(DEC_SEQ)
    lat_blk0 = TP // DEC_SEQ
    states = []
    moe_state = None
    for l in range(DEPTH):
        if moe_state is None:
            pm, pg, *new_kv = projection(*x_pair, mods[l], norm1_g[l], w_in_bf16, rope)
        else:
            x, pm, pg, *new_kv = projection_after_moe(*moe_state, mods[l - 1], mods[l], l, norm1_g[l], w_in_bf16, rope,
                                                 new_kv)
            x_pair = (x, x)

        lq = diff_lambda[l].astype(F32)
        lam_init = 0.8 - 0.6 * math.exp(-0.3 * l)
        lam = (jnp.exp(jnp.sum(lq[0] * lq[1])) - jnp.exp(jnp.sum(lq[2] * lq[3])) + lam_init).reshape(1)
        subln = jnp.tile(diff_subln_g[l], HEADS)[None, :]
        diff = functools.partial(attention, pm, cols=(M_AQ, M_AK, M_AV), lam=lam, norm_g=subln, n_maps=2,
                                 post_scale=1.0 - lam_init)
        (oa_ctx,), (of_ctx, ob_ctx, st_ctx), (oc_ctx,), (od_ctx,) = run_parts(
            [diff(row_blk0=0, n_seq=BATCH, seq_len=SEQ),
             hgrn(pg, 0, BATCH, SEQ, lb_all[l], None),
             attention(pm, 0, (M_CQ, M_CK, M_CV), BATCH, SEQ, lam, subln, n_maps=1, post_scale=1.0),
             fourier_mix(pm, 0, BATCH, SEQ, dft_ctx)],
            (BATCH, SEQ // TM), "context_mixers")
        (oa_lat,), (of_lat, ob_lat, _), (oc_lat,), (od_lat,) = run_parts(
            [diff(row_blk0=lat_blk0, n_seq=DEC_BATCH, seq_len=DEC_SEQ, cache=(cache_diff_k, cache_diff_v), layer=l),
             hgrn(pg, CTX_TILES, DEC_BATCH, DEC_SEQ, lb_all[l], _state_to_blockdiag(state_hgrn[:, l])),
             na_latent(pm, cache_na_k, cache_na_v, l, _na_bias_tables(na_rpb[l])),
             fourier_mix(pm, lat_blk0, DEC_BATCH, DEC_SEQ, dft_lat)],
            (DEC_BATCH, DEC_SEQ // TM), "latent_mixers")

        x1, h2, bucket, rank, counts = out_and_route(
            x_pair, l > 0, ((oa_ctx, oa_lat), (of_ctx, of_lat), (ob_ctx, ob_lat), (oc_ctx, oc_lat),
                            (od_ctx, od_lat)),
            pg, mods[l], l, hgrn_norm_g[l], w_out_bf16, norm2_g[l], router_pieces, router_b)
        dest, *tile_plan = routing_plan(bucket, rank, counts)
        h_slots = scatter_to_slots(h2, dest, jnp.zeros((N_SLOTS, ROW_W), F32) if l == 0 else h_slots)
        y_slots = moe(h_slots, *tile_plan, l, moe_w_gate, moe_w_up, moe_w_down)
        moe_state = (dest, y_slots, x1)

        states.append(st_ctx)
    y_prompt, y_sample = final_norm(*moe_state, mods[DEPTH - 1], final_norm_g)
    return (y_prompt.reshape(BATCH, SEQ, D_MODEL), y_sample.reshape(DEC_BATCH, DEC_SEQ, D_MODEL),
            *new_kv, jnp.stack(states, axis=1))
```

```python
import functools
import math
from typing import Any, NamedTuple

import numpy as np
import jax
import jax.numpy as jnp
from jax import lax
from jax.experimental import pallas as pl
from jax.experimental.pallas import tpu as pltpu

F32 = jnp.float32
BF16 = jnp.bfloat16
HIGHEST = lax.Precision.HIGHEST

D_MODEL = 1024
BATCH = 16
SEQ = 256
DEPTH = 2
DEC_BATCH = 2
DEC_SEQ = 2048
PAST_LEN = 256
GRID_W = 64
GRID_H = DEC_SEQ // GRID_W
EPS = 1e-6
NEG_BIG = -1e30
HEADS = 4
HEAD_DIM = 64
MIX_BLK = HEADS * HEAD_DIM
A_DIM = 32
ROPE_BASE = 10000.0
B_CHUNK = 32
NA_WIN_H = 8
NA_WIN_W = 16
N_EXPERTS = 16
N_GROUPS = 4
D_EXPERT = 512
PROJ_W = 12 * MIX_BLK
TP = BATCH * SEQ
TL = DEC_BATCH * DEC_SEQ
T = TP + TL
TM = 256
N_TILES = T // TM
CTX_TILES = TP // TM
LAT_TILES_PER_SEQ = DEC_SEQ // TM
(C_AQ, C_AK, C_AV, C_BQ, C_BFF, C_BFB, C_BV, C_BG, C_CQ, C_CK, C_CV, C_DU) = range(12)
PM_BLOCKS = (C_AQ, C_AK, C_AV, C_CQ, C_CK, C_CV, C_DU)
PG_BLOCKS = (C_BQ, C_BFF, C_BFB, C_BV, C_BG)
(M_AQ, M_AK, M_AV, M_CQ, M_CK, M_CV, M_DU) = range(len(PM_BLOCKS))
(G_BQ, G_BFF, G_BFB, G_BV, G_BG) = range(len(PG_BLOCKS))
PM_W = len(PM_BLOCKS) * MIX_BLK
PG_W = len(PG_BLOCKS) * MIX_BLK
NA_SLAB_ROWS = 12
NA_SLAB = NA_SLAB_ROWS * GRID_W
LANES = 128
ROW_W = D_MODEL + LANES
EXPERT_PAIRS = ((0, 1), (0, 2), (0, 3), (1, 3), (2, 3), (2, 1))
N_BUCKETS = N_GROUPS * len(EXPERT_PAIRS)
BUCKET_ROWS = 32
N_DMA_PRIORITIES = 2
TM_OUT = 512
TM_MOE = 384
MAX_TILES = T // TM_MOE + N_BUCKETS
N_SLOTS = MAX_TILES * TM_MOE
VMEM_LIMIT = 56 * 1024 * 1024


def _cparams(sem):
    return pltpu.CompilerParams(dimension_semantics=sem, vmem_limit_bytes=VMEM_LIMIT)


class Part(NamedTuple):
    kernel: Any
    in_specs: list
    args: list
    out_specs: list
    out_shapes: list
    scratch: list


def _run_parts_kernel(*refs, layout):
    n_in = sum(n for _, n, _, _ in layout)
    n_out = sum(n for _, _, n, _ in layout)
    ins, outs, scratch = refs[:n_in], refs[n_in:n_in + n_out], refs[n_in + n_out:]
    i = o = s = 0
    for kernel, k_in, k_out, k_scratch in layout:
        kernel(*ins[i:i + k_in], *outs[o:o + k_out], *scratch[s:s + k_scratch])
        i, o, s = i + k_in, o + k_out, s + k_scratch


def run_parts(parts, grid, name):
    layout = tuple((p.kernel, len(p.in_specs), len(p.out_specs), len(p.scratch)) for p in parts)
    outs = pl.pallas_call(
        functools.partial(_run_parts_kernel, layout=layout),
        grid=grid,
        in_specs=[s for p in parts for s in p.in_specs],
        out_specs=[s for p in parts for s in p.out_specs],
        out_shape=[s for p in parts for s in p.out_shapes],
        scratch_shapes=[s for p in parts for s in p.scratch],
        compiler_params=_cparams(("arbitrary", "arbitrary")),
        name=name,
    )(*[a for p in parts for a in p.args])
    result, o = [], 0
    for p in parts:
        result.append(outs[o:o + len(p.out_specs)])
        o += len(p.out_specs)
    return result


def _head_lanes(width=MIX_BLK):
    return lax.broadcasted_iota(jnp.int32, (1, width), 1)


def _lane_range(lane, lo, n):
    return (lane >= lo) & (lane < lo + n)


def _same_head_matrix():
    r = lax.broadcasted_iota(jnp.int32, (MIX_BLK, MIX_BLK), 0) // HEAD_DIM
    c = lax.broadcasted_iota(jnp.int32, (MIX_BLK, MIX_BLK), 1) // HEAD_DIM
    return r == c


def _bf16_pieces(x, n):
    pieces = []
    for _ in range(n):
        piece = x.astype(BF16)
        pieces.append(piece)
        x = x - piece.astype(F32)
    return pieces


def _select_sum_left(onehot_bf16, x):
    return sum(jnp.dot(onehot_bf16, piece, preferred_element_type=F32) for piece in _bf16_pieces(x, 3))


def _select_sum_right(x, onehot_bf16):
    return sum(jnp.dot(piece, onehot_bf16, preferred_element_type=F32) for piece in _bf16_pieces(x, 3))


def _head_mean_square(o):
    ones = jnp.where(_same_head_matrix(), 1.0, 0.0).astype(BF16)
    return _select_sum_right(o * o, ones) * (1.0 / HEAD_DIM)


def _mod_row(i):
    return jnp.where(i < CTX_TILES, 0, 1 + (i - CTX_TILES) // LAT_TILES_PER_SEQ)


def _mod_kernel(c_ref, w_ref, b_ref, o_ref):
    w = w_ref[0]
    for r in range(3):
        c = c_ref[r]
        s = c * jax.nn.sigmoid(c)
        o_ref[0, r:r + 1, :] = jnp.sum(s * w, axis=0, keepdims=True) + b_ref[0]


def modulation(c_rows, ada_w, ada_b):
    nt = 768
    n_out = 6 * D_MODEL
    return pl.pallas_call(
        _mod_kernel,
        grid=(DEPTH, n_out // nt),
        in_specs=[
            pl.BlockSpec((3, D_MODEL, 1), lambda l, j: (0, 0, 0)),
            pl.BlockSpec((1, D_MODEL, nt), lambda l, j: (l, 0, j)),
            pl.BlockSpec((1, 1, nt), lambda l, j: (l, 0, j)),
        ],
        out_specs=pl.BlockSpec((1, 3, nt), lambda l, j: (l, 0, j)),
        out_shape=jax.ShapeDtypeStruct((DEPTH, 3, n_out), F32),
        compiler_params=_cparams(("arbitrary", "arbitrary")),
        name="modulation",
    )(c_rows[:, :, None], ada_w, ada_b[:, None, :])


def _is_ctx_tile():
    return pl.program_id(0) < CTX_TILES


def _ctx_tile(i, *_):
    return (jnp.minimum(i, CTX_TILES - 1), 0)


def _lat_tile(i, *_):
    return (jnp.maximum(i - CTX_TILES, 0), 0)


def _proj_kernel(xc_ref, xl_ref, mod_ref, g_ref, w_ref, cos_ref, sa_ref, sb_ref, pm_ref, pg_ref, *cache_refs):
    x = jnp.where(_is_ctx_tile(), xc_ref[...], xl_ref[...])
    _proj_body(x, mod_ref, g_ref, w_ref, cos_ref, sa_ref, sb_ref, pm_ref, pg_ref, cache_refs)


def _proj_after_moe_kernel(dest_ref, y_ref, x1_ref, modp_ref, mod_ref, g_ref, w_ref, cos_ref, sa_ref, sb_ref,
                           *rest):
    x2_ref, pm_ref, pg_ref = rest[4:7]
    cache_refs, (buf_ref, sem) = rest[7:11], rest[11:]
    x2 = _moe_residual(dest_ref, y_ref, buf_ref, sem, x1_ref, modp_ref)
    x2_ref[...] = x2
    _proj_body(x2, mod_ref, g_ref, w_ref, cos_ref, sa_ref, sb_ref, pm_ref, pg_ref, cache_refs)


def _proj_body(x, mod_ref, g_ref, w_ref, cos_ref, sa_ref, sb_ref, pm_ref, pg_ref, cache_refs):
    ms = jnp.mean(x * x, axis=-1, keepdims=True)
    mod = mod_ref[0]
    h = x * lax.rsqrt(ms + EPS) * g_ref[...] * (1.0 + mod[1:2]) + mod[0:1]
    p = jnp.dot(h.astype(BF16), w_ref[0], preferred_element_type=F32)
    t = p[:, :2 * MIX_BLK]
    pm_ref[:, :2 * MIX_BLK] = (t * cos_ref[...] + pltpu.roll(t, 1, 1) * sa_ref[...]
                               + pltpu.roll(t, 2 * MIX_BLK - 1, 1) * sb_ref[...]).astype(BF16)
    pm_ref[:, 2 * MIX_BLK:] = p[:, 2 * MIX_BLK:PM_W].astype(BF16)
    pg_ref[...] = p[:, PM_W:]

    @pl.when(_is_ctx_tile())
    def _():
        for ref, col in zip(cache_refs, (M_AK, M_AV, M_CK, M_CV)):
            for hd in range(HEADS):
                lo = col * MIX_BLK + hd * HEAD_DIM
                ref[0, 0, hd] = p[:, lo:lo + HEAD_DIM]
            if ref.shape[1] > 1:
                ref[0, 1:] = jnp.zeros((ref.shape[1] - 1,) + tuple(ref.shape[2:]), F32)


_CACHE_SHAPE = jax.ShapeDtypeStruct((BATCH, DEPTH, HEADS, SEQ, HEAD_DIM), F32)


def _cache_spec(layer):
    n_layers = DEPTH if layer == 0 else 1
    return pl.BlockSpec((1, n_layers, HEADS, SEQ, HEAD_DIM),
                        lambda i, *_: (jnp.minimum(i, CTX_TILES - 1), layer, 0, 0, 0))


def _rope_tables():
    nf = A_DIM // 4
    freqs = ROPE_BASE ** (-np.arange(nf, dtype=np.float64) / nf)
    pos = np.arange(DEC_SEQ)
    row = (pos // GRID_W).astype(np.float64)
    col = (pos % GRID_W).astype(np.float64)
    ang = np.concatenate([row[:, None] * freqs, col[:, None] * freqs], axis=-1)
    cos = np.repeat(np.cos(ang), 2, axis=-1)
    sin = np.repeat(np.sin(ang), 2, axis=-1)
    odd = (np.arange(A_DIM) % 2 == 1)[None, :]
    sin_from_left = np.where(odd, sin, 0.0)
    sin_from_right = np.where(odd, 0.0, -sin)
    reps = 2 * MIX_BLK // A_DIM
    ident = (np.ones((TM, 2 * MIX_BLK)), np.zeros((TM, 2 * MIX_BLK)))
    return tuple(jnp.asarray(np.concatenate([np.tile(t, (1, reps)), tail], axis=0), F32)
                 for t, tail in ((cos, ident[0]), (sin_from_left, ident[1]), (sin_from_right, ident[1])))


def _rope_block(i):
    return (jnp.where(i < CTX_TILES, LAT_TILES_PER_SEQ, (i - CTX_TILES) % LAT_TILES_PER_SEQ), 0)


def projection_after_moe(dest, y_slots, x1, mod_prev, mod, layer, norm_g, w_in_bf16, rope, caches):
    rope_spec = pl.BlockSpec((TM, 2 * MIX_BLK), lambda i, d: _rope_block(i))
    mod_spec = pl.BlockSpec((1, 6, D_MODEL), lambda i, d: (_mod_row(i), 0, 0))
    n_in = 10
    return pl.pallas_call(
        _proj_after_moe_kernel,
        grid_spec=pltpu.PrefetchScalarGridSpec(
            num_scalar_prefetch=1,
            grid=(N_TILES,),
            in_specs=[pl.BlockSpec(memory_space=pl.ANY),
                      pl.BlockSpec((TM, D_MODEL), lambda i, d: (i, 0)),
                      mod_spec, mod_spec,
                      pl.BlockSpec((1, D_MODEL), lambda i, d: (0, 0)),
                      pl.BlockSpec((1, D_MODEL, PROJ_W), lambda i, d: (layer, 0, 0)),
                      rope_spec, rope_spec, rope_spec] + [pl.BlockSpec(memory_space=pl.ANY)] * 4,
            out_specs=[pl.BlockSpec((TM, D_MODEL), lambda i, d: (i, 0)),
                       pl.BlockSpec((TM, PM_W), lambda i, d: (i, 0)),
                       pl.BlockSpec((TM, PG_W), lambda i, d: (i, 0))] + [_cache_spec(layer)] * 4,
            scratch_shapes=_GATHER_SCRATCH,
        ),
        out_shape=[jax.ShapeDtypeStruct((T, D_MODEL), F32), jax.ShapeDtypeStruct((T, PM_W), BF16),
                   jax.ShapeDtypeStruct((T, PG_W), F32)] + [_CACHE_SHAPE] * 4,
        input_output_aliases={n_in + n: 3 + n for n in range(4)},
        compiler_params=_cparams(("arbitrary",)),
        name="projection_after_moe",
    )(dest, y_slots, x1, mod_prev, mod, norm_g[None, :], w_in_bf16, *rope, *caches)


def projection(x_ctx, x_lat, mod, norm_g, w_in_bf16, rope):
    layer = 0
    rope_spec = pl.BlockSpec((TM, 2 * MIX_BLK), _rope_block)
    return pl.pallas_call(
        _proj_kernel,
        grid=(N_TILES,),
        in_specs=[
            pl.BlockSpec((TM, D_MODEL), _ctx_tile),
            pl.BlockSpec((TM, D_MODEL), _lat_tile),
            pl.BlockSpec((1, 6, D_MODEL), lambda i: (_mod_row(i), 0, 0)),
            pl.BlockSpec((1, D_MODEL), lambda i: (0, 0)),
            pl.BlockSpec((1, D_MODEL, PROJ_W), lambda i: (layer, 0, 0)),
            rope_spec, rope_spec, rope_spec,
        ],
        out_specs=[pl.BlockSpec((TM, PM_W), lambda i: (i, 0)), pl.BlockSpec((TM, PG_W), lambda i: (i, 0))]
        + [_cache_spec(layer)] * 4,
        out_shape=[jax.ShapeDtypeStruct((T, PM_W), BF16), jax.ShapeDtypeStruct((T, PG_W), F32)]
        + [_CACHE_SHAPE] * 4,
        compiler_params=_cparams(("arbitrary",)),
        name="projection",
    )(x_ctx, x_lat, mod, norm_g[None, :], w_in_bf16, *rope)


LOG2_E = 1.4426950408889634


def _exp2_rows(s):
    e = jnp.exp2(s - jnp.max(s, axis=-1, keepdims=True))
    return e, 1.0 / jnp.sum(e, axis=-1, keepdims=True)


def _attn_kernel(lam_ref, q_ref, k_ref, v_ref, *rest, n_maps, post_scale, with_cache):
    if with_cache:
        kc_ref, vc_ref, g_ref, o_ref, kt_ref, vb_ref = rest
    else:
        g_ref, o_ref, kt_ref, vb_ref = rest

    @pl.when(pl.program_id(1) == 0)
    def _():
        k = k_ref[...].astype(F32)
        v = v_ref[...]
        if with_cache:
            k = jnp.concatenate([_cache_heads_on_lanes(kc_ref), k], axis=0)
            v = jnp.concatenate([_cache_heads_on_lanes(vc_ref).astype(BF16), v], axis=0)
        kt_ref[...] = k.T.astype(BF16)
        vb_ref[...] = v

    lane = _head_lanes()
    map_dim = HEAD_DIM // n_maps
    q = q_ref[...].astype(F32) * (map_dim ** -0.5 * LOG2_E)
    kt = kt_ref[...]
    vb = vb_ref[...]
    o = jnp.zeros(q.shape, F32)
    for h in range(HEADS):
        parts = []
        for j in range(n_maps):
            qm = jnp.where(_lane_range(lane, h * HEAD_DIM + j * map_dim, map_dim), q, 0.0)
            parts.append(_exp2_rows(jnp.dot(qm.astype(BF16), kt, preferred_element_type=F32)))
        w = parts[0][0] * parts[0][1]
        if n_maps == 2:
            w = w - parts[1][0] * (lam_ref[0] * parts[1][1])
        oh = jnp.dot(w.astype(BF16), vb, preferred_element_type=F32)
        o = jnp.where(_lane_range(lane, h * HEAD_DIM, HEAD_DIM), oh, o)
    if n_maps == 2:
        o = o * lax.rsqrt(_head_mean_square(o) + EPS) * g_ref[...] * post_scale
    o_ref[...] = o


def _cache_block_spec(layer):
    return pl.BlockSpec((1, 1, HEADS, PAST_LEN, HEAD_DIM), lambda b, i: (b, layer, 0, 0, 0))


def _cache_heads_on_lanes(ref):
    return jnp.concatenate([ref[0, 0, h] for h in range(HEADS)], axis=1)


def attention(p, row_blk0, cols, n_seq, seq_len, lam, norm_g, *, n_maps, post_scale, cache=None, layer=0):
    nb = seq_len // TM
    kv_len = seq_len + (PAST_LEN if cache is not None else 0)
    kern = functools.partial(_attn_kernel, n_maps=n_maps, post_scale=post_scale, with_cache=cache is not None)
    kv_spec = lambda col: pl.BlockSpec((seq_len, MIX_BLK), lambda b, i: (row_blk0 + b, col))
    cache_specs = [_cache_block_spec(layer)] * 2 if cache is not None else []
    return Part(
        kernel=kern,
        in_specs=[
            pl.BlockSpec(memory_space=pltpu.SMEM),
            pl.BlockSpec((TM, MIX_BLK), lambda b, i: ((row_blk0 + b) * nb + i, cols[0])),
            kv_spec(cols[1]), kv_spec(cols[2]), *cache_specs,
            pl.BlockSpec((1, MIX_BLK), lambda b, i: (0, 0)),
        ],
        out_specs=[pl.BlockSpec((TM, MIX_BLK), lambda b, i: (b * nb + i, 0))],
        out_shapes=[jax.ShapeDtypeStruct((n_seq * seq_len, MIX_BLK), F32)],
        scratch=[pltpu.VMEM((MIX_BLK, kv_len), BF16), pltpu.VMEM((kv_len, MIX_BLK), BF16)],
        args=[lam, p, p, p, *(cache or ()), norm_g])


def _na_slab_start(i):
    return jnp.clip(i - 1, 0, GRID_H // 4 - NA_SLAB_ROWS // 4)


def _na_kernel(q_ref, k_ref, v_ref, kc_ref, vc_ref, bias_ref, o_ref):
    i = pl.program_id(1)
    start = pl.multiple_of(_na_slab_start(i) * TM, TM)
    ks_t = k_ref[pl.ds(start, NA_SLAB), :].astype(F32).T.astype(BF16)
    vs = v_ref[pl.ds(start, NA_SLAB), :]
    kc_t = _cache_heads_on_lanes(kc_ref).T.astype(BF16)
    vc = _cache_heads_on_lanes(vc_ref).astype(BF16)
    q = q_ref[...].astype(F32) * (HEAD_DIM ** -0.5)
    lane = _head_lanes()
    o = jnp.zeros(q.shape, F32)
    for h in range(HEADS):
        in_head = _lane_range(lane, h * HEAD_DIM, HEAD_DIM)
        qm = jnp.where(in_head, q, 0.0).astype(BF16)
        s_loc = jnp.dot(qm, ks_t, preferred_element_type=F32) + bias_ref[0, h]
        s_ctx = jnp.dot(qm, kc_t, preferred_element_type=F32)
        m = jnp.maximum(jnp.max(s_loc, axis=-1, keepdims=True), jnp.max(s_ctx, axis=-1, keepdims=True))
        e_loc = jnp.exp(s_loc - m)
        e_ctx = jnp.exp(s_ctx - m)
        den = jnp.sum(e_loc, axis=-1, keepdims=True) + jnp.sum(e_ctx, axis=-1, keepdims=True)
        oh = (jnp.dot(e_loc.astype(BF16), vs, preferred_element_type=F32)
              + jnp.dot(e_ctx.astype(BF16), vc, preferred_element_type=F32)) / den
        o = jnp.where(in_head, oh, o)
    o_ref[...] = o


def _na_bias_tables(rpb):
    n_dr, n_dc = 2 * NA_WIN_H - 1, 2 * NA_WIN_W - 1
    cq = np.arange(GRID_W)[:, None]
    ck = np.arange(GRID_W)[None, :]
    wc0 = np.clip(cq - NA_WIN_W // 2, 0, GRID_W - NA_WIN_W)
    col_ok = (ck >= wc0) & (ck < wc0 + NA_WIN_W)
    col_pick = np.clip(ck - cq + NA_WIN_W - 1, 0, n_dc - 1)[..., None] == np.arange(n_dc)
    by_col = jnp.einsum("hab,qcb->haqc", rpb.astype(F32), jnp.asarray(col_pick, F32), precision=HIGHEST)
    margin = 4
    by_col = jnp.pad(by_col.transpose(0, 2, 1, 3), ((0, 0), (0, 0), (margin, margin), (0, 0)))
    by_col = by_col.reshape(HEADS, GRID_W, (n_dr + 2 * margin) * GRID_W)
    pieces, row_ok = [], []
    for tile in (0, 1, GRID_H // 4 - 1):
        slab0 = int(np.clip(tile - 1, 0, GRID_H // 4 - NA_SLAB_ROWS // 4)) * 4
        rq = tile * 4 + np.arange(4)
        rk = (slab0 + np.arange(NA_SLAB) // GRID_W)[None, :]
        wr0 = np.clip(rq - NA_WIN_H // 2, 0, GRID_H - NA_WIN_H)[:, None]
        row_ok.append((rk >= wr0) & (rk < wr0 + NA_WIN_H))
        for r in rq:
            first = slab0 - int(r) + NA_WIN_H - 1 + margin
            assert 0 <= first and first + NA_SLAB_ROWS <= n_dr + 2 * margin
            pieces.append(by_col[:, :, first * GRID_W:first * GRID_W + NA_SLAB])
    table = jnp.stack(pieces).reshape(3, 4, HEADS, GRID_W, NA_SLAB).transpose(0, 2, 1, 3, 4)
    valid = np.stack(row_ok)[:, None, :, None, :] & np.tile(col_ok, (1, NA_SLAB_ROWS))[None, None, None]
    table = jnp.where(jnp.asarray(valid), table, NEG_BIG)
    return table.reshape(3, HEADS, TM, NA_SLAB)


def na_latent(p, kc, vc, layer, bias):
    n_t = LAT_TILES_PER_SEQ
    seq_blk0 = TP // DEC_SEQ

    def bias_idx(b, i):
        return (jnp.minimum(i, 1) + i // (n_t - 1), 0, 0, 0)

    return Part(
        kernel=_na_kernel,
        in_specs=[
            pl.BlockSpec((TM, MIX_BLK), lambda b, i: (CTX_TILES + b * n_t + i, M_CQ)),
            pl.BlockSpec((DEC_SEQ, MIX_BLK), lambda b, i: (seq_blk0 + b, M_CK)),
            pl.BlockSpec((DEC_SEQ, MIX_BLK), lambda b, i: (seq_blk0 + b, M_CV)),
            _cache_block_spec(layer), _cache_block_spec(layer),
            pl.BlockSpec((1, HEADS, TM, NA_SLAB), bias_idx),
        ],
        out_specs=[pl.BlockSpec((TM, MIX_BLK), lambda b, i: (b * n_t + i, 0))],
        out_shapes=[jax.ShapeDtypeStruct((TL, MIX_BLK), F32)],
        scratch=[],
        args=[p, p, p, kc, vc, bias])


MAX_EXPONENT = 80.0


def _hgrn_direction(q_ref, f_ref, v_ref, lb, st_ref, o_ref, reverse):
    n_ch = TM // B_CHUNK
    r_idx = lax.broadcasted_iota(jnp.int32, (TM, TM), 0)
    c_idx = lax.broadcasted_iota(jnp.int32, (TM, TM), 1)
    tri = (c_idx >= r_idx) if reverse else (c_idx <= r_idx)
    zq = q_ref[...]
    q = zq * jax.nn.sigmoid(zq)
    z = f_ref[...]
    gate = (1.0 - lb) * jax.nn.sigmoid(z)
    logf = jnp.log(lb + gate)
    kk = (1.0 - lb) - gate
    b = _select_sum_left(jnp.where(tri, 1.0, 0.0).astype(BF16), logf)
    b3 = b.reshape(n_ch, B_CHUNK, MIX_BLK)
    mid = B_CHUNK // 2 if reverse else B_CHUNK // 2 - 1
    q_in = (q.reshape(b3.shape) * jnp.exp(b3 - b3[:, mid:mid + 1, :])).reshape(TM, MIX_BLK)
    q_dec = (q * jnp.exp(b)).astype(BF16)
    b_t = b.T
    kk_t = kk.T
    far = 0 if reverse else TM - 1
    b_far = b_t[:, far:far + 1]
    k_dec_t = (kk_t * jnp.exp(b_far - b_t)).astype(BF16)
    vb = v_ref[...].astype(BF16)
    st = st_ref[...]
    o_state = jnp.dot(q_dec, st.astype(BF16), preferred_element_type=F32)
    kv = jnp.dot(k_dec_t, vb, preferred_element_type=F32)
    st_ref[...] = st * jnp.exp(b_far) + jnp.where(_same_head_matrix(), kv, 0.0)
    lane = _head_lanes()
    token = lax.broadcasted_iota(jnp.int32, (1, TM), 1)
    local = lax.broadcasted_iota(jnp.int32, (HEADS * B_CHUNK, 1), 0) % B_CHUNK
    heads = [_lane_range(lane, h * HEAD_DIM, HEAD_DIM) for h in range(HEADS)]
    for c in range(n_ch):
        rows = slice(c * B_CHUNK, (c + 1) * B_CHUNK)
        ref = b_t[:, c * B_CHUNK + mid:c * B_CHUNK + mid + 1]
        k_c_t = (kk_t * jnp.exp(jnp.minimum(ref - b_t, MAX_EXPONENT))).astype(BF16)
        q_c = q_in[rows, :]
        lhs = jnp.concatenate([jnp.where(in_head, q_c, 0.0) for in_head in heads], axis=0)
        a = jnp.dot(lhs.astype(BF16), k_c_t, preferred_element_type=F32)
        t_abs = c * B_CHUNK + local
        a = jnp.where((token >= t_abs) if reverse else (token <= t_abs), a, 0.0)
        res = jnp.dot(a.astype(BF16), vb, preferred_element_type=F32)
        o_c = o_state[rows, :]
        for h, in_head in enumerate(heads):
            o_c = o_c + jnp.where(in_head, res[h * B_CHUNK:(h + 1) * B_CHUNK, :], 0.0)
        o_ref[rows, :] = o_c


def _hgrn_kernel(qf_ref, ff_ref, vf_ref, qb_ref, fb_ref, vb_ref, lb_ref, s0_ref,
                 of_ref, ob_ref, s_ref, stf_ref, stb_ref, *, has_s0):
    j = pl.program_id(1)

    @pl.when(j == 0)
    def _():
        if has_s0:
            stf_ref[...] = s0_ref[0, 0]
            stb_ref[...] = s0_ref[0, 1]
        else:
            stf_ref[...] = jnp.zeros((MIX_BLK, MIX_BLK), F32)
            stb_ref[...] = jnp.zeros((MIX_BLK, MIX_BLK), F32)

    lb = lb_ref[...]
    _hgrn_direction(qf_ref, ff_ref, vf_ref, lb[0:1], stf_ref, of_ref, False)
    _hgrn_direction(qb_ref, fb_ref, vb_ref, lb[1:2], stb_ref, ob_ref, True)

    @pl.when(j == pl.num_programs(1) - 1)
    def _():
        for d, st_ref in enumerate((stf_ref, stb_ref)):
            s = st_ref[...]
            for hd in range(HEADS):
                lo = hd * HEAD_DIM
                s_ref[0, d, hd] = s[lo:lo + HEAD_DIM, lo:lo + HEAD_DIM]


def hgrn(p, row_tile0, n_seq, seq_len, lb, s0):
    nb = seq_len // TM
    has_s0 = s0 is not None
    if s0 is None:
        s0 = jnp.zeros((1, 2, MIX_BLK, MIX_BLK), F32)

    def fwd(col):
        return pl.BlockSpec((TM, MIX_BLK), lambda s, j: (row_tile0 + s * nb + j, col))

    def bwd(col):
        return pl.BlockSpec((TM, MIX_BLK), lambda s, j: (row_tile0 + s * nb + nb - 1 - j, col))

    state_spec = pl.BlockSpec((1, 2, MIX_BLK, MIX_BLK), lambda s, j: (s if has_s0 else 0, 0, 0, 0))
    out_rows = n_seq * seq_len
    return Part(
        kernel=functools.partial(_hgrn_kernel, has_s0=has_s0),
        in_specs=[fwd(G_BQ), fwd(G_BFF), fwd(G_BV), bwd(G_BQ), bwd(G_BFB), bwd(G_BV),
                  pl.BlockSpec((2, MIX_BLK), lambda s, j: (0, 0)), state_spec],
        out_specs=[
            pl.BlockSpec((TM, MIX_BLK), lambda s, j: (s * nb + j, 0)),
            pl.BlockSpec((TM, MIX_BLK), lambda s, j: (s * nb + nb - 1 - j, 0)),
            pl.BlockSpec((1, 2, HEADS, HEAD_DIM, HEAD_DIM), lambda s, j: (s, 0, 0, 0, 0)),
        ],
        out_shapes=[jax.ShapeDtypeStruct((out_rows, MIX_BLK), F32),
                    jax.ShapeDtypeStruct((out_rows, MIX_BLK), F32),
                    jax.ShapeDtypeStruct((n_seq, 2, HEADS, HEAD_DIM, HEAD_DIM), F32)],
        scratch=[pltpu.VMEM((MIX_BLK, MIX_BLK), F32), pltpu.VMEM((MIX_BLK, MIX_BLK), F32)],
        args=[p, p, p, p, p, p, lb, s0])


def _state_to_blockdiag(s):
    eye = jnp.eye(HEADS, dtype=F32)
    full = s.astype(F32)[:, :, :, :, None, :] * eye[None, None, :, None, :, None]
    return full.reshape(s.shape[0], 2, MIX_BLK, MIX_BLK)


def _fft_kernel(u_ref, c64_ref, s64_ref, cl_ref, sl_ref, o_ref, a_ref, b_ref, *, norm):
    @pl.when(pl.program_id(1) == 0)
    def _():
        u = u_ref[...]
        a_ref[...] = jnp.dot(u, c64_ref[...], preferred_element_type=F32).astype(BF16)
        b_ref[...] = jnp.dot(u, s64_ref[...], preferred_element_type=F32).astype(BF16)

    o_ref[...] = (jnp.dot(cl_ref[...], a_ref[...], preferred_element_type=F32)
                  - jnp.dot(sl_ref[...], b_ref[...], preferred_element_type=F32)) * norm


def _dft_tables(n):
    k = np.arange(n)
    ang = 2.0 * np.pi * ((k[:, None] * k[None, :]) % n) / n
    return np.cos(ang), np.sin(ang)


def _dft_constants(seq_len):
    c64, s64 = _dft_tables(HEAD_DIM)
    eye = np.eye(HEADS)
    cl, sl = _dft_tables(seq_len)
    as_bf16 = lambda a: jnp.asarray(a, F32).astype(BF16)
    return as_bf16(np.kron(eye, c64)), as_bf16(np.kron(eye, s64)), as_bf16(cl), as_bf16(sl)


def fourier_mix(p, row_blk0, n_seq, seq_len, consts):
    c64, s64, cl, sl = consts
    nb = seq_len // TM
    norm = 1.0 / math.sqrt(seq_len * HEAD_DIM)
    return Part(
        kernel=functools.partial(_fft_kernel, norm=norm),
        in_specs=[
            pl.BlockSpec((seq_len, MIX_BLK), lambda s, i: (row_blk0 + s, M_DU)),
            pl.BlockSpec((MIX_BLK, MIX_BLK), lambda s, i: (0, 0)),
            pl.BlockSpec((MIX_BLK, MIX_BLK), lambda s, i: (0, 0)),
            pl.BlockSpec((TM, seq_len), lambda s, i: (i, 0)),
            pl.BlockSpec((TM, seq_len), lambda s, i: (i, 0)),
        ],
        out_specs=[pl.BlockSpec((TM, MIX_BLK), lambda s, i: (s * nb + i, 0))],
        out_shapes=[jax.ShapeDtypeStruct((n_seq * seq_len, MIX_BLK), F32)],
        scratch=[pltpu.VMEM((seq_len, MIX_BLK), BF16), pltpu.VMEM((seq_len, MIX_BLK), BF16)],
        args=[p, c64, s64, cl, sl])


def _route(logits_t, rb):
    per = N_EXPERTS // N_GROUPS
    score = [jax.nn.sigmoid(logits_t[e:e + 1, :]) for e in range(N_EXPERTS)]
    sel = [score[e] + rb[e:e + 1, :] for e in range(N_EXPERTS)]
    gscore = []
    for g in range(N_GROUPS):
        vals = sel[g * per:(g + 1) * per]
        best = None
        for a in range(per):
            for b in range(a + 1, per):
                pair = vals[a] + vals[b]
                best = pair if best is None else jnp.maximum(best, pair)
        gscore.append(best)
    chosen = []
    for g in range(N_GROUPS):
        ok = None
        for j in range(N_GROUPS):
            if j == g:
                continue
            cond = gscore[g] > gscore[j] if j < g else gscore[g] >= gscore[j]
            ok = cond if ok is None else ok & cond
        chosen.append(ok)
    picked = []
    for e in range(N_EXPERTS):
        g = e // per
        rank = jnp.zeros_like(sel[e])
        for j in range(g * per, (g + 1) * per):
            if j == e:
                continue
            ahead = sel[j] >= sel[e] if j < e else sel[j] > sel[e]
            rank = rank + jnp.where(ahead, 1.0, 0.0)
        picked.append(chosen[g] & (rank < 2.0))
    wsum = jnp.zeros_like(score[0])
    for e in range(N_EXPERTS):
        wsum = wsum + jnp.where(picked[e], score[e], 0.0)
    bucket = jnp.zeros_like(wsum)
    w_a = jnp.zeros_like(wsum)
    w_b = jnp.zeros_like(wsum)
    for g in range(N_GROUPS):
        for n, (a, b) in enumerate(EXPERT_PAIRS):
            hit = picked[g * per + a] & picked[g * per + b]
            bucket = jnp.where(hit, float(g * len(EXPERT_PAIRS) + n), bucket)
            w_a = jnp.where(hit, score[g * per + a] / wsum, w_a)
            w_b = jnp.where(hit, score[g * per + b] / wsum, w_b)
    return bucket, w_a, w_b


def _out_kernel(*refs):
    streams, rest = refs[:12], refs[12:]
    (bg_ref, mod_ref, hg_ref, w_ref, g2_ref, rw_ref, rb_ref,
     x1_ref, h2_ref, bucket_ref, rank_ref, counts_ref, run_ref) = rest
    is_ctx = pl.program_id(0) < TP // TM_OUT
    x, o_a, o_f, o_b, o_c, o_d = (jnp.where(is_ctx, streams[2 * n][...], streams[2 * n + 1][...])
                                  for n in range(6))

    @pl.when(pl.program_id(0) == 0)
    def _():
        run_ref[...] = jnp.zeros(run_ref.shape, F32)

    mod = mod_ref[0]
    hb = o_f + o_b
    zg = bg_ref[...]
    hb = hb * lax.rsqrt(_head_mean_square(hb) + EPS) * hg_ref[...] * (zg * jax.nn.sigmoid(zg))
    parts = (o_a, hb, o_c, o_d)
    mixed = jnp.zeros((TM_OUT, D_MODEL), F32)
    for n, part in enumerate(parts):
        mixed = mixed + jnp.dot(part.astype(BF16), w_ref[0, n * MIX_BLK:(n + 1) * MIX_BLK, :],
                                preferred_element_type=F32)
    x1 = x + mod[2:3] * mixed
    x1_ref[...] = x1
    ms = jnp.mean(x1 * x1, axis=-1, keepdims=True)
    h2 = x1 * lax.rsqrt(ms + EPS) * g2_ref[...] * (1.0 + mod[4:5]) + mod[3:4]
    rw = rw_ref[...]
    r = sum(jnp.dot(piece, rw, preferred_element_type=F32) for piece in _bf16_pieces(h2, 2))
    bucket, w_a, w_b = _route((r[:, :LANES] + r[:, LANES:]).T, rb_ref[...])
    h2_ref[:, :D_MODEL] = h2
    h2_ref[:, D_MODEL:] = jnp.concatenate([w_a, w_b, jnp.zeros((LANES - 2, TM_OUT), F32)], axis=0).T
    onehot = jnp.where(lax.broadcasted_iota(jnp.int32, (BUCKET_ROWS, 1), 0).astype(F32) == bucket, 1.0, 0.0)
    s_idx = lax.broadcasted_iota(jnp.int32, (TM_OUT, TM_OUT), 0)
    t_idx = lax.broadcasted_iota(jnp.int32, (TM_OUT, TM_OUT), 1)
    prefix = jnp.dot(onehot.astype(BF16), jnp.where(s_idx <= t_idx, 1.0, 0.0).astype(BF16),
                     preferred_element_type=F32)
    run = run_ref[...]
    rank = jnp.sum(onehot * (prefix - 1.0 + run[:, 0:1]), axis=0, keepdims=True)
    run = run + jnp.sum(onehot, axis=1, keepdims=True)
    run_ref[...] = run
    bucket_ref[...] = bucket.astype(jnp.int32)
    rank_ref[...] = rank.astype(jnp.int32)
    counts_ref[...] = run


def out_and_route(x_pair, x_is_combined, mixer_pairs, p, mod, layer, hgrn_g, w_out_bf16, norm2_g, router_pieces,
                  router_b):
    n_ctx = TP // TM_OUT
    ctx_tile = lambda i: (jnp.minimum(i, n_ctx - 1), 0)
    lat_tile = lambda i: (jnp.maximum(i - n_ctx, 0), 0)
    mod_row = lambda i: jnp.where(i < n_ctx, 0, 1 + (i - n_ctx) // (DEC_SEQ // TM_OUT))
    tile = lambda w: pl.BlockSpec((TM_OUT, w), lambda i: (i, 0))
    full = lambda r, c: pl.BlockSpec((r, c), lambda i: (0, 0))
    stream_specs = [pl.BlockSpec((TM_OUT, D_MODEL), ctx_tile),
                    pl.BlockSpec((TM_OUT, D_MODEL),
                                 (lambda i: (jnp.maximum(i, n_ctx), 0)) if x_is_combined else lat_tile)]
    stream_args = list(x_pair)
    for o_ctx, o_lat in mixer_pairs:
        stream_specs += [pl.BlockSpec((TM_OUT, MIX_BLK), ctx_tile), pl.BlockSpec((TM_OUT, MIX_BLK), lat_tile)]
        stream_args += [o_ctx, o_lat]
    return pl.pallas_call(
        _out_kernel,
        grid=(T // TM_OUT,),
        in_specs=stream_specs + [
            pl.BlockSpec((TM_OUT, MIX_BLK), lambda i: (i, G_BG)),
            pl.BlockSpec((1, 6, D_MODEL), lambda i: (mod_row(i), 0, 0)),
            full(1, MIX_BLK), pl.BlockSpec((1, D_MODEL, D_MODEL), lambda i: (layer, 0, 0)), full(1, D_MODEL),
            full(D_MODEL, 2 * LANES), full(N_EXPERTS, 1),
        ],
        out_specs=[tile(D_MODEL), tile(ROW_W), pl.BlockSpec((1, TM_OUT), lambda i: (0, i)),
                   pl.BlockSpec((1, TM_OUT), lambda i: (0, i)), full(BUCKET_ROWS, LANES)],
        out_shape=[jax.ShapeDtypeStruct((T, D_MODEL), F32),
                   jax.ShapeDtypeStruct((T, ROW_W), F32),
                   jax.ShapeDtypeStruct((1, T), jnp.int32),
                   jax.ShapeDtypeStruct((1, T), jnp.int32),
                   jax.ShapeDtypeStruct((BUCKET_ROWS, LANES), F32)],
        scratch_shapes=[pltpu.VMEM((BUCKET_ROWS, LANES), F32)],
        compiler_params=_cparams(("arbitrary",)),
        name="out_and_route",
    )(*stream_args, p, mod, jnp.tile(hgrn_g, HEADS)[None, :], w_out_bf16,
      norm2_g[None, :], router_pieces, router_b[:, None])


def _router_pieces(router_w):
    hi, lo = _bf16_pieces(router_w.astype(F32), 2)
    pad = lambda a: jnp.pad(a, ((0, 0), (0, LANES - N_EXPERTS)))
    return jnp.concatenate([pad(hi), pad(lo)], axis=1)


def routing_plan(bucket, rank, counts):
    counts = counts[:N_BUCKETS, 0].astype(jnp.int32)
    n_tiles = (counts + TM_MOE - 1) // TM_MOE
    tile_end = jnp.cumsum(n_tiles)
    tile_start = tile_end - n_tiles
    buckets = jnp.arange(N_BUCKETS, dtype=jnp.int32)
    start_of_token = jnp.sum(jnp.where(bucket[0][:, None] == buckets[None, :], tile_start[None, :], 0), axis=1)
    dest = start_of_token * TM_MOE + rank[0]
    tiles = jnp.arange(MAX_TILES, dtype=jnp.int32)
    valid = tiles < tile_end[-1]
    tile_bucket = jnp.sum((jnp.minimum(tiles, tile_end[-1] - 1)[:, None] >= tile_end[None, :]).astype(jnp.int32), axis=1)
    pair_a = np.array([a for a, _ in EXPERT_PAIRS], np.int32)
    pair_b = np.array([b for _, b in EXPERT_PAIRS], np.int32)
    per = N_EXPERTS // N_GROUPS
    exp_a = jnp.asarray((np.arange(N_BUCKETS) // len(EXPERT_PAIRS)) * per + np.tile(pair_a, N_GROUPS), jnp.int32)
    exp_b = jnp.asarray((np.arange(N_BUCKETS) // len(EXPERT_PAIRS)) * per + np.tile(pair_b, N_GROUPS), jnp.int32)
    pick = tile_bucket[:, None] == buckets[None, :]
    tile_a = jnp.sum(jnp.where(pick, exp_a[None, :], 0), axis=1)
    tile_b = jnp.sum(jnp.where(pick, exp_b[None, :], 0), axis=1)
    return dest.astype(jnp.int32), tile_a, tile_b, valid.astype(jnp.int32), (tile_end[-1:] - 1).astype(jnp.int32)


def _row_copy(src, src_row, dst, dst_row, sem):
    return pltpu.make_async_copy(src.at[pl.ds(src_row, 1), :], dst.at[pl.ds(dst_row, 1), :], sem)


def _scatter_kernel(dest_ref, h_ref, init_ref, o_ref, sem):
    del init_ref
    base = pl.program_id(0) * TM

    for r in range(TM):
        _row_copy(h_ref, r, o_ref, dest_ref[base + r], sem).start(priority=r % N_DMA_PRIORITIES)
    pltpu.make_async_copy(h_ref, o_ref.at[pl.ds(0, TM), :], sem).wait()


def scatter_to_slots(h2, dest, slots):
    return pl.pallas_call(
        _scatter_kernel,
        grid_spec=pltpu.PrefetchScalarGridSpec(
            num_scalar_prefetch=1,
            grid=(N_TILES,),
            in_specs=[pl.BlockSpec((TM, ROW_W), lambda i, d: (i, 0)),
                      pl.BlockSpec(memory_space=pl.ANY)],
            out_specs=pl.BlockSpec(memory_space=pl.ANY),
            scratch_shapes=[pltpu.SemaphoreType.DMA(())],
        ),
        out_shape=jax.ShapeDtypeStruct((N_SLOTS, ROW_W), F32),
        input_output_aliases={2: 0},
        compiler_params=_cparams(("arbitrary",)),
        name="scatter_to_slots",
    )(dest, h2, slots)


def _moe_kernel(ta_ref, tb_ref, valid_ref, last_ref, h_ref, wga_ref, wua_ref, wda_ref, wgb_ref, wub_ref, wdb_ref,
                o_ref):
    del ta_ref, tb_ref, last_ref
    i = pl.program_id(0)

    @pl.when(valid_ref[i] == 1)
    def _():
        x = h_ref[:, :D_MODEL].astype(BF16)
        gates = h_ref[:, D_MODEL:]
        y = jnp.zeros((TM_MOE, D_MODEL), F32)
        for n, (wg, wu, wd) in enumerate(((wga_ref, wua_ref, wda_ref), (wgb_ref, wub_ref, wdb_ref))):
            a = jnp.dot(x, wg[0, 0].astype(BF16), preferred_element_type=F32)
            u = jnp.dot(x, wu[0, 0].astype(BF16), preferred_element_type=F32)
            z = a * jax.nn.sigmoid(a) * u * gates[:, n:n + 1]
            y = y + jnp.dot(z.astype(BF16), wd[0, 0].astype(BF16), preferred_element_type=F32)
        o_ref[:, 0, :] = y

    @pl.when(valid_ref[i] == 0)
    def _():
        o_ref[...] = jnp.zeros((TM_MOE, 1, D_MODEL), F32)


def moe(h_slots, tile_a, tile_b, valid, last, layer, wg, wu, wd):
    up_a = pl.BlockSpec((1, 1, D_MODEL, D_EXPERT), lambda i, ta, tb, v, last: (layer, ta[i], 0, 0))
    up_b = pl.BlockSpec((1, 1, D_MODEL, D_EXPERT), lambda i, ta, tb, v, last: (layer, tb[i], 0, 0))
    down_a = pl.BlockSpec((1, 1, D_EXPERT, D_MODEL), lambda i, ta, tb, v, last: (layer, ta[i], 0, 0))
    down_b = pl.BlockSpec((1, 1, D_EXPERT, D_MODEL), lambda i, ta, tb, v, last: (layer, tb[i], 0, 0))
    return pl.pallas_call(
        _moe_kernel,
        grid_spec=pltpu.PrefetchScalarGridSpec(
            num_scalar_prefetch=4,
            grid=(MAX_TILES,),
            in_specs=[pl.BlockSpec((TM_MOE, ROW_W), lambda i, ta, tb, v, last: (jnp.minimum(i, last[0]), 0)),
                      up_a, up_a, down_a, up_b, up_b, down_b],
            out_specs=pl.BlockSpec((TM_MOE, 1, D_MODEL), lambda i, ta, tb, v, last: (i, 0, 0)),
        ),
        out_shape=jax.ShapeDtypeStruct((N_SLOTS, 1, D_MODEL), F32),
        compiler_params=_cparams(("arbitrary",)),
        name="moe",
    )(tile_a, tile_b, valid, last, h_slots, wg, wu, wd, wg, wu, wd)


def _gather_tile(dest_ref, y_ref, buf_ref, sem, tile, slot):
    for r in range(TM):
        pltpu.make_async_copy(y_ref.at[pl.ds(dest_ref[tile * TM + r], 1)],
                              buf_ref.at[slot, pl.ds(r, 1)], sem.at[slot]).start(priority=r % N_DMA_PRIORITIES)


def _moe_residual(dest_ref, y_ref, buf_ref, sem, x1_ref, mod_ref):
    i = pl.program_id(0)
    slot = i % 2

    @pl.when(i == 0)
    def _():
        _gather_tile(dest_ref, y_ref, buf_ref, sem, 0, 0)

    @pl.when(i + 1 < pl.num_programs(0))
    def _():
        _gather_tile(dest_ref, y_ref, buf_ref, sem, i + 1, 1 - slot)

    pltpu.make_async_copy(y_ref.at[pl.ds(0, TM)], buf_ref.at[slot], sem.at[slot]).wait()
    return x1_ref[...] + mod_ref[0][5:6] * buf_ref[slot, :, 0, :]


def _final_kernel(dest_ref, y_ref, x1_ref, mod_ref, g_ref, oc_ref, ol_ref, buf_ref, sem):
    x2 = _moe_residual(dest_ref, y_ref, buf_ref, sem, x1_ref, mod_ref)
    ms = jnp.mean(x2 * x2, axis=-1, keepdims=True)
    y = x2 * lax.rsqrt(ms + EPS) * g_ref[...]

    @pl.when(_is_ctx_tile())
    def _():
        oc_ref[...] = y

    @pl.when(jnp.logical_not(_is_ctx_tile()))
    def _():
        ol_ref[...] = y


_GATHER_SCRATCH = [pltpu.VMEM((2, TM, 1, D_MODEL), F32), pltpu.SemaphoreType.DMA((2,))]


def final_norm(dest, y_slots, x1, mod, final_g):
    return pl.pallas_call(
        _final_kernel,
        grid_spec=pltpu.PrefetchScalarGridSpec(
            num_scalar_prefetch=1,
            grid=(N_TILES,),
            in_specs=[pl.BlockSpec(memory_space=pl.ANY),
                      pl.BlockSpec((TM, D_MODEL), lambda i, d: (i, 0)),
                      pl.BlockSpec((1, 6, D_MODEL), lambda i, d: (_mod_row(i), 0, 0)),
                      pl.BlockSpec((1, D_MODEL), lambda i, d: (0, 0))],
            out_specs=[pl.BlockSpec((TM, D_MODEL), _ctx_tile), pl.BlockSpec((TM, D_MODEL), _lat_tile)],
            scratch_shapes=_GATHER_SCRATCH,
        ),
        out_shape=[jax.ShapeDtypeStruct((TP, D_MODEL), F32), jax.ShapeDtypeStruct((TL, D_MODEL), F32)],
        compiler_params=_cparams(("arbitrary",)),
        name="final_norm",
    )(dest, y_slots, x1, mod, final_g[None, :])


def kernel(x_prompt, x_sample, cache_diff_k, cache_diff_v, cache_na_k, cache_na_v, state_hgrn, c, c_ctx,
           norm1_g, norm2_g, ada_w, ada_b, w_in, w_out, diff_lambda, diff_subln_g, hgrn_lb_logits,
           hgrn_norm_g, na_rpb, router_w, router_b, moe_w_gate, moe_w_up, moe_w_down, final_norm_g):
    assert SEQ == TM and PAST_LEN == TM and DEC_SEQ % TM_OUT == 0 and TP % DEC_SEQ == 0
    x_pair = (x_prompt.reshape(TP, D_MODEL), x_sample.reshape(TL, D_MODEL))
    w_in_bf16 = jnp.concatenate([w_in[:, :, c * MIX_BLK:(c + 1) * MIX_BLK] for c in PM_BLOCKS + PG_BLOCKS],
                                axis=-1).astype(BF16)
    w_out_bf16 = w_out.astype(BF16)
    router_pieces = _router_pieces(router_w)
    mods = modulation(jnp.concatenate([c_ctx[None, :], c], axis=0), ada_w, ada_b)
    mods = mods.reshape(DEPTH, 3, 6, D_MODEL)
    lb_sm = jax.nn.softmax(hgrn_lb_logits.astype(F32), axis=0)
    lb_all = jnp.cumsum(lb_sm, axis=0) - lb_sm[0:1]
    rope = _rope_tables()
    dft_ctx = _dft_constants(SEQ)
    dft_lat = _dft_constants(DEC_SEQ)
    lat_blk0 = TP // DEC_SEQ
    states = []
    moe_state = None
    for l in range(DEPTH):
        if moe_state is None:
            pm, pg, *new_kv = projection(*x_pair, mods[l], norm1_g[l], w_in_bf16, rope)
        else:
            x, pm, pg, *new_kv = projection_after_moe(*moe_state, mods[l - 1], mods[l], l, norm1_g[l], w_in_bf16, rope,
                                                 new_kv)
            x_pair = (x, x)

        lq = diff_lambda[l].astype(F32)
        lam_init = 0.8 - 0.6 * math.exp(-0.3 * l)
        lam = (jnp.exp(jnp.sum(lq[0] * lq[1])) - jnp.exp(jnp.sum(lq[2] * lq[3])) + lam_init).reshape(1)
        subln = jnp.tile(diff_subln_g[l], HEADS)[None, :]
        diff = functools.partial(attention, pm, cols=(M_AQ, M_AK, M_AV), lam=lam, norm_g=subln, n_maps=2,
                                 post_scale=1.0 - lam_init)
        (oa_ctx,), (of_ctx, ob_ctx, st_ctx), (oc_ctx,), (od_ctx,) = run_parts(
            [diff(row_blk0=0, n_seq=BATCH, seq_len=SEQ),
             hgrn(pg, 0, BATCH, SEQ, lb_all[l], None),
             attention(pm, 0, (M_CQ, M_CK, M_CV), BATCH, SEQ, lam, subln, n_maps=1, post_scale=1.0),
             fourier_mix(pm, 0, BATCH, SEQ, dft_ctx)],
            (BATCH, SEQ // TM), "context_mixers")
        (oa_lat,), (of_lat, ob_lat, _), (oc_lat,), (od_lat,) = run_parts(
            [diff(row_blk0=lat_blk0, n_seq=DEC_BATCH, seq_len=DEC_SEQ, cache=(cache_diff_k, cache_diff_v), layer=l),
             hgrn(pg, CTX_TILES, DEC_BATCH, DEC_SEQ, lb_all[l], _state_to_blockdiag(state_hgrn[:, l])),
             na_latent(pm, cache_na_k, cache_na_v, l, _na_bias_tables(na_rpb[l])),
             fourier_mix(pm, lat_blk0, DEC_BATCH, DEC_SEQ, dft_lat)],
            (DEC_BATCH, DEC_SEQ // TM), "latent_mixers")

        x1, h2, bucket, rank, counts = out_and_route(
            x_pair, l > 0, ((oa_ctx, oa_lat), (of_ctx, of_lat), (ob_ctx, ob_lat), (oc_ctx, oc_lat),
                            (od_ctx, od_lat)),
            pg, mods[l], l, hgrn_norm_g[l], w_out_bf16, norm2_g[l], router_pieces, router_b)
        dest, *tile_plan = routing_plan(bucket, rank, counts)
        h_slots = scatter_to_slots(h2, dest, jnp.zeros((N_SLOTS, ROW_W), F32) if l == 0 else h_slots)
        y_slots = moe(h_slots, *tile_plan, l, moe_w_gate, moe_w_up, moe_w_down)
        moe_state = (dest, y_slots, x1)

        states.append(st_ctx)
    y_prompt, y_sample = final_norm(*moe_state, mods[DEPTH - 1], final_norm_g)
    return (y_prompt.reshape(BATCH, SEQ, D_MODEL), y_sample.reshape(DEC_BATCH, DEC_SEQ, D_MODEL),
            *new_kv, jnp.stack(states, axis=1))
```

```python
import functools
import math
from typing import Any, NamedTuple

import numpy as np
import jax
import jax.numpy as jnp
from jax import lax
from jax.experimental import pallas as pl
from jax.experimental.pallas import tpu as pltpu

F32 = jnp.float32
BF16 = jnp.bfloat16
HIGHEST = lax.Precision.HIGHEST

D_MODEL = 1024
BATCH = 16
SEQ = 256
DEPTH = 2
DEC_BATCH = 2
DEC_SEQ = 2048
PAST_LEN = 256
GRID_W = 64
GRID_H = DEC_SEQ // GRID_W
EPS = 1e-6
NEG_BIG = -1e30
HEADS = 4
HEAD_DIM = 64
MIX_BLK = HEADS * HEAD_DIM
A_DIM = 32
ROPE_BASE = 10000.0
B_CHUNK = 32
NA_WIN_H = 8
NA_WIN_W = 16
N_EXPERTS = 16
N_GROUPS = 4
D_EXPERT = 512
PROJ_W = 12 * MIX_BLK
TP = BATCH * SEQ
TL = DEC_BATCH * DEC_SEQ
T = TP + TL
TM = 256
N_TILES = T // TM
CTX_TILES = TP // TM
LAT_TILES_PER_SEQ = DEC_SEQ // TM
(C_AQ, C_AK, C_AV, C_BQ, C_BFF, C_BFB, C_BV, C_BG, C_CQ, C_CK, C_CV, C_DU) = range(12)
PM_BLOCKS = (C_AQ, C_AK, C_AV, C_CQ, C_CK, C_CV, C_DU)
PG_BLOCKS = (C_BQ, C_BFF, C_BFB, C_BV, C_BG)
(M_AQ, M_AK, M_AV, M_CQ, M_CK, M_CV, M_DU) = range(len(PM_BLOCKS))
(G_BQ, G_BFF, G_BFB, G_BV, G_BG) = range(len(PG_BLOCKS))
PM_W = len(PM_BLOCKS) * MIX_BLK
PG_W = len(PG_BLOCKS) * MIX_BLK
NA_SLAB_ROWS = 12
NA_SLAB = NA_SLAB_ROWS * GRID_W
LANES = 128
ROW_W = D_MODEL + LANES
EXPERT_PAIRS = ((0, 1), (0, 2), (0, 3), (1, 3), (2, 3), (2, 1))
N_BUCKETS = N_GROUPS * len(EXPERT_PAIRS)
BUCKET_ROWS = 32
N_DMA_PRIORITIES = 2
TM_OUT = 512
TM_MOE = 384
MAX_TILES = T // TM_MOE + N_BUCKETS
N_SLOTS = MAX_TILES * TM_MOE
VMEM_LIMIT = 56 * 1024 * 1024


def _cparams(sem):
    return pltpu.CompilerParams(dimension_semantics=sem, vmem_limit_bytes=VMEM_LIMIT)


class Part(NamedTuple):
    kernel: Any
    in_specs: list
    args: list
    out_specs: list
    out_shapes: list
    scratch: list


def _run_parts_kernel(*refs, layout):
    n_in = sum(n for _, n, _, _ in layout)
    n_out = sum(n for _, _, n, _ in layout)
    ins, outs, scratch = refs[:n_in], refs[n_in:n_in + n_out], refs[n_in + n_out:]
    i = o = s = 0
    for kernel, k_in, k_out, k_scratch in layout:
        kernel(*ins[i:i + k_in], *outs[o:o + k_out], *scratch[s:s + k_scratch])
        i, o, s = i + k_in, o + k_out, s + k_scratch


def run_parts(parts, grid, name):
    layout = tuple((p.kernel, len(p.in_specs), len(p.out_specs), len(p.scratch)) for p in parts)
    outs = pl.pallas_call(
        functools.partial(_run_parts_kernel, layout=layout),
        grid=grid,
        in_specs=[s for p in parts for s in p.in_specs],
        out_specs=[s for p in parts for s in p.out_specs],
        out_shape=[s for p in parts for s in p.out_shapes],
        scratch_shapes=[s for p in parts for s in p.scratch],
        compiler_params=_cparams(("arbitrary", "arbitrary")),
        name=name,
    )(*[a for p in parts for a in p.args])
    result, o = [], 0
    for p in parts:
        result.append(outs[o:o + len(p.out_specs)])
        o += len(p.out_specs)
    return result


def _head_lanes(width=MIX_BLK):
    return lax.broadcasted_iota(jnp.int32, (1, width), 1)


def _lane_range(lane, lo, n):
    return (lane >= lo) & (lane < lo + n)


def _same_head_matrix():
    r = lax.broadcasted_iota(jnp.int32, (MIX_BLK, MIX_BLK), 0) // HEAD_DIM
    c = lax.broadcasted_iota(jnp.int32, (MIX_BLK, MIX_BLK), 1) // HEAD_DIM
    return r == c


def _bf16_pieces(x, n):
    pieces = []
    for _ in range(n):
        piece = x.astype(BF16)
        pieces.append(piece)
        x = x - piece.astype(F32)
    return pieces


def _select_sum_left(onehot_bf16, x):
    return sum(jnp.dot(onehot_bf16, piece, preferred_element_type=F32) for piece in _bf16_pieces(x, 3))


def _select_sum_right(x, onehot_bf16):
    rows = x.shape[0]
    stacked = jnp.dot(jnp.concatenate(_bf16_pieces(x, 3), axis=0), onehot_bf16, preferred_element_type=F32)
    return stacked[:rows] + stacked[rows:2 * rows] + stacked[2 * rows:]


def _head_mean_square(o):
    ones = jnp.where(_same_head_matrix(), 1.0, 0.0).astype(BF16)
    return _select_sum_right(o * o, ones) * (1.0 / HEAD_DIM)


def _mod_row(i):
    return jnp.where(i < CTX_TILES, 0, 1 + (i - CTX_TILES) // LAT_TILES_PER_SEQ)


def _mod_kernel(c_ref, w_ref, b_ref, o_ref):
    w = w_ref[0]
    for r in range(3):
        c = c_ref[r]
        s = c * jax.nn.sigmoid(c)
        o_ref[0, r:r + 1, :] = jnp.sum(s * w, axis=0, keepdims=True) + b_ref[0]


def modulation(c_rows, ada_w, ada_b):
    nt = 768
    n_out = 6 * D_MODEL
    return pl.pallas_call(
        _mod_kernel,
        grid=(DEPTH, n_out // nt),
        in_specs=[
            pl.BlockSpec((3, D_MODEL, 1), lambda l, j: (0, 0, 0)),
            pl.BlockSpec((1, D_MODEL, nt), lambda l, j: (l, 0, j)),
            pl.BlockSpec((1, 1, nt), lambda l, j: (l, 0, j)),
        ],
        out_specs=pl.BlockSpec((1, 3, nt), lambda l, j: (l, 0, j)),
        out_shape=jax.ShapeDtypeStruct((DEPTH, 3, n_out), F32),
        compiler_params=_cparams(("arbitrary", "arbitrary")),
        name="modulation",
    )(c_rows[:, :, None], ada_w, ada_b[:, None, :])


def _is_ctx_tile():
    return pl.program_id(0) < CTX_TILES


def _ctx_tile(i, *_):
    return (jnp.minimum(i, CTX_TILES - 1), 0)


def _lat_tile(i, *_):
    return (jnp.maximum(i - CTX_TILES, 0), 0)


def _proj_kernel(xc_ref, xl_ref, mod_ref, g_ref, w_ref, cos_ref, sa_ref, sb_ref, pm_ref, pg_ref, *cache_refs):
    x = jnp.where(_is_ctx_tile(), xc_ref[...], xl_ref[...])
    _proj_body(x, mod_ref, g_ref, w_ref, cos_ref, sa_ref, sb_ref, pm_ref, pg_ref, cache_refs)


def _proj_after_moe_kernel(dest_ref, y_ref, x1_ref, modp_ref, mod_ref, g_ref, w_ref, cos_ref, sa_ref, sb_ref,
                           *rest):
    x2_ref, pm_ref, pg_ref = rest[4:7]
    cache_refs, (buf_ref, sem) = rest[7:11], rest[11:]
    x2 = _moe_residual(dest_ref, y_ref, buf_ref, sem, x1_ref, modp_ref)
    x2_ref[...] = x2
    _proj_body(x2, mod_ref, g_ref, w_ref, cos_ref, sa_ref, sb_ref, pm_ref, pg_ref, cache_refs)


def _proj_body(x, mod_ref, g_ref, w_ref, cos_ref, sa_ref, sb_ref, pm_ref, pg_ref, cache_refs):
    ms = jnp.mean(x * x, axis=-1, keepdims=True)
    mod = mod_ref[0]
    h = x * lax.rsqrt(ms + EPS) * g_ref[...] * (1.0 + mod[1:2]) + mod[0:1]
    p = jnp.dot(h.astype(BF16), w_ref[0], preferred_element_type=F32)
    t = p[:, :2 * MIX_BLK]
    pm_ref[:, :2 * MIX_BLK] = (t * cos_ref[...] + pltpu.roll(t, 1, 1) * sa_ref[...]
                               + pltpu.roll(t, 2 * MIX_BLK - 1, 1) * sb_ref[...]).astype(BF16)
    pm_ref[:, 2 * MIX_BLK:] = p[:, 2 * MIX_BLK:PM_W].astype(BF16)
    pg_ref[...] = p[:, PM_W:]

    @pl.when(_is_ctx_tile())
    def _():
        for ref, col in zip(cache_refs, (M_AK, M_AV, M_CK, M_CV)):
            for hd in range(HEADS):
                lo = col * MIX_BLK + hd * HEAD_DIM
                ref[0, 0, hd] = p[:, lo:lo + HEAD_DIM]
            if ref.shape[1] > 1:
                ref[0, 1:] = jnp.zeros((ref.shape[1] - 1,) + tuple(ref.shape[2:]), F32)


_CACHE_SHAPE = jax.ShapeDtypeStruct((BATCH, DEPTH, HEADS, SEQ, HEAD_DIM), F32)


def _cache_spec(layer):
    n_layers = DEPTH if layer == 0 else 1
    return pl.BlockSpec((1, n_layers, HEADS, SEQ, HEAD_DIM),
                        lambda i, *_: (jnp.minimum(i, CTX_TILES - 1), layer, 0, 0, 0))


def _rope_tables():
    nf = A_DIM // 4
    freqs = ROPE_BASE ** (-np.arange(nf, dtype=np.float64) / nf)
    pos = np.arange(DEC_SEQ)
    row = (pos // GRID_W).astype(np.float64)
    col = (pos % GRID_W).astype(np.float64)
    ang = np.concatenate([row[:, None] * freqs, col[:, None] * freqs], axis=-1)
    cos = np.repeat(np.cos(ang), 2, axis=-1)
    sin = np.repeat(np.sin(ang), 2, axis=-1)
    odd = (np.arange(A_DIM) % 2 == 1)[None, :]
    sin_from_left = np.where(odd, sin, 0.0)
    sin_from_right = np.where(odd, 0.0, -sin)
    reps = 2 * MIX_BLK // A_DIM
    ident = (np.ones((TM, 2 * MIX_BLK)), np.zeros((TM, 2 * MIX_BLK)))
    return tuple(jnp.asarray(np.concatenate([np.tile(t, (1, reps)), tail], axis=0), F32)
                 for t, tail in ((cos, ident[0]), (sin_from_left, ident[1]), (sin_from_right, ident[1])))


def _rope_block(i):
    return (jnp.where(i < CTX_TILES, LAT_TILES_PER_SEQ, (i - CTX_TILES) % LAT_TILES_PER_SEQ), 0)


def projection_after_moe(dest, y_slots, x1, mod_prev, mod, layer, norm_g, w_in_bf16, rope, caches):
    rope_spec = pl.BlockSpec((TM, 2 * MIX_BLK), lambda i, d: _rope_block(i))
    mod_spec = pl.BlockSpec((1, 6, D_MODEL), lambda i, d: (_mod_row(i), 0, 0))
    n_in = 10
    return pl.pallas_call(
        _proj_after_moe_kernel,
        grid_spec=pltpu.PrefetchScalarGridSpec(
            num_scalar_prefetch=1,
            grid=(N_TILES,),
            in_specs=[pl.BlockSpec(memory_space=pl.ANY),
                      pl.BlockSpec((TM, D_MODEL), lambda i, d: (i, 0)),
                      mod_spec, mod_spec,
                      pl.BlockSpec((1, D_MODEL), lambda i, d: (0, 0)),
                      pl.BlockSpec((1, D_MODEL, PROJ_W), lambda i, d: (layer, 0, 0)),
                      rope_spec, rope_spec, rope_spec] + [pl.BlockSpec(memory_space=pl.ANY)] * 4,
            out_specs=[pl.BlockSpec((TM, D_MODEL), lambda i, d: (i, 0)),
                       pl.BlockSpec((TM, PM_W), lambda i, d: (i, 0)),
                       pl.BlockSpec((TM, PG_W), lambda i, d: (i, 0))] + [_cache_spec(layer)] * 4,
            scratch_shapes=_GATHER_SCRATCH,
        ),
        out_shape=[jax.ShapeDtypeStruct((T, D_MODEL), F32), jax.ShapeDtypeStruct((T, PM_W), BF16),
                   jax.ShapeDtypeStruct((T, PG_W), F32)] + [_CACHE_SHAPE] * 4,
        input_output_aliases={n_in + n: 3 + n for n in range(4)},
        compiler_params=_cparams(("arbitrary",)),
        name="projection_after_moe",
    )(dest, y_slots, x1, mod_prev, mod, norm_g[None, :], w_in_bf16, *rope, *caches)


def projection(x_ctx, x_lat, mod, norm_g, w_in_bf16, rope):
    layer = 0
    rope_spec = pl.BlockSpec((TM, 2 * MIX_BLK), _rope_block)
    return pl.pallas_call(
        _proj_kernel,
        grid=(N_TILES,),
        in_specs=[
            pl.BlockSpec((TM, D_MODEL), _ctx_tile),
            pl.BlockSpec((TM, D_MODEL), _lat_tile),
            pl.BlockSpec((1, 6, D_MODEL), lambda i: (_mod_row(i), 0, 0)),
            pl.BlockSpec((1, D_MODEL), lambda i: (0, 0)),
            pl.BlockSpec((1, D_MODEL, PROJ_W), lambda i: (layer, 0, 0)),
            rope_spec, rope_spec, rope_spec,
        ],
        out_specs=[pl.BlockSpec((TM, PM_W), lambda i: (i, 0)), pl.BlockSpec((TM, PG_W), lambda i: (i, 0))]
        + [_cache_spec(layer)] * 4,
        out_shape=[jax.ShapeDtypeStruct((T, PM_W), BF16), jax.ShapeDtypeStruct((T, PG_W), F32)]
        + [_CACHE_SHAPE] * 4,
        compiler_params=_cparams(("arbitrary",)),
        name="projection",
    )(x_ctx, x_lat, mod, norm_g[None, :], w_in_bf16, *rope)


LOG2_E = 1.4426950408889634


def _exp2_rows(s):
    e = jnp.exp2(s - jnp.max(s, axis=-1, keepdims=True))
    return e, 1.0 / jnp.sum(e, axis=-1, keepdims=True)


def _attn_kernel(lam_ref, q_ref, k_ref, v_ref, *rest, n_maps, post_scale, with_cache):
    if with_cache:
        kc_ref, vc_ref, g_ref, o_ref, kt_ref, vb_ref = rest
    else:
        g_ref, o_ref, kt_ref, vb_ref = rest

    @pl.when(pl.program_id(1) == 0)
    def _():
        k = k_ref[...].astype(F32)
        v = v_ref[...]
        if with_cache:
            k = jnp.concatenate([_cache_heads_on_lanes(kc_ref), k], axis=0)
            v = jnp.concatenate([_cache_heads_on_lanes(vc_ref).astype(BF16), v], axis=0)
        kt_ref[...] = k.T.astype(BF16)
        vb_ref[...] = v

    lane = _head_lanes()
    map_dim = HEAD_DIM // n_maps
    q = q_ref[...].astype(F32) * (map_dim ** -0.5 * LOG2_E)
    kt = kt_ref[...]
    vb = vb_ref[...]
    stack_rows = kt_ref.shape[1] <= TM
    weights = []
    for h in range(HEADS):
        masked = [jnp.where(_lane_range(lane, h * HEAD_DIM + j * map_dim, map_dim), q, 0.0).astype(BF16)
                  for j in range(n_maps)]
        if stack_rows:
            s = jnp.dot(jnp.concatenate(masked, axis=0), kt, preferred_element_type=F32)
            scores = [s[j * TM:(j + 1) * TM] for j in range(n_maps)]
        else:
            scores = [jnp.dot(m, kt, preferred_element_type=F32) for m in masked]
        parts = [_exp2_rows(x) for x in scores]
        w = parts[0][0] * parts[0][1]
        if n_maps == 2:
            w = w - parts[1][0] * (lam_ref[0] * parts[1][1])
        weights.append(w.astype(BF16))
    if stack_rows:
        oh = jnp.dot(jnp.concatenate(weights, axis=0), vb, preferred_element_type=F32)
        outs = [oh[h * TM:(h + 1) * TM] for h in range(HEADS)]
    else:
        outs = [jnp.dot(w, vb, preferred_element_type=F32) for w in weights]
    o = jnp.zeros(q.shape, F32)
    for h in range(HEADS):
        o = jnp.where(_lane_range(lane, h * HEAD_DIM, HEAD_DIM), outs[h], o)
    if n_maps == 2:
        o = o * lax.rsqrt(_head_mean_square(o) + EPS) * g_ref[...] * post_scale
    o_ref[...] = o


def _cache_block_spec(layer):
    return pl.BlockSpec((1, 1, HEADS, PAST_LEN, HEAD_DIM), lambda b, i: (b, layer, 0, 0, 0))


def _cache_heads_on_lanes(ref):
    return jnp.concatenate([ref[0, 0, h] for h in range(HEADS)], axis=1)


def attention(p, row_blk0, cols, n_seq, seq_len, lam, norm_g, *, n_maps, post_scale, cache=None, layer=0):
    nb = seq_len // TM
    kv_len = seq_len + (PAST_LEN if cache is not None else 0)
    kern = functools.partial(_attn_kernel, n_maps=n_maps, post_scale=post_scale, with_cache=cache is not None)
    kv_spec = lambda col: pl.BlockSpec((seq_len, MIX_BLK), lambda b, i: (row_blk0 + b, col))
    cache_specs = [_cache_block_spec(layer)] * 2 if cache is not None else []
    return Part(
        kernel=kern,
        in_specs=[
            pl.BlockSpec(memory_space=pltpu.SMEM),
            pl.BlockSpec((TM, MIX_BLK), lambda b, i: ((row_blk0 + b) * nb + i, cols[0])),
            kv_spec(cols[1]), kv_spec(cols[2]), *cache_specs,
            pl.BlockSpec((1, MIX_BLK), lambda b, i: (0, 0)),
        ],
        out_specs=[pl.BlockSpec((TM, MIX_BLK), lambda b, i: (b * nb + i, 0))],
        out_shapes=[jax.ShapeDtypeStruct((n_seq * seq_len, MIX_BLK), F32)],
        scratch=[pltpu.VMEM((MIX_BLK, kv_len), BF16), pltpu.VMEM((kv_len, MIX_BLK), BF16)],
        args=[lam, p, p, p, *(cache or ()), norm_g])


def _na_slab_start(i):
    return jnp.clip(i - 1, 0, GRID_H // 4 - NA_SLAB_ROWS // 4)


def _na_kernel(q_ref, k_ref, v_ref, kc_ref, vc_ref, bias_ref, o_ref):
    i = pl.program_id(1)
    start = pl.multiple_of(_na_slab_start(i) * TM, TM)
    ks_t = k_ref[pl.ds(start, NA_SLAB), :].astype(F32).T.astype(BF16)
    vs = v_ref[pl.ds(start, NA_SLAB), :]
    kc_t = _cache_heads_on_lanes(kc_ref).T.astype(BF16)
    vc = _cache_heads_on_lanes(vc_ref).astype(BF16)
    q = q_ref[...].astype(F32) * (HEAD_DIM ** -0.5)
    lane = _head_lanes()
    heads = [_lane_range(lane, h * HEAD_DIM, HEAD_DIM) for h in range(HEADS)]
    qm = jnp.concatenate([jnp.where(in_head, q, 0.0) for in_head in heads], axis=0).astype(BF16)
    s_loc = jnp.dot(qm, ks_t, preferred_element_type=F32)
    s_ctx = jnp.dot(qm, kc_t, preferred_element_type=F32)
    e_locs, e_ctxs, dens = [], [], []
    for h in range(HEADS):
        rows = slice(h * TM, (h + 1) * TM)
        sl = s_loc[rows] + bias_ref[0, h]
        sc = s_ctx[rows]
        m = jnp.maximum(jnp.max(sl, axis=-1, keepdims=True), jnp.max(sc, axis=-1, keepdims=True))
        e_loc = jnp.exp(sl - m)
        e_ctx = jnp.exp(sc - m)
        dens.append(jnp.sum(e_loc, axis=-1, keepdims=True) + jnp.sum(e_ctx, axis=-1, keepdims=True))
        e_locs.append(e_loc.astype(BF16))
        e_ctxs.append(e_ctx.astype(BF16))
    oh = (jnp.dot(jnp.concatenate(e_locs, axis=0), vs, preferred_element_type=F32)
          + jnp.dot(jnp.concatenate(e_ctxs, axis=0), vc, preferred_element_type=F32))
    o = jnp.zeros(q.shape, F32)
    for h in range(HEADS):
        o = jnp.where(heads[h], oh[h * TM:(h + 1) * TM] / dens[h], o)
    o_ref[...] = o


def _na_bias_tables(rpb):
    n_dr, n_dc = 2 * NA_WIN_H - 1, 2 * NA_WIN_W - 1
    cq = np.arange(GRID_W)[:, None]
    ck = np.arange(GRID_W)[None, :]
    wc0 = np.clip(cq - NA_WIN_W // 2, 0, GRID_W - NA_WIN_W)
    col_ok = (ck >= wc0) & (ck < wc0 + NA_WIN_W)
    col_pick = np.clip(ck - cq + NA_WIN_W - 1, 0, n_dc - 1)[..., None] == np.arange(n_dc)
    by_col = jnp.einsum("hab,qcb->haqc", rpb.astype(F32), jnp.asarray(col_pick, F32), precision=HIGHEST)
    margin = 4
    by_col = jnp.pad(by_col.transpose(0, 2, 1, 3), ((0, 0), (0, 0), (margin, margin), (0, 0)))
    by_col = by_col.reshape(HEADS, GRID_W, (n_dr + 2 * margin) * GRID_W)
    pieces, row_ok = [], []
    for tile in (0, 1, GRID_H // 4 - 1):
        slab0 = int(np.clip(tile - 1, 0, GRID_H // 4 - NA_SLAB_ROWS // 4)) * 4
        rq = tile * 4 + np.arange(4)
        rk = (slab0 + np.arange(NA_SLAB) // GRID_W)[None, :]
        wr0 = np.clip(rq - NA_WIN_H // 2, 0, GRID_H - NA_WIN_H)[:, None]
        row_ok.append((rk >= wr0) & (rk < wr0 + NA_WIN_H))
        for r in rq:
            first = slab0 - int(r) + NA_WIN_H - 1 + margin
            assert 0 <= first and first + NA_SLAB_ROWS <= n_dr + 2 * margin
            pieces.append(by_col[:, :, first * GRID_W:first * GRID_W + NA_SLAB])
    table = jnp.stack(pieces).reshape(3, 4, HEADS, GRID_W, NA_SLAB).transpose(0, 2, 1, 3, 4)
    valid = np.stack(row_ok)[:, None, :, None, :] & np.tile(col_ok, (1, NA_SLAB_ROWS))[None, None, None]
    table = jnp.where(jnp.asarray(valid), table, NEG_BIG)
    return table.reshape(3, HEADS, TM, NA_SLAB)


def na_latent(p, kc, vc, layer, bias):
    n_t = LAT_TILES_PER_SEQ
    seq_blk0 = TP // DEC_SEQ

    def bias_idx(b, i):
        return (jnp.minimum(i, 1) + i // (n_t - 1), 0, 0, 0)

    return Part(
        kernel=_na_kernel,
        in_specs=[
            pl.BlockSpec((TM, MIX_BLK), lambda b, i: (CTX_TILES + b * n_t + i, M_CQ)),
            pl.BlockSpec((DEC_SEQ, MIX_BLK), lambda b, i: (seq_blk0 + b, M_CK)),
            pl.BlockSpec((DEC_SEQ, MIX_BLK), lambda b, i: (seq_blk0 + b, M_CV)),
            _cache_block_spec(layer), _cache_block_spec(layer),
            pl.BlockSpec((1, HEADS, TM, NA_SLAB), bias_idx),
        ],
        out_specs=[pl.BlockSpec((TM, MIX_BLK), lambda b, i: (b * n_t + i, 0))],
        out_shapes=[jax.ShapeDtypeStruct((TL, MIX_BLK), F32)],
        scratch=[],
        args=[p, p, p, kc, vc, bias])


MAX_EXPONENT = 80.0


def _hgrn_direction(q_ref, f_ref, v_ref, lb, st_ref, o_ref, reverse):
    n_ch = TM // B_CHUNK
    r_idx = lax.broadcasted_iota(jnp.int32, (TM, TM), 0)
    c_idx = lax.broadcasted_iota(jnp.int32, (TM, TM), 1)
    tri = (c_idx >= r_idx) if reverse else (c_idx <= r_idx)
    zq = q_ref[...]
    q = zq * jax.nn.sigmoid(zq)
    z = f_ref[...]
    gate = (1.0 - lb) * jax.nn.sigmoid(z)
    logf = jnp.log(lb + gate)
    kk = (1.0 - lb) - gate
    b = _select_sum_left(jnp.where(tri, 1.0, 0.0).astype(BF16), logf)
    b3 = b.reshape(n_ch, B_CHUNK, MIX_BLK)
    mid = B_CHUNK // 2 if reverse else B_CHUNK // 2 - 1
    q_in = (q.reshape(b3.shape) * jnp.exp(b3 - b3[:, mid:mid + 1, :])).reshape(TM, MIX_BLK)
    q_dec = (q * jnp.exp(b)).astype(BF16)
    b_t = b.T
    kk_t = kk.T
    far = 0 if reverse else TM - 1
    b_far = b_t[:, far:far + 1]
    k_dec_t = (kk_t * jnp.exp(b_far - b_t)).astype(BF16)
    vb = v_ref[...].astype(BF16)
    st = st_ref[...]
    o_state = jnp.dot(q_dec, st.astype(BF16), preferred_element_type=F32)
    kv = jnp.dot(k_dec_t, vb, preferred_element_type=F32)
    st_ref[...] = st * jnp.exp(b_far) + jnp.where(_same_head_matrix(), kv, 0.0)
    lane = _head_lanes()
    token = lax.broadcasted_iota(jnp.int32, (1, TM), 1)
    local = lax.broadcasted_iota(jnp.int32, (HEADS * B_CHUNK, 1), 0) % B_CHUNK
    heads = [_lane_range(lane, h * HEAD_DIM, HEAD_DIM) for h in range(HEADS)]
    weights = []
    for c in range(n_ch):
        ref = b_t[:, c * B_CHUNK + mid:c * B_CHUNK + mid + 1]
        k_c_t = (kk_t * jnp.exp(jnp.minimum(ref - b_t, MAX_EXPONENT))).astype(BF16)
        q_c = q_in[c * B_CHUNK:(c + 1) * B_CHUNK, :]
        lhs = jnp.concatenate([jnp.where(in_head, q_c, 0.0) for in_head in heads], axis=0)
        a = jnp.dot(lhs.astype(BF16), k_c_t, preferred_element_type=F32)
        t_abs = c * B_CHUNK + local
        weights.append(jnp.where((token >= t_abs) if reverse else (token <= t_abs), a, 0.0).astype(BF16))
    res = jnp.dot(jnp.concatenate(weights, axis=0), vb, preferred_element_type=F32)
    for c in range(n_ch):
        rows = slice(c * B_CHUNK, (c + 1) * B_CHUNK)
        o_c = o_state[rows, :]
        for h, in_head in enumerate(heads):
            lo = (c * HEADS + h) * B_CHUNK
            o_c = o_c + jnp.where(in_head, res[lo:lo + B_CHUNK, :], 0.0)
        o_ref[rows, :] = o_c


def _hgrn_kernel(qf_ref, ff_ref, vf_ref, qb_ref, fb_ref, vb_ref, lb_ref, s0_ref,
                 of_ref, ob_ref, s_ref, stf_ref, stb_ref, *, has_s0):
    j = pl.program_id(1)

    @pl.when(j == 0)
    def _():
        if has_s0:
            stf_ref[...] = s0_ref[0, 0]
            stb_ref[...] = s0_ref[0, 1]
        else:
            stf_ref[...] = jnp.zeros((MIX_BLK, MIX_BLK), F32)
            stb_ref[...] = jnp.zeros((MIX_BLK, MIX_BLK), F32)

    lb = lb_ref[...]
    _hgrn_direction(qf_ref, ff_ref, vf_ref, lb[0:1], stf_ref, of_ref, False)
    _hgrn_direction(qb_ref, fb_ref, vb_ref, lb[1:2], stb_ref, ob_ref, True)

    @pl.when(j == pl.num_programs(1) - 1)
    def _():
        for d, st_ref in enumerate((stf_ref, stb_ref)):
            s = st_ref[...]
            for hd in range(HEADS):
                lo = hd * HEAD_DIM
                s_ref[0, d, hd] = s[lo:lo + HEAD_DIM, lo:lo + HEAD_DIM]


def hgrn(p, row_tile0, n_seq, seq_len, lb, s0):
    nb = seq_len // TM
    has_s0 = s0 is not None
    if s0 is None:
        s0 = jnp.zeros((1, 2, MIX_BLK, MIX_BLK), F32)

    def fwd(col):
        return pl.BlockSpec((TM, MIX_BLK), lambda s, j: (row_tile0 + s * nb + j, col))

    def bwd(col):
        return pl.BlockSpec((TM, MIX_BLK), lambda s, j: (row_tile0 + s * nb + nb - 1 - j, col))

    state_spec = pl.BlockSpec((1, 2, MIX_BLK, MIX_BLK), lambda s, j: (s if has_s0 else 0, 0, 0, 0))
    out_rows = n_seq * seq_len
    return Part(
        kernel=functools.partial(_hgrn_kernel, has_s0=has_s0),
        in_specs=[fwd(G_BQ), fwd(G_BFF), fwd(G_BV), bwd(G_BQ), bwd(G_BFB), bwd(G_BV),
                  pl.BlockSpec((2, MIX_BLK), lambda s, j: (0, 0)), state_spec],
        out_specs=[
            pl.BlockSpec((TM, MIX_BLK), lambda s, j: (s * nb + j, 0)),
            pl.BlockSpec((TM, MIX_BLK), lambda s, j: (s * nb + nb - 1 - j, 0)),
            pl.BlockSpec((1, 2, HEADS, HEAD_DIM, HEAD_DIM), lambda s, j: (s, 0, 0, 0, 0)),
        ],
        out_shapes=[jax.ShapeDtypeStruct((out_rows, MIX_BLK), F32),
                    jax.ShapeDtypeStruct((out_rows, MIX_BLK), F32),
                    jax.ShapeDtypeStruct((n_seq, 2, HEADS, HEAD_DIM, HEAD_DIM), F32)],
        scratch=[pltpu.VMEM((MIX_BLK, MIX_BLK), F32), pltpu.VMEM((MIX_BLK, MIX_BLK), F32)],
        args=[p, p, p, p, p, p, lb, s0])


def _state_to_blockdiag(s):
    eye = jnp.eye(HEADS, dtype=F32)
    full = s.astype(F32)[:, :, :, :, None, :] * eye[None, None, :, None, :, None]
    return full.reshape(s.shape[0], 2, MIX_BLK, MIX_BLK)


def _fft_kernel(u_ref, c64_ref, s64_ref, cl_ref, sl_ref, o_ref, a_ref, b_ref, *, norm):
    @pl.when(pl.program_id(1) == 0)
    def _():
        u = u_ref[...]
        a_ref[...] = jnp.dot(u, c64_ref[...], preferred_element_type=F32).astype(BF16)
        b_ref[...] = jnp.dot(u, s64_ref[...], preferred_element_type=F32).astype(BF16)

    o_ref[...] = (jnp.dot(cl_ref[...], a_ref[...], preferred_element_type=F32)
                  - jnp.dot(sl_ref[...], b_ref[...], preferred_element_type=F32)) * norm


def _dft_tables(n):
    k = np.arange(n)
    ang = 2.0 * np.pi * ((k[:, None] * k[None, :]) % n) / n
    return np.cos(ang), np.sin(ang)


def _dft_constants(seq_len):
    c64, s64 = _dft_tables(HEAD_DIM)
    eye = np.eye(HEADS)
    cl, sl = _dft_tables(seq_len)
    as_bf16 = lambda a: jnp.asarray(a, F32).astype(BF16)
    return as_bf16(np.kron(eye, c64)), as_bf16(np.kron(eye, s64)), as_bf16(cl), as_bf16(sl)


def fourier_mix(p, row_blk0, n_seq, seq_len, consts):
    c64, s64, cl, sl = consts
    nb = seq_len // TM
    norm = 1.0 / math.sqrt(seq_len * HEAD_DIM)
    return Part(
        kernel=functools.partial(_fft_kernel, norm=norm),
        in_specs=[
            pl.BlockSpec((seq_len, MIX_BLK), lambda s, i: (row_blk0 + s, M_DU)),
            pl.BlockSpec((MIX_BLK, MIX_BLK), lambda s, i: (0, 0)),
            pl.BlockSpec((MIX_BLK, MIX_BLK), lambda s, i: (0, 0)),
            pl.BlockSpec((TM, seq_len), lambda s, i: (i, 0)),
            pl.BlockSpec((TM, seq_len), lambda s, i: (i, 0)),
        ],
        out_specs=[pl.BlockSpec((TM, MIX_BLK), lambda s, i: (s * nb + i, 0))],
        out_shapes=[jax.ShapeDtypeStruct((n_seq * seq_len, MIX_BLK), F32)],
        scratch=[pltpu.VMEM((seq_len, MIX_BLK), BF16), pltpu.VMEM((seq_len, MIX_BLK), BF16)],
        args=[p, c64, s64, cl, sl])


def _route(logits_t, rb):
    per = N_EXPERTS // N_GROUPS
    score = [jax.nn.sigmoid(logits_t[e:e + 1, :]) for e in range(N_EXPERTS)]
    sel = [score[e] + rb[e:e + 1, :] for e in range(N_EXPERTS)]
    gscore = []
    for g in range(N_GROUPS):
        vals = sel[g * per:(g + 1) * per]
        best = None
        for a in range(per):
            for b in range(a + 1, per):
                pair = vals[a] + vals[b]
                best = pair if best is None else jnp.maximum(best, pair)
        gscore.append(best)
    chosen = []
    for g in range(N_GROUPS):
        ok = None
        for j in range(N_GROUPS):
            if j == g:
                continue
            cond = gscore[g] > gscore[j] if j < g else gscore[g] >= gscore[j]
            ok = cond if ok is None else ok & cond
        chosen.append(ok)
    picked = []
    for e in range(N_EXPERTS):
        g = e // per
        rank = jnp.zeros_like(sel[e])
        for j in range(g * per, (g + 1) * per):
            if j == e:
                continue
            ahead = sel[j] >= sel[e] if j < e else sel[j] > sel[e]
            rank = rank + jnp.where(ahead, 1.0, 0.0)
        picked.append(chosen[g] & (rank < 2.0))
    wsum = jnp.zeros_like(score[0])
    for e in range(N_EXPERTS):
        wsum = wsum + jnp.where(picked[e], score[e], 0.0)
    bucket = jnp.zeros_like(wsum)
    w_a = jnp.zeros_like(wsum)
    w_b = jnp.zeros_like(wsum)
    for g in range(N_GROUPS):
        for n, (a, b) in enumerate(EXPERT_PAIRS):
            hit = picked[g * per + a] & picked[g * per + b]
            bucket = jnp.where(hit, float(g * len(EXPERT_PAIRS) + n), bucket)
            w_a = jnp.where(hit, score[g * per + a] / wsum, w_a)
            w_b = jnp.where(hit, score[g * per + b] / wsum, w_b)
    return bucket, w_a, w_b


def _out_kernel(*refs):
    streams, rest = refs[:12], refs[12:]
    (bg_ref, mod_ref, hg_ref, w_ref, g2_ref, rw_ref, rb_ref,
     x1_ref, h2_ref, bucket_ref, rank_ref, counts_ref, run_ref) = rest
    is_ctx = pl.program_id(0) < TP // TM_OUT
    x, o_a, o_f, o_b, o_c, o_d = (jnp.where(is_ctx, streams[2 * n][...], streams[2 * n + 1][...])
                                  for n in range(6))

    @pl.when(pl.program_id(0) == 0)
    def _():
        run_ref[...] = jnp.zeros(run_ref.shape, F32)

    mod = mod_ref[0]
    hb = o_f + o_b
    zg = bg_ref[...]
    hb = hb * lax.rsqrt(_head_mean_square(hb) + EPS) * hg_ref[...] * (zg * jax.nn.sigmoid(zg))
    parts = (o_a, hb, o_c, o_d)
    mixed = jnp.zeros((TM_OUT, D_MODEL), F32)
    for n, part in enumerate(parts):
        mixed = mixed + jnp.dot(part.astype(BF16), w_ref[0, n * MIX_BLK:(n + 1) * MIX_BLK, :],
                                preferred_element_type=F32)
    x1 = x + mod[2:3] * mixed
    x1_ref[...] = x1
    ms = jnp.mean(x1 * x1, axis=-1, keepdims=True)
    h2 = x1 * lax.rsqrt(ms + EPS) * g2_ref[...] * (1.0 + mod[4:5]) + mod[3:4]
    rw = rw_ref[...]
    r = jnp.dot(jnp.concatenate(_bf16_pieces(h2, 2), axis=0), rw, preferred_element_type=F32)
    r = r[:TM_OUT] + r[TM_OUT:]
    bucket, w_a, w_b = _route((r[:, :LANES] + r[:, LANES:]).T, rb_ref[...])
    h2_ref[:, :D_MODEL] = h2
    h2_ref[:, D_MODEL:] = jnp.concatenate([w_a, w_b, jnp.zeros((LANES - 2, TM_OUT), F32)], axis=0).T
    onehot = jnp.where(lax.broadcasted_iota(jnp.int32, (BUCKET_ROWS, 1), 0).astype(F32) == bucket, 1.0, 0.0)
    s_idx = lax.broadcasted_iota(jnp.int32, (TM_OUT, TM_OUT), 0)
    t_idx = lax.broadcasted_iota(jnp.int32, (TM_OUT, TM_OUT), 1)
    prefix = jnp.dot(onehot.astype(BF16), jnp.where(s_idx <= t_idx, 1.0, 0.0).astype(BF16),
                     preferred_element_type=F32)
    run = run_ref[...]
    rank = jnp.sum(onehot * (prefix - 1.0 + run[:, 0:1]), axis=0, keepdims=True)
    run = run + jnp.sum(onehot, axis=1, keepdims=True)
    run_ref[...] = run
    bucket_ref[...] = bucket.astype(jnp.int32)
    rank_ref[...] = rank.astype(jnp.int32)
    counts_ref[...] = run


def out_and_route(x_pair, x_is_combined, mixer_pairs, p, mod, layer, hgrn_g, w_out_bf16, norm2_g, router_pieces,
                  router_b):
    n_ctx = TP // TM_OUT
    ctx_tile = lambda i: (jnp.minimum(i, n_ctx - 1), 0)
    lat_tile = lambda i: (jnp.maximum(i - n_ctx, 0), 0)
    mod_row = lambda i: jnp.where(i < n_ctx, 0, 1 + (i - n_ctx) // (DEC_SEQ // TM_OUT))
    tile = lambda w: pl.BlockSpec((TM_OUT, w), lambda i: (i, 0))
    full = lambda r, c: pl.BlockSpec((r, c), lambda i: (0, 0))
    stream_specs = [pl.BlockSpec((TM_OUT, D_MODEL), ctx_tile),
                    pl.BlockSpec((TM_OUT, D_MODEL),
                                 (lambda i: (jnp.maximum(i, n_ctx), 0)) if x_is_combined else lat_tile)]
    stream_args = list(x_pair)
    for o_ctx, o_lat in mixer_pairs:
        stream_specs += [pl.BlockSpec((TM_OUT, MIX_BLK), ctx_tile), pl.BlockSpec((TM_OUT, MIX_BLK), lat_tile)]
        stream_args += [o_ctx, o_lat]
    return pl.pallas_call(
        _out_kernel,
        grid=(T // TM_OUT,),
        in_specs=stream_specs + [
            pl.BlockSpec((TM_OUT, MIX_BLK), lambda i: (i, G_BG)),
            pl.BlockSpec((1, 6, D_MODEL), lambda i: (mod_row(i), 0, 0)),
            full(1, MIX_BLK), pl.BlockSpec((1, D_MODEL, D_MODEL), lambda i: (layer, 0, 0)), full(1, D_MODEL),
            full(D_MODEL, 2 * LANES), full(N_EXPERTS, 1),
        ],
        out_specs=[tile(D_MODEL), tile(ROW_W), pl.BlockSpec((1, TM_OUT), lambda i: (0, i)),
                   pl.BlockSpec((1, TM_OUT), lambda i: (0, i)), full(BUCKET_ROWS, LANES)],
        out_shape=[jax.ShapeDtypeStruct((T, D_MODEL), F32),
                   jax.ShapeDtypeStruct((T, ROW_W), F32),
                   jax.ShapeDtypeStruct((1, T), jnp.int32),
                   jax.ShapeDtypeStruct((1, T), jnp.int32),
                   jax.ShapeDtypeStruct((BUCKET_ROWS, LANES), F32)],
        scratch_shapes=[pltpu.VMEM((BUCKET_ROWS, LANES), F32)],
        compiler_params=_cparams(("arbitrary",)),
        name="out_and_route",
    )(*stream_args, p, mod, jnp.tile(hgrn_g, HEADS)[None, :], w_out_bf16,
      norm2_g[None, :], router_pieces, router_b[:, None])


def _router_pieces(router_w):
    hi, lo = _bf16_pieces(router_w.astype(F32), 2)
    pad = lambda a: jnp.pad(a, ((0, 0), (0, LANES - N_EXPERTS)))
    return jnp.concatenate([pad(hi), pad(lo)], axis=1)


def routing_plan(bucket, rank, counts):
    counts = counts[:N_BUCKETS, 0].astype(jnp.int32)
    n_tiles = (counts + TM_MOE - 1) // TM_MOE
    tile_end = jnp.cumsum(n_tiles)
    tile_start = tile_end - n_tiles
    buckets = jnp.arange(N_BUCKETS, dtype=jnp.int32)
    start_of_token = jnp.sum(jnp.where(bucket[0][:, None] == buckets[None, :], tile_start[None, :], 0), axis=1)
    dest = start_of_token * TM_MOE + rank[0]
    tiles = jnp.arange(MAX_TILES, dtype=jnp.int32)
    valid = tiles < tile_end[-1]
    tile_bucket = jnp.sum((jnp.minimum(tiles, tile_end[-1] - 1)[:, None] >= tile_end[None, :]).astype(jnp.int32), axis=1)
    pair_a = np.array([a for a, _ in EXPERT_PAIRS], np.int32)
    pair_b = np.array([b for _, b in EXPERT_PAIRS], np.int32)
    per = N_EXPERTS // N_GROUPS
    exp_a = jnp.asarray((np.arange(N_BUCKETS) // len(EXPERT_PAIRS)) * per + np.tile(pair_a, N_GROUPS), jnp.int32)
    exp_b = jnp.asarray((np.arange(N_BUCKETS) // len(EXPERT_PAIRS)) * per + np.tile(pair_b, N_GROUPS), jnp.int32)
    pick = tile_bucket[:, None] == buckets[None, :]
    tile_a = jnp.sum(jnp.where(pick, exp_a[None, :], 0), axis=1)
    tile_b = jnp.sum(jnp.where(pick, exp_b[None, :], 0), axis=1)
    return dest.astype(jnp.int32), tile_a, tile_b, valid.astype(jnp.int32), (tile_end[-1:] - 1).astype(jnp.int32)


def _row_copy(src, src_row, dst, dst_row, sem):
    return pltpu.make_async_copy(src.at[pl.ds(src_row, 1), :], dst.at[pl.ds(dst_row, 1), :], sem)


def _scatter_kernel(dest_ref, h_ref, init_ref, o_ref, sem):
    del init_ref
    base = pl.program_id(0) * TM

    for r in range(TM):
        _row_copy(h_ref, r, o_ref, dest_ref[base + r], sem).start(priority=r % N_DMA_PRIORITIES)
    pltpu.make_async_copy(h_ref, o_ref.at[pl.ds(0, TM), :], sem).wait()


def scatter_to_slots(h2, dest, slots):
    return pl.pallas_call(
        _scatter_kernel,
        grid_spec=pltpu.PrefetchScalarGridSpec(
            num_scalar_prefetch=1,
            grid=(N_TILES,),
            in_specs=[pl.BlockSpec((TM, ROW_W), lambda i, d: (i, 0)),
                      pl.BlockSpec(memory_space=pl.ANY)],
            out_specs=pl.BlockSpec(memory_space=pl.ANY),
            scratch_shapes=[pltpu.SemaphoreType.DMA(())],
        ),
        out_shape=jax.ShapeDtypeStruct((N_SLOTS, ROW_W), F32),
        input_output_aliases={2: 0},
        compiler_params=_cparams(("arbitrary",)),
        name="scatter_to_slots",
    )(dest, h2, slots)


def _moe_kernel(ta_ref, tb_ref, valid_ref, last_ref, h_ref, wga_ref, wua_ref, wda_ref, wgb_ref, wub_ref, wdb_ref,
                o_ref):
    del ta_ref, tb_ref, last_ref
    i = pl.program_id(0)

    @pl.when(valid_ref[i] == 1)
    def _():
        x = h_ref[:, :D_MODEL].astype(BF16)
        gates = h_ref[:, D_MODEL:]
        y = jnp.zeros((TM_MOE, D_MODEL), F32)
        for n, (wg, wu, wd) in enumerate(((wga_ref, wua_ref, wda_ref), (wgb_ref, wub_ref, wdb_ref))):
            a = jnp.dot(x, wg[0, 0].astype(BF16), preferred_element_type=F32)
            u = jnp.dot(x, wu[0, 0].astype(BF16), preferred_element_type=F32)
            z = a * jax.nn.sigmoid(a) * u * gates[:, n:n + 1]
            y = y + jnp.dot(z.astype(BF16), wd[0, 0].astype(BF16), preferred_element_type=F32)
        o_ref[:, 0, :] = y

    @pl.when(valid_ref[i] == 0)
    def _():
        o_ref[...] = jnp.zeros((TM_MOE, 1, D_MODEL), F32)


def moe(h_slots, tile_a, tile_b, valid, last, layer, wg, wu, wd):
    up_a = pl.BlockSpec((1, 1, D_MODEL, D_EXPERT), lambda i, ta, tb, v, last: (layer, ta[i], 0, 0))
    up_b = pl.BlockSpec((1, 1, D_MODEL, D_EXPERT), lambda i, ta, tb, v, last: (layer, tb[i], 0, 0))
    down_a = pl.BlockSpec((1, 1, D_EXPERT, D_MODEL), lambda i, ta, tb, v, last: (layer, ta[i], 0, 0))
    down_b = pl.BlockSpec((1, 1, D_EXPERT, D_MODEL), lambda i, ta, tb, v, last: (layer, tb[i], 0, 0))
    return pl.pallas_call(
        _moe_kernel,
        grid_spec=pltpu.PrefetchScalarGridSpec(
            num_scalar_prefetch=4,
            grid=(MAX_TILES,),
            in_specs=[pl.BlockSpec((TM_MOE, ROW_W), lambda i, ta, tb, v, last: (jnp.minimum(i, last[0]), 0)),
                      up_a, up_a, down_a, up_b, up_b, down_b],
            out_specs=pl.BlockSpec((TM_MOE, 1, D_MODEL), lambda i, ta, tb, v, last: (i, 0, 0)),
        ),
        out_shape=jax.ShapeDtypeStruct((N_SLOTS, 1, D_MODEL), F32),
        compiler_params=_cparams(("arbitrary",)),
        name="moe",
    )(tile_a, tile_b, valid, last, h_slots, wg, wu, wd, wg, wu, wd)


def _gather_tile(dest_ref, y_ref, buf_ref, sem, tile, slot):
    for r in range(TM):
        pltpu.make_async_copy(y_ref.at[pl.ds(dest_ref[tile * TM + r], 1)],
                              buf_ref.at[slot, pl.ds(r, 1)], sem.at[slot]).start(priority=r % N_DMA_PRIORITIES)


def _moe_residual(dest_ref, y_ref, buf_ref, sem, x1_ref, mod_ref):
    i = pl.program_id(0)
    slot = i % 2

    @pl.when(i == 0)
    def _():
        _gather_tile(dest_ref, y_ref, buf_ref, sem, 0, 0)

    @pl.when(i + 1 < pl.num_programs(0))
    def _():
        _gather_tile(dest_ref, y_ref, buf_ref, sem, i + 1, 1 - slot)

    pltpu.make_async_copy(y_ref.at[pl.ds(0, TM)], buf_ref.at[slot], sem.at[slot]).wait()
    return x1_ref[...] + mod_ref[0][5:6] * buf_ref[slot, :, 0, :]


def _final_kernel(dest_ref, y_ref, x1_ref, mod_ref, g_ref, oc_ref, ol_ref, buf_ref, sem):
    x2 = _moe_residual(dest_ref, y_ref, buf_ref, sem, x1_ref, mod_ref)
    ms = jnp.mean(x2 * x2, axis=-1, keepdims=True)
    y = x2 * lax.rsqrt(ms + EPS) * g_ref[...]

    @pl.when(_is_ctx_tile())
    def _():
        oc_ref[...] = y

    @pl.when(jnp.logical_not(_is_ctx_tile()))
    def _():
        ol_ref[...] = y


_GATHER_SCRATCH = [pltpu.VMEM((2, TM, 1, D_MODEL), F32), pltpu.SemaphoreType.DMA((2,))]


def final_norm(dest, y_slots, x1, mod, final_g):
    return pl.pallas_call(
        _final_kernel,
        grid_spec=pltpu.PrefetchScalarGridSpec(
            num_scalar_prefetch=1,
            grid=(N_TILES,),
            in_specs=[pl.BlockSpec(memory_space=pl.ANY),
                      pl.BlockSpec((TM, D_MODEL), lambda i, d: (i, 0)),
                      pl.BlockSpec((1, 6, D_MODEL), lambda i, d: (_mod_row(i), 0, 0)),
                      pl.BlockSpec((1, D_MODEL), lambda i, d: (0, 0))],
            out_specs=[pl.BlockSpec((TM, D_MODEL), _ctx_tile), pl.BlockSpec((TM, D_MODEL), _lat_tile)],
            scratch_shapes=_GATHER_SCRATCH,
        ),
        out_shape=[jax.ShapeDtypeStruct((TP, D_MODEL), F32), jax.ShapeDtypeStruct((TL, D_MODEL), F32)],
        compiler_params=_cparams(("arbitrary",)),
        name="final_norm",
    )(dest, y_slots, x1, mod, final_g[None, :])


def kernel(x_prompt, x_sample, cache_diff_k, cache_diff_v, cache_na_k, cache_na_v, state_hgrn, c, c_ctx,
           norm1_g, norm2_g, ada_w, ada_b, w_in, w_out, diff_lambda, diff_subln_g, hgrn_lb_logits,
           hgrn_norm_g, na_rpb, router_w, router_b, moe_w_gate, moe_w_up, moe_w_down, final_norm_g):
    assert SEQ == TM and PAST_LEN == TM and DEC_SEQ % TM_OUT == 0 and TP % DEC_SEQ == 0
    x_pair = (x_prompt.reshape(TP, D_MODEL), x_sample.reshape(TL, D_MODEL))
    w_in_bf16 = jnp.concatenate([w_in[:, :, c * MIX_BLK:(c + 1) * MIX_BLK] for c in PM_BLOCKS + PG_BLOCKS],
                                axis=-1).astype(BF16)
    w_out_bf16 = w_out.astype(BF16)
    router_pieces = _router_pieces(router_w)
    mods = modulation(jnp.concatenate([c_ctx[None, :], c], axis=0), ada_w, ada_b)
    mods = mods.reshape(DEPTH, 3, 6, D_MODEL)
    lb_sm = jax.nn.softmax(hgrn_lb_logits.astype(F32), axis=0)
    lb_all = jnp.cumsum(lb_sm, axis=0) - lb_sm[0:1]
    rope = _rope_tables()
    dft_ctx = _dft_constants(SEQ)
    dft_lat = _dft_constants(DEC_SEQ)
    lat_blk0 = TP // DEC_SEQ
    states = []
    moe_state = None
    for l in range(DEPTH):
        if moe_state is None:
            pm, pg, *new_kv = projection(*x_pair, mods[l], norm1_g[l], w_in_bf16, rope)
        else:
            x, pm, pg, *new_kv = projection_after_moe(*moe_state, mods[l - 1], mods[l], l, norm1_g[l], w_in_bf16, rope,
                                                 new_kv)
            x_pair = (x, x)

        lq = diff_lambda[l].astype(F32)
        lam_init = 0.8 - 0.6 * math.exp(-0.3 * l)
        lam = (jnp.exp(jnp.sum(lq[0] * lq[1])) - jnp.exp(jnp.sum(lq[2] * lq[3])) + lam_init).reshape(1)
        subln = jnp.tile(diff_subln_g[l], HEADS)[None, :]
        diff = functools.partial(attention, pm, cols=(M_AQ, M_AK, M_AV), lam=lam, norm_g=subln, n_maps=2,
                                 post_scale=1.0 - lam_init)
        (oa_ctx,), (of_ctx, ob_ctx, st_ctx), (oc_ctx,), (od_ctx,) = run_parts(
            [diff(row_blk0=0, n_seq=BATCH, seq_len=SEQ),
             hgrn(pg, 0, BATCH, SEQ, lb_all[l], None),
             attention(pm, 0, (M_CQ, M_CK, M_CV), BATCH, SEQ, lam, subln, n_maps=1, post_scale=1.0),
             fourier_mix(pm, 0, BATCH, SEQ, dft_ctx)],
            (BATCH, SEQ // TM), "context_mixers")
        (oa_lat,), (of_lat, ob_lat, _), (oc_lat,), (od_lat,) = run_parts(
            [diff(row_blk0=lat_blk0, n_seq=DEC_BATCH, seq_len=DEC_SEQ, cache=(cache_diff_k, cache_diff_v), layer=l),
             hgrn(pg, CTX_TILES, DEC_BATCH, DEC_SEQ, lb_all[l], _state_to_blockdiag(state_hgrn[:, l])),
             na_latent(pm, cache_na_k, cache_na_v, l, _na_bias_tables(na_rpb[l])),
             fourier_mix(pm, lat_blk0, DEC_BATCH, DEC_SEQ, dft_lat)],
            (DEC_BATCH, DEC_SEQ // TM), "latent_mixers")

        x1, h2, bucket, rank, counts = out_and_route(
            x_pair, l > 0, ((oa_ctx, oa_lat), (of_ctx, of_lat), (ob_ctx, ob_lat), (oc_ctx, oc_lat),
                            (od_ctx, od_lat)),
            pg, mods[l], l, hgrn_norm_g[l], w_out_bf16, norm2_g[l], router_pieces, router_b)
        dest, *tile_plan = routing_plan(bucket, rank, counts)
        h_slots = scatter_to_slots(h2, dest, jnp.zeros((N_SLOTS, ROW_W), F32) if l == 0 else h_slots)
        y_slots = moe(h_slots, *tile_plan, l, moe_w_gate, moe_w_up, moe_w_down)
        moe_state = (dest, y_slots, x1)

        states.append(st_ctx)
    y_prompt, y_sample = final_norm(*moe_state, mods[DEPTH - 1], final_norm_g)
    return (y_prompt.reshape(BATCH, SEQ, D_MODEL), y_sample.reshape(DEC_BATCH, DEC_SEQ, D_MODEL),
            *new_kv, jnp.stack(states, axis=1))
```

```python
import functools
import math
from typing import Any, NamedTuple

import numpy as np
import jax
import jax.numpy as jnp
from jax import lax
from jax.experimental import pallas as pl
from jax.experimental.pallas import tpu as pltpu

F32 = jnp.float32
BF16 = jnp.bfloat16
HIGHEST = lax.Precision.HIGHEST

D_MODEL = 1024
BATCH = 16
SEQ = 256
DEPTH = 2
DEC_BATCH = 2
DEC_SEQ = 2048
PAST_LEN = 256
GRID_W = 64
GRID_H = DEC_SEQ // GRID_W
EPS = 1e-6
NEG_BIG = -1e30
HEADS = 4
HEAD_DIM = 64
MIX_BLK = HEADS * HEAD_DIM
A_DIM = 32
ROPE_BASE = 10000.0
B_CHUNK = 32
NA_WIN_H = 8
NA_WIN_W = 16
N_EXPERTS = 16
N_GROUPS = 4
D_EXPERT = 512
PROJ_W = 12 * MIX_BLK
TP = BATCH * SEQ
TL = DEC_BATCH * DEC_SEQ
T = TP + TL
TM = 256
N_TILES = T // TM
CTX_TILES = TP // TM
LAT_TILES_PER_SEQ = DEC_SEQ // TM
(C_AQ, C_AK, C_AV, C_BQ, C_BFF, C_BFB, C_BV, C_BG, C_CQ, C_CK, C_CV, C_DU) = range(12)
PM_BLOCKS = (C_AQ, C_AK, C_AV, C_CQ, C_CK, C_CV, C_DU)
PG_BLOCKS = (C_BQ, C_BFF, C_BFB, C_BV, C_BG)
(M_AQ, M_AK, M_AV, M_CQ, M_CK, M_CV, M_DU) = range(len(PM_BLOCKS))
(G_BQ, G_BFF, G_BFB, G_BV, G_BG) = range(len(PG_BLOCKS))
PM_W = len(PM_BLOCKS) * MIX_BLK
PG_W = len(PG_BLOCKS) * MIX_BLK
NA_SLAB_ROWS = 12
NA_SLAB = NA_SLAB_ROWS * GRID_W
LANES = 128
ROW_W = D_MODEL + LANES
EXPERT_PAIRS = ((0, 1), (0, 2), (0, 3), (1, 3), (2, 3), (2, 1))
N_BUCKETS = N_GROUPS * len(EXPERT_PAIRS)
BUCKET_ROWS = 32
N_DMA_PRIORITIES = 2
TM_OUT = 512
TM_MOE = 384
MAX_TILES = T // TM_MOE + N_BUCKETS
N_SLOTS = MAX_TILES * TM_MOE
VMEM_LIMIT = 56 * 1024 * 1024


def _cparams(sem):
    return pltpu.CompilerParams(dimension_semantics=sem, vmem_limit_bytes=VMEM_LIMIT)


class Part(NamedTuple):
    kernel: Any
    in_specs: list
    args: list
    out_specs: list
    out_shapes: list
    scratch: list


def _run_parts_kernel(*refs, layout):
    n_in = sum(n for _, n, _, _ in layout)
    n_out = sum(n for _, _, n, _ in layout)
    ins, outs, scratch = refs[:n_in], refs[n_in:n_in + n_out], refs[n_in + n_out:]
    i = o = s = 0
    for kernel, k_in, k_out, k_scratch in layout:
        kernel(*ins[i:i + k_in], *outs[o:o + k_out], *scratch[s:s + k_scratch])
        i, o, s = i + k_in, o + k_out, s + k_scratch


def run_parts(parts, grid, name):
    layout = tuple((p.kernel, len(p.in_specs), len(p.out_specs), len(p.scratch)) for p in parts)
    outs = pl.pallas_call(
        functools.partial(_run_parts_kernel, layout=layout),
        grid=grid,
        in_specs=[s for p in parts for s in p.in_specs],
        out_specs=[s for p in parts for s in p.out_specs],
        out_shape=[s for p in parts for s in p.out_shapes],
        scratch_shapes=[s for p in parts for s in p.scratch],
        compiler_params=_cparams(("arbitrary", "arbitrary")),
        name=name,
    )(*[a for p in parts for a in p.args])
    result, o = [], 0
    for p in parts:
        result.append(outs[o:o + len(p.out_specs)])
        o += len(p.out_specs)
    return result


def _head_lanes(width=MIX_BLK):
    return lax.broadcasted_iota(jnp.int32, (1, width), 1)


def _lane_range(lane, lo, n):
    return (lane >= lo) & (lane < lo + n)


def _same_head_matrix():
    r = lax.broadcasted_iota(jnp.int32, (MIX_BLK, MIX_BLK), 0) // HEAD_DIM
    c = lax.broadcasted_iota(jnp.int32, (MIX_BLK, MIX_BLK), 1) // HEAD_DIM
    return r == c


def _bf16_pieces(x, n):
    pieces = []
    for _ in range(n):
        piece = x.astype(BF16)
        pieces.append(piece)
        x = x - piece.astype(F32)
    return pieces


def _select_sum_left(onehot_bf16, x):
    cols = x.shape[1]
    stacked = jnp.dot(onehot_bf16, jnp.concatenate(_bf16_pieces(x, 3), axis=1), preferred_element_type=F32)
    return stacked[:, :cols] + stacked[:, cols:2 * cols] + stacked[:, 2 * cols:]


def _select_sum_right(x, onehot_bf16):
    rows = x.shape[0]
    stacked = jnp.dot(jnp.concatenate(_bf16_pieces(x, 3), axis=0), onehot_bf16, preferred_element_type=F32)
    return stacked[:rows] + stacked[rows:2 * rows] + stacked[2 * rows:]


def _head_mean_square(o):
    ones = jnp.where(_same_head_matrix(), 1.0, 0.0).astype(BF16)
    return _select_sum_right(o * o, ones) * (1.0 / HEAD_DIM)


def _mod_row(i):
    return jnp.where(i < CTX_TILES, 0, 1 + (i - CTX_TILES) // LAT_TILES_PER_SEQ)


def _mod_kernel(c_ref, w_ref, b_ref, o_ref):
    w = w_ref[0]
    for r in range(3):
        c = c_ref[r]
        s = c * jax.nn.sigmoid(c)
        o_ref[0, r:r + 1, :] = jnp.sum(s * w, axis=0, keepdims=True) + b_ref[0]


def modulation(c_rows, ada_w, ada_b):
    nt = 768
    n_out = 6 * D_MODEL
    return pl.pallas_call(
        _mod_kernel,
        grid=(DEPTH, n_out // nt),
        in_specs=[
            pl.BlockSpec((3, D_MODEL, 1), lambda l, j: (0, 0, 0)),
            pl.BlockSpec((1, D_MODEL, nt), lambda l, j: (l, 0, j)),
            pl.BlockSpec((1, 1, nt), lambda l, j: (l, 0, j)),
        ],
        out_specs=pl.BlockSpec((1, 3, nt), lambda l, j: (l, 0, j)),
        out_shape=jax.ShapeDtypeStruct((DEPTH, 3, n_out), F32),
        compiler_params=_cparams(("arbitrary", "arbitrary")),
        name="modulation",
    )(c_rows[:, :, None], ada_w, ada_b[:, None, :])


def _is_ctx_tile():
    return pl.program_id(0) < CTX_TILES


def _ctx_tile(i, *_):
    return (jnp.minimum(i, CTX_TILES - 1), 0)


def _lat_tile(i, *_):
    return (jnp.maximum(i - CTX_TILES, 0), 0)


def _proj_kernel(xc_ref, xl_ref, mod_ref, g_ref, w_ref, cos_ref, sa_ref, sb_ref, pm_ref, pg_ref, *cache_refs):
    x = jnp.where(_is_ctx_tile(), xc_ref[...], xl_ref[...])
    _proj_body(x, mod_ref, g_ref, w_ref, cos_ref, sa_ref, sb_ref, pm_ref, pg_ref, cache_refs)


def _proj_after_moe_kernel(dest_ref, y_ref, x1_ref, modp_ref, mod_ref, g_ref, w_ref, cos_ref, sa_ref, sb_ref,
                           *rest):
    x2_ref, pm_ref, pg_ref = rest[4:7]
    cache_refs, (buf_ref, sem) = rest[7:11], rest[11:]
    x2 = _moe_residual(dest_ref, y_ref, buf_ref, sem, x1_ref, modp_ref)
    x2_ref[...] = x2
    _proj_body(x2, mod_ref, g_ref, w_ref, cos_ref, sa_ref, sb_ref, pm_ref, pg_ref, cache_refs)


def _proj_body(x, mod_ref, g_ref, w_ref, cos_ref, sa_ref, sb_ref, pm_ref, pg_ref, cache_refs):
    ms = jnp.mean(x * x, axis=-1, keepdims=True)
    mod = mod_ref[0]
    h = x * lax.rsqrt(ms + EPS) * g_ref[...] * (1.0 + mod[1:2]) + mod[0:1]
    p = jnp.dot(h.astype(BF16), w_ref[0], preferred_element_type=F32)
    t = p[:, :2 * MIX_BLK]
    pm_ref[:, :2 * MIX_BLK] = (t * cos_ref[...] + pltpu.roll(t, 1, 1) * sa_ref[...]
                               + pltpu.roll(t, 2 * MIX_BLK - 1, 1) * sb_ref[...]).astype(BF16)
    pm_ref[:, 2 * MIX_BLK:] = p[:, 2 * MIX_BLK:PM_W].astype(BF16)
    pg_ref[...] = p[:, PM_W:]

    @pl.when(_is_ctx_tile())
    def _():
        for ref, col in zip(cache_refs, (M_AK, M_AV, M_CK, M_CV)):
            for hd in range(HEADS):
                lo = col * MIX_BLK + hd * HEAD_DIM
                ref[0, 0, hd] = p[:, lo:lo + HEAD_DIM]
            if ref.shape[1] > 1:
                ref[0, 1:] = jnp.zeros((ref.shape[1] - 1,) + tuple(ref.shape[2:]), F32)


_CACHE_SHAPE = jax.ShapeDtypeStruct((BATCH, DEPTH, HEADS, SEQ, HEAD_DIM), F32)


def _cache_spec(layer):
    n_layers = DEPTH if layer == 0 else 1
    return pl.BlockSpec((1, n_layers, HEADS, SEQ, HEAD_DIM),
                        lambda i, *_: (jnp.minimum(i, CTX_TILES - 1), layer, 0, 0, 0))


def _rope_tables():
    nf = A_DIM // 4
    freqs = ROPE_BASE ** (-np.arange(nf, dtype=np.float64) / nf)
    pos = np.arange(DEC_SEQ)
    row = (pos // GRID_W).astype(np.float64)
    col = (pos % GRID_W).astype(np.float64)
    ang = np.concatenate([row[:, None] * freqs, col[:, None] * freqs], axis=-1)
    cos = np.repeat(np.cos(ang), 2, axis=-1)
    sin = np.repeat(np.sin(ang), 2, axis=-1)
    odd = (np.arange(A_DIM) % 2 == 1)[None, :]
    sin_from_left = np.where(odd, sin, 0.0)
    sin_from_right = np.where(odd, 0.0, -sin)
    reps = 2 * MIX_BLK // A_DIM
    ident = (np.ones((TM, 2 * MIX_BLK)), np.zeros((TM, 2 * MIX_BLK)))
    return tuple(jnp.asarray(np.concatenate([np.tile(t, (1, reps)), tail], axis=0), F32)
                 for t, tail in ((cos, ident[0]), (sin_from_left, ident[1]), (sin_from_right, ident[1])))


def _rope_block(i):
    return (jnp.where(i < CTX_TILES, LAT_TILES_PER_SEQ, (i - CTX_TILES) % LAT_TILES_PER_SEQ), 0)


def projection_after_moe(dest, y_slots, x1, mod_prev, mod, layer, norm_g, w_in_bf16, rope, caches):
    rope_spec = pl.BlockSpec((TM, 2 * MIX_BLK), lambda i, d: _rope_block(i))
    mod_spec = pl.BlockSpec((1, 6, D_MODEL), lambda i, d: (_mod_row(i), 0, 0))
    n_in = 10
    return pl.pallas_call(
        _proj_after_moe_kernel,
        grid_spec=pltpu.PrefetchScalarGridSpec(
            num_scalar_prefetch=1,
            grid=(N_TILES,),
            in_specs=[pl.BlockSpec(memory_space=pl.ANY),
                      pl.BlockSpec((TM, D_MODEL), lambda i, d: (i, 0)),
                      mod_spec, mod_spec,
                      pl.BlockSpec((1, D_MODEL), lambda i, d: (0, 0)),
                      pl.BlockSpec((1, D_MODEL, PROJ_W), lambda i, d: (layer, 0, 0)),
                      rope_spec, rope_spec, rope_spec] + [pl.BlockSpec(memory_space=pl.ANY)] * 4,
            out_specs=[pl.BlockSpec((TM, D_MODEL), lambda i, d: (i, 0)),
                       pl.BlockSpec((TM, PM_W), lambda i, d: (i, 0)),
                       pl.BlockSpec((TM, PG_W), lambda i, d: (i, 0))] + [_cache_spec(layer)] * 4,
            scratch_shapes=_GATHER_SCRATCH,
        ),
        out_shape=[jax.ShapeDtypeStruct((T, D_MODEL), F32), jax.ShapeDtypeStruct((T, PM_W), BF16),
                   jax.ShapeDtypeStruct((T, PG_W), F32)] + [_CACHE_SHAPE] * 4,
        input_output_aliases={n_in + n: 3 + n for n in range(4)},
        compiler_params=_cparams(("arbitrary",)),
        name="projection_after_moe",
    )(dest, y_slots, x1, mod_prev, mod, norm_g[None, :], w_in_bf16, *rope, *caches)


def projection(x_ctx, x_lat, mod, norm_g, w_in_bf16, rope):
    layer = 0
    rope_spec = pl.BlockSpec((TM, 2 * MIX_BLK), _rope_block)
    return pl.pallas_call(
        _proj_kernel,
        grid=(N_TILES,),
        in_specs=[
            pl.BlockSpec((TM, D_MODEL), _ctx_tile),
            pl.BlockSpec((TM, D_MODEL), _lat_tile),
            pl.BlockSpec((1, 6, D_MODEL), lambda i: (_mod_row(i), 0, 0)),
            pl.BlockSpec((1, D_MODEL), lambda i: (0, 0)),
            pl.BlockSpec((1, D_MODEL, PROJ_W), lambda i: (layer, 0, 0)),
            rope_spec, rope_spec, rope_spec,
        ],
        out_specs=[pl.BlockSpec((TM, PM_W), lambda i: (i, 0)), pl.BlockSpec((TM, PG_W), lambda i: (i, 0))]
        + [_cache_spec(layer)] * 4,
        out_shape=[jax.ShapeDtypeStruct((T, PM_W), BF16), jax.ShapeDtypeStruct((T, PG_W), F32)]
        + [_CACHE_SHAPE] * 4,
        compiler_params=_cparams(("arbitrary",)),
        name="projection",
    )(x_ctx, x_lat, mod, norm_g[None, :], w_in_bf16, *rope)


LOG2_E = 1.4426950408889634


def _exp2_rows(s):
    e = jnp.exp2(s - jnp.max(s, axis=-1, keepdims=True))
    return e, 1.0 / jnp.sum(e, axis=-1, keepdims=True)


def _attn_kernel(lam_ref, q_ref, k_ref, v_ref, *rest, n_maps, post_scale, with_cache):
    if with_cache:
        kc_ref, vc_ref, g_ref, o_ref, kt_ref, vb_ref = rest
    else:
        g_ref, o_ref, kt_ref, vb_ref = rest

    @pl.when(pl.program_id(1) == 0)
    def _():
        k = k_ref[...].astype(F32)
        v = v_ref[...]
        if with_cache:
            k = jnp.concatenate([_cache_heads_on_lanes(kc_ref), k], axis=0)
            v = jnp.concatenate([_cache_heads_on_lanes(vc_ref).astype(BF16), v], axis=0)
        kt_ref[...] = k.T.astype(BF16)
        vb_ref[...] = v

    lane = _head_lanes()
    map_dim = HEAD_DIM // n_maps
    q = q_ref[...].astype(F32) * (map_dim ** -0.5 * LOG2_E)
    kt = kt_ref[...]
    vb = vb_ref[...]
    stack_rows = kt_ref.shape[1] <= TM
    weights = []
    for h in range(HEADS):
        masked = [jnp.where(_lane_range(lane, h * HEAD_DIM + j * map_dim, map_dim), q, 0.0).astype(BF16)
                  for j in range(n_maps)]
        if stack_rows:
            s = jnp.dot(jnp.concatenate(masked, axis=0), kt, preferred_element_type=F32)
            scores = [s[j * TM:(j + 1) * TM] for j in range(n_maps)]
        else:
            scores = [jnp.dot(m, kt, preferred_element_type=F32) for m in masked]
        parts = [_exp2_rows(x) for x in scores]
        w = parts[0][0] * parts[0][1]
        if n_maps == 2:
            w = w - parts[1][0] * (lam_ref[0] * parts[1][1])
        weights.append(w.astype(BF16))
    if stack_rows:
        oh = jnp.dot(jnp.concatenate(weights, axis=0), vb, preferred_element_type=F32)
        outs = [oh[h * TM:(h + 1) * TM] for h in range(HEADS)]
    else:
        outs = [jnp.dot(w, vb, preferred_element_type=F32) for w in weights]
    o = jnp.zeros(q.shape, F32)
    for h in range(HEADS):
        o = jnp.where(_lane_range(lane, h * HEAD_DIM, HEAD_DIM), outs[h], o)
    if n_maps == 2:
        o = o * lax.rsqrt(_head_mean_square(o) + EPS) * g_ref[...] * post_scale
    o_ref[...] = o


def _cache_block_spec(layer):
    return pl.BlockSpec((1, 1, HEADS, PAST_LEN, HEAD_DIM), lambda b, i: (b, layer, 0, 0, 0))


def _cache_heads_on_lanes(ref):
    return jnp.concatenate([ref[0, 0, h] for h in range(HEADS)], axis=1)


def attention(p, row_blk0, cols, n_seq, seq_len, lam, norm_g, *, n_maps, post_scale, cache=None, layer=0):
    nb = seq_len // TM
    kv_len = seq_len + (PAST_LEN if cache is not None else 0)
    kern = functools.partial(_attn_kernel, n_maps=n_maps, post_scale=post_scale, with_cache=cache is not None)
    kv_spec = lambda col: pl.BlockSpec((seq_len, MIX_BLK), lambda b, i: (row_blk0 + b, col))
    cache_specs = [_cache_block_spec(layer)] * 2 if cache is not None else []
    return Part(
        kernel=kern,
        in_specs=[
            pl.BlockSpec(memory_space=pltpu.SMEM),
            pl.BlockSpec((TM, MIX_BLK), lambda b, i: ((row_blk0 + b) * nb + i, cols[0])),
            kv_spec(cols[1]), kv_spec(cols[2]), *cache_specs,
            pl.BlockSpec((1, MIX_BLK), lambda b, i: (0, 0)),
        ],
        out_specs=[pl.BlockSpec((TM, MIX_BLK), lambda b, i: (b * nb + i, 0))],
        out_shapes=[jax.ShapeDtypeStruct((n_seq * seq_len, MIX_BLK), F32)],
        scratch=[pltpu.VMEM((MIX_BLK, kv_len), BF16), pltpu.VMEM((kv_len, MIX_BLK), BF16)],
        args=[lam, p, p, p, *(cache or ()), norm_g])


def _na_slab_start(i):
    return jnp.clip(i - 1, 0, GRID_H // 4 - NA_SLAB_ROWS // 4)


def _na_kernel(q_ref, k_ref, v_ref, kc_ref, vc_ref, bias_ref, o_ref):
    i = pl.program_id(1)
    start = pl.multiple_of(_na_slab_start(i) * TM, TM)
    ks_t = k_ref[pl.ds(start, NA_SLAB), :].astype(F32).T.astype(BF16)
    vs = v_ref[pl.ds(start, NA_SLAB), :]
    kc_t = _cache_heads_on_lanes(kc_ref).T.astype(BF16)
    vc = _cache_heads_on_lanes(vc_ref).astype(BF16)
    q = q_ref[...].astype(F32) * (HEAD_DIM ** -0.5)
    lane = _head_lanes()
    heads = [_lane_range(lane, h * HEAD_DIM, HEAD_DIM) for h in range(HEADS)]
    qm = jnp.concatenate([jnp.where(in_head, q, 0.0) for in_head in heads], axis=0).astype(BF16)
    s_loc = jnp.dot(qm, ks_t, preferred_element_type=F32)
    s_ctx = jnp.dot(qm, kc_t, preferred_element_type=F32)
    e_locs, e_ctxs, dens = [], [], []
    for h in range(HEADS):
        rows = slice(h * TM, (h + 1) * TM)
        sl = s_loc[rows] + bias_ref[0, h]
        sc = s_ctx[rows]
        m = jnp.maximum(jnp.max(sl, axis=-1, keepdims=True), jnp.max(sc, axis=-1, keepdims=True))
        e_loc = jnp.exp(sl - m)
        e_ctx = jnp.exp(sc - m)
        dens.append(jnp.sum(e_loc, axis=-1, keepdims=True) + jnp.sum(e_ctx, axis=-1, keepdims=True))
        e_locs.append(e_loc.astype(BF16))
        e_ctxs.append(e_ctx.astype(BF16))
    oh = (jnp.dot(jnp.concatenate(e_locs, axis=0), vs, preferred_element_type=F32)
          + jnp.dot(jnp.concatenate(e_ctxs, axis=0), vc, preferred_element_type=F32))
    o = jnp.zeros(q.shape, F32)
    for h in range(HEADS):
        o = jnp.where(heads[h], oh[h * TM:(h + 1) * TM] / dens[h], o)
    o_ref[...] = o


def _na_bias_tables(rpb):
    n_dr, n_dc = 2 * NA_WIN_H - 1, 2 * NA_WIN_W - 1
    cq = np.arange(GRID_W)[:, None]
    ck = np.arange(GRID_W)[None, :]
    wc0 = np.clip(cq - NA_WIN_W // 2, 0, GRID_W - NA_WIN_W)
    col_ok = (ck >= wc0) & (ck < wc0 + NA_WIN_W)
    col_pick = np.clip(ck - cq + NA_WIN_W - 1, 0, n_dc - 1)[..., None] == np.arange(n_dc)
    by_col = jnp.einsum("hab,qcb->haqc", rpb.astype(F32), jnp.asarray(col_pick, F32), precision=HIGHEST)
    margin = 4
    by_col = jnp.pad(by_col.transpose(0, 2, 1, 3), ((0, 0), (0, 0), (margin, margin), (0, 0)))
    by_col = by_col.reshape(HEADS, GRID_W, (n_dr + 2 * margin) * GRID_W)
    pieces, row_ok = [], []
    for tile in (0, 1, GRID_H // 4 - 1):
        slab0 = int(np.clip(tile - 1, 0, GRID_H // 4 - NA_SLAB_ROWS // 4)) * 4
        rq = tile * 4 + np.arange(4)
        rk = (slab0 + np.arange(NA_SLAB) // GRID_W)[None, :]
        wr0 = np.clip(rq - NA_WIN_H // 2, 0, GRID_H - NA_WIN_H)[:, None]
        row_ok.append((rk >= wr0) & (rk < wr0 + NA_WIN_H))
        for r in rq:
            first = slab0 - int(r) + NA_WIN_H - 1 + margin
            assert 0 <= first and first + NA_SLAB_ROWS <= n_dr + 2 * margin
            pieces.append(by_col[:, :, first * GRID_W:first * GRID_W + NA_SLAB])
    table = jnp.stack(pieces).reshape(3, 4, HEADS, GRID_W, NA_SLAB).transpose(0, 2, 1, 3, 4)
    valid = np.stack(row_ok)[:, None, :, None, :] & np.tile(col_ok, (1, NA_SLAB_ROWS))[None, None, None]
    table = jnp.where(jnp.asarray(valid), table, NEG_BIG)
    return table.reshape(3, HEADS, TM, NA_SLAB)


def na_latent(p, kc, vc, layer, bias):
    n_t = LAT_TILES_PER_SEQ
    seq_blk0 = TP // DEC_SEQ

    def bias_idx(b, i):
        return (jnp.minimum(i, 1) + i // (n_t - 1), 0, 0, 0)

    return Part(
        kernel=_na_kernel,
        in_specs=[
            pl.BlockSpec((TM, MIX_BLK), lambda b, i: (CTX_TILES + b * n_t + i, M_CQ)),
            pl.BlockSpec((DEC_SEQ, MIX_BLK), lambda b, i: (seq_blk0 + b, M_CK)),
            pl.BlockSpec((DEC_SEQ, MIX_BLK), lambda b, i: (seq_blk0 + b, M_CV)),
            _cache_block_spec(layer), _cache_block_spec(layer),
            pl.BlockSpec((1, HEADS, TM, NA_SLAB), bias_idx),
        ],
        out_specs=[pl.BlockSpec((TM, MIX_BLK), lambda b, i: (b * n_t + i, 0))],
        out_shapes=[jax.ShapeDtypeStruct((TL, MIX_BLK), F32)],
        scratch=[],
        args=[p, p, p, kc, vc, bias])


MAX_EXPONENT = 80.0


def _hgrn_direction(q_ref, f_ref, v_ref, lb, st_ref, o_ref, reverse):
    n_ch = TM // B_CHUNK
    r_idx = lax.broadcasted_iota(jnp.int32, (TM, TM), 0)
    c_idx = lax.broadcasted_iota(jnp.int32, (TM, TM), 1)
    tri = (c_idx >= r_idx) if reverse else (c_idx <= r_idx)
    zq = q_ref[...]
    q = zq * jax.nn.sigmoid(zq)
    z = f_ref[...]
    gate = (1.0 - lb) * jax.nn.sigmoid(z)
    logf = jnp.log(lb + gate)
    kk = (1.0 - lb) - gate
    b = _select_sum_left(jnp.where(tri, 1.0, 0.0).astype(BF16), logf)
    b3 = b.reshape(n_ch, B_CHUNK, MIX_BLK)
    mid = B_CHUNK // 2 if reverse else B_CHUNK // 2 - 1
    q_in = (q.reshape(b3.shape) * jnp.exp(b3 - b3[:, mid:mid + 1, :])).reshape(TM, MIX_BLK)
    q_dec = (q * jnp.exp(b)).astype(BF16)
    b_t = b.T
    kk_t = kk.T
    far = 0 if reverse else TM - 1
    b_far = b_t[:, far:far + 1]
    k_dec_t = (kk_t * jnp.exp(b_far - b_t)).astype(BF16)
    vb = v_ref[...].astype(BF16)
    st = st_ref[...]
    o_state = jnp.dot(q_dec, st.astype(BF16), preferred_element_type=F32)
    kv = jnp.dot(k_dec_t, vb, preferred_element_type=F32)
    st_ref[...] = st * jnp.exp(b_far) + jnp.where(_same_head_matrix(), kv, 0.0)
    lane = _head_lanes()
    token = lax.broadcasted_iota(jnp.int32, (1, TM), 1)
    local = lax.broadcasted_iota(jnp.int32, (HEADS * B_CHUNK, 1), 0) % B_CHUNK
    heads = [_lane_range(lane, h * HEAD_DIM, HEAD_DIM) for h in range(HEADS)]
    weights = []
    for c in range(n_ch):
        ref = b_t[:, c * B_CHUNK + mid:c * B_CHUNK + mid + 1]
        k_c_t = (kk_t * jnp.exp(jnp.minimum(ref - b_t, MAX_EXPONENT))).astype(BF16)
        q_c = q_in[c * B_CHUNK:(c + 1) * B_CHUNK, :]
        lhs = jnp.concatenate([jnp.where(in_head, q_c, 0.0) for in_head in heads], axis=0)
        a = jnp.dot(lhs.astype(BF16), k_c_t, preferred_element_type=F32)
        t_abs = c * B_CHUNK + local
        weights.append(jnp.where((token >= t_abs) if reverse else (token <= t_abs), a, 0.0).astype(BF16))
    res = jnp.dot(jnp.concatenate(weights, axis=0), vb, preferred_element_type=F32)
    for c in range(n_ch):
        rows = slice(c * B_CHUNK, (c + 1) * B_CHUNK)
        o_c = o_state[rows, :]
        for h, in_head in enumerate(heads):
            lo = (c * HEADS + h) * B_CHUNK
            o_c = o_c + jnp.where(in_head, res[lo:lo + B_CHUNK, :], 0.0)
        o_ref[rows, :] = o_c


def _hgrn_kernel(qf_ref, ff_ref, vf_ref, qb_ref, fb_ref, vb_ref, lb_ref, s0_ref,
                 of_ref, ob_ref, s_ref, stf_ref, stb_ref, *, has_s0):
    j = pl.program_id(1)

    @pl.when(j == 0)
    def _():
        if has_s0:
            stf_ref[...] = s0_ref[0, 0]
            stb_ref[...] = s0_ref[0, 1]
        else:
            stf_ref[...] = jnp.zeros((MIX_BLK, MIX_BLK), F32)
            stb_ref[...] = jnp.zeros((MIX_BLK, MIX_BLK), F32)

    lb = lb_ref[...]
    _hgrn_direction(qf_ref, ff_ref, vf_ref, lb[0:1], stf_ref, of_ref, False)
    _hgrn_direction(qb_ref, fb_ref, vb_ref, lb[1:2], stb_ref, ob_ref, True)

    @pl.when(j == pl.num_programs(1) - 1)
    def _():
        for d, st_ref in enumerate((stf_ref, stb_ref)):
            s = st_ref[...]
            for hd in range(HEADS):
                lo = hd * HEAD_DIM
                s_ref[0, d, hd] = s[lo:lo + HEAD_DIM, lo:lo + HEAD_DIM]


def hgrn(p, row_tile0, n_seq, seq_len, lb, s0):
    nb = seq_len // TM
    has_s0 = s0 is not None
    if s0 is None:
        s0 = jnp.zeros((1, 2, MIX_BLK, MIX_BLK), F32)

    def fwd(col):
        return pl.BlockSpec((TM, MIX_BLK), lambda s, j: (row_tile0 + s * nb + j, col))

    def bwd(col):
        return pl.BlockSpec((TM, MIX_BLK), lambda s, j: (row_tile0 + s * nb + nb - 1 - j, col))

    state_spec = pl.BlockSpec((1, 2, MIX_BLK, MIX_BLK), lambda s, j: (s if has_s0 else 0, 0, 0, 0))
    out_rows = n_seq * seq_len
    return Part(
        kernel=functools.partial(_hgrn_kernel, has_s0=has_s0),
        in_specs=[fwd(G_BQ), fwd(G_BFF), fwd(G_BV), bwd(G_BQ), bwd(G_BFB), bwd(G_BV),
                  pl.BlockSpec((2, MIX_BLK), lambda s, j: (0, 0)), state_spec],
        out_specs=[
            pl.BlockSpec((TM, MIX_BLK), lambda s, j: (s * nb + j, 0)),
            pl.BlockSpec((TM, MIX_BLK), lambda s, j: (s * nb + nb - 1 - j, 0)),
            pl.BlockSpec((1, 2, HEADS, HEAD_DIM, HEAD_DIM), lambda s, j: (s, 0, 0, 0, 0)),
        ],
        out_shapes=[jax.ShapeDtypeStruct((out_rows, MIX_BLK), F32),
                    jax.ShapeDtypeStruct((out_rows, MIX_BLK), F32),
                    jax.ShapeDtypeStruct((n_seq, 2, HEADS, HEAD_DIM, HEAD_DIM), F32)],
        scratch=[pltpu.VMEM((MIX_BLK, MIX_BLK), F32), pltpu.VMEM((MIX_BLK, MIX_BLK), F32)],
        args=[p, p, p, p, p, p, lb, s0])


def _state_to_blockdiag(s):
    eye = jnp.eye(HEADS, dtype=F32)
    full = s.astype(F32)[:, :, :, :, None, :] * eye[None, None, :, None, :, None]
    return full.reshape(s.shape[0], 2, MIX_BLK, MIX_BLK)


def _fft_kernel(u_ref, cs64_ref, csl_ref, o_ref, ab_ref, *, norm):
    seq_len = u_ref.shape[0]

    @pl.when(pl.program_id(1) == 0)
    def _():
        ab = jnp.dot(u_ref[...], cs64_ref[...], preferred_element_type=F32).astype(BF16)
        ab_ref[:seq_len, :] = ab[:, :MIX_BLK]
        ab_ref[seq_len:, :] = ab[:, MIX_BLK:]

    o_ref[...] = jnp.dot(csl_ref[...], ab_ref[...], preferred_element_type=F32) * norm


def _dft_tables(n):
    k = np.arange(n)
    ang = 2.0 * np.pi * ((k[:, None] * k[None, :]) % n) / n
    return np.cos(ang), np.sin(ang)


def _dft_constants(seq_len):
    c64, s64 = _dft_tables(HEAD_DIM)
    eye = np.eye(HEADS)
    cl, sl = _dft_tables(seq_len)
    as_bf16 = lambda a: jnp.asarray(a, F32).astype(BF16)
    return (as_bf16(np.concatenate([np.kron(eye, c64), -np.kron(eye, s64)], axis=1)),
            as_bf16(np.concatenate([cl, sl], axis=1)))


def fourier_mix(p, row_blk0, n_seq, seq_len, consts):
    cs64, csl = consts
    nb = seq_len // TM
    norm = 1.0 / math.sqrt(seq_len * HEAD_DIM)
    return Part(
        kernel=functools.partial(_fft_kernel, norm=norm),
        in_specs=[
            pl.BlockSpec((seq_len, MIX_BLK), lambda s, i: (row_blk0 + s, M_DU)),
            pl.BlockSpec((MIX_BLK, 2 * MIX_BLK), lambda s, i: (0, 0)),
            pl.BlockSpec((TM, 2 * seq_len), lambda s, i: (i, 0)),
        ],
        out_specs=[pl.BlockSpec((TM, MIX_BLK), lambda s, i: (s * nb + i, 0))],
        out_shapes=[jax.ShapeDtypeStruct((n_seq * seq_len, MIX_BLK), F32)],
        scratch=[pltpu.VMEM((2 * seq_len, MIX_BLK), BF16)],
        args=[p, cs64, csl])


def _route(logits_t, rb):
    per = N_EXPERTS // N_GROUPS
    score = [jax.nn.sigmoid(logits_t[e:e + 1, :]) for e in range(N_EXPERTS)]
    sel = [score[e] + rb[e:e + 1, :] for e in range(N_EXPERTS)]
    gscore = []
    for g in range(N_GROUPS):
        vals = sel[g * per:(g + 1) * per]
        best = None
        for a in range(per):
            for b in range(a + 1, per):
                pair = vals[a] + vals[b]
                best = pair if best is None else jnp.maximum(best, pair)
        gscore.append(best)
    chosen = []
    for g in range(N_GROUPS):
        ok = None
        for j in range(N_GROUPS):
            if j == g:
                continue
            cond = gscore[g] > gscore[j] if j < g else gscore[g] >= gscore[j]
            ok = cond if ok is None else ok & cond
        chosen.append(ok)
    picked = []
    for e in range(N_EXPERTS):
        g = e // per
        rank = jnp.zeros_like(sel[e])
        for j in range(g * per, (g + 1) * per):
            if j == e:
                continue
            ahead = sel[j] >= sel[e] if j < e else sel[j] > sel[e]
            rank = rank + jnp.where(ahead, 1.0, 0.0)
        picked.append(chosen[g] & (rank < 2.0))
    wsum = jnp.zeros_like(score[0])
    for e in range(N_EXPERTS):
        wsum = wsum + jnp.where(picked[e], score[e], 0.0)
    bucket = jnp.zeros_like(wsum)
    w_a = jnp.zeros_like(wsum)
    w_b = jnp.zeros_like(wsum)
    for g in range(N_GROUPS):
        for n, (a, b) in enumerate(EXPERT_PAIRS):
            hit = picked[g * per + a] & picked[g * per + b]
            bucket = jnp.where(hit, float(g * len(EXPERT_PAIRS) + n), bucket)
            w_a = jnp.where(hit, score[g * per + a] / wsum, w_a)
            w_b = jnp.where(hit, score[g * per + b] / wsum, w_b)
    return bucket, w_a, w_b


def _out_kernel(*refs):
    streams, rest = refs[:12], refs[12:]
    (bg_ref, mod_ref, hg_ref, w_ref, g2_ref, rw_ref, rb_ref,
     x1_ref, h2_ref, bucket_ref, rank_ref, counts_ref, run_ref) = rest
    is_ctx = pl.program_id(0) < TP // TM_OUT
    x, o_a, o_f, o_b, o_c, o_d = (jnp.where(is_ctx, streams[2 * n][...], streams[2 * n + 1][...])
                                  for n in range(6))

    @pl.when(pl.program_id(0) == 0)
    def _():
        run_ref[...] = jnp.zeros(run_ref.shape, F32)

    mod = mod_ref[0]
    hb = o_f + o_b
    zg = bg_ref[...]
    hb = hb * lax.rsqrt(_head_mean_square(hb) + EPS) * hg_ref[...] * (zg * jax.nn.sigmoid(zg))
    mixers = jnp.concatenate([part.astype(BF16) for part in (o_a, hb, o_c, o_d)], axis=1)
    mixed = jnp.dot(mixers, w_ref[0], preferred_element_type=F32)
    x1 = x + mod[2:3] * mixed
    x1_ref[...] = x1
    ms = jnp.mean(x1 * x1, axis=-1, keepdims=True)
    h2 = x1 * lax.rsqrt(ms + EPS) * g2_ref[...] * (1.0 + mod[4:5]) + mod[3:4]
    rw = rw_ref[...]
    r = jnp.dot(jnp.concatenate(_bf16_pieces(h2, 2), axis=0), rw, preferred_element_type=F32)
    r = r[:TM_OUT] + r[TM_OUT:]
    bucket, w_a, w_b = _route((r[:, :LANES] + r[:, LANES:]).T, rb_ref[...])
    h2_ref[:, :D_MODEL] = h2
    h2_ref[:, D_MODEL:] = jnp.concatenate([w_a, w_b, jnp.zeros((LANES - 2, TM_OUT), F32)], axis=0).T
    onehot = jnp.where(lax.broadcasted_iota(jnp.int32, (BUCKET_ROWS, 1), 0).astype(F32) == bucket, 1.0, 0.0)
    s_idx = lax.broadcasted_iota(jnp.int32, (TM_OUT, TM_OUT), 0)
    t_idx = lax.broadcasted_iota(jnp.int32, (TM_OUT, TM_OUT), 1)
    prefix = jnp.dot(onehot.astype(BF16), jnp.where(s_idx <= t_idx, 1.0, 0.0).astype(BF16),
                     preferred_element_type=F32)
    run = run_ref[...]
    rank = jnp.sum(onehot * (prefix - 1.0 + run[:, 0:1]), axis=0, keepdims=True)
    run = run + jnp.sum(onehot, axis=1, keepdims=True)
    run_ref[...] = run
    bucket_ref[...] = bucket.astype(jnp.int32)
    rank_ref[...] = rank.astype(jnp.int32)
    counts_ref[...] = run


def out_and_route(x_pair, x_is_combined, mixer_pairs, p, mod, layer, hgrn_g, w_out_bf16, norm2_g, router_pieces,
                  router_b):
    n_ctx = TP // TM_OUT
    ctx_tile = lambda i: (jnp.minimum(i, n_ctx - 1), 0)
    lat_tile = lambda i: (jnp.maximum(i - n_ctx, 0), 0)
    mod_row = lambda i: jnp.where(i < n_ctx, 0, 1 + (i - n_ctx) // (DEC_SEQ // TM_OUT))
    tile = lambda w: pl.BlockSpec((TM_OUT, w), lambda i: (i, 0))
    full = lambda r, c: pl.BlockSpec((r, c), lambda i: (0, 0))
    stream_specs = [pl.BlockSpec((TM_OUT, D_MODEL), ctx_tile),
                    pl.BlockSpec((TM_OUT, D_MODEL),
                                 (lambda i: (jnp.maximum(i, n_ctx), 0)) if x_is_combined else lat_tile)]
    stream_args = list(x_pair)
    for o_ctx, o_lat in mixer_pairs:
        stream_specs += [pl.BlockSpec((TM_OUT, MIX_BLK), ctx_tile), pl.BlockSpec((TM_OUT, MIX_BLK), lat_tile)]
        stream_args += [o_ctx, o_lat]
    return pl.pallas_call(
        _out_kernel,
        grid=(T // TM_OUT,),
        in_specs=stream_specs + [
            pl.BlockSpec((TM_OUT, MIX_BLK), lambda i: (i, G_BG)),
            pl.BlockSpec((1, 6, D_MODEL), lambda i: (mod_row(i), 0, 0)),
            full(1, MIX_BLK), pl.BlockSpec((1, D_MODEL, D_MODEL), lambda i: (layer, 0, 0)), full(1, D_MODEL),
            full(D_MODEL, 2 * LANES), full(N_EXPERTS, 1),
        ],
        out_specs=[tile(D_MODEL), tile(ROW_W), pl.BlockSpec((1, TM_OUT), lambda i: (0, i)),
                   pl.BlockSpec((1, TM_OUT), lambda i: (0, i)), full(BUCKET_ROWS, LANES)],
        out_shape=[jax.ShapeDtypeStruct((T, D_MODEL), F32),
                   jax.ShapeDtypeStruct((T, ROW_W), F32),
                   jax.ShapeDtypeStruct((1, T), jnp.int32),
                   jax.ShapeDtypeStruct((1, T), jnp.int32),
                   jax.ShapeDtypeStruct((BUCKET_ROWS, LANES), F32)],
        scratch_shapes=[pltpu.VMEM((BUCKET_ROWS, LANES), F32)],
        compiler_params=_cparams(("arbitrary",)),
        name="out_and_route",
    )(*stream_args, p, mod, jnp.tile(hgrn_g, HEADS)[None, :], w_out_bf16,
      norm2_g[None, :], router_pieces, router_b[:, None])


def _router_pieces(router_w):
    hi, lo = _bf16_pieces(router_w.astype(F32), 2)
    pad = lambda a: jnp.pad(a, ((0, 0), (0, LANES - N_EXPERTS)))
    return jnp.concatenate([pad(hi), pad(lo)], axis=1)


def routing_plan(bucket, rank, counts):
    counts = counts[:N_BUCKETS, 0].astype(jnp.int32)
    n_tiles = (counts + TM_MOE - 1) // TM_MOE
    tile_end = jnp.cumsum(n_tiles)
    tile_start = tile_end - n_tiles
    buckets = jnp.arange(N_BUCKETS, dtype=jnp.int32)
    start_of_token = jnp.sum(jnp.where(bucket[0][:, None] == buckets[None, :], tile_start[None, :], 0), axis=1)
    dest = start_of_token * TM_MOE + rank[0]
    tiles = jnp.arange(MAX_TILES, dtype=jnp.int32)
    valid = tiles < tile_end[-1]
    tile_bucket = jnp.sum((jnp.minimum(tiles, tile_end[-1] - 1)[:, None] >= tile_end[None, :]).astype(jnp.int32), axis=1)
    pair_a = np.array([a for a, _ in EXPERT_PAIRS], np.int32)
    pair_b = np.array([b for _, b in EXPERT_PAIRS], np.int32)
    per = N_EXPERTS // N_GROUPS
    exp_a = jnp.asarray((np.arange(N_BUCKETS) // len(EXPERT_PAIRS)) * per + np.tile(pair_a, N_GROUPS), jnp.int32)
    exp_b = jnp.asarray((np.arange(N_BUCKETS) // len(EXPERT_PAIRS)) * per + np.tile(pair_b, N_GROUPS), jnp.int32)
    pick = tile_bucket[:, None] == buckets[None, :]
    tile_a = jnp.sum(jnp.where(pick, exp_a[None, :], 0), axis=1)
    tile_b = jnp.sum(jnp.where(pick, exp_b[None, :], 0), axis=1)
    return dest.astype(jnp.int32), tile_a, tile_b, valid.astype(jnp.int32), (tile_end[-1:] - 1).astype(jnp.int32)


def _row_copy(src, src_row, dst, dst_row, sem):
    return pltpu.make_async_copy(src.at[pl.ds(src_row, 1), :], dst.at[pl.ds(dst_row, 1), :], sem)


def _scatter_kernel(dest_ref, h_ref, init_ref, o_ref, sem):
    del init_ref
    base = pl.program_id(0) * TM

    for r in range(TM):
        _row_copy(h_ref, r, o_ref, dest_ref[base + r], sem).start(priority=r % N_DMA_PRIORITIES)
    pltpu.make_async_copy(h_ref, o_ref.at[pl.ds(0, TM), :], sem).wait()


def scatter_to_slots(h2, dest, slots):
    return pl.pallas_call(
        _scatter_kernel,
        grid_spec=pltpu.PrefetchScalarGridSpec(
            num_scalar_prefetch=1,
            grid=(N_TILES,),
            in_specs=[pl.BlockSpec((TM, ROW_W), lambda i, d: (i, 0)),
                      pl.BlockSpec(memory_space=pl.ANY)],
            out_specs=pl.BlockSpec(memory_space=pl.ANY),
            scratch_shapes=[pltpu.SemaphoreType.DMA(())],
        ),
        out_shape=jax.ShapeDtypeStruct((N_SLOTS, ROW_W), F32),
        input_output_aliases={2: 0},
        compiler_params=_cparams(("arbitrary",)),
        name="scatter_to_slots",
    )(dest, h2, slots)


def _moe_kernel(ta_ref, tb_ref, valid_ref, last_ref, h_ref, wga_ref, wua_ref, wda_ref, wgb_ref, wub_ref, wdb_ref,
                o_ref):
    del ta_ref, tb_ref, last_ref
    i = pl.program_id(0)

    @pl.when(valid_ref[i] == 1)
    def _():
        x = h_ref[:, :D_MODEL].astype(BF16)
        gates = h_ref[:, D_MODEL:]
        y = jnp.zeros((TM_MOE, D_MODEL), F32)
        for n, (wg, wu, wd) in enumerate(((wga_ref, wua_ref, wda_ref), (wgb_ref, wub_ref, wdb_ref))):
            a = jnp.dot(x, wg[0, 0].astype(BF16), preferred_element_type=F32)
            u = jnp.dot(x, wu[0, 0].astype(BF16), preferred_element_type=F32)
            z = a * jax.nn.sigmoid(a) * u * gates[:, n:n + 1]
            y = y + jnp.dot(z.astype(BF16), wd[0, 0].astype(BF16), preferred_element_type=F32)
        o_ref[:, 0, :] = y

    @pl.when(valid_ref[i] == 0)
    def _():
        o_ref[...] = jnp.zeros((TM_MOE, 1, D_MODEL), F32)


def moe(h_slots, tile_a, tile_b, valid, last, layer, wg, wu, wd):
    up_a = pl.BlockSpec((1, 1, D_MODEL, D_EXPERT), lambda i, ta, tb, v, last: (layer, ta[i], 0, 0))
    up_b = pl.BlockSpec((1, 1, D_MODEL, D_EXPERT), lambda i, ta, tb, v, last: (layer, tb[i], 0, 0))
    down_a = pl.BlockSpec((1, 1, D_EXPERT, D_MODEL), lambda i, ta, tb, v, last: (layer, ta[i], 0, 0))
    down_b = pl.BlockSpec((1, 1, D_EXPERT, D_MODEL), lambda i, ta, tb, v, last: (layer, tb[i], 0, 0))
    return pl.pallas_call(
        _moe_kernel,
        grid_spec=pltpu.PrefetchScalarGridSpec(
            num_scalar_prefetch=4,
            grid=(MAX_TILES,),
            in_specs=[pl.BlockSpec((TM_MOE, ROW_W), lambda i, ta, tb, v, last: (jnp.minimum(i, last[0]), 0)),
                      up_a, up_a, down_a, up_b, up_b, down_b],
            out_specs=pl.BlockSpec((TM_MOE, 1, D_MODEL), lambda i, ta, tb, v, last: (i, 0, 0)),
        ),
        out_shape=jax.ShapeDtypeStruct((N_SLOTS, 1, D_MODEL), F32),
        compiler_params=_cparams(("arbitrary",)),
        name="moe",
    )(tile_a, tile_b, valid, last, h_slots, wg, wu, wd, wg, wu, wd)


def _gather_tile(dest_ref, y_ref, buf_ref, sem, tile, slot):
    for r in range(TM):
        pltpu.make_async_copy(y_ref.at[pl.ds(dest_ref[tile * TM + r], 1)],
                              buf_ref.at[slot, pl.ds(r, 1)], sem.at[slot]).start(priority=r % N_DMA_PRIORITIES)


def _moe_residual(dest_ref, y_ref, buf_ref, sem, x1_ref, mod_ref):
    i = pl.program_id(0)
    slot = i % 2

    @pl.when(i == 0)
    def _():
        _gather_tile(dest_ref, y_ref, buf_ref, sem, 0, 0)

    @pl.when(i + 1 < pl.num_programs(0))
    def _():
        _gather_tile(dest_ref, y_ref, buf_ref, sem, i + 1, 1 - slot)

    pltpu.make_async_copy(y_ref.at[pl.ds(0, TM)], buf_ref.at[slot], sem.at[slot]).wait()
    return x1_ref[...] + mod_ref[0][5:6] * buf_ref[slot, :, 0, :]


def _final_kernel(dest_ref, y_ref, x1_ref, mod_ref, g_ref, oc_ref, ol_ref, buf_ref, sem):
    x2 = _moe_residual(dest_ref, y_ref, buf_ref, sem, x1_ref, mod_ref)
    ms = jnp.mean(x2 * x2, axis=-1, keepdims=True)
    y = x2 * lax.rsqrt(ms + EPS) * g_ref[...]

    @pl.when(_is_ctx_tile())
    def _():
        oc_ref[...] = y

    @pl.when(jnp.logical_not(_is_ctx_tile()))
    def _():
        ol_ref[...] = y


_GATHER_SCRATCH = [pltpu.VMEM((2, TM, 1, D_MODEL), F32), pltpu.SemaphoreType.DMA((2,))]


def final_norm(dest, y_slots, x1, mod, final_g):
    return pl.pallas_call(
        _final_kernel,
        grid_spec=pltpu.PrefetchScalarGridSpec(
            num_scalar_prefetch=1,
            grid=(N_TILES,),
            in_specs=[pl.BlockSpec(memory_space=pl.ANY),
                      pl.BlockSpec((TM, D_MODEL), lambda i, d: (i, 0)),
                      pl.BlockSpec((1, 6, D_MODEL), lambda i, d: (_mod_row(i), 0, 0)),
                      pl.BlockSpec((1, D_MODEL), lambda i, d: (0, 0))],
            out_specs=[pl.BlockSpec((TM, D_MODEL), _ctx_tile), pl.BlockSpec((TM, D_MODEL), _lat_tile)],
            scratch_shapes=_GATHER_SCRATCH,
        ),
        out_shape=[jax.ShapeDtypeStruct((TP, D_MODEL), F32), jax.ShapeDtypeStruct((TL, D_MODEL), F32)],
        compiler_params=_cparams(("arbitrary",)),
        name="final_norm",
    )(dest, y_slots, x1, mod, final_g[None, :])


def kernel(x_prompt, x_sample, cache_diff_k, cache_diff_v, cache_na_k, cache_na_v, state_hgrn, c, c_ctx,
           norm1_g, norm2_g, ada_w, ada_b, w_in, w_out, diff_lambda, diff_subln_g, hgrn_lb_logits,
           hgrn_norm_g, na_rpb, router_w, router_b, moe_w_gate, moe_w_up, moe_w_down, final_norm_g):
    assert SEQ == TM and PAST_LEN == TM and DEC_SEQ % TM_OUT == 0 and TP % DEC_SEQ == 0
    x_pair = (x_prompt.reshape(TP, D_MODEL), x_sample.reshape(TL, D_MODEL))
    w_in_bf16 = jnp.concatenate([w_in[:, :, c * MIX_BLK:(c + 1) * MIX_BLK] for c in PM_BLOCKS + PG_BLOCKS],
                                axis=-1).astype(BF16)
    w_out_bf16 = w_out.astype(BF16)
    router_pieces = _router_pieces(router_w)
    mods = modulation(jnp.concatenate([c_ctx[None, :], c], axis=0), ada_w, ada_b)
    mods = mods.reshape(DEPTH, 3, 6, D_MODEL)
    lb_sm = jax.nn.softmax(hgrn_lb_logits.astype(F32), axis=0)
    lb_all = jnp.cumsum(lb_sm, axis=0) - lb_sm[0:1]
    rope = _rope_tables()
    dft_ctx = _dft_constants(SEQ)
    dft_lat = _dft_constants(DEC_SEQ)
    lat_blk0 = TP // DEC_SEQ
    states = []
    moe_state = None
    for l in range(DEPTH):
        if moe_state is None:
            pm, pg, *new_kv = projection(*x_pair, mods[l], norm1_g[l], w_in_bf16, rope)
        else:
            x, pm, pg, *new_kv = projection_after_moe(*moe_state, mods[l - 1], mods[l], l, norm1_g[l], w_in_bf16, rope,
                                                 new_kv)
            x_pair = (x, x)

        lq = diff_lambda[l].astype(F32)
        lam_init = 0.8 - 0.6 * math.exp(-0.3 * l)
        lam = (jnp.exp(jnp.sum(lq[0] * lq[1])) - jnp.exp(jnp.sum(lq[2] * lq[3])) + lam_init).reshape(1)
        subln = jnp.tile(diff_subln_g[l], HEADS)[None, :]
        diff = functools.partial(attention, pm, cols=(M_AQ, M_AK, M_AV), lam=lam, norm_g=subln, n_maps=2,
                                 post_scale=1.0 - lam_init)
        (oa_ctx,), (of_ctx, ob_ctx, st_ctx), (oc_ctx,), (od_ctx,) = run_parts(
            [diff(row_blk0=0, n_seq=BATCH, seq_len=SEQ),
             hgrn(pg, 0, BATCH, SEQ, lb_all[l], None),
             attention(pm, 0, (M_CQ, M_CK, M_CV), BATCH, SEQ, lam, subln, n_maps=1, post_scale=1.0),
             fourier_mix(pm, 0, BATCH, SEQ, dft_ctx)],
            (BATCH, SEQ // TM), "context_mixers")
        (oa_lat,), (of_lat, ob_lat, _), (oc_lat,), (od_lat,) = run_parts(
            [diff(row_blk0=lat_blk0, n_seq=DEC_BATCH, seq_len=DEC_SEQ, cache=(cache_diff_k, cache_diff_v), layer=l),
             hgrn(pg, CTX_TILES, DEC_BATCH, DEC_SEQ, lb_all[l], _state_to_blockdiag(state_hgrn[:, l])),
             na_latent(pm, cache_na_k, cache_na_v, l, _na_bias_tables(na_rpb[l])),
             fourier_mix(pm, lat_blk0, DEC_BATCH, DEC_SEQ, dft_lat)],
            (DEC_BATCH, DEC_SEQ // TM), "latent_mixers")

        x1, h2, bucket, rank, counts = out_and_route(
            x_pair, l > 0, ((oa_ctx, oa_lat), (of_ctx, of_lat), (ob_ctx, ob_lat), (oc_ctx, oc_lat),
                            (od_ctx, od_lat)),
            pg, mods[l], l, hgrn_norm_g[l], w_out_bf16, norm2_g[l], router_pieces, router_b)
        dest, *tile_plan = routing_plan(bucket, rank, counts)
        h_slots = scatter_to_slots(h2, dest, jnp.zeros((N_SLOTS, ROW_W), F32) if l == 0 else h_slots)
        y_slots = moe(h_slots, *tile_plan, l, moe_w_gate, moe_w_up, moe_w_down)
        moe_state = (dest, y_slots, x1)

        states.append(st_ctx)
    y_prompt, y_sample = final_norm(*moe_state, mods[DEPTH - 1], final_norm_g)
    return (y_prompt.reshape(BATCH, SEQ, D_MODEL), y_sample.reshape(DEC_BATCH, DEC_SEQ, D_MODEL),
            *new_kv, jnp.stack(states, axis=1))
```

```python
import functools
import math
from typing import Any, NamedTuple

import numpy as np
import jax
import jax.numpy as jnp
from jax import lax
from jax.experimental import pallas as pl
from jax.experimental.pallas import tpu as pltpu

F32 = jnp.float32
BF16 = jnp.bfloat16
HIGHEST = lax.Precision.HIGHEST

D_MODEL = 1024
BATCH = 16
SEQ = 256
DEPTH = 2
DEC_BATCH = 2
DEC_SEQ = 2048
PAST_LEN = 256
GRID_W = 64
GRID_H = DEC_SEQ // GRID_W
EPS = 1e-6
NEG_BIG = -1e30
HEADS = 4
HEAD_DIM = 64
MIX_BLK = HEADS * HEAD_DIM
A_DIM = 32
ROPE_BASE = 10000.0
B_CHUNK = 32
NA_WIN_H = 8
NA_WIN_W = 16
N_EXPERTS = 16
N_GROUPS = 4
D_EXPERT = 512
PROJ_W = 12 * MIX_BLK
TP = BATCH * SEQ
TL = DEC_BATCH * DEC_SEQ
T = TP + TL
TM = 256
N_TILES = T // TM
CTX_TILES = TP // TM
LAT_TILES_PER_SEQ = DEC_SEQ // TM
(C_AQ, C_AK, C_AV, C_BQ, C_BFF, C_BFB, C_BV, C_BG, C_CQ, C_CK, C_CV, C_DU) = range(12)
PM_BLOCKS = (C_AQ, C_AK, C_AV, C_CQ, C_CK, C_CV, C_DU)
PG_BLOCKS = (C_BQ, C_BFF, C_BFB, C_BV, C_BG)
(M_AQ, M_AK, M_AV, M_CQ, M_CK, M_CV, M_DU) = range(len(PM_BLOCKS))
(G_BQ, G_BFF, G_BFB, G_BV, G_BG) = range(len(PG_BLOCKS))
PM_W = len(PM_BLOCKS) * MIX_BLK
PG_W = len(PG_BLOCKS) * MIX_BLK
NA_SLAB_ROWS = 12
NA_SLAB = NA_SLAB_ROWS * GRID_W
LANES = 128
ROW_W = D_MODEL + LANES
EXPERT_PAIRS = ((0, 1), (0, 2), (0, 3), (1, 3), (2, 3), (2, 1))
N_BUCKETS = N_GROUPS * len(EXPERT_PAIRS)
BUCKET_ROWS = 32
N_DMA_PRIORITIES = 2
TM_OUT = 512
TM_SCATTER = 1024
TM_MOE = 384
MAX_TILES = T // TM_MOE + N_BUCKETS
N_SLOTS = MAX_TILES * TM_MOE
VMEM_LIMIT = 56 * 1024 * 1024


def _cparams(sem):
    return pltpu.CompilerParams(dimension_semantics=sem, vmem_limit_bytes=VMEM_LIMIT)


class Part(NamedTuple):
    kernel: Any
    in_specs: list
    args: list
    out_specs: list
    out_shapes: list
    scratch: list


def _run_parts_kernel(*refs, layout):
    n_in = sum(n for _, n, _, _ in layout)
    n_out = sum(n for _, _, n, _ in layout)
    ins, outs, scratch = refs[:n_in], refs[n_in:n_in + n_out], refs[n_in + n_out:]
    i = o = s = 0
    for kernel, k_in, k_out, k_scratch in layout:
        kernel(*ins[i:i + k_in], *outs[o:o + k_out], *scratch[s:s + k_scratch])
        i, o, s = i + k_in, o + k_out, s + k_scratch


def run_parts(parts, grid, name):
    layout = tuple((p.kernel, len(p.in_specs), len(p.out_specs), len(p.scratch)) for p in parts)
    outs = pl.pallas_call(
        functools.partial(_run_parts_kernel, layout=layout),
        grid=grid,
        in_specs=[s for p in parts for s in p.in_specs],
        out_specs=[s for p in parts for s in p.out_specs],
        out_shape=[s for p in parts for s in p.out_shapes],
        scratch_shapes=[s for p in parts for s in p.scratch],
        compiler_params=_cparams(("arbitrary", "arbitrary")),
        name=name,
    )(*[a for p in parts for a in p.args])
    result, o = [], 0
    for p in parts:
        result.append(outs[o:o + len(p.out_specs)])
        o += len(p.out_specs)
    return result


def _head_lanes(width=MIX_BLK):
    return lax.broadcasted_iota(jnp.int32, (1, width), 1)


def _lane_range(lane, lo, n):
    return (lane >= lo) & (lane < lo + n)


def _same_head_matrix():
    r = lax.broadcasted_iota(jnp.int32, (MIX_BLK, MIX_BLK), 0) // HEAD_DIM
    c = lax.broadcasted_iota(jnp.int32, (MIX_BLK, MIX_BLK), 1) // HEAD_DIM
    return r == c


def _bf16_pieces(x, n):
    pieces = []
    for _ in range(n):
        piece = x.astype(BF16)
        pieces.append(piece)
        x = x - piece.astype(F32)
    return pieces


def _select_sum_left(onehot_bf16, x):
    cols = x.shape[1]
    stacked = jnp.dot(onehot_bf16, jnp.concatenate(_bf16_pieces(x, 3), axis=1), preferred_element_type=F32)
    return stacked[:, :cols] + stacked[:, cols:2 * cols] + stacked[:, 2 * cols:]


def _select_sum_right(x, onehot_bf16):
    rows = x.shape[0]
    stacked = jnp.dot(jnp.concatenate(_bf16_pieces(x, 3), axis=0), onehot_bf16, preferred_element_type=F32)
    return stacked[:rows] + stacked[rows:2 * rows] + stacked[2 * rows:]


def _head_mean_square(o):
    ones = jnp.where(_same_head_matrix(), 1.0, 0.0).astype(BF16)
    return _select_sum_right(o * o, ones) * (1.0 / HEAD_DIM)


def _mod_row(i):
    return jnp.where(i < CTX_TILES, 0, 1 + (i - CTX_TILES) // LAT_TILES_PER_SEQ)


def _mod_kernel(c_ref, w_ref, b_ref, o_ref):
    w = w_ref[0]
    for r in range(3):
        c = c_ref[r]
        s = c * jax.nn.sigmoid(c)
        o_ref[0, r:r + 1, :] = jnp.sum(s * w, axis=0, keepdims=True) + b_ref[0]


def modulation(c_rows, ada_w, ada_b):
    nt = 768
    n_out = 6 * D_MODEL
    return pl.pallas_call(
        _mod_kernel,
        grid=(DEPTH, n_out // nt),
        in_specs=[
            pl.BlockSpec((3, D_MODEL, 1), lambda l, j: (0, 0, 0)),
            pl.BlockSpec((1, D_MODEL, nt), lambda l, j: (l, 0, j)),
            pl.BlockSpec((1, 1, nt), lambda l, j: (l, 0, j)),
        ],
        out_specs=pl.BlockSpec((1, 3, nt), lambda l, j: (l, 0, j)),
        out_shape=jax.ShapeDtypeStruct((DEPTH, 3, n_out), F32),
        compiler_params=_cparams(("arbitrary", "arbitrary")),
        name="modulation",
    )(c_rows[:, :, None], ada_w, ada_b[:, None, :])


def _is_ctx_tile():
    return pl.program_id(0) < CTX_TILES


def _ctx_tile(i, *_):
    return (jnp.minimum(i, CTX_TILES - 1), 0)


def _lat_tile(i, *_):
    return (jnp.maximum(i - CTX_TILES, 0), 0)


def _proj_kernel(xc_ref, xl_ref, mod_ref, g_ref, w_ref, cos_ref, sa_ref, sb_ref, pm_ref, pg_ref, *cache_refs):
    x = jnp.where(_is_ctx_tile(), xc_ref[...], xl_ref[...])
    _proj_body(x, mod_ref, g_ref, w_ref, cos_ref, sa_ref, sb_ref, pm_ref, pg_ref, cache_refs)


def _proj_after_moe_kernel(dest_ref, y_ref, x1_ref, modp_ref, mod_ref, g_ref, w_ref, cos_ref, sa_ref, sb_ref,
                           *rest):
    x2_ref, pm_ref, pg_ref = rest[4:7]
    cache_refs, (buf_ref, sem) = rest[7:11], rest[11:]
    x2 = _moe_residual(dest_ref, y_ref, buf_ref, sem, x1_ref, modp_ref)
    x2_ref[...] = x2
    _proj_body(x2, mod_ref, g_ref, w_ref, cos_ref, sa_ref, sb_ref, pm_ref, pg_ref, cache_refs)


def _proj_body(x, mod_ref, g_ref, w_ref, cos_ref, sa_ref, sb_ref, pm_ref, pg_ref, cache_refs):
    ms = jnp.mean(x * x, axis=-1, keepdims=True)
    mod = mod_ref[0]
    h = x * lax.rsqrt(ms + EPS) * g_ref[...] * (1.0 + mod[1:2]) + mod[0:1]
    p = jnp.dot(h.astype(BF16), w_ref[0], preferred_element_type=F32)
    t = p[:, :2 * MIX_BLK]
    pm_ref[:, :2 * MIX_BLK] = (t * cos_ref[...] + pltpu.roll(t, 1, 1) * sa_ref[...]
                               + pltpu.roll(t, 2 * MIX_BLK - 1, 1) * sb_ref[...]).astype(BF16)
    pm_ref[:, 2 * MIX_BLK:] = p[:, 2 * MIX_BLK:PM_W].astype(BF16)
    pg_ref[...] = p[:, PM_W:]

    @pl.when(_is_ctx_tile())
    def _():
        for ref, col in zip(cache_refs, (M_AK, M_AV, M_CK, M_CV)):
            for hd in range(HEADS):
                lo = col * MIX_BLK + hd * HEAD_DIM
                ref[0, 0, hd] = p[:, lo:lo + HEAD_DIM]
            if ref.shape[1] > 1:
                ref[0, 1:] = jnp.zeros((ref.shape[1] - 1,) + tuple(ref.shape[2:]), F32)


_CACHE_SHAPE = jax.ShapeDtypeStruct((BATCH, DEPTH, HEADS, SEQ, HEAD_DIM), F32)


def _cache_spec(layer):
    n_layers = DEPTH if layer == 0 else 1
    return pl.BlockSpec((1, n_layers, HEADS, SEQ, HEAD_DIM),
                        lambda i, *_: (jnp.minimum(i, CTX_TILES - 1), layer, 0, 0, 0))


def _rope_tables():
    nf = A_DIM // 4
    freqs = ROPE_BASE ** (-np.arange(nf, dtype=np.float64) / nf)
    pos = np.arange(DEC_SEQ)
    row = (pos // GRID_W).astype(np.float64)
    col = (pos % GRID_W).astype(np.float64)
    ang = np.concatenate([row[:, None] * freqs, col[:, None] * freqs], axis=-1)
    cos = np.repeat(np.cos(ang), 2, axis=-1)
    sin = np.repeat(np.sin(ang), 2, axis=-1)
    odd = (np.arange(A_DIM) % 2 == 1)[None, :]
    sin_from_left = np.where(odd, sin, 0.0)
    sin_from_right = np.where(odd, 0.0, -sin)
    reps = 2 * MIX_BLK // A_DIM
    ident = (np.ones((TM, 2 * MIX_BLK)), np.zeros((TM, 2 * MIX_BLK)))
    return tuple(jnp.asarray(np.concatenate([np.tile(t, (1, reps)), tail], axis=0), F32)
                 for t, tail in ((cos, ident[0]), (sin_from_left, ident[1]), (sin_from_right, ident[1])))


def _rope_block(i):
    return (jnp.where(i < CTX_TILES, LAT_TILES_PER_SEQ, (i - CTX_TILES) % LAT_TILES_PER_SEQ), 0)


def projection_after_moe(dest, y_slots, x1, mod_prev, mod, layer, norm_g, w_in_bf16, rope, caches):
    rope_spec = pl.BlockSpec((TM, 2 * MIX_BLK), lambda i, d: _rope_block(i))
    mod_spec = pl.BlockSpec((1, 6, D_MODEL), lambda i, d: (_mod_row(i), 0, 0))
    n_in = 10
    return pl.pallas_call(
        _proj_after_moe_kernel,
        grid_spec=pltpu.PrefetchScalarGridSpec(
            num_scalar_prefetch=1,
            grid=(N_TILES,),
            in_specs=[pl.BlockSpec(memory_space=pl.ANY),
                      pl.BlockSpec((TM, D_MODEL), lambda i, d: (i, 0)),
                      mod_spec, mod_spec,
                      pl.BlockSpec((1, D_MODEL), lambda i, d: (0, 0)),
                      pl.BlockSpec((1, D_MODEL, PROJ_W), lambda i, d: (layer, 0, 0)),
                      rope_spec, rope_spec, rope_spec] + [pl.BlockSpec(memory_space=pl.ANY)] * 4,
            out_specs=[pl.BlockSpec((TM, D_MODEL), lambda i, d: (i, 0)),
                       pl.BlockSpec((TM, PM_W), lambda i, d: (i, 0)),
                       pl.BlockSpec((TM, PG_W), lambda i, d: (i, 0))] + [_cache_spec(layer)] * 4,
            scratch_shapes=_GATHER_SCRATCH,
        ),
        out_shape=[jax.ShapeDtypeStruct((T, D_MODEL), F32), jax.ShapeDtypeStruct((T, PM_W), BF16),
                   jax.ShapeDtypeStruct((T, PG_W), F32)] + [_CACHE_SHAPE] * 4,
        input_output_aliases={n_in + n: 3 + n for n in range(4)},
        compiler_params=_cparams(("arbitrary",)),
        name="projection_after_moe",
    )(dest, y_slots, x1, mod_prev, mod, norm_g[None, :], w_in_bf16, *rope, *caches)


def projection(x_ctx, x_lat, mod, norm_g, w_in_bf16, rope):
    layer = 0
    rope_spec = pl.BlockSpec((TM, 2 * MIX_BLK), _rope_block)
    return pl.pallas_call(
        _proj_kernel,
        grid=(N_TILES,),
        in_specs=[
            pl.BlockSpec((TM, D_MODEL), _ctx_tile),
            pl.BlockSpec((TM, D_MODEL), _lat_tile),
            pl.BlockSpec((1, 6, D_MODEL), lambda i: (_mod_row(i), 0, 0)),
            pl.BlockSpec((1, D_MODEL), lambda i: (0, 0)),
            pl.BlockSpec((1, D_MODEL, PROJ_W), lambda i: (layer, 0, 0)),
            rope_spec, rope_spec, rope_spec,
        ],
        out_specs=[pl.BlockSpec((TM, PM_W), lambda i: (i, 0)), pl.BlockSpec((TM, PG_W), lambda i: (i, 0))]
        + [_cache_spec(layer)] * 4,
        out_shape=[jax.ShapeDtypeStruct((T, PM_W), BF16), jax.ShapeDtypeStruct((T, PG_W), F32)]
        + [_CACHE_SHAPE] * 4,
        compiler_params=_cparams(("arbitrary",)),
        name="projection",
    )(x_ctx, x_lat, mod, norm_g[None, :], w_in_bf16, *rope)


LOG2_E = 1.4426950408889634


def _exp2_rows(s):
    e = jnp.exp2(s - jnp.max(s, axis=-1, keepdims=True))
    return e, 1.0 / jnp.sum(e, axis=-1, keepdims=True)


def _attn_kernel(lam_ref, q_ref, k_ref, v_ref, *rest, n_maps, post_scale, with_cache):
    if with_cache:
        kc_ref, vc_ref, g_ref, o_ref, kt_ref, vb_ref = rest
    else:
        g_ref, o_ref, kt_ref, vb_ref = rest

    @pl.when(pl.program_id(1) == 0)
    def _():
        k = k_ref[...].astype(F32)
        v = v_ref[...]
        if with_cache:
            k = jnp.concatenate([_cache_heads_on_lanes(kc_ref), k], axis=0)
            v = jnp.concatenate([_cache_heads_on_lanes(vc_ref).astype(BF16), v], axis=0)
        kt_ref[...] = k.T.astype(BF16)
        vb_ref[...] = v

    lane = _head_lanes()
    map_dim = HEAD_DIM // n_maps
    q = q_ref[...].astype(F32) * (map_dim ** -0.5 * LOG2_E)
    kt = kt_ref[...]
    vb = vb_ref[...]
    stack_rows = kt_ref.shape[1] <= TM
    weights = []
    for h in range(HEADS):
        masked = [jnp.where(_lane_range(lane, h * HEAD_DIM + j * map_dim, map_dim), q, 0.0).astype(BF16)
                  for j in range(n_maps)]
        if stack_rows:
            s = jnp.dot(jnp.concatenate(masked, axis=0), kt, preferred_element_type=F32)
            scores = [s[j * TM:(j + 1) * TM] for j in range(n_maps)]
        else:
            scores = [jnp.dot(m, kt, preferred_element_type=F32) for m in masked]
        parts = [_exp2_rows(x) for x in scores]
        w = parts[0][0] * parts[0][1]
        if n_maps == 2:
            w = w - parts[1][0] * (lam_ref[0] * parts[1][1])
        weights.append(w.astype(BF16))
    if stack_rows:
        oh = jnp.dot(jnp.concatenate(weights, axis=0), vb, preferred_element_type=F32)
        outs = [oh[h * TM:(h + 1) * TM] for h in range(HEADS)]
    else:
        outs = [jnp.dot(w, vb, preferred_element_type=F32) for w in weights]
    o = jnp.zeros(q.shape, F32)
    for h in range(HEADS):
        o = jnp.where(_lane_range(lane, h * HEAD_DIM, HEAD_DIM), outs[h], o)
    if n_maps == 2:
        o = o * lax.rsqrt(_head_mean_square(o) + EPS) * g_ref[...] * post_scale
    o_ref[...] = o


def _cache_block_spec(layer):
    return pl.BlockSpec((1, 1, HEADS, PAST_LEN, HEAD_DIM), lambda b, i: (b, layer, 0, 0, 0))


def _cache_heads_on_lanes(ref):
    return jnp.concatenate([ref[0, 0, h] for h in range(HEADS)], axis=1)


def attention(p, row_blk0, cols, n_seq, seq_len, lam, norm_g, *, n_maps, post_scale, cache=None, layer=0):
    nb = seq_len // TM
    kv_len = seq_len + (PAST_LEN if cache is not None else 0)
    kern = functools.partial(_attn_kernel, n_maps=n_maps, post_scale=post_scale, with_cache=cache is not None)
    kv_spec = lambda col: pl.BlockSpec((seq_len, MIX_BLK), lambda b, i: (row_blk0 + b, col))
    cache_specs = [_cache_block_spec(layer)] * 2 if cache is not None else []
    return Part(
        kernel=kern,
        in_specs=[
            pl.BlockSpec(memory_space=pltpu.SMEM),
            pl.BlockSpec((TM, MIX_BLK), lambda b, i: ((row_blk0 + b) * nb + i, cols[0])),
            kv_spec(cols[1]), kv_spec(cols[2]), *cache_specs,
            pl.BlockSpec((1, MIX_BLK), lambda b, i: (0, 0)),
        ],
        out_specs=[pl.BlockSpec((TM, MIX_BLK), lambda b, i: (b * nb + i, 0))],
        out_shapes=[jax.ShapeDtypeStruct((n_seq * seq_len, MIX_BLK), F32)],
        scratch=[pltpu.VMEM((MIX_BLK, kv_len), BF16), pltpu.VMEM((kv_len, MIX_BLK), BF16)],
        args=[lam, p, p, p, *(cache or ()), norm_g])


def _na_slab_start(i):
    return jnp.clip(i - 1, 0, GRID_H // 4 - NA_SLAB_ROWS // 4)


def _na_kernel(q_ref, k_ref, v_ref, kc_ref, vc_ref, bias_ref, o_ref):
    i = pl.program_id(1)
    start = pl.multiple_of(_na_slab_start(i) * TM, TM)
    ks_t = k_ref[pl.ds(start, NA_SLAB), :].astype(F32).T.astype(BF16)
    vs = v_ref[pl.ds(start, NA_SLAB), :]
    kc_t = _cache_heads_on_lanes(kc_ref).T.astype(BF16)
    vc = _cache_heads_on_lanes(vc_ref).astype(BF16)
    q = q_ref[...].astype(F32) * (HEAD_DIM ** -0.5)
    lane = _head_lanes()
    heads = [_lane_range(lane, h * HEAD_DIM, HEAD_DIM) for h in range(HEADS)]
    qm = jnp.concatenate([jnp.where(in_head, q, 0.0) for in_head in heads], axis=0).astype(BF16)
    s_loc = jnp.dot(qm, ks_t, preferred_element_type=F32)
    s_ctx = jnp.dot(qm, kc_t, preferred_element_type=F32)
    e_locs, e_ctxs, dens = [], [], []
    for h in range(HEADS):
        rows = slice(h * TM, (h + 1) * TM)
        sl = s_loc[rows] + bias_ref[0, h]
        sc = s_ctx[rows]
        m = jnp.maximum(jnp.max(sl, axis=-1, keepdims=True), jnp.max(sc, axis=-1, keepdims=True))
        e_loc = jnp.exp(sl - m)
        e_ctx = jnp.exp(sc - m)
        dens.append(jnp.sum(e_loc, axis=-1, keepdims=True) + jnp.sum(e_ctx, axis=-1, keepdims=True))
        e_locs.append(e_loc.astype(BF16))
        e_ctxs.append(e_ctx.astype(BF16))
    oh = (jnp.dot(jnp.concatenate(e_locs, axis=0), vs, preferred_element_type=F32)
          + jnp.dot(jnp.concatenate(e_ctxs, axis=0), vc, preferred_element_type=F32))
    o = jnp.zeros(q.shape, F32)
    for h in range(HEADS):
        o = jnp.where(heads[h], oh[h * TM:(h + 1) * TM] / dens[h], o)
    o_ref[...] = o


def _na_bias_tables(rpb):
    n_dr, n_dc = 2 * NA_WIN_H - 1, 2 * NA_WIN_W - 1
    cq = np.arange(GRID_W)[:, None]
    ck = np.arange(GRID_W)[None, :]
    wc0 = np.clip(cq - NA_WIN_W // 2, 0, GRID_W - NA_WIN_W)
    col_ok = (ck >= wc0) & (ck < wc0 + NA_WIN_W)
    col_pick = np.clip(ck - cq + NA_WIN_W - 1, 0, n_dc - 1)[..., None] == np.arange(n_dc)
    by_col = jnp.einsum("hab,qcb->haqc", rpb.astype(F32), jnp.asarray(col_pick, F32), precision=HIGHEST)
    margin = 4
    by_col = jnp.pad(by_col.transpose(0, 2, 1, 3), ((0, 0), (0, 0), (margin, margin), (0, 0)))
    by_col = by_col.reshape(HEADS, GRID_W, (n_dr + 2 * margin) * GRID_W)
    pieces, row_ok = [], []
    for tile in (0, 1, GRID_H // 4 - 1):
        slab0 = int(np.clip(tile - 1, 0, GRID_H // 4 - NA_SLAB_ROWS // 4)) * 4
        rq = tile * 4 + np.arange(4)
        rk = (slab0 + np.arange(NA_SLAB) // GRID_W)[None, :]
        wr0 = np.clip(rq - NA_WIN_H // 2, 0, GRID_H - NA_WIN_H)[:, None]
        row_ok.append((rk >= wr0) & (rk < wr0 + NA_WIN_H))
        for r in rq:
            first = slab0 - int(r) + NA_WIN_H - 1 + margin
            assert 0 <= first and first + NA_SLAB_ROWS <= n_dr + 2 * margin
            pieces.append(by_col[:, :, first * GRID_W:first * GRID_W + NA_SLAB])
    table = jnp.stack(pieces).reshape(3, 4, HEADS, GRID_W, NA_SLAB).transpose(0, 2, 1, 3, 4)
    valid = np.stack(row_ok)[:, None, :, None, :] & np.tile(col_ok, (1, NA_SLAB_ROWS))[None, None, None]
    table = jnp.where(jnp.asarray(valid), table, NEG_BIG)
    return table.reshape(3, HEADS, TM, NA_SLAB)


def na_latent(p, kc, vc, layer, bias):
    n_t = LAT_TILES_PER_SEQ
    seq_blk0 = TP // DEC_SEQ

    def bias_idx(b, i):
        return (jnp.minimum(i, 1) + i // (n_t - 1), 0, 0, 0)

    return Part(
        kernel=_na_kernel,
        in_specs=[
            pl.BlockSpec((TM, MIX_BLK), lambda b, i: (CTX_TILES + b * n_t + i, M_CQ)),
            pl.BlockSpec((DEC_SEQ, MIX_BLK), lambda b, i: (seq_blk0 + b, M_CK)),
            pl.BlockSpec((DEC_SEQ, MIX_BLK), lambda b, i: (seq_blk0 + b, M_CV)),
            _cache_block_spec(layer), _cache_block_spec(layer),
            pl.BlockSpec((1, HEADS, TM, NA_SLAB), bias_idx),
        ],
        out_specs=[pl.BlockSpec((TM, MIX_BLK), lambda b, i: (b * n_t + i, 0))],
        out_shapes=[jax.ShapeDtypeStruct((TL, MIX_BLK), F32)],
        scratch=[],
        args=[p, p, p, kc, vc, bias])


MAX_EXPONENT = 80.0


def _hgrn_direction(q_ref, f_ref, v_ref, lb, st_ref, o_ref, reverse):
    n_ch = TM // B_CHUNK
    r_idx = lax.broadcasted_iota(jnp.int32, (TM, TM), 0)
    c_idx = lax.broadcasted_iota(jnp.int32, (TM, TM), 1)
    tri = (c_idx >= r_idx) if reverse else (c_idx <= r_idx)
    zq = q_ref[...]
    q = zq * jax.nn.sigmoid(zq)
    z = f_ref[...]
    gate = (1.0 - lb) * jax.nn.sigmoid(z)
    logf = jnp.log(lb + gate)
    kk = (1.0 - lb) - gate
    b = _select_sum_left(jnp.where(tri, 1.0, 0.0).astype(BF16), logf)
    b3 = b.reshape(n_ch, B_CHUNK, MIX_BLK)
    mid = B_CHUNK // 2 if reverse else B_CHUNK // 2 - 1
    q_in = (q.reshape(b3.shape) * jnp.exp(b3 - b3[:, mid:mid + 1, :])).reshape(TM, MIX_BLK)
    q_dec = (q * jnp.exp(b)).astype(BF16)
    b_t = b.T
    kk_t = kk.T
    far = 0 if reverse else TM - 1
    b_far = b_t[:, far:far + 1]
    k_dec_t = (kk_t * jnp.exp(b_far - b_t)).astype(BF16)
    vb = v_ref[...].astype(BF16)
    st = st_ref[...]
    o_state = jnp.dot(q_dec, st.astype(BF16), preferred_element_type=F32)
    kv = jnp.dot(k_dec_t, vb, preferred_element_type=F32)
    st_ref[...] = st * jnp.exp(b_far) + jnp.where(_same_head_matrix(), kv, 0.0)
    lane = _head_lanes()
    token = lax.broadcasted_iota(jnp.int32, (1, TM), 1)
    local = lax.broadcasted_iota(jnp.int32, (HEADS * B_CHUNK, 1), 0) % B_CHUNK
    heads = [_lane_range(lane, h * HEAD_DIM, HEAD_DIM) for h in range(HEADS)]
    weights = []
    for c in range(n_ch):
        ref = b_t[:, c * B_CHUNK + mid:c * B_CHUNK + mid + 1]
        k_c_t = (kk_t * jnp.exp(jnp.minimum(ref - b_t, MAX_EXPONENT))).astype(BF16)
        q_c = q_in[c * B_CHUNK:(c + 1) * B_CHUNK, :]
        lhs = jnp.concatenate([jnp.where(in_head, q_c, 0.0) for in_head in heads], axis=0)
        a = jnp.dot(lhs.astype(BF16), k_c_t, preferred_element_type=F32)
        t_abs = c * B_CHUNK + local
        weights.append(jnp.where((token >= t_abs) if reverse else (token <= t_abs), a, 0.0).astype(BF16))
    res = jnp.dot(jnp.concatenate(weights, axis=0), vb, preferred_element_type=F32)
    for c in range(n_ch):
        rows = slice(c * B_CHUNK, (c + 1) * B_CHUNK)
        o_c = o_state[rows, :]
        for h, in_head in enumerate(heads):
            lo = (c * HEADS + h) * B_CHUNK
            o_c = o_c + jnp.where(in_head, res[lo:lo + B_CHUNK, :], 0.0)
        o_ref[rows, :] = o_c


def _hgrn_kernel(qf_ref, ff_ref, vf_ref, qb_ref, fb_ref, vb_ref, lb_ref, s0_ref,
                 of_ref, ob_ref, s_ref, stf_ref, stb_ref, *, has_s0):
    j = pl.program_id(1)

    @pl.when(j == 0)
    def _():
        if has_s0:
            stf_ref[...] = s0_ref[0, 0]
            stb_ref[...] = s0_ref[0, 1]
        else:
            stf_ref[...] = jnp.zeros((MIX_BLK, MIX_BLK), F32)
            stb_ref[...] = jnp.zeros((MIX_BLK, MIX_BLK), F32)

    lb = lb_ref[...]
    _hgrn_direction(qf_ref, ff_ref, vf_ref, lb[0:1], stf_ref, of_ref, False)
    _hgrn_direction(qb_ref, fb_ref, vb_ref, lb[1:2], stb_ref, ob_ref, True)

    @pl.when(j == pl.num_programs(1) - 1)
    def _():
        for d, st_ref in enumerate((stf_ref, stb_ref)):
            s = st_ref[...]
            for hd in range(HEADS):
                lo = hd * HEAD_DIM
                s_ref[0, d, hd] = s[lo:lo + HEAD_DIM, lo:lo + HEAD_DIM]


def hgrn(p, row_tile0, n_seq, seq_len, lb, s0):
    nb = seq_len // TM
    has_s0 = s0 is not None
    if s0 is None:
        s0 = jnp.zeros((1, 2, MIX_BLK, MIX_BLK), F32)

    def fwd(col):
        return pl.BlockSpec((TM, MIX_BLK), lambda s, j: (row_tile0 + s * nb + j, col))

    def bwd(col):
        return pl.BlockSpec((TM, MIX_BLK), lambda s, j: (row_tile0 + s * nb + nb - 1 - j, col))

    state_spec = pl.BlockSpec((1, 2, MIX_BLK, MIX_BLK), lambda s, j: (s if has_s0 else 0, 0, 0, 0))
    out_rows = n_seq * seq_len
    return Part(
        kernel=functools.partial(_hgrn_kernel, has_s0=has_s0),
        in_specs=[fwd(G_BQ), fwd(G_BFF), fwd(G_BV), bwd(G_BQ), bwd(G_BFB), bwd(G_BV),
                  pl.BlockSpec((2, MIX_BLK), lambda s, j: (0, 0)), state_spec],
        out_specs=[
            pl.BlockSpec((TM, MIX_BLK), lambda s, j: (s * nb + j, 0)),
            pl.BlockSpec((TM, MIX_BLK), lambda s, j: (s * nb + nb - 1 - j, 0)),
            pl.BlockSpec((1, 2, HEADS, HEAD_DIM, HEAD_DIM), lambda s, j: (s, 0, 0, 0, 0)),
        ],
        out_shapes=[jax.ShapeDtypeStruct((out_rows, MIX_BLK), F32),
                    jax.ShapeDtypeStruct((out_rows, MIX_BLK), F32),
                    jax.ShapeDtypeStruct((n_seq, 2, HEADS, HEAD_DIM, HEAD_DIM), F32)],
        scratch=[pltpu.VMEM((MIX_BLK, MIX_BLK), F32), pltpu.VMEM((MIX_BLK, MIX_BLK), F32)],
        args=[p, p, p, p, p, p, lb, s0])


def _state_to_blockdiag(s):
    eye = jnp.eye(HEADS, dtype=F32)
    full = s.astype(F32)[:, :, :, :, None, :] * eye[None, None, :, None, :, None]
    return full.reshape(s.shape[0], 2, MIX_BLK, MIX_BLK)


def _fft_kernel(u_ref, cs64_ref, csl_ref, o_ref, ab_ref, *, norm):
    seq_len = u_ref.shape[0]

    @pl.when(pl.program_id(1) == 0)
    def _():
        ab = jnp.dot(u_ref[...], cs64_ref[...], preferred_element_type=F32).astype(BF16)
        ab_ref[:seq_len, :] = ab[:, :MIX_BLK]
        ab_ref[seq_len:, :] = ab[:, MIX_BLK:]

    o_ref[...] = jnp.dot(csl_ref[...], ab_ref[...], preferred_element_type=F32) * norm


def _dft_tables(n):
    k = np.arange(n)
    ang = 2.0 * np.pi * ((k[:, None] * k[None, :]) % n) / n
    return np.cos(ang), np.sin(ang)


def _dft_constants(seq_len):
    c64, s64 = _dft_tables(HEAD_DIM)
    eye = np.eye(HEADS)
    cl, sl = _dft_tables(seq_len)
    as_bf16 = lambda a: jnp.asarray(a, F32).astype(BF16)
    return (as_bf16(np.concatenate([np.kron(eye, c64), -np.kron(eye, s64)], axis=1)),
            as_bf16(np.concatenate([cl, sl], axis=1)))


def fourier_mix(p, row_blk0, n_seq, seq_len, consts):
    cs64, csl = consts
    nb = seq_len // TM
    norm = 1.0 / math.sqrt(seq_len * HEAD_DIM)
    return Part(
        kernel=functools.partial(_fft_kernel, norm=norm),
        in_specs=[
            pl.BlockSpec((seq_len, MIX_BLK), lambda s, i: (row_blk0 + s, M_DU)),
            pl.BlockSpec((MIX_BLK, 2 * MIX_BLK), lambda s, i: (0, 0)),
            pl.BlockSpec((TM, 2 * seq_len), lambda s, i: (i, 0)),
        ],
        out_specs=[pl.BlockSpec((TM, MIX_BLK), lambda s, i: (s * nb + i, 0))],
        out_shapes=[jax.ShapeDtypeStruct((n_seq * seq_len, MIX_BLK), F32)],
        scratch=[pltpu.VMEM((2 * seq_len, MIX_BLK), BF16)],
        args=[p, cs64, csl])


def _route(logits_t, rb):
    per = N_EXPERTS // N_GROUPS
    score = [jax.nn.sigmoid(logits_t[e:e + 1, :]) for e in range(N_EXPERTS)]
    sel = [score[e] + rb[e:e + 1, :] for e in range(N_EXPERTS)]
    gscore = []
    for g in range(N_GROUPS):
        vals = sel[g * per:(g + 1) * per]
        best = None
        for a in range(per):
            for b in range(a + 1, per):
                pair = vals[a] + vals[b]
                best = pair if best is None else jnp.maximum(best, pair)
        gscore.append(best)
    chosen = []
    for g in range(N_GROUPS):
        ok = None
        for j in range(N_GROUPS):
            if j == g:
                continue
            cond = gscore[g] > gscore[j] if j < g else gscore[g] >= gscore[j]
            ok = cond if ok is None else ok & cond
        chosen.append(ok)
    picked = []
    for e in range(N_EXPERTS):
        g = e // per
        rank = jnp.zeros_like(sel[e])
        for j in range(g * per, (g + 1) * per):
            if j == e:
                continue
            ahead = sel[j] >= sel[e] if j < e else sel[j] > sel[e]
            rank = rank + jnp.where(ahead, 1.0, 0.0)
        picked.append(chosen[g] & (rank < 2.0))
    wsum = jnp.zeros_like(score[0])
    for e in range(N_EXPERTS):
        wsum = wsum + jnp.where(picked[e], score[e], 0.0)
    bucket = jnp.zeros_like(wsum)
    w_a = jnp.zeros_like(wsum)
    w_b = jnp.zeros_like(wsum)
    for g in range(N_GROUPS):
        for n, (a, b) in enumerate(EXPERT_PAIRS):
            hit = picked[g * per + a] & picked[g * per + b]
            bucket = jnp.where(hit, float(g * len(EXPERT_PAIRS) + n), bucket)
            w_a = jnp.where(hit, score[g * per + a] / wsum, w_a)
            w_b = jnp.where(hit, score[g * per + b] / wsum, w_b)
    return bucket, w_a, w_b


def _out_kernel(*refs):
    streams, rest = refs[:12], refs[12:]
    (bg_ref, mod_ref, hg_ref, w_ref, g2_ref, rw_ref, rb_ref,
     x1_ref, h2_ref, bucket_ref, rank_ref, counts_ref, run_ref) = rest
    is_ctx = pl.program_id(0) < TP // TM_OUT
    x, o_a, o_f, o_b, o_c, o_d = (jnp.where(is_ctx, streams[2 * n][...], streams[2 * n + 1][...])
                                  for n in range(6))

    @pl.when(pl.program_id(0) == 0)
    def _():
        run_ref[...] = jnp.zeros(run_ref.shape, F32)

    mod = mod_ref[0]
    hb = o_f + o_b
    zg = bg_ref[...]
    hb = hb * lax.rsqrt(_head_mean_square(hb) + EPS) * hg_ref[...] * (zg * jax.nn.sigmoid(zg))
    mixers = jnp.concatenate([part.astype(BF16) for part in (o_a, hb, o_c, o_d)], axis=1)
    mixed = jnp.dot(mixers, w_ref[0], preferred_element_type=F32)
    x1 = x + mod[2:3] * mixed
    x1_ref[...] = x1
    ms = jnp.mean(x1 * x1, axis=-1, keepdims=True)
    h2 = x1 * lax.rsqrt(ms + EPS) * g2_ref[...] * (1.0 + mod[4:5]) + mod[3:4]
    rw = rw_ref[...]
    r = jnp.dot(jnp.concatenate(_bf16_pieces(h2, 2), axis=0), rw, preferred_element_type=F32)
    r = r[:TM_OUT] + r[TM_OUT:]
    bucket, w_a, w_b = _route((r[:, :LANES] + r[:, LANES:]).T, rb_ref[...])
    h2_ref[:, :D_MODEL] = h2
    h2_ref[:, D_MODEL:] = jnp.concatenate([w_a, w_b, jnp.zeros((LANES - 2, TM_OUT), F32)], axis=0).T
    onehot = jnp.where(lax.broadcasted_iota(jnp.int32, (BUCKET_ROWS, 1), 0).astype(F32) == bucket, 1.0, 0.0)
    s_idx = lax.broadcasted_iota(jnp.int32, (TM_OUT, TM_OUT), 0)
    t_idx = lax.broadcasted_iota(jnp.int32, (TM_OUT, TM_OUT), 1)
    prefix = jnp.dot(onehot.astype(BF16), jnp.where(s_idx <= t_idx, 1.0, 0.0).astype(BF16),
                     preferred_element_type=F32)
    run = run_ref[...]
    rank = jnp.sum(onehot * (prefix - 1.0 + run[:, 0:1]), axis=0, keepdims=True)
    run = run + jnp.sum(onehot, axis=1, keepdims=True)
    run_ref[...] = run
    bucket_ref[...] = bucket.astype(jnp.int32)
    rank_ref[...] = rank.astype(jnp.int32)
    counts_ref[...] = run


def out_and_route(x_pair, x_is_combined, mixer_pairs, p, mod, layer, hgrn_g, w_out_bf16, norm2_g, router_pieces,
                  router_b):
    n_ctx = TP // TM_OUT
    ctx_tile = lambda i: (jnp.minimum(i, n_ctx - 1), 0)
    lat_tile = lambda i: (jnp.maximum(i - n_ctx, 0), 0)
    mod_row = lambda i: jnp.where(i < n_ctx, 0, 1 + (i - n_ctx) // (DEC_SEQ // TM_OUT))
    tile = lambda w: pl.BlockSpec((TM_OUT, w), lambda i: (i, 0))
    full = lambda r, c: pl.BlockSpec((r, c), lambda i: (0, 0))
    stream_specs = [pl.BlockSpec((TM_OUT, D_MODEL), ctx_tile),
                    pl.BlockSpec((TM_OUT, D_MODEL),
                                 (lambda i: (jnp.maximum(i, n_ctx), 0)) if x_is_combined else lat_tile)]
    stream_args = list(x_pair)
    for o_ctx, o_lat in mixer_pairs:
        stream_specs += [pl.BlockSpec((TM_OUT, MIX_BLK), ctx_tile), pl.BlockSpec((TM_OUT, MIX_BLK), lat_tile)]
        stream_args += [o_ctx, o_lat]
    return pl.pallas_call(
        _out_kernel,
        grid=(T // TM_OUT,),
        in_specs=stream_specs + [
            pl.BlockSpec((TM_OUT, MIX_BLK), lambda i: (i, G_BG)),
            pl.BlockSpec((1, 6, D_MODEL), lambda i: (mod_row(i), 0, 0)),
            full(1, MIX_BLK), pl.BlockSpec((1, D_MODEL, D_MODEL), lambda i: (layer, 0, 0)), full(1, D_MODEL),
            full(D_MODEL, 2 * LANES), full(N_EXPERTS, 1),
        ],
        out_specs=[tile(D_MODEL), tile(ROW_W), pl.BlockSpec((1, TM_OUT), lambda i: (0, i)),
                   pl.BlockSpec((1, TM_OUT), lambda i: (0, i)), full(BUCKET_ROWS, LANES)],
        out_shape=[jax.ShapeDtypeStruct((T, D_MODEL), F32),
                   jax.ShapeDtypeStruct((T, ROW_W), F32),
                   jax.ShapeDtypeStruct((1, T), jnp.int32),
                   jax.ShapeDtypeStruct((1, T), jnp.int32),
                   jax.ShapeDtypeStruct((BUCKET_ROWS, LANES), F32)],
        scratch_shapes=[pltpu.VMEM((BUCKET_ROWS, LANES), F32)],
        compiler_params=_cparams(("arbitrary",)),
        name="out_and_route",
    )(*stream_args, p, mod, jnp.tile(hgrn_g, HEADS)[None, :], w_out_bf16,
      norm2_g[None, :], router_pieces, router_b[:, None])


def _router_pieces(router_w):
    hi, lo = _bf16_pieces(router_w.astype(F32), 2)
    pad = lambda a: jnp.pad(a, ((0, 0), (0, LANES - N_EXPERTS)))
    return jnp.concatenate([pad(hi), pad(lo)], axis=1)


def routing_plan(bucket, rank, counts):
    counts = counts[:N_BUCKETS, 0].astype(jnp.int32)
    n_tiles = (counts + TM_MOE - 1) // TM_MOE
    tile_end = jnp.cumsum(n_tiles)
    tile_start = tile_end - n_tiles
    buckets = jnp.arange(N_BUCKETS, dtype=jnp.int32)
    start_of_token = jnp.sum(jnp.where(bucket[0][:, None] == buckets[None, :], tile_start[None, :], 0), axis=1)
    dest = start_of_token * TM_MOE + rank[0]
    tiles = jnp.arange(MAX_TILES, dtype=jnp.int32)
    valid = tiles < tile_end[-1]
    tile_bucket = jnp.sum((jnp.minimum(tiles, tile_end[-1] - 1)[:, None] >= tile_end[None, :]).astype(jnp.int32), axis=1)
    pair_a = np.array([a for a, _ in EXPERT_PAIRS], np.int32)
    pair_b = np.array([b for _, b in EXPERT_PAIRS], np.int32)
    per = N_EXPERTS // N_GROUPS
    exp_a = jnp.asarray((np.arange(N_BUCKETS) // len(EXPERT_PAIRS)) * per + np.tile(pair_a, N_GROUPS), jnp.int32)
    exp_b = jnp.asarray((np.arange(N_BUCKETS) // len(EXPERT_PAIRS)) * per + np.tile(pair_b, N_GROUPS), jnp.int32)
    pick = tile_bucket[:, None] == buckets[None, :]
    tile_a = jnp.sum(jnp.where(pick, exp_a[None, :], 0), axis=1)
    tile_b = jnp.sum(jnp.where(pick, exp_b[None, :], 0), axis=1)
    return dest.astype(jnp.int32), tile_a, tile_b, valid.astype(jnp.int32), (tile_end[-1:] - 1).astype(jnp.int32)


def _row_copy(src, src_row, dst, dst_row, sem):
    return pltpu.make_async_copy(src.at[pl.ds(src_row, 1), :], dst.at[pl.ds(dst_row, 1), :], sem)


def _scatter_kernel(dest_ref, h_ref, init_ref, o_ref, sem):
    del init_ref
    base = pl.program_id(0) * TM_SCATTER

    for r in range(TM_SCATTER):
        _row_copy(h_ref, r, o_ref, dest_ref[base + r], sem).start(priority=r % N_DMA_PRIORITIES)
    pltpu.make_async_copy(h_ref, o_ref.at[pl.ds(0, TM_SCATTER), :], sem).wait()


def scatter_to_slots(h2, dest, slots):
    return pl.pallas_call(
        _scatter_kernel,
        grid_spec=pltpu.PrefetchScalarGridSpec(
            num_scalar_prefetch=1,
            grid=(T // TM_SCATTER,),
            in_specs=[pl.BlockSpec((TM_SCATTER, ROW_W), lambda i, d: (i, 0)),
                      pl.BlockSpec(memory_space=pl.ANY)],
            out_specs=pl.BlockSpec(memory_space=pl.ANY),
            scratch_shapes=[pltpu.SemaphoreType.DMA(())],
        ),
        out_shape=jax.ShapeDtypeStruct((N_SLOTS, ROW_W), F32),
        input_output_aliases={2: 0},
        compiler_params=_cparams(("arbitrary",)),
        name="scatter_to_slots",
    )(dest, h2, slots)


def _moe_kernel(ta_ref, tb_ref, valid_ref, last_ref, h_ref, wga_ref, wua_ref, wda_ref, wgb_ref, wub_ref, wdb_ref,
                o_ref):
    del ta_ref, tb_ref, last_ref
    i = pl.program_id(0)

    @pl.when(valid_ref[i] == 1)
    def _():
        x = h_ref[:, :D_MODEL].astype(BF16)
        gates = h_ref[:, D_MODEL:]
        y = jnp.zeros((TM_MOE, D_MODEL), F32)
        for n, (wg, wu, wd) in enumerate(((wga_ref, wua_ref, wda_ref), (wgb_ref, wub_ref, wdb_ref))):
            a = jnp.dot(x, wg[0, 0].astype(BF16), preferred_element_type=F32)
            u = jnp.dot(x, wu[0, 0].astype(BF16), preferred_element_type=F32)
            z = a * jax.nn.sigmoid(a) * u * gates[:, n:n + 1]
            y = y + jnp.dot(z.astype(BF16), wd[0, 0].astype(BF16), preferred_element_type=F32)
        o_ref[:, 0, :] = y

    @pl.when(valid_ref[i] == 0)
    def _():
        o_ref[...] = jnp.zeros((TM_MOE, 1, D_MODEL), F32)


def moe(h_slots, tile_a, tile_b, valid, last, layer, wg, wu, wd):
    up_a = pl.BlockSpec((1, 1, D_MODEL, D_EXPERT), lambda i, ta, tb, v, last: (layer, ta[i], 0, 0))
    up_b = pl.BlockSpec((1, 1, D_MODEL, D_EXPERT), lambda i, ta, tb, v, last: (layer, tb[i], 0, 0))
    down_a = pl.BlockSpec((1, 1, D_EXPERT, D_MODEL), lambda i, ta, tb, v, last: (layer, ta[i], 0, 0))
    down_b = pl.BlockSpec((1, 1, D_EXPERT, D_MODEL), lambda i, ta, tb, v, last: (layer, tb[i], 0, 0))
    return pl.pallas_call(
        _moe_kernel,
        grid_spec=pltpu.PrefetchScalarGridSpec(
            num_scalar_prefetch=4,
            grid=(MAX_TILES,),
            in_specs=[pl.BlockSpec((TM_MOE, ROW_W), lambda i, ta, tb, v, last: (jnp.minimum(i, last[0]), 0)),
                      up_a, up_a, down_a, up_b, up_b, down_b],
            out_specs=pl.BlockSpec((TM_MOE, 1, D_MODEL), lambda i, ta, tb, v, last: (i, 0, 0)),
        ),
        out_shape=jax.ShapeDtypeStruct((N_SLOTS, 1, D_MODEL), F32),
        compiler_params=_cparams(("arbitrary",)),
        name="moe",
    )(tile_a, tile_b, valid, last, h_slots, wg, wu, wd, wg, wu, wd)


def _gather_tile(dest_ref, y_ref, buf_ref, sem, tile, slot):
    for r in range(TM):
        pltpu.make_async_copy(y_ref.at[pl.ds(dest_ref[tile * TM + r], 1)],
                              buf_ref.at[slot, pl.ds(r, 1)], sem.at[slot]).start(priority=r % N_DMA_PRIORITIES)


def _moe_residual(dest_ref, y_ref, buf_ref, sem, x1_ref, mod_ref):
    i = pl.program_id(0)
    slot = i % 2

    @pl.when(i == 0)
    def _():
        _gather_tile(dest_ref, y_ref, buf_ref, sem, 0, 0)

    @pl.when(i + 1 < pl.num_programs(0))
    def _():
        _gather_tile(dest_ref, y_ref, buf_ref, sem, i + 1, 1 - slot)

    pltpu.make_async_copy(y_ref.at[pl.ds(0, TM)], buf_ref.at[slot], sem.at[slot]).wait()
    return x1_ref[...] + mod_ref[0][5:6] * buf_ref[slot, :, 0, :]


def _final_kernel(dest_ref, y_ref, x1_ref, mod_ref, g_ref, oc_ref, ol_ref, buf_ref, sem):
    x2 = _moe_residual(dest_ref, y_ref, buf_ref, sem, x1_ref, mod_ref)
    ms = jnp.mean(x2 * x2, axis=-1, keepdims=True)
    y = x2 * lax.rsqrt(ms + EPS) * g_ref[...]

    @pl.when(_is_ctx_tile())
    def _():
        oc_ref[...] = y

    @pl.when(jnp.logical_not(_is_ctx_tile()))
    def _():
        ol_ref[...] = y


_GATHER_SCRATCH = [pltpu.VMEM((2, TM, 1, D_MODEL), F32), pltpu.SemaphoreType.DMA((2,))]


def final_norm(dest, y_slots, x1, mod, final_g):
    return pl.pallas_call(
        _final_kernel,
        grid_spec=pltpu.PrefetchScalarGridSpec(
            num_scalar_prefetch=1,
            grid=(N_TILES,),
            in_specs=[pl.BlockSpec(memory_space=pl.ANY),
                      pl.BlockSpec((TM, D_MODEL), lambda i, d: (i, 0)),
                      pl.BlockSpec((1, 6, D_MODEL), lambda i, d: (_mod_row(i), 0, 0)),
                      pl.BlockSpec((1, D_MODEL), lambda i, d: (0, 0))],
            out_specs=[pl.BlockSpec((TM, D_MODEL), _ctx_tile), pl.BlockSpec((TM, D_MODEL), _lat_tile)],
            scratch_shapes=_GATHER_SCRATCH,
        ),
        out_shape=[jax.ShapeDtypeStruct((TP, D_MODEL), F32), jax.ShapeDtypeStruct((TL, D_MODEL), F32)],
        compiler_params=_cparams(("arbitrary",)),
        name="final_norm",
    )(dest, y_slots, x1, mod, final_g[None, :])


def kernel(x_prompt, x_sample, cache_diff_k, cache_diff_v, cache_na_k, cache_na_v, state_hgrn, c, c_ctx,
           norm1_g, norm2_g, ada_w, ada_b, w_in, w_out, diff_lambda, diff_subln_g, hgrn_lb_logits,
           hgrn_norm_g, na_rpb, router_w, router_b, moe_w_gate, moe_w_up, moe_w_down, final_norm_g):
    assert SEQ == TM and PAST_LEN == TM and DEC_SEQ % TM_OUT == 0 and TP % DEC_SEQ == 0
    x_pair = (x_prompt.reshape(TP, D_MODEL), x_sample.reshape(TL, D_MODEL))
    w_in_bf16 = jnp.concatenate([w_in[:, :, c * MIX_BLK:(c + 1) * MIX_BLK] for c in PM_BLOCKS + PG_BLOCKS],
                                axis=-1).astype(BF16)
    w_out_bf16 = w_out.astype(BF16)
    router_pieces = _router_pieces(router_w)
    mods = modulation(jnp.concatenate([c_ctx[None, :], c], axis=0), ada_w, ada_b)
    mods = mods.reshape(DEPTH, 3, 6, D_MODEL)
    lb_sm = jax.nn.softmax(hgrn_lb_logits.astype(F32), axis=0)
    lb_all = jnp.cumsum(lb_sm, axis=0) - lb_sm[0:1]
    rope = _rope_tables()
    dft_ctx = _dft_constants(SEQ)
    dft_lat = _dft_constants(DEC_SEQ)
    lat_blk0 = TP // DEC_SEQ
    states = []
    moe_state = None
    for l in range(DEPTH):
        if moe_state is None:
            pm, pg, *new_kv = projection(*x_pair, mods[l], norm1_g[l], w_in_bf16, rope)
        else:
            x, pm, pg, *new_kv = projection_after_moe(*moe_state, mods[l - 1], mods[l], l, norm1_g[l], w_in_bf16, rope,
                                                 new_kv)
            x_pair = (x, x)

        lq = diff_lambda[l].astype(F32)
        lam_init = 0.8 - 0.6 * math.exp(-0.3 * l)
        lam = (jnp.exp(jnp.sum(lq[0] * lq[1])) - jnp.exp(jnp.sum(lq[2] * lq[3])) + lam_init).reshape(1)
        subln = jnp.tile(diff_subln_g[l], HEADS)[None, :]
        diff = functools.partial(attention, pm, cols=(M_AQ, M_AK, M_AV), lam=lam, norm_g=subln, n_maps=2,
                                 post_scale=1.0 - lam_init)
        (oa_ctx,), (of_ctx, ob_ctx, st_ctx), (oc_ctx,), (od_ctx,) = run_parts(
            [diff(row_blk0=0, n_seq=BATCH, seq_len=SEQ),
             hgrn(pg, 0, BATCH, SEQ, lb_all[l], None),
             attention(pm, 0, (M_CQ, M_CK, M_CV), BATCH, SEQ, lam, subln, n_maps=1, post_scale=1.0),
             fourier_mix(pm, 0, BATCH, SEQ, dft_ctx)],
            (BATCH, SEQ // TM), "context_mixers")
        (oa_lat,), (of_lat, ob_lat, _), (oc_lat,), (od_lat,) = run_parts(
            [diff(row_blk0=lat_blk0, n_seq=DEC_BATCH, seq_len=DEC_SEQ, cache=(cache_diff_k, cache_diff_v), layer=l),
             hgrn(pg, CTX_TILES, DEC_BATCH, DEC_SEQ, lb_all[l], _state_to_blockdiag(state_hgrn[:, l])),
             na_latent(pm, cache_na_k, cache_na_v, l, _na_bias_tables(na_rpb[l])),
             fourier_mix(pm, lat_blk0, DEC_BATCH, DEC_SEQ, dft_lat)],
            (DEC_BATCH, DEC_SEQ // TM), "latent_mixers")

        x1, h2, bucket, rank, counts = out_and_route(
            x_pair, l > 0, ((oa_ctx, oa_lat), (of_ctx, of_lat), (ob_ctx, ob_lat), (oc_ctx, oc_lat),
                            (od_ctx, od_lat)),
            pg, mods[l], l, hgrn_norm_g[l], w_out_bf16, norm2_g[l], router_pieces, router_b)
        dest, *tile_plan = routing_plan(bucket, rank, counts)
        h_slots = scatter_to_slots(h2, dest, jnp.zeros((N_SLOTS, ROW_W), F32) if l == 0 else h_slots)
        y_slots = moe(h_slots, *tile_plan, l, moe_w_gate, moe_w_up, moe_w_down)
        moe_state = (dest, y_slots, x1)

        states.append(st_ctx)
    y_prompt, y_sample = final_norm(*moe_state, mods[DEPTH - 1], final_norm_g)
    return (y_prompt.reshape(BATCH, SEQ, D_MODEL), y_sample.reshape(DEC_BATCH, DEC_SEQ, D_MODEL),
            *new_kv, jnp.stack(states, axis=1))
```

```python
import functools
import math
from typing import Any, NamedTuple

import numpy as np
import jax
import jax.numpy as jnp
from jax import lax
from jax.experimental import pallas as pl
from jax.experimental.pallas import tpu as pltpu

F32 = jnp.float32
BF16 = jnp.bfloat16
HIGHEST = lax.Precision.HIGHEST

D_MODEL = 1024
BATCH = 16
SEQ = 256
DEPTH = 2
DEC_BATCH = 2
DEC_SEQ = 2048
PAST_LEN = 256
GRID_W = 64
GRID_H = DEC_SEQ // GRID_W
EPS = 1e-6
NEG_BIG = -1e30
HEADS = 4
HEAD_DIM = 64
MIX_BLK = HEADS * HEAD_DIM
A_DIM = 32
ROPE_BASE = 10000.0
B_CHUNK = 32
NA_WIN_H = 8
NA_WIN_W = 16
N_EXPERTS = 16
N_GROUPS = 4
D_EXPERT = 512
PROJ_W = 12 * MIX_BLK
TP = BATCH * SEQ
TL = DEC_BATCH * DEC_SEQ
T = TP + TL
TM = 256
N_TILES = T // TM
CTX_TILES = TP // TM
LAT_TILES_PER_SEQ = DEC_SEQ // TM
(C_AQ, C_AK, C_AV, C_BQ, C_BFF, C_BFB, C_BV, C_BG, C_CQ, C_CK, C_CV, C_DU) = range(12)
PM_BLOCKS = (C_AQ, C_AK, C_AV, C_CQ, C_CK, C_CV, C_DU)
PG_BLOCKS = (C_BQ, C_BFF, C_BFB, C_BV, C_BG)
(M_AQ, M_AK, M_AV, M_CQ, M_CK, M_CV, M_DU) = range(len(PM_BLOCKS))
(G_BQ, G_BFF, G_BFB, G_BV, G_BG) = range(len(PG_BLOCKS))
PM_W = len(PM_BLOCKS) * MIX_BLK
PG_W = len(PG_BLOCKS) * MIX_BLK
NA_SLAB_ROWS = 12
NA_SLAB = NA_SLAB_ROWS * GRID_W
LANES = 128
ROW_W = D_MODEL + LANES
EXPERT_PAIRS = ((0, 1), (0, 2), (0, 3), (1, 3), (2, 3), (2, 1))
N_BUCKETS = N_GROUPS * len(EXPERT_PAIRS)
BUCKET_ROWS = 32
N_DMA_PRIORITIES = 2
TM_OUT = 512
TM_SCATTER = 1024
TM_FINAL = 1024
TM_MOE = 384
MAX_TILES = T // TM_MOE + N_BUCKETS
N_SLOTS = MAX_TILES * TM_MOE
VMEM_LIMIT = 56 * 1024 * 1024


def _cparams(sem):
    return pltpu.CompilerParams(dimension_semantics=sem, vmem_limit_bytes=VMEM_LIMIT)


class Part(NamedTuple):
    kernel: Any
    in_specs: list
    args: list
    out_specs: list
    out_shapes: list
    scratch: list


def _run_parts_kernel(*refs, layout):
    n_in = sum(n for _, n, _, _ in layout)
    n_out = sum(n for _, _, n, _ in layout)
    ins, outs, scratch = refs[:n_in], refs[n_in:n_in + n_out], refs[n_in + n_out:]
    i = o = s = 0
    for kernel, k_in, k_out, k_scratch in layout:
        kernel(*ins[i:i + k_in], *outs[o:o + k_out], *scratch[s:s + k_scratch])
        i, o, s = i + k_in, o + k_out, s + k_scratch


def run_parts(parts, grid, name):
    layout = tuple((p.kernel, len(p.in_specs), len(p.out_specs), len(p.scratch)) for p in parts)
    outs = pl.pallas_call(
        functools.partial(_run_parts_kernel, layout=layout),
        grid=grid,
        in_specs=[s for p in parts for s in p.in_specs],
        out_specs=[s for p in parts for s in p.out_specs],
        out_shape=[s for p in parts for s in p.out_shapes],
        scratch_shapes=[s for p in parts for s in p.scratch],
        compiler_params=_cparams(("arbitrary", "arbitrary")),
        name=name,
    )(*[a for p in parts for a in p.args])
    result, o = [], 0
    for p in parts:
        result.append(outs[o:o + len(p.out_specs)])
        o += len(p.out_specs)
    return result


def _head_lanes(width=MIX_BLK):
    return lax.broadcasted_iota(jnp.int32, (1, width), 1)


def _lane_range(lane, lo, n):
    return (lane >= lo) & (lane < lo + n)


def _same_head_matrix():
    r = lax.broadcasted_iota(jnp.int32, (MIX_BLK, MIX_BLK), 0) // HEAD_DIM
    c = lax.broadcasted_iota(jnp.int32, (MIX_BLK, MIX_BLK), 1) // HEAD_DIM
    return r == c


def _bf16_pieces(x, n):
    pieces = []
    for _ in range(n):
        piece = x.astype(BF16)
        pieces.append(piece)
        x = x - piece.astype(F32)
    return pieces


def _select_sum_left(onehot_bf16, x):
    cols = x.shape[1]
    stacked = jnp.dot(onehot_bf16, jnp.concatenate(_bf16_pieces(x, 3), axis=1), preferred_element_type=F32)
    return stacked[:, :cols] + stacked[:, cols:2 * cols] + stacked[:, 2 * cols:]


def _select_sum_right(x, onehot_bf16):
    rows = x.shape[0]
    stacked = jnp.dot(jnp.concatenate(_bf16_pieces(x, 3), axis=0), onehot_bf16, preferred_element_type=F32)
    return stacked[:rows] + stacked[rows:2 * rows] + stacked[2 * rows:]


def _head_mean_square(o):
    ones = jnp.where(_same_head_matrix(), 1.0, 0.0).astype(BF16)
    return _select_sum_right(o * o, ones) * (1.0 / HEAD_DIM)


def _mod_row(i):
    return jnp.where(i < CTX_TILES, 0, 1 + (i - CTX_TILES) // LAT_TILES_PER_SEQ)


def _mod_kernel(c_ref, w_ref, b_ref, o_ref):
    w = w_ref[0]
    for r in range(3):
        c = c_ref[r]
        s = c * jax.nn.sigmoid(c)
        o_ref[0, r:r + 1, :] = jnp.sum(s * w, axis=0, keepdims=True) + b_ref[0]


def modulation(c_rows, ada_w, ada_b):
    nt = 768
    n_out = 6 * D_MODEL
    return pl.pallas_call(
        _mod_kernel,
        grid=(DEPTH, n_out // nt),
        in_specs=[
            pl.BlockSpec((3, D_MODEL, 1), lambda l, j: (0, 0, 0)),
            pl.BlockSpec((1, D_MODEL, nt), lambda l, j: (l, 0, j)),
            pl.BlockSpec((1, 1, nt), lambda l, j: (l, 0, j)),
        ],
        out_specs=pl.BlockSpec((1, 3, nt), lambda l, j: (l, 0, j)),
        out_shape=jax.ShapeDtypeStruct((DEPTH, 3, n_out), F32),
        compiler_params=_cparams(("arbitrary", "arbitrary")),
        name="modulation",
    )(c_rows[:, :, None], ada_w, ada_b[:, None, :])


def _is_ctx_tile():
    return pl.program_id(0) < CTX_TILES


def _ctx_tile(i, *_):
    return (jnp.minimum(i, CTX_TILES - 1), 0)


def _lat_tile(i, *_):
    return (jnp.maximum(i - CTX_TILES, 0), 0)


def _proj_kernel(xc_ref, xl_ref, mod_ref, g_ref, w_ref, cos_ref, sa_ref, sb_ref, pm_ref, pg_ref, *cache_refs):
    x = jnp.where(_is_ctx_tile(), xc_ref[...], xl_ref[...])
    _proj_body(x, mod_ref, g_ref, w_ref, cos_ref, sa_ref, sb_ref, pm_ref, pg_ref, cache_refs)


def _proj_after_moe_kernel(dest_ref, y_ref, x1_ref, modp_ref, mod_ref, g_ref, w_ref, cos_ref, sa_ref, sb_ref,
                           *rest):
    x2_ref, pm_ref, pg_ref = rest[4:7]
    cache_refs, (buf_ref, sem) = rest[7:11], rest[11:]
    x2 = _moe_residual(dest_ref, y_ref, buf_ref, sem, x1_ref, modp_ref)
    x2_ref[...] = x2
    _proj_body(x2, mod_ref, g_ref, w_ref, cos_ref, sa_ref, sb_ref, pm_ref, pg_ref, cache_refs)


def _proj_body(x, mod_ref, g_ref, w_ref, cos_ref, sa_ref, sb_ref, pm_ref, pg_ref, cache_refs):
    ms = jnp.mean(x * x, axis=-1, keepdims=True)
    mod = mod_ref[0]
    h = x * lax.rsqrt(ms + EPS) * g_ref[...] * (1.0 + mod[1:2]) + mod[0:1]
    p = jnp.dot(h.astype(BF16), w_ref[0], preferred_element_type=F32)
    t = p[:, :2 * MIX_BLK]
    pm_ref[:, :2 * MIX_BLK] = (t * cos_ref[...] + pltpu.roll(t, 1, 1) * sa_ref[...]
                               + pltpu.roll(t, 2 * MIX_BLK - 1, 1) * sb_ref[...]).astype(BF16)
    pm_ref[:, 2 * MIX_BLK:] = p[:, 2 * MIX_BLK:PM_W].astype(BF16)
    pg_ref[...] = p[:, PM_W:]

    @pl.when(_is_ctx_tile())
    def _():
        for ref, col in zip(cache_refs, (M_AK, M_AV, M_CK, M_CV)):
            for hd in range(HEADS):
                lo = col * MIX_BLK + hd * HEAD_DIM
                ref[0, 0, hd] = p[:, lo:lo + HEAD_DIM]
            if ref.shape[1] > 1:
                ref[0, 1:] = jnp.zeros((ref.shape[1] - 1,) + tuple(ref.shape[2:]), F32)


_CACHE_SHAPE = jax.ShapeDtypeStruct((BATCH, DEPTH, HEADS, SEQ, HEAD_DIM), F32)


def _cache_spec(layer):
    n_layers = DEPTH if layer == 0 else 1
    return pl.BlockSpec((1, n_layers, HEADS, SEQ, HEAD_DIM),
                        lambda i, *_: (jnp.minimum(i, CTX_TILES - 1), layer, 0, 0, 0))


def _rope_tables():
    nf = A_DIM // 4
    freqs = ROPE_BASE ** (-np.arange(nf, dtype=np.float64) / nf)
    pos = np.arange(DEC_SEQ)
    row = (pos // GRID_W).astype(np.float64)
    col = (pos % GRID_W).astype(np.float64)
    ang = np.concatenate([row[:, None] * freqs, col[:, None] * freqs], axis=-1)
    cos = np.repeat(np.cos(ang), 2, axis=-1)
    sin = np.repeat(np.sin(ang), 2, axis=-1)
    odd = (np.arange(A_DIM) % 2 == 1)[None, :]
    sin_from_left = np.where(odd, sin, 0.0)
    sin_from_right = np.where(odd, 0.0, -sin)
    reps = 2 * MIX_BLK // A_DIM
    ident = (np.ones((TM, 2 * MIX_BLK)), np.zeros((TM, 2 * MIX_BLK)))
    return tuple(jnp.asarray(np.concatenate([np.tile(t, (1, reps)), tail], axis=0), F32)
                 for t, tail in ((cos, ident[0]), (sin_from_left, ident[1]), (sin_from_right, ident[1])))


def _rope_block(i):
    return (jnp.where(i < CTX_TILES, LAT_TILES_PER_SEQ, (i - CTX_TILES) % LAT_TILES_PER_SEQ), 0)


def projection_after_moe(dest, y_slots, x1, mod_prev, mod, layer, norm_g, w_in_bf16, rope, caches):
    rope_spec = pl.BlockSpec((TM, 2 * MIX_BLK), lambda i, d: _rope_block(i))
    mod_spec = pl.BlockSpec((1, 6, D_MODEL), lambda i, d: (_mod_row(i), 0, 0))
    n_in = 10
    return pl.pallas_call(
        _proj_after_moe_kernel,
        grid_spec=pltpu.PrefetchScalarGridSpec(
            num_scalar_prefetch=1,
            grid=(N_TILES,),
            in_specs=[pl.BlockSpec(memory_space=pl.ANY),
                      pl.BlockSpec((TM, D_MODEL), lambda i, d: (i, 0)),
                      mod_spec, mod_spec,
                      pl.BlockSpec((1, D_MODEL), lambda i, d: (0, 0)),
                      pl.BlockSpec((1, D_MODEL, PROJ_W), lambda i, d: (layer, 0, 0)),
                      rope_spec, rope_spec, rope_spec] + [pl.BlockSpec(memory_space=pl.ANY)] * 4,
            out_specs=[pl.BlockSpec((TM, D_MODEL), lambda i, d: (i, 0)),
                       pl.BlockSpec((TM, PM_W), lambda i, d: (i, 0)),
                       pl.BlockSpec((TM, PG_W), lambda i, d: (i, 0))] + [_cache_spec(layer)] * 4,
            scratch_shapes=_GATHER_SCRATCH,
        ),
        out_shape=[jax.ShapeDtypeStruct((T, D_MODEL), F32), jax.ShapeDtypeStruct((T, PM_W), BF16),
                   jax.ShapeDtypeStruct((T, PG_W), F32)] + [_CACHE_SHAPE] * 4,
        input_output_aliases={n_in + n: 3 + n for n in range(4)},
        compiler_params=_cparams(("arbitrary",)),
        name="projection_after_moe",
    )(dest, y_slots, x1, mod_prev, mod, norm_g[None, :], w_in_bf16, *rope, *caches)


def projection(x_ctx, x_lat, mod, norm_g, w_in_bf16, rope):
    layer = 0
    rope_spec = pl.BlockSpec((TM, 2 * MIX_BLK), _rope_block)
    return pl.pallas_call(
        _proj_kernel,
        grid=(N_TILES,),
        in_specs=[
            pl.BlockSpec((TM, D_MODEL), _ctx_tile),
            pl.BlockSpec((TM, D_MODEL), _lat_tile),
            pl.BlockSpec((1, 6, D_MODEL), lambda i: (_mod_row(i), 0, 0)),
            pl.BlockSpec((1, D_MODEL), lambda i: (0, 0)),
            pl.BlockSpec((1, D_MODEL, PROJ_W), lambda i: (layer, 0, 0)),
            rope_spec, rope_spec, rope_spec,
        ],
        out_specs=[pl.BlockSpec((TM, PM_W), lambda i: (i, 0)), pl.BlockSpec((TM, PG_W), lambda i: (i, 0))]
        + [_cache_spec(layer)] * 4,
        out_shape=[jax.ShapeDtypeStruct((T, PM_W), BF16), jax.ShapeDtypeStruct((T, PG_W), F32)]
        + [_CACHE_SHAPE] * 4,
        compiler_params=_cparams(("arbitrary",)),
        name="projection",
    )(x_ctx, x_lat, mod, norm_g[None, :], w_in_bf16, *rope)


LOG2_E = 1.4426950408889634


def _exp2_rows(s):
    e = jnp.exp2(s - jnp.max(s, axis=-1, keepdims=True))
    return e, 1.0 / jnp.sum(e, axis=-1, keepdims=True)


def _attn_kernel(lam_ref, q_ref, k_ref, v_ref, *rest, n_maps, post_scale, with_cache):
    if with_cache:
        kc_ref, vc_ref, g_ref, o_ref, kt_ref, vb_ref = rest
    else:
        g_ref, o_ref, kt_ref, vb_ref = rest

    @pl.when(pl.program_id(1) == 0)
    def _():
        k = k_ref[...].astype(F32)
        v = v_ref[...]
        if with_cache:
            k = jnp.concatenate([_cache_heads_on_lanes(kc_ref), k], axis=0)
            v = jnp.concatenate([_cache_heads_on_lanes(vc_ref).astype(BF16), v], axis=0)
        kt_ref[...] = k.T.astype(BF16)
        vb_ref[...] = v

    lane = _head_lanes()
    map_dim = HEAD_DIM // n_maps
    q = q_ref[...].astype(F32) * (map_dim ** -0.5 * LOG2_E)
    kt = kt_ref[...]
    vb = vb_ref[...]
    stack_rows = kt_ref.shape[1] <= TM
    weights = []
    for h in range(HEADS):
        masked = [jnp.where(_lane_range(lane, h * HEAD_DIM + j * map_dim, map_dim), q, 0.0).astype(BF16)
                  for j in range(n_maps)]
        if stack_rows:
            s = jnp.dot(jnp.concatenate(masked, axis=0), kt, preferred_element_type=F32)
            scores = [s[j * TM:(j + 1) * TM] for j in range(n_maps)]
        else:
            scores = [jnp.dot(m, kt, preferred_element_type=F32) for m in masked]
        parts = [_exp2_rows(x) for x in scores]
        w = parts[0][0] * parts[0][1]
        if n_maps == 2:
            w = w - parts[1][0] * (lam_ref[0] * parts[1][1])
        weights.append(w.astype(BF16))
    if stack_rows:
        oh = jnp.dot(jnp.concatenate(weights, axis=0), vb, preferred_element_type=F32)
        outs = [oh[h * TM:(h + 1) * TM] for h in range(HEADS)]
    else:
        outs = [jnp.dot(w, vb, preferred_element_type=F32) for w in weights]
    o = jnp.zeros(q.shape, F32)
    for h in range(HEADS):
        o = jnp.where(_lane_range(lane, h * HEAD_DIM, HEAD_DIM), outs[h], o)
    if n_maps == 2:
        o = o * lax.rsqrt(_head_mean_square(o) + EPS) * g_ref[...] * post_scale
    o_ref[...] = o


def _cache_block_spec(layer):
    return pl.BlockSpec((1, 1, HEADS, PAST_LEN, HEAD_DIM), lambda b, i: (b, layer, 0, 0, 0))


def _cache_heads_on_lanes(ref):
    return jnp.concatenate([ref[0, 0, h] for h in range(HEADS)], axis=1)


def attention(p, row_blk0, cols, n_seq, seq_len, lam, norm_g, *, n_maps, post_scale, cache=None, layer=0):
    nb = seq_len // TM
    kv_len = seq_len + (PAST_LEN if cache is not None else 0)
    kern = functools.partial(_attn_kernel, n_maps=n_maps, post_scale=post_scale, with_cache=cache is not None)
    kv_spec = lambda col: pl.BlockSpec((seq_len, MIX_BLK), lambda b, i: (row_blk0 + b, col))
    cache_specs = [_cache_block_spec(layer)] * 2 if cache is not None else []
    return Part(
        kernel=kern,
        in_specs=[
            pl.BlockSpec(memory_space=pltpu.SMEM),
            pl.BlockSpec((TM, MIX_BLK), lambda b, i: ((row_blk0 + b) * nb + i, cols[0])),
            kv_spec(cols[1]), kv_spec(cols[2]), *cache_specs,
            pl.BlockSpec((1, MIX_BLK), lambda b, i: (0, 0)),
        ],
        out_specs=[pl.BlockSpec((TM, MIX_BLK), lambda b, i: (b * nb + i, 0))],
        out_shapes=[jax.ShapeDtypeStruct((n_seq * seq_len, MIX_BLK), F32)],
        scratch=[pltpu.VMEM((MIX_BLK, kv_len), BF16), pltpu.VMEM((kv_len, MIX_BLK), BF16)],
        args=[lam, p, p, p, *(cache or ()), norm_g])


def _na_slab_start(i):
    return jnp.clip(i - 1, 0, GRID_H // 4 - NA_SLAB_ROWS // 4)


def _na_kernel(q_ref, k_ref, v_ref, kc_ref, vc_ref, bias_ref, o_ref):
    i = pl.program_id(1)
    start = pl.multiple_of(_na_slab_start(i) * TM, TM)
    ks_t = k_ref[pl.ds(start, NA_SLAB), :].astype(F32).T.astype(BF16)
    vs = v_ref[pl.ds(start, NA_SLAB), :]
    kc_t = _cache_heads_on_lanes(kc_ref).T.astype(BF16)
    vc = _cache_heads_on_lanes(vc_ref).astype(BF16)
    q = q_ref[...].astype(F32) * (HEAD_DIM ** -0.5)
    lane = _head_lanes()
    heads = [_lane_range(lane, h * HEAD_DIM, HEAD_DIM) for h in range(HEADS)]
    qm = jnp.concatenate([jnp.where(in_head, q, 0.0) for in_head in heads], axis=0).astype(BF16)
    s_loc = jnp.dot(qm, ks_t, preferred_element_type=F32)
    s_ctx = jnp.dot(qm, kc_t, preferred_element_type=F32)
    e_locs, e_ctxs, dens = [], [], []
    for h in range(HEADS):
        rows = slice(h * TM, (h + 1) * TM)
        sl = s_loc[rows] + bias_ref[0, h]
        sc = s_ctx[rows]
        m = jnp.maximum(jnp.max(sl, axis=-1, keepdims=True), jnp.max(sc, axis=-1, keepdims=True))
        e_loc = jnp.exp(sl - m)
        e_ctx = jnp.exp(sc - m)
        dens.append(jnp.sum(e_loc, axis=-1, keepdims=True) + jnp.sum(e_ctx, axis=-1, keepdims=True))
        e_locs.append(e_loc.astype(BF16))
        e_ctxs.append(e_ctx.astype(BF16))
    oh = (jnp.dot(jnp.concatenate(e_locs, axis=0), vs, preferred_element_type=F32)
          + jnp.dot(jnp.concatenate(e_ctxs, axis=0), vc, preferred_element_type=F32))
    o = jnp.zeros(q.shape, F32)
    for h in range(HEADS):
        o = jnp.where(heads[h], oh[h * TM:(h + 1) * TM] / dens[h], o)
    o_ref[...] = o


def _na_bias_tables(rpb):
    n_dr, n_dc = 2 * NA_WIN_H - 1, 2 * NA_WIN_W - 1
    cq = np.arange(GRID_W)[:, None]
    ck = np.arange(GRID_W)[None, :]
    wc0 = np.clip(cq - NA_WIN_W // 2, 0, GRID_W - NA_WIN_W)
    col_ok = (ck >= wc0) & (ck < wc0 + NA_WIN_W)
    col_pick = np.clip(ck - cq + NA_WIN_W - 1, 0, n_dc - 1)[..., None] == np.arange(n_dc)
    by_col = jnp.einsum("hab,qcb->haqc", rpb.astype(F32), jnp.asarray(col_pick, F32), precision=HIGHEST)
    margin = 4
    by_col = jnp.pad(by_col.transpose(0, 2, 1, 3), ((0, 0), (0, 0), (margin, margin), (0, 0)))
    by_col = by_col.reshape(HEADS, GRID_W, (n_dr + 2 * margin) * GRID_W)
    pieces, row_ok = [], []
    for tile in (0, 1, GRID_H // 4 - 1):
        slab0 = int(np.clip(tile - 1, 0, GRID_H // 4 - NA_SLAB_ROWS // 4)) * 4
        rq = tile * 4 + np.arange(4)
        rk = (slab0 + np.arange(NA_SLAB) // GRID_W)[None, :]
        wr0 = np.clip(rq - NA_WIN_H // 2, 0, GRID_H - NA_WIN_H)[:, None]
        row_ok.append((rk >= wr0) & (rk < wr0 + NA_WIN_H))
        for r in rq:
            first = slab0 - int(r) + NA_WIN_H - 1 + margin
            assert 0 <= first and first + NA_SLAB_ROWS <= n_dr + 2 * margin
            pieces.append(by_col[:, :, first * GRID_W:first * GRID_W + NA_SLAB])
    table = jnp.stack(pieces).reshape(3, 4, HEADS, GRID_W, NA_SLAB).transpose(0, 2, 1, 3, 4)
    valid = np.stack(row_ok)[:, None, :, None, :] & np.tile(col_ok, (1, NA_SLAB_ROWS))[None, None, None]
    table = jnp.where(jnp.asarray(valid), table, NEG_BIG)
    return table.reshape(3, HEADS, TM, NA_SLAB)


def na_latent(p, kc, vc, layer, bias):
    n_t = LAT_TILES_PER_SEQ
    seq_blk0 = TP // DEC_SEQ

    def bias_idx(b, i):
        return (jnp.minimum(i, 1) + i // (n_t - 1), 0, 0, 0)

    return Part(
        kernel=_na_kernel,
        in_specs=[
            pl.BlockSpec((TM, MIX_BLK), lambda b, i: (CTX_TILES + b * n_t + i, M_CQ)),
            pl.BlockSpec((DEC_SEQ, MIX_BLK), lambda b, i: (seq_blk0 + b, M_CK)),
            pl.BlockSpec((DEC_SEQ, MIX_BLK), lambda b, i: (seq_blk0 + b, M_CV)),
            _cache_block_spec(layer), _cache_block_spec(layer),
            pl.BlockSpec((1, HEADS, TM, NA_SLAB), bias_idx),
        ],
        out_specs=[pl.BlockSpec((TM, MIX_BLK), lambda b, i: (b * n_t + i, 0))],
        out_shapes=[jax.ShapeDtypeStruct((TL, MIX_BLK), F32)],
        scratch=[],
        args=[p, p, p, kc, vc, bias])


MAX_EXPONENT = 80.0


def _hgrn_direction(q_ref, f_ref, v_ref, lb, st_ref, o_ref, reverse):
    n_ch = TM // B_CHUNK
    r_idx = lax.broadcasted_iota(jnp.int32, (TM, TM), 0)
    c_idx = lax.broadcasted_iota(jnp.int32, (TM, TM), 1)
    tri = (c_idx >= r_idx) if reverse else (c_idx <= r_idx)
    zq = q_ref[...]
    q = zq * jax.nn.sigmoid(zq)
    z = f_ref[...]
    gate = (1.0 - lb) * jax.nn.sigmoid(z)
    logf = jnp.log(lb + gate)
    kk = (1.0 - lb) - gate
    b = _select_sum_left(jnp.where(tri, 1.0, 0.0).astype(BF16), logf)
    b3 = b.reshape(n_ch, B_CHUNK, MIX_BLK)
    mid = B_CHUNK // 2 if reverse else B_CHUNK // 2 - 1
    q_in = (q.reshape(b3.shape) * jnp.exp(b3 - b3[:, mid:mid + 1, :])).reshape(TM, MIX_BLK)
    q_dec = (q * jnp.exp(b)).astype(BF16)
    b_t = b.T
    kk_t = kk.T
    far = 0 if reverse else TM - 1
    b_far = b_t[:, far:far + 1]
    k_dec_t = (kk_t * jnp.exp(b_far - b_t)).astype(BF16)
    vb = v_ref[...].astype(BF16)
    st = st_ref[...]
    o_state = jnp.dot(q_dec, st.astype(BF16), preferred_element_type=F32)
    kv = jnp.dot(k_dec_t, vb, preferred_element_type=F32)
    st_ref[...] = st * jnp.exp(b_far) + jnp.where(_same_head_matrix(), kv, 0.0)
    lane = _head_lanes()
    token = lax.broadcasted_iota(jnp.int32, (1, TM), 1)
    local = lax.broadcasted_iota(jnp.int32, (HEADS * B_CHUNK, 1), 0) % B_CHUNK
    heads = [_lane_range(lane, h * HEAD_DIM, HEAD_DIM) for h in range(HEADS)]
    weights = []
    for c in range(n_ch):
        ref = b_t[:, c * B_CHUNK + mid:c * B_CHUNK + mid + 1]
        k_c_t = (kk_t * jnp.exp(jnp.minimum(ref - b_t, MAX_EXPONENT))).astype(BF16)
        q_c = q_in[c * B_CHUNK:(c + 1) * B_CHUNK, :]
        lhs = jnp.concatenate([jnp.where(in_head, q_c, 0.0) for in_head in heads], axis=0)
        a = jnp.dot(lhs.astype(BF16), k_c_t, preferred_element_type=F32)
        t_abs = c * B_CHUNK + local
        weights.append(jnp.where((token >= t_abs) if reverse else (token <= t_abs), a, 0.0).astype(BF16))
    res = jnp.dot(jnp.concatenate(weights, axis=0), vb, preferred_element_type=F32)
    for c in range(n_ch):
        rows = slice(c * B_CHUNK, (c + 1) * B_CHUNK)
        o_c = o_state[rows, :]
        for h, in_head in enumerate(heads):
            lo = (c * HEADS + h) * B_CHUNK
            o_c = o_c + jnp.where(in_head, res[lo:lo + B_CHUNK, :], 0.0)
        o_ref[rows, :] = o_c


def _hgrn_kernel(qf_ref, ff_ref, vf_ref, qb_ref, fb_ref, vb_ref, lb_ref, s0_ref,
                 of_ref, ob_ref, s_ref, stf_ref, stb_ref, *, has_s0):
    j = pl.program_id(1)

    @pl.when(j == 0)
    def _():
        if has_s0:
            stf_ref[...] = s0_ref[0, 0]
            stb_ref[...] = s0_ref[0, 1]
        else:
            stf_ref[...] = jnp.zeros((MIX_BLK, MIX_BLK), F32)
            stb_ref[...] = jnp.zeros((MIX_BLK, MIX_BLK), F32)

    lb = lb_ref[...]
    _hgrn_direction(qf_ref, ff_ref, vf_ref, lb[0:1], stf_ref, of_ref, False)
    _hgrn_direction(qb_ref, fb_ref, vb_ref, lb[1:2], stb_ref, ob_ref, True)

    @pl.when(j == pl.num_programs(1) - 1)
    def _():
        for d, st_ref in enumerate((stf_ref, stb_ref)):
            s = st_ref[...]
            for hd in range(HEADS):
                lo = hd * HEAD_DIM
                s_ref[0, d, hd] = s[lo:lo + HEAD_DIM, lo:lo + HEAD_DIM]


def hgrn(p, row_tile0, n_seq, seq_len, lb, s0):
    nb = seq_len // TM
    has_s0 = s0 is not None
    if s0 is None:
        s0 = jnp.zeros((1, 2, MIX_BLK, MIX_BLK), F32)

    def fwd(col):
        return pl.BlockSpec((TM, MIX_BLK), lambda s, j: (row_tile0 + s * nb + j, col))

    def bwd(col):
        return pl.BlockSpec((TM, MIX_BLK), lambda s, j: (row_tile0 + s * nb + nb - 1 - j, col))

    state_spec = pl.BlockSpec((1, 2, MIX_BLK, MIX_BLK), lambda s, j: (s if has_s0 else 0, 0, 0, 0))
    out_rows = n_seq * seq_len
    return Part(
        kernel=functools.partial(_hgrn_kernel, has_s0=has_s0),
        in_specs=[fwd(G_BQ), fwd(G_BFF), fwd(G_BV), bwd(G_BQ), bwd(G_BFB), bwd(G_BV),
                  pl.BlockSpec((2, MIX_BLK), lambda s, j: (0, 0)), state_spec],
        out_specs=[
            pl.BlockSpec((TM, MIX_BLK), lambda s, j: (s * nb + j, 0)),
            pl.BlockSpec((TM, MIX_BLK), lambda s, j: (s * nb + nb - 1 - j, 0)),
            pl.BlockSpec((1, 2, HEADS, HEAD_DIM, HEAD_DIM), lambda s, j: (s, 0, 0, 0, 0)),
        ],
        out_shapes=[jax.ShapeDtypeStruct((out_rows, MIX_BLK), F32),
                    jax.ShapeDtypeStruct((out_rows, MIX_BLK), F32),
                    jax.ShapeDtypeStruct((n_seq, 2, HEADS, HEAD_DIM, HEAD_DIM), F32)],
        scratch=[pltpu.VMEM((MIX_BLK, MIX_BLK), F32), pltpu.VMEM((MIX_BLK, MIX_BLK), F32)],
        args=[p, p, p, p, p, p, lb, s0])


def _state_to_blockdiag(s):
    eye = jnp.eye(HEADS, dtype=F32)
    full = s.astype(F32)[:, :, :, :, None, :] * eye[None, None, :, None, :, None]
    return full.reshape(s.shape[0], 2, MIX_BLK, MIX_BLK)


def _fft_kernel(u_ref, cs64_ref, csl_ref, o_ref, ab_ref, *, norm):
    seq_len = u_ref.shape[0]

    @pl.when(pl.program_id(1) == 0)
    def _():
        ab = jnp.dot(u_ref[...], cs64_ref[...], preferred_element_type=F32).astype(BF16)
        ab_ref[:seq_len, :] = ab[:, :MIX_BLK]
        ab_ref[seq_len:, :] = ab[:, MIX_BLK:]

    o_ref[...] = jnp.dot(csl_ref[...], ab_ref[...], preferred_element_type=F32) * norm


def _dft_tables(n):
    k = np.arange(n)
    ang = 2.0 * np.pi * ((k[:, None] * k[None, :]) % n) / n
    return np.cos(ang), np.sin(ang)


def _dft_constants(seq_len):
    c64, s64 = _dft_tables(HEAD_DIM)
    eye = np.eye(HEADS)
    cl, sl = _dft_tables(seq_len)
    as_bf16 = lambda a: jnp.asarray(a, F32).astype(BF16)
    return (as_bf16(np.concatenate([np.kron(eye, c64), -np.kron(eye, s64)], axis=1)),
            as_bf16(np.concatenate([cl, sl], axis=1)))


def fourier_mix(p, row_blk0, n_seq, seq_len, consts):
    cs64, csl = consts
    nb = seq_len // TM
    norm = 1.0 / math.sqrt(seq_len * HEAD_DIM)
    return Part(
        kernel=functools.partial(_fft_kernel, norm=norm),
        in_specs=[
            pl.BlockSpec((seq_len, MIX_BLK), lambda s, i: (row_blk0 + s, M_DU)),
            pl.BlockSpec((MIX_BLK, 2 * MIX_BLK), lambda s, i: (0, 0)),
            pl.BlockSpec((TM, 2 * seq_len), lambda s, i: (i, 0)),
        ],
        out_specs=[pl.BlockSpec((TM, MIX_BLK), lambda s, i: (s * nb + i, 0))],
        out_shapes=[jax.ShapeDtypeStruct((n_seq * seq_len, MIX_BLK), F32)],
        scratch=[pltpu.VMEM((2 * seq_len, MIX_BLK), BF16)],
        args=[p, cs64, csl])


def _route(logits_t, rb):
    per = N_EXPERTS // N_GROUPS
    score = [jax.nn.sigmoid(logits_t[e:e + 1, :]) for e in range(N_EXPERTS)]
    sel = [score[e] + rb[e:e + 1, :] for e in range(N_EXPERTS)]
    gscore = []
    for g in range(N_GROUPS):
        vals = sel[g * per:(g + 1) * per]
        best = None
        for a in range(per):
            for b in range(a + 1, per):
                pair = vals[a] + vals[b]
                best = pair if best is None else jnp.maximum(best, pair)
        gscore.append(best)
    chosen = []
    for g in range(N_GROUPS):
        ok = None
        for j in range(N_GROUPS):
            if j == g:
                continue
            cond = gscore[g] > gscore[j] if j < g else gscore[g] >= gscore[j]
            ok = cond if ok is None else ok & cond
        chosen.append(ok)
    picked = []
    for e in range(N_EXPERTS):
        g = e // per
        rank = jnp.zeros_like(sel[e])
        for j in range(g * per, (g + 1) * per):
            if j == e:
                continue
            ahead = sel[j] >= sel[e] if j < e else sel[j] > sel[e]
            rank = rank + jnp.where(ahead, 1.0, 0.0)
        picked.append(chosen[g] & (rank < 2.0))
    wsum = jnp.zeros_like(score[0])
    for e in range(N_EXPERTS):
        wsum = wsum + jnp.where(picked[e], score[e], 0.0)
    bucket = jnp.zeros_like(wsum)
    w_a = jnp.zeros_like(wsum)
    w_b = jnp.zeros_like(wsum)
    for g in range(N_GROUPS):
        for n, (a, b) in enumerate(EXPERT_PAIRS):
            hit = picked[g * per + a] & picked[g * per + b]
            bucket = jnp.where(hit, float(g * len(EXPERT_PAIRS) + n), bucket)
            w_a = jnp.where(hit, score[g * per + a] / wsum, w_a)
            w_b = jnp.where(hit, score[g * per + b] / wsum, w_b)
    return bucket, w_a, w_b


def _out_kernel(*refs):
    streams, rest = refs[:12], refs[12:]
    (bg_ref, mod_ref, hg_ref, w_ref, g2_ref, rw_ref, rb_ref,
     x1_ref, h2_ref, bucket_ref, rank_ref, counts_ref, run_ref) = rest
    is_ctx = pl.program_id(0) < TP // TM_OUT
    x, o_a, o_f, o_b, o_c, o_d = (jnp.where(is_ctx, streams[2 * n][...], streams[2 * n + 1][...])
                                  for n in range(6))

    @pl.when(pl.program_id(0) == 0)
    def _():
        run_ref[...] = jnp.zeros(run_ref.shape, F32)

    mod = mod_ref[0]
    hb = o_f + o_b
    zg = bg_ref[...]
    hb = hb * lax.rsqrt(_head_mean_square(hb) + EPS) * hg_ref[...] * (zg * jax.nn.sigmoid(zg))
    mixers = jnp.concatenate([part.astype(BF16) for part in (o_a, hb, o_c, o_d)], axis=1)
    mixed = jnp.dot(mixers, w_ref[0], preferred_element_type=F32)
    x1 = x + mod[2:3] * mixed
    x1_ref[...] = x1
    ms = jnp.mean(x1 * x1, axis=-1, keepdims=True)
    h2 = x1 * lax.rsqrt(ms + EPS) * g2_ref[...] * (1.0 + mod[4:5]) + mod[3:4]
    rw = rw_ref[...]
    r = jnp.dot(jnp.concatenate(_bf16_pieces(h2, 2), axis=0), rw, preferred_element_type=F32)
    r = r[:TM_OUT] + r[TM_OUT:]
    bucket, w_a, w_b = _route((r[:, :LANES] + r[:, LANES:]).T, rb_ref[...])
    h2_ref[:, :D_MODEL] = h2
    h2_ref[:, D_MODEL:] = jnp.concatenate([w_a, w_b, jnp.zeros((LANES - 2, TM_OUT), F32)], axis=0).T
    onehot = jnp.where(lax.broadcasted_iota(jnp.int32, (BUCKET_ROWS, 1), 0).astype(F32) == bucket, 1.0, 0.0)
    s_idx = lax.broadcasted_iota(jnp.int32, (TM_OUT, TM_OUT), 0)
    t_idx = lax.broadcasted_iota(jnp.int32, (TM_OUT, TM_OUT), 1)
    prefix = jnp.dot(onehot.astype(BF16), jnp.where(s_idx <= t_idx, 1.0, 0.0).astype(BF16),
                     preferred_element_type=F32)
    run = run_ref[...]
    rank = jnp.sum(onehot * (prefix - 1.0 + run[:, 0:1]), axis=0, keepdims=True)
    run = run + jnp.sum(onehot, axis=1, keepdims=True)
    run_ref[...] = run
    bucket_ref[...] = bucket.astype(jnp.int32)
    rank_ref[...] = rank.astype(jnp.int32)
    counts_ref[...] = run


def out_and_route(x_pair, x_is_combined, mixer_pairs, p, mod, layer, hgrn_g, w_out_bf16, norm2_g, router_pieces,
                  router_b):
    n_ctx = TP // TM_OUT
    ctx_tile = lambda i: (jnp.minimum(i, n_ctx - 1), 0)
    lat_tile = lambda i: (jnp.maximum(i - n_ctx, 0), 0)
    mod_row = lambda i: jnp.where(i < n_ctx, 0, 1 + (i - n_ctx) // (DEC_SEQ // TM_OUT))
    tile = lambda w: pl.BlockSpec((TM_OUT, w), lambda i: (i, 0))
    full = lambda r, c: pl.BlockSpec((r, c), lambda i: (0, 0))
    stream_specs = [pl.BlockSpec((TM_OUT, D_MODEL), ctx_tile),
                    pl.BlockSpec((TM_OUT, D_MODEL),
                                 (lambda i: (jnp.maximum(i, n_ctx), 0)) if x_is_combined else lat_tile)]
    stream_args = list(x_pair)
    for o_ctx, o_lat in mixer_pairs:
        stream_specs += [pl.BlockSpec((TM_OUT, MIX_BLK), ctx_tile), pl.BlockSpec((TM_OUT, MIX_BLK), lat_tile)]
        stream_args += [o_ctx, o_lat]
    return pl.pallas_call(
        _out_kernel,
        grid=(T // TM_OUT,),
        in_specs=stream_specs + [
            pl.BlockSpec((TM_OUT, MIX_BLK), lambda i: (i, G_BG)),
            pl.BlockSpec((1, 6, D_MODEL), lambda i: (mod_row(i), 0, 0)),
            full(1, MIX_BLK), pl.BlockSpec((1, D_MODEL, D_MODEL), lambda i: (layer, 0, 0)), full(1, D_MODEL),
            full(D_MODEL, 2 * LANES), full(N_EXPERTS, 1),
        ],
        out_specs=[tile(D_MODEL), tile(ROW_W), pl.BlockSpec((1, TM_OUT), lambda i: (0, i)),
                   pl.BlockSpec((1, TM_OUT), lambda i: (0, i)), full(BUCKET_ROWS, LANES)],
        out_shape=[jax.ShapeDtypeStruct((T, D_MODEL), F32),
                   jax.ShapeDtypeStruct((T, ROW_W), F32),
                   jax.ShapeDtypeStruct((1, T), jnp.int32),
                   jax.ShapeDtypeStruct((1, T), jnp.int32),
                   jax.ShapeDtypeStruct((BUCKET_ROWS, LANES), F32)],
        scratch_shapes=[pltpu.VMEM((BUCKET_ROWS, LANES), F32)],
        compiler_params=_cparams(("arbitrary",)),
        name="out_and_route",
    )(*stream_args, p, mod, jnp.tile(hgrn_g, HEADS)[None, :], w_out_bf16,
      norm2_g[None, :], router_pieces, router_b[:, None])


def _router_pieces(router_w):
    hi, lo = _bf16_pieces(router_w.astype(F32), 2)
    pad = lambda a: jnp.pad(a, ((0, 0), (0, LANES - N_EXPERTS)))
    return jnp.concatenate([pad(hi), pad(lo)], axis=1)


def routing_plan(bucket, rank, counts):
    counts = counts[:N_BUCKETS, 0].astype(jnp.int32)
    n_tiles = (counts + TM_MOE - 1) // TM_MOE
    tile_end = jnp.cumsum(n_tiles)
    tile_start = tile_end - n_tiles
    buckets = jnp.arange(N_BUCKETS, dtype=jnp.int32)
    start_of_token = jnp.sum(jnp.where(bucket[0][:, None] == buckets[None, :], tile_start[None, :], 0), axis=1)
    dest = start_of_token * TM_MOE + rank[0]
    tiles = jnp.arange(MAX_TILES, dtype=jnp.int32)
    valid = tiles < tile_end[-1]
    tile_bucket = jnp.sum((jnp.minimum(tiles, tile_end[-1] - 1)[:, None] >= tile_end[None, :]).astype(jnp.int32), axis=1)
    pair_a = np.array([a for a, _ in EXPERT_PAIRS], np.int32)
    pair_b = np.array([b for _, b in EXPERT_PAIRS], np.int32)
    per = N_EXPERTS // N_GROUPS
    exp_a = jnp.asarray((np.arange(N_BUCKETS) // len(EXPERT_PAIRS)) * per + np.tile(pair_a, N_GROUPS), jnp.int32)
    exp_b = jnp.asarray((np.arange(N_BUCKETS) // len(EXPERT_PAIRS)) * per + np.tile(pair_b, N_GROUPS), jnp.int32)
    pick = tile_bucket[:, None] == buckets[None, :]
    tile_a = jnp.sum(jnp.where(pick, exp_a[None, :], 0), axis=1)
    tile_b = jnp.sum(jnp.where(pick, exp_b[None, :], 0), axis=1)
    return dest.astype(jnp.int32), tile_a, tile_b, valid.astype(jnp.int32), (tile_end[-1:] - 1).astype(jnp.int32)


def _row_copy(src, src_row, dst, dst_row, sem):
    return pltpu.make_async_copy(src.at[pl.ds(src_row, 1), :], dst.at[pl.ds(dst_row, 1), :], sem)


def _scatter_kernel(dest_ref, h_ref, init_ref, o_ref, sem):
    del init_ref
    base = pl.program_id(0) * TM_SCATTER

    for r in range(TM_SCATTER):
        _row_copy(h_ref, r, o_ref, dest_ref[base + r], sem).start(priority=r % N_DMA_PRIORITIES)
    pltpu.make_async_copy(h_ref, o_ref.at[pl.ds(0, TM_SCATTER), :], sem).wait()


def scatter_to_slots(h2, dest, slots):
    return pl.pallas_call(
        _scatter_kernel,
        grid_spec=pltpu.PrefetchScalarGridSpec(
            num_scalar_prefetch=1,
            grid=(T // TM_SCATTER,),
            in_specs=[pl.BlockSpec((TM_SCATTER, ROW_W), lambda i, d: (i, 0)),
                      pl.BlockSpec(memory_space=pl.ANY)],
            out_specs=pl.BlockSpec(memory_space=pl.ANY),
            scratch_shapes=[pltpu.SemaphoreType.DMA(())],
        ),
        out_shape=jax.ShapeDtypeStruct((N_SLOTS, ROW_W), F32),
        input_output_aliases={2: 0},
        compiler_params=_cparams(("arbitrary",)),
        name="scatter_to_slots",
    )(dest, h2, slots)


def _moe_kernel(ta_ref, tb_ref, valid_ref, last_ref, h_ref, wga_ref, wua_ref, wda_ref, wgb_ref, wub_ref, wdb_ref,
                o_ref):
    del ta_ref, tb_ref, last_ref
    i = pl.program_id(0)

    @pl.when(valid_ref[i] == 1)
    def _():
        x = h_ref[:, :D_MODEL].astype(BF16)
        gates = h_ref[:, D_MODEL:]
        y = jnp.zeros((TM_MOE, D_MODEL), F32)
        for n, (wg, wu, wd) in enumerate(((wga_ref, wua_ref, wda_ref), (wgb_ref, wub_ref, wdb_ref))):
            a = jnp.dot(x, wg[0, 0].astype(BF16), preferred_element_type=F32)
            u = jnp.dot(x, wu[0, 0].astype(BF16), preferred_element_type=F32)
            z = a * jax.nn.sigmoid(a) * u * gates[:, n:n + 1]
            y = y + jnp.dot(z.astype(BF16), wd[0, 0].astype(BF16), preferred_element_type=F32)
        o_ref[:, 0, :] = y

    @pl.when(valid_ref[i] == 0)
    def _():
        o_ref[...] = jnp.zeros((TM_MOE, 1, D_MODEL), F32)


def moe(h_slots, tile_a, tile_b, valid, last, layer, wg, wu, wd):
    up_a = pl.BlockSpec((1, 1, D_MODEL, D_EXPERT), lambda i, ta, tb, v, last: (layer, ta[i], 0, 0))
    up_b = pl.BlockSpec((1, 1, D_MODEL, D_EXPERT), lambda i, ta, tb, v, last: (layer, tb[i], 0, 0))
    down_a = pl.BlockSpec((1, 1, D_EXPERT, D_MODEL), lambda i, ta, tb, v, last: (layer, ta[i], 0, 0))
    down_b = pl.BlockSpec((1, 1, D_EXPERT, D_MODEL), lambda i, ta, tb, v, last: (layer, tb[i], 0, 0))
    return pl.pallas_call(
        _moe_kernel,
        grid_spec=pltpu.PrefetchScalarGridSpec(
            num_scalar_prefetch=4,
            grid=(MAX_TILES,),
            in_specs=[pl.BlockSpec((TM_MOE, ROW_W), lambda i, ta, tb, v, last: (jnp.minimum(i, last[0]), 0)),
                      up_a, up_a, down_a, up_b, up_b, down_b],
            out_specs=pl.BlockSpec((TM_MOE, 1, D_MODEL), lambda i, ta, tb, v, last: (i, 0, 0)),
        ),
        out_shape=jax.ShapeDtypeStruct((N_SLOTS, 1, D_MODEL), F32),
        compiler_params=_cparams(("arbitrary",)),
        name="moe",
    )(tile_a, tile_b, valid, last, h_slots, wg, wu, wd, wg, wu, wd)


def _gather_tile(dest_ref, y_ref, buf_ref, sem, tile, slot):
    rows = buf_ref.shape[1]
    for r in range(rows):
        pltpu.make_async_copy(y_ref.at[pl.ds(dest_ref[tile * rows + r], 1)],
                              buf_ref.at[slot, pl.ds(r, 1)], sem.at[slot]).start(priority=r % N_DMA_PRIORITIES)


def _moe_residual(dest_ref, y_ref, buf_ref, sem, x1_ref, mod_ref):
    i = pl.program_id(0)
    slot = i % 2

    @pl.when(i == 0)
    def _():
        _gather_tile(dest_ref, y_ref, buf_ref, sem, 0, 0)

    @pl.when(i + 1 < pl.num_programs(0))
    def _():
        _gather_tile(dest_ref, y_ref, buf_ref, sem, i + 1, 1 - slot)

    pltpu.make_async_copy(y_ref.at[pl.ds(0, buf_ref.shape[1])], buf_ref.at[slot], sem.at[slot]).wait()
    return x1_ref[...] + mod_ref[0][5:6] * buf_ref[slot, :, 0, :]


def _final_kernel(dest_ref, y_ref, x1_ref, mod_ref, g_ref, oc_ref, ol_ref, buf_ref, sem):
    x2 = _moe_residual(dest_ref, y_ref, buf_ref, sem, x1_ref, mod_ref)
    ms = jnp.mean(x2 * x2, axis=-1, keepdims=True)
    y = x2 * lax.rsqrt(ms + EPS) * g_ref[...]

    is_ctx = pl.program_id(0) < TP // TM_FINAL

    @pl.when(is_ctx)
    def _():
        oc_ref[...] = y

    @pl.when(jnp.logical_not(is_ctx))
    def _():
        ol_ref[...] = y


_GATHER_SCRATCH = [pltpu.VMEM((2, TM, 1, D_MODEL), F32), pltpu.SemaphoreType.DMA((2,))]


def final_norm(dest, y_slots, x1, mod, final_g):
    n_ctx = TP // TM_FINAL
    mod_row = lambda i: jnp.where(i < n_ctx, 0, 1 + (i - n_ctx) // (DEC_SEQ // TM_FINAL))
    return pl.pallas_call(
        _final_kernel,
        grid_spec=pltpu.PrefetchScalarGridSpec(
            num_scalar_prefetch=1,
            grid=(T // TM_FINAL,),
            in_specs=[pl.BlockSpec(memory_space=pl.ANY),
                      pl.BlockSpec((TM_FINAL, D_MODEL), lambda i, d: (i, 0)),
                      pl.BlockSpec((1, 6, D_MODEL), lambda i, d: (mod_row(i), 0, 0)),
                      pl.BlockSpec((1, D_MODEL), lambda i, d: (0, 0))],
            out_specs=[pl.BlockSpec((TM_FINAL, D_MODEL), lambda i, d: (jnp.minimum(i, n_ctx - 1), 0)),
                       pl.BlockSpec((TM_FINAL, D_MODEL), lambda i, d: (jnp.maximum(i - n_ctx, 0), 0))],
            scratch_shapes=[pltpu.VMEM((2, TM_FINAL, 1, D_MODEL), F32), pltpu.SemaphoreType.DMA((2,))],
        ),
        out_shape=[jax.ShapeDtypeStruct((TP, D_MODEL), F32), jax.ShapeDtypeStruct((TL, D_MODEL), F32)],
        compiler_params=_cparams(("arbitrary",)),
        name="final_norm",
    )(dest, y_slots, x1, mod, final_g[None, :])


def kernel(x_prompt, x_sample, cache_diff_k, cache_diff_v, cache_na_k, cache_na_v, state_hgrn, c, c_ctx,
           norm1_g, norm2_g, ada_w, ada_b, w_in, w_out, diff_lambda, diff_subln_g, hgrn_lb_logits,
           hgrn_norm_g, na_rpb, router_w, router_b, moe_w_gate, moe_w_up, moe_w_down, final_norm_g):
    assert SEQ == TM and PAST_LEN == TM and DEC_SEQ % TM_OUT == 0 and TP % DEC_SEQ == 0
    x_pair = (x_prompt.reshape(TP, D_MODEL), x_sample.reshape(TL, D_MODEL))
    w_in_bf16 = jnp.concatenate([w_in[:, :, c * MIX_BLK:(c + 1) * MIX_BLK] for c in PM_BLOCKS + PG_BLOCKS],
                                axis=-1).astype(BF16)
    w_out_bf16 = w_out.astype(BF16)
    router_pieces = _router_pieces(router_w)
    mods = modulation(jnp.concatenate([c_ctx[None, :], c], axis=0), ada_w, ada_b)
    mods = mods.reshape(DEPTH, 3, 6, D_MODEL)
    lb_sm = jax.nn.softmax(hgrn_lb_logits.astype(F32), axis=0)
    lb_all = jnp.cumsum(lb_sm, axis=0) - lb_sm[0:1]
    rope = _rope_tables()
    dft_ctx = _dft_constants(SEQ)
    dft_lat = _dft_constants(DEC_SEQ)
    lat_blk0 = TP // DEC_SEQ
    states = []
    moe_state = None
    for l in range(DEPTH):
        if moe_state is None:
            pm, pg, *new_kv = projection(*x_pair, mods[l], norm1_g[l], w_in_bf16, rope)
        else:
            x, pm, pg, *new_kv = projection_after_moe(*moe_state, mods[l - 1], mods[l], l, norm1_g[l], w_in_bf16, rope,
                                                 new_kv)
            x_pair = (x, x)

        lq = diff_lambda[l].astype(F32)
        lam_init = 0.8 - 0.6 * math.exp(-0.3 * l)
        lam = (jnp.exp(jnp.sum(lq[0] * lq[1])) - jnp.exp(jnp.sum(lq[2] * lq[3])) + lam_init).reshape(1)
        subln = jnp.tile(diff_subln_g[l], HEADS)[None, :]
        diff = functools.partial(attention, pm, cols=(M_AQ, M_AK, M_AV), lam=lam, norm_g=subln, n_maps=2,
                                 post_scale=1.0 - lam_init)
        (oa_ctx,), (of_ctx, ob_ctx, st_ctx), (oc_ctx,), (od_ctx,) = run_parts(
            [diff(row_blk0=0, n_seq=BATCH, seq_len=SEQ),
             hgrn(pg, 0, BATCH, SEQ, lb_all[l], None),
             attention(pm, 0, (M_CQ, M_CK, M_CV), BATCH, SEQ, lam, subln, n_maps=1, post_scale=1.0),
             fourier_mix(pm, 0, BATCH, SEQ, dft_ctx)],
            (BATCH, SEQ // TM), "context_mixers")
        (oa_lat,), (of_lat, ob_lat, _), (oc_lat,), (od_lat,) = run_parts(
            [diff(row_blk0=lat_blk0, n_seq=DEC_BATCH, seq_len=DEC_SEQ, cache=(cache_diff_k, cache_diff_v), layer=l),
             hgrn(pg, CTX_TILES, DEC_BATCH, DEC_SEQ, lb_all[l], _state_to_blockdiag(state_hgrn[:, l])),
             na_latent(pm, cache_na_k, cache_na_v, l, _na_bias_tables(na_rpb[l])),
             fourier_mix(pm, lat_blk0, DEC_BATCH, DEC_SEQ, dft_lat)],
            (DEC_BATCH, DEC_SEQ // TM), "latent_mixers")

        x1, h2, bucket, rank, counts = out_and_route(
            x_pair, l > 0, ((oa_ctx, oa_lat), (of_ctx, of_lat), (ob_ctx, ob_lat), (oc_ctx, oc_lat),
                            (od_ctx, od_lat)),
            pg, mods[l], l, hgrn_norm_g[l], w_out_bf16, norm2_g[l], router_pieces, router_b)
        dest, *tile_plan = routing_plan(bucket, rank, counts)
        h_slots = scatter_to_slots(h2, dest, jnp.zeros((N_SLOTS, ROW_W), F32) if l == 0 else h_slots)
        y_slots = moe(h_slots, *tile_plan, l, moe_w_gate, moe_w_up, moe_w_down)
        moe_state = (dest, y_slots, x1)

        states.append(st_ctx)
    y_prompt, y_sample = final_norm(*moe_state, mods[DEPTH - 1], final_norm_g)
    return (y_prompt.reshape(BATCH, SEQ, D_MODEL), y_sample.reshape(DEC_BATCH, DEC_SEQ, D_MODEL),
            *new_kv, jnp.stack(states, axis=1))
```

```python
import functools
import math
from typing import Any, NamedTuple

import numpy as np
import jax
import jax.numpy as jnp
from jax import lax
from jax.experimental import pallas as pl
from jax.experimental.pallas import tpu as pltpu

F32 = jnp.float32
BF16 = jnp.bfloat16
HIGHEST = lax.Precision.HIGHEST

D_MODEL = 1024
BATCH = 16
SEQ = 256
DEPTH = 2
DEC_BATCH = 2
DEC_SEQ = 2048
PAST_LEN = 256
GRID_W = 64
GRID_H = DEC_SEQ // GRID_W
EPS = 1e-6
NEG_BIG = -1e30
HEADS = 4
HEAD_DIM = 64
MIX_BLK = HEADS * HEAD_DIM
A_DIM = 32
ROPE_BASE = 10000.0
B_CHUNK = 32
NA_WIN_H = 8
NA_WIN_W = 16
N_EXPERTS = 16
N_GROUPS = 4
D_EXPERT = 512
PROJ_W = 12 * MIX_BLK
TP = BATCH * SEQ
TL = DEC_BATCH * DEC_SEQ
T = TP + TL
TM = 256
N_TILES = T // TM
CTX_TILES = TP // TM
LAT_TILES_PER_SEQ = DEC_SEQ // TM
(C_AQ, C_AK, C_AV, C_BQ, C_BFF, C_BFB, C_BV, C_BG, C_CQ, C_CK, C_CV, C_DU) = range(12)
PM_BLOCKS = (C_AQ, C_AK, C_AV, C_CQ, C_CK, C_CV, C_DU)
PG_BLOCKS = (C_BQ, C_BFF, C_BFB, C_BV, C_BG)
(M_AQ, M_AK, M_AV, M_CQ, M_CK, M_CV, M_DU) = range(len(PM_BLOCKS))
(G_BQ, G_BFF, G_BFB, G_BV, G_BG) = range(len(PG_BLOCKS))
PM_W = len(PM_BLOCKS) * MIX_BLK
PG_W = len(PG_BLOCKS) * MIX_BLK
NA_SLAB_ROWS = 12
NA_SLAB = NA_SLAB_ROWS * GRID_W
LANES = 128
ROW_W = D_MODEL + LANES
EXPERT_PAIRS = ((0, 1), (0, 2), (0, 3), (1, 3), (2, 3), (2, 1))
N_BUCKETS = N_GROUPS * len(EXPERT_PAIRS)
BUCKET_ROWS = 32
N_DMA_PRIORITIES = 2
TM_OUT = 512
TM_SCATTER = 1024
WEIGHT_LOOKAHEAD = 2
WEIGHT_RING = WEIGHT_LOOKAHEAD + 1
TM_MOE = 384
MAX_TILES = T // TM_MOE + N_BUCKETS
N_SLOTS = MAX_TILES * TM_MOE
VMEM_LIMIT = 56 * 1024 * 1024


def _cparams(sem):
    return pltpu.CompilerParams(dimension_semantics=sem, vmem_limit_bytes=VMEM_LIMIT)


class Part(NamedTuple):
    kernel: Any
    in_specs: list
    args: list
    out_specs: list
    out_shapes: list
    scratch: list


def _run_parts_kernel(*refs, layout):
    n_in = sum(n for _, n, _, _ in layout)
    n_out = sum(n for _, _, n, _ in layout)
    ins, outs, scratch = refs[:n_in], refs[n_in:n_in + n_out], refs[n_in + n_out:]
    i = o = s = 0
    for kernel, k_in, k_out, k_scratch in layout:
        kernel(*ins[i:i + k_in], *outs[o:o + k_out], *scratch[s:s + k_scratch])
        i, o, s = i + k_in, o + k_out, s + k_scratch


def run_parts(parts, grid, name):
    layout = tuple((p.kernel, len(p.in_specs), len(p.out_specs), len(p.scratch)) for p in parts)
    outs = pl.pallas_call(
        functools.partial(_run_parts_kernel, layout=layout),
        grid=grid,
        in_specs=[s for p in parts for s in p.in_specs],
        out_specs=[s for p in parts for s in p.out_specs],
        out_shape=[s for p in parts for s in p.out_shapes],
        scratch_shapes=[s for p in parts for s in p.scratch],
        compiler_params=_cparams(("arbitrary", "arbitrary")),
        name=name,
    )(*[a for p in parts for a in p.args])
    result, o = [], 0
    for p in parts:
        result.append(outs[o:o + len(p.out_specs)])
        o += len(p.out_specs)
    return result


def _head_lanes(width=MIX_BLK):
    return lax.broadcasted_iota(jnp.int32, (1, width), 1)


def _lane_range(lane, lo, n):
    return (lane >= lo) & (lane < lo + n)


def _same_head_matrix():
    r = lax.broadcasted_iota(jnp.int32, (MIX_BLK, MIX_BLK), 0) // HEAD_DIM
    c = lax.broadcasted_iota(jnp.int32, (MIX_BLK, MIX_BLK), 1) // HEAD_DIM
    return r == c


def _bf16_pieces(x, n):
    pieces = []
    for _ in range(n):
        piece = x.astype(BF16)
        pieces.append(piece)
        x = x - piece.astype(F32)
    return pieces


def _select_sum_left(onehot_bf16, x):
    cols = x.shape[1]
    stacked = jnp.dot(onehot_bf16, jnp.concatenate(_bf16_pieces(x, 3), axis=1), preferred_element_type=F32)
    return stacked[:, :cols] + stacked[:, cols:2 * cols] + stacked[:, 2 * cols:]


def _select_sum_right(x, onehot_bf16):
    rows = x.shape[0]
    stacked = jnp.dot(jnp.concatenate(_bf16_pieces(x, 3), axis=0), onehot_bf16, preferred_element_type=F32)
    return stacked[:rows] + stacked[rows:2 * rows] + stacked[2 * rows:]


def _head_mean_square(o):
    ones = jnp.where(_same_head_matrix(), 1.0, 0.0).astype(BF16)
    return _select_sum_right(o * o, ones) * (1.0 / HEAD_DIM)


def _mod_row(i):
    return jnp.where(i < CTX_TILES, 0, 1 + (i - CTX_TILES) // LAT_TILES_PER_SEQ)


def _mod_kernel(c_ref, w_ref, b_ref, o_ref):
    w = w_ref[0]
    for r in range(3):
        c = c_ref[r]
        s = c * jax.nn.sigmoid(c)
        o_ref[0, r:r + 1, :] = jnp.sum(s * w, axis=0, keepdims=True) + b_ref[0]


def modulation(c_rows, ada_w, ada_b):
    nt = 768
    n_out = 6 * D_MODEL
    return pl.pallas_call(
        _mod_kernel,
        grid=(DEPTH, n_out // nt),
        in_specs=[
            pl.BlockSpec((3, D_MODEL, 1), lambda l, j: (0, 0, 0)),
            pl.BlockSpec((1, D_MODEL, nt), lambda l, j: (l, 0, j)),
            pl.BlockSpec((1, 1, nt), lambda l, j: (l, 0, j)),
        ],
        out_specs=pl.BlockSpec((1, 3, nt), lambda l, j: (l, 0, j)),
        out_shape=jax.ShapeDtypeStruct((DEPTH, 3, n_out), F32),
        compiler_params=_cparams(("arbitrary", "arbitrary")),
        name="modulation",
    )(c_rows[:, :, None], ada_w, ada_b[:, None, :])


def _is_ctx_tile():
    return pl.program_id(0) < CTX_TILES


def _ctx_tile(i, *_):
    return (jnp.minimum(i, CTX_TILES - 1), 0)


def _lat_tile(i, *_):
    return (jnp.maximum(i - CTX_TILES, 0), 0)


def _proj_kernel(xc_ref, xl_ref, mod_ref, g_ref, w_ref, cos_ref, sa_ref, sb_ref, pm_ref, pg_ref, *cache_refs):
    x = jnp.where(_is_ctx_tile(), xc_ref[...], xl_ref[...])
    _proj_body(x, mod_ref, g_ref, w_ref, cos_ref, sa_ref, sb_ref, pm_ref, pg_ref, cache_refs)


def _proj_after_moe_kernel(dest_ref, y_ref, x1_ref, modp_ref, mod_ref, g_ref, w_ref, cos_ref, sa_ref, sb_ref,
                           *rest):
    x2_ref, pm_ref, pg_ref = rest[4:7]
    cache_refs, (buf_ref, sem) = rest[7:11], rest[11:]
    x2 = _moe_residual(dest_ref, y_ref, buf_ref, sem, x1_ref, modp_ref)
    x2_ref[...] = x2
    _proj_body(x2, mod_ref, g_ref, w_ref, cos_ref, sa_ref, sb_ref, pm_ref, pg_ref, cache_refs)


def _proj_body(x, mod_ref, g_ref, w_ref, cos_ref, sa_ref, sb_ref, pm_ref, pg_ref, cache_refs):
    ms = jnp.mean(x * x, axis=-1, keepdims=True)
    mod = mod_ref[0]
    h = x * lax.rsqrt(ms + EPS) * g_ref[...] * (1.0 + mod[1:2]) + mod[0:1]
    p = jnp.dot(h.astype(BF16), w_ref[0], preferred_element_type=F32)
    t = p[:, :2 * MIX_BLK]
    pm_ref[:, :2 * MIX_BLK] = (t * cos_ref[...] + pltpu.roll(t, 1, 1) * sa_ref[...]
                               + pltpu.roll(t, 2 * MIX_BLK - 1, 1) * sb_ref[...]).astype(BF16)
    pm_ref[:, 2 * MIX_BLK:] = p[:, 2 * MIX_BLK:PM_W].astype(BF16)
    pg_ref[...] = p[:, PM_W:]

    @pl.when(_is_ctx_tile())
    def _():
        for ref, col in zip(cache_refs, (M_AK, M_AV, M_CK, M_CV)):
            for hd in range(HEADS):
                lo = col * MIX_BLK + hd * HEAD_DIM
                ref[0, 0, hd] = p[:, lo:lo + HEAD_DIM]
            if ref.shape[1] > 1:
                ref[0, 1:] = jnp.zeros((ref.shape[1] - 1,) + tuple(ref.shape[2:]), F32)


_CACHE_SHAPE = jax.ShapeDtypeStruct((BATCH, DEPTH, HEADS, SEQ, HEAD_DIM), F32)


def _cache_spec(layer):
    n_layers = DEPTH if layer == 0 else 1
    return pl.BlockSpec((1, n_layers, HEADS, SEQ, HEAD_DIM),
                        lambda i, *_: (jnp.minimum(i, CTX_TILES - 1), layer, 0, 0, 0))


def _rope_tables():
    nf = A_DIM // 4
    freqs = ROPE_BASE ** (-np.arange(nf, dtype=np.float64) / nf)
    pos = np.arange(DEC_SEQ)
    row = (pos // GRID_W).astype(np.float64)
    col = (pos % GRID_W).astype(np.float64)
    ang = np.concatenate([row[:, None] * freqs, col[:, None] * freqs], axis=-1)
    cos = np.repeat(np.cos(ang), 2, axis=-1)
    sin = np.repeat(np.sin(ang), 2, axis=-1)
    odd = (np.arange(A_DIM) % 2 == 1)[None, :]
    sin_from_left = np.where(odd, sin, 0.0)
    sin_from_right = np.where(odd, 0.0, -sin)
    reps = 2 * MIX_BLK // A_DIM
    ident = (np.ones((TM, 2 * MIX_BLK)), np.zeros((TM, 2 * MIX_BLK)))
    return tuple(jnp.asarray(np.concatenate([np.tile(t, (1, reps)), tail], axis=0), F32)
                 for t, tail in ((cos, ident[0]), (sin_from_left, ident[1]), (sin_from_right, ident[1])))


def _rope_block(i):
    return (jnp.where(i < CTX_TILES, LAT_TILES_PER_SEQ, (i - CTX_TILES) % LAT_TILES_PER_SEQ), 0)


def projection_after_moe(dest, y_slots, x1, mod_prev, mod, layer, norm_g, w_in_bf16, rope, caches):
    rope_spec = pl.BlockSpec((TM, 2 * MIX_BLK), lambda i, d: _rope_block(i))
    mod_spec = pl.BlockSpec((1, 6, D_MODEL), lambda i, d: (_mod_row(i), 0, 0))
    n_in = 10
    return pl.pallas_call(
        _proj_after_moe_kernel,
        grid_spec=pltpu.PrefetchScalarGridSpec(
            num_scalar_prefetch=1,
            grid=(N_TILES,),
            in_specs=[pl.BlockSpec(memory_space=pl.ANY),
                      pl.BlockSpec((TM, D_MODEL), lambda i, d: (i, 0)),
                      mod_spec, mod_spec,
                      pl.BlockSpec((1, D_MODEL), lambda i, d: (0, 0)),
                      pl.BlockSpec((1, D_MODEL, PROJ_W), lambda i, d: (layer, 0, 0)),
                      rope_spec, rope_spec, rope_spec] + [pl.BlockSpec(memory_space=pl.ANY)] * 4,
            out_specs=[pl.BlockSpec((TM, D_MODEL), lambda i, d: (i, 0)),
                       pl.BlockSpec((TM, PM_W), lambda i, d: (i, 0)),
                       pl.BlockSpec((TM, PG_W), lambda i, d: (i, 0))] + [_cache_spec(layer)] * 4,
            scratch_shapes=_GATHER_SCRATCH,
        ),
        out_shape=[jax.ShapeDtypeStruct((T, D_MODEL), F32), jax.ShapeDtypeStruct((T, PM_W), BF16),
                   jax.ShapeDtypeStruct((T, PG_W), F32)] + [_CACHE_SHAPE] * 4,
        input_output_aliases={n_in + n: 3 + n for n in range(4)},
        compiler_params=_cparams(("arbitrary",)),
        name="projection_after_moe",
    )(dest, y_slots, x1, mod_prev, mod, norm_g[None, :], w_in_bf16, *rope, *caches)


def projection(x_ctx, x_lat, mod, norm_g, w_in_bf16, rope):
    layer = 0
    rope_spec = pl.BlockSpec((TM, 2 * MIX_BLK), _rope_block)
    return pl.pallas_call(
        _proj_kernel,
        grid=(N_TILES,),
        in_specs=[
            pl.BlockSpec((TM, D_MODEL), _ctx_tile),
            pl.BlockSpec((TM, D_MODEL), _lat_tile),
            pl.BlockSpec((1, 6, D_MODEL), lambda i: (_mod_row(i), 0, 0)),
            pl.BlockSpec((1, D_MODEL), lambda i: (0, 0)),
            pl.BlockSpec((1, D_MODEL, PROJ_W), lambda i: (layer, 0, 0)),
            rope_spec, rope_spec, rope_spec,
        ],
        out_specs=[pl.BlockSpec((TM, PM_W), lambda i: (i, 0)), pl.BlockSpec((TM, PG_W), lambda i: (i, 0))]
        + [_cache_spec(layer)] * 4,
        out_shape=[jax.ShapeDtypeStruct((T, PM_W), BF16), jax.ShapeDtypeStruct((T, PG_W), F32)]
        + [_CACHE_SHAPE] * 4,
        compiler_params=_cparams(("arbitrary",)),
        name="projection",
    )(x_ctx, x_lat, mod, norm_g[None, :], w_in_bf16, *rope)


LOG2_E = 1.4426950408889634


def _exp2_rows(s):
    e = jnp.exp2(s - jnp.max(s, axis=-1, keepdims=True))
    return e, 1.0 / jnp.sum(e, axis=-1, keepdims=True)


def _attn_kernel(lam_ref, q_ref, k_ref, v_ref, *rest, n_maps, post_scale, with_cache):
    if with_cache:
        kc_ref, vc_ref, g_ref, o_ref, kt_ref, vb_ref = rest
    else:
        g_ref, o_ref, kt_ref, vb_ref = rest

    @pl.when(pl.program_id(1) == 0)
    def _():
        k = k_ref[...].astype(F32)
        v = v_ref[...]
        if with_cache:
            k = jnp.concatenate([_cache_heads_on_lanes(kc_ref), k], axis=0)
            v = jnp.concatenate([_cache_heads_on_lanes(vc_ref).astype(BF16), v], axis=0)
        kt_ref[...] = k.T.astype(BF16)
        vb_ref[...] = v

    lane = _head_lanes()
    map_dim = HEAD_DIM // n_maps
    q = q_ref[...].astype(F32) * (map_dim ** -0.5 * LOG2_E)
    kt = kt_ref[...]
    vb = vb_ref[...]
    stack_rows = kt_ref.shape[1] <= TM
    weights = []
    for h in range(HEADS):
        masked = [jnp.where(_lane_range(lane, h * HEAD_DIM + j * map_dim, map_dim), q, 0.0).astype(BF16)
                  for j in range(n_maps)]
        if stack_rows:
            s = jnp.dot(jnp.concatenate(masked, axis=0), kt, preferred_element_type=F32)
            scores = [s[j * TM:(j + 1) * TM] for j in range(n_maps)]
        else:
            scores = [jnp.dot(m, kt, preferred_element_type=F32) for m in masked]
        parts = [_exp2_rows(x) for x in scores]
        w = parts[0][0] * parts[0][1]
        if n_maps == 2:
            w = w - parts[1][0] * (lam_ref[0] * parts[1][1])
        weights.append(w.astype(BF16))
    if stack_rows:
        oh = jnp.dot(jnp.concatenate(weights, axis=0), vb, preferred_element_type=F32)
        outs = [oh[h * TM:(h + 1) * TM] for h in range(HEADS)]
    else:
        outs = [jnp.dot(w, vb, preferred_element_type=F32) for w in weights]
    o = jnp.zeros(q.shape, F32)
    for h in range(HEADS):
        o = jnp.where(_lane_range(lane, h * HEAD_DIM, HEAD_DIM), outs[h], o)
    if n_maps == 2:
        o = o * lax.rsqrt(_head_mean_square(o) + EPS) * g_ref[...] * post_scale
    o_ref[...] = o


def _cache_block_spec(layer):
    return pl.BlockSpec((1, 1, HEADS, PAST_LEN, HEAD_DIM), lambda b, i: (b, layer, 0, 0, 0))


def _cache_heads_on_lanes(ref):
    return jnp.concatenate([ref[0, 0, h] for h in range(HEADS)], axis=1)


def attention(p, row_blk0, cols, n_seq, seq_len, lam, norm_g, *, n_maps, post_scale, cache=None, layer=0):
    nb = seq_len // TM
    kv_len = seq_len + (PAST_LEN if cache is not None else 0)
    kern = functools.partial(_attn_kernel, n_maps=n_maps, post_scale=post_scale, with_cache=cache is not None)
    kv_spec = lambda col: pl.BlockSpec((seq_len, MIX_BLK), lambda b, i: (row_blk0 + b, col))
    cache_specs = [_cache_block_spec(layer)] * 2 if cache is not None else []
    return Part(
        kernel=kern,
        in_specs=[
            pl.BlockSpec(memory_space=pltpu.SMEM),
            pl.BlockSpec((TM, MIX_BLK), lambda b, i: ((row_blk0 + b) * nb + i, cols[0])),
            kv_spec(cols[1]), kv_spec(cols[2]), *cache_specs,
            pl.BlockSpec((1, MIX_BLK), lambda b, i: (0, 0)),
        ],
        out_specs=[pl.BlockSpec((TM, MIX_BLK), lambda b, i: (b * nb + i, 0))],
        out_shapes=[jax.ShapeDtypeStruct((n_seq * seq_len, MIX_BLK), F32)],
        scratch=[pltpu.VMEM((MIX_BLK, kv_len), BF16), pltpu.VMEM((kv_len, MIX_BLK), BF16)],
        args=[lam, p, p, p, *(cache or ()), norm_g])


def _na_slab_start(i):
    return jnp.clip(i - 1, 0, GRID_H // 4 - NA_SLAB_ROWS // 4)


def _na_kernel(q_ref, k_ref, v_ref, kc_ref, vc_ref, bias_ref, o_ref):
    i = pl.program_id(1)
    start = pl.multiple_of(_na_slab_start(i) * TM, TM)
    ks_t = k_ref[pl.ds(start, NA_SLAB), :].astype(F32).T.astype(BF16)
    vs = v_ref[pl.ds(start, NA_SLAB), :]
    kc_t = _cache_heads_on_lanes(kc_ref).T.astype(BF16)
    vc = _cache_heads_on_lanes(vc_ref).astype(BF16)
    q = q_ref[...].astype(F32) * (HEAD_DIM ** -0.5)
    lane = _head_lanes()
    heads = [_lane_range(lane, h * HEAD_DIM, HEAD_DIM) for h in range(HEADS)]
    qm = jnp.concatenate([jnp.where(in_head, q, 0.0) for in_head in heads], axis=0).astype(BF16)
    s_loc = jnp.dot(qm, ks_t, preferred_element_type=F32)
    s_ctx = jnp.dot(qm, kc_t, preferred_element_type=F32)
    e_locs, e_ctxs, dens = [], [], []
    for h in range(HEADS):
        rows = slice(h * TM, (h + 1) * TM)
        sl = s_loc[rows] + bias_ref[0, h]
        sc = s_ctx[rows]
        m = jnp.maximum(jnp.max(sl, axis=-1, keepdims=True), jnp.max(sc, axis=-1, keepdims=True))
        e_loc = jnp.exp(sl - m)
        e_ctx = jnp.exp(sc - m)
        dens.append(jnp.sum(e_loc, axis=-1, keepdims=True) + jnp.sum(e_ctx, axis=-1, keepdims=True))
        e_locs.append(e_loc.astype(BF16))
        e_ctxs.append(e_ctx.astype(BF16))
    oh = (jnp.dot(jnp.concatenate(e_locs, axis=0), vs, preferred_element_type=F32)
          + jnp.dot(jnp.concatenate(e_ctxs, axis=0), vc, preferred_element_type=F32))
    o = jnp.zeros(q.shape, F32)
    for h in range(HEADS):
        o = jnp.where(heads[h], oh[h * TM:(h + 1) * TM] / dens[h], o)
    o_ref[...] = o


def _na_bias_tables(rpb):
    n_dr, n_dc = 2 * NA_WIN_H - 1, 2 * NA_WIN_W - 1
    cq = np.arange(GRID_W)[:, None]
    ck = np.arange(GRID_W)[None, :]
    wc0 = np.clip(cq - NA_WIN_W // 2, 0, GRID_W - NA_WIN_W)
    col_ok = (ck >= wc0) & (ck < wc0 + NA_WIN_W)
    col_pick = np.clip(ck - cq + NA_WIN_W - 1, 0, n_dc - 1)[..., None] == np.arange(n_dc)
    by_col = jnp.einsum("hab,qcb->haqc", rpb.astype(F32), jnp.asarray(col_pick, F32), precision=HIGHEST)
    margin = 4
    by_col = jnp.pad(by_col.transpose(0, 2, 1, 3), ((0, 0), (0, 0), (margin, margin), (0, 0)))
    by_col = by_col.reshape(HEADS, GRID_W, (n_dr + 2 * margin) * GRID_W)
    pieces, row_ok = [], []
    for tile in (0, 1, GRID_H // 4 - 1):
        slab0 = int(np.clip(tile - 1, 0, GRID_H // 4 - NA_SLAB_ROWS // 4)) * 4
        rq = tile * 4 + np.arange(4)
        rk = (slab0 + np.arange(NA_SLAB) // GRID_W)[None, :]
        wr0 = np.clip(rq - NA_WIN_H // 2, 0, GRID_H - NA_WIN_H)[:, None]
        row_ok.append((rk >= wr0) & (rk < wr0 + NA_WIN_H))
        for r in rq:
            first = slab0 - int(r) + NA_WIN_H - 1 + margin
            assert 0 <= first and first + NA_SLAB_ROWS <= n_dr + 2 * margin
            pieces.append(by_col[:, :, first * GRID_W:first * GRID_W + NA_SLAB])
    table = jnp.stack(pieces).reshape(3, 4, HEADS, GRID_W, NA_SLAB).transpose(0, 2, 1, 3, 4)
    valid = np.stack(row_ok)[:, None, :, None, :] & np.tile(col_ok, (1, NA_SLAB_ROWS))[None, None, None]
    table = jnp.where(jnp.asarray(valid), table, NEG_BIG)
    return table.reshape(3, HEADS, TM, NA_SLAB)


def na_latent(p, kc, vc, layer, bias):
    n_t = LAT_TILES_PER_SEQ
    seq_blk0 = TP // DEC_SEQ

    def bias_idx(b, i):
        return (jnp.minimum(i, 1) + i // (n_t - 1), 0, 0, 0)

    return Part(
        kernel=_na_kernel,
        in_specs=[
            pl.BlockSpec((TM, MIX_BLK), lambda b, i: (CTX_TILES + b * n_t + i, M_CQ)),
            pl.BlockSpec((DEC_SEQ, MIX_BLK), lambda b, i: (seq_blk0 + b, M_CK)),
            pl.BlockSpec((DEC_SEQ, MIX_BLK), lambda b, i: (seq_blk0 + b, M_CV)),
            _cache_block_spec(layer), _cache_block_spec(layer),
            pl.BlockSpec((1, HEADS, TM, NA_SLAB), bias_idx),
        ],
        out_specs=[pl.BlockSpec((TM, MIX_BLK), lambda b, i: (b * n_t + i, 0))],
        out_shapes=[jax.ShapeDtypeStruct((TL, MIX_BLK), F32)],
        scratch=[],
        args=[p, p, p, kc, vc, bias])


MAX_EXPONENT = 80.0


def _hgrn_direction(q_ref, f_ref, v_ref, lb, st_ref, o_ref, reverse):
    n_ch = TM // B_CHUNK
    r_idx = lax.broadcasted_iota(jnp.int32, (TM, TM), 0)
    c_idx = lax.broadcasted_iota(jnp.int32, (TM, TM), 1)
    tri = (c_idx >= r_idx) if reverse else (c_idx <= r_idx)
    zq = q_ref[...]
    q = zq * jax.nn.sigmoid(zq)
    z = f_ref[...]
    gate = (1.0 - lb) * jax.nn.sigmoid(z)
    logf = jnp.log(lb + gate)
    kk = (1.0 - lb) - gate
    b = _select_sum_left(jnp.where(tri, 1.0, 0.0).astype(BF16), logf)
    b3 = b.reshape(n_ch, B_CHUNK, MIX_BLK)
    mid = B_CHUNK // 2 if reverse else B_CHUNK // 2 - 1
    q_in = (q.reshape(b3.shape) * jnp.exp(b3 - b3[:, mid:mid + 1, :])).reshape(TM, MIX_BLK)
    q_dec = (q * jnp.exp(b)).astype(BF16)
    b_t = b.T
    kk_t = kk.T
    far = 0 if reverse else TM - 1
    b_far = b_t[:, far:far + 1]
    k_dec_t = (kk_t * jnp.exp(b_far - b_t)).astype(BF16)
    vb = v_ref[...].astype(BF16)
    st = st_ref[...]
    o_state = jnp.dot(q_dec, st.astype(BF16), preferred_element_type=F32)
    kv = jnp.dot(k_dec_t, vb, preferred_element_type=F32)
    st_ref[...] = st * jnp.exp(b_far) + jnp.where(_same_head_matrix(), kv, 0.0)
    lane = _head_lanes()
    token = lax.broadcasted_iota(jnp.int32, (1, TM), 1)
    local = lax.broadcasted_iota(jnp.int32, (HEADS * B_CHUNK, 1), 0) % B_CHUNK
    heads = [_lane_range(lane, h * HEAD_DIM, HEAD_DIM) for h in range(HEADS)]
    weights = []
    for c in range(n_ch):
        ref = b_t[:, c * B_CHUNK + mid:c * B_CHUNK + mid + 1]
        k_c_t = (kk_t * jnp.exp(jnp.minimum(ref - b_t, MAX_EXPONENT))).astype(BF16)
        q_c = q_in[c * B_CHUNK:(c + 1) * B_CHUNK, :]
        lhs = jnp.concatenate([jnp.where(in_head, q_c, 0.0) for in_head in heads], axis=0)
        a = jnp.dot(lhs.astype(BF16), k_c_t, preferred_element_type=F32)
        t_abs = c * B_CHUNK + local
        weights.append(jnp.where((token >= t_abs) if reverse else (token <= t_abs), a, 0.0).astype(BF16))
    res = jnp.dot(jnp.concatenate(weights, axis=0), vb, preferred_element_type=F32)
    for c in range(n_ch):
        rows = slice(c * B_CHUNK, (c + 1) * B_CHUNK)
        o_c = o_state[rows, :]
        for h, in_head in enumerate(heads):
            lo = (c * HEADS + h) * B_CHUNK
            o_c = o_c + jnp.where(in_head, res[lo:lo + B_CHUNK, :], 0.0)
        o_ref[rows, :] = o_c


def _hgrn_kernel(qf_ref, ff_ref, vf_ref, qb_ref, fb_ref, vb_ref, lb_ref, s0_ref,
                 of_ref, ob_ref, s_ref, stf_ref, stb_ref, *, has_s0):
    j = pl.program_id(1)

    @pl.when(j == 0)
    def _():
        if has_s0:
            stf_ref[...] = s0_ref[0, 0]
            stb_ref[...] = s0_ref[0, 1]
        else:
            stf_ref[...] = jnp.zeros((MIX_BLK, MIX_BLK), F32)
            stb_ref[...] = jnp.zeros((MIX_BLK, MIX_BLK), F32)

    lb = lb_ref[...]
    _hgrn_direction(qf_ref, ff_ref, vf_ref, lb[0:1], stf_ref, of_ref, False)
    _hgrn_direction(qb_ref, fb_ref, vb_ref, lb[1:2], stb_ref, ob_ref, True)

    @pl.when(j == pl.num_programs(1) - 1)
    def _():
        for d, st_ref in enumerate((stf_ref, stb_ref)):
            s = st_ref[...]
            for hd in range(HEADS):
                lo = hd * HEAD_DIM
                s_ref[0, d, hd] = s[lo:lo + HEAD_DIM, lo:lo + HEAD_DIM]


def hgrn(p, row_tile0, n_seq, seq_len, lb, s0):
    nb = seq_len // TM
    has_s0 = s0 is not None
    if s0 is None:
        s0 = jnp.zeros((1, 2, MIX_BLK, MIX_BLK), F32)

    def fwd(col):
        return pl.BlockSpec((TM, MIX_BLK), lambda s, j: (row_tile0 + s * nb + j, col))

    def bwd(col):
        return pl.BlockSpec((TM, MIX_BLK), lambda s, j: (row_tile0 + s * nb + nb - 1 - j, col))

    state_spec = pl.BlockSpec((1, 2, MIX_BLK, MIX_BLK), lambda s, j: (s if has_s0 else 0, 0, 0, 0))
    out_rows = n_seq * seq_len
    return Part(
        kernel=functools.partial(_hgrn_kernel, has_s0=has_s0),
        in_specs=[fwd(G_BQ), fwd(G_BFF), fwd(G_BV), bwd(G_BQ), bwd(G_BFB), bwd(G_BV),
                  pl.BlockSpec((2, MIX_BLK), lambda s, j: (0, 0)), state_spec],
        out_specs=[
            pl.BlockSpec((TM, MIX_BLK), lambda s, j: (s * nb + j, 0)),
            pl.BlockSpec((TM, MIX_BLK), lambda s, j: (s * nb + nb - 1 - j, 0)),
            pl.BlockSpec((1, 2, HEADS, HEAD_DIM, HEAD_DIM), lambda s, j: (s, 0, 0, 0, 0)),
        ],
        out_shapes=[jax.ShapeDtypeStruct((out_rows, MIX_BLK), F32),
                    jax.ShapeDtypeStruct((out_rows, MIX_BLK), F32),
                    jax.ShapeDtypeStruct((n_seq, 2, HEADS, HEAD_DIM, HEAD_DIM), F32)],
        scratch=[pltpu.VMEM((MIX_BLK, MIX_BLK), F32), pltpu.VMEM((MIX_BLK, MIX_BLK), F32)],
        args=[p, p, p, p, p, p, lb, s0])


def _state_to_blockdiag(s):
    eye = jnp.eye(HEADS, dtype=F32)
    full = s.astype(F32)[:, :, :, :, None, :] * eye[None, None, :, None, :, None]
    return full.reshape(s.shape[0], 2, MIX_BLK, MIX_BLK)


def _fft_kernel(u_ref, cs64_ref, csl_ref, o_ref, ab_ref, *, norm):
    seq_len = u_ref.shape[0]

    @pl.when(pl.program_id(1) == 0)
    def _():
        ab = jnp.dot(u_ref[...], cs64_ref[...], preferred_element_type=F32).astype(BF16)
        ab_ref[:seq_len, :] = ab[:, :MIX_BLK]
        ab_ref[seq_len:, :] = ab[:, MIX_BLK:]

    o_ref[...] = jnp.dot(csl_ref[...], ab_ref[...], preferred_element_type=F32) * norm


def _dft_tables(n):
    k = np.arange(n)
    ang = 2.0 * np.pi * ((k[:, None] * k[None, :]) % n) / n
    return np.cos(ang), np.sin(ang)


def _dft_constants(seq_len):
    c64, s64 = _dft_tables(HEAD_DIM)
    eye = np.eye(HEADS)
    cl, sl = _dft_tables(seq_len)
    as_bf16 = lambda a: jnp.asarray(a, F32).astype(BF16)
    return (as_bf16(np.concatenate([np.kron(eye, c64), -np.kron(eye, s64)], axis=1)),
            as_bf16(np.concatenate([cl, sl], axis=1)))


def fourier_mix(p, row_blk0, n_seq, seq_len, consts):
    cs64, csl = consts
    nb = seq_len // TM
    norm = 1.0 / math.sqrt(seq_len * HEAD_DIM)
    return Part(
        kernel=functools.partial(_fft_kernel, norm=norm),
        in_specs=[
            pl.BlockSpec((seq_len, MIX_BLK), lambda s, i: (row_blk0 + s, M_DU)),
            pl.BlockSpec((MIX_BLK, 2 * MIX_BLK), lambda s, i: (0, 0)),
            pl.BlockSpec((TM, 2 * seq_len), lambda s, i: (i, 0)),
        ],
        out_specs=[pl.BlockSpec((TM, MIX_BLK), lambda s, i: (s * nb + i, 0))],
        out_shapes=[jax.ShapeDtypeStruct((n_seq * seq_len, MIX_BLK), F32)],
        scratch=[pltpu.VMEM((2 * seq_len, MIX_BLK), BF16)],
        args=[p, cs64, csl])


def _route(logits_t, rb):
    per = N_EXPERTS // N_GROUPS
    score = [jax.nn.sigmoid(logits_t[e:e + 1, :]) for e in range(N_EXPERTS)]
    sel = [score[e] + rb[e:e + 1, :] for e in range(N_EXPERTS)]
    gscore = []
    for g in range(N_GROUPS):
        vals = sel[g * per:(g + 1) * per]
        best = None
        for a in range(per):
            for b in range(a + 1, per):
                pair = vals[a] + vals[b]
                best = pair if best is None else jnp.maximum(best, pair)
        gscore.append(best)
    chosen = []
    for g in range(N_GROUPS):
        ok = None
        for j in range(N_GROUPS):
            if j == g:
                continue
            cond = gscore[g] > gscore[j] if j < g else gscore[g] >= gscore[j]
            ok = cond if ok is None else ok & cond
        chosen.append(ok)
    picked = []
    for e in range(N_EXPERTS):
        g = e // per
        rank = jnp.zeros_like(sel[e])
        for j in range(g * per, (g + 1) * per):
            if j == e:
                continue
            ahead = sel[j] >= sel[e] if j < e else sel[j] > sel[e]
            rank = rank + jnp.where(ahead, 1.0, 0.0)
        picked.append(chosen[g] & (rank < 2.0))
    wsum = jnp.zeros_like(score[0])
    for e in range(N_EXPERTS):
        wsum = wsum + jnp.where(picked[e], score[e], 0.0)
    bucket = jnp.zeros_like(wsum)
    w_a = jnp.zeros_like(wsum)
    w_b = jnp.zeros_like(wsum)
    for g in range(N_GROUPS):
        for n, (a, b) in enumerate(EXPERT_PAIRS):
            hit = picked[g * per + a] & picked[g * per + b]
            bucket = jnp.where(hit, float(g * len(EXPERT_PAIRS) + n), bucket)
            w_a = jnp.where(hit, score[g * per + a] / wsum, w_a)
            w_b = jnp.where(hit, score[g * per + b] / wsum, w_b)
    return bucket, w_a, w_b


def _out_kernel(*refs):
    streams, rest = refs[:12], refs[12:]
    (bg_ref, mod_ref, hg_ref, w_ref, g2_ref, rw_ref, rb_ref,
     x1_ref, h2_ref, bucket_ref, rank_ref, counts_ref, run_ref) = rest
    is_ctx = pl.program_id(0) < TP // TM_OUT
    x, o_a, o_f, o_b, o_c, o_d = (jnp.where(is_ctx, streams[2 * n][...], streams[2 * n + 1][...])
                                  for n in range(6))

    @pl.when(pl.program_id(0) == 0)
    def _():
        run_ref[...] = jnp.zeros(run_ref.shape, F32)

    mod = mod_ref[0]
    hb = o_f + o_b
    zg = bg_ref[...]
    hb = hb * lax.rsqrt(_head_mean_square(hb) + EPS) * hg_ref[...] * (zg * jax.nn.sigmoid(zg))
    mixers = jnp.concatenate([part.astype(BF16) for part in (o_a, hb, o_c, o_d)], axis=1)
    mixed = jnp.dot(mixers, w_ref[0], preferred_element_type=F32)
    x1 = x + mod[2:3] * mixed
    x1_ref[...] = x1
    ms = jnp.mean(x1 * x1, axis=-1, keepdims=True)
    h2 = x1 * lax.rsqrt(ms + EPS) * g2_ref[...] * (1.0 + mod[4:5]) + mod[3:4]
    rw = rw_ref[...]
    r = jnp.dot(jnp.concatenate(_bf16_pieces(h2, 2), axis=0), rw, preferred_element_type=F32)
    r = r[:TM_OUT] + r[TM_OUT:]
    bucket, w_a, w_b = _route((r[:, :LANES] + r[:, LANES:]).T, rb_ref[...])
    h2_ref[:, :D_MODEL] = h2
    h2_ref[:, D_MODEL:] = jnp.concatenate([w_a, w_b, jnp.zeros((LANES - 2, TM_OUT), F32)], axis=0).T
    onehot = jnp.where(lax.broadcasted_iota(jnp.int32, (BUCKET_ROWS, 1), 0).astype(F32) == bucket, 1.0, 0.0)
    s_idx = lax.broadcasted_iota(jnp.int32, (TM_OUT, TM_OUT), 0)
    t_idx = lax.broadcasted_iota(jnp.int32, (TM_OUT, TM_OUT), 1)
    prefix = jnp.dot(onehot.astype(BF16), jnp.where(s_idx <= t_idx, 1.0, 0.0).astype(BF16),
                     preferred_element_type=F32)
    run = run_ref[...]
    rank = jnp.sum(onehot * (prefix - 1.0 + run[:, 0:1]), axis=0, keepdims=True)
    run = run + jnp.sum(onehot, axis=1, keepdims=True)
    run_ref[...] = run
    bucket_ref[...] = bucket.astype(jnp.int32)
    rank_ref[...] = rank.astype(jnp.int32)
    counts_ref[...] = run


def out_and_route(x_pair, x_is_combined, mixer_pairs, p, mod, layer, hgrn_g, w_out_bf16, norm2_g, router_pieces,
                  router_b):
    n_ctx = TP // TM_OUT
    ctx_tile = lambda i: (jnp.minimum(i, n_ctx - 1), 0)
    lat_tile = lambda i: (jnp.maximum(i - n_ctx, 0), 0)
    mod_row = lambda i: jnp.where(i < n_ctx, 0, 1 + (i - n_ctx) // (DEC_SEQ // TM_OUT))
    tile = lambda w: pl.BlockSpec((TM_OUT, w), lambda i: (i, 0))
    full = lambda r, c: pl.BlockSpec((r, c), lambda i: (0, 0))
    stream_specs = [pl.BlockSpec((TM_OUT, D_MODEL), ctx_tile),
                    pl.BlockSpec((TM_OUT, D_MODEL),
                                 (lambda i: (jnp.maximum(i, n_ctx), 0)) if x_is_combined else lat_tile)]
    stream_args = list(x_pair)
    for o_ctx, o_lat in mixer_pairs:
        stream_specs += [pl.BlockSpec((TM_OUT, MIX_BLK), ctx_tile), pl.BlockSpec((TM_OUT, MIX_BLK), lat_tile)]
        stream_args += [o_ctx, o_lat]
    return pl.pallas_call(
        _out_kernel,
        grid=(T // TM_OUT,),
        in_specs=stream_specs + [
            pl.BlockSpec((TM_OUT, MIX_BLK), lambda i: (i, G_BG)),
            pl.BlockSpec((1, 6, D_MODEL), lambda i: (mod_row(i), 0, 0)),
            full(1, MIX_BLK), pl.BlockSpec((1, D_MODEL, D_MODEL), lambda i: (layer, 0, 0)), full(1, D_MODEL),
            full(D_MODEL, 2 * LANES), full(N_EXPERTS, 1),
        ],
        out_specs=[tile(D_MODEL), tile(ROW_W), pl.BlockSpec((1, TM_OUT), lambda i: (0, i)),
                   pl.BlockSpec((1, TM_OUT), lambda i: (0, i)), full(BUCKET_ROWS, LANES)],
        out_shape=[jax.ShapeDtypeStruct((T, D_MODEL), F32),
                   jax.ShapeDtypeStruct((T, ROW_W), F32),
                   jax.ShapeDtypeStruct((1, T), jnp.int32),
                   jax.ShapeDtypeStruct((1, T), jnp.int32),
                   jax.ShapeDtypeStruct((BUCKET_ROWS, LANES), F32)],
        scratch_shapes=[pltpu.VMEM((BUCKET_ROWS, LANES), F32)],
        compiler_params=_cparams(("arbitrary",)),
        name="out_and_route",
    )(*stream_args, p, mod, jnp.tile(hgrn_g, HEADS)[None, :], w_out_bf16,
      norm2_g[None, :], router_pieces, router_b[:, None])


def _router_pieces(router_w):
    hi, lo = _bf16_pieces(router_w.astype(F32), 2)
    pad = lambda a: jnp.pad(a, ((0, 0), (0, LANES - N_EXPERTS)))
    return jnp.concatenate([pad(hi), pad(lo)], axis=1)


def routing_plan(bucket, rank, counts):
    counts = counts[:N_BUCKETS, 0].astype(jnp.int32)
    n_tiles = (counts + TM_MOE - 1) // TM_MOE
    tile_end = jnp.cumsum(n_tiles)
    tile_start = tile_end - n_tiles
    buckets = jnp.arange(N_BUCKETS, dtype=jnp.int32)
    start_of_token = jnp.sum(jnp.where(bucket[0][:, None] == buckets[None, :], tile_start[None, :], 0), axis=1)
    dest = start_of_token * TM_MOE + rank[0]
    tiles = jnp.arange(MAX_TILES, dtype=jnp.int32)
    valid = tiles < tile_end[-1]
    tile_bucket = jnp.sum((jnp.minimum(tiles, tile_end[-1] - 1)[:, None] >= tile_end[None, :]).astype(jnp.int32), axis=1)
    pair_a = np.array([a for a, _ in EXPERT_PAIRS], np.int32)
    pair_b = np.array([b for _, b in EXPERT_PAIRS], np.int32)
    per = N_EXPERTS // N_GROUPS
    exp_a = jnp.asarray((np.arange(N_BUCKETS) // len(EXPERT_PAIRS)) * per + np.tile(pair_a, N_GROUPS), jnp.int32)
    exp_b = jnp.asarray((np.arange(N_BUCKETS) // len(EXPERT_PAIRS)) * per + np.tile(pair_b, N_GROUPS), jnp.int32)
    pick = tile_bucket[:, None] == buckets[None, :]
    tile_a = jnp.sum(jnp.where(pick, exp_a[None, :], 0), axis=1)
    tile_b = jnp.sum(jnp.where(pick, exp_b[None, :], 0), axis=1)
    plan = [dest.astype(jnp.int32), valid.astype(jnp.int32), (tile_end[-1:] - 1).astype(jnp.int32)]
    for tile_e in (tile_a, tile_b):
        changed = jnp.concatenate([jnp.ones((1,), bool), tile_e[1:] != tile_e[:-1]])
        load = (valid & changed).astype(jnp.int32)
        ring = (jnp.cumsum(load) - 1) % WEIGHT_RING
        plan += [tile_e.astype(jnp.int32), load, ring.astype(jnp.int32)]
    return plan


def _row_copy(src, src_row, dst, dst_row, sem):
    return pltpu.make_async_copy(src.at[pl.ds(src_row, 1), :], dst.at[pl.ds(dst_row, 1), :], sem)


def _scatter_kernel(dest_ref, h_ref, init_ref, o_ref, sem):
    del init_ref
    base = pl.program_id(0) * TM_SCATTER

    for r in range(TM_SCATTER):
        _row_copy(h_ref, r, o_ref, dest_ref[base + r], sem).start(priority=r % N_DMA_PRIORITIES)
    pltpu.make_async_copy(h_ref, o_ref.at[pl.ds(0, TM_SCATTER), :], sem).wait()


def scatter_to_slots(h2, dest, slots):
    return pl.pallas_call(
        _scatter_kernel,
        grid_spec=pltpu.PrefetchScalarGridSpec(
            num_scalar_prefetch=1,
            grid=(T // TM_SCATTER,),
            in_specs=[pl.BlockSpec((TM_SCATTER, ROW_W), lambda i, d: (i, 0)),
                      pl.BlockSpec(memory_space=pl.ANY)],
            out_specs=pl.BlockSpec(memory_space=pl.ANY),
            scratch_shapes=[pltpu.SemaphoreType.DMA(())],
        ),
        out_shape=jax.ShapeDtypeStruct((N_SLOTS, ROW_W), F32),
        input_output_aliases={2: 0},
        compiler_params=_cparams(("arbitrary",)),
        name="scatter_to_slots",
    )(dest, h2, slots)


def _moe_kernel(valid_ref, last_ref, ta_ref, la_ref, ra_ref, tb_ref, lb_ref, rb_ref, h_ref, wg_hbm, wu_hbm, wd_hbm,
                o_ref, wg_buf, wu_buf, wd_buf, sem, *, layer):
    del last_ref
    i = pl.program_id(0)
    plans = ((ta_ref, la_ref, ra_ref), (tb_ref, lb_ref, rb_ref))

    def copies(tile, position):
        expert_ref, _, ring_ref = plans[position]
        expert, entry = expert_ref[tile], ring_ref[tile]
        return [pltpu.make_async_copy(hbm.at[layer, expert], buf.at[position, entry], sem.at[position, entry])
                for hbm, buf in ((wg_hbm, wg_buf), (wu_hbm, wu_buf), (wd_hbm, wd_buf))]

    def start_loads(tile):
        for position in range(2):
            @pl.when(plans[position][1][tile] == 1)
            def _(position=position):
                for copy in copies(tile, position):
                    copy.start()

    @pl.when(i == 0)
    def _():
        for tile in range(WEIGHT_LOOKAHEAD):
            start_loads(tile)

    @pl.when(i + WEIGHT_LOOKAHEAD < MAX_TILES)
    def _():
        start_loads(i + WEIGHT_LOOKAHEAD)

    for position in range(2):
        @pl.when(plans[position][1][i] == 1)
        def _(position=position):
            for copy in copies(i, position):
                copy.wait()

    @pl.when(valid_ref[i] == 1)
    def _():
        x = h_ref[:, :D_MODEL].astype(BF16)
        gates = h_ref[:, D_MODEL:]
        y = jnp.zeros((TM_MOE, D_MODEL), F32)
        for n, (_, _, ring_ref) in enumerate(plans):
            entry = ring_ref[i]
            a = jnp.dot(x, wg_buf[n, entry].astype(BF16), preferred_element_type=F32)
            u = jnp.dot(x, wu_buf[n, entry].astype(BF16), preferred_element_type=F32)
            z = a * jax.nn.sigmoid(a) * u * gates[:, n:n + 1]
            y = y + jnp.dot(z.astype(BF16), wd_buf[n, entry].astype(BF16), preferred_element_type=F32)
        o_ref[:, 0, :] = y

    @pl.when(valid_ref[i] == 0)
    def _():
        o_ref[...] = jnp.zeros((TM_MOE, 1, D_MODEL), F32)


def moe(h_slots, valid, last, tile_a, load_a, ring_a, tile_b, load_b, ring_b, layer, wg, wu, wd):
    assert WEIGHT_RING > WEIGHT_LOOKAHEAD and MAX_TILES >= WEIGHT_LOOKAHEAD
    hbm = pl.BlockSpec(memory_space=pl.ANY)
    return pl.pallas_call(
        functools.partial(_moe_kernel, layer=layer),
        grid_spec=pltpu.PrefetchScalarGridSpec(
            num_scalar_prefetch=8,
            grid=(MAX_TILES,),
            in_specs=[pl.BlockSpec((TM_MOE, ROW_W), lambda i, valid, last, *_: (jnp.minimum(i, last[0]), 0)),
                      hbm, hbm, hbm],
            out_specs=pl.BlockSpec((TM_MOE, 1, D_MODEL), lambda i, *_: (i, 0, 0)),
            scratch_shapes=[pltpu.VMEM((2, WEIGHT_RING, D_MODEL, D_EXPERT), F32),
                            pltpu.VMEM((2, WEIGHT_RING, D_MODEL, D_EXPERT), F32),
                            pltpu.VMEM((2, WEIGHT_RING, D_EXPERT, D_MODEL), F32),
                            pltpu.SemaphoreType.DMA((2, WEIGHT_RING))],
        ),
        out_shape=jax.ShapeDtypeStruct((N_SLOTS, 1, D_MODEL), F32),
        compiler_params=_cparams(("arbitrary",)),
        name="moe",
    )(valid, last, tile_a, load_a, ring_a, tile_b, load_b, ring_b, h_slots, wg, wu, wd)


def _gather_tile(dest_ref, y_ref, buf_ref, sem, tile, slot):
    for r in range(TM):
        pltpu.make_async_copy(y_ref.at[pl.ds(dest_ref[tile * TM + r], 1)],
                              buf_ref.at[slot, pl.ds(r, 1)], sem.at[slot]).start(priority=r % N_DMA_PRIORITIES)


def _moe_residual(dest_ref, y_ref, buf_ref, sem, x1_ref, mod_ref):
    i = pl.program_id(0)
    slot = i % 2

    @pl.when(i == 0)
    def _():
        _gather_tile(dest_ref, y_ref, buf_ref, sem, 0, 0)

    @pl.when(i + 1 < pl.num_programs(0))
    def _():
        _gather_tile(dest_ref, y_ref, buf_ref, sem, i + 1, 1 - slot)

    pltpu.make_async_copy(y_ref.at[pl.ds(0, TM)], buf_ref.at[slot], sem.at[slot]).wait()
    return x1_ref[...] + mod_ref[0][5:6] * buf_ref[slot, :, 0, :]


def _final_kernel(dest_ref, y_ref, x1_ref, mod_ref, g_ref, oc_ref, ol_ref, buf_ref, sem):
    x2 = _moe_residual(dest_ref, y_ref, buf_ref, sem, x1_ref, mod_ref)
    ms = jnp.mean(x2 * x2, axis=-1, keepdims=True)
    y = x2 * lax.rsqrt(ms + EPS) * g_ref[...]

    @pl.when(_is_ctx_tile())
    def _():
        oc_ref[...] = y

    @pl.when(jnp.logical_not(_is_ctx_tile()))
    def _():
        ol_ref[...] = y


_GATHER_SCRATCH = [pltpu.VMEM((2, TM, 1, D_MODEL), F32), pltpu.SemaphoreType.DMA((2,))]


def final_norm(dest, y_slots, x1, mod, final_g):
    return pl.pallas_call(
        _final_kernel,
        grid_spec=pltpu.PrefetchScalarGridSpec(
            num_scalar_prefetch=1,
            grid=(N_TILES,),
            in_specs=[pl.BlockSpec(memory_space=pl.ANY),
                      pl.BlockSpec((TM, D_MODEL), lambda i, d: (i, 0)),
                      pl.BlockSpec((1, 6, D_MODEL), lambda i, d: (_mod_row(i), 0, 0)),
                      pl.BlockSpec((1, D_MODEL), lambda i, d: (0, 0))],
            out_specs=[pl.BlockSpec((TM, D_MODEL), _ctx_tile), pl.BlockSpec((TM, D_MODEL), _lat_tile)],
            scratch_shapes=_GATHER_SCRATCH,
        ),
        out_shape=[jax.ShapeDtypeStruct((TP, D_MODEL), F32), jax.ShapeDtypeStruct((TL, D_MODEL), F32)],
        compiler_params=_cparams(("arbitrary",)),
        name="final_norm",
    )(dest, y_slots, x1, mod, final_g[None, :])


def kernel(x_prompt, x_sample, cache_diff_k, cache_diff_v, cache_na_k, cache_na_v, state_hgrn, c, c_ctx,
           norm1_g, norm2_g, ada_w, ada_b, w_in, w_out, diff_lambda, diff_subln_g, hgrn_lb_logits,
           hgrn_norm_g, na_rpb, router_w, router_b, moe_w_gate, moe_w_up, moe_w_down, final_norm_g):
    assert SEQ == TM and PAST_LEN == TM and DEC_SEQ % TM_OUT == 0 and TP % DEC_SEQ == 0
    x_pair = (x_prompt.reshape(TP, D_MODEL), x_sample.reshape(TL, D_MODEL))
    w_in_bf16 = jnp.concatenate([w_in[:, :, c * MIX_BLK:(c + 1) * MIX_BLK] for c in PM_BLOCKS + PG_BLOCKS],
                                axis=-1).astype(BF16)
    w_out_bf16 = w_out.astype(BF16)
    router_pieces = _router_pieces(router_w)
    mods = modulation(jnp.concatenate([c_ctx[None, :], c], axis=0), ada_w, ada_b)
    mods = mods.reshape(DEPTH, 3, 6, D_MODEL)
    lb_sm = jax.nn.softmax(hgrn_lb_logits.astype(F32), axis=0)
    lb_all = jnp.cumsum(lb_sm, axis=0) - lb_sm[0:1]
    rope = _rope_tables()
    dft_ctx = _dft_constants(SEQ)
    dft_lat = _dft_constants(DEC_SEQ)
    lat_blk0 = TP // DEC_SEQ
    states = []
    moe_state = None
    for l in range(DEPTH):
        if moe_state is None:
            pm, pg, *new_kv = projection(*x_pair, mods[l], norm1_g[l], w_in_bf16, rope)
        else:
            x, pm, pg, *new_kv = projection_after_moe(*moe_state, mods[l - 1], mods[l], l, norm1_g[l], w_in_bf16, rope,
                                                 new_kv)
            x_pair = (x, x)

        lq = diff_lambda[l].astype(F32)
        lam_init = 0.8 - 0.6 * math.exp(-0.3 * l)
        lam = (jnp.exp(jnp.sum(lq[0] * lq[1])) - jnp.exp(jnp.sum(lq[2] * lq[3])) + lam_init).reshape(1)
        subln = jnp.tile(diff_subln_g[l], HEADS)[None, :]
        diff = functools.partial(attention, pm, cols=(M_AQ, M_AK, M_AV), lam=lam, norm_g=subln, n_maps=2,
                                 post_scale=1.0 - lam_init)
        (oa_ctx,), (of_ctx, ob_ctx, st_ctx), (oc_ctx,), (od_ctx,) = run_parts(
            [diff(row_blk0=0, n_seq=BATCH, seq_len=SEQ),
             hgrn(pg, 0, BATCH, SEQ, lb_all[l], None),
             attention(pm, 0, (M_CQ, M_CK, M_CV), BATCH, SEQ, lam, subln, n_maps=1, post_scale=1.0),
             fourier_mix(pm, 0, BATCH, SEQ, dft_ctx)],
            (BATCH, SEQ // TM), "context_mixers")
        (oa_lat,), (of_lat, ob_lat, _), (oc_lat,), (od_lat,) = run_parts(
            [diff(row_blk0=lat_blk0, n_seq=DEC_BATCH, seq_len=DEC_SEQ, cache=(cache_diff_k, cache_diff_v), layer=l),
             hgrn(pg, CTX_TILES, DEC_BATCH, DEC_SEQ, lb_all[l], _state_to_blockdiag(state_hgrn[:, l])),
             na_latent(pm, cache_na_k, cache_na_v, l, _na_bias_tables(na_rpb[l])),
             fourier_mix(pm, lat_blk0, DEC_BATCH, DEC_SEQ, dft_lat)],
            (DEC_BATCH, DEC_SEQ // TM), "latent_mixers")

        x1, h2, bucket, rank, counts = out_and_route(
            x_pair, l > 0, ((oa_ctx, oa_lat), (of_ctx, of_lat), (ob_ctx, ob_lat), (oc_ctx, oc_lat),
                            (od_ctx, od_lat)),
            pg, mods[l], l, hgrn_norm_g[l], w_out_bf16, norm2_g[l], router_pieces, router_b)
        dest, *tile_plan = routing_plan(bucket, rank, counts)
        h_slots = scatter_to_slots(h2, dest, jnp.zeros((N_SLOTS, ROW_W), F32) if l == 0 else h_slots)
        y_slots = moe(h_slots, *tile_plan, l, moe_w_gate, moe_w_up, moe_w_down)
        moe_state = (dest, y_slots, x1)

        states.append(st_ctx)
    y_prompt, y_sample = final_norm(*moe_state, mods[DEPTH - 1], final_norm_g)
    return (y_prompt.reshape(BATCH, SEQ, D_MODEL), y_sample.reshape(DEC_BATCH, DEC_SEQ, D_MODEL),
            *new_kv, jnp.stack(states, axis=1))
```

```python
import functools
import math
from typing import Any, NamedTuple

import numpy as np
import jax
import jax.numpy as jnp
from jax import lax
from jax.experimental import pallas as pl
from jax.experimental.pallas import tpu as pltpu

F32 = jnp.float32
BF16 = jnp.bfloat16
HIGHEST = lax.Precision.HIGHEST

D_MODEL = 1024
BATCH = 16
SEQ = 256
DEPTH = 2
DEC_BATCH = 2
DEC_SEQ = 2048
PAST_LEN = 256
GRID_W = 64
GRID_H = DEC_SEQ // GRID_W
EPS = 1e-6
NEG_BIG = -1e30
HEADS = 4
HEAD_DIM = 64
MIX_BLK = HEADS * HEAD_DIM
A_DIM = 32
ROPE_BASE = 10000.0
B_CHUNK = 32
NA_WIN_H = 8
NA_WIN_W = 16
N_EXPERTS = 16
N_GROUPS = 4
D_EXPERT = 512
PROJ_W = 12 * MIX_BLK
TP = BATCH * SEQ
TL = DEC_BATCH * DEC_SEQ
T = TP + TL
TM = 256
N_TILES = T // TM
CTX_TILES = TP // TM
LAT_TILES_PER_SEQ = DEC_SEQ // TM
(C_AQ, C_AK, C_AV, C_BQ, C_BFF, C_BFB, C_BV, C_BG, C_CQ, C_CK, C_CV, C_DU) = range(12)
PM_BLOCKS = (C_AQ, C_AK, C_AV, C_CQ, C_CK, C_CV, C_DU)
PG_BLOCKS = (C_BQ, C_BFF, C_BFB, C_BV, C_BG)
(M_AQ, M_AK, M_AV, M_CQ, M_CK, M_CV, M_DU) = range(len(PM_BLOCKS))
(G_BQ, G_BFF, G_BFB, G_BV, G_BG) = range(len(PG_BLOCKS))
PM_W = len(PM_BLOCKS) * MIX_BLK
PG_W = len(PG_BLOCKS) * MIX_BLK
NA_SLAB_ROWS = 12
NA_SLAB = NA_SLAB_ROWS * GRID_W
LANES = 128
ROW_W = D_MODEL + LANES
EXPERT_PAIRS = ((0, 1), (0, 2), (0, 3), (1, 3), (2, 3), (2, 1))
N_BUCKETS = N_GROUPS * len(EXPERT_PAIRS)
BUCKET_ROWS = 32
N_DMA_PRIORITIES = 2
TM_OUT = 512
TM_SCATTER = 1024
WEIGHT_LOOKAHEAD = 2
WEIGHT_RING = WEIGHT_LOOKAHEAD + 1
TM_MOE = 384
MAX_TILES = T // TM_MOE + N_BUCKETS
N_SLOTS = MAX_TILES * TM_MOE
VMEM_LIMIT = 56 * 1024 * 1024


def _cparams(sem):
    return pltpu.CompilerParams(dimension_semantics=sem, vmem_limit_bytes=VMEM_LIMIT)


class Part(NamedTuple):
    kernel: Any
    in_specs: list
    args: list
    out_specs: list
    out_shapes: list
    scratch: list


def _run_parts_kernel(*refs, layout):
    n_in = sum(n for _, n, _, _ in layout)
    n_out = sum(n for _, _, n, _ in layout)
    ins, outs, scratch = refs[:n_in], refs[n_in:n_in + n_out], refs[n_in + n_out:]
    i = o = s = 0
    for kernel, k_in, k_out, k_scratch in layout:
        kernel(*ins[i:i + k_in], *outs[o:o + k_out], *scratch[s:s + k_scratch])
        i, o, s = i + k_in, o + k_out, s + k_scratch


def run_parts(parts, grid, name):
    layout = tuple((p.kernel, len(p.in_specs), len(p.out_specs), len(p.scratch)) for p in parts)
    outs = pl.pallas_call(
        functools.partial(_run_parts_kernel, layout=layout),
        grid=grid,
        in_specs=[s for p in parts for s in p.in_specs],
        out_specs=[s for p in parts for s in p.out_specs],
        out_shape=[s for p in parts for s in p.out_shapes],
        scratch_shapes=[s for p in parts for s in p.scratch],
        compiler_params=_cparams(("arbitrary", "arbitrary")),
        name=name,
    )(*[a for p in parts for a in p.args])
    result, o = [], 0
    for p in parts:
        result.append(outs[o:o + len(p.out_specs)])
        o += len(p.out_specs)
    return result


def _head_lanes(width=MIX_BLK):
    return lax.broadcasted_iota(jnp.int32, (1, width), 1)


def _lane_range(lane, lo, n):
    return (lane >= lo) & (lane < lo + n)


def _same_head_matrix():
    r = lax.broadcasted_iota(jnp.int32, (MIX_BLK, MIX_BLK), 0) // HEAD_DIM
    c = lax.broadcasted_iota(jnp.int32, (MIX_BLK, MIX_BLK), 1) // HEAD_DIM
    return r == c


def _bf16_pieces(x, n):
    pieces = []
    for _ in range(n):
        piece = x.astype(BF16)
        pieces.append(piece)
        x = x - piece.astype(F32)
    return pieces


def _select_sum_left(onehot_bf16, x):
    cols = x.shape[1]
    stacked = jnp.dot(onehot_bf16, jnp.concatenate(_bf16_pieces(x, 3), axis=1), preferred_element_type=F32)
    return stacked[:, :cols] + stacked[:, cols:2 * cols] + stacked[:, 2 * cols:]


def _select_sum_right(x, onehot_bf16):
    rows = x.shape[0]
    stacked = jnp.dot(jnp.concatenate(_bf16_pieces(x, 3), axis=0), onehot_bf16, preferred_element_type=F32)
    return stacked[:rows] + stacked[rows:2 * rows] + stacked[2 * rows:]


def _head_mean_square(o):
    ones = jnp.where(_same_head_matrix(), 1.0, 0.0).astype(BF16)
    return _select_sum_right(o * o, ones) * (1.0 / HEAD_DIM)


def _mod_row(i):
    return jnp.where(i < CTX_TILES, 0, 1 + (i - CTX_TILES) // LAT_TILES_PER_SEQ)


def _mod_kernel(c_ref, w_ref, b_ref, o_ref):
    w = w_ref[0]
    for r in range(3):
        c = c_ref[r]
        s = c * jax.nn.sigmoid(c)
        o_ref[0, r:r + 1, :] = jnp.sum(s * w, axis=0, keepdims=True) + b_ref[0]


def modulation(c_rows, ada_w, ada_b):
    nt = 768
    n_out = 6 * D_MODEL
    return pl.pallas_call(
        _mod_kernel,
        grid=(DEPTH, n_out // nt),
        in_specs=[
            pl.BlockSpec((3, D_MODEL, 1), lambda l, j: (0, 0, 0)),
            pl.BlockSpec((1, D_MODEL, nt), lambda l, j: (l, 0, j)),
            pl.BlockSpec((1, 1, nt), lambda l, j: (l, 0, j)),
        ],
        out_specs=pl.BlockSpec((1, 3, nt), lambda l, j: (l, 0, j)),
        out_shape=jax.ShapeDtypeStruct((DEPTH, 3, n_out), F32),
        compiler_params=_cparams(("arbitrary", "arbitrary")),
        name="modulation",
    )(c_rows[:, :, None], ada_w, ada_b[:, None, :])


def _is_ctx_tile():
    return pl.program_id(0) < CTX_TILES


def _ctx_tile(i, *_):
    return (jnp.minimum(i, CTX_TILES - 1), 0)


def _lat_tile(i, *_):
    return (jnp.maximum(i - CTX_TILES, 0), 0)


def _proj_kernel(xc_ref, xl_ref, mod_ref, g_ref, w_ref, cos_ref, sa_ref, sb_ref, pm_ref, pg_ref, *cache_refs):
    x = jnp.where(_is_ctx_tile(), xc_ref[...], xl_ref[...])
    _proj_body(x, mod_ref, g_ref, w_ref, cos_ref, sa_ref, sb_ref, pm_ref, pg_ref, cache_refs)


def _proj_after_moe_kernel(dest_ref, y_ref, x1_ref, modp_ref, mod_ref, g_ref, w_ref, cos_ref, sa_ref, sb_ref,
                           *rest):
    x2_ref, pm_ref, pg_ref = rest[4:7]
    cache_refs, (buf_ref, sem) = rest[7:11], rest[11:]
    x2 = _moe_residual(dest_ref, y_ref, buf_ref, sem, x1_ref, modp_ref)
    x2_ref[...] = x2
    _proj_body(x2, mod_ref, g_ref, w_ref, cos_ref, sa_ref, sb_ref, pm_ref, pg_ref, cache_refs)


def _proj_body(x, mod_ref, g_ref, w_ref, cos_ref, sa_ref, sb_ref, pm_ref, pg_ref, cache_refs):
    ms = jnp.mean(x * x, axis=-1, keepdims=True)
    mod = mod_ref[0]
    h = x * lax.rsqrt(ms + EPS) * g_ref[...] * (1.0 + mod[1:2]) + mod[0:1]
    p = jnp.dot(h.astype(BF16), w_ref[0], preferred_element_type=F32)
    t = p[:, :2 * MIX_BLK]
    pm_ref[:, :2 * MIX_BLK] = (t * cos_ref[...] + pltpu.roll(t, 1, 1) * sa_ref[...]
                               + pltpu.roll(t, 2 * MIX_BLK - 1, 1) * sb_ref[...]).astype(BF16)
    pm_ref[:, 2 * MIX_BLK:] = p[:, 2 * MIX_BLK:PM_W].astype(BF16)
    pg_ref[...] = p[:, PM_W:]

    @pl.when(_is_ctx_tile())
    def _():
        for ref, col in zip(cache_refs, (M_AK, M_AV, M_CK, M_CV)):
            for hd in range(HEADS):
                lo = col * MIX_BLK + hd * HEAD_DIM
                ref[0, 0, hd] = p[:, lo:lo + HEAD_DIM]
            if ref.shape[1] > 1:
                ref[0, 1:] = jnp.zeros((ref.shape[1] - 1,) + tuple(ref.shape[2:]), F32)


_CACHE_SHAPE = jax.ShapeDtypeStruct((BATCH, DEPTH, HEADS, SEQ, HEAD_DIM), F32)


def _cache_spec(layer):
    n_layers = DEPTH if layer == 0 else 1
    return pl.BlockSpec((1, n_layers, HEADS, SEQ, HEAD_DIM),
                        lambda i, *_: (jnp.minimum(i, CTX_TILES - 1), layer, 0, 0, 0))


def _rope_tables():
    nf = A_DIM // 4
    freqs = ROPE_BASE ** (-np.arange(nf, dtype=np.float64) / nf)
    pos = np.arange(DEC_SEQ)
    row = (pos // GRID_W).astype(np.float64)
    col = (pos % GRID_W).astype(np.float64)
    ang = np.concatenate([row[:, None] * freqs, col[:, None] * freqs], axis=-1)
    cos = np.repeat(np.cos(ang), 2, axis=-1)
    sin = np.repeat(np.sin(ang), 2, axis=-1)
    odd = (np.arange(A_DIM) % 2 == 1)[None, :]
    sin_from_left = np.where(odd, sin, 0.0)
    sin_from_right = np.where(odd, 0.0, -sin)
    reps = 2 * MIX_BLK // A_DIM
    ident = (np.ones((TM, 2 * MIX_BLK)), np.zeros((TM, 2 * MIX_BLK)))
    return tuple(jnp.asarray(np.concatenate([np.tile(t, (1, reps)), tail], axis=0), F32)
                 for t, tail in ((cos, ident[0]), (sin_from_left, ident[1]), (sin_from_right, ident[1])))


def _rope_block(i):
    return (jnp.where(i < CTX_TILES, LAT_TILES_PER_SEQ, (i - CTX_TILES) % LAT_TILES_PER_SEQ), 0)


def projection_after_moe(dest, y_slots, x1, mod_prev, mod, layer, norm_g, w_in_bf16, rope, caches):
    rope_spec = pl.BlockSpec((TM, 2 * MIX_BLK), lambda i, d: _rope_block(i))
    mod_spec = pl.BlockSpec((1, 6, D_MODEL), lambda i, d: (_mod_row(i), 0, 0))
    n_in = 10
    return pl.pallas_call(
        _proj_after_moe_kernel,
        grid_spec=pltpu.PrefetchScalarGridSpec(
            num_scalar_prefetch=1,
            grid=(N_TILES,),
            in_specs=[pl.BlockSpec(memory_space=pl.ANY),
                      pl.BlockSpec((TM, D_MODEL), lambda i, d: (i, 0)),
                      mod_spec, mod_spec,
                      pl.BlockSpec((1, D_MODEL), lambda i, d: (0, 0)),
                      pl.BlockSpec((1, D_MODEL, PROJ_W), lambda i, d: (layer, 0, 0)),
                      rope_spec, rope_spec, rope_spec] + [pl.BlockSpec(memory_space=pl.ANY)] * 4,
            out_specs=[pl.BlockSpec((TM, D_MODEL), lambda i, d: (i, 0)),
                       pl.BlockSpec((TM, PM_W), lambda i, d: (i, 0)),
                       pl.BlockSpec((TM, PG_W), lambda i, d: (i, 0))] + [_cache_spec(layer)] * 4,
            scratch_shapes=_GATHER_SCRATCH,
        ),
        out_shape=[jax.ShapeDtypeStruct((T, D_MODEL), F32), jax.ShapeDtypeStruct((T, PM_W), BF16),
                   jax.ShapeDtypeStruct((T, PG_W), F32)] + [_CACHE_SHAPE] * 4,
        input_output_aliases={n_in + n: 3 + n for n in range(4)},
        compiler_params=_cparams(("arbitrary",)),
        name="projection_after_moe",
    )(dest, y_slots, x1, mod_prev, mod, norm_g[None, :], w_in_bf16, *rope, *caches)


def projection(x_ctx, x_lat, mod, norm_g, w_in_bf16, rope):
    layer = 0
    rope_spec = pl.BlockSpec((TM, 2 * MIX_BLK), _rope_block)
    return pl.pallas_call(
        _proj_kernel,
        grid=(N_TILES,),
        in_specs=[
            pl.BlockSpec((TM, D_MODEL), _ctx_tile),
            pl.BlockSpec((TM, D_MODEL), _lat_tile),
            pl.BlockSpec((1, 6, D_MODEL), lambda i: (_mod_row(i), 0, 0)),
            pl.BlockSpec((1, D_MODEL), lambda i: (0, 0)),
            pl.BlockSpec((1, D_MODEL, PROJ_W), lambda i: (layer, 0, 0)),
            rope_spec, rope_spec, rope_spec,
        ],
        out_specs=[pl.BlockSpec((TM, PM_W), lambda i: (i, 0)), pl.BlockSpec((TM, PG_W), lambda i: (i, 0))]
        + [_cache_spec(layer)] * 4,
        out_shape=[jax.ShapeDtypeStruct((T, PM_W), BF16), jax.ShapeDtypeStruct((T, PG_W), F32)]
        + [_CACHE_SHAPE] * 4,
        compiler_params=_cparams(("arbitrary",)),
        name="projection",
    )(x_ctx, x_lat, mod, norm_g[None, :], w_in_bf16, *rope)


LOG2_E = 1.4426950408889634


def _exp2_rows(s):
    e = jnp.exp2(s - jnp.max(s, axis=-1, keepdims=True))
    return e, 1.0 / jnp.sum(e, axis=-1, keepdims=True)


def _attn_kernel(lam_ref, q_ref, k_ref, v_ref, *rest, n_maps, post_scale, with_cache):
    if with_cache:
        kc_ref, vc_ref, g_ref, o_ref, kt_ref, vb_ref = rest
    else:
        g_ref, o_ref, kt_ref, vb_ref = rest

    @pl.when(pl.program_id(1) == 0)
    def _():
        k = k_ref[...].astype(F32)
        v = v_ref[...]
        if with_cache:
            k = jnp.concatenate([_cache_heads_on_lanes(kc_ref), k], axis=0)
            v = jnp.concatenate([_cache_heads_on_lanes(vc_ref).astype(BF16), v], axis=0)
        kt_ref[...] = k.T.astype(BF16)
        vb_ref[...] = v

    lane = _head_lanes()
    map_dim = HEAD_DIM // n_maps
    q = q_ref[...].astype(F32) * (map_dim ** -0.5 * LOG2_E)
    kt = kt_ref[...]
    vb = vb_ref[...]
    stack_rows = kt_ref.shape[1] <= TM
    weights = []
    for h in range(HEADS):
        masked = [jnp.where(_lane_range(lane, h * HEAD_DIM + j * map_dim, map_dim), q, 0.0).astype(BF16)
                  for j in range(n_maps)]
        if stack_rows:
            s = jnp.dot(jnp.concatenate(masked, axis=0), kt, preferred_element_type=F32)
            scores = [s[j * TM:(j + 1) * TM] for j in range(n_maps)]
        else:
            scores = [jnp.dot(m, kt, preferred_element_type=F32) for m in masked]
        parts = [_exp2_rows(x) for x in scores]
        w = parts[0][0] * parts[0][1]
        if n_maps == 2:
            w = w - parts[1][0] * (lam_ref[0] * parts[1][1])
        weights.append(w.astype(BF16))
    if stack_rows:
        oh = jnp.dot(jnp.concatenate(weights, axis=0), vb, preferred_element_type=F32)
        outs = [oh[h * TM:(h + 1) * TM] for h in range(HEADS)]
    else:
        outs = [jnp.dot(w, vb, preferred_element_type=F32) for w in weights]
    o = jnp.zeros(q.shape, F32)
    for h in range(HEADS):
        o = jnp.where(_lane_range(lane, h * HEAD_DIM, HEAD_DIM), outs[h], o)
    if n_maps == 2:
        o = o * lax.rsqrt(_head_mean_square(o) + EPS) * g_ref[...] * post_scale
    o_ref[...] = o


def _cache_block_spec(layer):
    return pl.BlockSpec((1, 1, HEADS, PAST_LEN, HEAD_DIM), lambda b, i: (b, layer, 0, 0, 0))


def _cache_heads_on_lanes(ref):
    return jnp.concatenate([ref[0, 0, h] for h in range(HEADS)], axis=1)


def attention(p, row_blk0, cols, n_seq, seq_len, lam, norm_g, *, n_maps, post_scale, cache=None, layer=0):
    nb = seq_len // TM
    kv_len = seq_len + (PAST_LEN if cache is not None else 0)
    kern = functools.partial(_attn_kernel, n_maps=n_maps, post_scale=post_scale, with_cache=cache is not None)
    kv_spec = lambda col: pl.BlockSpec((seq_len, MIX_BLK), lambda b, i: (row_blk0 + b, col))
    cache_specs = [_cache_block_spec(layer)] * 2 if cache is not None else []
    return Part(
        kernel=kern,
        in_specs=[
            pl.BlockSpec(memory_space=pltpu.SMEM),
            pl.BlockSpec((TM, MIX_BLK), lambda b, i: ((row_blk0 + b) * nb + i, cols[0])),
            kv_spec(cols[1]), kv_spec(cols[2]), *cache_specs,
            pl.BlockSpec((1, MIX_BLK), lambda b, i: (0, 0)),
        ],
        out_specs=[pl.BlockSpec((TM, MIX_BLK), lambda b, i: (b * nb + i, 0))],
        out_shapes=[jax.ShapeDtypeStruct((n_seq * seq_len, MIX_BLK), F32)],
        scratch=[pltpu.VMEM((MIX_BLK, kv_len), BF16), pltpu.VMEM((kv_len, MIX_BLK), BF16)],
        args=[lam, p, p, p, *(cache or ()), norm_g])


def _na_slab_start(i):
    return jnp.clip(i - 1, 0, GRID_H // 4 - NA_SLAB_ROWS // 4)


def _na_kernel(q_ref, k_ref, v_ref, kc_ref, vc_ref, bias_ref, o_ref):
    i = pl.program_id(1)
    start = pl.multiple_of(_na_slab_start(i) * TM, TM)
    ks_t = k_ref[pl.ds(start, NA_SLAB), :].astype(F32).T.astype(BF16)
    vs = v_ref[pl.ds(start, NA_SLAB), :]
    kc_t = _cache_heads_on_lanes(kc_ref).T.astype(BF16)
    vc = _cache_heads_on_lanes(vc_ref).astype(BF16)
    q = q_ref[...].astype(F32) * (HEAD_DIM ** -0.5)
    lane = _head_lanes()
    heads = [_lane_range(lane, h * HEAD_DIM, HEAD_DIM) for h in range(HEADS)]
    qm = jnp.concatenate([jnp.where(in_head, q, 0.0) for in_head in heads], axis=0).astype(BF16)
    s_loc = jnp.dot(qm, ks_t, preferred_element_type=F32)
    s_ctx = jnp.dot(qm, kc_t, preferred_element_type=F32)
    e_locs, e_ctxs, dens = [], [], []
    for h in range(HEADS):
        rows = slice(h * TM, (h + 1) * TM)
        sl = s_loc[rows] + bias_ref[0, h]
        sc = s_ctx[rows]
        m = jnp.maximum(jnp.max(sl, axis=-1, keepdims=True), jnp.max(sc, axis=-1, keepdims=True))
        e_loc = jnp.exp(sl - m)
        e_ctx = jnp.exp(sc - m)
        dens.append(jnp.sum(e_loc, axis=-1, keepdims=True) + jnp.sum(e_ctx, axis=-1, keepdims=True))
        e_locs.append(e_loc.astype(BF16))
        e_ctxs.append(e_ctx.astype(BF16))
    oh = (jnp.dot(jnp.concatenate(e_locs, axis=0), vs, preferred_element_type=F32)
          + jnp.dot(jnp.concatenate(e_ctxs, axis=0), vc, preferred_element_type=F32))
    o = jnp.zeros(q.shape, F32)
    for h in range(HEADS):
        o = jnp.where(heads[h], oh[h * TM:(h + 1) * TM] / dens[h], o)
    o_ref[...] = o


def _na_bias_tables(rpb):
    n_dr, n_dc = 2 * NA_WIN_H - 1, 2 * NA_WIN_W - 1
    cq = np.arange(GRID_W)[:, None]
    ck = np.arange(GRID_W)[None, :]
    wc0 = np.clip(cq - NA_WIN_W // 2, 0, GRID_W - NA_WIN_W)
    col_ok = (ck >= wc0) & (ck < wc0 + NA_WIN_W)
    col_pick = np.clip(ck - cq + NA_WIN_W - 1, 0, n_dc - 1)[..., None] == np.arange(n_dc)
    by_col = jnp.einsum("hab,qcb->haqc", rpb.astype(F32), jnp.asarray(col_pick, F32), precision=HIGHEST)
    margin = 4
    by_col = jnp.pad(by_col.transpose(0, 2, 1, 3), ((0, 0), (0, 0), (margin, margin), (0, 0)))
    by_col = by_col.reshape(HEADS, GRID_W, (n_dr + 2 * margin) * GRID_W)
    pieces, row_ok = [], []
    for tile in (0, 1, GRID_H // 4 - 1):
        slab0 = int(np.clip(tile - 1, 0, GRID_H // 4 - NA_SLAB_ROWS // 4)) * 4
        rq = tile * 4 + np.arange(4)
        rk = (slab0 + np.arange(NA_SLAB) // GRID_W)[None, :]
        wr0 = np.clip(rq - NA_WIN_H // 2, 0, GRID_H - NA_WIN_H)[:, None]
        row_ok.append((rk >= wr0) & (rk < wr0 + NA_WIN_H))
        for r in rq:
            first = slab0 - int(r) + NA_WIN_H - 1 + margin
            assert 0 <= first and first + NA_SLAB_ROWS <= n_dr + 2 * margin
            pieces.append(by_col[:, :, first * GRID_W:first * GRID_W + NA_SLAB])
    table = jnp.stack(pieces).reshape(3, 4, HEADS, GRID_W, NA_SLAB).transpose(0, 2, 1, 3, 4)
    valid = np.stack(row_ok)[:, None, :, None, :] & np.tile(col_ok, (1, NA_SLAB_ROWS))[None, None, None]
    table = jnp.where(jnp.asarray(valid), table, NEG_BIG)
    return table.reshape(3, HEADS, TM, NA_SLAB)


def na_latent(p, kc, vc, layer, bias):
    n_t = LAT_TILES_PER_SEQ
    seq_blk0 = TP // DEC_SEQ

    def bias_idx(b, i):
        return (jnp.minimum(i, 1) + i // (n_t - 1), 0, 0, 0)

    return Part(
        kernel=_na_kernel,
        in_specs=[
            pl.BlockSpec((TM, MIX_BLK), lambda b, i: (CTX_TILES + b * n_t + i, M_CQ)),
            pl.BlockSpec((DEC_SEQ, MIX_BLK), lambda b, i: (seq_blk0 + b, M_CK)),
            pl.BlockSpec((DEC_SEQ, MIX_BLK), lambda b, i: (seq_blk0 + b, M_CV)),
            _cache_block_spec(layer), _cache_block_spec(layer),
            pl.BlockSpec((1, HEADS, TM, NA_SLAB), bias_idx),
        ],
        out_specs=[pl.BlockSpec((TM, MIX_BLK), lambda b, i: (b * n_t + i, 0))],
        out_shapes=[jax.ShapeDtypeStruct((TL, MIX_BLK), F32)],
        scratch=[],
        args=[p, p, p, kc, vc, bias])


MAX_EXPONENT = 80.0


def _hgrn_direction(q_ref, f_ref, v_ref, lb, st_ref, o_ref, reverse):
    n_ch = TM // B_CHUNK
    r_idx = lax.broadcasted_iota(jnp.int32, (TM, TM), 0)
    c_idx = lax.broadcasted_iota(jnp.int32, (TM, TM), 1)
    tri = (c_idx >= r_idx) if reverse else (c_idx <= r_idx)
    zq = q_ref[...]
    q = zq * jax.nn.sigmoid(zq)
    z = f_ref[...]
    gate = (1.0 - lb) * jax.nn.sigmoid(z)
    logf = jnp.log(lb + gate)
    kk = (1.0 - lb) - gate
    b = _select_sum_left(jnp.where(tri, 1.0, 0.0).astype(BF16), logf)
    b3 = b.reshape(n_ch, B_CHUNK, MIX_BLK)
    mid = B_CHUNK // 2 if reverse else B_CHUNK // 2 - 1
    q_in = (q.reshape(b3.shape) * jnp.exp(b3 - b3[:, mid:mid + 1, :])).reshape(TM, MIX_BLK)
    q_dec = (q * jnp.exp(b)).astype(BF16)
    b_t = b.T
    kk_t = kk.T
    far = 0 if reverse else TM - 1
    b_far = b_t[:, far:far + 1]
    k_dec_t = (kk_t * jnp.exp(b_far - b_t)).astype(BF16)
    vb = v_ref[...].astype(BF16)
    st = st_ref[...]
    o_state = jnp.dot(q_dec, st.astype(BF16), preferred_element_type=F32)
    kv = jnp.dot(k_dec_t, vb, preferred_element_type=F32)
    st_ref[...] = st * jnp.exp(b_far) + jnp.where(_same_head_matrix(), kv, 0.0)
    lane = _head_lanes()
    token = lax.broadcasted_iota(jnp.int32, (1, TM), 1)
    local = lax.broadcasted_iota(jnp.int32, (HEADS * B_CHUNK, 1), 0) % B_CHUNK
    heads = [_lane_range(lane, h * HEAD_DIM, HEAD_DIM) for h in range(HEADS)]
    weights = []
    for c in range(n_ch):
        ref = b_t[:, c * B_CHUNK + mid:c * B_CHUNK + mid + 1]
        k_c_t = (kk_t * jnp.exp(jnp.minimum(ref - b_t, MAX_EXPONENT))).astype(BF16)
        q_c = q_in[c * B_CHUNK:(c + 1) * B_CHUNK, :]
        lhs = jnp.concatenate([jnp.where(in_head, q_c, 0.0) for in_head in heads], axis=0)
        a = jnp.dot(lhs.astype(BF16), k_c_t, preferred_element_type=F32)
        t_abs = c * B_CHUNK + local
        weights.append(jnp.where((token >= t_abs) if reverse else (token <= t_abs), a, 0.0).astype(BF16))
    res = jnp.dot(jnp.concatenate(weights, axis=0), vb, preferred_element_type=F32)
    for c in range(n_ch):
        rows = slice(c * B_CHUNK, (c + 1) * B_CHUNK)
        o_c = o_state[rows, :]
        for h, in_head in enumerate(heads):
            lo = (c * HEADS + h) * B_CHUNK
            o_c = o_c + jnp.where(in_head, res[lo:lo + B_CHUNK, :], 0.0)
        o_ref[rows, :] = o_c


def _hgrn_kernel(qf_ref, ff_ref, vf_ref, qb_ref, fb_ref, vb_ref, lb_ref, s0_ref,
                 of_ref, ob_ref, s_ref, stf_ref, stb_ref, *, has_s0):
    j = pl.program_id(1)

    @pl.when(j == 0)
    def _():
        if has_s0:
            stf_ref[...] = s0_ref[0, 0]
            stb_ref[...] = s0_ref[0, 1]
        else:
            stf_ref[...] = jnp.zeros((MIX_BLK, MIX_BLK), F32)
            stb_ref[...] = jnp.zeros((MIX_BLK, MIX_BLK), F32)

    lb = lb_ref[...]
    _hgrn_direction(qf_ref, ff_ref, vf_ref, lb[0:1], stf_ref, of_ref, False)
    _hgrn_direction(qb_ref, fb_ref, vb_ref, lb[1:2], stb_ref, ob_ref, True)

    @pl.when(j == pl.num_programs(1) - 1)
    def _():
        for d, st_ref in enumerate((stf_ref, stb_ref)):
            s = st_ref[...]
            for hd in range(HEADS):
                lo = hd * HEAD_DIM
                s_ref[0, d, hd] = s[lo:lo + HEAD_DIM, lo:lo + HEAD_DIM]


def hgrn(p, row_tile0, n_seq, seq_len, lb, s0):
    nb = seq_len // TM
    has_s0 = s0 is not None
    if s0 is None:
        s0 = jnp.zeros((1, 2, MIX_BLK, MIX_BLK), F32)

    def fwd(col):
        return pl.BlockSpec((TM, MIX_BLK), lambda s, j: (row_tile0 + s * nb + j, col))

    def bwd(col):
        return pl.BlockSpec((TM, MIX_BLK), lambda s, j: (row_tile0 + s * nb + nb - 1 - j, col))

    state_spec = pl.BlockSpec((1, 2, MIX_BLK, MIX_BLK), lambda s, j: (s if has_s0 else 0, 0, 0, 0))
    out_rows = n_seq * seq_len
    return Part(
        kernel=functools.partial(_hgrn_kernel, has_s0=has_s0),
        in_specs=[fwd(G_BQ), fwd(G_BFF), fwd(G_BV), bwd(G_BQ), bwd(G_BFB), bwd(G_BV),
                  pl.BlockSpec((2, MIX_BLK), lambda s, j: (0, 0)), state_spec],
        out_specs=[
            pl.BlockSpec((TM, MIX_BLK), lambda s, j: (s * nb + j, 0)),
            pl.BlockSpec((TM, MIX_BLK), lambda s, j: (s * nb + nb - 1 - j, 0)),
            pl.BlockSpec((1, 2, HEADS, HEAD_DIM, HEAD_DIM), lambda s, j: (s, 0, 0, 0, 0)),
        ],
        out_shapes=[jax.ShapeDtypeStruct((out_rows, MIX_BLK), F32),
                    jax.ShapeDtypeStruct((out_rows, MIX_BLK), F32),
                    jax.ShapeDtypeStruct((n_seq, 2, HEADS, HEAD_DIM, HEAD_DIM), F32)],
        scratch=[pltpu.VMEM((MIX_BLK, MIX_BLK), F32), pltpu.VMEM((MIX_BLK, MIX_BLK), F32)],
        args=[p, p, p, p, p, p, lb, s0])


def _state_to_blockdiag(s):
    eye = jnp.eye(HEADS, dtype=F32)
    full = s.astype(F32)[:, :, :, :, None, :] * eye[None, None, :, None, :, None]
    return full.reshape(s.shape[0], 2, MIX_BLK, MIX_BLK)


def _fft_kernel(u_ref, cs64_ref, csl_ref, o_ref, ab_ref, *, norm):
    seq_len = u_ref.shape[0]

    @pl.when(pl.program_id(1) == 0)
    def _():
        ab = jnp.dot(u_ref[...], cs64_ref[...], preferred_element_type=F32).astype(BF16)
        ab_ref[:seq_len, :] = ab[:, :MIX_BLK]
        ab_ref[seq_len:, :] = ab[:, MIX_BLK:]

    o_ref[...] = jnp.dot(csl_ref[...], ab_ref[...], preferred_element_type=F32) * norm


def _dft_tables(n):
    k = np.arange(n)
    ang = 2.0 * np.pi * ((k[:, None] * k[None, :]) % n) / n
    return np.cos(ang), np.sin(ang)


def _dft_constants(seq_len):
    c64, s64 = _dft_tables(HEAD_DIM)
    eye = np.eye(HEADS)
    cl, sl = _dft_tables(seq_len)
    as_bf16 = lambda a: jnp.asarray(a, F32).astype(BF16)
    return (as_bf16(np.concatenate([np.kron(eye, c64), -np.kron(eye, s64)], axis=1)),
            as_bf16(np.concatenate([cl, sl], axis=1)))


def fourier_mix(p, row_blk0, n_seq, seq_len, consts):
    cs64, csl = consts
    nb = seq_len // TM
    norm = 1.0 / math.sqrt(seq_len * HEAD_DIM)
    return Part(
        kernel=functools.partial(_fft_kernel, norm=norm),
        in_specs=[
            pl.BlockSpec((seq_len, MIX_BLK), lambda s, i: (row_blk0 + s, M_DU)),
            pl.BlockSpec((MIX_BLK, 2 * MIX_BLK), lambda s, i: (0, 0)),
            pl.BlockSpec((TM, 2 * seq_len), lambda s, i: (i, 0)),
        ],
        out_specs=[pl.BlockSpec((TM, MIX_BLK), lambda s, i: (s * nb + i, 0))],
        out_shapes=[jax.ShapeDtypeStruct((n_seq * seq_len, MIX_BLK), F32)],
        scratch=[pltpu.VMEM((2 * seq_len, MIX_BLK), BF16)],
        args=[p, cs64, csl])


def _route(logits_t, rb):
    per = N_EXPERTS // N_GROUPS
    score = [jax.nn.sigmoid(logits_t[e:e + 1, :]) for e in range(N_EXPERTS)]
    sel = [score[e] + rb[e:e + 1, :] for e in range(N_EXPERTS)]
    gscore = []
    for g in range(N_GROUPS):
        vals = sel[g * per:(g + 1) * per]
        best = None
        for a in range(per):
            for b in range(a + 1, per):
                pair = vals[a] + vals[b]
                best = pair if best is None else jnp.maximum(best, pair)
        gscore.append(best)
    chosen = []
    for g in range(N_GROUPS):
        ok = None
        for j in range(N_GROUPS):
            if j == g:
                continue
            cond = gscore[g] > gscore[j] if j < g else gscore[g] >= gscore[j]
            ok = cond if ok is None else ok & cond
        chosen.append(ok)
    picked = []
    for e in range(N_EXPERTS):
        g = e // per
        rank = jnp.zeros_like(sel[e])
        for j in range(g * per, (g + 1) * per):
            if j == e:
                continue
            ahead = sel[j] >= sel[e] if j < e else sel[j] > sel[e]
            rank = rank + jnp.where(ahead, 1.0, 0.0)
        picked.append(chosen[g] & (rank < 2.0))
    wsum = jnp.zeros_like(score[0])
    for e in range(N_EXPERTS):
        wsum = wsum + jnp.where(picked[e], score[e], 0.0)
    bucket = jnp.zeros_like(wsum)
    w_a = jnp.zeros_like(wsum)
    w_b = jnp.zeros_like(wsum)
    for g in range(N_GROUPS):
        for n, (a, b) in enumerate(EXPERT_PAIRS):
            hit = picked[g * per + a] & picked[g * per + b]
            bucket = jnp.where(hit, float(g * len(EXPERT_PAIRS) + n), bucket)
            w_a = jnp.where(hit, score[g * per + a] / wsum, w_a)
            w_b = jnp.where(hit, score[g * per + b] / wsum, w_b)
    return bucket, w_a, w_b


def _out_kernel(*refs):
    streams, rest = refs[:12], refs[12:]
    (bg_ref, mod_ref, hg_ref, w_ref, g2_ref, rw_ref, rb_ref,
     x1_ref, h2_ref, bucket_ref, rank_ref, counts_ref, run_ref) = rest
    is_ctx = pl.program_id(0) < TP // TM_OUT
    x, o_a, o_f, o_b, o_c, o_d = (jnp.where(is_ctx, streams[2 * n][...], streams[2 * n + 1][...])
                                  for n in range(6))

    @pl.when(pl.program_id(0) == 0)
    def _():
        run_ref[...] = jnp.zeros(run_ref.shape, F32)

    mod = mod_ref[0]
    hb = o_f + o_b
    zg = bg_ref[...]
    hb = hb * lax.rsqrt(_head_mean_square(hb) + EPS) * hg_ref[...] * (zg * jax.nn.sigmoid(zg))
    mixers = jnp.concatenate([part.astype(BF16) for part in (o_a, hb, o_c, o_d)], axis=1)
    mixed = jnp.dot(mixers, w_ref[0], preferred_element_type=F32)
    x1 = x + mod[2:3] * mixed
    x1_ref[...] = x1
    ms = jnp.mean(x1 * x1, axis=-1, keepdims=True)
    h2 = x1 * lax.rsqrt(ms + EPS) * g2_ref[...] * (1.0 + mod[4:5]) + mod[3:4]
    rw = rw_ref[...]
    r = jnp.dot(jnp.concatenate(_bf16_pieces(h2, 2), axis=0), rw, preferred_element_type=F32)
    r = r[:TM_OUT] + r[TM_OUT:]
    bucket, w_a, w_b = _route((r[:, :LANES] + r[:, LANES:]).T, rb_ref[...])
    h2_ref[:, :D_MODEL] = h2
    h2_ref[:, D_MODEL:] = jnp.concatenate([w_a, w_b, jnp.zeros((LANES - 2, TM_OUT), F32)], axis=0).T
    onehot = jnp.where(lax.broadcasted_iota(jnp.int32, (BUCKET_ROWS, 1), 0).astype(F32) == bucket, 1.0, 0.0)
    s_idx = lax.broadcasted_iota(jnp.int32, (TM_OUT, TM_OUT), 0)
    t_idx = lax.broadcasted_iota(jnp.int32, (TM_OUT, TM_OUT), 1)
    prefix = jnp.dot(onehot.astype(BF16), jnp.where(s_idx <= t_idx, 1.0, 0.0).astype(BF16),
                     preferred_element_type=F32)
    run = run_ref[...]
    rank = jnp.sum(onehot * (prefix - 1.0 + run[:, 0:1]), axis=0, keepdims=True)
    run = run + jnp.sum(onehot, axis=1, keepdims=True)
    run_ref[...] = run
    bucket_ref[...] = bucket.astype(jnp.int32)
    rank_ref[...] = rank.astype(jnp.int32)
    counts_ref[...] = run


def out_and_route(x_pair, x_is_combined, mixer_pairs, p, mod, layer, hgrn_g, w_out_bf16, norm2_g, router_pieces,
                  router_b):
    n_ctx = TP // TM_OUT
    ctx_tile = lambda i: (jnp.minimum(i, n_ctx - 1), 0)
    lat_tile = lambda i: (jnp.maximum(i - n_ctx, 0), 0)
    mod_row = lambda i: jnp.where(i < n_ctx, 0, 1 + (i - n_ctx) // (DEC_SEQ // TM_OUT))
    tile = lambda w: pl.BlockSpec((TM_OUT, w), lambda i: (i, 0))
    full = lambda r, c: pl.BlockSpec((r, c), lambda i: (0, 0))
    stream_specs = [pl.BlockSpec((TM_OUT, D_MODEL), ctx_tile),
                    pl.BlockSpec((TM_OUT, D_MODEL),
                                 (lambda i: (jnp.maximum(i, n_ctx), 0)) if x_is_combined else lat_tile)]
    stream_args = list(x_pair)
    for o_ctx, o_lat in mixer_pairs:
        stream_specs += [pl.BlockSpec((TM_OUT, MIX_BLK), ctx_tile), pl.BlockSpec((TM_OUT, MIX_BLK), lat_tile)]
        stream_args += [o_ctx, o_lat]
    return pl.pallas_call(
        _out_kernel,
        grid=(T // TM_OUT,),
        in_specs=stream_specs + [
            pl.BlockSpec((TM_OUT, MIX_BLK), lambda i: (i, G_BG)),
            pl.BlockSpec((1, 6, D_MODEL), lambda i: (mod_row(i), 0, 0)),
            full(1, MIX_BLK), pl.BlockSpec((1, D_MODEL, D_MODEL), lambda i: (layer, 0, 0)), full(1, D_MODEL),
            full(D_MODEL, 2 * LANES), full(N_EXPERTS, 1),
        ],
        out_specs=[tile(D_MODEL), tile(ROW_W), pl.BlockSpec((1, TM_OUT), lambda i: (0, i)),
                   pl.BlockSpec((1, TM_OUT), lambda i: (0, i)), full(BUCKET_ROWS, LANES)],
        out_shape=[jax.ShapeDtypeStruct((T, D_MODEL), F32),
                   jax.ShapeDtypeStruct((T, ROW_W), F32),
                   jax.ShapeDtypeStruct((1, T), jnp.int32),
                   jax.ShapeDtypeStruct((1, T), jnp.int32),
                   jax.ShapeDtypeStruct((BUCKET_ROWS, LANES), F32)],
        scratch_shapes=[pltpu.VMEM((BUCKET_ROWS, LANES), F32)],
        compiler_params=_cparams(("arbitrary",)),
        name="out_and_route",
    )(*stream_args, p, mod, jnp.tile(hgrn_g, HEADS)[None, :], w_out_bf16,
      norm2_g[None, :], router_pieces, router_b[:, None])


def _router_pieces(router_w):
    hi, lo = _bf16_pieces(router_w.astype(F32), 2)
    pad = lambda a: jnp.pad(a, ((0, 0), (0, LANES - N_EXPERTS)))
    return jnp.concatenate([pad(hi), pad(lo)], axis=1)


def routing_plan(bucket, rank, counts):
    counts = counts[:N_BUCKETS, 0].astype(jnp.int32)
    n_tiles = (counts + TM_MOE - 1) // TM_MOE
    tile_end = jnp.cumsum(n_tiles)
    tile_start = tile_end - n_tiles
    buckets = jnp.arange(N_BUCKETS, dtype=jnp.int32)
    start_of_token = jnp.sum(jnp.where(bucket[0][:, None] == buckets[None, :], tile_start[None, :], 0), axis=1)
    dest = start_of_token * TM_MOE + rank[0]
    tiles = jnp.arange(MAX_TILES, dtype=jnp.int32)
    valid = tiles < tile_end[-1]
    tile_bucket = jnp.sum((jnp.minimum(tiles, tile_end[-1] - 1)[:, None] >= tile_end[None, :]).astype(jnp.int32), axis=1)
    pair_a = np.array([a for a, _ in EXPERT_PAIRS], np.int32)
    pair_b = np.array([b for _, b in EXPERT_PAIRS], np.int32)
    per = N_EXPERTS // N_GROUPS
    exp_a = jnp.asarray((np.arange(N_BUCKETS) // len(EXPERT_PAIRS)) * per + np.tile(pair_a, N_GROUPS), jnp.int32)
    exp_b = jnp.asarray((np.arange(N_BUCKETS) // len(EXPERT_PAIRS)) * per + np.tile(pair_b, N_GROUPS), jnp.int32)
    pick = tile_bucket[:, None] == buckets[None, :]
    tile_a = jnp.sum(jnp.where(pick, exp_a[None, :], 0), axis=1)
    tile_b = jnp.sum(jnp.where(pick, exp_b[None, :], 0), axis=1)
    plan = [dest.astype(jnp.int32), valid.astype(jnp.int32), (tile_end[-1:] - 1).astype(jnp.int32)]
    for tile_e in (tile_a, tile_b):
        changed = jnp.concatenate([jnp.ones((1,), bool), tile_e[1:] != tile_e[:-1]])
        load = (valid & changed).astype(jnp.int32)
        ring = (jnp.cumsum(load) - 1) % WEIGHT_RING
        plan += [tile_e.astype(jnp.int32), load, ring.astype(jnp.int32)]
    return plan


def _row_copy(src, src_row, dst, dst_row, sem):
    return pltpu.make_async_copy(src.at[pl.ds(src_row, 1), :], dst.at[pl.ds(dst_row, 1), :], sem)


def _scatter_kernel(dest_ref, h_ref, init_ref, o_ref, sem):
    del init_ref
    base = pl.program_id(0) * TM_SCATTER

    for r in range(TM_SCATTER):
        _row_copy(h_ref, r, o_ref, dest_ref[base + r], sem).start(priority=r % N_DMA_PRIORITIES)
    pltpu.make_async_copy(h_ref, o_ref.at[pl.ds(0, TM_SCATTER), :], sem).wait()


def scatter_to_slots(h2, dest, slots):
    return pl.pallas_call(
        _scatter_kernel,
        grid_spec=pltpu.PrefetchScalarGridSpec(
            num_scalar_prefetch=1,
            grid=(T // TM_SCATTER,),
            in_specs=[pl.BlockSpec((TM_SCATTER, ROW_W), lambda i, d: (i, 0)),
                      pl.BlockSpec(memory_space=pl.ANY)],
            out_specs=pl.BlockSpec(memory_space=pl.ANY),
            scratch_shapes=[pltpu.SemaphoreType.DMA(())],
        ),
        out_shape=jax.ShapeDtypeStruct((N_SLOTS, ROW_W), F32),
        input_output_aliases={2: 0},
        compiler_params=_cparams(("arbitrary",)),
        name="scatter_to_slots",
    )(dest, h2, slots)


def _moe_kernel(valid_ref, last_ref, ta_ref, la_ref, ra_ref, tb_ref, lb_ref, rb_ref, h_ref, wg_hbm, wu_hbm, wd_hbm,
                o_ref, wg_buf, wu_buf, wd_buf, sem, *, layer):
    del last_ref
    i = pl.program_id(0)
    plans = ((ta_ref, la_ref, ra_ref), (tb_ref, lb_ref, rb_ref))

    def copies(tile, position):
        expert_ref, _, ring_ref = plans[position]
        expert, entry = expert_ref[tile], ring_ref[tile]
        return [pltpu.make_async_copy(hbm.at[layer, expert], buf.at[position, entry], sem.at[position, entry])
                for hbm, buf in ((wg_hbm, wg_buf), (wu_hbm, wu_buf), (wd_hbm, wd_buf))]

    def start_loads(tile):
        for position in range(2):
            @pl.when(plans[position][1][tile] == 1)
            def _(position=position):
                for copy in copies(tile, position):
                    copy.start(priority=1)

    @pl.when(i == 0)
    def _():
        for tile in range(WEIGHT_LOOKAHEAD):
            start_loads(tile)

    @pl.when(i + WEIGHT_LOOKAHEAD < MAX_TILES)
    def _():
        start_loads(i + WEIGHT_LOOKAHEAD)

    for position in range(2):
        @pl.when(plans[position][1][i] == 1)
        def _(position=position):
            for copy in copies(i, position):
                copy.wait()

    @pl.when(valid_ref[i] == 1)
    def _():
        x = h_ref[:, :D_MODEL].astype(BF16)
        gates = h_ref[:, D_MODEL:]
        y = jnp.zeros((TM_MOE, D_MODEL), F32)
        for n, (_, _, ring_ref) in enumerate(plans):
            entry = ring_ref[i]
            a = jnp.dot(x, wg_buf[n, entry].astype(BF16), preferred_element_type=F32)
            u = jnp.dot(x, wu_buf[n, entry].astype(BF16), preferred_element_type=F32)
            z = a * jax.nn.sigmoid(a) * u * gates[:, n:n + 1]
            y = y + jnp.dot(z.astype(BF16), wd_buf[n, entry].astype(BF16), preferred_element_type=F32)
        o_ref[:, 0, :] = y

    @pl.when(valid_ref[i] == 0)
    def _():
        o_ref[...] = jnp.zeros((TM_MOE, 1, D_MODEL), F32)


def moe(h_slots, valid, last, tile_a, load_a, ring_a, tile_b, load_b, ring_b, layer, wg, wu, wd):
    assert WEIGHT_RING > WEIGHT_LOOKAHEAD and MAX_TILES >= WEIGHT_LOOKAHEAD
    hbm = pl.BlockSpec(memory_space=pl.ANY)
    return pl.pallas_call(
        functools.partial(_moe_kernel, layer=layer),
        grid_spec=pltpu.PrefetchScalarGridSpec(
            num_scalar_prefetch=8,
            grid=(MAX_TILES,),
            in_specs=[pl.BlockSpec((TM_MOE, ROW_W), lambda i, valid, last, *_: (jnp.minimum(i, last[0]), 0)),
                      hbm, hbm, hbm],
            out_specs=pl.BlockSpec((TM_MOE, 1, D_MODEL), lambda i, *_: (i, 0, 0)),
            scratch_shapes=[pltpu.VMEM((2, WEIGHT_RING, D_MODEL, D_EXPERT), F32),
                            pltpu.VMEM((2, WEIGHT_RING, D_MODEL, D_EXPERT), F32),
                            pltpu.VMEM((2, WEIGHT_RING, D_EXPERT, D_MODEL), F32),
                            pltpu.SemaphoreType.DMA((2, WEIGHT_RING))],
        ),
        out_shape=jax.ShapeDtypeStruct((N_SLOTS, 1, D_MODEL), F32),
        compiler_params=_cparams(("arbitrary",)),
        name="moe",
    )(valid, last, tile_a, load_a, ring_a, tile_b, load_b, ring_b, h_slots, wg, wu, wd)


def _gather_tile(dest_ref, y_ref, buf_ref, sem, tile, slot):
    for r in range(TM):
        pltpu.make_async_copy(y_ref.at[pl.ds(dest_ref[tile * TM + r], 1)],
                              buf_ref.at[slot, pl.ds(r, 1)], sem.at[slot]).start(priority=r % N_DMA_PRIORITIES)


def _moe_residual(dest_ref, y_ref, buf_ref, sem, x1_ref, mod_ref):
    i = pl.program_id(0)
    slot = i % 2

    @pl.when(i == 0)
    def _():
        _gather_tile(dest_ref, y_ref, buf_ref, sem, 0, 0)

    @pl.when(i + 1 < pl.num_programs(0))
    def _():
        _gather_tile(dest_ref, y_ref, buf_ref, sem, i + 1, 1 - slot)

    pltpu.make_async_copy(y_ref.at[pl.ds(0, TM)], buf_ref.at[slot], sem.at[slot]).wait()
    return x1_ref[...] + mod_ref[0][5:6] * buf_ref[slot, :, 0, :]


def _final_kernel(dest_ref, y_ref, x1_ref, mod_ref, g_ref, oc_ref, ol_ref, buf_ref, sem):
    x2 = _moe_residual(dest_ref, y_ref, buf_ref, sem, x1_ref, mod_ref)
    ms = jnp.mean(x2 * x2, axis=-1, keepdims=True)
    y = x2 * lax.rsqrt(ms + EPS) * g_ref[...]

    @pl.when(_is_ctx_tile())
    def _():
        oc_ref[...] = y

    @pl.when(jnp.logical_not(_is_ctx_tile()))
    def _():
        ol_ref[...] = y


_GATHER_SCRATCH = [pltpu.VMEM((2, TM, 1, D_MODEL), F32), pltpu.SemaphoreType.DMA((2,))]


def final_norm(dest, y_slots, x1, mod, final_g):
    return pl.pallas_call(
        _final_kernel,
        grid_spec=pltpu.PrefetchScalarGridSpec(
            num_scalar_prefetch=1,
            grid=(N_TILES,),
            in_specs=[pl.BlockSpec(memory_space=pl.ANY),
                      pl.BlockSpec((TM, D_MODEL), lambda i, d: (i, 0)),
                      pl.BlockSpec((1, 6, D_MODEL), lambda i, d: (_mod_row(i), 0, 0)),
                      pl.BlockSpec((1, D_MODEL), lambda i, d: (0, 0))],
            out_specs=[pl.BlockSpec((TM, D_MODEL), _ctx_tile), pl.BlockSpec((TM, D_MODEL), _lat_tile)],
            scratch_shapes=_GATHER_SCRATCH,
        ),
        out_shape=[jax.ShapeDtypeStruct((TP, D_MODEL), F32), jax.ShapeDtypeStruct((TL, D_MODEL), F32)],
        compiler_params=_cparams(("arbitrary",)),
        name="final_norm",
    )(dest, y_slots, x1, mod, final_g[None, :])


def kernel(x_prompt, x_sample, cache_diff_k, cache_diff_v, cache_na_k, cache_na_v, state_hgrn, c, c_ctx,
           norm1_g, norm2_g, ada_w, ada_b, w_in, w_out, diff_lambda, diff_subln_g, hgrn_lb_logits,
           hgrn_norm_g, na_rpb, router_w, router_b, moe_w_gate, moe_w_up, moe_w_down, final_norm_g):
    assert SEQ == TM and PAST_LEN == TM and DEC_SEQ % TM_OUT == 0 and TP % DEC_SEQ == 0
    x_pair = (x_prompt.reshape(TP, D_MODEL), x_sample.reshape(TL, D_MODEL))
    w_in_bf16 = jnp.concatenate([w_in[:, :, c * MIX_BLK:(c + 1) * MIX_BLK] for c in PM_BLOCKS + PG_BLOCKS],
                                axis=-1).astype(BF16)
    w_out_bf16 = w_out.astype(BF16)
    router_pieces = _router_pieces(router_w)
    mods = modulation(jnp.concatenate([c_ctx[None, :], c], axis=0), ada_w, ada_b)
    mods = mods.reshape(DEPTH, 3, 6, D_MODEL)
    lb_sm = jax.nn.softmax(hgrn_lb_logits.astype(F32), axis=0)
    lb_all = jnp.cumsum(lb_sm, axis=0) - lb_sm[0:1]
    rope = _rope_tables()
    dft_ctx = _dft_constants(SEQ)
    dft_lat = _dft_constants(DEC_SEQ)
    lat_blk0 = TP // DEC_SEQ
    states = []
    moe_state = None
    for l in range(DEPTH):
        if moe_state is None:
            pm, pg, *new_kv = projection(*x_pair, mods[l], norm1_g[l], w_in_bf16, rope)
        else:
            x, pm, pg, *new_kv = projection_after_moe(*moe_state, mods[l - 1], mods[l], l, norm1_g[l], w_in_bf16, rope,
                                                 new_kv)
            x_pair = (x, x)

        lq = diff_lambda[l].astype(F32)
        lam_init = 0.8 - 0.6 * math.exp(-0.3 * l)
        lam = (jnp.exp(jnp.sum(lq[0] * lq[1])) - jnp.exp(jnp.sum(lq[2] * lq[3])) + lam_init).reshape(1)
        subln = jnp.tile(diff_subln_g[l], HEADS)[None, :]
        diff = functools.partial(attention, pm, cols=(M_AQ, M_AK, M_AV), lam=lam, norm_g=subln, n_maps=2,
                                 post_scale=1.0 - lam_init)
        (oa_ctx,), (of_ctx, ob_ctx, st_ctx), (oc_ctx,), (od_ctx,) = run_parts(
            [diff(row_blk0=0, n_seq=BATCH, seq_len=SEQ),
             hgrn(pg, 0, BATCH, SEQ, lb_all[l], None),
             attention(pm, 0, (M_CQ, M_CK, M_CV), BATCH, SEQ, lam, subln, n_maps=1, post_scale=1.0),
             fourier_mix(pm, 0, BATCH, SEQ, dft_ctx)],
            (BATCH, SEQ // TM), "context_mixers")
        (oa_lat,), (of_lat, ob_lat, _), (oc_lat,), (od_lat,) = run_parts(
            [diff(row_blk0=lat_blk0, n_seq=DEC_BATCH, seq_len=DEC_SEQ, cache=(cache_diff_k, cache_diff_v), layer=l),
             hgrn(pg, CTX_TILES, DEC_BATCH, DEC_SEQ, lb_all[l], _state_to_blockdiag(state_hgrn[:, l])),
             na_latent(pm, cache_na_k, cache_na_v, l, _na_bias_tables(na_rpb[l])),
             fourier_mix(pm, lat_blk0, DEC_BATCH, DEC_SEQ, dft_lat)],
            (DEC_BATCH, DEC_SEQ // TM), "latent_mixers")

        x1, h2, bucket, rank, counts = out_and_route(
            x_pair, l > 0, ((oa_ctx, oa_lat), (of_ctx, of_lat), (ob_ctx, ob_lat), (oc_ctx, oc_lat),
                            (od_ctx, od_lat)),
            pg, mods[l], l, hgrn_norm_g[l], w_out_bf16, norm2_g[l], router_pieces, router_b)
        dest, *tile_plan = routing_plan(bucket, rank, counts)
        h_slots = scatter_to_slots(h2, dest, jnp.zeros((N_SLOTS, ROW_W), F32) if l == 0 else h_slots)
        y_slots = moe(h_slots, *tile_plan, l, moe_w_gate, moe_w_up, moe_w_down)
        moe_state = (dest, y_slots, x1)

        states.append(st_ctx)
    y_prompt, y_sample = final_norm(*moe_state, mods[DEPTH - 1], final_norm_g)
    return (y_prompt.reshape(BATCH, SEQ, D_MODEL), y_sample.reshape(DEC_BATCH, DEC_SEQ, D_MODEL),
            *new_kv, jnp.stack(states, axis=1))
```

```python
import functools
import math
from typing import Any, NamedTuple

import numpy as np
import jax
import jax.numpy as jnp
from jax import lax
from jax.experimental import pallas as pl
from jax.experimental.pallas import tpu as pltpu

F32 = jnp.float32
BF16 = jnp.bfloat16
HIGHEST = lax.Precision.HIGHEST

D_MODEL = 1024
BATCH = 16
SEQ = 256
DEPTH = 2
DEC_BATCH = 2
DEC_SEQ = 2048
PAST_LEN = 256
GRID_W = 64
GRID_H = DEC_SEQ // GRID_W
EPS = 1e-6
NEG_BIG = -1e30
HEADS = 4
HEAD_DIM = 64
MIX_BLK = HEADS * HEAD_DIM
A_DIM = 32
ROPE_BASE = 10000.0
B_CHUNK = 32
NA_WIN_H = 8
NA_WIN_W = 16
N_EXPERTS = 16
N_GROUPS = 4
D_EXPERT = 512
PROJ_W = 12 * MIX_BLK
TP = BATCH * SEQ
TL = DEC_BATCH * DEC_SEQ
T = TP + TL
TM = 256
N_TILES = T // TM
CTX_TILES = TP // TM
LAT_TILES_PER_SEQ = DEC_SEQ // TM
(C_AQ, C_AK, C_AV, C_BQ, C_BFF, C_BFB, C_BV, C_BG, C_CQ, C_CK, C_CV, C_DU) = range(12)
PM_BLOCKS = (C_AQ, C_AK, C_AV, C_CQ, C_CK, C_CV, C_DU)
PG_BLOCKS = (C_BQ, C_BFF, C_BFB, C_BV, C_BG)
(M_AQ, M_AK, M_AV, M_CQ, M_CK, M_CV, M_DU) = range(len(PM_BLOCKS))
(G_BQ, G_BFF, G_BFB, G_BV, G_BG) = range(len(PG_BLOCKS))
PM_W = len(PM_BLOCKS) * MIX_BLK
PG_W = len(PG_BLOCKS) * MIX_BLK
NA_SLAB_ROWS = 12
NA_SLAB = NA_SLAB_ROWS * GRID_W
LANES = 128
ROW_W = D_MODEL + LANES
EXPERT_PAIRS = ((0, 1), (0, 2), (0, 3), (1, 3), (2, 3), (2, 1))
N_BUCKETS = N_GROUPS * len(EXPERT_PAIRS)
BUCKET_ROWS = 32
N_DMA_PRIORITIES = 2
TM_OUT = 512
TM_SCATTER = 1024
WEIGHT_LOOKAHEAD = 2
WEIGHT_RING = WEIGHT_LOOKAHEAD + 1
TM_MOE = 384
MAX_TILES = T // TM_MOE + N_BUCKETS
N_SLOTS = MAX_TILES * TM_MOE
VMEM_LIMIT = 56 * 1024 * 1024


def _cparams(sem):
    return pltpu.CompilerParams(dimension_semantics=sem, vmem_limit_bytes=VMEM_LIMIT)


class Part(NamedTuple):
    kernel: Any
    in_specs: list
    args: list
    out_specs: list
    out_shapes: list
    scratch: list


def _run_parts_kernel(*refs, layout):
    n_in = sum(n for _, n, _, _ in layout)
    n_out = sum(n for _, _, n, _ in layout)
    ins, outs, scratch = refs[:n_in], refs[n_in:n_in + n_out], refs[n_in + n_out:]
    i = o = s = 0
    for kernel, k_in, k_out, k_scratch in layout:
        kernel(*ins[i:i + k_in], *outs[o:o + k_out], *scratch[s:s + k_scratch])
        i, o, s = i + k_in, o + k_out, s + k_scratch


def run_parts(parts, grid, name):
    layout = tuple((p.kernel, len(p.in_specs), len(p.out_specs), len(p.scratch)) for p in parts)
    outs = pl.pallas_call(
        functools.partial(_run_parts_kernel, layout=layout),
        grid=grid,
        in_specs=[s for p in parts for s in p.in_specs],
        out_specs=[s for p in parts for s in p.out_specs],
        out_shape=[s for p in parts for s in p.out_shapes],
        scratch_shapes=[s for p in parts for s in p.scratch],
        compiler_params=_cparams(("arbitrary", "arbitrary")),
        name=name,
    )(*[a for p in parts for a in p.args])
    result, o = [], 0
    for p in parts:
        result.append(outs[o:o + len(p.out_specs)])
        o += len(p.out_specs)
    return result


def _head_lanes(width=MIX_BLK):
    return lax.broadcasted_iota(jnp.int32, (1, width), 1)


def _lane_range(lane, lo, n):
    return (lane >= lo) & (lane < lo + n)


def _same_head_matrix():
    r = lax.broadcasted_iota(jnp.int32, (MIX_BLK, MIX_BLK), 0) // HEAD_DIM
    c = lax.broadcasted_iota(jnp.int32, (MIX_BLK, MIX_BLK), 1) // HEAD_DIM
    return r == c


def _bf16_pieces(x, n):
    pieces = []
    for _ in range(n):
        piece = x.astype(BF16)
        pieces.append(piece)
        x = x - piece.astype(F32)
    return pieces


def _select_sum_left(onehot_bf16, x):
    cols = x.shape[1]
    stacked = jnp.dot(onehot_bf16, jnp.concatenate(_bf16_pieces(x, 3), axis=1), preferred_element_type=F32)
    return stacked[:, :cols] + stacked[:, cols:2 * cols] + stacked[:, 2 * cols:]


def _select_sum_right(x, onehot_bf16):
    rows = x.shape[0]
    stacked = jnp.dot(jnp.concatenate(_bf16_pieces(x, 3), axis=0), onehot_bf16, preferred_element_type=F32)
    return stacked[:rows] + stacked[rows:2 * rows] + stacked[2 * rows:]


def _head_mean_square(o):
    ones = jnp.where(_same_head_matrix(), 1.0, 0.0).astype(BF16)
    return _select_sum_right(o * o, ones) * (1.0 / HEAD_DIM)


def _mod_row(i):
    return jnp.where(i < CTX_TILES, 0, 1 + (i - CTX_TILES) // LAT_TILES_PER_SEQ)


def _mod_kernel(c_ref, w_ref, b_ref, o_ref):
    w = w_ref[0]
    for r in range(3):
        c = c_ref[r]
        s = c * jax.nn.sigmoid(c)
        o_ref[0, r:r + 1, :] = jnp.sum(s * w, axis=0, keepdims=True) + b_ref[0]


def modulation(c_rows, ada_w, ada_b):
    nt = 768
    n_out = 6 * D_MODEL
    return pl.pallas_call(
        _mod_kernel,
        grid=(DEPTH, n_out // nt),
        in_specs=[
            pl.BlockSpec((3, D_MODEL, 1), lambda l, j: (0, 0, 0)),
            pl.BlockSpec((1, D_MODEL, nt), lambda l, j: (l, 0, j)),
            pl.BlockSpec((1, 1, nt), lambda l, j: (l, 0, j)),
        ],
        out_specs=pl.BlockSpec((1, 3, nt), lambda l, j: (l, 0, j)),
        out_shape=jax.ShapeDtypeStruct((DEPTH, 3, n_out), F32),
        compiler_params=_cparams(("arbitrary", "arbitrary")),
        name="modulation",
    )(c_rows[:, :, None], ada_w, ada_b[:, None, :])


def _is_ctx_tile():
    return pl.program_id(0) < CTX_TILES


def _ctx_tile(i, *_):
    return (jnp.minimum(i, CTX_TILES - 1), 0)


def _lat_tile(i, *_):
    return (jnp.maximum(i - CTX_TILES, 0), 0)


def _proj_kernel(xc_ref, xl_ref, mod_ref, g_ref, w_ref, cos_ref, sa_ref, sb_ref, pm_ref, pg_ref, *cache_refs):
    x = jnp.where(_is_ctx_tile(), xc_ref[...], xl_ref[...])
    _proj_body(x, mod_ref, g_ref, w_ref, cos_ref, sa_ref, sb_ref, pm_ref, pg_ref, cache_refs)


def _proj_after_moe_kernel(dest_ref, y_ref, x1_ref, modp_ref, mod_ref, g_ref, w_ref, cos_ref, sa_ref, sb_ref,
                           *rest):
    x2_ref, pm_ref, pg_ref = rest[4:7]
    cache_refs, (buf_ref, sem) = rest[7:11], rest[11:]
    x2 = _moe_residual(dest_ref, y_ref, buf_ref, sem, x1_ref, modp_ref)
    x2_ref[...] = x2
    _proj_body(x2, mod_ref, g_ref, w_ref, cos_ref, sa_ref, sb_ref, pm_ref, pg_ref, cache_refs)


def _proj_body(x, mod_ref, g_ref, w_ref, cos_ref, sa_ref, sb_ref, pm_ref, pg_ref, cache_refs):
    ms = jnp.mean(x * x, axis=-1, keepdims=True)
    mod = mod_ref[0]
    h = x * lax.rsqrt(ms + EPS) * g_ref[...] * (1.0 + mod[1:2]) + mod[0:1]
    p = jnp.dot(h.astype(BF16), w_ref[0], preferred_element_type=F32)
    t = p[:, :2 * MIX_BLK]
    pm_ref[:, :2 * MIX_BLK] = (t * cos_ref[...] + pltpu.roll(t, 1, 1) * sa_ref[...]
                               + pltpu.roll(t, 2 * MIX_BLK - 1, 1) * sb_ref[...]).astype(BF16)
    pm_ref[:, 2 * MIX_BLK:] = p[:, 2 * MIX_BLK:PM_W].astype(BF16)
    pg_ref[...] = p[:, PM_W:]

    @pl.when(_is_ctx_tile())
    def _():
        for ref, col in zip(cache_refs, (M_AK, M_AV, M_CK, M_CV)):
            for hd in range(HEADS):
                lo = col * MIX_BLK + hd * HEAD_DIM
                ref[0, 0, hd] = p[:, lo:lo + HEAD_DIM]
            if ref.shape[1] > 1:
                ref[0, 1:] = jnp.zeros((ref.shape[1] - 1,) + tuple(ref.shape[2:]), F32)


_CACHE_SHAPE = jax.ShapeDtypeStruct((BATCH, DEPTH, HEADS, SEQ, HEAD_DIM), F32)


def _cache_spec(layer):
    n_layers = DEPTH if layer == 0 else 1
    return pl.BlockSpec((1, n_layers, HEADS, SEQ, HEAD_DIM),
                        lambda i, *_: (jnp.minimum(i, CTX_TILES - 1), layer, 0, 0, 0))


def _rope_tables():
    nf = A_DIM // 4
    freqs = ROPE_BASE ** (-np.arange(nf, dtype=np.float64) / nf)
    pos = np.arange(DEC_SEQ)
    row = (pos // GRID_W).astype(np.float64)
    col = (pos % GRID_W).astype(np.float64)
    ang = np.concatenate([row[:, None] * freqs, col[:, None] * freqs], axis=-1)
    cos = np.repeat(np.cos(ang), 2, axis=-1)
    sin = np.repeat(np.sin(ang), 2, axis=-1)
    odd = (np.arange(A_DIM) % 2 == 1)[None, :]
    sin_from_left = np.where(odd, sin, 0.0)
    sin_from_right = np.where(odd, 0.0, -sin)
    reps = 2 * MIX_BLK // A_DIM
    ident = (np.ones((TM, 2 * MIX_BLK)), np.zeros((TM, 2 * MIX_BLK)))
    return tuple(jnp.asarray(np.concatenate([np.tile(t, (1, reps)), tail], axis=0), F32)
                 for t, tail in ((cos, ident[0]), (sin_from_left, ident[1]), (sin_from_right, ident[1])))


def _rope_block(i):
    return (jnp.where(i < CTX_TILES, LAT_TILES_PER_SEQ, (i - CTX_TILES) % LAT_TILES_PER_SEQ), 0)


def projection_after_moe(dest, y_slots, x1, mod_prev, mod, layer, norm_g, w_in_bf16, rope, caches):
    rope_spec = pl.BlockSpec((TM, 2 * MIX_BLK), lambda i, d: _rope_block(i))
    mod_spec = pl.BlockSpec((1, 6, D_MODEL), lambda i, d: (_mod_row(i), 0, 0))
    n_in = 10
    return pl.pallas_call(
        _proj_after_moe_kernel,
        grid_spec=pltpu.PrefetchScalarGridSpec(
            num_scalar_prefetch=1,
            grid=(N_TILES,),
            in_specs=[pl.BlockSpec(memory_space=pl.ANY),
                      pl.BlockSpec((TM, D_MODEL), lambda i, d: (i, 0)),
                      mod_spec, mod_spec,
                      pl.BlockSpec((1, D_MODEL), lambda i, d: (0, 0)),
                      pl.BlockSpec((1, D_MODEL, PROJ_W), lambda i, d: (layer, 0, 0)),
                      rope_spec, rope_spec, rope_spec] + [pl.BlockSpec(memory_space=pl.ANY)] * 4,
            out_specs=[pl.BlockSpec((TM, D_MODEL), lambda i, d: (i, 0)),
                       pl.BlockSpec((TM, PM_W), lambda i, d: (i, 0)),
                       pl.BlockSpec((TM, PG_W), lambda i, d: (i, 0))] + [_cache_spec(layer)] * 4,
            scratch_shapes=_GATHER_SCRATCH,
        ),
        out_shape=[jax.ShapeDtypeStruct((T, D_MODEL), F32), jax.ShapeDtypeStruct((T, PM_W), BF16),
                   jax.ShapeDtypeStruct((T, PG_W), F32)] + [_CACHE_SHAPE] * 4,
        input_output_aliases={n_in + n: 3 + n for n in range(4)},
        compiler_params=_cparams(("arbitrary",)),
        name="projection_after_moe",
    )(dest, y_slots, x1, mod_prev, mod, norm_g[None, :], w_in_bf16, *rope, *caches)


def projection(x_ctx, x_lat, mod, norm_g, w_in_bf16, rope):
    layer = 0
    rope_spec = pl.BlockSpec((TM, 2 * MIX_BLK), _rope_block)
    return pl.pallas_call(
        _proj_kernel,
        grid=(N_TILES,),
        in_specs=[
            pl.BlockSpec((TM, D_MODEL), _ctx_tile),
            pl.BlockSpec((TM, D_MODEL), _lat_tile),
            pl.BlockSpec((1, 6, D_MODEL), lambda i: (_mod_row(i), 0, 0)),
            pl.BlockSpec((1, D_MODEL), lambda i: (0, 0)),
            pl.BlockSpec((1, D_MODEL, PROJ_W), lambda i: (layer, 0, 0)),
            rope_spec, rope_spec, rope_spec,
        ],
        out_specs=[pl.BlockSpec((TM, PM_W), lambda i: (i, 0)), pl.BlockSpec((TM, PG_W), lambda i: (i, 0))]
        + [_cache_spec(layer)] * 4,
        out_shape=[jax.ShapeDtypeStruct((T, PM_W), BF16), jax.ShapeDtypeStruct((T, PG_W), F32)]
        + [_CACHE_SHAPE] * 4,
        compiler_params=_cparams(("arbitrary",)),
        name="projection",
    )(x_ctx, x_lat, mod, norm_g[None, :], w_in_bf16, *rope)


LOG2_E = 1.4426950408889634


def _exp2_rows(s):
    e = jnp.exp2(s - jnp.max(s, axis=-1, keepdims=True))
    return e, 1.0 / jnp.sum(e, axis=-1, keepdims=True)


def _attn_kernel(lam_ref, q_ref, k_ref, v_ref, *rest, n_maps, post_scale, with_cache):
    if with_cache:
        kc_ref, vc_ref, g_ref, o_ref, kt_ref, vb_ref = rest
    else:
        g_ref, o_ref, kt_ref, vb_ref = rest

    @pl.when(pl.program_id(1) == 0)
    def _():
        k = k_ref[...].astype(F32)
        v = v_ref[...]
        if with_cache:
            k = jnp.concatenate([_cache_heads_on_lanes(kc_ref), k], axis=0)
            v = jnp.concatenate([_cache_heads_on_lanes(vc_ref).astype(BF16), v], axis=0)
        kt_ref[...] = k.T.astype(BF16)
        vb_ref[...] = v

    lane = _head_lanes()
    map_dim = HEAD_DIM // n_maps
    q = q_ref[...].astype(F32) * (map_dim ** -0.5 * LOG2_E)
    kt = kt_ref[...]
    vb = vb_ref[...]
    stack_rows = kt_ref.shape[1] <= TM
    weights = []
    for h in range(HEADS):
        masked = [jnp.where(_lane_range(lane, h * HEAD_DIM + j * map_dim, map_dim), q, 0.0).astype(BF16)
                  for j in range(n_maps)]
        if stack_rows:
            s = jnp.dot(jnp.concatenate(masked, axis=0), kt, preferred_element_type=F32)
            scores = [s[j * TM:(j + 1) * TM] for j in range(n_maps)]
        else:
            scores = [jnp.dot(m, kt, preferred_element_type=F32) for m in masked]
        parts = [_exp2_rows(x) for x in scores]
        w = parts[0][0] * parts[0][1]
        if n_maps == 2:
            w = w - parts[1][0] * (lam_ref[0] * parts[1][1])
        weights.append(w.astype(BF16))
    if stack_rows:
        oh = jnp.dot(jnp.concatenate(weights, axis=0), vb, preferred_element_type=F32)
        outs = [oh[h * TM:(h + 1) * TM] for h in range(HEADS)]
    else:
        outs = [jnp.dot(w, vb, preferred_element_type=F32) for w in weights]
    o = jnp.zeros(q.shape, F32)
    for h in range(HEADS):
        o = jnp.where(_lane_range(lane, h * HEAD_DIM, HEAD_DIM), outs[h], o)
    if n_maps == 2:
        o = o * lax.rsqrt(_head_mean_square(o) + EPS) * g_ref[...] * post_scale
    o_ref[...] = o


def _cache_block_spec(layer):
    return pl.BlockSpec((1, 1, HEADS, PAST_LEN, HEAD_DIM), lambda b, i: (b, layer, 0, 0, 0))


def _cache_heads_on_lanes(ref):
    return jnp.concatenate([ref[0, 0, h] for h in range(HEADS)], axis=1)


def attention(p, row_blk0, cols, n_seq, seq_len, lam, norm_g, *, n_maps, post_scale, cache=None, layer=0):
    nb = seq_len // TM
    kv_len = seq_len + (PAST_LEN if cache is not None else 0)
    kern = functools.partial(_attn_kernel, n_maps=n_maps, post_scale=post_scale, with_cache=cache is not None)
    kv_spec = lambda col: pl.BlockSpec((seq_len, MIX_BLK), lambda b, i: (row_blk0 + b, col))
    cache_specs = [_cache_block_spec(layer)] * 2 if cache is not None else []
    return Part(
        kernel=kern,
        in_specs=[
            pl.BlockSpec(memory_space=pltpu.SMEM),
            pl.BlockSpec((TM, MIX_BLK), lambda b, i: ((row_blk0 + b) * nb + i, cols[0])),
            kv_spec(cols[1]), kv_spec(cols[2]), *cache_specs,
            pl.BlockSpec((1, MIX_BLK), lambda b, i: (0, 0)),
        ],
        out_specs=[pl.BlockSpec((TM, MIX_BLK), lambda b, i: (b * nb + i, 0))],
        out_shapes=[jax.ShapeDtypeStruct((n_seq * seq_len, MIX_BLK), F32)],
        scratch=[pltpu.VMEM((MIX_BLK, kv_len), BF16), pltpu.VMEM((kv_len, MIX_BLK), BF16)],
        args=[lam, p, p, p, *(cache or ()), norm_g])


def _na_slab_start(i):
    return jnp.clip(i - 1, 0, GRID_H // 4 - NA_SLAB_ROWS // 4)


def _na_kernel(q_ref, k_ref, v_ref, kc_ref, vc_ref, bias_ref, o_ref):
    i = pl.program_id(1)
    start = pl.multiple_of(_na_slab_start(i) * TM, TM)
    ks_t = k_ref[pl.ds(start, NA_SLAB), :].astype(F32).T.astype(BF16)
    vs = v_ref[pl.ds(start, NA_SLAB), :]
    kc_t = _cache_heads_on_lanes(kc_ref).T.astype(BF16)
    vc = _cache_heads_on_lanes(vc_ref).astype(BF16)
    q = q_ref[...].astype(F32) * (HEAD_DIM ** -0.5)
    lane = _head_lanes()
    heads = [_lane_range(lane, h * HEAD_DIM, HEAD_DIM) for h in range(HEADS)]
    qm = jnp.concatenate([jnp.where(in_head, q, 0.0) for in_head in heads], axis=0).astype(BF16)
    s_loc = jnp.dot(qm, ks_t, preferred_element_type=F32)
    s_ctx = jnp.dot(qm, kc_t, preferred_element_type=F32)
    e_locs, e_ctxs, dens = [], [], []
    for h in range(HEADS):
        rows = slice(h * TM, (h + 1) * TM)
        sl = s_loc[rows] + bias_ref[0, h]
        sc = s_ctx[rows]
        m = jnp.maximum(jnp.max(sl, axis=-1, keepdims=True), jnp.max(sc, axis=-1, keepdims=True))
        e_loc = jnp.exp(sl - m)
        e_ctx = jnp.exp(sc - m)
        dens.append(jnp.sum(e_loc, axis=-1, keepdims=True) + jnp.sum(e_ctx, axis=-1, keepdims=True))
        e_locs.append(e_loc.astype(BF16))
        e_ctxs.append(e_ctx.astype(BF16))
    oh = (jnp.dot(jnp.concatenate(e_locs, axis=0), vs, preferred_element_type=F32)
          + jnp.dot(jnp.concatenate(e_ctxs, axis=0), vc, preferred_element_type=F32))
    o = jnp.zeros(q.shape, F32)
    for h in range(HEADS):
        o = jnp.where(heads[h], oh[h * TM:(h + 1) * TM] / dens[h], o)
    o_ref[...] = o


def _na_bias_tables(rpb):
    n_dr, n_dc = 2 * NA_WIN_H - 1, 2 * NA_WIN_W - 1
    cq = np.arange(GRID_W)[:, None]
    ck = np.arange(GRID_W)[None, :]
    wc0 = np.clip(cq - NA_WIN_W // 2, 0, GRID_W - NA_WIN_W)
    col_ok = (ck >= wc0) & (ck < wc0 + NA_WIN_W)
    col_pick = np.clip(ck - cq + NA_WIN_W - 1, 0, n_dc - 1)[..., None] == np.arange(n_dc)
    by_col = jnp.einsum("hab,qcb->haqc", rpb.astype(F32), jnp.asarray(col_pick, F32), precision=HIGHEST)
    margin = 4
    by_col = jnp.pad(by_col.transpose(0, 2, 1, 3), ((0, 0), (0, 0), (margin, margin), (0, 0)))
    by_col = by_col.reshape(HEADS, GRID_W, (n_dr + 2 * margin) * GRID_W)
    pieces, row_ok = [], []
    for tile in (0, 1, GRID_H // 4 - 1):
        slab0 = int(np.clip(tile - 1, 0, GRID_H // 4 - NA_SLAB_ROWS // 4)) * 4
        rq = tile * 4 + np.arange(4)
        rk = (slab0 + np.arange(NA_SLAB) // GRID_W)[None, :]
        wr0 = np.clip(rq - NA_WIN_H // 2, 0, GRID_H - NA_WIN_H)[:, None]
        row_ok.append((rk >= wr0) & (rk < wr0 + NA_WIN_H))
        for r in rq:
            first = slab0 - int(r) + NA_WIN_H - 1 + margin
            assert 0 <= first and first + NA_SLAB_ROWS <= n_dr + 2 * margin
            pieces.append(by_col[:, :, first * GRID_W:first * GRID_W + NA_SLAB])
    table = jnp.stack(pieces).reshape(3, 4, HEADS, GRID_W, NA_SLAB).transpose(0, 2, 1, 3, 4)
    valid = np.stack(row_ok)[:, None, :, None, :] & np.tile(col_ok, (1, NA_SLAB_ROWS))[None, None, None]
    table = jnp.where(jnp.asarray(valid), table, NEG_BIG)
    return table.reshape(3, HEADS, TM, NA_SLAB)


def na_latent(p, kc, vc, layer, bias):
    n_t = LAT_TILES_PER_SEQ
    seq_blk0 = TP // DEC_SEQ

    def bias_idx(b, i):
        return (jnp.minimum(i, 1) + i // (n_t - 1), 0, 0, 0)

    return Part(
        kernel=_na_kernel,
        in_specs=[
            pl.BlockSpec((TM, MIX_BLK), lambda b, i: (CTX_TILES + b * n_t + i, M_CQ)),
            pl.BlockSpec((DEC_SEQ, MIX_BLK), lambda b, i: (seq_blk0 + b, M_CK)),
            pl.BlockSpec((DEC_SEQ, MIX_BLK), lambda b, i: (seq_blk0 + b, M_CV)),
            _cache_block_spec(layer), _cache_block_spec(layer),
            pl.BlockSpec((1, HEADS, TM, NA_SLAB), bias_idx),
        ],
        out_specs=[pl.BlockSpec((TM, MIX_BLK), lambda b, i: (b * n_t + i, 0))],
        out_shapes=[jax.ShapeDtypeStruct((TL, MIX_BLK), F32)],
        scratch=[],
        args=[p, p, p, kc, vc, bias])


MAX_EXPONENT = 80.0


def _hgrn_direction(q_ref, f_ref, v_ref, lb, st_ref, o_ref, reverse):
    n_ch = TM // B_CHUNK
    r_idx = lax.broadcasted_iota(jnp.int32, (TM, TM), 0)
    c_idx = lax.broadcasted_iota(jnp.int32, (TM, TM), 1)
    tri = (c_idx >= r_idx) if reverse else (c_idx <= r_idx)
    zq = q_ref[...]
    q = zq * jax.nn.sigmoid(zq)
    z = f_ref[...]
    gate = (1.0 - lb) * jax.nn.sigmoid(z)
    logf = jnp.log(lb + gate)
    kk = (1.0 - lb) - gate
    b = _select_sum_left(jnp.where(tri, 1.0, 0.0).astype(BF16), logf)
    b3 = b.reshape(n_ch, B_CHUNK, MIX_BLK)
    mid = B_CHUNK // 2 if reverse else B_CHUNK // 2 - 1
    q_in = (q.reshape(b3.shape) * jnp.exp(b3 - b3[:, mid:mid + 1, :])).reshape(TM, MIX_BLK)
    q_dec = (q * jnp.exp(b)).astype(BF16)
    b_t = b.T
    kk_t = kk.T
    far = 0 if reverse else TM - 1
    b_far = b_t[:, far:far + 1]
    k_dec_t = (kk_t * jnp.exp(b_far - b_t)).astype(BF16)
    vb = v_ref[...].astype(BF16)
    st = st_ref[...]
    o_state = jnp.dot(q_dec, st.astype(BF16), preferred_element_type=F32)
    kv = jnp.dot(k_dec_t, vb, preferred_element_type=F32)
    st_ref[...] = st * jnp.exp(b_far) + jnp.where(_same_head_matrix(), kv, 0.0)
    lane = _head_lanes()
    token = lax.broadcasted_iota(jnp.int32, (1, TM), 1)
    local = lax.broadcasted_iota(jnp.int32, (HEADS * B_CHUNK, 1), 0) % B_CHUNK
    heads = [_lane_range(lane, h * HEAD_DIM, HEAD_DIM) for h in range(HEADS)]
    weights = []
    half = TM // 2
    for c in range(n_ch):
        near_half = (c >= n_ch // 2) if reverse else (c < n_ch // 2)
        keys = (slice(half, TM) if reverse else slice(0, half)) if near_half else slice(0, TM)
        ref = b_t[:, c * B_CHUNK + mid:c * B_CHUNK + mid + 1]
        k_c_t = (kk_t[:, keys] * jnp.exp(jnp.minimum(ref - b_t[:, keys], MAX_EXPONENT))).astype(BF16)
        q_c = q_in[c * B_CHUNK:(c + 1) * B_CHUNK, :]
        lhs = jnp.concatenate([jnp.where(in_head, q_c, 0.0) for in_head in heads], axis=0)
        a = jnp.dot(lhs.astype(BF16), k_c_t, preferred_element_type=F32)
        t_abs = c * B_CHUNK + local
        tok = token[:, keys]
        a = jnp.where((tok >= t_abs) if reverse else (tok <= t_abs), a, 0.0).astype(BF16)
        if near_half:
            unseen = jnp.zeros((HEADS * B_CHUNK, half), BF16)
            a = jnp.concatenate([unseen, a] if reverse else [a, unseen], axis=1)
        weights.append(a)
    res = jnp.dot(jnp.concatenate(weights, axis=0), vb, preferred_element_type=F32)
    for c in range(n_ch):
        rows = slice(c * B_CHUNK, (c + 1) * B_CHUNK)
        o_c = o_state[rows, :]
        for h, in_head in enumerate(heads):
            lo = (c * HEADS + h) * B_CHUNK
            o_c = o_c + jnp.where(in_head, res[lo:lo + B_CHUNK, :], 0.0)
        o_ref[rows, :] = o_c


def _hgrn_kernel(qf_ref, ff_ref, vf_ref, qb_ref, fb_ref, vb_ref, lb_ref, s0_ref,
                 of_ref, ob_ref, s_ref, stf_ref, stb_ref, *, has_s0):
    j = pl.program_id(1)

    @pl.when(j == 0)
    def _():
        if has_s0:
            stf_ref[...] = s0_ref[0, 0]
            stb_ref[...] = s0_ref[0, 1]
        else:
            stf_ref[...] = jnp.zeros((MIX_BLK, MIX_BLK), F32)
            stb_ref[...] = jnp.zeros((MIX_BLK, MIX_BLK), F32)

    lb = lb_ref[...]
    _hgrn_direction(qf_ref, ff_ref, vf_ref, lb[0:1], stf_ref, of_ref, False)
    _hgrn_direction(qb_ref, fb_ref, vb_ref, lb[1:2], stb_ref, ob_ref, True)

    @pl.when(j == pl.num_programs(1) - 1)
    def _():
        for d, st_ref in enumerate((stf_ref, stb_ref)):
            s = st_ref[...]
            for hd in range(HEADS):
                lo = hd * HEAD_DIM
                s_ref[0, d, hd] = s[lo:lo + HEAD_DIM, lo:lo + HEAD_DIM]


def hgrn(p, row_tile0, n_seq, seq_len, lb, s0):
    nb = seq_len // TM
    has_s0 = s0 is not None
    if s0 is None:
        s0 = jnp.zeros((1, 2, MIX_BLK, MIX_BLK), F32)

    def fwd(col):
        return pl.BlockSpec((TM, MIX_BLK), lambda s, j: (row_tile0 + s * nb + j, col))

    def bwd(col):
        return pl.BlockSpec((TM, MIX_BLK), lambda s, j: (row_tile0 + s * nb + nb - 1 - j, col))

    state_spec = pl.BlockSpec((1, 2, MIX_BLK, MIX_BLK), lambda s, j: (s if has_s0 else 0, 0, 0, 0))
    out_rows = n_seq * seq_len
    return Part(
        kernel=functools.partial(_hgrn_kernel, has_s0=has_s0),
        in_specs=[fwd(G_BQ), fwd(G_BFF), fwd(G_BV), bwd(G_BQ), bwd(G_BFB), bwd(G_BV),
                  pl.BlockSpec((2, MIX_BLK), lambda s, j: (0, 0)), state_spec],
        out_specs=[
            pl.BlockSpec((TM, MIX_BLK), lambda s, j: (s * nb + j, 0)),
            pl.BlockSpec((TM, MIX_BLK), lambda s, j: (s * nb + nb - 1 - j, 0)),
            pl.BlockSpec((1, 2, HEADS, HEAD_DIM, HEAD_DIM), lambda s, j: (s, 0, 0, 0, 0)),
        ],
        out_shapes=[jax.ShapeDtypeStruct((out_rows, MIX_BLK), F32),
                    jax.ShapeDtypeStruct((out_rows, MIX_BLK), F32),
                    jax.ShapeDtypeStruct((n_seq, 2, HEADS, HEAD_DIM, HEAD_DIM), F32)],
        scratch=[pltpu.VMEM((MIX_BLK, MIX_BLK), F32), pltpu.VMEM((MIX_BLK, MIX_BLK), F32)],
        args=[p, p, p, p, p, p, lb, s0])


def _state_to_blockdiag(s):
    eye = jnp.eye(HEADS, dtype=F32)
    full = s.astype(F32)[:, :, :, :, None, :] * eye[None, None, :, None, :, None]
    return full.reshape(s.shape[0], 2, MIX_BLK, MIX_BLK)


def _fft_kernel(u_ref, cs64_ref, csl_ref, o_ref, ab_ref, *, norm):
    seq_len = u_ref.shape[0]

    @pl.when(pl.program_id(1) == 0)
    def _():
        ab = jnp.dot(u_ref[...], cs64_ref[...], preferred_element_type=F32).astype(BF16)
        ab_ref[:seq_len, :] = ab[:, :MIX_BLK]
        ab_ref[seq_len:, :] = ab[:, MIX_BLK:]

    o_ref[...] = jnp.dot(csl_ref[...], ab_ref[...], preferred_element_type=F32) * norm


def _dft_tables(n):
    k = np.arange(n)
    ang = 2.0 * np.pi * ((k[:, None] * k[None, :]) % n) / n
    return np.cos(ang), np.sin(ang)


def _dft_constants(seq_len):
    c64, s64 = _dft_tables(HEAD_DIM)
    eye = np.eye(HEADS)
    cl, sl = _dft_tables(seq_len)
    as_bf16 = lambda a: jnp.asarray(a, F32).astype(BF16)
    return (as_bf16(np.concatenate([np.kron(eye, c64), -np.kron(eye, s64)], axis=1)),
            as_bf16(np.concatenate([cl, sl], axis=1)))


def fourier_mix(p, row_blk0, n_seq, seq_len, consts):
    cs64, csl = consts
    nb = seq_len // TM
    norm = 1.0 / math.sqrt(seq_len * HEAD_DIM)
    return Part(
        kernel=functools.partial(_fft_kernel, norm=norm),
        in_specs=[
            pl.BlockSpec((seq_len, MIX_BLK), lambda s, i: (row_blk0 + s, M_DU)),
            pl.BlockSpec((MIX_BLK, 2 * MIX_BLK), lambda s, i: (0, 0)),
            pl.BlockSpec((TM, 2 * seq_len), lambda s, i: (i, 0)),
        ],
        out_specs=[pl.BlockSpec((TM, MIX_BLK), lambda s, i: (s * nb + i, 0))],
        out_shapes=[jax.ShapeDtypeStruct((n_seq * seq_len, MIX_BLK), F32)],
        scratch=[pltpu.VMEM((2 * seq_len, MIX_BLK), BF16)],
        args=[p, cs64, csl])


def _route(logits_t, rb):
    per = N_EXPERTS // N_GROUPS
    score = [jax.nn.sigmoid(logits_t[e:e + 1, :]) for e in range(N_EXPERTS)]
    sel = [score[e] + rb[e:e + 1, :] for e in range(N_EXPERTS)]
    gscore = []
    for g in range(N_GROUPS):
        vals = sel[g * per:(g + 1) * per]
        best = None
        for a in range(per):
            for b in range(a + 1, per):
                pair = vals[a] + vals[b]
                best = pair if best is None else jnp.maximum(best, pair)
        gscore.append(best)
    chosen = []
    for g in range(N_GROUPS):
        ok = None
        for j in range(N_GROUPS):
            if j == g:
                continue
            cond = gscore[g] > gscore[j] if j < g else gscore[g] >= gscore[j]
            ok = cond if ok is None else ok & cond
        chosen.append(ok)
    picked = []
    for e in range(N_EXPERTS):
        g = e // per
        rank = jnp.zeros_like(sel[e])
        for j in range(g * per, (g + 1) * per):
            if j == e:
                continue
            ahead = sel[j] >= sel[e] if j < e else sel[j] > sel[e]
            rank = rank + jnp.where(ahead, 1.0, 0.0)
        picked.append(chosen[g] & (rank < 2.0))
    wsum = jnp.zeros_like(score[0])
    for e in range(N_EXPERTS):
        wsum = wsum + jnp.where(picked[e], score[e], 0.0)
    bucket = jnp.zeros_like(wsum)
    w_a = jnp.zeros_like(wsum)
    w_b = jnp.zeros_like(wsum)
    for g in range(N_GROUPS):
        for n, (a, b) in enumerate(EXPERT_PAIRS):
            hit = picked[g * per + a] & picked[g * per + b]
            bucket = jnp.where(hit, float(g * len(EXPERT_PAIRS) + n), bucket)
            w_a = jnp.where(hit, score[g * per + a] / wsum, w_a)
            w_b = jnp.where(hit, score[g * per + b] / wsum, w_b)
    return bucket, w_a, w_b


def _out_kernel(*refs):
    streams, rest = refs[:12], refs[12:]
    (bg_ref, mod_ref, hg_ref, w_ref, g2_ref, rw_ref, rb_ref,
     x1_ref, h2_ref, bucket_ref, rank_ref, counts_ref, run_ref) = rest
    is_ctx = pl.program_id(0) < TP // TM_OUT
    x, o_a, o_f, o_b, o_c, o_d = (jnp.where(is_ctx, streams[2 * n][...], streams[2 * n + 1][...])
                                  for n in range(6))

    @pl.when(pl.program_id(0) == 0)
    def _():
        run_ref[...] = jnp.zeros(run_ref.shape, F32)

    mod = mod_ref[0]
    hb = o_f + o_b
    zg = bg_ref[...]
    hb = hb * lax.rsqrt(_head_mean_square(hb) + EPS) * hg_ref[...] * (zg * jax.nn.sigmoid(zg))
    mixers = jnp.concatenate([part.astype(BF16) for part in (o_a, hb, o_c, o_d)], axis=1)
    mixed = jnp.dot(mixers, w_ref[0], preferred_element_type=F32)
    x1 = x + mod[2:3] * mixed
    x1_ref[...] = x1
    ms = jnp.mean(x1 * x1, axis=-1, keepdims=True)
    h2 = x1 * lax.rsqrt(ms + EPS) * g2_ref[...] * (1.0 + mod[4:5]) + mod[3:4]
    rw = rw_ref[...]
    r = jnp.dot(jnp.concatenate(_bf16_pieces(h2, 2), axis=0), rw, preferred_element_type=F32)
    r = r[:TM_OUT] + r[TM_OUT:]
    bucket, w_a, w_b = _route((r[:, :LANES] + r[:, LANES:]).T, rb_ref[...])
    h2_ref[:, :D_MODEL] = h2
    h2_ref[:, D_MODEL:] = jnp.concatenate([w_a, w_b, jnp.zeros((LANES - 2, TM_OUT), F32)], axis=0).T
    onehot = jnp.where(lax.broadcasted_iota(jnp.int32, (BUCKET_ROWS, 1), 0).astype(F32) == bucket, 1.0, 0.0)
    s_idx = lax.broadcasted_iota(jnp.int32, (TM_OUT, TM_OUT), 0)
    t_idx = lax.broadcasted_iota(jnp.int32, (TM_OUT, TM_OUT), 1)
    prefix = jnp.dot(onehot.astype(BF16), jnp.where(s_idx <= t_idx, 1.0, 0.0).astype(BF16),
                     preferred_element_type=F32)
    run = run_ref[...]
    rank = jnp.sum(onehot * (prefix - 1.0 + run[:, 0:1]), axis=0, keepdims=True)
    run = run + jnp.sum(onehot, axis=1, keepdims=True)
    run_ref[...] = run
    bucket_ref[...] = bucket.astype(jnp.int32)
    rank_ref[...] = rank.astype(jnp.int32)
    counts_ref[...] = run


def out_and_route(x_pair, x_is_combined, mixer_pairs, p, mod, layer, hgrn_g, w_out_bf16, norm2_g, router_pieces,
                  router_b):
    n_ctx = TP // TM_OUT
    ctx_tile = lambda i: (jnp.minimum(i, n_ctx - 1), 0)
    lat_tile = lambda i: (jnp.maximum(i - n_ctx, 0), 0)
    mod_row = lambda i: jnp.where(i < n_ctx, 0, 1 + (i - n_ctx) // (DEC_SEQ // TM_OUT))
    tile = lambda w: pl.BlockSpec((TM_OUT, w), lambda i: (i, 0))
    full = lambda r, c: pl.BlockSpec((r, c), lambda i: (0, 0))
    stream_specs = [pl.BlockSpec((TM_OUT, D_MODEL), ctx_tile),
                    pl.BlockSpec((TM_OUT, D_MODEL),
                                 (lambda i: (jnp.maximum(i, n_ctx), 0)) if x_is_combined else lat_tile)]
    stream_args = list(x_pair)
    for o_ctx, o_lat in mixer_pairs:
        stream_specs += [pl.BlockSpec((TM_OUT, MIX_BLK), ctx_tile), pl.BlockSpec((TM_OUT, MIX_BLK), lat_tile)]
        stream_args += [o_ctx, o_lat]
    return pl.pallas_call(
        _out_kernel,
        grid=(T // TM_OUT,),
        in_specs=stream_specs + [
            pl.BlockSpec((TM_OUT, MIX_BLK), lambda i: (i, G_BG)),
            pl.BlockSpec((1, 6, D_MODEL), lambda i: (mod_row(i), 0, 0)),
            full(1, MIX_BLK), pl.BlockSpec((1, D_MODEL, D_MODEL), lambda i: (layer, 0, 0)), full(1, D_MODEL),
            full(D_MODEL, 2 * LANES), full(N_EXPERTS, 1),
        ],
        out_specs=[tile(D_MODEL), tile(ROW_W), pl.BlockSpec((1, TM_OUT), lambda i: (0, i)),
                   pl.BlockSpec((1, TM_OUT), lambda i: (0, i)), full(BUCKET_ROWS, LANES)],
        out_shape=[jax.ShapeDtypeStruct((T, D_MODEL), F32),
                   jax.ShapeDtypeStruct((T, ROW_W), F32),
                   jax.ShapeDtypeStruct((1, T), jnp.int32),
                   jax.ShapeDtypeStruct((1, T), jnp.int32),
                   jax.ShapeDtypeStruct((BUCKET_ROWS, LANES), F32)],
        scratch_shapes=[pltpu.VMEM((BUCKET_ROWS, LANES), F32)],
        compiler_params=_cparams(("arbitrary",)),
        name="out_and_route",
    )(*stream_args, p, mod, jnp.tile(hgrn_g, HEADS)[None, :], w_out_bf16,
      norm2_g[None, :], router_pieces, router_b[:, None])


def _router_pieces(router_w):
    hi, lo = _bf16_pieces(router_w.astype(F32), 2)
    pad = lambda a: jnp.pad(a, ((0, 0), (0, LANES - N_EXPERTS)))
    return jnp.concatenate([pad(hi), pad(lo)], axis=1)


def routing_plan(bucket, rank, counts):
    counts = counts[:N_BUCKETS, 0].astype(jnp.int32)
    n_tiles = (counts + TM_MOE - 1) // TM_MOE
    tile_end = jnp.cumsum(n_tiles)
    tile_start = tile_end - n_tiles
    buckets = jnp.arange(N_BUCKETS, dtype=jnp.int32)
    start_of_token = jnp.sum(jnp.where(bucket[0][:, None] == buckets[None, :], tile_start[None, :], 0), axis=1)
    dest = start_of_token * TM_MOE + rank[0]
    tiles = jnp.arange(MAX_TILES, dtype=jnp.int32)
    valid = tiles < tile_end[-1]
    tile_bucket = jnp.sum((jnp.minimum(tiles, tile_end[-1] - 1)[:, None] >= tile_end[None, :]).astype(jnp.int32), axis=1)
    pair_a = np.array([a for a, _ in EXPERT_PAIRS], np.int32)
    pair_b = np.array([b for _, b in EXPERT_PAIRS], np.int32)
    per = N_EXPERTS // N_GROUPS
    exp_a = jnp.asarray((np.arange(N_BUCKETS) // len(EXPERT_PAIRS)) * per + np.tile(pair_a, N_GROUPS), jnp.int32)
    exp_b = jnp.asarray((np.arange(N_BUCKETS) // len(EXPERT_PAIRS)) * per + np.tile(pair_b, N_GROUPS), jnp.int32)
    pick = tile_bucket[:, None] == buckets[None, :]
    tile_a = jnp.sum(jnp.where(pick, exp_a[None, :], 0), axis=1)
    tile_b = jnp.sum(jnp.where(pick, exp_b[None, :], 0), axis=1)
    plan = [dest.astype(jnp.int32), valid.astype(jnp.int32), (tile_end[-1:] - 1).astype(jnp.int32)]
    for tile_e in (tile_a, tile_b):
        changed = jnp.concatenate([jnp.ones((1,), bool), tile_e[1:] != tile_e[:-1]])
        load = (valid & changed).astype(jnp.int32)
        ring = (jnp.cumsum(load) - 1) % WEIGHT_RING
        plan += [tile_e.astype(jnp.int32), load, ring.astype(jnp.int32)]
    return plan


def _row_copy(src, src_row, dst, dst_row, sem):
    return pltpu.make_async_copy(src.at[pl.ds(src_row, 1), :], dst.at[pl.ds(dst_row, 1), :], sem)


def _scatter_kernel(dest_ref, h_ref, init_ref, o_ref, sem):
    del init_ref
    base = pl.program_id(0) * TM_SCATTER

    for r in range(TM_SCATTER):
        _row_copy(h_ref, r, o_ref, dest_ref[base + r], sem).start(priority=r % N_DMA_PRIORITIES)
    pltpu.make_async_copy(h_ref, o_ref.at[pl.ds(0, TM_SCATTER), :], sem).wait()


def scatter_to_slots(h2, dest, slots):
    return pl.pallas_call(
        _scatter_kernel,
        grid_spec=pltpu.PrefetchScalarGridSpec(
            num_scalar_prefetch=1,
            grid=(T // TM_SCATTER,),
            in_specs=[pl.BlockSpec((TM_SCATTER, ROW_W), lambda i, d: (i, 0)),
                      pl.BlockSpec(memory_space=pl.ANY)],
            out_specs=pl.BlockSpec(memory_space=pl.ANY),
            scratch_shapes=[pltpu.SemaphoreType.DMA(())],
        ),
        out_shape=jax.ShapeDtypeStruct((N_SLOTS, ROW_W), F32),
        input_output_aliases={2: 0},
        compiler_params=_cparams(("arbitrary",)),
        name="scatter_to_slots",
    )(dest, h2, slots)


def _moe_kernel(valid_ref, last_ref, ta_ref, la_ref, ra_ref, tb_ref, lb_ref, rb_ref, h_ref, wg_hbm, wu_hbm, wd_hbm,
                o_ref, wg_buf, wu_buf, wd_buf, sem, *, layer):
    del last_ref
    i = pl.program_id(0)
    plans = ((ta_ref, la_ref, ra_ref), (tb_ref, lb_ref, rb_ref))

    def copies(tile, position):
        expert_ref, _, ring_ref = plans[position]
        expert, entry = expert_ref[tile], ring_ref[tile]
        return [pltpu.make_async_copy(hbm.at[layer, expert], buf.at[position, entry], sem.at[position, entry])
                for hbm, buf in ((wg_hbm, wg_buf), (wu_hbm, wu_buf), (wd_hbm, wd_buf))]

    def start_loads(tile):
        for position in range(2):
            @pl.when(plans[position][1][tile] == 1)
            def _(position=position):
                for copy in copies(tile, position):
                    copy.start(priority=1)

    @pl.when(i == 0)
    def _():
        for tile in range(WEIGHT_LOOKAHEAD):
            start_loads(tile)

    @pl.when(i + WEIGHT_LOOKAHEAD < MAX_TILES)
    def _():
        start_loads(i + WEIGHT_LOOKAHEAD)

    for position in range(2):
        @pl.when(plans[position][1][i] == 1)
        def _(position=position):
            for copy in copies(i, position):
                copy.wait()

    @pl.when(valid_ref[i] == 1)
    def _():
        x = h_ref[:, :D_MODEL].astype(BF16)
        gates = h_ref[:, D_MODEL:]
        y = jnp.zeros((TM_MOE, D_MODEL), F32)
        for n, (_, _, ring_ref) in enumerate(plans):
            entry = ring_ref[i]
            a = jnp.dot(x, wg_buf[n, entry].astype(BF16), preferred_element_type=F32)
            u = jnp.dot(x, wu_buf[n, entry].astype(BF16), preferred_element_type=F32)
            z = a * jax.nn.sigmoid(a) * u * gates[:, n:n + 1]
            y = y + jnp.dot(z.astype(BF16), wd_buf[n, entry].astype(BF16), preferred_element_type=F32)
        o_ref[:, 0, :] = y

    @pl.when(valid_ref[i] == 0)
    def _():
        o_ref[...] = jnp.zeros((TM_MOE, 1, D_MODEL), F32)


def moe(h_slots, valid, last, tile_a, load_a, ring_a, tile_b, load_b, ring_b, layer, wg, wu, wd):
    assert WEIGHT_RING > WEIGHT_LOOKAHEAD and MAX_TILES >= WEIGHT_LOOKAHEAD
    hbm = pl.BlockSpec(memory_space=pl.ANY)
    return pl.pallas_call(
        functools.partial(_moe_kernel, layer=layer),
        grid_spec=pltpu.PrefetchScalarGridSpec(
            num_scalar_prefetch=8,
            grid=(MAX_TILES,),
            in_specs=[pl.BlockSpec((TM_MOE, ROW_W), lambda i, valid, last, *_: (jnp.minimum(i, last[0]), 0)),
                      hbm, hbm, hbm],
            out_specs=pl.BlockSpec((TM_MOE, 1, D_MODEL), lambda i, *_: (i, 0, 0)),
            scratch_shapes=[pltpu.VMEM((2, WEIGHT_RING, D_MODEL, D_EXPERT), F32),
                            pltpu.VMEM((2, WEIGHT_RING, D_MODEL, D_EXPERT), F32),
                            pltpu.VMEM((2, WEIGHT_RING, D_EXPERT, D_MODEL), F32),
                            pltpu.SemaphoreType.DMA((2, WEIGHT_RING))],
        ),
        out_shape=jax.ShapeDtypeStruct((N_SLOTS, 1, D_MODEL), F32),
        compiler_params=_cparams(("arbitrary",)),
        name="moe",
    )(valid, last, tile_a, load_a, ring_a, tile_b, load_b, ring_b, h_slots, wg, wu, wd)


def _gather_tile(dest_ref, y_ref, buf_ref, sem, tile, slot):
    for r in range(TM):
        pltpu.make_async_copy(y_ref.at[pl.ds(dest_ref[tile * TM + r], 1)],
                              buf_ref.at[slot, pl.ds(r, 1)], sem.at[slot]).start(priority=r % N_DMA_PRIORITIES)


def _moe_residual(dest_ref, y_ref, buf_ref, sem, x1_ref, mod_ref):
    i = pl.program_id(0)
    slot = i % 2

    @pl.when(i == 0)
    def _():
        _gather_tile(dest_ref, y_ref, buf_ref, sem, 0, 0)

    @pl.when(i + 1 < pl.num_programs(0))
    def _():
        _gather_tile(dest_ref, y_ref, buf_ref, sem, i + 1, 1 - slot)

    pltpu.make_async_copy(y_ref.at[pl.ds(0, TM)], buf_ref.at[slot], sem.at[slot]).wait()
    return x1_ref[...] + mod_ref[0][5:6] * buf_ref[slot, :, 0, :]


def _final_kernel(dest_ref, y_ref, x1_ref, mod_ref, g_ref, oc_ref, ol_ref, buf_ref, sem):
    x2 = _moe_residual(dest_ref, y_ref, buf_ref, sem, x1_ref, mod_ref)
    ms = jnp.mean(x2 * x2, axis=-1, keepdims=True)
    y = x2 * lax.rsqrt(ms + EPS) * g_ref[...]

    @pl.when(_is_ctx_tile())
    def _():
        oc_ref[...] = y

    @pl.when(jnp.logical_not(_is_ctx_tile()))
    def _():
        ol_ref[...] = y


_GATHER_SCRATCH = [pltpu.VMEM((2, TM, 1, D_MODEL), F32), pltpu.SemaphoreType.DMA((2,))]


def final_norm(dest, y_slots, x1, mod, final_g):
    return pl.pallas_call(
        _final_kernel,
        grid_spec=pltpu.PrefetchScalarGridSpec(
            num_scalar_prefetch=1,
            grid=(N_TILES,),
            in_specs=[pl.BlockSpec(memory_space=pl.ANY),
                      pl.BlockSpec((TM, D_MODEL), lambda i, d: (i, 0)),
                      pl.BlockSpec((1, 6, D_MODEL), lambda i, d: (_mod_row(i), 0, 0)),
                      pl.BlockSpec((1, D_MODEL), lambda i, d: (0, 0))],
            out_specs=[pl.BlockSpec((TM, D_MODEL), _ctx_tile), pl.BlockSpec((TM, D_MODEL), _lat_tile)],
            scratch_shapes=_GATHER_SCRATCH,
        ),
        out_shape=[jax.ShapeDtypeStruct((TP, D_MODEL), F32), jax.ShapeDtypeStruct((TL, D_MODEL), F32)],
        compiler_params=_cparams(("arbitrary",)),
        name="final_norm",
    )(dest, y_slots, x1, mod, final_g[None, :])


def kernel(x_prompt, x_sample, cache_diff_k, cache_diff_v, cache_na_k, cache_na_v, state_hgrn, c, c_ctx,
           norm1_g, norm2_g, ada_w, ada_b, w_in, w_out, diff_lambda, diff_subln_g, hgrn_lb_logits,
           hgrn_norm_g, na_rpb, router_w, router_b, moe_w_gate, moe_w_up, moe_w_down, final_norm_g):
    assert SEQ == TM and PAST_LEN == TM and DEC_SEQ % TM_OUT == 0 and TP % DEC_SEQ == 0
    x_pair = (x_prompt.reshape(TP, D_MODEL), x_sample.reshape(TL, D_MODEL))
    w_in_bf16 = jnp.concatenate([w_in[:, :, c * MIX_BLK:(c + 1) * MIX_BLK] for c in PM_BLOCKS + PG_BLOCKS],
                                axis=-1).astype(BF16)
    w_out_bf16 = w_out.astype(BF16)
    router_pieces = _router_pieces(router_w)
    mods = modulation(jnp.concatenate([c_ctx[None, :], c], axis=0), ada_w, ada_b)
    mods = mods.reshape(DEPTH, 3, 6, D_MODEL)
    lb_sm = jax.nn.softmax(hgrn_lb_logits.astype(F32), axis=0)
    lb_all = jnp.cumsum(lb_sm, axis=0) - lb_sm[0:1]
    rope = _rope_tables()
    dft_ctx = _dft_constants(SEQ)
    dft_lat = _dft_constants(DEC_SEQ)
    lat_blk0 = TP // DEC_SEQ
    states = []
    moe_state = None
    for l in range(DEPTH):
        if moe_state is None:
            pm, pg, *new_kv = projection(*x_pair, mods[l], norm1_g[l], w_in_bf16, rope)
        else:
            x, pm, pg, *new_kv = projection_after_moe(*moe_state, mods[l - 1], mods[l], l, norm1_g[l], w_in_bf16, rope,
                                                 new_kv)
            x_pair = (x, x)

        lq = diff_lambda[l].astype(F32)
        lam_init = 0.8 - 0.6 * math.exp(-0.3 * l)
        lam = (jnp.exp(jnp.sum(lq[0] * lq[1])) - jnp.exp(jnp.sum(lq[2] * lq[3])) + lam_init).reshape(1)
        subln = jnp.tile(diff_subln_g[l], HEADS)[None, :]
        diff = functools.partial(attention, pm, cols=(M_AQ, M_AK, M_AV), lam=lam, norm_g=subln, n_maps=2,
                                 post_scale=1.0 - lam_init)
        (oa_ctx,), (of_ctx, ob_ctx, st_ctx), (oc_ctx,), (od_ctx,) = run_parts(
            [diff(row_blk0=0, n_seq=BATCH, seq_len=SEQ),
             hgrn(pg, 0, BATCH, SEQ, lb_all[l], None),
             attention(pm, 0, (M_CQ, M_CK, M_CV), BATCH, SEQ, lam, subln, n_maps=1, post_scale=1.0),
             fourier_mix(pm, 0, BATCH, SEQ, dft_ctx)],
            (BATCH, SEQ // TM), "context_mixers")
        (oa_lat,), (of_lat, ob_lat, _), (oc_lat,), (od_lat,) = run_parts(
            [diff(row_blk0=lat_blk0, n_seq=DEC_BATCH, seq_len=DEC_SEQ, cache=(cache_diff_k, cache_diff_v), layer=l),
             hgrn(pg, CTX_TILES, DEC_BATCH, DEC_SEQ, lb_all[l], _state_to_blockdiag(state_hgrn[:, l])),
             na_latent(pm, cache_na_k, cache_na_v, l, _na_bias_tables(na_rpb[l])),
             fourier_mix(pm, lat_blk0, DEC_BATCH, DEC_SEQ, dft_lat)],
            (DEC_BATCH, DEC_SEQ // TM), "latent_mixers")

        x1, h2, bucket, rank, counts = out_and_route(
            x_pair, l > 0, ((oa_ctx, oa_lat), (of_ctx, of_lat), (ob_ctx, ob_lat), (oc_ctx, oc_lat),
                            (od_ctx, od_lat)),
            pg, mods[l], l, hgrn_norm_g[l], w_out_bf16, norm2_g[l], router_pieces, router_b)
        dest, *tile_plan = routing_plan(bucket, rank, counts)
        h_slots = scatter_to_slots(h2, dest, jnp.zeros((N_SLOTS, ROW_W), F32) if l == 0 else h_slots)
        y_slots = moe(h_slots, *tile_plan, l, moe_w_gate, moe_w_up, moe_w_down)
        moe_state = (dest, y_slots, x1)

        states.append(st_ctx)
    y_prompt, y_sample = final_norm(*moe_state, mods[DEPTH - 1], final_norm_g)
    return (y_prompt.reshape(BATCH, SEQ, D_MODEL), y_sample.reshape(DEC_BATCH, DEC_SEQ, D_MODEL),
            *new_kv, jnp.stack(states, axis=1))
```

```python
import functools
import math
from typing import Any, NamedTuple

import numpy as np
import jax
import jax.numpy as jnp
from jax import lax
from jax.experimental import pallas as pl
from jax.experimental.pallas import tpu as pltpu

F32 = jnp.float32
BF16 = jnp.bfloat16
HIGHEST = lax.Precision.HIGHEST

D_MODEL = 1024
BATCH = 16
SEQ = 256
DEPTH = 2
DEC_BATCH = 2
DEC_SEQ = 2048
PAST_LEN = 256
GRID_W = 64
GRID_H = DEC_SEQ // GRID_W
EPS = 1e-6
NEG_BIG = -1e30
HEADS = 4
HEAD_DIM = 64
MIX_BLK = HEADS * HEAD_DIM
A_DIM = 32
ROPE_BASE = 10000.0
B_CHUNK = 32
NA_WIN_H = 8
NA_WIN_W = 16
N_EXPERTS = 16
N_GROUPS = 4
D_EXPERT = 512
PROJ_W = 12 * MIX_BLK
TP = BATCH * SEQ
TL = DEC_BATCH * DEC_SEQ
T = TP + TL
TM = 256
N_TILES = T // TM
CTX_TILES = TP // TM
LAT_TILES_PER_SEQ = DEC_SEQ // TM
(C_AQ, C_AK, C_AV, C_BQ, C_BFF, C_BFB, C_BV, C_BG, C_CQ, C_CK, C_CV, C_DU) = range(12)
PM_BLOCKS = (C_AQ, C_AK, C_AV, C_CQ, C_CK, C_CV, C_DU)
PG_BLOCKS = (C_BQ, C_BFF, C_BFB, C_BV, C_BG)
(M_AQ, M_AK, M_AV, M_CQ, M_CK, M_CV, M_DU) = range(len(PM_BLOCKS))
(G_BQ, G_BFF, G_BFB, G_BV, G_BG) = range(len(PG_BLOCKS))
PM_W = len(PM_BLOCKS) * MIX_BLK
PG_W = len(PG_BLOCKS) * MIX_BLK
NA_SLAB_ROWS = 12
NA_SLAB = NA_SLAB_ROWS * GRID_W
LANES = 128
ROW_W = D_MODEL + LANES
EXPERT_PAIRS = ((0, 1), (0, 2), (0, 3), (1, 3), (2, 3), (2, 1))
N_BUCKETS = N_GROUPS * len(EXPERT_PAIRS)
BUCKET_ROWS = 32
N_DMA_PRIORITIES = 2
TM_OUT = 512
TM_SCATTER = 2048
WEIGHT_LOOKAHEAD = 2
WEIGHT_RING = WEIGHT_LOOKAHEAD + 1
TM_MOE = 384
MAX_TILES = T // TM_MOE + N_BUCKETS
N_SLOTS = MAX_TILES * TM_MOE
VMEM_LIMIT = 56 * 1024 * 1024


def _cparams(sem):
    return pltpu.CompilerParams(dimension_semantics=sem, vmem_limit_bytes=VMEM_LIMIT)


class Part(NamedTuple):
    kernel: Any
    in_specs: list
    args: list
    out_specs: list
    out_shapes: list
    scratch: list


def _run_parts_kernel(*refs, layout):
    n_in = sum(n for _, n, _, _ in layout)
    n_out = sum(n for _, _, n, _ in layout)
    ins, outs, scratch = refs[:n_in], refs[n_in:n_in + n_out], refs[n_in + n_out:]
    i = o = s = 0
    for kernel, k_in, k_out, k_scratch in layout:
        kernel(*ins[i:i + k_in], *outs[o:o + k_out], *scratch[s:s + k_scratch])
        i, o, s = i + k_in, o + k_out, s + k_scratch


def run_parts(parts, grid, name):
    layout = tuple((p.kernel, len(p.in_specs), len(p.out_specs), len(p.scratch)) for p in parts)
    outs = pl.pallas_call(
        functools.partial(_run_parts_kernel, layout=layout),
        grid=grid,
        in_specs=[s for p in parts for s in p.in_specs],
        out_specs=[s for p in parts for s in p.out_specs],
        out_shape=[s for p in parts for s in p.out_shapes],
        scratch_shapes=[s for p in parts for s in p.scratch],
        compiler_params=_cparams(("arbitrary", "arbitrary")),
        name=name,
    )(*[a for p in parts for a in p.args])
    result, o = [], 0
    for p in parts:
        result.append(outs[o:o + len(p.out_specs)])
        o += len(p.out_specs)
    return result


def _head_lanes(width=MIX_BLK):
    return lax.broadcasted_iota(jnp.int32, (1, width), 1)


def _lane_range(lane, lo, n):
    return (lane >= lo) & (lane < lo + n)


def _same_head_matrix():
    r = lax.broadcasted_iota(jnp.int32, (MIX_BLK, MIX_BLK), 0) // HEAD_DIM
    c = lax.broadcasted_iota(jnp.int32, (MIX_BLK, MIX_BLK), 1) // HEAD_DIM
    return r == c


def _bf16_pieces(x, n):
    pieces = []
    for _ in range(n):
        piece = x.astype(BF16)
        pieces.append(piece)
        x = x - piece.astype(F32)
    return pieces


def _select_sum_left(onehot_bf16, x):
    cols = x.shape[1]
    stacked = jnp.dot(onehot_bf16, jnp.concatenate(_bf16_pieces(x, 3), axis=1), preferred_element_type=F32)
    return stacked[:, :cols] + stacked[:, cols:2 * cols] + stacked[:, 2 * cols:]


def _select_sum_right(x, onehot_bf16):
    rows = x.shape[0]
    stacked = jnp.dot(jnp.concatenate(_bf16_pieces(x, 3), axis=0), onehot_bf16, preferred_element_type=F32)
    return stacked[:rows] + stacked[rows:2 * rows] + stacked[2 * rows:]


def _head_mean_square(o):
    ones = jnp.where(_same_head_matrix(), 1.0, 0.0).astype(BF16)
    return _select_sum_right(o * o, ones) * (1.0 / HEAD_DIM)


def _mod_row(i):
    return jnp.where(i < CTX_TILES, 0, 1 + (i - CTX_TILES) // LAT_TILES_PER_SEQ)


def _mod_kernel(c_ref, w_ref, b_ref, o_ref):
    w = w_ref[0]
    for r in range(3):
        c = c_ref[r]
        s = c * jax.nn.sigmoid(c)
        o_ref[0, r:r + 1, :] = jnp.sum(s * w, axis=0, keepdims=True) + b_ref[0]


def modulation(c_rows, ada_w, ada_b):
    nt = 768
    n_out = 6 * D_MODEL
    return pl.pallas_call(
        _mod_kernel,
        grid=(DEPTH, n_out // nt),
        in_specs=[
            pl.BlockSpec((3, D_MODEL, 1), lambda l, j: (0, 0, 0)),
            pl.BlockSpec((1, D_MODEL, nt), lambda l, j: (l, 0, j)),
            pl.BlockSpec((1, 1, nt), lambda l, j: (l, 0, j)),
        ],
        out_specs=pl.BlockSpec((1, 3, nt), lambda l, j: (l, 0, j)),
        out_shape=jax.ShapeDtypeStruct((DEPTH, 3, n_out), F32),
        compiler_params=_cparams(("arbitrary", "arbitrary")),
        name="modulation",
    )(c_rows[:, :, None], ada_w, ada_b[:, None, :])


def _is_ctx_tile():
    return pl.program_id(0) < CTX_TILES


def _ctx_tile(i, *_):
    return (jnp.minimum(i, CTX_TILES - 1), 0)


def _lat_tile(i, *_):
    return (jnp.maximum(i - CTX_TILES, 0), 0)


def _proj_kernel(xc_ref, xl_ref, mod_ref, g_ref, w_ref, cos_ref, sa_ref, sb_ref, pm_ref, pg_ref, *cache_refs):
    x = jnp.where(_is_ctx_tile(), xc_ref[...], xl_ref[...])
    _proj_body(x, mod_ref, g_ref, w_ref, cos_ref, sa_ref, sb_ref, pm_ref, pg_ref, cache_refs)


def _proj_after_moe_kernel(dest_ref, y_ref, x1_ref, modp_ref, mod_ref, g_ref, w_ref, cos_ref, sa_ref, sb_ref,
                           *rest):
    x2_ref, pm_ref, pg_ref = rest[4:7]
    cache_refs, (buf_ref, sem) = rest[7:11], rest[11:]
    x2 = _moe_residual(dest_ref, y_ref, buf_ref, sem, x1_ref, modp_ref)
    x2_ref[...] = x2
    _proj_body(x2, mod_ref, g_ref, w_ref, cos_ref, sa_ref, sb_ref, pm_ref, pg_ref, cache_refs)


def _proj_body(x, mod_ref, g_ref, w_ref, cos_ref, sa_ref, sb_ref, pm_ref, pg_ref, cache_refs):
    ms = jnp.mean(x * x, axis=-1, keepdims=True)
    mod = mod_ref[0]
    h = x * lax.rsqrt(ms + EPS) * g_ref[...] * (1.0 + mod[1:2]) + mod[0:1]
    p = jnp.dot(h.astype(BF16), w_ref[0], preferred_element_type=F32)
    t = p[:, :2 * MIX_BLK]
    pm_ref[:, :2 * MIX_BLK] = (t * cos_ref[...] + pltpu.roll(t, 1, 1) * sa_ref[...]
                               + pltpu.roll(t, 2 * MIX_BLK - 1, 1) * sb_ref[...]).astype(BF16)
    pm_ref[:, 2 * MIX_BLK:] = p[:, 2 * MIX_BLK:PM_W].astype(BF16)
    pg_ref[...] = p[:, PM_W:]

    @pl.when(_is_ctx_tile())
    def _():
        for ref, col in zip(cache_refs, (M_AK, M_AV, M_CK, M_CV)):
            for hd in range(HEADS):
                lo = col * MIX_BLK + hd * HEAD_DIM
                ref[0, 0, hd] = p[:, lo:lo + HEAD_DIM]
            if ref.shape[1] > 1:
                ref[0, 1:] = jnp.zeros((ref.shape[1] - 1,) + tuple(ref.shape[2:]), F32)


_CACHE_SHAPE = jax.ShapeDtypeStruct((BATCH, DEPTH, HEADS, SEQ, HEAD_DIM), F32)


def _cache_spec(layer):
    n_layers = DEPTH if layer == 0 else 1
    return pl.BlockSpec((1, n_layers, HEADS, SEQ, HEAD_DIM),
                        lambda i, *_: (jnp.minimum(i, CTX_TILES - 1), layer, 0, 0, 0))


def _rope_tables():
    nf = A_DIM // 4
    freqs = ROPE_BASE ** (-np.arange(nf, dtype=np.float64) / nf)
    pos = np.arange(DEC_SEQ)
    row = (pos // GRID_W).astype(np.float64)
    col = (pos % GRID_W).astype(np.float64)
    ang = np.concatenate([row[:, None] * freqs, col[:, None] * freqs], axis=-1)
    cos = np.repeat(np.cos(ang), 2, axis=-1)
    sin = np.repeat(np.sin(ang), 2, axis=-1)
    odd = (np.arange(A_DIM) % 2 == 1)[None, :]
    sin_from_left = np.where(odd, sin, 0.0)
    sin_from_right = np.where(odd, 0.0, -sin)
    reps = 2 * MIX_BLK // A_DIM
    ident = (np.ones((TM, 2 * MIX_BLK)), np.zeros((TM, 2 * MIX_BLK)))
    return tuple(jnp.asarray(np.concatenate([np.tile(t, (1, reps)), tail], axis=0), F32)
                 for t, tail in ((cos, ident[0]), (sin_from_left, ident[1]), (sin_from_right, ident[1])))


def _rope_block(i):
    return (jnp.where(i < CTX_TILES, LAT_TILES_PER_SEQ, (i - CTX_TILES) % LAT_TILES_PER_SEQ), 0)


def projection_after_moe(dest, y_slots, x1, mod_prev, mod, layer, norm_g, w_in_bf16, rope, caches):
    rope_spec = pl.BlockSpec((TM, 2 * MIX_BLK), lambda i, d: _rope_block(i))
    mod_spec = pl.BlockSpec((1, 6, D_MODEL), lambda i, d: (_mod_row(i), 0, 0))
    n_in = 10
    return pl.pallas_call(
        _proj_after_moe_kernel,
        grid_spec=pltpu.PrefetchScalarGridSpec(
            num_scalar_prefetch=1,
            grid=(N_TILES,),
            in_specs=[pl.BlockSpec(memory_space=pl.ANY),
                      pl.BlockSpec((TM, D_MODEL), lambda i, d: (i, 0)),
                      mod_spec, mod_spec,
                      pl.BlockSpec((1, D_MODEL), lambda i, d: (0, 0)),
                      pl.BlockSpec((1, D_MODEL, PROJ_W), lambda i, d: (layer, 0, 0)),
                      rope_spec, rope_spec, rope_spec] + [pl.BlockSpec(memory_space=pl.ANY)] * 4,
            out_specs=[pl.BlockSpec((TM, D_MODEL), lambda i, d: (i, 0)),
                       pl.BlockSpec((TM, PM_W), lambda i, d: (i, 0)),
                       pl.BlockSpec((TM, PG_W), lambda i, d: (i, 0))] + [_cache_spec(layer)] * 4,
            scratch_shapes=_GATHER_SCRATCH,
        ),
        out_shape=[jax.ShapeDtypeStruct((T, D_MODEL), F32), jax.ShapeDtypeStruct((T, PM_W), BF16),
                   jax.ShapeDtypeStruct((T, PG_W), F32)] + [_CACHE_SHAPE] * 4,
        input_output_aliases={n_in + n: 3 + n for n in range(4)},
        compiler_params=_cparams(("arbitrary",)),
        name="projection_after_moe",
    )(dest, y_slots, x1, mod_prev, mod, norm_g[None, :], w_in_bf16, *rope, *caches)


def projection(x_ctx, x_lat, mod, norm_g, w_in_bf16, rope):
    layer = 0
    rope_spec = pl.BlockSpec((TM, 2 * MIX_BLK), _rope_block)
    return pl.pallas_call(
        _proj_kernel,
        grid=(N_TILES,),
        in_specs=[
            pl.BlockSpec((TM, D_MODEL), _ctx_tile),
            pl.BlockSpec((TM, D_MODEL), _lat_tile),
            pl.BlockSpec((1, 6, D_MODEL), lambda i: (_mod_row(i), 0, 0)),
            pl.BlockSpec((1, D_MODEL), lambda i: (0, 0)),
            pl.BlockSpec((1, D_MODEL, PROJ_W), lambda i: (layer, 0, 0)),
            rope_spec, rope_spec, rope_spec,
        ],
        out_specs=[pl.BlockSpec((TM, PM_W), lambda i: (i, 0)), pl.BlockSpec((TM, PG_W), lambda i: (i, 0))]
        + [_cache_spec(layer)] * 4,
        out_shape=[jax.ShapeDtypeStruct((T, PM_W), BF16), jax.ShapeDtypeStruct((T, PG_W), F32)]
        + [_CACHE_SHAPE] * 4,
        compiler_params=_cparams(("arbitrary",)),
        name="projection",
    )(x_ctx, x_lat, mod, norm_g[None, :], w_in_bf16, *rope)


LOG2_E = 1.4426950408889634


def _exp2_rows(s):
    e = jnp.exp2(s - jnp.max(s, axis=-1, keepdims=True))
    return e, 1.0 / jnp.sum(e, axis=-1, keepdims=True)


def _attn_kernel(lam_ref, q_ref, k_ref, v_ref, *rest, n_maps, post_scale, with_cache):
    if with_cache:
        kc_ref, vc_ref, g_ref, o_ref, kt_ref, vb_ref = rest
    else:
        g_ref, o_ref, kt_ref, vb_ref = rest

    @pl.when(pl.program_id(1) == 0)
    def _():
        k = k_ref[...].astype(F32)
        v = v_ref[...]
        if with_cache:
            k = jnp.concatenate([_cache_heads_on_lanes(kc_ref), k], axis=0)
            v = jnp.concatenate([_cache_heads_on_lanes(vc_ref).astype(BF16), v], axis=0)
        kt_ref[...] = k.T.astype(BF16)
        vb_ref[...] = v

    lane = _head_lanes()
    map_dim = HEAD_DIM // n_maps
    q = q_ref[...].astype(F32) * (map_dim ** -0.5 * LOG2_E)
    kt = kt_ref[...]
    vb = vb_ref[...]
    stack_rows = kt_ref.shape[1] <= TM
    weights = []
    for h in range(HEADS):
        masked = [jnp.where(_lane_range(lane, h * HEAD_DIM + j * map_dim, map_dim), q, 0.0).astype(BF16)
                  for j in range(n_maps)]
        if stack_rows:
            s = jnp.dot(jnp.concatenate(masked, axis=0), kt, preferred_element_type=F32)
            scores = [s[j * TM:(j + 1) * TM] for j in range(n_maps)]
        else:
            scores = [jnp.dot(m, kt, preferred_element_type=F32) for m in masked]
        parts = [_exp2_rows(x) for x in scores]
        w = parts[0][0] * parts[0][1]
        if n_maps == 2:
            w = w - parts[1][0] * (lam_ref[0] * parts[1][1])
        weights.append(w.astype(BF16))
    if stack_rows:
        oh = jnp.dot(jnp.concatenate(weights, axis=0), vb, preferred_element_type=F32)
        outs = [oh[h * TM:(h + 1) * TM] for h in range(HEADS)]
    else:
        outs = [jnp.dot(w, vb, preferred_element_type=F32) for w in weights]
    o = jnp.zeros(q.shape, F32)
    for h in range(HEADS):
        o = jnp.where(_lane_range(lane, h * HEAD_DIM, HEAD_DIM), outs[h], o)
    if n_maps == 2:
        o = o * lax.rsqrt(_head_mean_square(o) + EPS) * g_ref[...] * post_scale
    o_ref[...] = o


def _cache_block_spec(layer):
    return pl.BlockSpec((1, 1, HEADS, PAST_LEN, HEAD_DIM), lambda b, i: (b, layer, 0, 0, 0))


def _cache_heads_on_lanes(ref):
    return jnp.concatenate([ref[0, 0, h] for h in range(HEADS)], axis=1)


def attention(p, row_blk0, cols, n_seq, seq_len, lam, norm_g, *, n_maps, post_scale, cache=None, layer=0):
    nb = seq_len // TM
    kv_len = seq_len + (PAST_LEN if cache is not None else 0)
    kern = functools.partial(_attn_kernel, n_maps=n_maps, post_scale=post_scale, with_cache=cache is not None)
    kv_spec = lambda col: pl.BlockSpec((seq_len, MIX_BLK), lambda b, i: (row_blk0 + b, col))
    cache_specs = [_cache_block_spec(layer)] * 2 if cache is not None else []
    return Part(
        kernel=kern,
        in_specs=[
            pl.BlockSpec(memory_space=pltpu.SMEM),
            pl.BlockSpec((TM, MIX_BLK), lambda b, i: ((row_blk0 + b) * nb + i, cols[0])),
            kv_spec(cols[1]), kv_spec(cols[2]), *cache_specs,
            pl.BlockSpec((1, MIX_BLK), lambda b, i: (0, 0)),
        ],
        out_specs=[pl.BlockSpec((TM, MIX_BLK), lambda b, i: (b * nb + i, 0))],
        out_shapes=[jax.ShapeDtypeStruct((n_seq * seq_len, MIX_BLK), F32)],
        scratch=[pltpu.VMEM((MIX_BLK, kv_len), BF16), pltpu.VMEM((kv_len, MIX_BLK), BF16)],
        args=[lam, p, p, p, *(cache or ()), norm_g])


def _na_slab_start(i):
    return jnp.clip(i - 1, 0, GRID_H // 4 - NA_SLAB_ROWS // 4)


def _na_kernel(q_ref, k_ref, v_ref, kc_ref, vc_ref, bias_ref, o_ref):
    i = pl.program_id(1)
    start = pl.multiple_of(_na_slab_start(i) * TM, TM)
    ks_t = k_ref[pl.ds(start, NA_SLAB), :].astype(F32).T.astype(BF16)
    vs = v_ref[pl.ds(start, NA_SLAB), :]
    kc_t = _cache_heads_on_lanes(kc_ref).T.astype(BF16)
    vc = _cache_heads_on_lanes(vc_ref).astype(BF16)
    q = q_ref[...].astype(F32) * (HEAD_DIM ** -0.5)
    lane = _head_lanes()
    heads = [_lane_range(lane, h * HEAD_DIM, HEAD_DIM) for h in range(HEADS)]
    qm = jnp.concatenate([jnp.where(in_head, q, 0.0) for in_head in heads], axis=0).astype(BF16)
    s_loc = jnp.dot(qm, ks_t, preferred_element_type=F32)
    s_ctx = jnp.dot(qm, kc_t, preferred_element_type=F32)
    e_locs, e_ctxs, dens = [], [], []
    for h in range(HEADS):
        rows = slice(h * TM, (h + 1) * TM)
        sl = s_loc[rows] + bias_ref[0, h]
        sc = s_ctx[rows]
        m = jnp.maximum(jnp.max(sl, axis=-1, keepdims=True), jnp.max(sc, axis=-1, keepdims=True))
        e_loc = jnp.exp(sl - m)
        e_ctx = jnp.exp(sc - m)
        dens.append(jnp.sum(e_loc, axis=-1, keepdims=True) + jnp.sum(e_ctx, axis=-1, keepdims=True))
        e_locs.append(e_loc.astype(BF16))
        e_ctxs.append(e_ctx.astype(BF16))
    oh = (jnp.dot(jnp.concatenate(e_locs, axis=0), vs, preferred_element_type=F32)
          + jnp.dot(jnp.concatenate(e_ctxs, axis=0), vc, preferred_element_type=F32))
    o = jnp.zeros(q.shape, F32)
    for h in range(HEADS):
        o = jnp.where(heads[h], oh[h * TM:(h + 1) * TM] / dens[h], o)
    o_ref[...] = o


def _na_bias_tables(rpb):
    n_dr, n_dc = 2 * NA_WIN_H - 1, 2 * NA_WIN_W - 1
    cq = np.arange(GRID_W)[:, None]
    ck = np.arange(GRID_W)[None, :]
    wc0 = np.clip(cq - NA_WIN_W // 2, 0, GRID_W - NA_WIN_W)
    col_ok = (ck >= wc0) & (ck < wc0 + NA_WIN_W)
    col_pick = np.clip(ck - cq + NA_WIN_W - 1, 0, n_dc - 1)[..., None] == np.arange(n_dc)
    by_col = jnp.einsum("hab,qcb->haqc", rpb.astype(F32), jnp.asarray(col_pick, F32), precision=HIGHEST)
    margin = 4
    by_col = jnp.pad(by_col.transpose(0, 2, 1, 3), ((0, 0), (0, 0), (margin, margin), (0, 0)))
    by_col = by_col.reshape(HEADS, GRID_W, (n_dr + 2 * margin) * GRID_W)
    pieces, row_ok = [], []
    for tile in (0, 1, GRID_H // 4 - 1):
        slab0 = int(np.clip(tile - 1, 0, GRID_H // 4 - NA_SLAB_ROWS // 4)) * 4
        rq = tile * 4 + np.arange(4)
        rk = (slab0 + np.arange(NA_SLAB) // GRID_W)[None, :]
        wr0 = np.clip(rq - NA_WIN_H // 2, 0, GRID_H - NA_WIN_H)[:, None]
        row_ok.append((rk >= wr0) & (rk < wr0 + NA_WIN_H))
        for r in rq:
            first = slab0 - int(r) + NA_WIN_H - 1 + margin
            assert 0 <= first and first + NA_SLAB_ROWS <= n_dr + 2 * margin
            pieces.append(by_col[:, :, first * GRID_W:first * GRID_W + NA_SLAB])
    table = jnp.stack(pieces).reshape(3, 4, HEADS, GRID_W, NA_SLAB).transpose(0, 2, 1, 3, 4)
    valid = np.stack(row_ok)[:, None, :, None, :] & np.tile(col_ok, (1, NA_SLAB_ROWS))[None, None, None]
    table = jnp.where(jnp.asarray(valid), table, NEG_BIG)
    return table.reshape(3, HEADS, TM, NA_SLAB)


def na_latent(p, kc, vc, layer, bias):
    n_t = LAT_TILES_PER_SEQ
    seq_blk0 = TP // DEC_SEQ

    def bias_idx(b, i):
        return (jnp.minimum(i, 1) + i // (n_t - 1), 0, 0, 0)

    return Part(
        kernel=_na_kernel,
        in_specs=[
            pl.BlockSpec((TM, MIX_BLK), lambda b, i: (CTX_TILES + b * n_t + i, M_CQ)),
            pl.BlockSpec((DEC_SEQ, MIX_BLK), lambda b, i: (seq_blk0 + b, M_CK)),
            pl.BlockSpec((DEC_SEQ, MIX_BLK), lambda b, i: (seq_blk0 + b, M_CV)),
            _cache_block_spec(layer), _cache_block_spec(layer),
            pl.BlockSpec((1, HEADS, TM, NA_SLAB), bias_idx),
        ],
        out_specs=[pl.BlockSpec((TM, MIX_BLK), lambda b, i: (b * n_t + i, 0))],
        out_shapes=[jax.ShapeDtypeStruct((TL, MIX_BLK), F32)],
        scratch=[],
        args=[p, p, p, kc, vc, bias])


MAX_EXPONENT = 80.0


def _hgrn_direction(q_ref, f_ref, v_ref, lb, st_ref, o_ref, reverse):
    n_ch = TM // B_CHUNK
    r_idx = lax.broadcasted_iota(jnp.int32, (TM, TM), 0)
    c_idx = lax.broadcasted_iota(jnp.int32, (TM, TM), 1)
    tri = (c_idx >= r_idx) if reverse else (c_idx <= r_idx)
    zq = q_ref[...]
    q = zq * jax.nn.sigmoid(zq)
    z = f_ref[...]
    gate = (1.0 - lb) * jax.nn.sigmoid(z)
    logf = jnp.log(lb + gate)
    kk = (1.0 - lb) - gate
    b = _select_sum_left(jnp.where(tri, 1.0, 0.0).astype(BF16), logf)
    b3 = b.reshape(n_ch, B_CHUNK, MIX_BLK)
    mid = B_CHUNK // 2 if reverse else B_CHUNK // 2 - 1
    q_in = (q.reshape(b3.shape) * jnp.exp(b3 - b3[:, mid:mid + 1, :])).reshape(TM, MIX_BLK)
    q_dec = (q * jnp.exp(b)).astype(BF16)
    b_t = b.T
    kk_t = kk.T
    far = 0 if reverse else TM - 1
    b_far = b_t[:, far:far + 1]
    k_dec_t = (kk_t * jnp.exp(b_far - b_t)).astype(BF16)
    vb = v_ref[...].astype(BF16)
    st = st_ref[...]
    o_state = jnp.dot(q_dec, st.astype(BF16), preferred_element_type=F32)
    kv = jnp.dot(k_dec_t, vb, preferred_element_type=F32)
    st_ref[...] = st * jnp.exp(b_far) + jnp.where(_same_head_matrix(), kv, 0.0)
    lane = _head_lanes()
    token = lax.broadcasted_iota(jnp.int32, (1, TM), 1)
    local = lax.broadcasted_iota(jnp.int32, (HEADS * B_CHUNK, 1), 0) % B_CHUNK
    heads = [_lane_range(lane, h * HEAD_DIM, HEAD_DIM) for h in range(HEADS)]
    weights = []
    half = TM // 2
    for c in range(n_ch):
        near_half = (c >= n_ch // 2) if reverse else (c < n_ch // 2)
        keys = (slice(half, TM) if reverse else slice(0, half)) if near_half else slice(0, TM)
        ref = b_t[:, c * B_CHUNK + mid:c * B_CHUNK + mid + 1]
        k_c_t = (kk_t[:, keys] * jnp.exp(jnp.minimum(ref - b_t[:, keys], MAX_EXPONENT))).astype(BF16)
        q_c = q_in[c * B_CHUNK:(c + 1) * B_CHUNK, :]
        lhs = jnp.concatenate([jnp.where(in_head, q_c, 0.0) for in_head in heads], axis=0)
        a = jnp.dot(lhs.astype(BF16), k_c_t, preferred_element_type=F32)
        t_abs = c * B_CHUNK + local
        tok = token[:, keys]
        a = jnp.where((tok >= t_abs) if reverse else (tok <= t_abs), a, 0.0).astype(BF16)
        if near_half:
            unseen = jnp.zeros((HEADS * B_CHUNK, half), BF16)
            a = jnp.concatenate([unseen, a] if reverse else [a, unseen], axis=1)
        weights.append(a)
    res = jnp.dot(jnp.concatenate(weights, axis=0), vb, preferred_element_type=F32)
    for c in range(n_ch):
        rows = slice(c * B_CHUNK, (c + 1) * B_CHUNK)
        o_c = o_state[rows, :]
        for h, in_head in enumerate(heads):
            lo = (c * HEADS + h) * B_CHUNK
            o_c = o_c + jnp.where(in_head, res[lo:lo + B_CHUNK, :], 0.0)
        o_ref[rows, :] = o_c


def _hgrn_kernel(qf_ref, ff_ref, vf_ref, qb_ref, fb_ref, vb_ref, lb_ref, s0_ref,
                 of_ref, ob_ref, s_ref, stf_ref, stb_ref, *, has_s0):
    j = pl.program_id(1)

    @pl.when(j == 0)
    def _():
        if has_s0:
            stf_ref[...] = s0_ref[0, 0]
            stb_ref[...] = s0_ref[0, 1]
        else:
            stf_ref[...] = jnp.zeros((MIX_BLK, MIX_BLK), F32)
            stb_ref[...] = jnp.zeros((MIX_BLK, MIX_BLK), F32)

    lb = lb_ref[...]
    _hgrn_direction(qf_ref, ff_ref, vf_ref, lb[0:1], stf_ref, of_ref, False)
    _hgrn_direction(qb_ref, fb_ref, vb_ref, lb[1:2], stb_ref, ob_ref, True)

    @pl.when(j == pl.num_programs(1) - 1)
    def _():
        for d, st_ref in enumerate((stf_ref, stb_ref)):
            s = st_ref[...]
            for hd in range(HEADS):
                lo = hd * HEAD_DIM
                s_ref[0, d, hd] = s[lo:lo + HEAD_DIM, lo:lo + HEAD_DIM]


def hgrn(p, row_tile0, n_seq, seq_len, lb, s0):
    nb = seq_len // TM
    has_s0 = s0 is not None
    if s0 is None:
        s0 = jnp.zeros((1, 2, MIX_BLK, MIX_BLK), F32)

    def fwd(col):
        return pl.BlockSpec((TM, MIX_BLK), lambda s, j: (row_tile0 + s * nb + j, col))

    def bwd(col):
        return pl.BlockSpec((TM, MIX_BLK), lambda s, j: (row_tile0 + s * nb + nb - 1 - j, col))

    state_spec = pl.BlockSpec((1, 2, MIX_BLK, MIX_BLK), lambda s, j: (s if has_s0 else 0, 0, 0, 0))
    out_rows = n_seq * seq_len
    return Part(
        kernel=functools.partial(_hgrn_kernel, has_s0=has_s0),
        in_specs=[fwd(G_BQ), fwd(G_BFF), fwd(G_BV), bwd(G_BQ), bwd(G_BFB), bwd(G_BV),
                  pl.BlockSpec((2, MIX_BLK), lambda s, j: (0, 0)), state_spec],
        out_specs=[
            pl.BlockSpec((TM, MIX_BLK), lambda s, j: (s * nb + j, 0)),
            pl.BlockSpec((TM, MIX_BLK), lambda s, j: (s * nb + nb - 1 - j, 0)),
            pl.BlockSpec((1, 2, HEADS, HEAD_DIM, HEAD_DIM), lambda s, j: (s, 0, 0, 0, 0)),
        ],
        out_shapes=[jax.ShapeDtypeStruct((out_rows, MIX_BLK), F32),
                    jax.ShapeDtypeStruct((out_rows, MIX_BLK), F32),
                    jax.ShapeDtypeStruct((n_seq, 2, HEADS, HEAD_DIM, HEAD_DIM), F32)],
        scratch=[pltpu.VMEM((MIX_BLK, MIX_BLK), F32), pltpu.VMEM((MIX_BLK, MIX_BLK), F32)],
        args=[p, p, p, p, p, p, lb, s0])


def _state_to_blockdiag(s):
    eye = jnp.eye(HEADS, dtype=F32)
    full = s.astype(F32)[:, :, :, :, None, :] * eye[None, None, :, None, :, None]
    return full.reshape(s.shape[0], 2, MIX_BLK, MIX_BLK)


def _fft_kernel(u_ref, cs64_ref, csl_ref, o_ref, ab_ref, *, norm):
    seq_len = u_ref.shape[0]

    @pl.when(pl.program_id(1) == 0)
    def _():
        ab = jnp.dot(u_ref[...], cs64_ref[...], preferred_element_type=F32).astype(BF16)
        ab_ref[:seq_len, :] = ab[:, :MIX_BLK]
        ab_ref[seq_len:, :] = ab[:, MIX_BLK:]

    o_ref[...] = jnp.dot(csl_ref[...], ab_ref[...], preferred_element_type=F32) * norm


def _dft_tables(n):
    k = np.arange(n)
    ang = 2.0 * np.pi * ((k[:, None] * k[None, :]) % n) / n
    return np.cos(ang), np.sin(ang)


def _dft_constants(seq_len):
    c64, s64 = _dft_tables(HEAD_DIM)
    eye = np.eye(HEADS)
    cl, sl = _dft_tables(seq_len)
    as_bf16 = lambda a: jnp.asarray(a, F32).astype(BF16)
    return (as_bf16(np.concatenate([np.kron(eye, c64), -np.kron(eye, s64)], axis=1)),
            as_bf16(np.concatenate([cl, sl], axis=1)))


def fourier_mix(p, row_blk0, n_seq, seq_len, consts):
    cs64, csl = consts
    nb = seq_len // TM
    norm = 1.0 / math.sqrt(seq_len * HEAD_DIM)
    return Part(
        kernel=functools.partial(_fft_kernel, norm=norm),
        in_specs=[
            pl.BlockSpec((seq_len, MIX_BLK), lambda s, i: (row_blk0 + s, M_DU)),
            pl.BlockSpec((MIX_BLK, 2 * MIX_BLK), lambda s, i: (0, 0)),
            pl.BlockSpec((TM, 2 * seq_len), lambda s, i: (i, 0)),
        ],
        out_specs=[pl.BlockSpec((TM, MIX_BLK), lambda s, i: (s * nb + i, 0))],
        out_shapes=[jax.ShapeDtypeStruct((n_seq * seq_len, MIX_BLK), F32)],
        scratch=[pltpu.VMEM((2 * seq_len, MIX_BLK), BF16)],
        args=[p, cs64, csl])


def _route(logits_t, rb):
    per = N_EXPERTS // N_GROUPS
    score = [jax.nn.sigmoid(logits_t[e:e + 1, :]) for e in range(N_EXPERTS)]
    sel = [score[e] + rb[e:e + 1, :] for e in range(N_EXPERTS)]
    gscore = []
    for g in range(N_GROUPS):
        vals = sel[g * per:(g + 1) * per]
        best = None
        for a in range(per):
            for b in range(a + 1, per):
                pair = vals[a] + vals[b]
                best = pair if best is None else jnp.maximum(best, pair)
        gscore.append(best)
    chosen = []
    for g in range(N_GROUPS):
        ok = None
        for j in range(N_GROUPS):
            if j == g:
                continue
            cond = gscore[g] > gscore[j] if j < g else gscore[g] >= gscore[j]
            ok = cond if ok is None else ok & cond
        chosen.append(ok)
    picked = []
    for e in range(N_EXPERTS):
        g = e // per
        rank = jnp.zeros_like(sel[e])
        for j in range(g * per, (g + 1) * per):
            if j == e:
                continue
            ahead = sel[j] >= sel[e] if j < e else sel[j] > sel[e]
            rank = rank + jnp.where(ahead, 1.0, 0.0)
        picked.append(chosen[g] & (rank < 2.0))
    wsum = jnp.zeros_like(score[0])
    for e in range(N_EXPERTS):
        wsum = wsum + jnp.where(picked[e], score[e], 0.0)
    bucket = jnp.zeros_like(wsum)
    w_a = jnp.zeros_like(wsum)
    w_b = jnp.zeros_like(wsum)
    for g in range(N_GROUPS):
        for n, (a, b) in enumerate(EXPERT_PAIRS):
            hit = picked[g * per + a] & picked[g * per + b]
            bucket = jnp.where(hit, float(g * len(EXPERT_PAIRS) + n), bucket)
            w_a = jnp.where(hit, score[g * per + a] / wsum, w_a)
            w_b = jnp.where(hit, score[g * per + b] / wsum, w_b)
    return bucket, w_a, w_b


def _out_kernel(*refs):
    streams, rest = refs[:12], refs[12:]
    (bg_ref, mod_ref, hg_ref, w_ref, g2_ref, rw_ref, rb_ref,
     x1_ref, h2_ref, bucket_ref, rank_ref, counts_ref, run_ref) = rest
    is_ctx = pl.program_id(0) < TP // TM_OUT
    x, o_a, o_f, o_b, o_c, o_d = (jnp.where(is_ctx, streams[2 * n][...], streams[2 * n + 1][...])
                                  for n in range(6))

    @pl.when(pl.program_id(0) == 0)
    def _():
        run_ref[...] = jnp.zeros(run_ref.shape, F32)

    mod = mod_ref[0]
    hb = o_f + o_b
    zg = bg_ref[...]
    hb = hb * lax.rsqrt(_head_mean_square(hb) + EPS) * hg_ref[...] * (zg * jax.nn.sigmoid(zg))
    mixers = jnp.concatenate([part.astype(BF16) for part in (o_a, hb, o_c, o_d)], axis=1)
    mixed = jnp.dot(mixers, w_ref[0], preferred_element_type=F32)
    x1 = x + mod[2:3] * mixed
    x1_ref[...] = x1
    ms = jnp.mean(x1 * x1, axis=-1, keepdims=True)
    h2 = x1 * lax.rsqrt(ms + EPS) * g2_ref[...] * (1.0 + mod[4:5]) + mod[3:4]
    rw = rw_ref[...]
    r = jnp.dot(jnp.concatenate(_bf16_pieces(h2, 2), axis=0), rw, preferred_element_type=F32)
    r = r[:TM_OUT] + r[TM_OUT:]
    bucket, w_a, w_b = _route((r[:, :LANES] + r[:, LANES:]).T, rb_ref[...])
    h2_ref[:, :D_MODEL] = h2
    h2_ref[:, D_MODEL:] = jnp.concatenate([w_a, w_b, jnp.zeros((LANES - 2, TM_OUT), F32)], axis=0).T
    onehot = jnp.where(lax.broadcasted_iota(jnp.int32, (BUCKET_ROWS, 1), 0).astype(F32) == bucket, 1.0, 0.0)
    s_idx = lax.broadcasted_iota(jnp.int32, (TM_OUT, TM_OUT), 0)
    t_idx = lax.broadcasted_iota(jnp.int32, (TM_OUT, TM_OUT), 1)
    prefix = jnp.dot(onehot.astype(BF16), jnp.where(s_idx <= t_idx, 1.0, 0.0).astype(BF16),
                     preferred_element_type=F32)
    run = run_ref[...]
    rank = jnp.sum(onehot * (prefix - 1.0 + run[:, 0:1]), axis=0, keepdims=True)
    run = run + jnp.sum(onehot, axis=1, keepdims=True)
    run_ref[...] = run
    bucket_ref[...] = bucket.astype(jnp.int32)
    rank_ref[...] = rank.astype(jnp.int32)
    counts_ref[...] = run


def out_and_route(x_pair, x_is_combined, mixer_pairs, p, mod, layer, hgrn_g, w_out_bf16, norm2_g, router_pieces,
                  router_b):
    n_ctx = TP // TM_OUT
    ctx_tile = lambda i: (jnp.minimum(i, n_ctx - 1), 0)
    lat_tile = lambda i: (jnp.maximum(i - n_ctx, 0), 0)
    mod_row = lambda i: jnp.where(i < n_ctx, 0, 1 + (i - n_ctx) // (DEC_SEQ // TM_OUT))
    tile = lambda w: pl.BlockSpec((TM_OUT, w), lambda i: (i, 0))
    full = lambda r, c: pl.BlockSpec((r, c), lambda i: (0, 0))
    stream_specs = [pl.BlockSpec((TM_OUT, D_MODEL), ctx_tile),
                    pl.BlockSpec((TM_OUT, D_MODEL),
                                 (lambda i: (jnp.maximum(i, n_ctx), 0)) if x_is_combined else lat_tile)]
    stream_args = list(x_pair)
    for o_ctx, o_lat in mixer_pairs:
        stream_specs += [pl.BlockSpec((TM_OUT, MIX_BLK), ctx_tile), pl.BlockSpec((TM_OUT, MIX_BLK), lat_tile)]
        stream_args += [o_ctx, o_lat]
    return pl.pallas_call(
        _out_kernel,
        grid=(T // TM_OUT,),
        in_specs=stream_specs + [
            pl.BlockSpec((TM_OUT, MIX_BLK), lambda i: (i, G_BG)),
            pl.BlockSpec((1, 6, D_MODEL), lambda i: (mod_row(i), 0, 0)),
            full(1, MIX_BLK), pl.BlockSpec((1, D_MODEL, D_MODEL), lambda i: (layer, 0, 0)), full(1, D_MODEL),
            full(D_MODEL, 2 * LANES), full(N_EXPERTS, 1),
        ],
        out_specs=[tile(D_MODEL), tile(ROW_W), pl.BlockSpec((1, TM_OUT), lambda i: (0, i)),
                   pl.BlockSpec((1, TM_OUT), lambda i: (0, i)), full(BUCKET_ROWS, LANES)],
        out_shape=[jax.ShapeDtypeStruct((T, D_MODEL), F32),
                   jax.ShapeDtypeStruct((T, ROW_W), F32),
                   jax.ShapeDtypeStruct((1, T), jnp.int32),
                   jax.ShapeDtypeStruct((1, T), jnp.int32),
                   jax.ShapeDtypeStruct((BUCKET_ROWS, LANES), F32)],
        scratch_shapes=[pltpu.VMEM((BUCKET_ROWS, LANES), F32)],
        compiler_params=_cparams(("arbitrary",)),
        name="out_and_route",
    )(*stream_args, p, mod, jnp.tile(hgrn_g, HEADS)[None, :], w_out_bf16,
      norm2_g[None, :], router_pieces, router_b[:, None])


def _router_pieces(router_w):
    hi, lo = _bf16_pieces(router_w.astype(F32), 2)
    pad = lambda a: jnp.pad(a, ((0, 0), (0, LANES - N_EXPERTS)))
    return jnp.concatenate([pad(hi), pad(lo)], axis=1)


def routing_plan(bucket, rank, counts):
    counts = counts[:N_BUCKETS, 0].astype(jnp.int32)
    n_tiles = (counts + TM_MOE - 1) // TM_MOE
    tile_end = jnp.cumsum(n_tiles)
    tile_start = tile_end - n_tiles
    buckets = jnp.arange(N_BUCKETS, dtype=jnp.int32)
    start_of_token = jnp.sum(jnp.where(bucket[0][:, None] == buckets[None, :], tile_start[None, :], 0), axis=1)
    dest = start_of_token * TM_MOE + rank[0]
    tiles = jnp.arange(MAX_TILES, dtype=jnp.int32)
    valid = tiles < tile_end[-1]
    tile_bucket = jnp.sum((jnp.minimum(tiles, tile_end[-1] - 1)[:, None] >= tile_end[None, :]).astype(jnp.int32), axis=1)
    pair_a = np.array([a for a, _ in EXPERT_PAIRS], np.int32)
    pair_b = np.array([b for _, b in EXPERT_PAIRS], np.int32)
    per = N_EXPERTS // N_GROUPS
    exp_a = jnp.asarray((np.arange(N_BUCKETS) // len(EXPERT_PAIRS)) * per + np.tile(pair_a, N_GROUPS), jnp.int32)
    exp_b = jnp.asarray((np.arange(N_BUCKETS) // len(EXPERT_PAIRS)) * per + np.tile(pair_b, N_GROUPS), jnp.int32)
    pick = tile_bucket[:, None] == buckets[None, :]
    tile_a = jnp.sum(jnp.where(pick, exp_a[None, :], 0), axis=1)
    tile_b = jnp.sum(jnp.where(pick, exp_b[None, :], 0), axis=1)
    plan = [dest.astype(jnp.int32), valid.astype(jnp.int32), (tile_end[-1:] - 1).astype(jnp.int32)]
    for tile_e in (tile_a, tile_b):
        changed = jnp.concatenate([jnp.ones((1,), bool), tile_e[1:] != tile_e[:-1]])
        load = (valid & changed).astype(jnp.int32)
        ring = (jnp.cumsum(load) - 1) % WEIGHT_RING
        plan += [tile_e.astype(jnp.int32), load, ring.astype(jnp.int32)]
    return plan


def _row_copy(src, src_row, dst, dst_row, sem):
    return pltpu.make_async_copy(src.at[pl.ds(src_row, 1), :], dst.at[pl.ds(dst_row, 1), :], sem)


def _scatter_kernel(dest_ref, h_ref, init_ref, o_ref, sem):
    del init_ref
    base = pl.program_id(0) * TM_SCATTER

    for r in range(TM_SCATTER):
        _row_copy(h_ref, r, o_ref, dest_ref[base + r], sem).start(priority=r % N_DMA_PRIORITIES)
    pltpu.make_async_copy(h_ref, o_ref.at[pl.ds(0, TM_SCATTER), :], sem).wait()


def scatter_to_slots(h2, dest, slots):
    return pl.pallas_call(
        _scatter_kernel,
        grid_spec=pltpu.PrefetchScalarGridSpec(
            num_scalar_prefetch=1,
            grid=(T // TM_SCATTER,),
            in_specs=[pl.BlockSpec((TM_SCATTER, ROW_W), lambda i, d: (i, 0)),
                      pl.BlockSpec(memory_space=pl.ANY)],
            out_specs=pl.BlockSpec(memory_space=pl.ANY),
            scratch_shapes=[pltpu.SemaphoreType.DMA(())],
        ),
        out_shape=jax.ShapeDtypeStruct((N_SLOTS, ROW_W), F32),
        input_output_aliases={2: 0},
        compiler_params=_cparams(("arbitrary",)),
        name="scatter_to_slots",
    )(dest, h2, slots)


def _moe_kernel(valid_ref, last_ref, ta_ref, la_ref, ra_ref, tb_ref, lb_ref, rb_ref, h_ref, wg_hbm, wu_hbm, wd_hbm,
                o_ref, wg_buf, wu_buf, wd_buf, sem, *, layer):
    del last_ref
    i = pl.program_id(0)
    plans = ((ta_ref, la_ref, ra_ref), (tb_ref, lb_ref, rb_ref))

    def copies(tile, position):
        expert_ref, _, ring_ref = plans[position]
        expert, entry = expert_ref[tile], ring_ref[tile]
        return [pltpu.make_async_copy(hbm.at[layer, expert], buf.at[position, entry], sem.at[position, entry])
                for hbm, buf in ((wg_hbm, wg_buf), (wu_hbm, wu_buf), (wd_hbm, wd_buf))]

    def start_loads(tile):
        for position in range(2):
            @pl.when(plans[position][1][tile] == 1)
            def _(position=position):
                for copy in copies(tile, position):
                    copy.start(priority=1)

    @pl.when(i == 0)
    def _():
        for tile in range(WEIGHT_LOOKAHEAD):
            start_loads(tile)

    @pl.when(i + WEIGHT_LOOKAHEAD < MAX_TILES)
    def _():
        start_loads(i + WEIGHT_LOOKAHEAD)

    for position in range(2):
        @pl.when(plans[position][1][i] == 1)
        def _(position=position):
            for copy in copies(i, position):
                copy.wait()

    @pl.when(valid_ref[i] == 1)
    def _():
        x = h_ref[:, :D_MODEL].astype(BF16)
        gates = h_ref[:, D_MODEL:]
        y = jnp.zeros((TM_MOE, D_MODEL), F32)
        for n, (_, _, ring_ref) in enumerate(plans):
            entry = ring_ref[i]
            a = jnp.dot(x, wg_buf[n, entry].astype(BF16), preferred_element_type=F32)
            u = jnp.dot(x, wu_buf[n, entry].astype(BF16), preferred_element_type=F32)
            z = a * jax.nn.sigmoid(a) * u * gates[:, n:n + 1]
            y = y + jnp.dot(z.astype(BF16), wd_buf[n, entry].astype(BF16), preferred_element_type=F32)
        o_ref[:, 0, :] = y

    @pl.when(valid_ref[i] == 0)
    def _():
        o_ref[...] = jnp.zeros((TM_MOE, 1, D_MODEL), F32)


def moe(h_slots, valid, last, tile_a, load_a, ring_a, tile_b, load_b, ring_b, layer, wg, wu, wd):
    assert WEIGHT_RING > WEIGHT_LOOKAHEAD and MAX_TILES >= WEIGHT_LOOKAHEAD
    hbm = pl.BlockSpec(memory_space=pl.ANY)
    return pl.pallas_call(
        functools.partial(_moe_kernel, layer=layer),
        grid_spec=pltpu.PrefetchScalarGridSpec(
            num_scalar_prefetch=8,
            grid=(MAX_TILES,),
            in_specs=[pl.BlockSpec((TM_MOE, ROW_W), lambda i, valid, last, *_: (jnp.minimum(i, last[0]), 0)),
                      hbm, hbm, hbm],
            out_specs=pl.BlockSpec((TM_MOE, 1, D_MODEL), lambda i, *_: (i, 0, 0)),
            scratch_shapes=[pltpu.VMEM((2, WEIGHT_RING, D_MODEL, D_EXPERT), F32),
                            pltpu.VMEM((2, WEIGHT_RING, D_MODEL, D_EXPERT), F32),
                            pltpu.VMEM((2, WEIGHT_RING, D_EXPERT, D_MODEL), F32),
                            pltpu.SemaphoreType.DMA((2, WEIGHT_RING))],
        ),
        out_shape=jax.ShapeDtypeStruct((N_SLOTS, 1, D_MODEL), F32),
        compiler_params=_cparams(("arbitrary",)),
        name="moe",
    )(valid, last, tile_a, load_a, ring_a, tile_b, load_b, ring_b, h_slots, wg, wu, wd)


def _gather_tile(dest_ref, y_ref, buf_ref, sem, tile, slot):
    for r in range(TM):
        pltpu.make_async_copy(y_ref.at[pl.ds(dest_ref[tile * TM + r], 1)],
                              buf_ref.at[slot, pl.ds(r, 1)], sem.at[slot]).start(priority=r % N_DMA_PRIORITIES)


def _moe_residual(dest_ref, y_ref, buf_ref, sem, x1_ref, mod_ref):
    i = pl.program_id(0)
    slot = i % 2

    @pl.when(i == 0)
    def _():
        _gather_tile(dest_ref, y_ref, buf_ref, sem, 0, 0)

    @pl.when(i + 1 < pl.num_programs(0))
    def _():
        _gather_tile(dest_ref, y_ref, buf_ref, sem, i + 1, 1 - slot)

    pltpu.make_async_copy(y_ref.at[pl.ds(0, TM)], buf_ref.at[slot], sem.at[slot]).wait()
    return x1_ref[...] + mod_ref[0][5:6] * buf_ref[slot, :, 0, :]


def _final_kernel(dest_ref, y_ref, x1_ref, mod_ref, g_ref, oc_ref, ol_ref, buf_ref, sem):
    x2 = _moe_residual(dest_ref, y_ref, buf_ref, sem, x1_ref, mod_ref)
    ms = jnp.mean(x2 * x2, axis=-1, keepdims=True)
    y = x2 * lax.rsqrt(ms + EPS) * g_ref[...]

    @pl.when(_is_ctx_tile())
    def _():
        oc_ref[...] = y

    @pl.when(jnp.logical_not(_is_ctx_tile()))
    def _():
        ol_ref[...] = y


_GATHER_SCRATCH = [pltpu.VMEM((2, TM, 1, D_MODEL), F32), pltpu.SemaphoreType.DMA((2,))]


def final_norm(dest, y_slots, x1, mod, final_g):
    return pl.pallas_call(
        _final_kernel,
        grid_spec=pltpu.PrefetchScalarGridSpec(
            num_scalar_prefetch=1,
            grid=(N_TILES,),
            in_specs=[pl.BlockSpec(memory_space=pl.ANY),
                      pl.BlockSpec((TM, D_MODEL), lambda i, d: (i, 0)),
                      pl.BlockSpec((1, 6, D_MODEL), lambda i, d: (_mod_row(i), 0, 0)),
                      pl.BlockSpec((1, D_MODEL), lambda i, d: (0, 0))],
            out_specs=[pl.BlockSpec((TM, D_MODEL), _ctx_tile), pl.BlockSpec((TM, D_MODEL), _lat_tile)],
            scratch_shapes=_GATHER_SCRATCH,
        ),
        out_shape=[jax.ShapeDtypeStruct((TP, D_MODEL), F32), jax.ShapeDtypeStruct((TL, D_MODEL), F32)],
        compiler_params=_cparams(("arbitrary",)),
        name="final_norm",
    )(dest, y_slots, x1, mod, final_g[None, :])


def kernel(x_prompt, x_sample, cache_diff_k, cache_diff_v, cache_na_k, cache_na_v, state_hgrn, c, c_ctx,
           norm1_g, norm2_g, ada_w, ada_b, w_in, w_out, diff_lambda, diff_subln_g, hgrn_lb_logits,
           hgrn_norm_g, na_rpb, router_w, router_b, moe_w_gate, moe_w_up, moe_w_down, final_norm_g):
    assert SEQ == TM and PAST_LEN == TM and DEC_SEQ % TM_OUT == 0 and TP % DEC_SEQ == 0
    x_pair = (x_prompt.reshape(TP, D_MODEL), x_sample.reshape(TL, D_MODEL))
    w_in_bf16 = jnp.concatenate([w_in[:, :, c * MIX_BLK:(c + 1) * MIX_BLK] for c in PM_BLOCKS + PG_BLOCKS],
                                axis=-1).astype(BF16)
    w_out_bf16 = w_out.astype(BF16)
    router_pieces = _router_pieces(router_w)
    mods = modulation(jnp.concatenate([c_ctx[None, :], c], axis=0), ada_w, ada_b)
    mods = mods.reshape(DEPTH, 3, 6, D_MODEL)
    lb_sm = jax.nn.softmax(hgrn_lb_logits.astype(F32), axis=0)
    lb_all = jnp.cumsum(lb_sm, axis=0) - lb_sm[0:1]
    rope = _rope_tables()
    dft_ctx = _dft_constants(SEQ)
    dft_lat = _dft_constants(DEC_SEQ)
    lat_blk0 = TP // DEC_SEQ
    states = []
    moe_state = None
    for l in range(DEPTH):
        if moe_state is None:
            pm, pg, *new_kv = projection(*x_pair, mods[l], norm1_g[l], w_in_bf16, rope)
        else:
            x, pm, pg, *new_kv = projection_after_moe(*moe_state, mods[l - 1], mods[l], l, norm1_g[l], w_in_bf16, rope,
                                                 new_kv)
            x_pair = (x, x)

        lq = diff_lambda[l].astype(F32)
        lam_init = 0.8 - 0.6 * math.exp(-0.3 * l)
        lam = (jnp.exp(jnp.sum(lq[0] * lq[1])) - jnp.exp(jnp.sum(lq[2] * lq[3])) + lam_init).reshape(1)
        subln = jnp.tile(diff_subln_g[l], HEADS)[None, :]
        diff = functools.partial(attention, pm, cols=(M_AQ, M_AK, M_AV), lam=lam, norm_g=subln, n_maps=2,
                                 post_scale=1.0 - lam_init)
        (oa_ctx,), (of_ctx, ob_ctx, st_ctx), (oc_ctx,), (od_ctx,) = run_parts(
            [diff(row_blk0=0, n_seq=BATCH, seq_len=SEQ),
             hgrn(pg, 0, BATCH, SEQ, lb_all[l], None),
             attention(pm, 0, (M_CQ, M_CK, M_CV), BATCH, SEQ, lam, subln, n_maps=1, post_scale=1.0),
             fourier_mix(pm, 0, BATCH, SEQ, dft_ctx)],
            (BATCH, SEQ // TM), "context_mixers")
        (oa_lat,), (of_lat, ob_lat, _), (oc_lat,), (od_lat,) = run_parts(
            [diff(row_blk0=lat_blk0, n_seq=DEC_BATCH, seq_len=DEC_SEQ, cache=(cache_diff_k, cache_diff_v), layer=l),
             hgrn(pg, CTX_TILES, DEC_BATCH, DEC_SEQ, lb_all[l], _state_to_blockdiag(state_hgrn[:, l])),
             na_latent(pm, cache_na_k, cache_na_v, l, _na_bias_tables(na_rpb[l])),
             fourier_mix(pm, lat_blk0, DEC_BATCH, DEC_SEQ, dft_lat)],
            (DEC_BATCH, DEC_SEQ // TM), "latent_mixers")

        x1, h2, bucket, rank, counts = out_and_route(
            x_pair, l > 0, ((oa_ctx, oa_lat), (of_ctx, of_lat), (ob_ctx, ob_lat), (oc_ctx, oc_lat),
                            (od_ctx, od_lat)),
            pg, mods[l], l, hgrn_norm_g[l], w_out_bf16, norm2_g[l], router_pieces, router_b)
        dest, *tile_plan = routing_plan(bucket, rank, counts)
        h_slots = scatter_to_slots(h2, dest, jnp.zeros((N_SLOTS, ROW_W), F32) if l == 0 else h_slots)
        y_slots = moe(h_slots, *tile_plan, l, moe_w_gate, moe_w_up, moe_w_down)
        moe_state = (dest, y_slots, x1)

        states.append(st_ctx)
    y_prompt, y_sample = final_norm(*moe_state, mods[DEPTH - 1], final_norm_g)
    return (y_prompt.reshape(BATCH, SEQ, D_MODEL), y_sample.reshape(DEC_BATCH, DEC_SEQ, D_MODEL),
            *new_kv, jnp.stack(states, axis=1))
```
